```python
import math
import jax, jax.numpy as jnp
from jax import lax
import numpy as np

D_MODEL = 1024
BATCH = 8
SEQ = 8192
DEPTH = 2

D_MIX = D_MODEL
HEAD_DIM = 64
N_HEADS = 8
N_KV_HEADS = 2
Q_PER_KV = N_HEADS // N_KV_HEADS
D_ATTN = N_HEADS * HEAD_DIM
D_KV = N_KV_HEADS * HEAD_DIM
WINDOW = 128
BLOCK = 128
N_GM_GROUPS = 8
GM_GROUP_DIM = 64
D_GM = N_GM_GROUPS * GM_GROUP_DIM
CHUNK = 128
D_IN = D_ATTN + 2 * D_KV + D_ATTN + 3 * D_GM
EPS = 1e-6
NEG_INF = -1e30

kernel_name = "hymba_style_bidir_attn_gmlp_hybrid"


def _rms(x, eps=EPS):
    xf = x.astype(jnp.float32)
    return (xf * lax.rsqrt(jnp.mean(xf * xf, axis=-1, keepdims=True) + eps)).astype(x.dtype)


def _alibi_slopes(n_heads):
    return 2.0 ** (-8.0 * jnp.arange(1, n_heads + 1, dtype=jnp.float32) / n_heads)


def _windowed_gqa(q, k, v, q_gain, k_gain, sink):
    b, s = q.shape[0], q.shape[1]
    nb = s // BLOCK
    q = _rms(q) * q_gain
    k = _rms(k) * k_gain
    qb = q.reshape(b, nb, BLOCK, N_KV_HEADS, Q_PER_KV, HEAD_DIM)
    pad = ((0, 0), (BLOCK, BLOCK), (0, 0), (0, 0))
    kp = jnp.pad(k, pad).reshape(b, nb + 2, BLOCK, N_KV_HEADS, HEAD_DIM)
    vp = jnp.pad(v, pad).reshape(b, nb + 2, BLOCK, N_KV_HEADS, HEAD_DIM)
    kb = jnp.concatenate([kp[:, :-2], kp[:, 1:-1], kp[:, 2:]], axis=2)
    vb = jnp.concatenate([vp[:, :-2], vp[:, 1:-1], vp[:, 2:]], axis=2)
    scores = jnp.einsum('bnqkgd,bnskd->bnkgqs', qb, kb).astype(jnp.float32) / math.sqrt(HEAD_DIM)
    blk = jnp.arange(nb)[:, None, None]
    qpos = blk * BLOCK + jnp.arange(BLOCK)[None, :, None]
    kpos = blk * BLOCK - BLOCK + jnp.arange(3 * BLOCK)[None, None, :]
    dist = jnp.abs(kpos - qpos).astype(jnp.float32)
    valid = (dist <= WINDOW) & (kpos >= 0) & (kpos < s)
    slopes = _alibi_slopes(N_HEADS).reshape(N_KV_HEADS, Q_PER_KV)
    scores = scores - slopes[None, None, :, :, None, None] * dist[None, :, None, None]
    scores = jnp.where(valid[None, :, None, None], scores, NEG_INF)
    sink_col = jnp.broadcast_to(
        sink.astype(jnp.float32).reshape(N_KV_HEADS, Q_PER_KV)[None, None, :, :, None, None],
        scores.shape[:-1] + (1,))
    probs = jax.nn.softmax(jnp.concatenate([scores, sink_col], axis=-1), axis=-1)[..., :-1]
    out = jnp.einsum('bnkgqs,bnskd->bnqkgd', probs.astype(v.dtype), vb)
    return out.reshape(b, s, D_ATTN)


def _chunked_gmlp(u, vg, w_s, b_s):
    b, s = u.shape[0], u.shape[1]
    nc = s // CHUNK
    vn = _rms(vg.reshape(b, nc, CHUNK, N_GM_GROUPS, GM_GROUP_DIM))
    sv = jnp.einsum('gts,bcsge->bctge', w_s, vn) + b_s.T[None, None, :, :, None]
    return u * sv.reshape(b, s, D_GM)


def _fwd_setup_inputs(seed: int = 0) -> dict:
    key = jax.random.key(seed)
    ks = jax.random.split(key, 12)
    f32 = jnp.float32
    x = jax.random.normal(ks[0], (BATCH, SEQ, D_MODEL), f32)
    c = jax.random.normal(ks[1], (BATCH, D_MODEL), f32)
    w_ada = jax.random.normal(ks[2], (DEPTH, D_MODEL, 3 * D_MODEL), f32) * D_MODEL ** -0.5
    b_ada = jax.random.normal(ks[3], (DEPTH, 3 * D_MODEL), f32) * 0.02
    norm_gain = 1.0 + 0.01 * jax.random.normal(ks[4], (DEPTH, D_MODEL), f32)
    w_in = jax.random.normal(ks[5], (DEPTH, D_MODEL, D_IN), f32) * D_MODEL ** -0.5
    q_gain = 1.0 + 0.01 * jax.random.normal(ks[6], (DEPTH, HEAD_DIM), f32)
    k_gain = 1.0 + 0.01 * jax.random.normal(ks[7], (DEPTH, HEAD_DIM), f32)
    sink = jax.random.normal(ks[8], (DEPTH, N_HEADS), f32) * 0.5
    w_s = jax.random.normal(ks[9], (DEPTH, N_GM_GROUPS, CHUNK, CHUNK), f32) * (0.5 * CHUNK ** -0.5)
    b_s = 1.0 + 0.01 * jax.random.normal(ks[10], (DEPTH, N_GM_GROUPS, CHUNK), f32)
    w_out = jax.random.normal(ks[11], (DEPTH, D_MIX, D_MODEL), f32) * D_MIX ** -0.5
    return {"x": x, "c": c, "w_ada": w_ada, "b_ada": b_ada, "norm_gain": norm_gain,
            "w_in": w_in, "q_gain": q_gain, "k_gain": k_gain, "sink": sink,
            "w_s": w_s, "b_s": b_s, "w_out": w_out}


def _fwd_reference(x, c, w_ada, b_ada, norm_gain, w_in, q_gain, k_gain, sink, w_s, b_s, w_out):
    b, s, _ = x.shape
    cond = jax.nn.silu(c)
    splits = np.cumsum([D_ATTN, D_KV, D_KV, D_ATTN, D_GM, D_GM])
    for l in range(DEPTH):
        ada = cond @ w_ada[l] + b_ada[l]
        shift, scale, gate = jnp.split(ada, 3, axis=-1)
        h = _rms(x) * norm_gain[l]
        h = h * (1.0 + scale[:, None, :]) + shift[:, None, :]
        proj = h @ w_in[l]
        q, k, v, g_attn, u, v_gm, g_gm = jnp.split(proj, splits, axis=-1)
        attn = _windowed_gqa(q.reshape(b, s, N_HEADS, HEAD_DIM),
                             k.reshape(b, s, N_KV_HEADS, HEAD_DIM),
                             v.reshape(b, s, N_KV_HEADS, HEAD_DIM),
                             q_gain[l], k_gain[l], sink[l])
        gm = _chunked_gmlp(u, v_gm, w_s[l], b_s[l])
        y = jnp.concatenate([attn * jax.nn.silu(g_attn), gm * jax.nn.silu(g_gm)], axis=-1)
        x = x + gate[:, None, :] * (y @ w_out[l])
    return x


import jax as _jax
import jax.numpy as _jnp

TWIN_FORMAT = 'train_step'
FWD_PARAMS = ['x', 'c', 'w_ada', 'b_ada', 'norm_gain', 'w_in', 'q_gain', 'k_gain', 'sink', 'w_s', 'b_s', 'w_out']
TWIN_WEIGHTS = ['w_ada', 'b_ada', 'norm_gain', 'w_in', 'q_gain', 'k_gain', 'sink', 'w_s', 'b_s', 'w_out']
TWIN_DIFF_INPUT = 'x'
TWIN_INPUTS = ['x', 'c', 'w_ada', 'b_ada', 'norm_gain', 'w_in', 'q_gain', 'k_gain', 'sink', 'w_s', 'b_s', 'w_out', 'loss_target', 'm_w_ada', 'm_b_ada', 'm_norm_gain', 'm_w_in', 'm_q_gain', 'm_k_gain', 'm_sink', 'm_w_s', 'm_b_s', 'm_w_out', 'v_w_ada', 'v_b_ada', 'v_norm_gain', 'v_w_in', 'v_q_gain', 'v_k_gain', 'v_sink', 'v_w_s', 'v_b_s', 'v_w_out']
TWIN_OUTPUTS = ['loss', 'grad_x', 'grad_w_ada', 'grad_b_ada', 'grad_norm_gain', 'grad_w_in', 'grad_q_gain', 'grad_k_gain', 'grad_sink', 'grad_w_s', 'grad_b_s', 'grad_w_out', 'delta_w_ada', 'delta_b_ada', 'delta_norm_gain', 'delta_w_in', 'delta_q_gain', 'delta_k_gain', 'delta_sink', 'delta_w_s', 'delta_b_s', 'delta_w_out', 'new_m_w_ada', 'new_m_b_ada', 'new_m_norm_gain', 'new_m_w_in', 'new_m_q_gain', 'new_m_k_gain', 'new_m_sink', 'new_m_w_s', 'new_m_b_s', 'new_m_w_out', 'new_v_w_ada', 'new_v_b_ada', 'new_v_norm_gain', 'new_v_w_in', 'new_v_q_gain', 'new_v_k_gain', 'new_v_sink', 'new_v_w_s', 'new_v_b_s', 'new_v_w_out']
TWIN_LEAF_KINDS = {'loss': 'loss', 'grad_x': 'grad_x', 'grad_w_ada': 'grad_w', 'grad_b_ada': 'grad_w', 'grad_norm_gain': 'grad_w', 'grad_w_in': 'grad_w', 'grad_q_gain': 'grad_w', 'grad_k_gain': 'grad_w', 'grad_sink': 'grad_w', 'grad_w_s': 'grad_w', 'grad_b_s': 'grad_w', 'grad_w_out': 'grad_w', 'delta_w_ada': 'delta_w', 'delta_b_ada': 'delta_w', 'delta_norm_gain': 'delta_w', 'delta_w_in': 'delta_w', 'delta_q_gain': 'delta_w', 'delta_k_gain': 'delta_w', 'delta_sink': 'delta_w', 'delta_w_s': 'delta_w', 'delta_b_s': 'delta_w', 'delta_w_out': 'delta_w', 'new_m_w_ada': 'new_m', 'new_m_b_ada': 'new_m', 'new_m_norm_gain': 'new_m', 'new_m_w_in': 'new_m', 'new_m_q_gain': 'new_m', 'new_m_k_gain': 'new_m', 'new_m_sink': 'new_m', 'new_m_w_s': 'new_m', 'new_m_b_s': 'new_m', 'new_m_w_out': 'new_m', 'new_v_w_ada': 'new_v', 'new_v_b_ada': 'new_v', 'new_v_norm_gain': 'new_v', 'new_v_w_in': 'new_v', 'new_v_q_gain': 'new_v', 'new_v_k_gain': 'new_v', 'new_v_sink': 'new_v', 'new_v_w_s': 'new_v', 'new_v_b_s': 'new_v', 'new_v_w_out': 'new_v'}


def _forward(args):
    return _fwd_reference(*[args[k] for k in FWD_PARAMS])


def _output_shape():
    def fwd():
        inp = _fwd_setup_inputs(0)
        return _fwd_reference(*[inp[k] for k in FWD_PARAMS])
    out = _jax.eval_shape(fwd)
    return out.shape, out.dtype

N_MICROBATCH = 1
ADAM_LR = 0.001
ADAM_B1 = 0.9
ADAM_B2 = 0.999
ADAM_EPS = 1e-08
ADAM_WD = 0.01
ADAM_STEP = 10
PER_EXAMPLE_BATCH_AXIS = {'x': 0, 'c': 0, 'loss_target': 0}
SHARED_INPUTS = []
_WEIGHT_DTYPES = {'w_ada': _jnp.float32, 'b_ada': _jnp.float32, 'norm_gain': _jnp.float32, 'w_in': _jnp.float32, 'q_gain': _jnp.float32, 'k_gain': _jnp.float32, 'sink': _jnp.float32, 'w_s': _jnp.float32, 'b_s': _jnp.float32, 'w_out': _jnp.float32}
MOMENT_SCALE = {'w_ada': 7.978792e+00, 'b_ada': 2.092370e+01, 'norm_gain': 3.594035e+01, 'w_in': 4.054891e+00, 'q_gain': 4.159609e+00, 'k_gain': 4.177346e+00, 'sink': 2.470576e+01, 'w_s': 1.161103e+00, 'b_s': 1.362137e+01, 'w_out': 1.879089e+00}


def _to_microbatches(a, axis):
    t = _jnp.moveaxis(a, axis, 0)
    t = t.reshape((N_MICROBATCH, t.shape[0] // N_MICROBATCH) + t.shape[1:])
    return _jnp.moveaxis(t, 1, axis + 1)


def setup_inputs(seed: int = 0) -> dict:
    inp = _fwd_setup_inputs(seed)
    key = _jax.random.fold_in(_jax.random.key(seed), 7919)
    shape, _ = _output_shape()
    out = dict(inp)
    out["loss_target"] = _jax.random.normal(_jax.random.fold_in(key, 0), shape, _jnp.float32)
    for i, name in enumerate(TWIN_WEIGHTS):
        w = inp[name].astype(_jnp.float32)
        if MOMENT_SCALE is None:
            s = _jnp.sqrt(_jnp.mean(_jnp.square(w)) + 1e-30)
        else:
            s = MOMENT_SCALE[name]
        km, kv = _jax.random.split(_jax.random.fold_in(key, i + 1))
        out[name] = w
        out["m_" + name] = s * _jax.random.normal(km, w.shape, _jnp.float32)
        out["v_" + name] = (s * s) * _jax.random.uniform(kv, w.shape, _jnp.float32, 0.5, 1.5)
    if N_MICROBATCH > 1:
        for name, axis in PER_EXAMPLE_BATCH_AXIS.items():
            out[name] = _to_microbatches(out[name], axis)
    return {'x': out['x'], 'c': out['c'], 'w_ada': out['w_ada'], 'b_ada': out['b_ada'], 'norm_gain': out['norm_gain'], 'w_in': out['w_in'], 'q_gain': out['q_gain'], 'k_gain': out['k_gain'], 'sink': out['sink'], 'w_s': out['w_s'], 'b_s': out['b_s'], 'w_out': out['w_out'], 'loss_target': out['loss_target'], 'm_w_ada': out['m_w_ada'], 'm_b_ada': out['m_b_ada'], 'm_norm_gain': out['m_norm_gain'], 'm_w_in': out['m_w_in'], 'm_q_gain': out['m_q_gain'], 'm_k_gain': out['m_k_gain'], 'm_sink': out['m_sink'], 'm_w_s': out['m_w_s'], 'm_b_s': out['m_b_s'], 'm_w_out': out['m_w_out'], 'v_w_ada': out['v_w_ada'], 'v_b_ada': out['v_b_ada'], 'v_norm_gain': out['v_norm_gain'], 'v_w_in': out['v_w_in'], 'v_q_gain': out['v_q_gain'], 'v_k_gain': out['v_k_gain'], 'v_sink': out['v_sink'], 'v_w_s': out['v_w_s'], 'v_b_s': out['v_b_s'], 'v_w_out': out['v_w_out']}


def _loss(weights, diff, rest, loss_target):
    with _jax.named_scope("forward"):
        args = {**rest, TWIN_DIFF_INPUT: diff, **{k: w.astype(_WEIGHT_DTYPES[k]) for k, w in weights.items()}}
        y = _forward(args)
    with _jax.named_scope("loss_head"):
        err = _jnp.square(y.astype(_jnp.float32) - loss_target)
        return 0.5 * _jnp.sum(_jnp.mean(err, axis=-1)) if err.ndim else 0.5 * err


def _adamw(w, g, m, v):
    m = ADAM_B1 * m + (1.0 - ADAM_B1) * g
    v = ADAM_B2 * v + (1.0 - ADAM_B2) * _jnp.square(g)
    m_hat = m / (1.0 - ADAM_B1 ** ADAM_STEP)
    v_hat = v / (1.0 - ADAM_B2 ** ADAM_STEP)
    delta = -ADAM_LR * (m_hat / (_jnp.sqrt(v_hat) + ADAM_EPS) + ADAM_WD * w)
    return delta, m, v


def reference(x, c, w_ada, b_ada, norm_gain, w_in, q_gain, k_gain, sink, w_s, b_s, w_out, loss_target, m_w_ada, m_b_ada, m_norm_gain, m_w_in, m_q_gain, m_k_gain, m_sink, m_w_s, m_b_s, m_w_out, v_w_ada, v_b_ada, v_norm_gain, v_w_in, v_q_gain, v_k_gain, v_sink, v_w_s, v_b_s, v_w_out):
    given = dict(x=x, c=c, w_ada=w_ada, b_ada=b_ada, norm_gain=norm_gain, w_in=w_in, q_gain=q_gain, k_gain=k_gain, sink=sink, w_s=w_s, b_s=b_s, w_out=w_out, loss_target=loss_target, m_w_ada=m_w_ada, m_b_ada=m_b_ada, m_norm_gain=m_norm_gain, m_w_in=m_w_in, m_q_gain=m_q_gain, m_k_gain=m_k_gain, m_sink=m_sink, m_w_s=m_w_s, m_b_s=m_b_s, m_w_out=m_w_out, v_w_ada=v_w_ada, v_b_ada=v_b_ada, v_norm_gain=v_norm_gain, v_w_in=v_w_in, v_q_gain=v_q_gain, v_k_gain=v_k_gain, v_sink=v_sink, v_w_s=v_w_s, v_b_s=v_b_s, v_w_out=v_w_out)
    weights = {n: given[n] for n in TWIN_WEIGHTS}
    shared = {n: given[n] for n in SHARED_INPUTS}
    per_example = {n: given[n] for n in ['x', 'c']}
    grad_fn = _jax.value_and_grad(_loss, argnums=(0, 1))

    def one_microbatch(ex, loss_target):
        ex = dict(ex)
        diff = ex.pop(TWIN_DIFF_INPUT)
        return grad_fn(weights, diff, {**shared, **ex}, loss_target)

    if N_MICROBATCH == 1:
        loss, (grad_w, grad_x) = one_microbatch(per_example, given["loss_target"])
    else:
        def body(carry, xs):
            loss_sum, grad_sum = carry
            l_k, (gw_k, gx_k) = one_microbatch(xs[0], xs[1])
            with _jax.named_scope("update"):
                return (loss_sum + l_k, _jax.tree.map(_jnp.add, grad_sum, gw_k)), gx_k

        init = (_jnp.zeros((), _jnp.float32), _jax.tree.map(_jnp.zeros_like, weights))
        (loss, grad_w), grad_x = _jax.lax.scan(body, init, (per_example, given["loss_target"]))
    with _jax.named_scope("update"):
        delta_w, new_m, new_v = {}, {}, {}
        for n in TWIN_WEIGHTS:
            delta_w[n], new_m[n], new_v[n] = _adamw(weights[n], grad_w[n], given["m_" + n], given["v_" + n])
    return (loss, grad_x, *[grad_w[n] for n in TWIN_WEIGHTS], *[delta_w[n] for n in TWIN_WEIGHTS],
            *[new_m[n] for n in TWIN_WEIGHTS], *[new_v[n] for n in TWIN_WEIGHTS])
```

```python
import functools

import jax
import jax.numpy as jnp
from jax import lax
from jax.experimental import pallas as pl
from jax.experimental.pallas import tpu as pltpu

F32 = jnp.float32
MXU_DTYPE = jnp.bfloat16
MESH_ID = pl.DeviceIdType.MESH

N_DEV = 8
HEAD_DIM = 64
N_HEADS = 8
Q_PER_KV = 4
D_ATTN = 512
D_KV = 128
D_GM = 512
N_GROUPS = 8
D_MIX = D_ATTN + D_GM
BLK = 128
LANES = 128
SUBLANES = 8
N_PAIRS = D_ATTN // LANES
D_QKV = D_ATTN + 2 * D_KV
D_REST = D_ATTN + 3 * D_GM
D_IN = D_QKV + D_REST
EPS = 1e-6
NEG_INF = -1e30
ALIBI_SLOPES = tuple(2.0 ** (-8.0 * (h + 1) / N_HEADS) for h in range(N_HEADS))
Q_SCALE = 1.0 / 8.0

ADAM_LR = 0.001
ADAM_B1 = 0.9
ADAM_B2 = 0.999
ADAM_EPS = 1e-08
ADAM_WD = 0.01
ADAM_STEP = 10

TOKEN_TILE = 512
VMEM_LIMIT_BYTES = 56 * 1024 * 1024


def _params(semantics=None):
    return pltpu.CompilerParams(dimension_semantics=semantics, vmem_limit_bytes=VMEM_LIMIT_BYTES)


def _dot(a, b):
    return jnp.dot(a, b, preferred_element_type=F32)


def _dot_nt(a, b):
    return lax.dot_general(a, b, (((1,), (1,)), ((), ())), preferred_element_type=F32)


def _dot_tn(a, b):
    return lax.dot_general(a, b, (((0,), (0,)), ((), ())), preferred_element_type=F32)


def _mx(v):
    return v.astype(MXU_DTYPE)


def _lane_lo(rows):
    return lax.broadcasted_iota(jnp.int32, (rows, LANES), 1) < HEAD_DIM


def _half_ones():
    r = lax.broadcasted_iota(jnp.int32, (LANES, LANES), 0) < HEAD_DIM
    c = lax.broadcasted_iota(jnp.int32, (LANES, LANES), 1) < HEAD_DIM
    return jnp.where(r == c, 1.0, 0.0).astype(jnp.bfloat16)


def _half_sum(v, ones):
    p1 = v.astype(jnp.bfloat16)
    r1 = v - p1.astype(F32)
    p2 = r1.astype(jnp.bfloat16)
    p3 = (r1 - p2.astype(F32)).astype(jnp.bfloat16)
    return _dot(p1, ones) + _dot(p2, ones) + _dot(p3, ones)


def _half_rms(v, ones):
    r = lax.rsqrt(_half_sum(v * v, ones) * (1.0 / HEAD_DIM) + EPS)
    return v * r, r


def _half_rms_bwd(dy, vhat, r, ones):
    return r * (dy - vhat * (_half_sum(vhat * dy, ones) * (1.0 / HEAD_DIM)))


def _group_rows(v):
    rows, n = v.shape
    return v.reshape(rows // SUBLANES, SUBLANES, n).sum(axis=0)


def _sigmoid(v):
    return 1.0 / (1.0 + jnp.exp(-v))


def _attn_bias_base(block, seq):
    qi = lax.broadcasted_iota(jnp.int32, (BLK, 3 * BLK), 0)
    ci = lax.broadcasted_iota(jnp.int32, (BLK, 3 * BLK), 1)
    dist = jnp.abs(ci - BLK - qi)
    kpos = (block - 1) * BLK + ci
    valid = (dist <= BLK) & (kpos >= 0) & (kpos < seq)
    return dist.astype(F32), jnp.where(valid, 0.0, NEG_INF)


def _softmax_with_sink(s, sink):
    m = jnp.maximum(jnp.max(s, axis=-1, keepdims=True), sink)
    p = jnp.exp(s - m)
    e_sink = jnp.exp(sink - m)
    inv = 1.0 / (jnp.sum(p, axis=-1, keepdims=True) + e_sink)
    return p * inv, e_sink * inv


def _stage_keys(kvp_ref, qkv_ref, kvn_ref, kg, ones, tile, ks, kr, vs, vr, khat_s=None, rk_s=None):
    pieces = (
        (0, BLK, kvp_ref[:, 0:D_KV], kvp_ref[:, D_KV : 2 * D_KV]),
        (BLK, tile, qkv_ref[:, D_ATTN : D_ATTN + D_KV], qkv_ref[:, D_ATTN + D_KV : D_QKV]),
        (BLK + tile, BLK, kvn_ref[:, 0:D_KV], kvn_ref[:, D_KV : 2 * D_KV]),
    )
    for r0, n, k, v in pieces:
        khat, rk = _half_rms(k, ones)
        kn = khat * kg
        ks[r0 : r0 + n, :] = _mx(kn)
        kr[r0 : r0 + n, :] = _mx(pltpu.roll(kn, HEAD_DIM, 1))
        vs[r0 : r0 + n, :] = _mx(v)
        vr[r0 : r0 + n, :] = _mx(pltpu.roll(v, HEAD_DIM, 1))
        if khat_s is not None:
            khat_s[r0 : r0 + n, :] = khat
            rk_s[r0 : r0 + n, :] = rk


def _halo_specs(tile, seq):
    nb = tile // BLK
    last = seq // BLK - 1
    kv_col = D_ATTN // (2 * D_KV)
    prev = pl.BlockSpec((BLK, 2 * D_KV), lambda i: (jnp.maximum(i * nb - 1, 0), kv_col))
    nxt = pl.BlockSpec((BLK, 2 * D_KV), lambda i: (jnp.minimum((i + 1) * nb, last), kv_col))
    return prev, nxt


def _row_spec(tile, width):
    return pl.BlockSpec((tile, width), lambda i: (i, 0))


def _full_spec(shape):
    nd = len(shape)
    return pl.BlockSpec(shape, lambda i: (0,) * nd)


SMEM_SPEC = pl.BlockSpec(memory_space=pltpu.SMEM)
VMEM_SPEC = pl.BlockSpec(memory_space=pltpu.VMEM)


def _ln_proj_fwd(x, gain, scale1, shift, w_in, name):
    seq, d = x.shape
    tile = min(TOKEN_TILE, seq)

    def body(x_ref, g_ref, s1_ref, sh_ref, w_ref, pa_ref, pb_ref):
        xv = x_ref[...]
        r = lax.rsqrt(jnp.mean(xv * xv, axis=-1, keepdims=True) + EPS)
        h = _mx((xv * r) * g_ref[...] * s1_ref[...] + sh_ref[...])
        pa_ref[...] = _dot(h, w_ref[:, 0:D_QKV])
        pb_ref[...] = _dot(h, w_ref[:, D_QKV:D_IN])

    vec = _full_spec((1, d))
    return pl.pallas_call(
        body,
        name=name,
        grid=(seq // tile,),
        in_specs=[_row_spec(tile, d), vec, vec, vec, _full_spec((d, D_IN))],
        out_specs=[_row_spec(tile, D_QKV), _row_spec(tile, D_REST)],
        out_shape=[jax.ShapeDtypeStruct((seq, D_QKV), F32), jax.ShapeDtypeStruct((seq, D_REST), F32)],
        compiler_params=_params(("parallel",)),
    )(x, gain, scale1, shift, w_in)


def _attn_fwd(pa, q_gain2, k_gain2, sink, name):
    seq = pa.shape[0]
    tile = min(TOKEN_TILE, seq)
    nb = tile // BLK
    ext = tile + 2 * BLK

    def body(sink_ref, qkv_ref, kvp_ref, kvn_ref, qg_ref, kg_ref, o_ref, qs, ks, kr, vs, vr):
        i = pl.program_id(0)
        ones = _half_ones()
        lo = _lane_lo(BLK)
        lo_t = _lane_lo(tile)
        _stage_keys(kvp_ref, qkv_ref, kvn_ref, kg_ref[...], ones, tile, ks, kr, vs, vr)
        for j in range(N_PAIRS):
            cols = slice(j * LANES, (j + 1) * LANES)
            qhat, _ = _half_rms(qkv_ref[:, cols], ones)
            qn = qhat * (qg_ref[...] * Q_SCALE)
            qs[0, :, cols] = _mx(jnp.where(lo_t, qn, 0.0))
            qs[1, :, cols] = _mx(jnp.where(lo_t, 0.0, qn))

        def block(n, carry):
            r0 = pl.multiple_of(n * BLK, BLK)
            dist, base = _attn_bias_base(i * nb + n, seq)
            for j in range(N_PAIRS):
                cols = slice(j * LANES, (j + 1) * LANES)
                outs = []
                for a in range(2):
                    h = 2 * j + a
                    same = a == h // Q_PER_KV
                    kx = (ks if same else kr)[pl.ds(r0, 3 * BLK), :]
                    vx = (vs if same else vr)[pl.ds(r0, 3 * BLK), :]
                    s = _dot_nt(qs[a, pl.ds(r0, BLK), cols], kx) + (base - ALIBI_SLOPES[h] * dist)
                    p, _ = _softmax_with_sink(s, sink_ref[h])
                    outs.append(_dot(_mx(p), vx))
                o_ref[pl.ds(r0, BLK), cols] = jnp.where(lo, outs[0], outs[1])
            return carry

        lax.fori_loop(0, nb, block, 0)

    prev, nxt = _halo_specs(tile, seq)
    vec = _full_spec((1, LANES))
    return pl.pallas_call(
        body,
        name=name,
        grid=(seq // tile,),
        in_specs=[SMEM_SPEC, _row_spec(tile, D_QKV), prev, nxt, vec, vec],
        out_specs=_row_spec(tile, D_ATTN),
        out_shape=jax.ShapeDtypeStruct((seq, D_ATTN), F32),
        scratch_shapes=[
            pltpu.VMEM((2, tile, D_ATTN), MXU_DTYPE),
            pltpu.VMEM((ext, LANES), MXU_DTYPE),
            pltpu.VMEM((ext, LANES), MXU_DTYPE),
            pltpu.VMEM((ext, LANES), MXU_DTYPE),
            pltpu.VMEM((ext, LANES), MXU_DTYPE),
        ],
        compiler_params=_params(("parallel",)),
    )(sink, pa, pa, pa, q_gain2, k_gain2)


def _mix_out_fwd(pb, o, x, gate, w_out, w_s, b_st, name):
    seq, d = x.shape
    tile = min(TOKEN_TILE, seq)
    nb = tile // BLK

    def body(pb_ref, o_ref, x_ref, gate_ref, wo_ref, ws_ref, bs_ref, xo_ref, y_s, vn_s):
        ones = _half_ones()
        lo = _lane_lo(BLK)
        ga = pb_ref[:, 0:D_ATTN]
        y_s[:, 0:D_ATTN] = _mx(o_ref[...] * (ga * _sigmoid(ga)))
        for j in range(N_PAIRS):
            cols = slice(2 * D_GM + j * LANES, 2 * D_GM + (j + 1) * LANES)
            vhat, _ = _half_rms(pb_ref[:, cols], ones)
            vn_s[:, j * LANES : (j + 1) * LANES] = _mx(vhat)

        def chunk(n, carry):
            rows = pl.ds(pl.multiple_of(n * BLK, BLK), BLK)
            for j in range(N_PAIRS):
                cols = slice(j * LANES, (j + 1) * LANES)
                vn = vn_s[rows, cols]
                sv = jnp.where(lo, _dot(ws_ref[2 * j], vn), _dot(ws_ref[2 * j + 1], vn)) + bs_ref[:, cols]
                u = pb_ref[rows, D_ATTN + j * LANES : D_ATTN + (j + 1) * LANES]
                gg = pb_ref[rows, D_ATTN + 2 * D_GM + j * LANES : D_ATTN + 2 * D_GM + (j + 1) * LANES]
                y_s[rows, D_ATTN + j * LANES : D_ATTN + (j + 1) * LANES] = _mx((u * sv) * (gg * _sigmoid(gg)))
            return carry

        lax.fori_loop(0, nb, chunk, 0)
        xo_ref[...] = x_ref[...] + gate_ref[...] * _dot(y_s[...], wo_ref[...])

    return pl.pallas_call(
        body,
        name=name,
        grid=(seq // tile,),
        in_specs=[
            _row_spec(tile, D_REST),
            _row_spec(tile, D_ATTN),
            _row_spec(tile, d),
            _full_spec((1, d)),
            _full_spec((D_MIX, d)),
            _full_spec((N_GROUPS, BLK, BLK)),
            _full_spec((BLK, D_GM)),
        ],
        out_specs=_row_spec(tile, d),
        out_shape=jax.ShapeDtypeStruct((seq, d), F32),
        scratch_shapes=[pltpu.VMEM((tile, D_MIX), MXU_DTYPE), pltpu.VMEM((tile, D_GM), MXU_DTYPE)],
        compiler_params=_params(("parallel",)),
    )(pb, o, x, gate, w_out, w_s, b_st)


def _loss_grad(y, target):
    seq, d = y.shape
    tile = min(TOKEN_TILE, seq)

    def body(y_ref, t_ref, dy_ref, acc_ref):
        @pl.when(pl.program_id(0) == 0)
        def _():
            acc_ref[...] = jnp.zeros_like(acc_ref)

        e = y_ref[...] - t_ref[...]
        dy_ref[...] = e * (1.0 / d)
        acc_ref[...] += jnp.sum(jnp.sum(e * e, axis=-1, keepdims=True), axis=0, keepdims=True)

    return pl.pallas_call(
        body,
        name="loss_grad",
        grid=(seq // tile,),
        in_specs=[_row_spec(tile, d), _row_spec(tile, d)],
        out_specs=[_row_spec(tile, d), _full_spec((SUBLANES, LANES))],
        out_shape=[jax.ShapeDtypeStruct((seq, d), F32), jax.ShapeDtypeStruct((SUBLANES, LANES), F32)],
        compiler_params=_params(("arbitrary",)),
    )(y, target)


def _mix_out_bwd(dxn, pb, o, gate, w_out_t, w_s, w_s_t, b_st, name):
    seq, d = dxn.shape
    tile = min(TOKEN_TILE, seq)
    nb = tile // BLK

    def body(dxn_ref, pb_ref, o_ref, gate_ref, wot_ref, ws_ref, wst_ref, bs_ref,
             dpb_ref, do_ref, g_ref, dws_ref, dbs_ref, y_s, dy_s, vn_s, rv_s):
        @pl.when(pl.program_id(0) == 0)
        def _():
            g_ref[...] = jnp.zeros_like(g_ref)
            dws_ref[...] = jnp.zeros_like(dws_ref)
            dbs_ref[...] = jnp.zeros_like(dbs_ref)

        ones = _half_ones()
        lo = _lane_lo(BLK)
        dxv = dxn_ref[...]
        dy_s[...] = _dot(_mx(dxv * gate_ref[...]), wot_ref[...])
        ga = pb_ref[:, 0:D_ATTN]
        sig = _sigmoid(ga)
        sil = ga * sig
        ov = o_ref[...]
        y_s[:, 0:D_ATTN] = _mx(ov * sil)
        da = dy_s[:, 0:D_ATTN]
        do_ref[...] = da * sil
        dpb_ref[:, 0:D_ATTN] = (da * ov * (sig * (1.0 + ga * (1.0 - sig)))).astype(dpb_ref.dtype)
        for j in range(N_PAIRS):
            cols = slice(j * LANES, (j + 1) * LANES)
            vhat, rv = _half_rms(pb_ref[:, 2 * D_GM + j * LANES : 2 * D_GM + (j + 1) * LANES], ones)
            vn_s[:, cols] = vhat
            rv_s[:, cols] = rv

        def chunk(n, carry):
            rows = pl.ds(pl.multiple_of(n * BLK, BLK), BLK)
            for j in range(N_PAIRS):
                cols = slice(j * LANES, (j + 1) * LANES)
                c_u = slice(D_ATTN + j * LANES, D_ATTN + (j + 1) * LANES)
                c_vg = slice(D_ATTN + D_GM + j * LANES, D_ATTN + D_GM + (j + 1) * LANES)
                c_gg = slice(D_ATTN + 2 * D_GM + j * LANES, D_ATTN + 2 * D_GM + (j + 1) * LANES)
                vhat = vn_s[rows, cols]
                vn = _mx(vhat)
                sv = jnp.where(lo, _dot(ws_ref[2 * j], vn), _dot(ws_ref[2 * j + 1], vn)) + bs_ref[:, cols]
                u = pb_ref[rows, c_u]
                gg = pb_ref[rows, c_gg]
                sg = _sigmoid(gg)
                silg = gg * sg
                m0 = u * sv
                y_s[rows, c_u] = _mx(m0 * silg)
                dm = dy_s[rows, c_u]
                dm0 = dm * silg
                dpb_ref[rows, c_gg] = (dm * m0 * (sg * (1.0 + gg * (1.0 - sg)))).astype(dpb_ref.dtype)
                dpb_ref[rows, c_u] = (dm0 * sv).astype(dpb_ref.dtype)
                dsv = dm0 * u
                dbs_ref[:, cols] += dsv
                dws_ref[2 * j] += _dot_nt(_mx(jnp.where(lo, dsv, 0.0)), vn)
                dws_ref[2 * j + 1] += _dot_nt(_mx(jnp.where(lo, 0.0, dsv)), vn)
                dsv_m = _mx(dsv)
                dvn = jnp.where(lo, _dot(wst_ref[2 * j], dsv_m), _dot(wst_ref[2 * j + 1], dsv_m))
                dpb_ref[rows, c_vg] = _half_rms_bwd(dvn, vhat, rv_s[rows, cols], ones).astype(dpb_ref.dtype)
            return carry

        lax.fori_loop(0, nb, chunk, 0)
        g_ref[...] += _dot_tn(y_s[...], _mx(dxv))

    return pl.pallas_call(
        body,
        name=name,
        grid=(seq // tile,),
        in_specs=[
            _row_spec(tile, d),
            _row_spec(tile, D_REST),
            _row_spec(tile, D_ATTN),
            _full_spec((1, d)),
            _full_spec((d, D_MIX)),
            _full_spec((N_GROUPS, BLK, BLK)),
            _full_spec((N_GROUPS, BLK, BLK)),
            _full_spec((BLK, D_GM)),
        ],
        out_specs=[
            _row_spec(tile, D_REST),
            _row_spec(tile, D_ATTN),
            _full_spec((D_MIX, d)),
            _full_spec((N_GROUPS, BLK, BLK)),
            _full_spec((BLK, D_GM)),
        ],
        out_shape=[
            jax.ShapeDtypeStruct((seq, D_REST), MXU_DTYPE),
            jax.ShapeDtypeStruct((seq, D_ATTN), F32),
            jax.ShapeDtypeStruct((D_MIX, d), F32),
            jax.ShapeDtypeStruct((N_GROUPS, BLK, BLK), F32),
            jax.ShapeDtypeStruct((BLK, D_GM), F32),
        ],
        scratch_shapes=[
            pltpu.VMEM((tile, D_MIX), MXU_DTYPE),
            pltpu.VMEM((tile, D_MIX), F32),
            pltpu.VMEM((tile, D_GM), F32),
            pltpu.VMEM((tile, D_GM), F32),
        ],
        compiler_params=_params(("arbitrary",)),
    )(dxn, pb, o, gate, w_out_t, w_s, w_s_t, b_st)


def _attn_bwd(pa, o, do, q_gain2, k_gain2, sink, name):
    seq = pa.shape[0]
    tile = min(TOKEN_TILE, seq)
    nb = tile // BLK
    nt = seq // tile
    ext = tile + 2 * BLK

    def body(sink_ref, qkv_ref, kvp_ref, kvn_ref, o_ref, do_ref, qg_ref, kg_ref,
             dq_ref, dkv_ref, hp_ref, hn_ref, dqg_ref, dkg_ref, dsk_ref,
             qs, qhat_s, rq_s, ks, kr, vs, vr, khat_s, rk_s, dom, dqn_s, dka, dva):
        i = pl.program_id(0)

        @pl.when(i == 0)
        def _():
            dqg_ref[...] = jnp.zeros_like(dqg_ref)
            dkg_ref[...] = jnp.zeros_like(dkg_ref)
            dsk_ref[...] = jnp.zeros_like(dsk_ref)

        ones = _half_ones()
        lo = _lane_lo(BLK)
        lo_t = _lane_lo(tile)
        qg = qg_ref[...] * Q_SCALE
        kg = kg_ref[...]
        _stage_keys(kvp_ref, qkv_ref, kvn_ref, kg, ones, tile, ks, kr, vs, vr, khat_s, rk_s)
        for j in range(N_PAIRS):
            cols = slice(j * LANES, (j + 1) * LANES)
            qhat, rq = _half_rms(qkv_ref[:, cols], ones)
            qhat_s[:, cols] = qhat
            rq_s[:, cols] = rq
            qn = qhat * qg
            qs[0, :, cols] = _mx(jnp.where(lo_t, qn, 0.0))
            qs[1, :, cols] = _mx(jnp.where(lo_t, 0.0, qn))
            dov = do_ref[:, cols]
            dom[0, :, cols] = jnp.where(lo_t, dov, 0.0)
            dom[1, :, cols] = jnp.where(lo_t, 0.0, dov)
        dka[...] = jnp.zeros_like(dka)
        dva[...] = jnp.zeros_like(dva)
        head_lane = lax.broadcasted_iota(jnp.int32, (1, LANES), 1)

        def block(n, dsink):
            r0 = pl.multiple_of(n * BLK, BLK)
            rows = pl.ds(r0, BLK)
            krows = pl.ds(r0, 3 * BLK)
            dist, base = _attn_bias_base(i * nb + n, seq)
            for j in range(N_PAIRS):
                cols = slice(j * LANES, (j + 1) * LANES)
                ov = o_ref[rows, cols]
                dqs = []
                for a in range(2):
                    h = 2 * j + a
                    var = 0 if a == h // Q_PER_KV else 1
                    kx = (kr if var else ks)[krows, :]
                    vx = (vr if var else vs)[krows, :]
                    qm = qs[a, rows, cols]
                    dof = dom[a, rows, cols]
                    dob = _mx(dof)
                    s = _dot_nt(qm, kx) + (base - ALIBI_SLOPES[h] * dist)
                    p, p_sink = _softmax_with_sink(s, sink_ref[h])
                    dcol = jnp.sum(dof * ov, axis=-1, keepdims=True)
                    ds = _mx(p * (_dot_nt(dob, vx) - dcol))
                    dsink = dsink - jnp.where(head_lane == h, jnp.sum(p_sink * dcol), 0.0)
                    dqs.append(_dot(ds, kx))
                    dka[var, krows, :] += _dot_tn(ds, qm)
                    dva[var, krows, :] += _dot_tn(_mx(p), dob)
                dqn_s[rows, cols] = jnp.where(lo, dqs[0], dqs[1])
            return dsink

        dsink = lax.fori_loop(0, nb, block, jnp.zeros((1, LANES), F32))
        dsk_ref[...] += jnp.broadcast_to(dsink, (SUBLANES, LANES))
        for j in range(N_PAIRS):
            cols = slice(j * LANES, (j + 1) * LANES)
            dqn = dqn_s[:, cols]
            qhat = qhat_s[:, cols]
            dqg_ref[:, cols] += _group_rows(dqn * qhat) * Q_SCALE
            dq_ref[:, cols] = _half_rms_bwd(dqn * qg, qhat, rq_s[:, cols], ones).astype(dq_ref.dtype)
        dkn = dka[0] + pltpu.roll(dka[1], HEAD_DIM, 1)
        khat = khat_s[...]
        dkg_ref[...] += _group_rows(dkn * khat)
        dk = _half_rms_bwd(dkn * kg, khat, rk_s[...], ones)
        dv = dva[0] + pltpu.roll(dva[1], HEAD_DIM, 1)
        hp_ref[:, 0:D_KV] = dk[0:BLK]
        hp_ref[:, D_KV : 2 * D_KV] = dv[0:BLK]
        dkv_ref[:, 0:D_KV] = dk[BLK : BLK + tile]
        dkv_ref[:, D_KV : 2 * D_KV] = dv[BLK : BLK + tile]
        hn_ref[:, 0:D_KV] = dk[BLK + tile : ext]
        hn_ref[:, D_KV : 2 * D_KV] = dv[BLK + tile : ext]

    prev, nxt = _halo_specs(tile, seq)
    vec = _full_spec((1, LANES))
    halo = pl.BlockSpec((None, BLK, 2 * D_KV), lambda i: (i, 0, 0))
    return pl.pallas_call(
        body,
        name=name,
        grid=(nt,),
        in_specs=[SMEM_SPEC, _row_spec(tile, D_QKV), prev, nxt, _row_spec(tile, D_ATTN), _row_spec(tile, D_ATTN), vec, vec],
        out_specs=[
            _row_spec(tile, D_ATTN),
            _row_spec(tile, 2 * D_KV),
            halo,
            halo,
            _full_spec((SUBLANES, D_ATTN)),
            _full_spec((SUBLANES, LANES)),
            _full_spec((SUBLANES, LANES)),
        ],
        out_shape=[
            jax.ShapeDtypeStruct((seq, D_ATTN), MXU_DTYPE),
            jax.ShapeDtypeStruct((seq, 2 * D_KV), F32),
            jax.ShapeDtypeStruct((nt, BLK, 2 * D_KV), F32),
            jax.ShapeDtypeStruct((nt, BLK, 2 * D_KV), F32),
            jax.ShapeDtypeStruct((SUBLANES, D_ATTN), F32),
            jax.ShapeDtypeStruct((SUBLANES, LANES), F32),
            jax.ShapeDtypeStruct((SUBLANES, LANES), F32),
        ],
        scratch_shapes=[
            pltpu.VMEM((2, tile, D_ATTN), MXU_DTYPE),
            pltpu.VMEM((tile, D_ATTN), F32),
            pltpu.VMEM((tile, D_ATTN), F32),
            pltpu.VMEM((ext, LANES), MXU_DTYPE),
            pltpu.VMEM((ext, LANES), MXU_DTYPE),
            pltpu.VMEM((ext, LANES), MXU_DTYPE),
            pltpu.VMEM((ext, LANES), MXU_DTYPE),
            pltpu.VMEM((ext, LANES), F32),
            pltpu.VMEM((ext, LANES), F32),
            pltpu.VMEM((2, tile, D_ATTN), F32),
            pltpu.VMEM((tile, D_ATTN), F32),
            pltpu.VMEM((2, ext, LANES), F32),
            pltpu.VMEM((2, ext, LANES), F32),
        ],
        compiler_params=_params(("arbitrary",)),
    )(sink, pa, pa, pa, o, do, q_gain2, k_gain2)


def _halo_in_specs(tile, nt):
    from_prev = pl.BlockSpec((None, BLK, 2 * D_KV), lambda i: (jnp.maximum(i - 1, 0), 0, 0))
    from_next = pl.BlockSpec((None, BLK, 2 * D_KV), lambda i: (jnp.minimum(i + 1, nt - 1), 0, 0))
    return from_prev, from_next


def _proj_bwd_dx(x, dxn, dq, dkv, halo_prev, halo_next, dpb, w_in_t, gain, scale1, name):
    seq, d = x.shape
    tile = min(TOKEN_TILE, seq)
    nt = seq // tile

    def body(x_ref, dxn_ref, dq_ref, dkv_ref, hn_ref, hp_ref, dpb_ref, wt_ref, g_ref, s1_ref,
             dx_ref, dkvb_ref, c0_ref, c1_ref):
        i = pl.program_id(0)

        @pl.when(i == 0)
        def _():
            c0_ref[...] = jnp.zeros_like(c0_ref)
            c1_ref[...] = jnp.zeros_like(c1_ref)

        top = dkv_ref[0:BLK, :] + jnp.where(i > 0, hn_ref[...], 0.0)
        bot = dkv_ref[tile - BLK : tile, :] + jnp.where(i < nt - 1, hp_ref[...], 0.0)
        if tile == BLK:
            dkvb_ref[...] = (top + bot - dkv_ref[...]).astype(dkvb_ref.dtype)
        else:
            dkvb_ref[0:BLK, :] = top.astype(dkvb_ref.dtype)
            dkvb_ref[tile - BLK : tile, :] = bot.astype(dkvb_ref.dtype)
            if tile > 2 * BLK:
                dkvb_ref[BLK : tile - BLK, :] = dkv_ref[BLK : tile - BLK, :].astype(dkvb_ref.dtype)
        dh = (
            _dot(dq_ref[...], wt_ref[0:D_ATTN, :])
            + _dot(dkvb_ref[...], wt_ref[D_ATTN:D_QKV, :])
            + _dot(dpb_ref[...], wt_ref[D_QKV:D_IN, :])
        )
        xv = x_ref[...]
        r = lax.rsqrt(jnp.mean(xv * xv, axis=-1, keepdims=True) + EPS)
        xn = xv * r
        c0_ref[...] += _group_rows(dh)
        c1_ref[...] += _group_rows(dh * xn)
        dxn_ = dh * (g_ref[...] * s1_ref[...])
        dx_ref[...] = dxn_ref[...] + r * (dxn_ - xn * jnp.mean(xn * dxn_, axis=-1, keepdims=True))

    from_prev, from_next = _halo_in_specs(tile, nt)
    vec = _full_spec((1, d))
    return pl.pallas_call(
        body,
        name=name,
        grid=(nt,),
        in_specs=[
            _row_spec(tile, d),
            _row_spec(tile, d),
            _row_spec(tile, D_ATTN),
            _row_spec(tile, 2 * D_KV),
            from_prev,
            from_next,
            _row_spec(tile, D_REST),
            _full_spec((D_IN, d)),
            vec,
            vec,
        ],
        out_specs=[_row_spec(tile, d), _row_spec(tile, 2 * D_KV), _full_spec((SUBLANES, d)), _full_spec((SUBLANES, d))],
        out_shape=[
            jax.ShapeDtypeStruct((seq, d), F32),
            jax.ShapeDtypeStruct((seq, 2 * D_KV), MXU_DTYPE),
            jax.ShapeDtypeStruct((SUBLANES, d), F32),
            jax.ShapeDtypeStruct((SUBLANES, d), F32),
        ],
        compiler_params=_params(("arbitrary",)),
    )(x, dxn, dq, dkv, halo_next, halo_prev, dpb, w_in_t, gain, scale1)


def _proj_bwd_dw(x, gain, scale1, shift, dq, dkvb, dpb, name):
    seq, d = x.shape
    tile = min(TOKEN_TILE, seq)

    def body(x_ref, g_ref, s1_ref, sh_ref, dq_ref, dkv_ref, dpb_ref, dw_ref):
        @pl.when(pl.program_id(0) == 0)
        def _():
            dw_ref[...] = jnp.zeros_like(dw_ref)

        xv = x_ref[...]
        r = lax.rsqrt(jnp.mean(xv * xv, axis=-1, keepdims=True) + EPS)
        h = _mx((xv * r) * g_ref[...] * s1_ref[...] + sh_ref[...])
        dw_ref[:, 0:D_ATTN] += _dot_tn(h, dq_ref[...])
        dw_ref[:, D_ATTN:D_QKV] += _dot_tn(h, dkv_ref[...])
        dw_ref[:, D_QKV:D_IN] += _dot_tn(h, dpb_ref[...])

    vec = _full_spec((1, d))
    return pl.pallas_call(
        body,
        name=name,
        grid=(seq // tile,),
        in_specs=[_row_spec(tile, d), vec, vec, vec, _row_spec(tile, D_ATTN), _row_spec(tile, 2 * D_KV), _row_spec(tile, D_REST)],
        out_specs=_full_spec((d, D_IN)),
        out_shape=jax.ShapeDtypeStruct((d, D_IN), F32),
        compiler_params=_params(("arbitrary",)),
    )(x, gain, scale1, shift, dq, dkvb, dpb)


def _w_out_finish(g, w_out, gate, name):
    d_mix, d = g.shape

    def body(g_ref, w_ref, gate_ref, dw_ref, dgate_ref):
        gv = g_ref[...]
        dw_ref[...] = gv * gate_ref[...]
        dgate_ref[...] = _group_rows(gv * w_ref[...].astype(F32))

    return pl.pallas_call(
        body,
        name=name,
        in_specs=[VMEM_SPEC, VMEM_SPEC, VMEM_SPEC],
        out_specs=[VMEM_SPEC, VMEM_SPEC],
        out_shape=[jax.ShapeDtypeStruct((d_mix, d), F32), jax.ShapeDtypeStruct((SUBLANES, d), F32)],
        compiler_params=_params(),
    )(g, w_out, gate)


def _adamw_math(w, g, m, v):
    m = ADAM_B1 * m + (1.0 - ADAM_B1) * g
    v = ADAM_B2 * v + (1.0 - ADAM_B2) * (g * g)
    m_hat = m / (1.0 - ADAM_B1**ADAM_STEP)
    v_hat = v / (1.0 - ADAM_B2**ADAM_STEP)
    delta = -ADAM_LR * (m_hat / (jnp.sqrt(v_hat) + ADAM_EPS) + ADAM_WD * w)
    return delta, m, v


def _adamw(w, g, m, v, name):
    rows, cols = w.shape
    tile = min(TOKEN_TILE, rows)

    def body(w_ref, g_ref, m_ref, v_ref, d_ref, mo_ref, vo_ref):
        d_ref[...], mo_ref[...], vo_ref[...] = _adamw_math(w_ref[...], g_ref[...], m_ref[...], v_ref[...])

    spec = _row_spec(tile, cols)
    shape = jax.ShapeDtypeStruct((rows, cols), F32)
    return pl.pallas_call(
        body,
        name=name,
        grid=(rows // tile,),
        in_specs=[spec] * 4,
        out_specs=[spec] * 3,
        out_shape=[shape] * 3,
        compiler_params=_params(("parallel",)),
    )(w, g, m, v)


def _small_update(gathered, w, m, v):
    rows = w.shape[0]

    def body(ga_ref, w_ref, m_ref, v_ref, g_ref, d_ref, mo_ref, vo_ref):
        g = ga_ref[0]
        for j in range(1, N_DEV):
            g = g + ga_ref[j]
        g_ref[...] = g
        d_ref[...], mo_ref[...], vo_ref[...] = _adamw_math(w_ref[...], g, m_ref[...], v_ref[...])

    shape = jax.ShapeDtypeStruct((rows, LANES), F32)
    return pl.pallas_call(
        body,
        name="small_update",
        in_specs=[VMEM_SPEC] * 4,
        out_specs=[VMEM_SPEC] * 4,
        out_shape=[shape] * 4,
        compiler_params=_params(),
    )(gathered, w, m, v)


def _ada_weight_grad(c_all, d_ada_cols):
    d = c_all.shape[-1]
    n_layers, _, width = d_ada_cols.shape

    def body(c_ref, da_ref, dw_ref):
        cv = c_ref[...]
        cond = cv * _sigmoid(cv)
        for l in range(n_layers):
            dw_ref[l] = lax.dot_general(
                cond, da_ref[l], (((0,), (0,)), ((), ())), preferred_element_type=F32, precision=lax.Precision.HIGHEST
            )

    return pl.pallas_call(
        body,
        name="ada_weight_grad",
        in_specs=[VMEM_SPEC, VMEM_SPEC],
        out_specs=VMEM_SPEC,
        out_shape=jax.ShapeDtypeStruct((n_layers, d, width), F32),
        compiler_params=_params(),
    )(c_all, d_ada_cols)


def _position():
    return lax.axis_index("x"), lax.axis_index("y"), lax.axis_index("c")


def _flip(pos, k):
    x, y, c = pos
    return (1 - x if k & 4 else x, 1 - y if k & 2 else y, 1 - c if k & 1 else c)


def _index(pos):
    x, y, c = pos
    return 4 * x + 2 * y + c


def _remote(src, dst, send_sem, recv_sem, to):
    return pltpu.make_async_remote_copy(
        src_ref=src, dst_ref=dst, send_sem=send_sem, recv_sem=recv_sem, device_id=to, device_id_type=MESH_ID
    )


def _two_level_all_gather(refs, send_sems, recv_sems):
    me = _position()
    sibling = _flip(me, 1)
    others = (4, 2, 6)

    def copy(t, k, block, to):
        slot = refs[t].at[_index(block)]
        return _remote(slot, slot, send_sems.at[7 * t + k], recv_sems.at[7 * t + k], to)

    started = []
    for t in range(len(refs)):
        started.append(copy(t, 0, me, sibling))
        started += [copy(t, 1 + j, me, _flip(me, f)) for j, f in enumerate(others)]
    for cp in started:
        cp.start()
    for j, f in enumerate(others):
        for t in range(len(refs)):
            copy(t, 1 + j, _flip(me, f), me).wait_recv()
            passed = copy(t, 4 + j, _flip(me, f), sibling)
            passed.start()
            started.append(passed)
    for t in range(len(refs)):
        copy(t, 0, sibling, me).wait_recv()
        for j, f in enumerate(others):
            copy(t, 4 + j, _flip(sibling, f), me).wait_recv()
    for cp in started:
        cp.wait_send()


def _ada_rows(c_row, w_ada):
    d = c_row.shape[-1]
    n_layers, _, width = w_ada.shape

    def body(c_ref, w_ref, call_ref, parts_ref, sbuf, sem_s1, sem_r1, sem_s2, sem_r2):
        me = _position()
        my = _index(me)
        call_ref[my] = jnp.broadcast_to(c_ref[...], (SUBLANES, d))
        mine = call_ref.at[my]
        first = [_remote(mine, mine, sem_s1.at[k - 1], sem_r1.at[k - 1], _flip(me, k)) for k in range(1, N_DEV)]
        for cp in first:
            cp.start()
        for k in range(1, N_DEV):
            theirs = call_ref.at[_index(_flip(me, k))]
            _remote(theirs, theirs, sem_s1.at[k - 1], sem_r1.at[k - 1], _flip(me, k)).wait_recv()
        for b in range(N_DEV):
            cv = call_ref[b]
            cond = cv * _sigmoid(cv)
            for l in range(n_layers):
                sbuf[b, l] = jnp.dot(cond, w_ref[l], preferred_element_type=F32, precision=lax.Precision.HIGHEST)
        parts_ref[my] = sbuf[my]
        second = []
        for k in range(1, N_DEV):
            to = _flip(me, k)
            second.append(_remote(sbuf.at[_index(to)], parts_ref.at[my], sem_s2.at[k - 1], sem_r2.at[k - 1], to))
        for cp in second:
            cp.start()
        for k in range(1, N_DEV):
            theirs = parts_ref.at[_index(_flip(me, k))]
            _remote(theirs, theirs, sem_s2.at[k - 1], sem_r2.at[k - 1], _flip(me, k)).wait_recv()
        for cp in first + second:
            cp.wait_send()

    return pl.pallas_call(
        body,
        name="ada_rows",
        in_specs=[VMEM_SPEC, VMEM_SPEC],
        out_specs=[VMEM_SPEC, VMEM_SPEC],
        out_shape=[
            jax.ShapeDtypeStruct((N_DEV, SUBLANES, d), F32),
            jax.ShapeDtypeStruct((N_DEV, n_layers, SUBLANES, width), F32),
        ],
        scratch_shapes=[
            pltpu.VMEM((N_DEV, n_layers, SUBLANES, width), F32),
            pltpu.SemaphoreType.DMA((N_DEV - 1,)),
            pltpu.SemaphoreType.DMA((N_DEV - 1,)),
            pltpu.SemaphoreType.DMA((N_DEV - 1,)),
            pltpu.SemaphoreType.DMA((N_DEV - 1,)),
        ],
        compiler_params=_params(),
    )(c_row, w_ada)


def _gather_weights(w_in_dense, w_out):
    def body(wi_ref, wo_ref, gi_ref, go_ref, send_sems, recv_sems):
        my = _index(_position())
        gi_ref[my] = wi_ref[...].astype(gi_ref.dtype)
        go_ref[my] = wo_ref[...].astype(go_ref.dtype)
        _two_level_all_gather((gi_ref, go_ref), send_sems, recv_sems)

    return pl.pallas_call(
        body,
        name="gather_weights",
        in_specs=[VMEM_SPEC, VMEM_SPEC],
        out_specs=[VMEM_SPEC, VMEM_SPEC],
        out_shape=[
            jax.ShapeDtypeStruct((N_DEV,) + w_in_dense.shape, MXU_DTYPE),
            jax.ShapeDtypeStruct((N_DEV,) + w_out.shape, MXU_DTYPE),
        ],
        scratch_shapes=[pltpu.SemaphoreType.DMA((14,)), pltpu.SemaphoreType.DMA((14,))],
        compiler_params=_params(),
    )(w_in_dense, w_out)


def _gather_small(packed):
    def body(p_ref, g_ref, send_sems, recv_sems):
        g_ref[_index(_position())] = p_ref[...]
        _two_level_all_gather((g_ref,), send_sems, recv_sems)

    return pl.pallas_call(
        body,
        name="gather_small",
        in_specs=[VMEM_SPEC],
        out_specs=VMEM_SPEC,
        out_shape=jax.ShapeDtypeStruct((N_DEV,) + packed.shape, F32),
        scratch_shapes=[pltpu.SemaphoreType.DMA((7,)), pltpu.SemaphoreType.DMA((7,))],
        compiler_params=_params(),
    )(packed)


def _reduce_scatter(blocks_a, blocks_b, name):
    def body(a_ref, b_ref, oa_ref, ob_ref, half_a, half_b, send_a, send_b, chips_a, chips_b, send_sems, recv_sems):
        me = _position()
        x, y, c = me
        sibling = _flip(me, 1)
        my_chip = 2 * x + y
        others = (4, 2, 6)
        pairs = ((a_ref, half_a, send_a, chips_a, oa_ref), (b_ref, half_b, send_b, chips_b, ob_ref))
        to_sibling = []
        for t, (src, half, _, _, _) in enumerate(pairs):
            cp = _remote(src.at[pl.ds(4 * (1 - c), 4)], half, send_sems.at[4 * t], recv_sems.at[4 * t], sibling)
            cp.start()
            to_sibling.append(cp)
        to_chips = []
        for t, (src, half, send, chips, _) in enumerate(pairs):
            to_sibling[t].wait_recv()
            chips[my_chip] = src[4 * c + my_chip] + half[my_chip]
            for j, f in enumerate(others):
                px, py, _ = _flip(me, f)
                chip = 2 * px + py
                send[j] = src[4 * c + chip] + half[chip]
                cp = _remote(send.at[j], chips.at[my_chip], send_sems.at[4 * t + 1 + j], recv_sems.at[4 * t + 1 + j], _flip(me, f))
                cp.start()
                to_chips.append(cp)
        for t, (_, _, send, chips, out) in enumerate(pairs):
            for j, f in enumerate(others):
                px, py, _ = _flip(me, f)
                slot = chips.at[2 * px + py]
                _remote(slot, slot, send_sems.at[4 * t + 1 + j], recv_sems.at[4 * t + 1 + j], _flip(me, f)).wait_recv()
            out[...] = ((chips[0] + chips[1]) + chips[2]) + chips[3]
        for cp in to_sibling + to_chips:
            cp.wait_send()

    def scratch(blocks):
        blk = blocks.shape[1:]
        return [pltpu.VMEM((4,) + blk, F32), pltpu.VMEM((3,) + blk, F32), pltpu.VMEM((4,) + blk, F32)]

    sa, sb = scratch(blocks_a), scratch(blocks_b)
    return pl.pallas_call(
        body,
        name=name,
        in_specs=[VMEM_SPEC, VMEM_SPEC],
        out_specs=[VMEM_SPEC, VMEM_SPEC],
        out_shape=[jax.ShapeDtypeStruct(blocks_a.shape[1:], F32), jax.ShapeDtypeStruct(blocks_b.shape[1:], F32)],
        scratch_shapes=[sa[0], sb[0], sa[1], sb[1], sa[2], sb[2], pltpu.SemaphoreType.DMA((8,)), pltpu.SemaphoreType.DMA((8,))],
        compiler_params=_params(),
    )(blocks_a, blocks_b)


def _by_core_chip(blocks):
    rest = blocks.shape[1:]
    return blocks.reshape((2, 2, 2) + rest).transpose((2, 0, 1) + tuple(range(3, 3 + len(rest)))).reshape((N_DEV,) + rest)


def _pack_rows(parts):
    rows, offsets, at = [], [], 0
    for p in parts:
        flat = p.reshape(-1)
        n = -(-flat.shape[0] // (SUBLANES * LANES)) * SUBLANES
        rows.append(jnp.pad(flat, (0, n * LANES - flat.shape[0])).reshape(n, LANES))
        offsets.append(at)
        at += n
    return jnp.concatenate(rows, axis=0), offsets


def _unpack_rows(packed, offsets, shapes):
    out = []
    for off, shape in zip(offsets, shapes):
        size = 1
        for s in shape:
            size *= s
        n = -(-size // (SUBLANES * LANES)) * SUBLANES
        out.append(packed[off : off + n].reshape(-1)[:size].reshape(shape))
    return out


def kernel(x, c, w_ada, b_ada, norm_gain, w_in, q_gain, k_gain, sink, w_s, b_s, w_out, loss_target, m_w_ada, m_b_ada, m_norm_gain, m_w_in, m_q_gain, m_k_gain, m_sink, m_w_s, m_b_s, m_w_out, v_w_ada, v_b_ada, v_norm_gain, v_w_in, v_q_gain, v_k_gain, v_sink, v_w_s, v_b_s, v_w_out):
    seq, d = x.shape[1], x.shape[2]
    n_layers = w_in.shape[0]
    w_cols = w_in.shape[2]
    ada_cols = w_ada.shape[2]
    my = _index(_position())
    xs = x.reshape(seq, d)
    target = loss_target.reshape(seq, d)

    c_all, ada_parts = _ada_rows(c, w_ada)
    ada = ada_parts[:, :, 0, :].transpose(1, 0, 2).reshape(n_layers, 3 * d) + b_ada
    shift, scale1, gate = ada[:, None, 0:d], 1.0 + ada[:, None, d : 2 * d], ada[:, None, 2 * d : 3 * d]
    gain = norm_gain[:, None, :]

    dense_rows = d * w_cols // LANES
    gi, go = _gather_weights(w_in.reshape(n_layers, dense_rows, LANES), w_out)
    w_in_full = gi.reshape(N_DEV, n_layers, d, w_cols).transpose(1, 2, 0, 3).reshape(n_layers, d, D_IN)
    w_in_t = w_in_full.transpose(0, 2, 1)
    w_out_full = go.transpose(1, 0, 2, 3).reshape(n_layers, D_MIX, d)
    w_out_t = w_out_full.transpose(0, 2, 1)
    w_s_m = w_s.astype(MXU_DTYPE)
    w_s_t = w_s_m.transpose(0, 1, 3, 2)
    b_st = jnp.repeat(b_s.transpose(0, 2, 1), HEAD_DIM, axis=2)
    q_gain2 = jnp.tile(q_gain, (1, 2))[:, None, :]
    k_gain2 = jnp.tile(k_gain, (1, 2))[:, None, :]

    xl, saved = xs, []
    for l in range(n_layers):
        pa, pb = _ln_proj_fwd(xl, gain[l], scale1[l], shift[l], w_in_full[l], f"ln_proj_fwd_{l}")
        o = _attn_fwd(pa, q_gain2[l], k_gain2[l], sink[l], f"attn_fwd_{l}")
        x_next = _mix_out_fwd(pb, o, xl, gate[l], w_out_full[l], w_s_m[l], b_st[l], f"mix_out_fwd_{l}")
        saved.append((xl, pa, pb, o))
        xl = x_next
    dx, sq_err = _loss_grad(xl, target)
    loss = lax.psum(sq_err[0, 0] * (0.5 / d), ("x", "y", "c"))

    g_w_in, g_w_out, small, d_ada_rows = [None] * n_layers, [None] * n_layers, [None] * n_layers, [None] * n_layers
    for l in reversed(range(n_layers)):
        x_l, pa, pb, o = saved[l]
        dpb, do, g_acc, d_ws, d_bs = _mix_out_bwd(dx, pb, o, gate[l], w_out_t[l], w_s_m[l], w_s_t[l], b_st[l], f"mix_out_bwd_{l}")
        dq, dkv, halo_prev, halo_next, d_qg, d_kg, d_sk = _attn_bwd(pa, o, do, q_gain2[l], k_gain2[l], sink[l], f"attn_bwd_{l}")
        dx, dkvb, c0, c1 = _proj_bwd_dx(x_l, dx, dq, dkv, halo_prev, halo_next, dpb, w_in_t[l], gain[l], scale1[l], f"proj_bwd_dx_{l}")
        dw_in = _proj_bwd_dw(x_l, gain[l], scale1[l], shift[l], dq, dkvb, dpb, f"proj_bwd_dw_{l}")
        dw_out, d_gate8 = _w_out_finish(g_acc, w_out_full[l], gate[l], f"w_out_finish_{l}")
        blocks_in = _by_core_chip(dw_in.reshape(d, N_DEV, w_cols).transpose(1, 0, 2).reshape(N_DEV, dense_rows, LANES))
        blocks_out = _by_core_chip(dw_out.reshape(N_DEV, D_MIX // N_DEV, d))
        r_in, r_out = _reduce_scatter(blocks_in, blocks_out, f"reduce_scatter_{l}")
        g_w_in[l] = r_in.reshape(d, w_cols)
        g_w_out[l] = r_out
        c0s, c1s = c0.sum(axis=0), c1.sum(axis=0)
        d_ada_rows[l] = jnp.concatenate([c0s, norm_gain[l] * c1s, d_gate8.sum(axis=0)])
        small[l] = (
            scale1[l, 0] * c1s,
            d_qg.sum(axis=0).reshape(N_HEADS, HEAD_DIM).sum(axis=0),
            d_kg.sum(axis=0).reshape(2, HEAD_DIM).sum(axis=0),
            d_sk[0, 0:N_HEADS],
            d_ws,
            d_bs.reshape(BLK, N_GROUPS, HEAD_DIM).sum(axis=2).transpose(1, 0),
        )

    names = ("norm_gain", "q_gain", "k_gain", "sink", "w_s", "b_s")
    stacked = [jnp.stack([small[l][t] for l in range(n_layers)]) for t in range(len(names))]
    d_ada = jnp.stack(d_ada_rows)
    packed, offsets = _pack_rows(stacked + [d_ada])
    gathered = _gather_small(packed)
    weights = (norm_gain, q_gain, k_gain, sink, w_s, b_s, b_ada)
    moments_m = (m_norm_gain, m_q_gain, m_k_gain, m_sink, m_w_s, m_b_s, m_b_ada)
    moments_v = (v_norm_gain, v_q_gain, v_k_gain, v_sink, v_w_s, v_b_s, v_b_ada)
    w_pack, _ = _pack_rows(weights)
    m_pack, _ = _pack_rows(moments_m)
    v_pack, _ = _pack_rows(moments_v)
    shapes = [w.shape for w in weights]
    g_small, d_small, m_small, v_small = (
        _unpack_rows(p, offsets, shapes) for p in _small_update(gathered, w_pack, m_pack, v_pack)
    )

    ada_off = offsets[-1]
    ada_n = -(-n_layers * 3 * d // (SUBLANES * LANES)) * SUBLANES
    d_ada_all = gathered[:, ada_off : ada_off + ada_n].reshape(N_DEV, -1)[:, : n_layers * 3 * d].reshape(N_DEV, n_layers, 3 * d)
    d_ada_cols = lax.dynamic_slice_in_dim(d_ada_all, my * ada_cols, ada_cols, axis=2)
    g_w_ada = _ada_weight_grad(c_all[:, 0, :], d_ada_cols.transpose(1, 0, 2))

    def update(w, g, m, v, name):
        shape = w.shape
        flat = lambda a: a.reshape(-1, shape[-1])
        return tuple(a.reshape(shape) for a in _adamw(flat(w), flat(g), flat(m), flat(v), name))

    g_w_in, g_w_out = jnp.stack(g_w_in), jnp.stack(g_w_out)
    upd_ada = update(w_ada, g_w_ada, m_w_ada, v_w_ada, "adamw_w_ada")
    upd_in = update(w_in, g_w_in, m_w_in, v_w_in, "adamw_w_in")
    upd_out = update(w_out, g_w_out, m_w_out, v_w_out, "adamw_w_out")

    def ordered(ada_, in_, out_, small_):
        ng, qg, kg, sk, ws, bs, ba = small_
        return (ada_, ba, ng, in_, qg, kg, sk, ws, bs, out_)

    grads = ordered(g_w_ada, g_w_in, g_w_out, g_small)
    deltas = ordered(upd_ada[0], upd_in[0], upd_out[0], d_small)
    new_m = ordered(upd_ada[1], upd_in[1], upd_out[1], m_small)
    new_v = ordered(upd_ada[2], upd_in[2], upd_out[2], v_small)
    return (loss, dx.reshape(x.shape), *grads, *deltas, *new_m, *new_v)
```

```python
import functools

import jax
import jax.numpy as jnp
from jax import lax
from jax.experimental import pallas as pl
from jax.experimental.pallas import tpu as pltpu

F32 = jnp.float32
MXU_DTYPE = jnp.bfloat16
MESH_ID = pl.DeviceIdType.MESH

N_DEV = 8
HEAD_DIM = 64
N_HEADS = 8
Q_PER_KV = 4
D_ATTN = 512
D_KV = 128
D_GM = 512
N_GROUPS = 8
D_MIX = D_ATTN + D_GM
BLK = 128
LANES = 128
SUBLANES = 8
N_PAIRS = D_ATTN // LANES
D_QKV = D_ATTN + 2 * D_KV
D_REST = D_ATTN + 3 * D_GM
D_IN = D_QKV + D_REST
EPS = 1e-6
NEG_INF = -1e30
ALIBI_SLOPES = tuple(2.0 ** (-8.0 * (h + 1) / N_HEADS) for h in range(N_HEADS))
Q_SCALE = 1.0 / 8.0

ADAM_LR = 0.001
ADAM_B1 = 0.9
ADAM_B2 = 0.999
ADAM_EPS = 1e-08
ADAM_WD = 0.01
ADAM_STEP = 10

TOKEN_TILE = 512
VMEM_LIMIT_BYTES = 56 * 1024 * 1024


def _params(semantics=None):
    return pltpu.CompilerParams(dimension_semantics=semantics, vmem_limit_bytes=VMEM_LIMIT_BYTES)


def _dot(a, b):
    return jnp.dot(a, b, preferred_element_type=F32)


def _dot_nt(a, b):
    return lax.dot_general(a, b, (((1,), (1,)), ((), ())), preferred_element_type=F32)


def _dot_tn(a, b):
    return lax.dot_general(a, b, (((0,), (0,)), ((), ())), preferred_element_type=F32)


def _mx(v):
    return v.astype(MXU_DTYPE)


def _lane_lo(rows):
    return lax.broadcasted_iota(jnp.int32, (rows, LANES), 1) < HEAD_DIM


def _half_ones():
    r = lax.broadcasted_iota(jnp.int32, (LANES, LANES), 0) < HEAD_DIM
    c = lax.broadcasted_iota(jnp.int32, (LANES, LANES), 1) < HEAD_DIM
    return jnp.where(r == c, 1.0, 0.0).astype(jnp.bfloat16)


def _half_sum(v, ones):
    p1 = v.astype(jnp.bfloat16)
    r1 = v - p1.astype(F32)
    p2 = r1.astype(jnp.bfloat16)
    p3 = (r1 - p2.astype(F32)).astype(jnp.bfloat16)
    return _dot(p1, ones) + _dot(p2, ones) + _dot(p3, ones)


def _half_rms(v, ones):
    r = lax.rsqrt(_half_sum(v * v, ones) * (1.0 / HEAD_DIM) + EPS)
    return v * r, r


def _half_rms_bwd(dy, vhat, r, ones):
    return r * (dy - vhat * (_half_sum(vhat * dy, ones) * (1.0 / HEAD_DIM)))


def _group_rows(v):
    rows, n = v.shape
    return v.reshape(rows // SUBLANES, SUBLANES, n).sum(axis=0)


def _sigmoid(v):
    return 1.0 / (1.0 + jnp.exp(-v))


ROW_CHUNK = 32
VARIANT_HEADS = ((0, 2, 5, 7), (1, 3, 4, 6))
HEAD_SLOT = {h: (v, s) for v, heads in enumerate(VARIANT_HEADS) for s, h in enumerate(heads)}
STACK = Q_PER_KV * BLK


def _fill_attn_bias(bias_s):
    qi = lax.broadcasted_iota(jnp.int32, (BLK, 3 * BLK), 0)
    ci = lax.broadcasted_iota(jnp.int32, (BLK, 3 * BLK), 1)
    dist = jnp.abs(ci - BLK - qi)
    distf = dist.astype(F32)
    for h in range(N_HEADS):
        bias_s[h] = jnp.where(dist <= BLK, -(ALIBI_SLOPES[h] * distf), NEG_INF)


def _edge_mask(block, seq):
    kpos = (block - 1) * BLK + lax.broadcasted_iota(jnp.int32, (1, 3 * BLK), 1)
    return jnp.where((kpos >= 0) & (kpos < seq), 0.0, NEG_INF)


def _stage_queries(qn, lo_t, j, nb, qs):
    for a in range(2):
        v, slot = HEAD_SLOT[2 * j + a]
        qm = _mx(jnp.where(lo_t, qn, 0.0) if a == 0 else jnp.where(lo_t, 0.0, qn))
        for n in range(nb):
            qs[n, v, slot * BLK : (slot + 1) * BLK, :] = qm[n * BLK : (n + 1) * BLK]


def _unstack_pair(stacked, j, lo):
    (v0, s0), (v1, s1) = HEAD_SLOT[2 * j], HEAD_SLOT[2 * j + 1]
    return jnp.where(lo, stacked[v0][s0 * BLK : (s0 + 1) * BLK], stacked[v1][s1 * BLK : (s1 + 1) * BLK])


def _stage_keys(kvp_ref, qkv_ref, kvn_ref, kg, ones, tile, ks, kr, vs, vr, khat_s=None, rk_s=None):
    pieces = (
        (0, BLK, kvp_ref[:, 0:D_KV], kvp_ref[:, D_KV : 2 * D_KV]),
        (BLK, tile, qkv_ref[:, D_ATTN : D_ATTN + D_KV], qkv_ref[:, D_ATTN + D_KV : D_QKV]),
        (BLK + tile, BLK, kvn_ref[:, 0:D_KV], kvn_ref[:, D_KV : 2 * D_KV]),
    )
    for r0, n, k, v in pieces:
        khat, rk = _half_rms(k, ones)
        kn = khat * kg
        ks[r0 : r0 + n, :] = _mx(kn)
        kr[r0 : r0 + n, :] = _mx(pltpu.roll(kn, HEAD_DIM, 1))
        vs[r0 : r0 + n, :] = _mx(v)
        vr[r0 : r0 + n, :] = _mx(pltpu.roll(v, HEAD_DIM, 1))
        if khat_s is not None:
            khat_s[r0 : r0 + n, :] = khat
            rk_s[r0 : r0 + n, :] = rk


def _halo_specs(tile, seq):
    nb = tile // BLK
    last = seq // BLK - 1
    kv_col = D_ATTN // (2 * D_KV)
    prev = pl.BlockSpec((BLK, 2 * D_KV), lambda i: (jnp.maximum(i * nb - 1, 0), kv_col))
    nxt = pl.BlockSpec((BLK, 2 * D_KV), lambda i: (jnp.minimum((i + 1) * nb, last), kv_col))
    return prev, nxt


def _row_spec(tile, width):
    return pl.BlockSpec((tile, width), lambda i: (i, 0))


def _full_spec(shape):
    nd = len(shape)
    return pl.BlockSpec(shape, lambda i: (0,) * nd)


SMEM_SPEC = pl.BlockSpec(memory_space=pltpu.SMEM)
VMEM_SPEC = pl.BlockSpec(memory_space=pltpu.VMEM)


def _ln_proj_fwd(x, gain, scale1, shift, w_in, name):
    seq, d = x.shape
    tile = min(TOKEN_TILE, seq)

    def body(x_ref, g_ref, s1_ref, sh_ref, w_ref, pa_ref, pb_ref):
        xv = x_ref[...]
        r = lax.rsqrt(jnp.mean(xv * xv, axis=-1, keepdims=True) + EPS)
        h = _mx((xv * r) * g_ref[...] * s1_ref[...] + sh_ref[...])
        pa_ref[...] = _dot(h, w_ref[:, 0:D_QKV])
        pb_ref[...] = _dot(h, w_ref[:, D_QKV:D_IN])

    vec = _full_spec((1, d))
    return pl.pallas_call(
        body,
        name=name,
        grid=(seq // tile,),
        in_specs=[_row_spec(tile, d), vec, vec, vec, _full_spec((d, D_IN))],
        out_specs=[_row_spec(tile, D_QKV), _row_spec(tile, D_REST)],
        out_shape=[jax.ShapeDtypeStruct((seq, D_QKV), F32), jax.ShapeDtypeStruct((seq, D_REST), F32)],
        compiler_params=_params(("parallel",)),
    )(x, gain, scale1, shift, w_in)


def _attn_fwd(pa, q_gain2, k_gain2, sink, name):
    seq = pa.shape[0]
    tile = min(TOKEN_TILE, seq)
    nb = tile // BLK
    ext = tile + 2 * BLK

    def body(sink_ref, qkv_ref, kvp_ref, kvn_ref, qg_ref, kg_ref, o_ref, qs, ks, kr, vs, vr, bias_s, s_scr, p_scr, inv_scr):
        i = pl.program_id(0)

        @pl.when(i == 0)
        def _():
            _fill_attn_bias(bias_s)

        ones = _half_ones()
        lo = _lane_lo(BLK)
        lo_t = _lane_lo(tile)
        _stage_keys(kvp_ref, qkv_ref, kvn_ref, kg_ref[...], ones, tile, ks, kr, vs, vr)
        for j in range(N_PAIRS):
            qhat, _ = _half_rms(qkv_ref[:, j * LANES : (j + 1) * LANES], ones)
            _stage_queries(qhat * (qg_ref[...] * Q_SCALE), lo_t, j, nb, qs)

        def block(n, carry):
            r0 = pl.multiple_of(n * BLK, BLK)
            krows = pl.ds(r0, 3 * BLK)
            edge = _edge_mask(i * nb + n, seq)
            for v in range(2):
                s_scr[v] = _dot_nt(qs[n, v], (kr if v else ks)[krows, :])
            for h in range(N_HEADS):
                v, slot = HEAD_SLOT[h]
                sink_h = sink_ref[h]
                for rc in range(0, BLK, ROW_CHUNK):
                    rows = slice(slot * BLK + rc, slot * BLK + rc + ROW_CHUNK)
                    s = s_scr[v, rows, :] + bias_s[h, rc : rc + ROW_CHUNK, :] + edge
                    m = jnp.maximum(jnp.max(s, axis=-1, keepdims=True), sink_h)
                    p = jnp.exp(s - m)
                    total = jnp.sum(p, axis=-1, keepdims=True) + jnp.exp(sink_h - m)
                    p_scr[v, rows, :] = _mx(p)
                    inv_scr[v, rows, :] = jnp.broadcast_to(1.0 / total, (ROW_CHUNK, LANES))
            outs = [_dot(p_scr[v], (vr if v else vs)[krows, :]) * inv_scr[v] for v in range(2)]
            for j in range(N_PAIRS):
                o_ref[pl.ds(r0, BLK), j * LANES : (j + 1) * LANES] = _unstack_pair(outs, j, lo)
            return carry

        lax.fori_loop(0, nb, block, 0)

    prev, nxt = _halo_specs(tile, seq)
    vec = _full_spec((1, LANES))
    return pl.pallas_call(
        body,
        name=name,
        grid=(seq // tile,),
        in_specs=[SMEM_SPEC, _row_spec(tile, D_QKV), prev, nxt, vec, vec],
        out_specs=_row_spec(tile, D_ATTN),
        out_shape=jax.ShapeDtypeStruct((seq, D_ATTN), F32),
        scratch_shapes=[
            pltpu.VMEM((nb, 2, STACK, LANES), MXU_DTYPE),
            pltpu.VMEM((ext, LANES), MXU_DTYPE),
            pltpu.VMEM((ext, LANES), MXU_DTYPE),
            pltpu.VMEM((ext, LANES), MXU_DTYPE),
            pltpu.VMEM((ext, LANES), MXU_DTYPE),
            pltpu.VMEM((N_HEADS, BLK, 3 * BLK), F32),
            pltpu.VMEM((2, STACK, 3 * BLK), F32),
            pltpu.VMEM((2, STACK, 3 * BLK), MXU_DTYPE),
            pltpu.VMEM((2, STACK, LANES), F32),
        ],
        compiler_params=_params(("arbitrary",)),
    )(sink, pa, pa, pa, q_gain2, k_gain2)


def _mix_out_fwd(pb, o, x, gate, w_out, w_s, b_st, name):
    seq, d = x.shape
    tile = min(TOKEN_TILE, seq)
    nb = tile // BLK

    def body(pb_ref, o_ref, x_ref, gate_ref, wo_ref, ws_ref, bs_ref, xo_ref, y_s, vn_s):
        ones = _half_ones()
        lo = _lane_lo(BLK)
        ga = pb_ref[:, 0:D_ATTN]
        y_s[:, 0:D_ATTN] = _mx(o_ref[...] * (ga * _sigmoid(ga)))
        for j in range(N_PAIRS):
            cols = slice(2 * D_GM + j * LANES, 2 * D_GM + (j + 1) * LANES)
            vhat, _ = _half_rms(pb_ref[:, cols], ones)
            vn_s[:, j * LANES : (j + 1) * LANES] = _mx(vhat)

        def chunk(n, carry):
            rows = pl.ds(pl.multiple_of(n * BLK, BLK), BLK)
            for j in range(N_PAIRS):
                cols = slice(j * LANES, (j + 1) * LANES)
                vn = vn_s[rows, cols]
                sv = jnp.where(lo, _dot(ws_ref[2 * j], vn), _dot(ws_ref[2 * j + 1], vn)) + bs_ref[:, cols]
                u = pb_ref[rows, D_ATTN + j * LANES : D_ATTN + (j + 1) * LANES]
                gg = pb_ref[rows, D_ATTN + 2 * D_GM + j * LANES : D_ATTN + 2 * D_GM + (j + 1) * LANES]
                y_s[rows, D_ATTN + j * LANES : D_ATTN + (j + 1) * LANES] = _mx((u * sv) * (gg * _sigmoid(gg)))
            return carry

        lax.fori_loop(0, nb, chunk, 0)
        xo_ref[...] = x_ref[...] + gate_ref[...] * _dot(y_s[...], wo_ref[...])

    return pl.pallas_call(
        body,
        name=name,
        grid=(seq // tile,),
        in_specs=[
            _row_spec(tile, D_REST),
            _row_spec(tile, D_ATTN),
            _row_spec(tile, d),
            _full_spec((1, d)),
            _full_spec((D_MIX, d)),
            _full_spec((N_GROUPS, BLK, BLK)),
            _full_spec((BLK, D_GM)),
        ],
        out_specs=_row_spec(tile, d),
        out_shape=jax.ShapeDtypeStruct((seq, d), F32),
        scratch_shapes=[pltpu.VMEM((tile, D_MIX), MXU_DTYPE), pltpu.VMEM((tile, D_GM), MXU_DTYPE)],
        compiler_params=_params(("parallel",)),
    )(pb, o, x, gate, w_out, w_s, b_st)


def _loss_grad(y, target):
    seq, d = y.shape
    tile = min(TOKEN_TILE, seq)

    def body(y_ref, t_ref, dy_ref, acc_ref):
        @pl.when(pl.program_id(0) == 0)
        def _():
            acc_ref[...] = jnp.zeros_like(acc_ref)

        e = y_ref[...] - t_ref[...]
        dy_ref[...] = e * (1.0 / d)
        acc_ref[...] += jnp.sum(jnp.sum(e * e, axis=-1, keepdims=True), axis=0, keepdims=True)

    return pl.pallas_call(
        body,
        name="loss_grad",
        grid=(seq // tile,),
        in_specs=[_row_spec(tile, d), _row_spec(tile, d)],
        out_specs=[_row_spec(tile, d), _full_spec((SUBLANES, LANES))],
        out_shape=[jax.ShapeDtypeStruct((seq, d), F32), jax.ShapeDtypeStruct((SUBLANES, LANES), F32)],
        compiler_params=_params(("arbitrary",)),
    )(y, target)


def _mix_out_bwd(dxn, pb, o, gate, w_out_t, w_s, w_s_t, b_st, name):
    seq, d = dxn.shape
    tile = min(TOKEN_TILE, seq)
    nb = tile // BLK

    def body(dxn_ref, pb_ref, o_ref, gate_ref, wot_ref, ws_ref, wst_ref, bs_ref,
             dpb_ref, do_ref, g_ref, dws_ref, dbs_ref, y_s, dy_s, vn_s, rv_s):
        @pl.when(pl.program_id(0) == 0)
        def _():
            g_ref[...] = jnp.zeros_like(g_ref)
            dws_ref[...] = jnp.zeros_like(dws_ref)
            dbs_ref[...] = jnp.zeros_like(dbs_ref)

        ones = _half_ones()
        lo = _lane_lo(BLK)
        dxv = dxn_ref[...]
        dy_s[...] = _dot(_mx(dxv * gate_ref[...]), wot_ref[...])
        ga = pb_ref[:, 0:D_ATTN]
        sig = _sigmoid(ga)
        sil = ga * sig
        ov = o_ref[...]
        y_s[:, 0:D_ATTN] = _mx(ov * sil)
        da = dy_s[:, 0:D_ATTN]
        do_ref[...] = da * sil
        dpb_ref[:, 0:D_ATTN] = (da * ov * (sig * (1.0 + ga * (1.0 - sig)))).astype(dpb_ref.dtype)
        for j in range(N_PAIRS):
            cols = slice(j * LANES, (j + 1) * LANES)
            vhat, rv = _half_rms(pb_ref[:, 2 * D_GM + j * LANES : 2 * D_GM + (j + 1) * LANES], ones)
            vn_s[:, cols] = vhat
            rv_s[:, cols] = rv

        def chunk(n, carry):
            rows = pl.ds(pl.multiple_of(n * BLK, BLK), BLK)
            for j in range(N_PAIRS):
                cols = slice(j * LANES, (j + 1) * LANES)
                c_u = slice(D_ATTN + j * LANES, D_ATTN + (j + 1) * LANES)
                c_vg = slice(D_ATTN + D_GM + j * LANES, D_ATTN + D_GM + (j + 1) * LANES)
                c_gg = slice(D_ATTN + 2 * D_GM + j * LANES, D_ATTN + 2 * D_GM + (j + 1) * LANES)
                vhat = vn_s[rows, cols]
                vn = _mx(vhat)
                sv = jnp.where(lo, _dot(ws_ref[2 * j], vn), _dot(ws_ref[2 * j + 1], vn)) + bs_ref[:, cols]
                u = pb_ref[rows, c_u]
                gg = pb_ref[rows, c_gg]
                sg = _sigmoid(gg)
                silg = gg * sg
                m0 = u * sv
                y_s[rows, c_u] = _mx(m0 * silg)
                dm = dy_s[rows, c_u]
                dm0 = dm * silg
                dpb_ref[rows, c_gg] = (dm * m0 * (sg * (1.0 + gg * (1.0 - sg)))).astype(dpb_ref.dtype)
                dpb_ref[rows, c_u] = (dm0 * sv).astype(dpb_ref.dtype)
                dsv = dm0 * u
                dbs_ref[:, cols] += dsv
                dws_ref[2 * j] += _dot_nt(_mx(jnp.where(lo, dsv, 0.0)), vn)
                dws_ref[2 * j + 1] += _dot_nt(_mx(jnp.where(lo, 0.0, dsv)), vn)
                dsv_m = _mx(dsv)
                dvn = jnp.where(lo, _dot(wst_ref[2 * j], dsv_m), _dot(wst_ref[2 * j + 1], dsv_m))
                dpb_ref[rows, c_vg] = _half_rms_bwd(dvn, vhat, rv_s[rows, cols], ones).astype(dpb_ref.dtype)
            return carry

        lax.fori_loop(0, nb, chunk, 0)
        g_ref[...] += _dot_tn(y_s[...], _mx(dxv))

    return pl.pallas_call(
        body,
        name=name,
        grid=(seq // tile,),
        in_specs=[
            _row_spec(tile, d),
            _row_spec(tile, D_REST),
            _row_spec(tile, D_ATTN),
            _full_spec((1, d)),
            _full_spec((d, D_MIX)),
            _full_spec((N_GROUPS, BLK, BLK)),
            _full_spec((N_GROUPS, BLK, BLK)),
            _full_spec((BLK, D_GM)),
        ],
        out_specs=[
            _row_spec(tile, D_REST),
            _row_spec(tile, D_ATTN),
            _full_spec((D_MIX, d)),
            _full_spec((N_GROUPS, BLK, BLK)),
            _full_spec((BLK, D_GM)),
        ],
        out_shape=[
            jax.ShapeDtypeStruct((seq, D_REST), MXU_DTYPE),
            jax.ShapeDtypeStruct((seq, D_ATTN), F32),
            jax.ShapeDtypeStruct((D_MIX, d), F32),
            jax.ShapeDtypeStruct((N_GROUPS, BLK, BLK), F32),
            jax.ShapeDtypeStruct((BLK, D_GM), F32),
        ],
        scratch_shapes=[
            pltpu.VMEM((tile, D_MIX), MXU_DTYPE),
            pltpu.VMEM((tile, D_MIX), F32),
            pltpu.VMEM((tile, D_GM), F32),
            pltpu.VMEM((tile, D_GM), F32),
        ],
        compiler_params=_params(("arbitrary",)),
    )(dxn, pb, o, gate, w_out_t, w_s, w_s_t, b_st)


def _attn_bwd(pa, o, do, q_gain2, k_gain2, sink, name):
    seq = pa.shape[0]
    tile = min(TOKEN_TILE, seq)
    nb = tile // BLK
    nt = seq // tile
    ext = tile + 2 * BLK

    def body(sink_ref, qkv_ref, kvp_ref, kvn_ref, o_ref, do_ref, qg_ref, kg_ref,
             dq_ref, dkv_ref, hp_ref, hn_ref, dqg_ref, dkg_ref, dsk_ref,
             qs, dos, qhat_s, rq_s, ks, kr, vs, vr, khat_s, rk_s, dqn_s, dka, dva, bias_s, s_scr, dp_scr, p_scr, ds_scr):
        i = pl.program_id(0)

        @pl.when(i == 0)
        def _():
            dqg_ref[...] = jnp.zeros_like(dqg_ref)
            dkg_ref[...] = jnp.zeros_like(dkg_ref)
            dsk_ref[...] = jnp.zeros_like(dsk_ref)
            _fill_attn_bias(bias_s)

        ones = _half_ones()
        lo = _lane_lo(BLK)
        lo_t = _lane_lo(tile)
        lo_c = _lane_lo(ROW_CHUNK)
        qg = qg_ref[...] * Q_SCALE
        kg = kg_ref[...]
        _stage_keys(kvp_ref, qkv_ref, kvn_ref, kg, ones, tile, ks, kr, vs, vr, khat_s, rk_s)
        for j in range(N_PAIRS):
            cols = slice(j * LANES, (j + 1) * LANES)
            qhat, rq = _half_rms(qkv_ref[:, cols], ones)
            qhat_s[:, cols] = qhat
            rq_s[:, cols] = rq
            _stage_queries(qhat * qg, lo_t, j, nb, qs)
            _stage_queries(do_ref[:, cols], lo_t, j, nb, dos)
        dka[...] = jnp.zeros_like(dka)
        dva[...] = jnp.zeros_like(dva)
        head_lane = lax.broadcasted_iota(jnp.int32, (1, LANES), 1)

        def block(n, dsink):
            r0 = pl.multiple_of(n * BLK, BLK)
            krows = pl.ds(r0, 3 * BLK)
            edge = _edge_mask(i * nb + n, seq)
            for v in range(2):
                s_scr[v] = _dot_nt(qs[n, v], (kr if v else ks)[krows, :])
                dp_scr[v] = _dot_nt(dos[n, v], (vr if v else vs)[krows, :])
            for h in range(N_HEADS):
                v, slot = HEAD_SLOT[h]
                j, a = divmod(h, 2)
                cols = slice(j * LANES, (j + 1) * LANES)
                sink_h = sink_ref[h]
                sink_part = jnp.zeros((ROW_CHUNK, 1), F32)
                for rc in range(0, BLK, ROW_CHUNK):
                    rows = slice(slot * BLK + rc, slot * BLK + rc + ROW_CHUNK)
                    trows = pl.ds(pl.multiple_of(r0 + rc, ROW_CHUNK), ROW_CHUNK)
                    s = s_scr[v, rows, :] + bias_s[h, rc : rc + ROW_CHUNK, :] + edge
                    m = jnp.maximum(jnp.max(s, axis=-1, keepdims=True), sink_h)
                    p = jnp.exp(s - m)
                    e_sink = jnp.exp(sink_h - m)
                    inv = 1.0 / (jnp.sum(p, axis=-1, keepdims=True) + e_sink)
                    pn = p * inv
                    prod = do_ref[trows, cols] * o_ref[trows, cols]
                    prod = jnp.where(lo_c, prod, 0.0) if a == 0 else jnp.where(lo_c, 0.0, prod)
                    dcol = jnp.sum(prod, axis=-1, keepdims=True)
                    ds_scr[v, rows, :] = _mx(pn * (dp_scr[v, rows, :] - dcol))
                    p_scr[v, rows, :] = _mx(pn)
                    sink_part = sink_part + (e_sink * inv) * dcol
                dsink = dsink - jnp.where(head_lane == h, jnp.sum(sink_part, axis=0, keepdims=True), 0.0)
            dqv = []
            for v in range(2):
                dqv.append(_dot(ds_scr[v], (kr if v else ks)[krows, :]))
                dka[v, krows, :] += _dot_tn(ds_scr[v], qs[n, v])
                dva[v, krows, :] += _dot_tn(p_scr[v], dos[n, v])
            for j in range(N_PAIRS):
                dqn_s[pl.ds(r0, BLK), j * LANES : (j + 1) * LANES] = _unstack_pair(dqv, j, lo)
            return dsink

        dsink = lax.fori_loop(0, nb, block, jnp.zeros((1, LANES), F32))
        dsk_ref[...] += jnp.broadcast_to(dsink, (SUBLANES, LANES))
        for j in range(N_PAIRS):
            cols = slice(j * LANES, (j + 1) * LANES)
            dqn = dqn_s[:, cols]
            qhat = qhat_s[:, cols]
            dqg_ref[:, cols] += _group_rows(dqn * qhat) * Q_SCALE
            dq_ref[:, cols] = _half_rms_bwd(dqn * qg, qhat, rq_s[:, cols], ones).astype(dq_ref.dtype)
        dkn = dka[0] + pltpu.roll(dka[1], HEAD_DIM, 1)
        khat = khat_s[...]
        dkg_ref[...] += _group_rows(dkn * khat)
        dk = _half_rms_bwd(dkn * kg, khat, rk_s[...], ones)
        dv = dva[0] + pltpu.roll(dva[1], HEAD_DIM, 1)
        hp_ref[:, 0:D_KV] = dk[0:BLK]
        hp_ref[:, D_KV : 2 * D_KV] = dv[0:BLK]
        dkv_ref[:, 0:D_KV] = dk[BLK : BLK + tile]
        dkv_ref[:, D_KV : 2 * D_KV] = dv[BLK : BLK + tile]
        hn_ref[:, 0:D_KV] = dk[BLK + tile : ext]
        hn_ref[:, D_KV : 2 * D_KV] = dv[BLK + tile : ext]

    prev, nxt = _halo_specs(tile, seq)
    vec = _full_spec((1, LANES))
    halo = pl.BlockSpec((None, BLK, 2 * D_KV), lambda i: (i, 0, 0))
    return pl.pallas_call(
        body,
        name=name,
        grid=(nt,),
        in_specs=[SMEM_SPEC, _row_spec(tile, D_QKV), prev, nxt, _row_spec(tile, D_ATTN), _row_spec(tile, D_ATTN), vec, vec],
        out_specs=[
            _row_spec(tile, D_ATTN),
            _row_spec(tile, 2 * D_KV),
            halo,
            halo,
            _full_spec((SUBLANES, D_ATTN)),
            _full_spec((SUBLANES, LANES)),
            _full_spec((SUBLANES, LANES)),
        ],
        out_shape=[
            jax.ShapeDtypeStruct((seq, D_ATTN), MXU_DTYPE),
            jax.ShapeDtypeStruct((seq, 2 * D_KV), F32),
            jax.ShapeDtypeStruct((nt, BLK, 2 * D_KV), F32),
            jax.ShapeDtypeStruct((nt, BLK, 2 * D_KV), F32),
            jax.ShapeDtypeStruct((SUBLANES, D_ATTN), F32),
            jax.ShapeDtypeStruct((SUBLANES, LANES), F32),
            jax.ShapeDtypeStruct((SUBLANES, LANES), F32),
        ],
        scratch_shapes=[
            pltpu.VMEM((nb, 2, STACK, LANES), MXU_DTYPE),
            pltpu.VMEM((nb, 2, STACK, LANES), MXU_DTYPE),
            pltpu.VMEM((tile, D_ATTN), F32),
            pltpu.VMEM((tile, D_ATTN), F32),
            pltpu.VMEM((ext, LANES), MXU_DTYPE),
            pltpu.VMEM((ext, LANES), MXU_DTYPE),
            pltpu.VMEM((ext, LANES), MXU_DTYPE),
            pltpu.VMEM((ext, LANES), MXU_DTYPE),
            pltpu.VMEM((ext, LANES), F32),
            pltpu.VMEM((ext, LANES), F32),
            pltpu.VMEM((tile, D_ATTN), F32),
            pltpu.VMEM((2, ext, LANES), F32),
            pltpu.VMEM((2, ext, LANES), F32),
            pltpu.VMEM((N_HEADS, BLK, 3 * BLK), F32),
            pltpu.VMEM((2, STACK, 3 * BLK), F32),
            pltpu.VMEM((2, STACK, 3 * BLK), F32),
            pltpu.VMEM((2, STACK, 3 * BLK), MXU_DTYPE),
            pltpu.VMEM((2, STACK, 3 * BLK), MXU_DTYPE),
        ],
        compiler_params=_params(("arbitrary",)),
    )(sink, pa, pa, pa, o, do, q_gain2, k_gain2)


def _halo_in_specs(tile, nt):
    from_prev = pl.BlockSpec((None, BLK, 2 * D_KV), lambda i: (jnp.maximum(i - 1, 0), 0, 0))
    from_next = pl.BlockSpec((None, BLK, 2 * D_KV), lambda i: (jnp.minimum(i + 1, nt - 1), 0, 0))
    return from_prev, from_next


def _proj_bwd_dx(x, dxn, dq, dkv, halo_prev, halo_next, dpb, w_in_t, gain, scale1, name):
    seq, d = x.shape
    tile = min(TOKEN_TILE, seq)
    nt = seq // tile

    def body(x_ref, dxn_ref, dq_ref, dkv_ref, hn_ref, hp_ref, dpb_ref, wt_ref, g_ref, s1_ref,
             dx_ref, dkvb_ref, c0_ref, c1_ref):
        i = pl.program_id(0)

        @pl.when(i == 0)
        def _():
            c0_ref[...] = jnp.zeros_like(c0_ref)
            c1_ref[...] = jnp.zeros_like(c1_ref)

        top = dkv_ref[0:BLK, :] + jnp.where(i > 0, hn_ref[...], 0.0)
        bot = dkv_ref[tile - BLK : tile, :] + jnp.where(i < nt - 1, hp_ref[...], 0.0)
        if tile == BLK:
            dkvb_ref[...] = (top + bot - dkv_ref[...]).astype(dkvb_ref.dtype)
        else:
            dkvb_ref[0:BLK, :] = top.astype(dkvb_ref.dtype)
            dkvb_ref[tile - BLK : tile, :] = bot.astype(dkvb_ref.dtype)
            if tile > 2 * BLK:
                dkvb_ref[BLK : tile - BLK, :] = dkv_ref[BLK : tile - BLK, :].astype(dkvb_ref.dtype)
        dh = (
            _dot(dq_ref[...], wt_ref[0:D_ATTN, :])
            + _dot(dkvb_ref[...], wt_ref[D_ATTN:D_QKV, :])
            + _dot(dpb_ref[...], wt_ref[D_QKV:D_IN, :])
        )
        xv = x_ref[...]
        r = lax.rsqrt(jnp.mean(xv * xv, axis=-1, keepdims=True) + EPS)
        xn = xv * r
        c0_ref[...] += _group_rows(dh)
        c1_ref[...] += _group_rows(dh * xn)
        dxn_ = dh * (g_ref[...] * s1_ref[...])
        dx_ref[...] = dxn_ref[...] + r * (dxn_ - xn * jnp.mean(xn * dxn_, axis=-1, keepdims=True))

    from_prev, from_next = _halo_in_specs(tile, nt)
    vec = _full_spec((1, d))
    return pl.pallas_call(
        body,
        name=name,
        grid=(nt,),
        in_specs=[
            _row_spec(tile, d),
            _row_spec(tile, d),
            _row_spec(tile, D_ATTN),
            _row_spec(tile, 2 * D_KV),
            from_prev,
            from_next,
            _row_spec(tile, D_REST),
            _full_spec((D_IN, d)),
            vec,
            vec,
        ],
        out_specs=[_row_spec(tile, d), _row_spec(tile, 2 * D_KV), _full_spec((SUBLANES, d)), _full_spec((SUBLANES, d))],
        out_shape=[
            jax.ShapeDtypeStruct((seq, d), F32),
            jax.ShapeDtypeStruct((seq, 2 * D_KV), MXU_DTYPE),
            jax.ShapeDtypeStruct((SUBLANES, d), F32),
            jax.ShapeDtypeStruct((SUBLANES, d), F32),
        ],
        compiler_params=_params(("arbitrary",)),
    )(x, dxn, dq, dkv, halo_next, halo_prev, dpb, w_in_t, gain, scale1)


def _proj_bwd_dw(x, gain, scale1, shift, dq, dkvb, dpb, name):
    seq, d = x.shape
    tile = min(TOKEN_TILE, seq)

    def body(x_ref, g_ref, s1_ref, sh_ref, dq_ref, dkv_ref, dpb_ref, dw_ref):
        @pl.when(pl.program_id(0) == 0)
        def _():
            dw_ref[...] = jnp.zeros_like(dw_ref)

        xv = x_ref[...]
        r = lax.rsqrt(jnp.mean(xv * xv, axis=-1, keepdims=True) + EPS)
        h = _mx((xv * r) * g_ref[...] * s1_ref[...] + sh_ref[...])
        dw_ref[:, 0:D_ATTN] += _dot_tn(h, dq_ref[...])
        dw_ref[:, D_ATTN:D_QKV] += _dot_tn(h, dkv_ref[...])
        dw_ref[:, D_QKV:D_IN] += _dot_tn(h, dpb_ref[...])

    vec = _full_spec((1, d))
    return pl.pallas_call(
        body,
        name=name,
        grid=(seq // tile,),
        in_specs=[_row_spec(tile, d), vec, vec, vec, _row_spec(tile, D_ATTN), _row_spec(tile, 2 * D_KV), _row_spec(tile, D_REST)],
        out_specs=_full_spec((d, D_IN)),
        out_shape=jax.ShapeDtypeStruct((d, D_IN), F32),
        compiler_params=_params(("arbitrary",)),
    )(x, gain, scale1, shift, dq, dkvb, dpb)


def _w_out_finish(g, w_out, gate, name):
    d_mix, d = g.shape

    def body(g_ref, w_ref, gate_ref, dw_ref, dgate_ref):
        gv = g_ref[...]
        dw_ref[...] = gv * gate_ref[...]
        dgate_ref[...] = _group_rows(gv * w_ref[...].astype(F32))

    return pl.pallas_call(
        body,
        name=name,
        in_specs=[VMEM_SPEC, VMEM_SPEC, VMEM_SPEC],
        out_specs=[VMEM_SPEC, VMEM_SPEC],
        out_shape=[jax.ShapeDtypeStruct((d_mix, d), F32), jax.ShapeDtypeStruct((SUBLANES, d), F32)],
        compiler_params=_params(),
    )(g, w_out, gate)


def _adamw_math(w, g, m, v):
    m = ADAM_B1 * m + (1.0 - ADAM_B1) * g
    v = ADAM_B2 * v + (1.0 - ADAM_B2) * (g * g)
    m_hat = m / (1.0 - ADAM_B1**ADAM_STEP)
    v_hat = v / (1.0 - ADAM_B2**ADAM_STEP)
    delta = -ADAM_LR * (m_hat / (jnp.sqrt(v_hat) + ADAM_EPS) + ADAM_WD * w)
    return delta, m, v


def _adamw(w, g, m, v, name):
    rows, cols = w.shape
    tile = min(TOKEN_TILE, rows)

    def body(w_ref, g_ref, m_ref, v_ref, d_ref, mo_ref, vo_ref):
        d_ref[...], mo_ref[...], vo_ref[...] = _adamw_math(w_ref[...], g_ref[...], m_ref[...], v_ref[...])

    spec = _row_spec(tile, cols)
    shape = jax.ShapeDtypeStruct((rows, cols), F32)
    return pl.pallas_call(
        body,
        name=name,
        grid=(rows // tile,),
        in_specs=[spec] * 4,
        out_specs=[spec] * 3,
        out_shape=[shape] * 3,
        compiler_params=_params(("parallel",)),
    )(w, g, m, v)


def _small_update(gathered, w, m, v):
    rows = w.shape[0]

    def body(ga_ref, w_ref, m_ref, v_ref, g_ref, d_ref, mo_ref, vo_ref):
        g = ga_ref[0]
        for j in range(1, N_DEV):
            g = g + ga_ref[j]
        g_ref[...] = g
        d_ref[...], mo_ref[...], vo_ref[...] = _adamw_math(w_ref[...], g, m_ref[...], v_ref[...])

    shape = jax.ShapeDtypeStruct((rows, LANES), F32)
    return pl.pallas_call(
        body,
        name="small_update",
        in_specs=[VMEM_SPEC] * 4,
        out_specs=[VMEM_SPEC] * 4,
        out_shape=[shape] * 4,
        compiler_params=_params(),
    )(gathered, w, m, v)


def _ada_weight_grad(c_all, d_ada_cols):
    d = c_all.shape[-1]
    n_layers, _, width = d_ada_cols.shape

    def body(c_ref, da_ref, dw_ref):
        cv = c_ref[...]
        cond = cv * _sigmoid(cv)
        for l in range(n_layers):
            dw_ref[l] = lax.dot_general(
                cond, da_ref[l], (((0,), (0,)), ((), ())), preferred_element_type=F32, precision=lax.Precision.HIGHEST
            )

    return pl.pallas_call(
        body,
        name="ada_weight_grad",
        in_specs=[VMEM_SPEC, VMEM_SPEC],
        out_specs=VMEM_SPEC,
        out_shape=jax.ShapeDtypeStruct((n_layers, d, width), F32),
        compiler_params=_params(),
    )(c_all, d_ada_cols)


def _position():
    return lax.axis_index("x"), lax.axis_index("y"), lax.axis_index("c")


def _flip(pos, k):
    x, y, c = pos
    return (1 - x if k & 4 else x, 1 - y if k & 2 else y, 1 - c if k & 1 else c)


def _index(pos):
    x, y, c = pos
    return 4 * x + 2 * y + c


def _remote(src, dst, send_sem, recv_sem, to):
    return pltpu.make_async_remote_copy(
        src_ref=src, dst_ref=dst, send_sem=send_sem, recv_sem=recv_sem, device_id=to, device_id_type=MESH_ID
    )


def _two_level_all_gather(refs, send_sems, recv_sems):
    me = _position()
    sibling = _flip(me, 1)
    others = (4, 2, 6)

    def copy(t, k, block, to):
        slot = refs[t].at[_index(block)]
        return _remote(slot, slot, send_sems.at[7 * t + k], recv_sems.at[7 * t + k], to)

    started = []
    for t in range(len(refs)):
        started.append(copy(t, 0, me, sibling))
        started += [copy(t, 1 + j, me, _flip(me, f)) for j, f in enumerate(others)]
    for cp in started:
        cp.start()
    for j, f in enumerate(others):
        for t in range(len(refs)):
            copy(t, 1 + j, _flip(me, f), me).wait_recv()
            passed = copy(t, 4 + j, _flip(me, f), sibling)
            passed.start()
            started.append(passed)
    for t in range(len(refs)):
        copy(t, 0, sibling, me).wait_recv()
        for j, f in enumerate(others):
            copy(t, 4 + j, _flip(sibling, f), me).wait_recv()
    for cp in started:
        cp.wait_send()


def _ada_rows(c_row, w_ada):
    d = c_row.shape[-1]
    n_layers, _, width = w_ada.shape

    def body(c_ref, w_ref, call_ref, parts_ref, sbuf, sem_s1, sem_r1, sem_s2, sem_r2):
        me = _position()
        my = _index(me)
        call_ref[my] = jnp.broadcast_to(c_ref[...], (SUBLANES, d))
        mine = call_ref.at[my]
        first = [_remote(mine, mine, sem_s1.at[k - 1], sem_r1.at[k - 1], _flip(me, k)) for k in range(1, N_DEV)]
        for cp in first:
            cp.start()
        for k in range(1, N_DEV):
            theirs = call_ref.at[_index(_flip(me, k))]
            _remote(theirs, theirs, sem_s1.at[k - 1], sem_r1.at[k - 1], _flip(me, k)).wait_recv()
        for b in range(N_DEV):
            cv = call_ref[b]
            cond = cv * _sigmoid(cv)
            for l in range(n_layers):
                sbuf[b, l] = jnp.dot(cond, w_ref[l], preferred_element_type=F32, precision=lax.Precision.HIGHEST)
        parts_ref[my] = sbuf[my]
        second = []
        for k in range(1, N_DEV):
            to = _flip(me, k)
            second.append(_remote(sbuf.at[_index(to)], parts_ref.at[my], sem_s2.at[k - 1], sem_r2.at[k - 1], to))
        for cp in second:
            cp.start()
        for k in range(1, N_DEV):
            theirs = parts_ref.at[_index(_flip(me, k))]
            _remote(theirs, theirs, sem_s2.at[k - 1], sem_r2.at[k - 1], _flip(me, k)).wait_recv()
        for cp in first + second:
            cp.wait_send()

    return pl.pallas_call(
        body,
        name="ada_rows",
        in_specs=[VMEM_SPEC, VMEM_SPEC],
        out_specs=[VMEM_SPEC, VMEM_SPEC],
        out_shape=[
            jax.ShapeDtypeStruct((N_DEV, SUBLANES, d), F32),
            jax.ShapeDtypeStruct((N_DEV, n_layers, SUBLANES, width), F32),
        ],
        scratch_shapes=[
            pltpu.VMEM((N_DEV, n_layers, SUBLANES, width), F32),
            pltpu.SemaphoreType.DMA((N_DEV - 1,)),
            pltpu.SemaphoreType.DMA((N_DEV - 1,)),
            pltpu.SemaphoreType.DMA((N_DEV - 1,)),
            pltpu.SemaphoreType.DMA((N_DEV - 1,)),
        ],
        compiler_params=_params(),
    )(c_row, w_ada)


def _gather_weights(w_in_dense, w_out):
    def body(wi_ref, wo_ref, gi_ref, go_ref, send_sems, recv_sems):
        my = _index(_position())
        gi_ref[my] = wi_ref[...].astype(gi_ref.dtype)
        go_ref[my] = wo_ref[...].astype(go_ref.dtype)
        _two_level_all_gather((gi_ref, go_ref), send_sems, recv_sems)

    return pl.pallas_call(
        body,
        name="gather_weights",
        in_specs=[VMEM_SPEC, VMEM_SPEC],
        out_specs=[VMEM_SPEC, VMEM_SPEC],
        out_shape=[
            jax.ShapeDtypeStruct((N_DEV,) + w_in_dense.shape, MXU_DTYPE),
            jax.ShapeDtypeStruct((N_DEV,) + w_out.shape, MXU_DTYPE),
        ],
        scratch_shapes=[pltpu.SemaphoreType.DMA((14,)), pltpu.SemaphoreType.DMA((14,))],
        compiler_params=_params(),
    )(w_in_dense, w_out)


def _gather_small(packed):
    def body(p_ref, g_ref, send_sems, recv_sems):
        g_ref[_index(_position())] = p_ref[...]
        _two_level_all_gather((g_ref,), send_sems, recv_sems)

    return pl.pallas_call(
        body,
        name="gather_small",
        in_specs=[VMEM_SPEC],
        out_specs=VMEM_SPEC,
        out_shape=jax.ShapeDtypeStruct((N_DEV,) + packed.shape, F32),
        scratch_shapes=[pltpu.SemaphoreType.DMA((7,)), pltpu.SemaphoreType.DMA((7,))],
        compiler_params=_params(),
    )(packed)


def _reduce_scatter(blocks_a, blocks_b, name):
    def body(a_ref, b_ref, oa_ref, ob_ref, half_a, half_b, send_a, send_b, chips_a, chips_b, send_sems, recv_sems):
        me = _position()
        x, y, c = me
        sibling = _flip(me, 1)
        my_chip = 2 * x + y
        others = (4, 2, 6)
        pairs = ((a_ref, half_a, send_a, chips_a, oa_ref), (b_ref, half_b, send_b, chips_b, ob_ref))
        to_sibling = []
        for t, (src, half, _, _, _) in enumerate(pairs):
            cp = _remote(src.at[pl.ds(4 * (1 - c), 4)], half, send_sems.at[4 * t], recv_sems.at[4 * t], sibling)
            cp.start()
            to_sibling.append(cp)
        to_chips = []
        for t, (src, half, send, chips, _) in enumerate(pairs):
            to_sibling[t].wait_recv()
            chips[my_chip] = src[4 * c + my_chip] + half[my_chip]
            for j, f in enumerate(others):
                px, py, _ = _flip(me, f)
                chip = 2 * px + py
                send[j] = src[4 * c + chip] + half[chip]
                cp = _remote(send.at[j], chips.at[my_chip], send_sems.at[4 * t + 1 + j], recv_sems.at[4 * t + 1 + j], _flip(me, f))
                cp.start()
                to_chips.append(cp)
        for t, (_, _, send, chips, out) in enumerate(pairs):
            for j, f in enumerate(others):
                px, py, _ = _flip(me, f)
                slot = chips.at[2 * px + py]
                _remote(slot, slot, send_sems.at[4 * t + 1 + j], recv_sems.at[4 * t + 1 + j], _flip(me, f)).wait_recv()
            out[...] = ((chips[0] + chips[1]) + chips[2]) + chips[3]
        for cp in to_sibling + to_chips:
            cp.wait_send()

    def scratch(blocks):
        blk = blocks.shape[1:]
        return [pltpu.VMEM((4,) + blk, F32), pltpu.VMEM((3,) + blk, F32), pltpu.VMEM((4,) + blk, F32)]

    sa, sb = scratch(blocks_a), scratch(blocks_b)
    return pl.pallas_call(
        body,
        name=name,
        in_specs=[VMEM_SPEC, VMEM_SPEC],
        out_specs=[VMEM_SPEC, VMEM_SPEC],
        out_shape=[jax.ShapeDtypeStruct(blocks_a.shape[1:], F32), jax.ShapeDtypeStruct(blocks_b.shape[1:], F32)],
        scratch_shapes=[sa[0], sb[0], sa[1], sb[1], sa[2], sb[2], pltpu.SemaphoreType.DMA((8,)), pltpu.SemaphoreType.DMA((8,))],
        compiler_params=_params(),
    )(blocks_a, blocks_b)


def _by_core_chip(blocks):
    rest = blocks.shape[1:]
    return blocks.reshape((2, 2, 2) + rest).transpose((2, 0, 1) + tuple(range(3, 3 + len(rest)))).reshape((N_DEV,) + rest)


def _pack_rows(parts):
    rows, offsets, at = [], [], 0
    for p in parts:
        flat = p.reshape(-1)
        n = -(-flat.shape[0] // (SUBLANES * LANES)) * SUBLANES
        rows.append(jnp.pad(flat, (0, n * LANES - flat.shape[0])).reshape(n, LANES))
        offsets.append(at)
        at += n
    return jnp.concatenate(rows, axis=0), offsets


def _unpack_rows(packed, offsets, shapes):
    out = []
    for off, shape in zip(offsets, shapes):
        size = 1
        for s in shape:
            size *= s
        n = -(-size // (SUBLANES * LANES)) * SUBLANES
        out.append(packed[off : off + n].reshape(-1)[:size].reshape(shape))
    return out


def kernel(x, c, w_ada, b_ada, norm_gain, w_in, q_gain, k_gain, sink, w_s, b_s, w_out, loss_target, m_w_ada, m_b_ada, m_norm_gain, m_w_in, m_q_gain, m_k_gain, m_sink, m_w_s, m_b_s, m_w_out, v_w_ada, v_b_ada, v_norm_gain, v_w_in, v_q_gain, v_k_gain, v_sink, v_w_s, v_b_s, v_w_out):
    seq, d = x.shape[1], x.shape[2]
    n_layers = w_in.shape[0]
    w_cols = w_in.shape[2]
    ada_cols = w_ada.shape[2]
    my = _index(_position())
    xs = x.reshape(seq, d)
    target = loss_target.reshape(seq, d)

    c_all, ada_parts = _ada_rows(c, w_ada)
    ada = ada_parts[:, :, 0, :].transpose(1, 0, 2).reshape(n_layers, 3 * d) + b_ada
    shift, scale1, gate = ada[:, None, 0:d], 1.0 + ada[:, None, d : 2 * d], ada[:, None, 2 * d : 3 * d]
    gain = norm_gain[:, None, :]

    dense_rows = d * w_cols // LANES
    gi, go = _gather_weights(w_in.reshape(n_layers, dense_rows, LANES), w_out)
    w_in_full = gi.reshape(N_DEV, n_layers, d, w_cols).transpose(1, 2, 0, 3).reshape(n_layers, d, D_IN)
    w_in_t = w_in_full.transpose(0, 2, 1)
    w_out_full = go.transpose(1, 0, 2, 3).reshape(n_layers, D_MIX, d)
    w_out_t = w_out_full.transpose(0, 2, 1)
    w_s_m = w_s.astype(MXU_DTYPE)
    w_s_t = w_s_m.transpose(0, 1, 3, 2)
    b_st = jnp.repeat(b_s.transpose(0, 2, 1), HEAD_DIM, axis=2)
    q_gain2 = jnp.tile(q_gain, (1, 2))[:, None, :]
    k_gain2 = jnp.tile(k_gain, (1, 2))[:, None, :]

    xl, saved = xs, []
    for l in range(n_layers):
        pa, pb = _ln_proj_fwd(xl, gain[l], scale1[l], shift[l], w_in_full[l], f"ln_proj_fwd_{l}")
        o = _attn_fwd(pa, q_gain2[l], k_gain2[l], sink[l], f"attn_fwd_{l}")
        x_next = _mix_out_fwd(pb, o, xl, gate[l], w_out_full[l], w_s_m[l], b_st[l], f"mix_out_fwd_{l}")
        saved.append((xl, pa, pb, o))
        xl = x_next
    dx, sq_err = _loss_grad(xl, target)
    loss = lax.psum(sq_err[0, 0] * (0.5 / d), ("x", "y", "c"))

    g_w_in, g_w_out, small, d_ada_rows = [None] * n_layers, [None] * n_layers, [None] * n_layers, [None] * n_layers
    for l in reversed(range(n_layers)):
        x_l, pa, pb, o = saved[l]
        dpb, do, g_acc, d_ws, d_bs = _mix_out_bwd(dx, pb, o, gate[l], w_out_t[l], w_s_m[l], w_s_t[l], b_st[l], f"mix_out_bwd_{l}")
        dq, dkv, halo_prev, halo_next, d_qg, d_kg, d_sk = _attn_bwd(pa, o, do, q_gain2[l], k_gain2[l], sink[l], f"attn_bwd_{l}")
        dx, dkvb, c0, c1 = _proj_bwd_dx(x_l, dx, dq, dkv, halo_prev, halo_next, dpb, w_in_t[l], gain[l], scale1[l], f"proj_bwd_dx_{l}")
        dw_in = _proj_bwd_dw(x_l, gain[l], scale1[l], shift[l], dq, dkvb, dpb, f"proj_bwd_dw_{l}")
        dw_out, d_gate8 = _w_out_finish(g_acc, w_out_full[l], gate[l], f"w_out_finish_{l}")
        blocks_in = _by_core_chip(dw_in.reshape(d, N_DEV, w_cols).transpose(1, 0, 2).reshape(N_DEV, dense_rows, LANES))
        blocks_out = _by_core_chip(dw_out.reshape(N_DEV, D_MIX // N_DEV, d))
        r_in, r_out = _reduce_scatter(blocks_in, blocks_out, f"reduce_scatter_{l}")
        g_w_in[l] = r_in.reshape(d, w_cols)
        g_w_out[l] = r_out
        c0s, c1s = c0.sum(axis=0), c1.sum(axis=0)
        d_ada_rows[l] = jnp.concatenate([c0s, norm_gain[l] * c1s, d_gate8.sum(axis=0)])
        small[l] = (
            scale1[l, 0] * c1s,
            d_qg.sum(axis=0).reshape(N_HEADS, HEAD_DIM).sum(axis=0),
            d_kg.sum(axis=0).reshape(2, HEAD_DIM).sum(axis=0),
            d_sk[0, 0:N_HEADS],
            d_ws,
            d_bs.reshape(BLK, N_GROUPS, HEAD_DIM).sum(axis=2).transpose(1, 0),
        )

    names = ("norm_gain", "q_gain", "k_gain", "sink", "w_s", "b_s")
    stacked = [jnp.stack([small[l][t] for l in range(n_layers)]) for t in range(len(names))]
    d_ada = jnp.stack(d_ada_rows)
    packed, offsets = _pack_rows(stacked + [d_ada])
    gathered = _gather_small(packed)
    weights = (norm_gain, q_gain, k_gain, sink, w_s, b_s, b_ada)
    moments_m = (m_norm_gain, m_q_gain, m_k_gain, m_sink, m_w_s, m_b_s, m_b_ada)
    moments_v = (v_norm_gain, v_q_gain, v_k_gain, v_sink, v_w_s, v_b_s, v_b_ada)
    w_pack, _ = _pack_rows(weights)
    m_pack, _ = _pack_rows(moments_m)
    v_pack, _ = _pack_rows(moments_v)
    shapes = [w.shape for w in weights]
    g_small, d_small, m_small, v_small = (
        _unpack_rows(p, offsets, shapes) for p in _small_update(gathered, w_pack, m_pack, v_pack)
    )

    ada_off = offsets[-1]
    ada_n = -(-n_layers * 3 * d // (SUBLANES * LANES)) * SUBLANES
    d_ada_all = gathered[:, ada_off : ada_off + ada_n].reshape(N_DEV, -1)[:, : n_layers * 3 * d].reshape(N_DEV, n_layers, 3 * d)
    d_ada_cols = lax.dynamic_slice_in_dim(d_ada_all, my * ada_cols, ada_cols, axis=2)
    g_w_ada = _ada_weight_grad(c_all[:, 0, :], d_ada_cols.transpose(1, 0, 2))

    def update(w, g, m, v, name):
        shape = w.shape
        flat = lambda a: a.reshape(-1, shape[-1])
        return tuple(a.reshape(shape) for a in _adamw(flat(w), flat(g), flat(m), flat(v), name))

    g_w_in, g_w_out = jnp.stack(g_w_in), jnp.stack(g_w_out)
    upd_ada = update(w_ada, g_w_ada, m_w_ada, v_w_ada, "adamw_w_ada")
    upd_in = update(w_in, g_w_in, m_w_in, v_w_in, "adamw_w_in")
    upd_out = update(w_out, g_w_out, m_w_out, v_w_out, "adamw_w_out")

    def ordered(ada_, in_, out_, small_):
        ng, qg, kg, sk, ws, bs, ba = small_
        return (ada_, ba, ng, in_, qg, kg, sk, ws, bs, out_)

    grads = ordered(g_w_ada, g_w_in, g_w_out, g_small)
    deltas = ordered(upd_ada[0], upd_in[0], upd_out[0], d_small)
    new_m = ordered(upd_ada[1], upd_in[1], upd_out[1], m_small)
    new_v = ordered(upd_ada[2], upd_in[2], upd_out[2], v_small)
    return (loss, dx.reshape(x.shape), *grads, *deltas, *new_m, *new_v)
```

```python
import functools

import jax
import jax.numpy as jnp
from jax import lax
from jax.experimental import pallas as pl
from jax.experimental.pallas import tpu as pltpu

F32 = jnp.float32
MXU_DTYPE = jnp.bfloat16
MESH_ID = pl.DeviceIdType.MESH

N_DEV = 8
HEAD_DIM = 64
N_HEADS = 8
Q_PER_KV = 4
D_ATTN = 512
D_KV = 128
D_GM = 512
N_GROUPS = 8
D_MIX = D_ATTN + D_GM
BLK = 128
LANES = 128
SUBLANES = 8
N_PAIRS = D_ATTN // LANES
D_QKV = D_ATTN + 2 * D_KV
D_REST = D_ATTN + 3 * D_GM
D_IN = D_QKV + D_REST
EPS = 1e-6
NEG_INF = -1e30
ALIBI_SLOPES = tuple(2.0 ** (-8.0 * (h + 1) / N_HEADS) for h in range(N_HEADS))
Q_SCALE = 1.0 / 8.0

ADAM_LR = 0.001
ADAM_B1 = 0.9
ADAM_B2 = 0.999
ADAM_EPS = 1e-08
ADAM_WD = 0.01
ADAM_STEP = 10

TOKEN_TILE = 512
VMEM_LIMIT_BYTES = 56 * 1024 * 1024


def _params(semantics=None):
    return pltpu.CompilerParams(dimension_semantics=semantics, vmem_limit_bytes=VMEM_LIMIT_BYTES)


def _dot(a, b):
    return jnp.dot(a, b, preferred_element_type=F32)


def _dot_nt(a, b):
    return lax.dot_general(a, b, (((1,), (1,)), ((), ())), preferred_element_type=F32)


def _dot_tn(a, b):
    return lax.dot_general(a, b, (((0,), (0,)), ((), ())), preferred_element_type=F32)


def _mx(v):
    return v.astype(MXU_DTYPE)


def _lane_lo(rows):
    return lax.broadcasted_iota(jnp.int32, (rows, LANES), 1) < HEAD_DIM


def _half_ones():
    r = lax.broadcasted_iota(jnp.int32, (LANES, LANES), 0) < HEAD_DIM
    c = lax.broadcasted_iota(jnp.int32, (LANES, LANES), 1) < HEAD_DIM
    return jnp.where(r == c, 1.0, 0.0).astype(jnp.bfloat16)


def _half_sum(v, ones):
    p1 = v.astype(jnp.bfloat16)
    r1 = v - p1.astype(F32)
    p2 = r1.astype(jnp.bfloat16)
    p3 = (r1 - p2.astype(F32)).astype(jnp.bfloat16)
    return _dot(p1, ones) + _dot(p2, ones) + _dot(p3, ones)


def _half_rms(v, ones):
    r = lax.rsqrt(_half_sum(v * v, ones) * (1.0 / HEAD_DIM) + EPS)
    return v * r, r


def _half_rms_bwd(dy, vhat, r, ones):
    return r * (dy - vhat * (_half_sum(vhat * dy, ones) * (1.0 / HEAD_DIM)))


def _group_rows(v):
    rows, n = v.shape
    return v.reshape(rows // SUBLANES, SUBLANES, n).sum(axis=0)


def _sigmoid(v):
    return 1.0 / (1.0 + jnp.exp(-v))


ROW_CHUNK = 32
VARIANT_HEADS = ((0, 2, 5, 7), (1, 3, 4, 6))
HEAD_SLOT = {h: (v, s) for v, heads in enumerate(VARIANT_HEADS) for s, h in enumerate(heads)}
STACK = Q_PER_KV * BLK


def _fill_attn_bias(bias_s):
    qi = lax.broadcasted_iota(jnp.int32, (BLK, 3 * BLK), 0)
    ci = lax.broadcasted_iota(jnp.int32, (BLK, 3 * BLK), 1)
    dist = jnp.abs(ci - BLK - qi)
    distf = dist.astype(F32)
    for h in range(N_HEADS):
        bias_s[h] = jnp.where(dist <= BLK, -(ALIBI_SLOPES[h] * distf), NEG_INF)


def _edge_mask(block, seq):
    kpos = (block - 1) * BLK + lax.broadcasted_iota(jnp.int32, (1, 3 * BLK), 1)
    return jnp.where((kpos >= 0) & (kpos < seq), 0.0, NEG_INF)


def _stage_queries(qn, lo_t, j, nb, qs):
    for a in range(2):
        v, slot = HEAD_SLOT[2 * j + a]
        qm = _mx(jnp.where(lo_t, qn, 0.0) if a == 0 else jnp.where(lo_t, 0.0, qn))
        for n in range(nb):
            qs[n, v, slot * BLK : (slot + 1) * BLK, :] = qm[n * BLK : (n + 1) * BLK]


def _unstack_pair(stacked, j, lo):
    (v0, s0), (v1, s1) = HEAD_SLOT[2 * j], HEAD_SLOT[2 * j + 1]
    return jnp.where(lo, stacked[v0][s0 * BLK : (s0 + 1) * BLK], stacked[v1][s1 * BLK : (s1 + 1) * BLK])


def _stage_keys(kvp_ref, qkv_ref, kvn_ref, kg, ones, tile, ks, kr, vs, vr, khat_s=None, rk_s=None):
    pieces = (
        (0, BLK, kvp_ref[:, 0:D_KV], kvp_ref[:, D_KV : 2 * D_KV]),
        (BLK, tile, qkv_ref[:, D_ATTN : D_ATTN + D_KV], qkv_ref[:, D_ATTN + D_KV : D_QKV]),
        (BLK + tile, BLK, kvn_ref[:, 0:D_KV], kvn_ref[:, D_KV : 2 * D_KV]),
    )
    for r0, n, k, v in pieces:
        khat, rk = _half_rms(k, ones)
        kn = khat * kg
        ks[r0 : r0 + n, :] = _mx(kn)
        kr[r0 : r0 + n, :] = _mx(pltpu.roll(kn, HEAD_DIM, 1))
        vs[r0 : r0 + n, :] = _mx(v)
        vr[r0 : r0 + n, :] = _mx(pltpu.roll(v, HEAD_DIM, 1))
        if khat_s is not None:
            khat_s[r0 : r0 + n, :] = khat
            rk_s[r0 : r0 + n, :] = rk


def _halo_specs(tile, seq):
    nb = tile // BLK
    last = seq // BLK - 1
    kv_col = D_ATTN // (2 * D_KV)
    prev = pl.BlockSpec((BLK, 2 * D_KV), lambda i: (jnp.maximum(i * nb - 1, 0), kv_col))
    nxt = pl.BlockSpec((BLK, 2 * D_KV), lambda i: (jnp.minimum((i + 1) * nb, last), kv_col))
    return prev, nxt


def _row_spec(tile, width):
    return pl.BlockSpec((tile, width), lambda i: (i, 0))


def _full_spec(shape):
    nd = len(shape)
    return pl.BlockSpec(shape, lambda i: (0,) * nd)


SMEM_SPEC = pl.BlockSpec(memory_space=pltpu.SMEM)
VMEM_SPEC = pl.BlockSpec(memory_space=pltpu.VMEM)


def _layer_spec(layer, rows, cols):
    return pl.BlockSpec((None, rows, cols), lambda i: (layer, 0, 0))


def _ln_proj_fwd(x, gain, scale1, shift, w_in_t, layer, name):
    seq, d = x.shape
    tile = min(TOKEN_TILE, seq)

    def body(x_ref, g_ref, s1_ref, sh_ref, wt_ref, pa_ref, pb_ref):
        xv = x_ref[...]
        r = lax.rsqrt(jnp.mean(xv * xv, axis=-1, keepdims=True) + EPS)
        h = _mx((xv * r) * g_ref[...] * s1_ref[...] + sh_ref[...])
        pa_ref[...] = _dot_nt(h, wt_ref[0:D_QKV, :])
        pb_ref[...] = _dot_nt(h, wt_ref[D_QKV:D_IN, :])

    vec = _full_spec((1, d))
    return pl.pallas_call(
        body,
        name=name,
        grid=(seq // tile,),
        in_specs=[_row_spec(tile, d), vec, vec, vec, _layer_spec(layer, D_IN, d)],
        out_specs=[_row_spec(tile, D_QKV), _row_spec(tile, D_REST)],
        out_shape=[jax.ShapeDtypeStruct((seq, D_QKV), F32), jax.ShapeDtypeStruct((seq, D_REST), F32)],
        compiler_params=_params(("parallel",)),
    )(x, gain, scale1, shift, w_in_t)


def _attn_fwd(pa, q_gain2, k_gain2, sink, name):
    seq = pa.shape[0]
    tile = min(TOKEN_TILE, seq)
    nb = tile // BLK
    ext = tile + 2 * BLK

    def body(sink_ref, qkv_ref, kvp_ref, kvn_ref, qg_ref, kg_ref, o_ref, qs, ks, kr, vs, vr, bias_s, s_scr, p_scr, inv_scr):
        i = pl.program_id(0)

        @pl.when(i == 0)
        def _():
            _fill_attn_bias(bias_s)

        ones = _half_ones()
        lo = _lane_lo(BLK)
        lo_t = _lane_lo(tile)
        _stage_keys(kvp_ref, qkv_ref, kvn_ref, kg_ref[...], ones, tile, ks, kr, vs, vr)
        for j in range(N_PAIRS):
            qhat, _ = _half_rms(qkv_ref[:, j * LANES : (j + 1) * LANES], ones)
            _stage_queries(qhat * (qg_ref[...] * Q_SCALE), lo_t, j, nb, qs)

        def block(n, carry):
            r0 = pl.multiple_of(n * BLK, BLK)
            krows = pl.ds(r0, 3 * BLK)
            edge = _edge_mask(i * nb + n, seq)
            for v in range(2):
                s_scr[v] = _dot_nt(qs[n, v], (kr if v else ks)[krows, :])
            for h in range(N_HEADS):
                v, slot = HEAD_SLOT[h]
                sink_h = sink_ref[h]
                for rc in range(0, BLK, ROW_CHUNK):
                    rows = slice(slot * BLK + rc, slot * BLK + rc + ROW_CHUNK)
                    s = s_scr[v, rows, :] + bias_s[h, rc : rc + ROW_CHUNK, :] + edge
                    m = jnp.maximum(jnp.max(s, axis=-1, keepdims=True), sink_h)
                    p = jnp.exp(s - m)
                    total = jnp.sum(p, axis=-1, keepdims=True) + jnp.exp(sink_h - m)
                    p_scr[v, rows, :] = _mx(p)
                    inv_scr[v, rows, :] = jnp.broadcast_to(1.0 / total, (ROW_CHUNK, LANES))
            outs = [_dot(p_scr[v], (vr if v else vs)[krows, :]) * inv_scr[v] for v in range(2)]
            for j in range(N_PAIRS):
                o_ref[pl.ds(r0, BLK), j * LANES : (j + 1) * LANES] = _unstack_pair(outs, j, lo)
            return carry

        lax.fori_loop(0, nb, block, 0)

    prev, nxt = _halo_specs(tile, seq)
    vec = _full_spec((1, LANES))
    return pl.pallas_call(
        body,
        name=name,
        grid=(seq // tile,),
        in_specs=[SMEM_SPEC, _row_spec(tile, D_QKV), prev, nxt, vec, vec],
        out_specs=_row_spec(tile, D_ATTN),
        out_shape=jax.ShapeDtypeStruct((seq, D_ATTN), F32),
        scratch_shapes=[
            pltpu.VMEM((nb, 2, STACK, LANES), MXU_DTYPE),
            pltpu.VMEM((ext, LANES), MXU_DTYPE),
            pltpu.VMEM((ext, LANES), MXU_DTYPE),
            pltpu.VMEM((ext, LANES), MXU_DTYPE),
            pltpu.VMEM((ext, LANES), MXU_DTYPE),
            pltpu.VMEM((N_HEADS, BLK, 3 * BLK), F32),
            pltpu.VMEM((2, STACK, 3 * BLK), F32),
            pltpu.VMEM((2, STACK, 3 * BLK), MXU_DTYPE),
            pltpu.VMEM((2, STACK, LANES), F32),
        ],
        compiler_params=_params(("arbitrary",)),
    )(sink, pa, pa, pa, q_gain2, k_gain2)


def _mix_out_fwd(pb, o, x, gate, w_out, layer, w_s, b_st, name):
    seq, d = x.shape
    tile = min(TOKEN_TILE, seq)
    nb = tile // BLK

    def body(pb_ref, o_ref, x_ref, gate_ref, wo_ref, ws_ref, bs_ref, xo_ref, y_s, vn_s):
        ones = _half_ones()
        lo = _lane_lo(BLK)
        ga = pb_ref[:, 0:D_ATTN]
        y_s[:, 0:D_ATTN] = _mx(o_ref[...] * (ga * _sigmoid(ga)))
        for j in range(N_PAIRS):
            cols = slice(2 * D_GM + j * LANES, 2 * D_GM + (j + 1) * LANES)
            vhat, _ = _half_rms(pb_ref[:, cols], ones)
            vn_s[:, j * LANES : (j + 1) * LANES] = _mx(vhat)

        def chunk(n, carry):
            rows = pl.ds(pl.multiple_of(n * BLK, BLK), BLK)
            for j in range(N_PAIRS):
                cols = slice(j * LANES, (j + 1) * LANES)
                vn = vn_s[rows, cols]
                sv = jnp.where(lo, _dot(ws_ref[2 * j], vn), _dot(ws_ref[2 * j + 1], vn)) + bs_ref[:, cols]
                u = pb_ref[rows, D_ATTN + j * LANES : D_ATTN + (j + 1) * LANES]
                gg = pb_ref[rows, D_ATTN + 2 * D_GM + j * LANES : D_ATTN + 2 * D_GM + (j + 1) * LANES]
                y_s[rows, D_ATTN + j * LANES : D_ATTN + (j + 1) * LANES] = _mx((u * sv) * (gg * _sigmoid(gg)))
            return carry

        lax.fori_loop(0, nb, chunk, 0)
        xo_ref[...] = x_ref[...] + gate_ref[...] * _dot(y_s[...], wo_ref[...])

    return pl.pallas_call(
        body,
        name=name,
        grid=(seq // tile,),
        in_specs=[
            _row_spec(tile, D_REST),
            _row_spec(tile, D_ATTN),
            _row_spec(tile, d),
            _full_spec((1, d)),
            _layer_spec(layer, D_MIX, d),
            _full_spec((N_GROUPS, BLK, BLK)),
            _full_spec((BLK, D_GM)),
        ],
        out_specs=_row_spec(tile, d),
        out_shape=jax.ShapeDtypeStruct((seq, d), F32),
        scratch_shapes=[pltpu.VMEM((tile, D_MIX), MXU_DTYPE), pltpu.VMEM((tile, D_GM), MXU_DTYPE)],
        compiler_params=_params(("parallel",)),
    )(pb, o, x, gate, w_out, w_s, b_st)


def _loss_grad(y, target):
    seq, d = y.shape
    tile = min(TOKEN_TILE, seq)

    def body(y_ref, t_ref, dy_ref, acc_ref):
        @pl.when(pl.program_id(0) == 0)
        def _():
            acc_ref[...] = jnp.zeros_like(acc_ref)

        e = y_ref[...] - t_ref[...]
        dy_ref[...] = e * (1.0 / d)
        acc_ref[...] += jnp.sum(jnp.sum(e * e, axis=-1, keepdims=True), axis=0, keepdims=True)

    return pl.pallas_call(
        body,
        name="loss_grad",
        grid=(seq // tile,),
        in_specs=[_row_spec(tile, d), _row_spec(tile, d)],
        out_specs=[_row_spec(tile, d), _full_spec((SUBLANES, LANES))],
        out_shape=[jax.ShapeDtypeStruct((seq, d), F32), jax.ShapeDtypeStruct((SUBLANES, LANES), F32)],
        compiler_params=_params(("arbitrary",)),
    )(y, target)


def _mix_out_bwd(dxn, pb, o, gate, w_out, layer, w_s, w_s_t, b_st, name):
    seq, d = dxn.shape
    tile = min(TOKEN_TILE, seq)
    nb = tile // BLK

    def body(dxn_ref, pb_ref, o_ref, gate_ref, wo_ref, ws_ref, wst_ref, bs_ref,
             dpb_ref, do_ref, g_ref, dws_ref, dbs_ref, y_s, dy_s, vn_s, rv_s):
        @pl.when(pl.program_id(0) == 0)
        def _():
            g_ref[...] = jnp.zeros_like(g_ref)
            dws_ref[...] = jnp.zeros_like(dws_ref)
            dbs_ref[...] = jnp.zeros_like(dbs_ref)

        ones = _half_ones()
        lo = _lane_lo(BLK)
        dxv = dxn_ref[...]
        dy_s[...] = _dot_nt(_mx(dxv * gate_ref[...]), wo_ref[...])
        ga = pb_ref[:, 0:D_ATTN]
        sig = _sigmoid(ga)
        sil = ga * sig
        ov = o_ref[...]
        y_s[:, 0:D_ATTN] = _mx(ov * sil)
        da = dy_s[:, 0:D_ATTN]
        do_ref[...] = da * sil
        dpb_ref[:, 0:D_ATTN] = (da * ov * (sig * (1.0 + ga * (1.0 - sig)))).astype(dpb_ref.dtype)
        for j in range(N_PAIRS):
            cols = slice(j * LANES, (j + 1) * LANES)
            vhat, rv = _half_rms(pb_ref[:, 2 * D_GM + j * LANES : 2 * D_GM + (j + 1) * LANES], ones)
            vn_s[:, cols] = vhat
            rv_s[:, cols] = rv

        def chunk(n, carry):
            rows = pl.ds(pl.multiple_of(n * BLK, BLK), BLK)
            for j in range(N_PAIRS):
                cols = slice(j * LANES, (j + 1) * LANES)
                c_u = slice(D_ATTN + j * LANES, D_ATTN + (j + 1) * LANES)
                c_vg = slice(D_ATTN + D_GM + j * LANES, D_ATTN + D_GM + (j + 1) * LANES)
                c_gg = slice(D_ATTN + 2 * D_GM + j * LANES, D_ATTN + 2 * D_GM + (j + 1) * LANES)
                vhat = vn_s[rows, cols]
                vn = _mx(vhat)
                sv = jnp.where(lo, _dot(ws_ref[2 * j], vn), _dot(ws_ref[2 * j + 1], vn)) + bs_ref[:, cols]
                u = pb_ref[rows, c_u]
                gg = pb_ref[rows, c_gg]
                sg = _sigmoid(gg)
                silg = gg * sg
                m0 = u * sv
                y_s[rows, c_u] = _mx(m0 * silg)
                dm = dy_s[rows, c_u]
                dm0 = dm * silg
                dpb_ref[rows, c_gg] = (dm * m0 * (sg * (1.0 + gg * (1.0 - sg)))).astype(dpb_ref.dtype)
                dpb_ref[rows, c_u] = (dm0 * sv).astype(dpb_ref.dtype)
                dsv = dm0 * u
                dbs_ref[:, cols] += dsv
                dws_ref[2 * j] += _dot_nt(_mx(jnp.where(lo, dsv, 0.0)), vn)
                dws_ref[2 * j + 1] += _dot_nt(_mx(jnp.where(lo, 0.0, dsv)), vn)
                dsv_m = _mx(dsv)
                dvn = jnp.where(lo, _dot(wst_ref[2 * j], dsv_m), _dot(wst_ref[2 * j + 1], dsv_m))
                dpb_ref[rows, c_vg] = _half_rms_bwd(dvn, vhat, rv_s[rows, cols], ones).astype(dpb_ref.dtype)
            return carry

        lax.fori_loop(0, nb, chunk, 0)
        g_ref[...] += _dot_tn(y_s[...], _mx(dxv))

    return pl.pallas_call(
        body,
        name=name,
        grid=(seq // tile,),
        in_specs=[
            _row_spec(tile, d),
            _row_spec(tile, D_REST),
            _row_spec(tile, D_ATTN),
            _full_spec((1, d)),
            _layer_spec(layer, D_MIX, d),
            _full_spec((N_GROUPS, BLK, BLK)),
            _full_spec((N_GROUPS, BLK, BLK)),
            _full_spec((BLK, D_GM)),
        ],
        out_specs=[
            _row_spec(tile, D_REST),
            _row_spec(tile, D_ATTN),
            _full_spec((D_MIX, d)),
            _full_spec((N_GROUPS, BLK, BLK)),
            _full_spec((BLK, D_GM)),
        ],
        out_shape=[
            jax.ShapeDtypeStruct((seq, D_REST), MXU_DTYPE),
            jax.ShapeDtypeStruct((seq, D_ATTN), F32),
            jax.ShapeDtypeStruct((D_MIX, d), F32),
            jax.ShapeDtypeStruct((N_GROUPS, BLK, BLK), F32),
            jax.ShapeDtypeStruct((BLK, D_GM), F32),
        ],
        scratch_shapes=[
            pltpu.VMEM((tile, D_MIX), MXU_DTYPE),
            pltpu.VMEM((tile, D_MIX), F32),
            pltpu.VMEM((tile, D_GM), F32),
            pltpu.VMEM((tile, D_GM), F32),
        ],
        compiler_params=_params(("arbitrary",)),
    )(dxn, pb, o, gate, w_out, w_s, w_s_t, b_st)


def _attn_bwd(pa, o, do, q_gain2, k_gain2, sink, name):
    seq = pa.shape[0]
    tile = min(TOKEN_TILE, seq)
    nb = tile // BLK
    nt = seq // tile
    ext = tile + 2 * BLK

    def body(sink_ref, qkv_ref, kvp_ref, kvn_ref, o_ref, do_ref, qg_ref, kg_ref,
             dq_ref, dkv_ref, hp_ref, hn_ref, dqg_ref, dkg_ref, dsk_ref,
             qs, dos, qhat_s, rq_s, ks, kr, vs, vr, khat_s, rk_s, dqn_s, dka, dva, bias_s, s_scr, dp_scr, p_scr, ds_scr):
        i = pl.program_id(0)

        @pl.when(i == 0)
        def _():
            dqg_ref[...] = jnp.zeros_like(dqg_ref)
            dkg_ref[...] = jnp.zeros_like(dkg_ref)
            dsk_ref[...] = jnp.zeros_like(dsk_ref)
            _fill_attn_bias(bias_s)

        ones = _half_ones()
        lo = _lane_lo(BLK)
        lo_t = _lane_lo(tile)
        lo_c = _lane_lo(ROW_CHUNK)
        qg = qg_ref[...] * Q_SCALE
        kg = kg_ref[...]
        _stage_keys(kvp_ref, qkv_ref, kvn_ref, kg, ones, tile, ks, kr, vs, vr, khat_s, rk_s)
        for j in range(N_PAIRS):
            cols = slice(j * LANES, (j + 1) * LANES)
            qhat, rq = _half_rms(qkv_ref[:, cols], ones)
            qhat_s[:, cols] = qhat
            rq_s[:, cols] = rq
            _stage_queries(qhat * qg, lo_t, j, nb, qs)
            _stage_queries(do_ref[:, cols], lo_t, j, nb, dos)
        dka[...] = jnp.zeros_like(dka)
        dva[...] = jnp.zeros_like(dva)
        head_lane = lax.broadcasted_iota(jnp.int32, (1, LANES), 1)

        def block(n, dsink):
            r0 = pl.multiple_of(n * BLK, BLK)
            krows = pl.ds(r0, 3 * BLK)
            edge = _edge_mask(i * nb + n, seq)
            for v in range(2):
                s_scr[v] = _dot_nt(qs[n, v], (kr if v else ks)[krows, :])
                dp_scr[v] = _dot_nt(dos[n, v], (vr if v else vs)[krows, :])
            for h in range(N_HEADS):
                v, slot = HEAD_SLOT[h]
                j, a = divmod(h, 2)
                cols = slice(j * LANES, (j + 1) * LANES)
                sink_h = sink_ref[h]
                sink_part = jnp.zeros((ROW_CHUNK, 1), F32)
                for rc in range(0, BLK, ROW_CHUNK):
                    rows = slice(slot * BLK + rc, slot * BLK + rc + ROW_CHUNK)
                    trows = pl.ds(pl.multiple_of(r0 + rc, ROW_CHUNK), ROW_CHUNK)
                    s = s_scr[v, rows, :] + bias_s[h, rc : rc + ROW_CHUNK, :] + edge
                    m = jnp.maximum(jnp.max(s, axis=-1, keepdims=True), sink_h)
                    p = jnp.exp(s - m)
                    e_sink = jnp.exp(sink_h - m)
                    inv = 1.0 / (jnp.sum(p, axis=-1, keepdims=True) + e_sink)
                    pn = p * inv
                    prod = do_ref[trows, cols] * o_ref[trows, cols]
                    prod = jnp.where(lo_c, prod, 0.0) if a == 0 else jnp.where(lo_c, 0.0, prod)
                    dcol = jnp.sum(prod, axis=-1, keepdims=True)
                    ds_scr[v, rows, :] = _mx(pn * (dp_scr[v, rows, :] - dcol))
                    p_scr[v, rows, :] = _mx(pn)
                    sink_part = sink_part + (e_sink * inv) * dcol
                dsink = dsink - jnp.where(head_lane == h, jnp.sum(sink_part, axis=0, keepdims=True), 0.0)
            dqv = []
            for v in range(2):
                dqv.append(_dot(ds_scr[v], (kr if v else ks)[krows, :]))
                dka[v, krows, :] += _dot_tn(ds_scr[v], qs[n, v])
                dva[v, krows, :] += _dot_tn(p_scr[v], dos[n, v])
            for j in range(N_PAIRS):
                dqn_s[pl.ds(r0, BLK), j * LANES : (j + 1) * LANES] = _unstack_pair(dqv, j, lo)
            return dsink

        dsink = lax.fori_loop(0, nb, block, jnp.zeros((1, LANES), F32))
        dsk_ref[...] += jnp.broadcast_to(dsink, (SUBLANES, LANES))
        for j in range(N_PAIRS):
            cols = slice(j * LANES, (j + 1) * LANES)
            dqn = dqn_s[:, cols]
            qhat = qhat_s[:, cols]
            dqg_ref[:, cols] += _group_rows(dqn * qhat) * Q_SCALE
            dq_ref[:, cols] = _half_rms_bwd(dqn * qg, qhat, rq_s[:, cols], ones).astype(dq_ref.dtype)
        dkn = dka[0] + pltpu.roll(dka[1], HEAD_DIM, 1)
        khat = khat_s[...]
        dkg_ref[...] += _group_rows(dkn * khat)
        dk = _half_rms_bwd(dkn * kg, khat, rk_s[...], ones)
        dv = dva[0] + pltpu.roll(dva[1], HEAD_DIM, 1)
        hp_ref[:, 0:D_KV] = dk[0:BLK]
        hp_ref[:, D_KV : 2 * D_KV] = dv[0:BLK]
        dkv_ref[:, 0:D_KV] = dk[BLK : BLK + tile]
        dkv_ref[:, D_KV : 2 * D_KV] = dv[BLK : BLK + tile]
        hn_ref[:, 0:D_KV] = dk[BLK + tile : ext]
        hn_ref[:, D_KV : 2 * D_KV] = dv[BLK + tile : ext]

    prev, nxt = _halo_specs(tile, seq)
    vec = _full_spec((1, LANES))
    halo = pl.BlockSpec((None, BLK, 2 * D_KV), lambda i: (i, 0, 0))
    return pl.pallas_call(
        body,
        name=name,
        grid=(nt,),
        in_specs=[SMEM_SPEC, _row_spec(tile, D_QKV), prev, nxt, _row_spec(tile, D_ATTN), _row_spec(tile, D_ATTN), vec, vec],
        out_specs=[
            _row_spec(tile, D_ATTN),
            _row_spec(tile, 2 * D_KV),
            halo,
            halo,
            _full_spec((SUBLANES, D_ATTN)),
            _full_spec((SUBLANES, LANES)),
            _full_spec((SUBLANES, LANES)),
        ],
        out_shape=[
            jax.ShapeDtypeStruct((seq, D_ATTN), MXU_DTYPE),
            jax.ShapeDtypeStruct((seq, 2 * D_KV), F32),
            jax.ShapeDtypeStruct((nt, BLK, 2 * D_KV), F32),
            jax.ShapeDtypeStruct((nt, BLK, 2 * D_KV), F32),
            jax.ShapeDtypeStruct((SUBLANES, D_ATTN), F32),
            jax.ShapeDtypeStruct((SUBLANES, LANES), F32),
            jax.ShapeDtypeStruct((SUBLANES, LANES), F32),
        ],
        scratch_shapes=[
            pltpu.VMEM((nb, 2, STACK, LANES), MXU_DTYPE),
            pltpu.VMEM((nb, 2, STACK, LANES), MXU_DTYPE),
            pltpu.VMEM((tile, D_ATTN), F32),
            pltpu.VMEM((tile, D_ATTN), F32),
            pltpu.VMEM((ext, LANES), MXU_DTYPE),
            pltpu.VMEM((ext, LANES), MXU_DTYPE),
            pltpu.VMEM((ext, LANES), MXU_DTYPE),
            pltpu.VMEM((ext, LANES), MXU_DTYPE),
            pltpu.VMEM((ext, LANES), F32),
            pltpu.VMEM((ext, LANES), F32),
            pltpu.VMEM((tile, D_ATTN), F32),
            pltpu.VMEM((2, ext, LANES), F32),
            pltpu.VMEM((2, ext, LANES), F32),
            pltpu.VMEM((N_HEADS, BLK, 3 * BLK), F32),
            pltpu.VMEM((2, STACK, 3 * BLK), F32),
            pltpu.VMEM((2, STACK, 3 * BLK), F32),
            pltpu.VMEM((2, STACK, 3 * BLK), MXU_DTYPE),
            pltpu.VMEM((2, STACK, 3 * BLK), MXU_DTYPE),
        ],
        compiler_params=_params(("arbitrary",)),
    )(sink, pa, pa, pa, o, do, q_gain2, k_gain2)


def _halo_in_specs(tile, nt):
    from_prev = pl.BlockSpec((None, BLK, 2 * D_KV), lambda i: (jnp.maximum(i - 1, 0), 0, 0))
    from_next = pl.BlockSpec((None, BLK, 2 * D_KV), lambda i: (jnp.minimum(i + 1, nt - 1), 0, 0))
    return from_prev, from_next


def _proj_bwd_dx(x, dxn, dq, dkv, halo_prev, halo_next, dpb, w_in_t, layer, gain, scale1, name):
    seq, d = x.shape
    tile = min(TOKEN_TILE, seq)
    nt = seq // tile

    def body(x_ref, dxn_ref, dq_ref, dkv_ref, hn_ref, hp_ref, dpb_ref, wt_ref, g_ref, s1_ref,
             dx_ref, dkvb_ref, c0_ref, c1_ref):
        i = pl.program_id(0)

        @pl.when(i == 0)
        def _():
            c0_ref[...] = jnp.zeros_like(c0_ref)
            c1_ref[...] = jnp.zeros_like(c1_ref)

        top = dkv_ref[0:BLK, :] + jnp.where(i > 0, hn_ref[...], 0.0)
        bot = dkv_ref[tile - BLK : tile, :] + jnp.where(i < nt - 1, hp_ref[...], 0.0)
        if tile == BLK:
            dkvb_ref[...] = (top + bot - dkv_ref[...]).astype(dkvb_ref.dtype)
        else:
            dkvb_ref[0:BLK, :] = top.astype(dkvb_ref.dtype)
            dkvb_ref[tile - BLK : tile, :] = bot.astype(dkvb_ref.dtype)
            if tile > 2 * BLK:
                dkvb_ref[BLK : tile - BLK, :] = dkv_ref[BLK : tile - BLK, :].astype(dkvb_ref.dtype)
        dh = (
            _dot(dq_ref[...], wt_ref[0:D_ATTN, :])
            + _dot(dkvb_ref[...], wt_ref[D_ATTN:D_QKV, :])
            + _dot(dpb_ref[...], wt_ref[D_QKV:D_IN, :])
        )
        xv = x_ref[...]
        r = lax.rsqrt(jnp.mean(xv * xv, axis=-1, keepdims=True) + EPS)
        xn = xv * r
        c0_ref[...] += _group_rows(dh)
        c1_ref[...] += _group_rows(dh * xn)
        dxn_ = dh * (g_ref[...] * s1_ref[...])
        dx_ref[...] = dxn_ref[...] + r * (dxn_ - xn * jnp.mean(xn * dxn_, axis=-1, keepdims=True))

    from_prev, from_next = _halo_in_specs(tile, nt)
    vec = _full_spec((1, d))
    return pl.pallas_call(
        body,
        name=name,
        grid=(nt,),
        in_specs=[
            _row_spec(tile, d),
            _row_spec(tile, d),
            _row_spec(tile, D_ATTN),
            _row_spec(tile, 2 * D_KV),
            from_prev,
            from_next,
            _row_spec(tile, D_REST),
            _layer_spec(layer, D_IN, d),
            vec,
            vec,
        ],
        out_specs=[_row_spec(tile, d), _row_spec(tile, 2 * D_KV), _full_spec((SUBLANES, d)), _full_spec((SUBLANES, d))],
        out_shape=[
            jax.ShapeDtypeStruct((seq, d), F32),
            jax.ShapeDtypeStruct((seq, 2 * D_KV), MXU_DTYPE),
            jax.ShapeDtypeStruct((SUBLANES, d), F32),
            jax.ShapeDtypeStruct((SUBLANES, d), F32),
        ],
        compiler_params=_params(("arbitrary",)),
    )(x, dxn, dq, dkv, halo_next, halo_prev, dpb, w_in_t, gain, scale1)


def _proj_bwd_dw(x, gain, scale1, shift, dq, dkvb, dpb, name):
    seq, d = x.shape
    tile = min(TOKEN_TILE, seq)

    def body(x_ref, g_ref, s1_ref, sh_ref, dq_ref, dkv_ref, dpb_ref, dw_ref):
        @pl.when(pl.program_id(0) == 0)
        def _():
            dw_ref[...] = jnp.zeros_like(dw_ref)

        xv = x_ref[...]
        r = lax.rsqrt(jnp.mean(xv * xv, axis=-1, keepdims=True) + EPS)
        h = _mx((xv * r) * g_ref[...] * s1_ref[...] + sh_ref[...])
        dw_ref[0:D_ATTN, :] += _dot_tn(dq_ref[...], h)
        dw_ref[D_ATTN:D_QKV, :] += _dot_tn(dkv_ref[...], h)
        dw_ref[D_QKV:D_IN, :] += _dot_tn(dpb_ref[...], h)

    vec = _full_spec((1, d))
    return pl.pallas_call(
        body,
        name=name,
        grid=(seq // tile,),
        in_specs=[_row_spec(tile, d), vec, vec, vec, _row_spec(tile, D_ATTN), _row_spec(tile, 2 * D_KV), _row_spec(tile, D_REST)],
        out_specs=_full_spec((D_IN, d)),
        out_shape=jax.ShapeDtypeStruct((D_IN, d), F32),
        compiler_params=_params(("arbitrary",)),
    )(x, gain, scale1, shift, dq, dkvb, dpb)


def _w_out_finish(g, w_out, layer, gate, name):
    d_mix, d = g.shape

    def body(g_ref, w_ref, gate_ref, dw_ref, dgate_ref):
        gv = g_ref[...]
        dw_ref[...] = gv * gate_ref[...]
        dgate_ref[...] = _group_rows(gv * w_ref[layer].astype(F32))

    return pl.pallas_call(
        body,
        name=name,
        in_specs=[VMEM_SPEC, VMEM_SPEC, VMEM_SPEC],
        out_specs=[VMEM_SPEC, VMEM_SPEC],
        out_shape=[jax.ShapeDtypeStruct((d_mix, d), F32), jax.ShapeDtypeStruct((SUBLANES, d), F32)],
        compiler_params=_params(),
    )(g, w_out, gate)


def _adamw_math(w, g, m, v):
    m = ADAM_B1 * m + (1.0 - ADAM_B1) * g
    v = ADAM_B2 * v + (1.0 - ADAM_B2) * (g * g)
    m_hat = m / (1.0 - ADAM_B1**ADAM_STEP)
    v_hat = v / (1.0 - ADAM_B2**ADAM_STEP)
    delta = -ADAM_LR * (m_hat / (jnp.sqrt(v_hat) + ADAM_EPS) + ADAM_WD * w)
    return delta, m, v


def _adamw(w, g, m, v, name):
    rows, cols = w.shape
    tile = min(TOKEN_TILE, rows)

    def body(w_ref, g_ref, m_ref, v_ref, d_ref, mo_ref, vo_ref):
        d_ref[...], mo_ref[...], vo_ref[...] = _adamw_math(w_ref[...], g_ref[...], m_ref[...], v_ref[...])

    spec = _row_spec(tile, cols)
    shape = jax.ShapeDtypeStruct((rows, cols), F32)
    return pl.pallas_call(
        body,
        name=name,
        grid=(rows // tile,),
        in_specs=[spec] * 4,
        out_specs=[spec] * 3,
        out_shape=[shape] * 3,
        compiler_params=_params(("parallel",)),
    )(w, g, m, v)


def _small_update(gathered, w, m, v):
    rows = w.shape[0]

    def body(ga_ref, w_ref, m_ref, v_ref, g_ref, d_ref, mo_ref, vo_ref):
        g = ga_ref[0]
        for j in range(1, N_DEV):
            g = g + ga_ref[j]
        g_ref[...] = g
        d_ref[...], mo_ref[...], vo_ref[...] = _adamw_math(w_ref[...], g, m_ref[...], v_ref[...])

    shape = jax.ShapeDtypeStruct((rows, LANES), F32)
    return pl.pallas_call(
        body,
        name="small_update",
        in_specs=[VMEM_SPEC] * 4,
        out_specs=[VMEM_SPEC] * 4,
        out_shape=[shape] * 4,
        compiler_params=_params(),
    )(gathered, w, m, v)


def _ada_weight_grad(c_all, d_ada_cols):
    d = c_all.shape[-1]
    n_layers, _, width = d_ada_cols.shape

    def body(c_ref, da_ref, dw_ref):
        cv = c_ref[...]
        cond = cv * _sigmoid(cv)
        for l in range(n_layers):
            dw_ref[l] = lax.dot_general(
                cond, da_ref[l], (((0,), (0,)), ((), ())), preferred_element_type=F32, precision=lax.Precision.HIGHEST
            )

    return pl.pallas_call(
        body,
        name="ada_weight_grad",
        in_specs=[VMEM_SPEC, VMEM_SPEC],
        out_specs=VMEM_SPEC,
        out_shape=jax.ShapeDtypeStruct((n_layers, d, width), F32),
        compiler_params=_params(),
    )(c_all, d_ada_cols)


def _position():
    return lax.axis_index("x"), lax.axis_index("y"), lax.axis_index("c")


def _flip(pos, k):
    x, y, c = pos
    return (1 - x if k & 4 else x, 1 - y if k & 2 else y, 1 - c if k & 1 else c)


def _index(pos):
    x, y, c = pos
    return 4 * x + 2 * y + c


def _remote(src, dst, send_sem, recv_sem, to):
    return pltpu.make_async_remote_copy(
        src_ref=src, dst_ref=dst, send_sem=send_sem, recv_sem=recv_sem, device_id=to, device_id_type=MESH_ID
    )


def _two_level_all_gather(slots, send_sems, recv_sems):
    me = _position()
    sibling = _flip(me, 1)
    others = (4, 2, 6)

    def copy(t, k, block, to):
        slot = slots[t](_index(block))
        return _remote(slot, slot, send_sems.at[7 * t + k], recv_sems.at[7 * t + k], to)

    started = []
    for t in range(len(slots)):
        started.append(copy(t, 0, me, sibling))
        started += [copy(t, 1 + j, me, _flip(me, f)) for j, f in enumerate(others)]
    for cp in started:
        cp.start()
    for j, f in enumerate(others):
        for t in range(len(slots)):
            copy(t, 1 + j, _flip(me, f), me).wait_recv()
            passed = copy(t, 4 + j, _flip(me, f), sibling)
            passed.start()
            started.append(passed)
    for t in range(len(slots)):
        copy(t, 0, sibling, me).wait_recv()
        for j, f in enumerate(others):
            copy(t, 4 + j, _flip(sibling, f), me).wait_recv()
    for cp in started:
        cp.wait_send()


def _row_block(ref, rows):
    return lambda j: ref.at[:, pl.ds(pl.multiple_of(j * rows, 16), rows), :]


def _ada_rows(c_row, w_ada):
    d = c_row.shape[-1]
    n_layers, _, width = w_ada.shape

    def body(c_ref, w_ref, call_ref, parts_ref, sbuf, sem_s1, sem_r1, sem_s2, sem_r2):
        me = _position()
        my = _index(me)
        call_ref[my] = jnp.broadcast_to(c_ref[...], (SUBLANES, d))
        mine = call_ref.at[my]
        first = [_remote(mine, mine, sem_s1.at[k - 1], sem_r1.at[k - 1], _flip(me, k)) for k in range(1, N_DEV)]
        for cp in first:
            cp.start()
        for k in range(1, N_DEV):
            theirs = call_ref.at[_index(_flip(me, k))]
            _remote(theirs, theirs, sem_s1.at[k - 1], sem_r1.at[k - 1], _flip(me, k)).wait_recv()
        for b in range(N_DEV):
            cv = call_ref[b]
            cond = cv * _sigmoid(cv)
            for l in range(n_layers):
                sbuf[b, l] = jnp.dot(cond, w_ref[l], preferred_element_type=F32, precision=lax.Precision.HIGHEST)
        parts_ref[my] = sbuf[my]
        second = []
        for k in range(1, N_DEV):
            to = _flip(me, k)
            second.append(_remote(sbuf.at[_index(to)], parts_ref.at[my], sem_s2.at[k - 1], sem_r2.at[k - 1], to))
        for cp in second:
            cp.start()
        for k in range(1, N_DEV):
            theirs = parts_ref.at[_index(_flip(me, k))]
            _remote(theirs, theirs, sem_s2.at[k - 1], sem_r2.at[k - 1], _flip(me, k)).wait_recv()
        for cp in first + second:
            cp.wait_send()

    return pl.pallas_call(
        body,
        name="ada_rows",
        in_specs=[VMEM_SPEC, VMEM_SPEC],
        out_specs=[VMEM_SPEC, VMEM_SPEC],
        out_shape=[
            jax.ShapeDtypeStruct((N_DEV, SUBLANES, d), F32),
            jax.ShapeDtypeStruct((N_DEV, n_layers, SUBLANES, width), F32),
        ],
        scratch_shapes=[
            pltpu.VMEM((N_DEV, n_layers, SUBLANES, width), F32),
            pltpu.SemaphoreType.DMA((N_DEV - 1,)),
            pltpu.SemaphoreType.DMA((N_DEV - 1,)),
            pltpu.SemaphoreType.DMA((N_DEV - 1,)),
            pltpu.SemaphoreType.DMA((N_DEV - 1,)),
        ],
        compiler_params=_params(),
    )(c_row, w_ada)


def _gather_weights(w_in_t, w_out):
    n_layers, rows_in, d = w_in_t.shape
    rows_out = w_out.shape[1]

    def body(wi_ref, wo_ref, gi_ref, go_ref, send_sems, recv_sems):
        my = _index(_position())
        gi_ref[:, pl.ds(pl.multiple_of(my * rows_in, 16), rows_in), :] = wi_ref[...].astype(gi_ref.dtype)
        go_ref[:, pl.ds(pl.multiple_of(my * rows_out, 16), rows_out), :] = wo_ref[...].astype(go_ref.dtype)
        _two_level_all_gather((_row_block(gi_ref, rows_in), _row_block(go_ref, rows_out)), send_sems, recv_sems)

    return pl.pallas_call(
        body,
        name="gather_weights",
        in_specs=[VMEM_SPEC, VMEM_SPEC],
        out_specs=[VMEM_SPEC, VMEM_SPEC],
        out_shape=[
            jax.ShapeDtypeStruct((n_layers, N_DEV * rows_in, d), MXU_DTYPE),
            jax.ShapeDtypeStruct((n_layers, N_DEV * rows_out, d), MXU_DTYPE),
        ],
        scratch_shapes=[pltpu.SemaphoreType.DMA((14,)), pltpu.SemaphoreType.DMA((14,))],
        compiler_params=_params(),
    )(w_in_t, w_out)


def _gather_small(packed):
    def body(p_ref, g_ref, send_sems, recv_sems):
        g_ref[_index(_position())] = p_ref[...]
        _two_level_all_gather((lambda j: g_ref.at[j],), send_sems, recv_sems)

    return pl.pallas_call(
        body,
        name="gather_small",
        in_specs=[VMEM_SPEC],
        out_specs=VMEM_SPEC,
        out_shape=jax.ShapeDtypeStruct((N_DEV,) + packed.shape, F32),
        scratch_shapes=[pltpu.SemaphoreType.DMA((7,)), pltpu.SemaphoreType.DMA((7,))],
        compiler_params=_params(),
    )(packed)


def _reduce_scatter(blocks_a, blocks_b, name):
    wire = jnp.bfloat16

    def body(a_ref, b_ref, oa_ref, ob_ref, stage_a, stage_b, half_a, half_b, send_a, send_b, chips_a, chips_b, send_sems, recv_sems):
        me = _position()
        x, y, c = me
        sibling = _flip(me, 1)
        my_chip = 2 * x + y
        others = (4, 2, 6)
        arrays = ((a_ref, stage_a, half_a, send_a, chips_a, oa_ref), (b_ref, stage_b, half_b, send_b, chips_b, ob_ref))
        to_sibling = []
        for t, (src, stage, half, _, _, _) in enumerate(arrays):
            for chip in range(4):
                stage[chip] = src[chip, 1 - c].astype(wire)
            cp = _remote(stage, half, send_sems.at[4 * t], recv_sems.at[4 * t], sibling)
            cp.start()
            to_sibling.append(cp)
        to_chips = []
        for t, (src, _, half, send, chips, _) in enumerate(arrays):
            to_sibling[t].wait_recv()
            chips[my_chip] = (src[my_chip, c] + half[my_chip].astype(F32)).astype(wire)
            for j, f in enumerate(others):
                px, py, _ = _flip(me, f)
                chip = 2 * px + py
                send[j] = (src[chip, c] + half[chip].astype(F32)).astype(wire)
                cp = _remote(send.at[j], chips.at[my_chip], send_sems.at[4 * t + 1 + j], recv_sems.at[4 * t + 1 + j], _flip(me, f))
                cp.start()
                to_chips.append(cp)
        for t, (_, _, _, _, chips, out) in enumerate(arrays):
            for j, f in enumerate(others):
                px, py, _ = _flip(me, f)
                slot = chips.at[2 * px + py]
                _remote(slot, slot, send_sems.at[4 * t + 1 + j], recv_sems.at[4 * t + 1 + j], _flip(me, f)).wait_recv()
            out[...] = ((chips[0].astype(F32) + chips[1].astype(F32)) + chips[2].astype(F32)) + chips[3].astype(F32)
        for cp in to_sibling + to_chips:
            cp.wait_send()

    def scratch(blocks):
        blk = blocks.shape[2:]
        return [pltpu.VMEM((4,) + blk, wire), pltpu.VMEM((4,) + blk, wire), pltpu.VMEM((3,) + blk, wire), pltpu.VMEM((4,) + blk, wire)]

    sa, sb = scratch(blocks_a), scratch(blocks_b)
    return pl.pallas_call(
        body,
        name=name,
        in_specs=[VMEM_SPEC, VMEM_SPEC],
        out_specs=[VMEM_SPEC, VMEM_SPEC],
        out_shape=[jax.ShapeDtypeStruct(blocks_a.shape[2:], F32), jax.ShapeDtypeStruct(blocks_b.shape[2:], F32)],
        scratch_shapes=[sa[0], sb[0], sa[1], sb[1], sa[2], sb[2], sa[3], sb[3], pltpu.SemaphoreType.DMA((8,)), pltpu.SemaphoreType.DMA((8,))],
        compiler_params=_params(),
    )(blocks_a, blocks_b)


def _pack_rows(parts):
    rows, offsets, at = [], [], 0
    for p in parts:
        flat = p.reshape(-1)
        n = -(-flat.shape[0] // (SUBLANES * LANES)) * SUBLANES
        rows.append(jnp.pad(flat, (0, n * LANES - flat.shape[0])).reshape(n, LANES))
        offsets.append(at)
        at += n
    return jnp.concatenate(rows, axis=0), offsets


def _unpack_rows(packed, offsets, shapes):
    out = []
    for off, shape in zip(offsets, shapes):
        size = 1
        for s in shape:
            size *= s
        n = -(-size // (SUBLANES * LANES)) * SUBLANES
        out.append(packed[off : off + n].reshape(-1)[:size].reshape(shape))
    return out


def kernel(x, c, w_ada, b_ada, norm_gain, w_in, q_gain, k_gain, sink, w_s, b_s, w_out, loss_target, m_w_ada, m_b_ada, m_norm_gain, m_w_in, m_q_gain, m_k_gain, m_sink, m_w_s, m_b_s, m_w_out, v_w_ada, v_b_ada, v_norm_gain, v_w_in, v_q_gain, v_k_gain, v_sink, v_w_s, v_b_s, v_w_out):
    seq, d = x.shape[1], x.shape[2]
    n_layers = w_in.shape[0]
    w_cols = w_in.shape[2]
    ada_cols = w_ada.shape[2]
    my = _index(_position())
    xs = x.reshape(seq, d)
    target = loss_target.reshape(seq, d)

    c_all, ada_parts = _ada_rows(c, w_ada)
    ada = ada_parts[:, :, 0, :].transpose(1, 0, 2).reshape(n_layers, 3 * d) + b_ada
    shift, scale1, gate = ada[:, None, 0:d], 1.0 + ada[:, None, d : 2 * d], ada[:, None, 2 * d : 3 * d]
    gain = norm_gain[:, None, :]

    w_in_t, w_out_full = _gather_weights(w_in.transpose(0, 2, 1), w_out)
    w_s_m = w_s.astype(MXU_DTYPE)
    w_s_t = w_s_m.transpose(0, 1, 3, 2)
    b_st = jnp.repeat(b_s.transpose(0, 2, 1), HEAD_DIM, axis=2)
    q_gain2 = jnp.tile(q_gain, (1, 2))[:, None, :]
    k_gain2 = jnp.tile(k_gain, (1, 2))[:, None, :]

    xl, saved = xs, []
    for l in range(n_layers):
        pa, pb = _ln_proj_fwd(xl, gain[l], scale1[l], shift[l], w_in_t, l, f"ln_proj_fwd_{l}")
        o = _attn_fwd(pa, q_gain2[l], k_gain2[l], sink[l], f"attn_fwd_{l}")
        x_next = _mix_out_fwd(pb, o, xl, gate[l], w_out_full, l, w_s_m[l], b_st[l], f"mix_out_fwd_{l}")
        saved.append((xl, pa, pb, o))
        xl = x_next
    dx, sq_err = _loss_grad(xl, target)
    loss = lax.psum(sq_err[0, 0] * (0.5 / d), ("x", "y", "c"))

    g_w_in, g_w_out, small, d_ada_rows = [None] * n_layers, [None] * n_layers, [None] * n_layers, [None] * n_layers
    for l in reversed(range(n_layers)):
        x_l, pa, pb, o = saved[l]
        dpb, do, g_acc, d_ws, d_bs = _mix_out_bwd(dx, pb, o, gate[l], w_out_full, l, w_s_m[l], w_s_t[l], b_st[l], f"mix_out_bwd_{l}")
        dq, dkv, halo_prev, halo_next, d_qg, d_kg, d_sk = _attn_bwd(pa, o, do, q_gain2[l], k_gain2[l], sink[l], f"attn_bwd_{l}")
        dx, dkvb, c0, c1 = _proj_bwd_dx(x_l, dx, dq, dkv, halo_prev, halo_next, dpb, w_in_t, l, gain[l], scale1[l], f"proj_bwd_dx_{l}")
        dw_in_t = _proj_bwd_dw(x_l, gain[l], scale1[l], shift[l], dq, dkvb, dpb, f"proj_bwd_dw_{l}")
        dw_out, d_gate8 = _w_out_finish(g_acc, w_out_full, l, gate[l], f"w_out_finish_{l}")
        r_in, r_out = _reduce_scatter(
            dw_in_t.reshape(4, 2, w_cols, d), dw_out.reshape(4, 2, D_MIX // N_DEV, d), f"reduce_scatter_{l}"
        )
        g_w_in[l] = r_in.transpose(1, 0)
        g_w_out[l] = r_out
        c0s, c1s = c0.sum(axis=0), c1.sum(axis=0)
        d_ada_rows[l] = jnp.concatenate([c0s, norm_gain[l] * c1s, d_gate8.sum(axis=0)])
        small[l] = (
            scale1[l, 0] * c1s,
            d_qg.sum(axis=0).reshape(N_HEADS, HEAD_DIM).sum(axis=0),
            d_kg.sum(axis=0).reshape(2, HEAD_DIM).sum(axis=0),
            d_sk[0, 0:N_HEADS],
            d_ws,
            d_bs.reshape(BLK, N_GROUPS, HEAD_DIM).sum(axis=2).transpose(1, 0),
        )

    names = ("norm_gain", "q_gain", "k_gain", "sink", "w_s", "b_s")
    stacked = [jnp.stack([small[l][t] for l in range(n_layers)]) for t in range(len(names))]
    d_ada = jnp.stack(d_ada_rows)
    packed, offsets = _pack_rows(stacked + [d_ada])
    gathered = _gather_small(packed)
    weights = (norm_gain, q_gain, k_gain, sink, w_s, b_s, b_ada)
    moments_m = (m_norm_gain, m_q_gain, m_k_gain, m_sink, m_w_s, m_b_s, m_b_ada)
    moments_v = (v_norm_gain, v_q_gain, v_k_gain, v_sink, v_w_s, v_b_s, v_b_ada)
    w_pack, _ = _pack_rows(weights)
    m_pack, _ = _pack_rows(moments_m)
    v_pack, _ = _pack_rows(moments_v)
    shapes = [w.shape for w in weights]
    g_small, d_small, m_small, v_small = (
        _unpack_rows(p, offsets, shapes) for p in _small_update(gathered, w_pack, m_pack, v_pack)
    )

    ada_off = offsets[-1]
    ada_n = -(-n_layers * 3 * d // (SUBLANES * LANES)) * SUBLANES
    d_ada_all = gathered[:, ada_off : ada_off + ada_n].reshape(N_DEV, -1)[:, : n_layers * 3 * d].reshape(N_DEV, n_layers, 3 * d)
    d_ada_cols = lax.dynamic_slice_in_dim(d_ada_all, my * ada_cols, ada_cols, axis=2)
    g_w_ada = _ada_weight_grad(c_all[:, 0, :], d_ada_cols.transpose(1, 0, 2))

    def update(w, g, m, v, name):
        shape = w.shape
        flat = lambda a: a.reshape(-1, shape[-1])
        return tuple(a.reshape(shape) for a in _adamw(flat(w), flat(g), flat(m), flat(v), name))

    g_w_in, g_w_out = jnp.stack(g_w_in), jnp.stack(g_w_out)
    upd_ada = update(w_ada, g_w_ada, m_w_ada, v_w_ada, "adamw_w_ada")
    upd_in = update(w_in, g_w_in, m_w_in, v_w_in, "adamw_w_in")
    upd_out = update(w_out, g_w_out, m_w_out, v_w_out, "adamw_w_out")

    def ordered(ada_, in_, out_, small_):
        ng, qg, kg, sk, ws, bs, ba = small_
        return (ada_, ba, ng, in_, qg, kg, sk, ws, bs, out_)

    grads = ordered(g_w_ada, g_w_in, g_w_out, g_small)
    deltas = ordered(upd_ada[0], upd_in[0], upd_out[0], d_small)
    new_m = ordered(upd_ada[1], upd_in[1], upd_out[1], m_small)
    new_v = ordered(upd_ada[2], upd_in[2], upd_out[2], v_small)
    return (loss, dx.reshape(x.shape), *grads, *deltas, *new_m, *new_v)
```

```python
import functools

import jax
import jax.numpy as jnp
from jax import lax
from jax.experimental import pallas as pl
from jax.experimental.pallas import tpu as pltpu

F32 = jnp.float32
MXU_DTYPE = jnp.bfloat16
MESH_ID = pl.DeviceIdType.MESH

N_DEV = 8
HEAD_DIM = 64
N_HEADS = 8
Q_PER_KV = 4
D_ATTN = 512
D_KV = 128
D_GM = 512
N_GROUPS = 8
D_MIX = D_ATTN + D_GM
BLK = 128
LANES = 128
SUBLANES = 8
N_PAIRS = D_ATTN // LANES
D_QKV = D_ATTN + 2 * D_KV
D_REST = D_ATTN + 3 * D_GM
D_IN = D_QKV + D_REST
EPS = 1e-6
NEG_INF = -1e30
ALIBI_SLOPES = tuple(2.0 ** (-8.0 * (h + 1) / N_HEADS) for h in range(N_HEADS))
Q_SCALE = 1.0 / 8.0

ADAM_LR = 0.001
ADAM_B1 = 0.9
ADAM_B2 = 0.999
ADAM_EPS = 1e-08
ADAM_WD = 0.01
ADAM_STEP = 10

TOKEN_TILE = 512
VMEM_LIMIT_BYTES = 56 * 1024 * 1024


def _params(semantics=None):
    return pltpu.CompilerParams(dimension_semantics=semantics, vmem_limit_bytes=VMEM_LIMIT_BYTES)


def _dot(a, b):
    return jnp.dot(a, b, preferred_element_type=F32)


def _dot_nt(a, b):
    return lax.dot_general(a, b, (((1,), (1,)), ((), ())), preferred_element_type=F32)


def _dot_tn(a, b):
    return lax.dot_general(a, b, (((0,), (0,)), ((), ())), preferred_element_type=F32)


def _mx(v):
    return v.astype(MXU_DTYPE)


def _lane_lo(rows):
    return lax.broadcasted_iota(jnp.int32, (rows, LANES), 1) < HEAD_DIM


def _half_ones():
    r = lax.broadcasted_iota(jnp.int32, (LANES, LANES), 0) < HEAD_DIM
    c = lax.broadcasted_iota(jnp.int32, (LANES, LANES), 1) < HEAD_DIM
    return jnp.where(r == c, 1.0, 0.0).astype(jnp.bfloat16)


def _half_sum(v, ones):
    p1 = v.astype(jnp.bfloat16)
    p2 = (v - p1.astype(F32)).astype(jnp.bfloat16)
    return _dot(p1, ones) + _dot(p2, ones)


def _half_rms(v, ones):
    r = lax.rsqrt(_half_sum(v * v, ones) * (1.0 / HEAD_DIM) + EPS)
    return v * r, r


def _half_rms_bwd(dy, vhat, r, ones):
    return r * (dy - vhat * (_half_sum(vhat * dy, ones) * (1.0 / HEAD_DIM)))


def _group_rows(v):
    rows, n = v.shape
    return v.reshape(rows // SUBLANES, SUBLANES, n).sum(axis=0)


def _sigmoid(v):
    return 1.0 / (1.0 + jnp.exp(-v))


ROW_CHUNK = 32
VARIANT_HEADS = ((0, 2, 5, 7), (1, 3, 4, 6))
HEAD_SLOT = {h: (v, s) for v, heads in enumerate(VARIANT_HEADS) for s, h in enumerate(heads)}
STACK = Q_PER_KV * BLK


def _fill_attn_bias(bias_s):
    qi = lax.broadcasted_iota(jnp.int32, (BLK, 3 * BLK), 0)
    ci = lax.broadcasted_iota(jnp.int32, (BLK, 3 * BLK), 1)
    dist = jnp.abs(ci - BLK - qi)
    distf = dist.astype(F32)
    for h in range(N_HEADS):
        bias_s[h] = jnp.where(dist <= BLK, -(ALIBI_SLOPES[h] * distf), NEG_INF)


def _edge_mask(block, seq):
    kpos = (block - 1) * BLK + lax.broadcasted_iota(jnp.int32, (1, 3 * BLK), 1)
    return jnp.where((kpos >= 0) & (kpos < seq), 0.0, NEG_INF)


def _stage_queries(qn, lo_t, j, nb, qs):
    for a in range(2):
        v, slot = HEAD_SLOT[2 * j + a]
        qm = _mx(jnp.where(lo_t, qn, 0.0) if a == 0 else jnp.where(lo_t, 0.0, qn))
        for n in range(nb):
            qs[n, v, slot * BLK : (slot + 1) * BLK, :] = qm[n * BLK : (n + 1) * BLK]


def _unstack_pair(stacked, j, lo):
    (v0, s0), (v1, s1) = HEAD_SLOT[2 * j], HEAD_SLOT[2 * j + 1]
    return jnp.where(lo, stacked[v0][s0 * BLK : (s0 + 1) * BLK], stacked[v1][s1 * BLK : (s1 + 1) * BLK])


def _stage_keys(kvp_ref, qkv_ref, kvn_ref, kg, ones, tile, ks, kr, vs, vr, khat_s=None, rk_s=None):
    pieces = (
        (0, BLK, kvp_ref[:, 0:D_KV], kvp_ref[:, D_KV : 2 * D_KV]),
        (BLK, tile, qkv_ref[:, D_ATTN : D_ATTN + D_KV], qkv_ref[:, D_ATTN + D_KV : D_QKV]),
        (BLK + tile, BLK, kvn_ref[:, 0:D_KV], kvn_ref[:, D_KV : 2 * D_KV]),
    )
    for r0, n, k, v in pieces:
        khat, rk = _half_rms(k, ones)
        kn = khat * kg
        ks[r0 : r0 + n, :] = _mx(kn)
        kr[r0 : r0 + n, :] = _mx(pltpu.roll(kn, HEAD_DIM, 1))
        vs[r0 : r0 + n, :] = _mx(v)
        vr[r0 : r0 + n, :] = _mx(pltpu.roll(v, HEAD_DIM, 1))
        if khat_s is not None:
            khat_s[r0 : r0 + n, :] = khat
            rk_s[r0 : r0 + n, :] = rk


def _halo_specs(tile, seq):
    nb = tile // BLK
    last = seq // BLK - 1
    kv_col = D_ATTN // (2 * D_KV)
    prev = pl.BlockSpec((BLK, 2 * D_KV), lambda i: (jnp.maximum(i * nb - 1, 0), kv_col))
    nxt = pl.BlockSpec((BLK, 2 * D_KV), lambda i: (jnp.minimum((i + 1) * nb, last), kv_col))
    return prev, nxt


def _row_spec(tile, width):
    return pl.BlockSpec((tile, width), lambda i: (i, 0))


def _full_spec(shape):
    nd = len(shape)
    return pl.BlockSpec(shape, lambda i: (0,) * nd)


SMEM_SPEC = pl.BlockSpec(memory_space=pltpu.SMEM)
VMEM_SPEC = pl.BlockSpec(memory_space=pltpu.VMEM)


def _layer_spec(layer, rows, cols):
    return pl.BlockSpec((None, rows, cols), lambda i: (layer, 0, 0))


def _ln_proj_fwd(x, gain, scale1, shift, w_in_t, layer, name):
    seq, d = x.shape
    tile = min(TOKEN_TILE, seq)

    def body(x_ref, g_ref, s1_ref, sh_ref, wt_ref, pa_ref, pb_ref):
        xv = x_ref[...]
        r = lax.rsqrt(jnp.mean(xv * xv, axis=-1, keepdims=True) + EPS)
        h = _mx((xv * r) * g_ref[...] * s1_ref[...] + sh_ref[...])
        pa_ref[...] = _dot_nt(h, wt_ref[0:D_QKV, :])
        pb_ref[...] = _dot_nt(h, wt_ref[D_QKV:D_IN, :])

    vec = _full_spec((1, d))
    return pl.pallas_call(
        body,
        name=name,
        grid=(seq // tile,),
        in_specs=[_row_spec(tile, d), vec, vec, vec, _layer_spec(layer, D_IN, d)],
        out_specs=[_row_spec(tile, D_QKV), _row_spec(tile, D_REST)],
        out_shape=[jax.ShapeDtypeStruct((seq, D_QKV), F32), jax.ShapeDtypeStruct((seq, D_REST), F32)],
        compiler_params=_params(("parallel",)),
    )(x, gain, scale1, shift, w_in_t)


def _attn_fwd(pa, q_gain2, k_gain2, sink, name):
    seq = pa.shape[0]
    tile = min(TOKEN_TILE, seq)
    nb = tile // BLK
    ext = tile + 2 * BLK

    def body(sink_ref, qkv_ref, kvp_ref, kvn_ref, qg_ref, kg_ref, o_ref, qs, ks, kr, vs, vr, bias_s, s_scr, p_scr, inv_scr):
        i = pl.program_id(0)

        @pl.when(i == 0)
        def _():
            _fill_attn_bias(bias_s)

        ones = _half_ones()
        lo = _lane_lo(BLK)
        lo_t = _lane_lo(tile)
        _stage_keys(kvp_ref, qkv_ref, kvn_ref, kg_ref[...], ones, tile, ks, kr, vs, vr)
        for j in range(N_PAIRS):
            qhat, _ = _half_rms(qkv_ref[:, j * LANES : (j + 1) * LANES], ones)
            _stage_queries(qhat * (qg_ref[...] * Q_SCALE), lo_t, j, nb, qs)

        def block(n, carry):
            r0 = pl.multiple_of(n * BLK, BLK)
            krows = pl.ds(r0, 3 * BLK)
            edge = _edge_mask(i * nb + n, seq)
            for v in range(2):
                s_scr[v] = _dot_nt(qs[n, v], (kr if v else ks)[krows, :])
            for h in range(N_HEADS):
                v, slot = HEAD_SLOT[h]
                sink_h = sink_ref[h]
                for rc in range(0, BLK, ROW_CHUNK):
                    rows = slice(slot * BLK + rc, slot * BLK + rc + ROW_CHUNK)
                    s = s_scr[v, rows, :] + bias_s[h, rc : rc + ROW_CHUNK, :] + edge
                    m = jnp.maximum(jnp.max(s, axis=-1, keepdims=True), sink_h)
                    p = jnp.exp(s - m)
                    total = jnp.sum(p, axis=-1, keepdims=True) + jnp.exp(sink_h - m)
                    p_scr[v, rows, :] = _mx(p)
                    inv_scr[v, rows, :] = jnp.broadcast_to(1.0 / total, (ROW_CHUNK, LANES))
            outs = [_dot(p_scr[v], (vr if v else vs)[krows, :]) * inv_scr[v] for v in range(2)]
            for j in range(N_PAIRS):
                o_ref[pl.ds(r0, BLK), j * LANES : (j + 1) * LANES] = _unstack_pair(outs, j, lo)
            return carry

        lax.fori_loop(0, nb, block, 0)

    prev, nxt = _halo_specs(tile, seq)
    vec = _full_spec((1, LANES))
    return pl.pallas_call(
        body,
        name=name,
        grid=(seq // tile,),
        in_specs=[SMEM_SPEC, _row_spec(tile, D_QKV), prev, nxt, vec, vec],
        out_specs=_row_spec(tile, D_ATTN),
        out_shape=jax.ShapeDtypeStruct((seq, D_ATTN), F32),
        scratch_shapes=[
            pltpu.VMEM((nb, 2, STACK, LANES), MXU_DTYPE),
            pltpu.VMEM((ext, LANES), MXU_DTYPE),
            pltpu.VMEM((ext, LANES), MXU_DTYPE),
            pltpu.VMEM((ext, LANES), MXU_DTYPE),
            pltpu.VMEM((ext, LANES), MXU_DTYPE),
            pltpu.VMEM((N_HEADS, BLK, 3 * BLK), F32),
            pltpu.VMEM((2, STACK, 3 * BLK), F32),
            pltpu.VMEM((2, STACK, 3 * BLK), MXU_DTYPE),
            pltpu.VMEM((2, STACK, LANES), F32),
        ],
        compiler_params=_params(("arbitrary",)),
    )(sink, pa, pa, pa, q_gain2, k_gain2)


def _mix_out_fwd(pb, o, x, gate, w_out, layer, w_s, b_st, name, target=None):
    seq, d = x.shape
    tile = min(TOKEN_TILE, seq)
    nb = tile // BLK
    with_loss = target is not None

    def body(pb_ref, o_ref, x_ref, gate_ref, wo_ref, ws_ref, bs_ref, *rest):
        if with_loss:
            t_ref, xo_ref, acc_ref, y_s, vn_s = rest

            @pl.when(pl.program_id(0) == 0)
            def _():
                acc_ref[...] = jnp.zeros_like(acc_ref)
        else:
            xo_ref, y_s, vn_s = rest
        ones = _half_ones()
        lo = _lane_lo(BLK)
        ga = pb_ref[:, 0:D_ATTN]
        y_s[:, 0:D_ATTN] = _mx(o_ref[...] * (ga * _sigmoid(ga)))
        for j in range(N_PAIRS):
            cols = slice(2 * D_GM + j * LANES, 2 * D_GM + (j + 1) * LANES)
            vhat, _ = _half_rms(pb_ref[:, cols], ones)
            vn_s[:, j * LANES : (j + 1) * LANES] = _mx(vhat)

        def chunk(n, carry):
            rows = pl.ds(pl.multiple_of(n * BLK, BLK), BLK)
            for j in range(N_PAIRS):
                cols = slice(j * LANES, (j + 1) * LANES)
                vn = vn_s[rows, cols]
                sv = jnp.where(lo, _dot(ws_ref[2 * j], vn), _dot(ws_ref[2 * j + 1], vn)) + bs_ref[:, cols]
                u = pb_ref[rows, D_ATTN + j * LANES : D_ATTN + (j + 1) * LANES]
                gg = pb_ref[rows, D_ATTN + 2 * D_GM + j * LANES : D_ATTN + 2 * D_GM + (j + 1) * LANES]
                y_s[rows, D_ATTN + j * LANES : D_ATTN + (j + 1) * LANES] = _mx((u * sv) * (gg * _sigmoid(gg)))
            return carry

        lax.fori_loop(0, nb, chunk, 0)
        y = x_ref[...] + gate_ref[...] * _dot(y_s[...], wo_ref[...])
        if with_loss:
            e = y - t_ref[...]
            xo_ref[...] = e * (1.0 / d)
            acc_ref[...] += jnp.sum(jnp.sum(e * e, axis=-1, keepdims=True), axis=0, keepdims=True)
        else:
            xo_ref[...] = y

    row = _row_spec(tile, d)
    acc_shape = (SUBLANES, LANES)
    return pl.pallas_call(
        body,
        name=name,
        grid=(seq // tile,),
        in_specs=[
            _row_spec(tile, D_REST),
            _row_spec(tile, D_ATTN),
            row,
            _full_spec((1, d)),
            _layer_spec(layer, D_MIX, d),
            _full_spec((N_GROUPS, BLK, BLK)),
            _full_spec((BLK, D_GM)),
        ]
        + ([row] if with_loss else []),
        out_specs=[row, _full_spec(acc_shape)] if with_loss else row,
        out_shape=[jax.ShapeDtypeStruct((seq, d), F32), jax.ShapeDtypeStruct(acc_shape, F32)]
        if with_loss
        else jax.ShapeDtypeStruct((seq, d), F32),
        scratch_shapes=[pltpu.VMEM((tile, D_MIX), MXU_DTYPE), pltpu.VMEM((tile, D_GM), MXU_DTYPE)],
        compiler_params=_params(("arbitrary",) if with_loss else ("parallel",)),
    )(pb, o, x, gate, w_out, w_s, b_st, *([target] if with_loss else []))


def _mix_out_bwd(dxn, pb, o, gate, w_out, layer, w_s, w_s_t, b_st, name):
    seq, d = dxn.shape
    tile = min(TOKEN_TILE, seq)
    nb = tile // BLK

    def body(dxn_ref, pb_ref, o_ref, gate_ref, wo_ref, ws_ref, wst_ref, bs_ref,
             dpb_ref, do_ref, g_ref, dws_ref, dbs_ref, y_s, dy_s, vn_s, rv_s):
        @pl.when(pl.program_id(0) == 0)
        def _():
            g_ref[...] = jnp.zeros_like(g_ref)
            dws_ref[...] = jnp.zeros_like(dws_ref)
            dbs_ref[...] = jnp.zeros_like(dbs_ref)

        ones = _half_ones()
        lo = _lane_lo(BLK)
        dxv = dxn_ref[...]
        dy_s[...] = _dot_nt(_mx(dxv * gate_ref[...]), wo_ref[...])
        ga = pb_ref[:, 0:D_ATTN]
        sig = _sigmoid(ga)
        sil = ga * sig
        ov = o_ref[...]
        y_s[:, 0:D_ATTN] = _mx(ov * sil)
        da = dy_s[:, 0:D_ATTN]
        do_ref[...] = da * sil
        dpb_ref[:, 0:D_ATTN] = (da * ov * (sig * (1.0 + ga * (1.0 - sig)))).astype(dpb_ref.dtype)
        for j in range(N_PAIRS):
            cols = slice(j * LANES, (j + 1) * LANES)
            vhat, rv = _half_rms(pb_ref[:, 2 * D_GM + j * LANES : 2 * D_GM + (j + 1) * LANES], ones)
            vn_s[:, cols] = vhat
            rv_s[:, cols] = rv

        def chunk(n, carry):
            rows = pl.ds(pl.multiple_of(n * BLK, BLK), BLK)
            for j in range(N_PAIRS):
                cols = slice(j * LANES, (j + 1) * LANES)
                c_u = slice(D_ATTN + j * LANES, D_ATTN + (j + 1) * LANES)
                c_vg = slice(D_ATTN + D_GM + j * LANES, D_ATTN + D_GM + (j + 1) * LANES)
                c_gg = slice(D_ATTN + 2 * D_GM + j * LANES, D_ATTN + 2 * D_GM + (j + 1) * LANES)
                vhat = vn_s[rows, cols]
                vn = _mx(vhat)
                sv = jnp.where(lo, _dot(ws_ref[2 * j], vn), _dot(ws_ref[2 * j + 1], vn)) + bs_ref[:, cols]
                u = pb_ref[rows, c_u]
                gg = pb_ref[rows, c_gg]
                sg = _sigmoid(gg)
                silg = gg * sg
                m0 = u * sv
                y_s[rows, c_u] = _mx(m0 * silg)
                dm = dy_s[rows, c_u]
                dm0 = dm * silg
                dpb_ref[rows, c_gg] = (dm * m0 * (sg * (1.0 + gg * (1.0 - sg)))).astype(dpb_ref.dtype)
                dpb_ref[rows, c_u] = (dm0 * sv).astype(dpb_ref.dtype)
                dsv = dm0 * u
                dbs_ref[:, cols] += dsv
                dws_ref[2 * j] += _dot_nt(_mx(jnp.where(lo, dsv, 0.0)), vn)
                dws_ref[2 * j + 1] += _dot_nt(_mx(jnp.where(lo, 0.0, dsv)), vn)
                dsv_m = _mx(dsv)
                dvn = jnp.where(lo, _dot(wst_ref[2 * j], dsv_m), _dot(wst_ref[2 * j + 1], dsv_m))
                dpb_ref[rows, c_vg] = _half_rms_bwd(dvn, vhat, rv_s[rows, cols], ones).astype(dpb_ref.dtype)
            return carry

        lax.fori_loop(0, nb, chunk, 0)
        g_ref[...] += _dot_tn(y_s[...], _mx(dxv))

    return pl.pallas_call(
        body,
        name=name,
        grid=(seq // tile,),
        in_specs=[
            _row_spec(tile, d),
            _row_spec(tile, D_REST),
            _row_spec(tile, D_ATTN),
            _full_spec((1, d)),
            _layer_spec(layer, D_MIX, d),
            _full_spec((N_GROUPS, BLK, BLK)),
            _full_spec((N_GROUPS, BLK, BLK)),
            _full_spec((BLK, D_GM)),
        ],
        out_specs=[
            _row_spec(tile, D_REST),
            _row_spec(tile, D_ATTN),
            _full_spec((D_MIX, d)),
            _full_spec((N_GROUPS, BLK, BLK)),
            _full_spec((BLK, D_GM)),
        ],
        out_shape=[
            jax.ShapeDtypeStruct((seq, D_REST), MXU_DTYPE),
            jax.ShapeDtypeStruct((seq, D_ATTN), F32),
            jax.ShapeDtypeStruct((D_MIX, d), F32),
            jax.ShapeDtypeStruct((N_GROUPS, BLK, BLK), F32),
            jax.ShapeDtypeStruct((BLK, D_GM), F32),
        ],
        scratch_shapes=[
            pltpu.VMEM((tile, D_MIX), MXU_DTYPE),
            pltpu.VMEM((tile, D_MIX), F32),
            pltpu.VMEM((tile, D_GM), F32),
            pltpu.VMEM((tile, D_GM), F32),
        ],
        compiler_params=_params(("arbitrary",)),
    )(dxn, pb, o, gate, w_out, w_s, w_s_t, b_st)


def _attn_bwd(pa, o, do, q_gain2, k_gain2, sink, name):
    seq = pa.shape[0]
    tile = min(TOKEN_TILE, seq)
    nb = tile // BLK
    nt = seq // tile
    ext = tile + 2 * BLK

    def body(sink_ref, qkv_ref, kvp_ref, kvn_ref, o_ref, do_ref, qg_ref, kg_ref,
             dq_ref, dkv_ref, hp_ref, hn_ref, dqg_ref, dkg_ref, dsk_ref,
             qs, dos, qhat_s, rq_s, ks, kr, vs, vr, khat_s, rk_s, dqn_s, dka, dva, bias_s, s_scr, dp_scr, p_scr, ds_scr):
        i = pl.program_id(0)

        @pl.when(i == 0)
        def _():
            dqg_ref[...] = jnp.zeros_like(dqg_ref)
            dkg_ref[...] = jnp.zeros_like(dkg_ref)
            dsk_ref[...] = jnp.zeros_like(dsk_ref)
            _fill_attn_bias(bias_s)

        ones = _half_ones()
        lo = _lane_lo(BLK)
        lo_t = _lane_lo(tile)
        lo_c = _lane_lo(ROW_CHUNK)
        qg = qg_ref[...] * Q_SCALE
        kg = kg_ref[...]
        _stage_keys(kvp_ref, qkv_ref, kvn_ref, kg, ones, tile, ks, kr, vs, vr, khat_s, rk_s)
        for j in range(N_PAIRS):
            cols = slice(j * LANES, (j + 1) * LANES)
            qhat, rq = _half_rms(qkv_ref[:, cols], ones)
            qhat_s[:, cols] = qhat
            rq_s[:, cols] = rq
            _stage_queries(qhat * qg, lo_t, j, nb, qs)
            _stage_queries(do_ref[:, cols], lo_t, j, nb, dos)
        dka[...] = jnp.zeros_like(dka)
        dva[...] = jnp.zeros_like(dva)
        head_lane = lax.broadcasted_iota(jnp.int32, (1, LANES), 1)

        def block(n, dsink):
            r0 = pl.multiple_of(n * BLK, BLK)
            krows = pl.ds(r0, 3 * BLK)
            edge = _edge_mask(i * nb + n, seq)
            for v in range(2):
                s_scr[v] = _dot_nt(qs[n, v], (kr if v else ks)[krows, :])
                dp_scr[v] = _dot_nt(dos[n, v], (vr if v else vs)[krows, :])
            for h in range(N_HEADS):
                v, slot = HEAD_SLOT[h]
                j, a = divmod(h, 2)
                cols = slice(j * LANES, (j + 1) * LANES)
                sink_h = sink_ref[h]
                sink_part = jnp.zeros((ROW_CHUNK, 1), F32)
                for rc in range(0, BLK, ROW_CHUNK):
                    rows = slice(slot * BLK + rc, slot * BLK + rc + ROW_CHUNK)
                    trows = pl.ds(pl.multiple_of(r0 + rc, ROW_CHUNK), ROW_CHUNK)
                    s = s_scr[v, rows, :] + bias_s[h, rc : rc + ROW_CHUNK, :] + edge
                    m = jnp.maximum(jnp.max(s, axis=-1, keepdims=True), sink_h)
                    p = jnp.exp(s - m)
                    e_sink = jnp.exp(sink_h - m)
                    inv = 1.0 / (jnp.sum(p, axis=-1, keepdims=True) + e_sink)
                    pn = p * inv
                    prod = do_ref[trows, cols] * o_ref[trows, cols]
                    prod = jnp.where(lo_c, prod, 0.0) if a == 0 else jnp.where(lo_c, 0.0, prod)
                    dcol = jnp.sum(prod, axis=-1, keepdims=True)
                    ds_scr[v, rows, :] = _mx(pn * (dp_scr[v, rows, :] - dcol))
                    p_scr[v, rows, :] = _mx(pn)
                    sink_part = sink_part + (e_sink * inv) * dcol
                dsink = dsink - jnp.where(head_lane == h, jnp.sum(sink_part, axis=0, keepdims=True), 0.0)
            dqv = []
            for v in range(2):
                dqv.append(_dot(ds_scr[v], (kr if v else ks)[krows, :]))
                dka[v, krows, :] += _dot_tn(ds_scr[v], qs[n, v])
                dva[v, krows, :] += _dot_tn(p_scr[v], dos[n, v])
            for j in range(N_PAIRS):
                dqn_s[pl.ds(r0, BLK), j * LANES : (j + 1) * LANES] = _unstack_pair(dqv, j, lo)
            return dsink

        dsink = lax.fori_loop(0, nb, block, jnp.zeros((1, LANES), F32))
        dsk_ref[...] += jnp.broadcast_to(dsink, (SUBLANES, LANES))
        for j in range(N_PAIRS):
            cols = slice(j * LANES, (j + 1) * LANES)
            dqn = dqn_s[:, cols]
            qhat = qhat_s[:, cols]
            dqg_ref[:, cols] += _group_rows(dqn * qhat) * Q_SCALE
            dq_ref[:, cols] = _half_rms_bwd(dqn * qg, qhat, rq_s[:, cols], ones).astype(dq_ref.dtype)
        dkn = dka[0] + pltpu.roll(dka[1], HEAD_DIM, 1)
        khat = khat_s[...]
        dkg_ref[...] += _group_rows(dkn * khat)
        dk = _half_rms_bwd(dkn * kg, khat, rk_s[...], ones)
        dv = dva[0] + pltpu.roll(dva[1], HEAD_DIM, 1)
        hp_ref[:, 0:D_KV] = dk[0:BLK]
        hp_ref[:, D_KV : 2 * D_KV] = dv[0:BLK]
        dkv_ref[:, 0:D_KV] = dk[BLK : BLK + tile]
        dkv_ref[:, D_KV : 2 * D_KV] = dv[BLK : BLK + tile]
        hn_ref[:, 0:D_KV] = dk[BLK + tile : ext]
        hn_ref[:, D_KV : 2 * D_KV] = dv[BLK + tile : ext]

    prev, nxt = _halo_specs(tile, seq)
    vec = _full_spec((1, LANES))
    halo = pl.BlockSpec((None, BLK, 2 * D_KV), lambda i: (i, 0, 0))
    return pl.pallas_call(
        body,
        name=name,
        grid=(nt,),
        in_specs=[SMEM_SPEC, _row_spec(tile, D_QKV), prev, nxt, _row_spec(tile, D_ATTN), _row_spec(tile, D_ATTN), vec, vec],
        out_specs=[
            _row_spec(tile, D_ATTN),
            _row_spec(tile, 2 * D_KV),
            halo,
            halo,
            _full_spec((SUBLANES, D_ATTN)),
            _full_spec((SUBLANES, LANES)),
            _full_spec((SUBLANES, LANES)),
        ],
        out_shape=[
            jax.ShapeDtypeStruct((seq, D_ATTN), MXU_DTYPE),
            jax.ShapeDtypeStruct((seq, 2 * D_KV), F32),
            jax.ShapeDtypeStruct((nt, BLK, 2 * D_KV), F32),
            jax.ShapeDtypeStruct((nt, BLK, 2 * D_KV), F32),
            jax.ShapeDtypeStruct((SUBLANES, D_ATTN), F32),
            jax.ShapeDtypeStruct((SUBLANES, LANES), F32),
            jax.ShapeDtypeStruct((SUBLANES, LANES), F32),
        ],
        scratch_shapes=[
            pltpu.VMEM((nb, 2, STACK, LANES), MXU_DTYPE),
            pltpu.VMEM((nb, 2, STACK, LANES), MXU_DTYPE),
            pltpu.VMEM((tile, D_ATTN), F32),
            pltpu.VMEM((tile, D_ATTN), F32),
            pltpu.VMEM((ext, LANES), MXU_DTYPE),
            pltpu.VMEM((ext, LANES), MXU_DTYPE),
            pltpu.VMEM((ext, LANES), MXU_DTYPE),
            pltpu.VMEM((ext, LANES), MXU_DTYPE),
            pltpu.VMEM((ext, LANES), F32),
            pltpu.VMEM((ext, LANES), F32),
            pltpu.VMEM((tile, D_ATTN), F32),
            pltpu.VMEM((2, ext, LANES), F32),
            pltpu.VMEM((2, ext, LANES), F32),
            pltpu.VMEM((N_HEADS, BLK, 3 * BLK), F32),
            pltpu.VMEM((2, STACK, 3 * BLK), F32),
            pltpu.VMEM((2, STACK, 3 * BLK), F32),
            pltpu.VMEM((2, STACK, 3 * BLK), MXU_DTYPE),
            pltpu.VMEM((2, STACK, 3 * BLK), MXU_DTYPE),
        ],
        compiler_params=_params(("arbitrary",)),
    )(sink, pa, pa, pa, o, do, q_gain2, k_gain2)


def _halo_in_specs(tile, nt):
    from_prev = pl.BlockSpec((None, BLK, 2 * D_KV), lambda i: (jnp.maximum(i - 1, 0), 0, 0))
    from_next = pl.BlockSpec((None, BLK, 2 * D_KV), lambda i: (jnp.minimum(i + 1, nt - 1), 0, 0))
    return from_prev, from_next


def _proj_bwd_dx(x, dxn, dq, dkv, halo_prev, halo_next, dpb, w_in_t, layer, gain, scale1, name):
    seq, d = x.shape
    tile = min(TOKEN_TILE, seq)
    nt = seq // tile

    def body(x_ref, dxn_ref, dq_ref, dkv_ref, hn_ref, hp_ref, dpb_ref, wt_ref, g_ref, s1_ref,
             dx_ref, dkvb_ref, c0_ref, c1_ref):
        i = pl.program_id(0)

        @pl.when(i == 0)
        def _():
            c0_ref[...] = jnp.zeros_like(c0_ref)
            c1_ref[...] = jnp.zeros_like(c1_ref)

        top = dkv_ref[0:BLK, :] + jnp.where(i > 0, hn_ref[...], 0.0)
        bot = dkv_ref[tile - BLK : tile, :] + jnp.where(i < nt - 1, hp_ref[...], 0.0)
        if tile == BLK:
            dkvb_ref[...] = (top + bot - dkv_ref[...]).astype(dkvb_ref.dtype)
        else:
            dkvb_ref[0:BLK, :] = top.astype(dkvb_ref.dtype)
            dkvb_ref[tile - BLK : tile, :] = bot.astype(dkvb_ref.dtype)
            if tile > 2 * BLK:
                dkvb_ref[BLK : tile - BLK, :] = dkv_ref[BLK : tile - BLK, :].astype(dkvb_ref.dtype)
        dh = (
            _dot(dq_ref[...], wt_ref[0:D_ATTN, :])
            + _dot(dkvb_ref[...], wt_ref[D_ATTN:D_QKV, :])
            + _dot(dpb_ref[...], wt_ref[D_QKV:D_IN, :])
        )
        xv = x_ref[...]
        r = lax.rsqrt(jnp.mean(xv * xv, axis=-1, keepdims=True) + EPS)
        xn = xv * r
        c0_ref[...] += _group_rows(dh)
        c1_ref[...] += _group_rows(dh * xn)
        dxn_ = dh * (g_ref[...] * s1_ref[...])
        dx_ref[...] = dxn_ref[...] + r * (dxn_ - xn * jnp.mean(xn * dxn_, axis=-1, keepdims=True))

    from_prev, from_next = _halo_in_specs(tile, nt)
    vec = _full_spec((1, d))
    return pl.pallas_call(
        body,
        name=name,
        grid=(nt,),
        in_specs=[
            _row_spec(tile, d),
            _row_spec(tile, d),
            _row_spec(tile, D_ATTN),
            _row_spec(tile, 2 * D_KV),
            from_prev,
            from_next,
            _row_spec(tile, D_REST),
            _layer_spec(layer, D_IN, d),
            vec,
            vec,
        ],
        out_specs=[_row_spec(tile, d), _row_spec(tile, 2 * D_KV), _full_spec((SUBLANES, d)), _full_spec((SUBLANES, d))],
        out_shape=[
            jax.ShapeDtypeStruct((seq, d), F32),
            jax.ShapeDtypeStruct((seq, 2 * D_KV), MXU_DTYPE),
            jax.ShapeDtypeStruct((SUBLANES, d), F32),
            jax.ShapeDtypeStruct((SUBLANES, d), F32),
        ],
        compiler_params=_params(("arbitrary",)),
    )(x, dxn, dq, dkv, halo_next, halo_prev, dpb, w_in_t, gain, scale1)


def _proj_bwd_dw(x, gain, scale1, shift, dq, dkvb, dpb, name):
    seq, d = x.shape
    tile = min(TOKEN_TILE, seq)

    def body(x_ref, g_ref, s1_ref, sh_ref, dq_ref, dkv_ref, dpb_ref, dw_ref):
        @pl.when(pl.program_id(0) == 0)
        def _():
            dw_ref[...] = jnp.zeros_like(dw_ref)

        xv = x_ref[...]
        r = lax.rsqrt(jnp.mean(xv * xv, axis=-1, keepdims=True) + EPS)
        h = _mx((xv * r) * g_ref[...] * s1_ref[...] + sh_ref[...])
        dw_ref[0:D_ATTN, :] += _dot_tn(dq_ref[...], h)
        dw_ref[D_ATTN:D_QKV, :] += _dot_tn(dkv_ref[...], h)
        dw_ref[D_QKV:D_IN, :] += _dot_tn(dpb_ref[...], h)

    vec = _full_spec((1, d))
    return pl.pallas_call(
        body,
        name=name,
        grid=(seq // tile,),
        in_specs=[_row_spec(tile, d), vec, vec, vec, _row_spec(tile, D_ATTN), _row_spec(tile, 2 * D_KV), _row_spec(tile, D_REST)],
        out_specs=_full_spec((D_IN, d)),
        out_shape=jax.ShapeDtypeStruct((D_IN, d), F32),
        compiler_params=_params(("arbitrary",)),
    )(x, gain, scale1, shift, dq, dkvb, dpb)


def _w_out_finish(g, w_out, layer, gate, name):
    d_mix, d = g.shape

    def body(g_ref, w_ref, gate_ref, dw_ref, dgate_ref):
        gv = g_ref[...]
        dw_ref[...] = gv * gate_ref[...]
        dgate_ref[...] = _group_rows(gv * w_ref[layer].astype(F32))

    return pl.pallas_call(
        body,
        name=name,
        in_specs=[VMEM_SPEC, VMEM_SPEC, VMEM_SPEC],
        out_specs=[VMEM_SPEC, VMEM_SPEC],
        out_shape=[jax.ShapeDtypeStruct((d_mix, d), F32), jax.ShapeDtypeStruct((SUBLANES, d), F32)],
        compiler_params=_params(),
    )(g, w_out, gate)


def _adamw_math(w, g, m, v):
    m = ADAM_B1 * m + (1.0 - ADAM_B1) * g
    v = ADAM_B2 * v + (1.0 - ADAM_B2) * (g * g)
    m_hat = m / (1.0 - ADAM_B1**ADAM_STEP)
    v_hat = v / (1.0 - ADAM_B2**ADAM_STEP)
    delta = -ADAM_LR * (m_hat / (jnp.sqrt(v_hat) + ADAM_EPS) + ADAM_WD * w)
    return delta, m, v


def _adamw(w, g, m, v, name):
    rows, cols = w.shape
    tile = min(TOKEN_TILE, rows)

    def body(w_ref, g_ref, m_ref, v_ref, d_ref, mo_ref, vo_ref):
        d_ref[...], mo_ref[...], vo_ref[...] = _adamw_math(w_ref[...], g_ref[...], m_ref[...], v_ref[...])

    spec = _row_spec(tile, cols)
    shape = jax.ShapeDtypeStruct((rows, cols), F32)
    return pl.pallas_call(
        body,
        name=name,
        grid=(rows // tile,),
        in_specs=[spec] * 4,
        out_specs=[spec] * 3,
        out_shape=[shape] * 3,
        compiler_params=_params(("parallel",)),
    )(w, g, m, v)


def _small_update(gathered, gathered_ws, w, m, v, ws, m_ws, v_ws):
    def body(ga_ref, gws_ref, w_ref, m_ref, v_ref, ws_ref, mws_ref, vws_ref, *outs):
        for src, refs, out in ((ga_ref, (w_ref, m_ref, v_ref), outs[0:4]), (gws_ref, (ws_ref, mws_ref, vws_ref), outs[4:8])):
            g = src[0].astype(F32)
            for j in range(1, N_DEV):
                g = g + src[j].astype(F32)
            out[0][...] = g
            out[1][...], out[2][...], out[3][...] = _adamw_math(refs[0][...], g, refs[1][...], refs[2][...])

    shapes = [jax.ShapeDtypeStruct(w.shape, F32)] * 4 + [jax.ShapeDtypeStruct(ws.shape, F32)] * 4
    return pl.pallas_call(
        body,
        name="small_update",
        in_specs=[VMEM_SPEC] * 8,
        out_specs=[VMEM_SPEC] * 8,
        out_shape=shapes,
        compiler_params=_params(),
    )(gathered, gathered_ws, w, m, v, ws, m_ws, v_ws)


def _ada_weight_grad(c_all, d_ada_cols):
    d = c_all.shape[-1]
    n_layers, _, width = d_ada_cols.shape

    def body(c_ref, da_ref, dw_ref):
        cv = c_ref[...]
        cond = cv * _sigmoid(cv)
        for l in range(n_layers):
            dw_ref[l] = lax.dot_general(
                cond, da_ref[l], (((0,), (0,)), ((), ())), preferred_element_type=F32, precision=lax.Precision.HIGHEST
            )

    return pl.pallas_call(
        body,
        name="ada_weight_grad",
        in_specs=[VMEM_SPEC, VMEM_SPEC],
        out_specs=VMEM_SPEC,
        out_shape=jax.ShapeDtypeStruct((n_layers, d, width), F32),
        compiler_params=_params(),
    )(c_all, d_ada_cols)


def _position():
    return lax.axis_index("x"), lax.axis_index("y"), lax.axis_index("c")


def _flip(pos, k):
    x, y, c = pos
    return (1 - x if k & 4 else x, 1 - y if k & 2 else y, 1 - c if k & 1 else c)


def _index(pos):
    x, y, c = pos
    return 4 * x + 2 * y + c


def _remote(src, dst, send_sem, recv_sem, to):
    return pltpu.make_async_remote_copy(
        src_ref=src, dst_ref=dst, send_sem=send_sem, recv_sem=recv_sem, device_id=to, device_id_type=MESH_ID
    )


def _two_level_all_gather(slots, send_sems, recv_sems, between=None):
    me = _position()
    sibling = _flip(me, 1)
    others = (4, 2, 6)

    def copy(t, k, block, to):
        slot = slots[t](_index(block))
        return _remote(slot, slot, send_sems.at[7 * t + k], recv_sems.at[7 * t + k], to)

    started = []
    for t in range(len(slots)):
        started.append(copy(t, 0, me, sibling))
        started += [copy(t, 1 + j, me, _flip(me, f)) for j, f in enumerate(others)]
    for cp in started:
        cp.start()
    if between is not None:
        between()
    for j, f in enumerate(others):
        for t in range(len(slots)):
            copy(t, 1 + j, _flip(me, f), me).wait_recv()
            passed = copy(t, 4 + j, _flip(me, f), sibling)
            passed.start()
            started.append(passed)
    for t in range(len(slots)):
        copy(t, 0, sibling, me).wait_recv()
        for j, f in enumerate(others):
            copy(t, 4 + j, _flip(sibling, f), me).wait_recv()
    for cp in started:
        cp.wait_send()


def _row_block(ref, rows):
    return lambda j: ref.at[:, pl.ds(pl.multiple_of(j * rows, 16), rows), :]


def _ada_exchange(c_ref, w_ref, call_ref, parts_ref, sbuf, sem_s1, sem_r1, sem_s2, sem_r2):
    d = c_ref.shape[-1]
    n_layers = w_ref.shape[0]
    me = _position()
    my = _index(me)
    call_ref[my] = jnp.broadcast_to(c_ref[...], (SUBLANES, d))
    mine = call_ref.at[my]
    first = [_remote(mine, mine, sem_s1.at[k - 1], sem_r1.at[k - 1], _flip(me, k)) for k in range(1, N_DEV)]
    for cp in first:
        cp.start()
    for k in range(1, N_DEV):
        theirs = call_ref.at[_index(_flip(me, k))]
        _remote(theirs, theirs, sem_s1.at[k - 1], sem_r1.at[k - 1], _flip(me, k)).wait_recv()
    for b in range(N_DEV):
        cv = call_ref[b]
        cond = cv * _sigmoid(cv)
        for l in range(n_layers):
            sbuf[b, l] = jnp.dot(cond, w_ref[l], preferred_element_type=F32, precision=lax.Precision.HIGHEST)
    parts_ref[my] = sbuf[my]
    second = []
    for k in range(1, N_DEV):
        to = _flip(me, k)
        second.append(_remote(sbuf.at[_index(to)], parts_ref.at[my], sem_s2.at[k - 1], sem_r2.at[k - 1], to))
    for cp in second:
        cp.start()
    for k in range(1, N_DEV):
        theirs = parts_ref.at[_index(_flip(me, k))]
        _remote(theirs, theirs, sem_s2.at[k - 1], sem_r2.at[k - 1], _flip(me, k)).wait_recv()
    for cp in first + second:
        cp.wait_send()


def _gather_weights(w_in_t, w_out, c_row, w_ada):
    n_layers, rows_in, d = w_in_t.shape
    rows_out = w_out.shape[1]
    width = w_ada.shape[2]

    def body(wi_ref, wo_ref, c_ref, wa_ref, gi_ref, go_ref, call_ref, parts_ref, sbuf, send_sems, recv_sems, *ada_sems):
        my = _index(_position())
        gi_ref[:, pl.ds(pl.multiple_of(my * rows_in, 16), rows_in), :] = wi_ref[...].astype(gi_ref.dtype)
        go_ref[:, pl.ds(pl.multiple_of(my * rows_out, 16), rows_out), :] = wo_ref[...].astype(go_ref.dtype)
        _two_level_all_gather(
            (_row_block(gi_ref, rows_in), _row_block(go_ref, rows_out)),
            send_sems,
            recv_sems,
            between=functools.partial(_ada_exchange, c_ref, wa_ref, call_ref, parts_ref, sbuf, *ada_sems),
        )

    return pl.pallas_call(
        body,
        name="gather_weights",
        in_specs=[VMEM_SPEC] * 4,
        out_specs=[VMEM_SPEC] * 4,
        out_shape=[
            jax.ShapeDtypeStruct((n_layers, N_DEV * rows_in, d), MXU_DTYPE),
            jax.ShapeDtypeStruct((n_layers, N_DEV * rows_out, d), MXU_DTYPE),
            jax.ShapeDtypeStruct((N_DEV, SUBLANES, d), F32),
            jax.ShapeDtypeStruct((N_DEV, n_layers, SUBLANES, width), F32),
        ],
        scratch_shapes=[
            pltpu.VMEM((N_DEV, n_layers, SUBLANES, width), F32),
            pltpu.SemaphoreType.DMA((14,)),
            pltpu.SemaphoreType.DMA((14,)),
        ]
        + [pltpu.SemaphoreType.DMA((N_DEV - 1,))] * 4,
        compiler_params=_params(),
    )(w_in_t, w_out, c_row, w_ada)


def _gather_small(packed, d_ws):
    def body(p_ref, ws_ref, g_ref, gws_ref, send_sems, recv_sems):
        my = _index(_position())
        g_ref[my] = p_ref[...]
        gws_ref[my] = ws_ref[...].astype(gws_ref.dtype)
        _two_level_all_gather((lambda j: g_ref.at[j], lambda j: gws_ref.at[j]), send_sems, recv_sems)

    return pl.pallas_call(
        body,
        name="gather_small",
        in_specs=[VMEM_SPEC, VMEM_SPEC],
        out_specs=[VMEM_SPEC, VMEM_SPEC],
        out_shape=[
            jax.ShapeDtypeStruct((N_DEV,) + packed.shape, F32),
            jax.ShapeDtypeStruct((N_DEV,) + d_ws.shape, jnp.bfloat16),
        ],
        scratch_shapes=[pltpu.SemaphoreType.DMA((14,)), pltpu.SemaphoreType.DMA((14,))],
        compiler_params=_params(),
    )(packed, d_ws)


def _reduce_scatter(blocks_a, blocks_b, name):
    wire = jnp.bfloat16

    def body(a_ref, b_ref, oa_ref, ob_ref, stage_a, stage_b, half_a, half_b, send_a, send_b, chips_a, chips_b, send_sems, recv_sems):
        me = _position()
        x, y, c = me
        sibling = _flip(me, 1)
        my_chip = 2 * x + y
        others = (4, 2, 6)
        arrays = ((a_ref, stage_a, half_a, send_a, chips_a, oa_ref), (b_ref, stage_b, half_b, send_b, chips_b, ob_ref))
        to_sibling = []
        for t, (src, stage, half, _, _, _) in enumerate(arrays):
            for chip in range(4):
                stage[chip] = src[chip, 1 - c].astype(wire)
            cp = _remote(stage, half, send_sems.at[4 * t], recv_sems.at[4 * t], sibling)
            cp.start()
            to_sibling.append(cp)
        to_chips = []
        for t, (src, _, half, send, chips, _) in enumerate(arrays):
            to_sibling[t].wait_recv()
            chips[my_chip] = (src[my_chip, c] + half[my_chip].astype(F32)).astype(wire)
            for j, f in enumerate(others):
                px, py, _ = _flip(me, f)
                chip = 2 * px + py
                send[j] = (src[chip, c] + half[chip].astype(F32)).astype(wire)
                cp = _remote(send.at[j], chips.at[my_chip], send_sems.at[4 * t + 1 + j], recv_sems.at[4 * t + 1 + j], _flip(me, f))
                cp.start()
                to_chips.append(cp)
        for t, (_, _, _, _, chips, out) in enumerate(arrays):
            for j, f in enumerate(others):
                px, py, _ = _flip(me, f)
                slot = chips.at[2 * px + py]
                _remote(slot, slot, send_sems.at[4 * t + 1 + j], recv_sems.at[4 * t + 1 + j], _flip(me, f)).wait_recv()
            out[...] = ((chips[0].astype(F32) + chips[1].astype(F32)) + chips[2].astype(F32)) + chips[3].astype(F32)
        for cp in to_sibling + to_chips:
            cp.wait_send()

    def scratch(blocks):
        blk = blocks.shape[2:]
        return [pltpu.VMEM((4,) + blk, wire), pltpu.VMEM((4,) + blk, wire), pltpu.VMEM((3,) + blk, wire), pltpu.VMEM((4,) + blk, wire)]

    sa, sb = scratch(blocks_a), scratch(blocks_b)
    return pl.pallas_call(
        body,
        name=name,
        in_specs=[VMEM_SPEC, VMEM_SPEC],
        out_specs=[VMEM_SPEC, VMEM_SPEC],
        out_shape=[jax.ShapeDtypeStruct(blocks_a.shape[2:], F32), jax.ShapeDtypeStruct(blocks_b.shape[2:], F32)],
        scratch_shapes=[sa[0], sb[0], sa[1], sb[1], sa[2], sb[2], sa[3], sb[3], pltpu.SemaphoreType.DMA((8,)), pltpu.SemaphoreType.DMA((8,))],
        compiler_params=_params(),
    )(blocks_a, blocks_b)


def _pack_rows(parts):
    rows, offsets, at = [], [], 0
    for p in parts:
        flat = p.reshape(-1)
        n = -(-flat.shape[0] // (SUBLANES * LANES)) * SUBLANES
        rows.append(jnp.pad(flat, (0, n * LANES - flat.shape[0])).reshape(n, LANES))
        offsets.append(at)
        at += n
    return jnp.concatenate(rows, axis=0), offsets


def _unpack_rows(packed, offsets, shapes):
    out = []
    for off, shape in zip(offsets, shapes):
        size = 1
        for s in shape:
            size *= s
        n = -(-size // (SUBLANES * LANES)) * SUBLANES
        out.append(packed[off : off + n].reshape(-1)[:size].reshape(shape))
    return out


def kernel(x, c, w_ada, b_ada, norm_gain, w_in, q_gain, k_gain, sink, w_s, b_s, w_out, loss_target, m_w_ada, m_b_ada, m_norm_gain, m_w_in, m_q_gain, m_k_gain, m_sink, m_w_s, m_b_s, m_w_out, v_w_ada, v_b_ada, v_norm_gain, v_w_in, v_q_gain, v_k_gain, v_sink, v_w_s, v_b_s, v_w_out):
    seq, d = x.shape[1], x.shape[2]
    n_layers = w_in.shape[0]
    w_cols = w_in.shape[2]
    ada_cols = w_ada.shape[2]
    my = _index(_position())
    xs = x.reshape(seq, d)
    target = loss_target.reshape(seq, d)

    w_in_t, w_out_full, c_all, ada_parts = _gather_weights(w_in.transpose(0, 2, 1), w_out, c, w_ada)
    ada = ada_parts[:, :, 0, :].transpose(1, 0, 2).reshape(n_layers, 3 * d) + b_ada
    shift, scale1, gate = ada[:, None, 0:d], 1.0 + ada[:, None, d : 2 * d], ada[:, None, 2 * d : 3 * d]
    gain = norm_gain[:, None, :]

    w_s_m = w_s.astype(MXU_DTYPE)
    w_s_t = w_s_m.transpose(0, 1, 3, 2)
    b_st = jnp.repeat(b_s.transpose(0, 2, 1), HEAD_DIM, axis=2)
    q_gain2 = jnp.tile(q_gain, (1, 2))[:, None, :]
    k_gain2 = jnp.tile(k_gain, (1, 2))[:, None, :]

    xl, saved = xs, []
    for l in range(n_layers):
        pa, pb = _ln_proj_fwd(xl, gain[l], scale1[l], shift[l], w_in_t, l, f"ln_proj_fwd_{l}")
        o = _attn_fwd(pa, q_gain2[l], k_gain2[l], sink[l], f"attn_fwd_{l}")
        saved.append((xl, pa, pb, o))
        last = l == n_layers - 1
        out = _mix_out_fwd(pb, o, xl, gate[l], w_out_full, l, w_s_m[l], b_st[l], f"mix_out_fwd_{l}", target if last else None)
        if last:
            dx, sq_err = out
        else:
            xl = out

    g_w_in, g_w_out, small, d_ada_rows = [None] * n_layers, [None] * n_layers, [None] * n_layers, [None] * n_layers
    for l in reversed(range(n_layers)):
        x_l, pa, pb, o = saved[l]
        dpb, do, g_acc, d_ws, d_bs = _mix_out_bwd(dx, pb, o, gate[l], w_out_full, l, w_s_m[l], w_s_t[l], b_st[l], f"mix_out_bwd_{l}")
        dq, dkv, halo_prev, halo_next, d_qg, d_kg, d_sk = _attn_bwd(pa, o, do, q_gain2[l], k_gain2[l], sink[l], f"attn_bwd_{l}")
        dx, dkvb, c0, c1 = _proj_bwd_dx(x_l, dx, dq, dkv, halo_prev, halo_next, dpb, w_in_t, l, gain[l], scale1[l], f"proj_bwd_dx_{l}")
        dw_in_t = _proj_bwd_dw(x_l, gain[l], scale1[l], shift[l], dq, dkvb, dpb, f"proj_bwd_dw_{l}")
        dw_out, d_gate8 = _w_out_finish(g_acc, w_out_full, l, gate[l], f"w_out_finish_{l}")
        r_in, r_out = _reduce_scatter(
            dw_in_t.reshape(4, 2, w_cols, d), dw_out.reshape(4, 2, D_MIX // N_DEV, d), f"reduce_scatter_{l}"
        )
        g_w_in[l] = r_in.transpose(1, 0)
        g_w_out[l] = r_out
        c0s, c1s = c0.sum(axis=0), c1.sum(axis=0)
        d_ada_rows[l] = jnp.concatenate([c0s, norm_gain[l] * c1s, d_gate8.sum(axis=0)])
        small[l] = (
            scale1[l, 0] * c1s,
            d_qg.sum(axis=0).reshape(N_HEADS, HEAD_DIM).sum(axis=0),
            d_kg.sum(axis=0).reshape(2, HEAD_DIM).sum(axis=0),
            d_sk[0, 0:N_HEADS],
            d_bs.reshape(BLK, N_GROUPS, HEAD_DIM).sum(axis=2).transpose(1, 0),
            d_ws,
        )

    names = ("norm_gain", "q_gain", "k_gain", "sink", "b_s")
    stacked = [jnp.stack([small[l][t] for l in range(n_layers)]) for t in range(len(names))]
    d_ada = jnp.stack(d_ada_rows)
    packed, offsets = _pack_rows(stacked + [d_ada, sq_err[0, 0:1]])
    d_ws = jnp.stack([small[l][len(names)] for l in range(n_layers)]).reshape(-1, LANES)
    gathered, gathered_ws = _gather_small(packed, d_ws)
    no_weight = jnp.zeros((1,), F32)
    weights = (norm_gain, q_gain, k_gain, sink, b_s, b_ada, no_weight)
    moments_m = (m_norm_gain, m_q_gain, m_k_gain, m_sink, m_b_s, m_b_ada, no_weight)
    moments_v = (v_norm_gain, v_q_gain, v_k_gain, v_sink, v_b_s, v_b_ada, no_weight)
    w_pack, _ = _pack_rows(weights)
    m_pack, _ = _pack_rows(moments_m)
    v_pack, _ = _pack_rows(moments_v)
    shapes = [w.shape for w in weights]
    flat_ws = lambda a: a.reshape(-1, LANES)
    updated = _small_update(gathered, gathered_ws, w_pack, m_pack, v_pack, flat_ws(w_s), flat_ws(m_w_s), flat_ws(v_w_s))
    g_small, d_small, m_small, v_small = (_unpack_rows(p, offsets, shapes) for p in updated[0:4])
    ws_small = [p.reshape(w_s.shape) for p in updated[4:8]]
    loss = g_small[-1][0] * (0.5 / d)

    ada_off = offsets[-2]
    ada_n = -(-n_layers * 3 * d // (SUBLANES * LANES)) * SUBLANES
    d_ada_all = gathered[:, ada_off : ada_off + ada_n].reshape(N_DEV, -1)[:, : n_layers * 3 * d].reshape(N_DEV, n_layers, 3 * d)
    d_ada_cols = lax.dynamic_slice_in_dim(d_ada_all, my * ada_cols, ada_cols, axis=2)
    g_w_ada = _ada_weight_grad(c_all[:, 0, :], d_ada_cols.transpose(1, 0, 2))

    def update(w, g, m, v, name):
        shape = w.shape
        flat = lambda a: a.reshape(-1, shape[-1])
        return tuple(a.reshape(shape) for a in _adamw(flat(w), flat(g), flat(m), flat(v), name))

    g_w_in, g_w_out = jnp.stack(g_w_in), jnp.stack(g_w_out)
    upd_ada = update(w_ada, g_w_ada, m_w_ada, v_w_ada, "adamw_w_ada")
    upd_in = update(w_in, g_w_in, m_w_in, v_w_in, "adamw_w_in")
    upd_out = update(w_out, g_w_out, m_w_out, v_w_out, "adamw_w_out")

    def ordered(ada_, in_, out_, small_, ws):
        ng, qg, kg, sk, bs, ba, _ = small_
        return (ada_, ba, ng, in_, qg, kg, sk, ws, bs, out_)

    grads = ordered(g_w_ada, g_w_in, g_w_out, g_small, ws_small[0])
    deltas = ordered(upd_ada[0], upd_in[0], upd_out[0], d_small, ws_small[1])
    new_m = ordered(upd_ada[1], upd_in[1], upd_out[1], m_small, ws_small[2])
    new_v = ordered(upd_ada[2], upd_in[2], upd_out[2], v_small, ws_small[3])
    return (loss, dx.reshape(x.shape), *grads, *deltas, *new_m, *new_v)
```

```python
import functools

import jax
import jax.numpy as jnp
from jax import lax
from jax.experimental import pallas as pl
from jax.experimental.pallas import tpu as pltpu

F32 = jnp.float32
MXU_DTYPE = jnp.bfloat16
MESH_ID = pl.DeviceIdType.MESH

N_DEV = 8
HEAD_DIM = 64
N_HEADS = 8
Q_PER_KV = 4
D_ATTN = 512
D_KV = 128
D_GM = 512
N_GROUPS = 8
D_MIX = D_ATTN + D_GM
BLK = 128
LANES = 128
SUBLANES = 8
N_PAIRS = D_ATTN // LANES
D_QKV = D_ATTN + 2 * D_KV
D_REST = D_ATTN + 3 * D_GM
D_IN = D_QKV + D_REST
EPS = 1e-6
NEG_INF = -1e30
ALIBI_SLOPES = tuple(2.0 ** (-8.0 * (h + 1) / N_HEADS) for h in range(N_HEADS))
Q_SCALE = 1.0 / 8.0

ADAM_LR = 0.001
ADAM_B1 = 0.9
ADAM_B2 = 0.999
ADAM_EPS = 1e-08
ADAM_WD = 0.01
ADAM_STEP = 10

TOKEN_TILE = 512
VMEM_LIMIT_BYTES = 56 * 1024 * 1024


def _params(semantics=None):
    return pltpu.CompilerParams(dimension_semantics=semantics, vmem_limit_bytes=VMEM_LIMIT_BYTES)


def _dot(a, b):
    return jnp.dot(a, b, preferred_element_type=F32)


def _dot_nt(a, b):
    return lax.dot_general(a, b, (((1,), (1,)), ((), ())), preferred_element_type=F32)


def _dot_tn(a, b):
    return lax.dot_general(a, b, (((0,), (0,)), ((), ())), preferred_element_type=F32)


def _mx(v):
    return v.astype(MXU_DTYPE)


def _lane_lo(rows):
    return lax.broadcasted_iota(jnp.int32, (rows, LANES), 1) < HEAD_DIM


def _half_ones():
    r = lax.broadcasted_iota(jnp.int32, (LANES, LANES), 0) < HEAD_DIM
    c = lax.broadcasted_iota(jnp.int32, (LANES, LANES), 1) < HEAD_DIM
    return jnp.where(r == c, 1.0, 0.0).astype(jnp.bfloat16)


def _half_sum(v, ones):
    p1 = v.astype(jnp.bfloat16)
    p2 = (v - p1.astype(F32)).astype(jnp.bfloat16)
    return _dot(p1, ones) + _dot(p2, ones)


def _half_rms(v, ones):
    r = lax.rsqrt(_half_sum(v * v, ones) * (1.0 / HEAD_DIM) + EPS)
    return v * r, r


def _half_rms_bwd(dy, vhat, r, ones):
    return r * (dy - vhat * (_half_sum(vhat * dy, ones) * (1.0 / HEAD_DIM)))


def _group_rows(v):
    rows, n = v.shape
    return v.reshape(rows // SUBLANES, SUBLANES, n).sum(axis=0)


def _sigmoid(v):
    return 1.0 / (1.0 + jnp.exp(-v))


ROW_CHUNK = 32
VARIANT_HEADS = ((0, 2, 5, 7), (1, 3, 4, 6))
HEAD_SLOT = {h: (v, s) for v, heads in enumerate(VARIANT_HEADS) for s, h in enumerate(heads)}
STACK = Q_PER_KV * BLK


def _fill_attn_bias(bias_s):
    qi = lax.broadcasted_iota(jnp.int32, (BLK, 3 * BLK), 0)
    ci = lax.broadcasted_iota(jnp.int32, (BLK, 3 * BLK), 1)
    dist = jnp.abs(ci - BLK - qi)
    distf = dist.astype(F32)
    for h in range(N_HEADS):
        bias_s[h] = jnp.where(dist <= BLK, -(ALIBI_SLOPES[h] * distf), NEG_INF)


def _edge_mask(block, seq):
    kpos = (block - 1) * BLK + lax.broadcasted_iota(jnp.int32, (1, 3 * BLK), 1)
    return jnp.where((kpos >= 0) & (kpos < seq), 0.0, NEG_INF)


def _stage_queries(qn, lo_t, j, nb, qs):
    for a in range(2):
        v, slot = HEAD_SLOT[2 * j + a]
        qm = _mx(jnp.where(lo_t, qn, 0.0) if a == 0 else jnp.where(lo_t, 0.0, qn))
        for n in range(nb):
            qs[n, v, slot * BLK : (slot + 1) * BLK, :] = qm[n * BLK : (n + 1) * BLK]


def _unstack_pair(stacked, j, lo):
    (v0, s0), (v1, s1) = HEAD_SLOT[2 * j], HEAD_SLOT[2 * j + 1]
    return jnp.where(lo, stacked[v0][s0 * BLK : (s0 + 1) * BLK], stacked[v1][s1 * BLK : (s1 + 1) * BLK])


def _stage_keys(kvp_ref, qkv_ref, kvn_ref, kg, ones, tile, ks, kr, vs, vr, khat_s=None, rk_s=None):
    pieces = (
        (0, BLK, kvp_ref[:, 0:D_KV], kvp_ref[:, D_KV : 2 * D_KV]),
        (BLK, tile, qkv_ref[:, D_ATTN : D_ATTN + D_KV], qkv_ref[:, D_ATTN + D_KV : D_QKV]),
        (BLK + tile, BLK, kvn_ref[:, 0:D_KV], kvn_ref[:, D_KV : 2 * D_KV]),
    )
    for r0, n, k, v in pieces:
        khat, rk = _half_rms(k, ones)
        kn = khat * kg
        ks[r0 : r0 + n, :] = _mx(kn)
        kr[r0 : r0 + n, :] = _mx(pltpu.roll(kn, HEAD_DIM, 1))
        vs[r0 : r0 + n, :] = _mx(v)
        vr[r0 : r0 + n, :] = _mx(pltpu.roll(v, HEAD_DIM, 1))
        if khat_s is not None:
            khat_s[r0 : r0 + n, :] = khat
            rk_s[r0 : r0 + n, :] = rk


def _halo_specs(tile, seq):
    nb = tile // BLK
    last = seq // BLK - 1
    kv_col = D_ATTN // (2 * D_KV)
    prev = pl.BlockSpec((BLK, 2 * D_KV), lambda i: (jnp.maximum(i * nb - 1, 0), kv_col))
    nxt = pl.BlockSpec((BLK, 2 * D_KV), lambda i: (jnp.minimum((i + 1) * nb, last), kv_col))
    return prev, nxt


def _row_spec(tile, width):
    return pl.BlockSpec((tile, width), lambda i: (i, 0))


def _full_spec(shape):
    nd = len(shape)
    return pl.BlockSpec(shape, lambda i: (0,) * nd)


SMEM_SPEC = pl.BlockSpec(memory_space=pltpu.SMEM)
VMEM_SPEC = pl.BlockSpec(memory_space=pltpu.VMEM)


def _layer_spec(layer, rows, cols):
    return pl.BlockSpec((None, rows, cols), lambda i: (layer, 0, 0))


def _ln_proj_fwd(x, gain, scale1, shift, w_in_t, layer, name):
    seq, d = x.shape
    tile = min(TOKEN_TILE, seq)

    def body(x_ref, g_ref, s1_ref, sh_ref, wt_ref, pa_ref, pb_ref):
        xv = x_ref[...]
        r = lax.rsqrt(jnp.mean(xv * xv, axis=-1, keepdims=True) + EPS)
        h = _mx((xv * r) * g_ref[...] * s1_ref[...] + sh_ref[...])
        pa_ref[...] = _dot_nt(h, wt_ref[0:D_QKV, :])
        pb_ref[...] = _dot_nt(h, wt_ref[D_QKV:D_IN, :])

    vec = _full_spec((1, d))
    return pl.pallas_call(
        body,
        name=name,
        grid=(seq // tile,),
        in_specs=[_row_spec(tile, d), vec, vec, vec, _layer_spec(layer, D_IN, d)],
        out_specs=[_row_spec(tile, D_QKV), _row_spec(tile, D_REST)],
        out_shape=[jax.ShapeDtypeStruct((seq, D_QKV), F32), jax.ShapeDtypeStruct((seq, D_REST), F32)],
        compiler_params=_params(("parallel",)),
    )(x, gain, scale1, shift, w_in_t)


def _attn_fwd(pa, q_gain2, k_gain2, sink, name):
    seq = pa.shape[0]
    tile = min(TOKEN_TILE, seq)
    nb = tile // BLK
    ext = tile + 2 * BLK

    def body(sink_ref, qkv_ref, kvp_ref, kvn_ref, qg_ref, kg_ref, o_ref, qs, ks, kr, vs, vr, bias_s, s_scr, p_scr, inv_scr):
        i = pl.program_id(0)

        @pl.when(i == 0)
        def _():
            _fill_attn_bias(bias_s)

        ones = _half_ones()
        lo = _lane_lo(BLK)
        lo_t = _lane_lo(tile)
        _stage_keys(kvp_ref, qkv_ref, kvn_ref, kg_ref[...], ones, tile, ks, kr, vs, vr)
        for j in range(N_PAIRS):
            qhat, _ = _half_rms(qkv_ref[:, j * LANES : (j + 1) * LANES], ones)
            _stage_queries(qhat * (qg_ref[...] * Q_SCALE), lo_t, j, nb, qs)

        def block(n, carry):
            r0 = pl.multiple_of(n * BLK, BLK)
            krows = pl.ds(r0, 3 * BLK)
            edge = _edge_mask(i * nb + n, seq)
            for v in range(2):
                s_scr[v] = _dot_nt(qs[n, v], (kr if v else ks)[krows, :])
            for h in range(N_HEADS):
                v, slot = HEAD_SLOT[h]
                sink_h = sink_ref[h]
                for rc in range(0, BLK, ROW_CHUNK):
                    rows = slice(slot * BLK + rc, slot * BLK + rc + ROW_CHUNK)
                    s = s_scr[v, rows, :] + bias_s[h, rc : rc + ROW_CHUNK, :] + edge
                    m = jnp.maximum(jnp.max(s, axis=-1, keepdims=True), sink_h)
                    p = jnp.exp(s - m)
                    total = jnp.sum(p, axis=-1, keepdims=True) + jnp.exp(sink_h - m)
                    p_scr[v, rows, :] = _mx(p)
                    inv_scr[v, rows, :] = jnp.broadcast_to(1.0 / total, (ROW_CHUNK, LANES))
            outs = [_dot(p_scr[v], (vr if v else vs)[krows, :]) * inv_scr[v] for v in range(2)]
            for j in range(N_PAIRS):
                o_ref[pl.ds(r0, BLK), j * LANES : (j + 1) * LANES] = _unstack_pair(outs, j, lo)
            return carry

        lax.fori_loop(0, nb, block, 0)

    prev, nxt = _halo_specs(tile, seq)
    vec = _full_spec((1, LANES))
    return pl.pallas_call(
        body,
        name=name,
        grid=(seq // tile,),
        in_specs=[SMEM_SPEC, _row_spec(tile, D_QKV), prev, nxt, vec, vec],
        out_specs=_row_spec(tile, D_ATTN),
        out_shape=jax.ShapeDtypeStruct((seq, D_ATTN), F32),
        scratch_shapes=[
            pltpu.VMEM((nb, 2, STACK, LANES), MXU_DTYPE),
            pltpu.VMEM((ext, LANES), MXU_DTYPE),
            pltpu.VMEM((ext, LANES), MXU_DTYPE),
            pltpu.VMEM((ext, LANES), MXU_DTYPE),
            pltpu.VMEM((ext, LANES), MXU_DTYPE),
            pltpu.VMEM((N_HEADS, BLK, 3 * BLK), F32),
            pltpu.VMEM((2, STACK, 3 * BLK), F32),
            pltpu.VMEM((2, STACK, 3 * BLK), MXU_DTYPE),
            pltpu.VMEM((2, STACK, LANES), F32),
        ],
        compiler_params=_params(("arbitrary",)),
    )(sink, pa, pa, pa, q_gain2, k_gain2)


def _mix_out_fwd(pb, o, x, gate, w_out, layer, w_s, b_st, name, target=None):
    seq, d = x.shape
    tile = min(TOKEN_TILE, seq)
    nb = tile // BLK
    with_loss = target is not None

    def body(pb_ref, o_ref, x_ref, gate_ref, wo_ref, ws_ref, bs_ref, *rest):
        if with_loss:
            t_ref, xo_ref, acc_ref, y_s, vn_s = rest

            @pl.when(pl.program_id(0) == 0)
            def _():
                acc_ref[...] = jnp.zeros_like(acc_ref)
        else:
            xo_ref, y_s, vn_s = rest
        ones = _half_ones()
        lo = _lane_lo(BLK)
        ga = pb_ref[:, 0:D_ATTN]
        y_s[:, 0:D_ATTN] = _mx(o_ref[...] * (ga * _sigmoid(ga)))
        for j in range(N_PAIRS):
            cols = slice(2 * D_GM + j * LANES, 2 * D_GM + (j + 1) * LANES)
            vhat, _ = _half_rms(pb_ref[:, cols], ones)
            vn_s[:, j * LANES : (j + 1) * LANES] = _mx(vhat)

        def chunk(n, carry):
            rows = pl.ds(pl.multiple_of(n * BLK, BLK), BLK)
            for j in range(N_PAIRS):
                cols = slice(j * LANES, (j + 1) * LANES)
                vn = vn_s[rows, cols]
                sv = jnp.where(lo, _dot(ws_ref[2 * j], vn), _dot(ws_ref[2 * j + 1], vn)) + bs_ref[:, cols]
                u = pb_ref[rows, D_ATTN + j * LANES : D_ATTN + (j + 1) * LANES]
                gg = pb_ref[rows, D_ATTN + 2 * D_GM + j * LANES : D_ATTN + 2 * D_GM + (j + 1) * LANES]
                y_s[rows, D_ATTN + j * LANES : D_ATTN + (j + 1) * LANES] = _mx((u * sv) * (gg * _sigmoid(gg)))
            return carry

        lax.fori_loop(0, nb, chunk, 0)
        y = x_ref[...] + gate_ref[...] * _dot(y_s[...], wo_ref[...])
        if with_loss:
            e = y - t_ref[...]
            xo_ref[...] = e * (1.0 / d)
            acc_ref[...] += jnp.sum(jnp.sum(e * e, axis=-1, keepdims=True), axis=0, keepdims=True)
        else:
            xo_ref[...] = y

    row = _row_spec(tile, d)
    acc_shape = (SUBLANES, LANES)
    return pl.pallas_call(
        body,
        name=name,
        grid=(seq // tile,),
        in_specs=[
            _row_spec(tile, D_REST),
            _row_spec(tile, D_ATTN),
            row,
            _full_spec((1, d)),
            _layer_spec(layer, D_MIX, d),
            _full_spec((N_GROUPS, BLK, BLK)),
            _full_spec((BLK, D_GM)),
        ]
        + ([row] if with_loss else []),
        out_specs=[row, _full_spec(acc_shape)] if with_loss else row,
        out_shape=[jax.ShapeDtypeStruct((seq, d), F32), jax.ShapeDtypeStruct(acc_shape, F32)]
        if with_loss
        else jax.ShapeDtypeStruct((seq, d), F32),
        scratch_shapes=[pltpu.VMEM((tile, D_MIX), MXU_DTYPE), pltpu.VMEM((tile, D_GM), MXU_DTYPE)],
        compiler_params=_params(("arbitrary",) if with_loss else ("parallel",)),
    )(pb, o, x, gate, w_out, w_s, b_st, *([target] if with_loss else []))


def _mix_out_bwd(dxn, pb, o, gate, w_out, layer, w_s, w_s_t, b_st, name):
    seq, d = dxn.shape
    tile = min(TOKEN_TILE, seq)
    nb = tile // BLK

    def body(dxn_ref, pb_ref, o_ref, gate_ref, wo_ref, ws_ref, wst_ref, bs_ref,
             dpb_ref, do_ref, g_ref, dws_ref, dbs_ref, y_s, dy_s, vn_s, rv_s, vnb_s, sv_s, dsv_s, dvn_s):
        @pl.when(pl.program_id(0) == 0)
        def _():
            g_ref[...] = jnp.zeros_like(g_ref)
            dws_ref[...] = jnp.zeros_like(dws_ref)
            dbs_ref[...] = jnp.zeros_like(dbs_ref)

        ones = _half_ones()
        lo = _lane_lo(BLK)
        c_u = slice(D_ATTN, D_ATTN + D_GM)
        c_vg = slice(D_ATTN + D_GM, D_ATTN + 2 * D_GM)
        c_gg = slice(D_ATTN + 2 * D_GM, D_REST)
        dxv = dxn_ref[...]
        dy_s[...] = _dot_nt(_mx(dxv * gate_ref[...]), wo_ref[...])
        ga = pb_ref[:, 0:D_ATTN]
        sig = _sigmoid(ga)
        sil = ga * sig
        ov = o_ref[...]
        y_s[:, 0:D_ATTN] = _mx(ov * sil)
        da = dy_s[:, 0:D_ATTN]
        do_ref[...] = da * sil
        dpb_ref[:, 0:D_ATTN] = (da * ov * (sig * (1.0 + ga * (1.0 - sig)))).astype(dpb_ref.dtype)
        for j in range(N_PAIRS):
            cols = slice(j * LANES, (j + 1) * LANES)
            vhat, rv = _half_rms(pb_ref[:, 2 * D_GM + j * LANES : 2 * D_GM + (j + 1) * LANES], ones)
            vn_s[:, cols] = vhat
            rv_s[:, cols] = rv
            vnb_s[:, cols] = _mx(vhat)

        def spatial_fwd(n, carry):
            rows = pl.ds(pl.multiple_of(n * BLK, BLK), BLK)
            for j in range(N_PAIRS):
                cols = slice(j * LANES, (j + 1) * LANES)
                vn = vnb_s[rows, cols]
                sv_s[rows, cols] = jnp.where(lo, _dot(ws_ref[2 * j], vn), _dot(ws_ref[2 * j + 1], vn)) + bs_ref[:, cols]
            return carry

        lax.fori_loop(0, nb, spatial_fwd, 0)

        def gating(n, carry):
            rows = pl.ds(pl.multiple_of(n * BLK, BLK), BLK)
            sv = sv_s[rows, :]
            u = pb_ref[rows, c_u]
            gg = pb_ref[rows, c_gg]
            sg = _sigmoid(gg)
            silg = gg * sg
            m0 = u * sv
            y_s[rows, D_ATTN:D_MIX] = _mx(m0 * silg)
            dm = dy_s[rows, D_ATTN:D_MIX]
            dm0 = dm * silg
            dpb_ref[rows, c_gg] = (dm * m0 * (sg * (1.0 + gg * (1.0 - sg)))).astype(dpb_ref.dtype)
            dpb_ref[rows, c_u] = (dm0 * sv).astype(dpb_ref.dtype)
            dsv = dm0 * u
            dsv_s[rows, :] = _mx(dsv)
            dbs_ref[...] += dsv
            return carry

        lax.fori_loop(0, nb, gating, 0)

        def spatial_bwd(n, carry):
            rows = pl.ds(pl.multiple_of(n * BLK, BLK), BLK)
            for j in range(N_PAIRS):
                cols = slice(j * LANES, (j + 1) * LANES)
                dsv = dsv_s[rows, cols]
                dvn_s[rows, cols] = jnp.where(lo, _dot(wst_ref[2 * j], dsv), _dot(wst_ref[2 * j + 1], dsv))
            return carry

        lax.fori_loop(0, nb, spatial_bwd, 0)
        zero = jnp.zeros((BLK, LANES), MXU_DTYPE)
        for j in range(N_PAIRS):
            cols = slice(j * LANES, (j + 1) * LANES)
            chunks = [dsv_s[n * BLK : (n + 1) * BLK, cols] for n in range(nb)]
            vn_all = jnp.concatenate([vnb_s[n * BLK : (n + 1) * BLK, cols] for n in range(nb)], axis=1)
            dws_ref[2 * j] += _dot_nt(jnp.concatenate([jnp.where(lo, c, zero) for c in chunks], axis=1), vn_all)
            dws_ref[2 * j + 1] += _dot_nt(jnp.concatenate([jnp.where(lo, zero, c) for c in chunks], axis=1), vn_all)
            dpb_ref[:, D_ATTN + D_GM + j * LANES : D_ATTN + D_GM + (j + 1) * LANES] = _half_rms_bwd(
                dvn_s[:, cols], vn_s[:, cols], rv_s[:, cols], ones
            ).astype(dpb_ref.dtype)
        g_ref[...] += _dot_tn(y_s[...], _mx(dxv))

    return pl.pallas_call(
        body,
        name=name,
        grid=(seq // tile,),
        in_specs=[
            _row_spec(tile, d),
            _row_spec(tile, D_REST),
            _row_spec(tile, D_ATTN),
            _full_spec((1, d)),
            _layer_spec(layer, D_MIX, d),
            _full_spec((N_GROUPS, BLK, BLK)),
            _full_spec((N_GROUPS, BLK, BLK)),
            _full_spec((BLK, D_GM)),
        ],
        out_specs=[
            _row_spec(tile, D_REST),
            _row_spec(tile, D_ATTN),
            _full_spec((D_MIX, d)),
            _full_spec((N_GROUPS, BLK, BLK)),
            _full_spec((BLK, D_GM)),
        ],
        out_shape=[
            jax.ShapeDtypeStruct((seq, D_REST), MXU_DTYPE),
            jax.ShapeDtypeStruct((seq, D_ATTN), F32),
            jax.ShapeDtypeStruct((D_MIX, d), F32),
            jax.ShapeDtypeStruct((N_GROUPS, BLK, BLK), F32),
            jax.ShapeDtypeStruct((BLK, D_GM), F32),
        ],
        scratch_shapes=[
            pltpu.VMEM((tile, D_MIX), MXU_DTYPE),
            pltpu.VMEM((tile, D_MIX), F32),
            pltpu.VMEM((tile, D_GM), F32),
            pltpu.VMEM((tile, D_GM), F32),
            pltpu.VMEM((tile, D_GM), MXU_DTYPE),
            pltpu.VMEM((tile, D_GM), F32),
            pltpu.VMEM((tile, D_GM), MXU_DTYPE),
            pltpu.VMEM((tile, D_GM), F32),
        ],
        compiler_params=_params(("arbitrary",)),
    )(dxn, pb, o, gate, w_out, w_s, w_s_t, b_st)


def _attn_bwd(pa, o, do, q_gain2, k_gain2, sink, name):
    seq = pa.shape[0]
    tile = min(TOKEN_TILE, seq)
    nb = tile // BLK
    nt = seq // tile
    ext = tile + 2 * BLK

    def body(sink_ref, qkv_ref, kvp_ref, kvn_ref, o_ref, do_ref, qg_ref, kg_ref,
             dq_ref, dkv_ref, hp_ref, hn_ref, dqg_ref, dkg_ref, dsk_ref,
             qs, dos, qhat_s, rq_s, ks, kr, vs, vr, khat_s, rk_s, dqn_s, dka, dva, bias_s, s_scr, dp_scr, p_scr, ds_scr):
        i = pl.program_id(0)

        @pl.when(i == 0)
        def _():
            dqg_ref[...] = jnp.zeros_like(dqg_ref)
            dkg_ref[...] = jnp.zeros_like(dkg_ref)
            dsk_ref[...] = jnp.zeros_like(dsk_ref)
            _fill_attn_bias(bias_s)

        ones = _half_ones()
        lo = _lane_lo(BLK)
        lo_t = _lane_lo(tile)
        lo_c = _lane_lo(ROW_CHUNK)
        qg = qg_ref[...] * Q_SCALE
        kg = kg_ref[...]
        _stage_keys(kvp_ref, qkv_ref, kvn_ref, kg, ones, tile, ks, kr, vs, vr, khat_s, rk_s)
        for j in range(N_PAIRS):
            cols = slice(j * LANES, (j + 1) * LANES)
            qhat, rq = _half_rms(qkv_ref[:, cols], ones)
            qhat_s[:, cols] = qhat
            rq_s[:, cols] = rq
            _stage_queries(qhat * qg, lo_t, j, nb, qs)
            _stage_queries(do_ref[:, cols], lo_t, j, nb, dos)
        dka[...] = jnp.zeros_like(dka)
        dva[...] = jnp.zeros_like(dva)
        head_lane = lax.broadcasted_iota(jnp.int32, (1, LANES), 1)

        def block(n, dsink):
            r0 = pl.multiple_of(n * BLK, BLK)
            krows = pl.ds(r0, 3 * BLK)
            edge = _edge_mask(i * nb + n, seq)
            for v in range(2):
                s_scr[v] = _dot_nt(qs[n, v], (kr if v else ks)[krows, :])
                dp_scr[v] = _dot_nt(dos[n, v], (vr if v else vs)[krows, :])
            for h in range(N_HEADS):
                v, slot = HEAD_SLOT[h]
                j, a = divmod(h, 2)
                cols = slice(j * LANES, (j + 1) * LANES)
                sink_h = sink_ref[h]
                sink_part = jnp.zeros((ROW_CHUNK, 1), F32)
                for rc in range(0, BLK, ROW_CHUNK):
                    rows = slice(slot * BLK + rc, slot * BLK + rc + ROW_CHUNK)
                    trows = pl.ds(pl.multiple_of(r0 + rc, ROW_CHUNK), ROW_CHUNK)
                    s = s_scr[v, rows, :] + bias_s[h, rc : rc + ROW_CHUNK, :] + edge
                    m = jnp.maximum(jnp.max(s, axis=-1, keepdims=True), sink_h)
                    p = jnp.exp(s - m)
                    e_sink = jnp.exp(sink_h - m)
                    inv = 1.0 / (jnp.sum(p, axis=-1, keepdims=True) + e_sink)
                    pn = p * inv
                    prod = do_ref[trows, cols] * o_ref[trows, cols]
                    prod = jnp.where(lo_c, prod, 0.0) if a == 0 else jnp.where(lo_c, 0.0, prod)
                    dcol = jnp.sum(prod, axis=-1, keepdims=True)
                    ds_scr[v, rows, :] = _mx(pn * (dp_scr[v, rows, :] - dcol))
                    p_scr[v, rows, :] = _mx(pn)
                    sink_part = sink_part + (e_sink * inv) * dcol
                dsink = dsink - jnp.where(head_lane == h, jnp.sum(sink_part, axis=0, keepdims=True), 0.0)
            dqv = []
            for v in range(2):
                dqv.append(_dot(ds_scr[v], (kr if v else ks)[krows, :]))
                dka[v, krows, :] += _dot_tn(ds_scr[v], qs[n, v])
                dva[v, krows, :] += _dot_tn(p_scr[v], dos[n, v])
            for j in range(N_PAIRS):
                dqn_s[pl.ds(r0, BLK), j * LANES : (j + 1) * LANES] = _unstack_pair(dqv, j, lo)
            return dsink

        dsink = lax.fori_loop(0, nb, block, jnp.zeros((1, LANES), F32))
        dsk_ref[...] += jnp.broadcast_to(dsink, (SUBLANES, LANES))
        for j in range(N_PAIRS):
            cols = slice(j * LANES, (j + 1) * LANES)
            dqn = dqn_s[:, cols]
            qhat = qhat_s[:, cols]
            dqg_ref[:, cols] += _group_rows(dqn * qhat) * Q_SCALE
            dq_ref[:, cols] = _half_rms_bwd(dqn * qg, qhat, rq_s[:, cols], ones).astype(dq_ref.dtype)
        dkn = dka[0] + pltpu.roll(dka[1], HEAD_DIM, 1)
        khat = khat_s[...]
        dkg_ref[...] += _group_rows(dkn * khat)
        dk = _half_rms_bwd(dkn * kg, khat, rk_s[...], ones)
        dv = dva[0] + pltpu.roll(dva[1], HEAD_DIM, 1)
        hp_ref[:, 0:D_KV] = dk[0:BLK]
        hp_ref[:, D_KV : 2 * D_KV] = dv[0:BLK]
        dkv_ref[:, 0:D_KV] = dk[BLK : BLK + tile]
        dkv_ref[:, D_KV : 2 * D_KV] = dv[BLK : BLK + tile]
        hn_ref[:, 0:D_KV] = dk[BLK + tile : ext]
        hn_ref[:, D_KV : 2 * D_KV] = dv[BLK + tile : ext]

    prev, nxt = _halo_specs(tile, seq)
    vec = _full_spec((1, LANES))
    halo = pl.BlockSpec((None, BLK, 2 * D_KV), lambda i: (i, 0, 0))
    return pl.pallas_call(
        body,
        name=name,
        grid=(nt,),
        in_specs=[SMEM_SPEC, _row_spec(tile, D_QKV), prev, nxt, _row_spec(tile, D_ATTN), _row_spec(tile, D_ATTN), vec, vec],
        out_specs=[
            _row_spec(tile, D_ATTN),
            _row_spec(tile, 2 * D_KV),
            halo,
            halo,
            _full_spec((SUBLANES, D_ATTN)),
            _full_spec((SUBLANES, LANES)),
            _full_spec((SUBLANES, LANES)),
        ],
        out_shape=[
            jax.ShapeDtypeStruct((seq, D_ATTN), MXU_DTYPE),
            jax.ShapeDtypeStruct((seq, 2 * D_KV), F32),
            jax.ShapeDtypeStruct((nt, BLK, 2 * D_KV), F32),
            jax.ShapeDtypeStruct((nt, BLK, 2 * D_KV), F32),
            jax.ShapeDtypeStruct((SUBLANES, D_ATTN), F32),
            jax.ShapeDtypeStruct((SUBLANES, LANES), F32),
            jax.ShapeDtypeStruct((SUBLANES, LANES), F32),
        ],
        scratch_shapes=[
            pltpu.VMEM((nb, 2, STACK, LANES), MXU_DTYPE),
            pltpu.VMEM((nb, 2, STACK, LANES), MXU_DTYPE),
            pltpu.VMEM((tile, D_ATTN), F32),
            pltpu.VMEM((tile, D_ATTN), F32),
            pltpu.VMEM((ext, LANES), MXU_DTYPE),
            pltpu.VMEM((ext, LANES), MXU_DTYPE),
            pltpu.VMEM((ext, LANES), MXU_DTYPE),
            pltpu.VMEM((ext, LANES), MXU_DTYPE),
            pltpu.VMEM((ext, LANES), F32),
            pltpu.VMEM((ext, LANES), F32),
            pltpu.VMEM((tile, D_ATTN), F32),
            pltpu.VMEM((2, ext, LANES), F32),
            pltpu.VMEM((2, ext, LANES), F32),
            pltpu.VMEM((N_HEADS, BLK, 3 * BLK), F32),
            pltpu.VMEM((2, STACK, 3 * BLK), F32),
            pltpu.VMEM((2, STACK, 3 * BLK), F32),
            pltpu.VMEM((2, STACK, 3 * BLK), MXU_DTYPE),
            pltpu.VMEM((2, STACK, 3 * BLK), MXU_DTYPE),
        ],
        compiler_params=_params(("arbitrary",)),
    )(sink, pa, pa, pa, o, do, q_gain2, k_gain2)


def _halo_in_specs(tile, nt):
    from_prev = pl.BlockSpec((None, BLK, 2 * D_KV), lambda i: (jnp.maximum(i - 1, 0), 0, 0))
    from_next = pl.BlockSpec((None, BLK, 2 * D_KV), lambda i: (jnp.minimum(i + 1, nt - 1), 0, 0))
    return from_prev, from_next


def _proj_bwd_dx(x, dxn, dq, dkv, halo_prev, halo_next, dpb, w_in_t, layer, gain, scale1, name):
    seq, d = x.shape
    tile = min(TOKEN_TILE, seq)
    nt = seq // tile

    def body(x_ref, dxn_ref, dq_ref, dkv_ref, hn_ref, hp_ref, dpb_ref, wt_ref, g_ref, s1_ref,
             dx_ref, dkvb_ref, c0_ref, c1_ref):
        i = pl.program_id(0)

        @pl.when(i == 0)
        def _():
            c0_ref[...] = jnp.zeros_like(c0_ref)
            c1_ref[...] = jnp.zeros_like(c1_ref)

        top = dkv_ref[0:BLK, :] + jnp.where(i > 0, hn_ref[...], 0.0)
        bot = dkv_ref[tile - BLK : tile, :] + jnp.where(i < nt - 1, hp_ref[...], 0.0)
        if tile == BLK:
            dkvb_ref[...] = (top + bot - dkv_ref[...]).astype(dkvb_ref.dtype)
        else:
            dkvb_ref[0:BLK, :] = top.astype(dkvb_ref.dtype)
            dkvb_ref[tile - BLK : tile, :] = bot.astype(dkvb_ref.dtype)
            if tile > 2 * BLK:
                dkvb_ref[BLK : tile - BLK, :] = dkv_ref[BLK : tile - BLK, :].astype(dkvb_ref.dtype)
        dh = (
            _dot(dq_ref[...], wt_ref[0:D_ATTN, :])
            + _dot(dkvb_ref[...], wt_ref[D_ATTN:D_QKV, :])
            + _dot(dpb_ref[...], wt_ref[D_QKV:D_IN, :])
        )
        xv = x_ref[...]
        r = lax.rsqrt(jnp.mean(xv * xv, axis=-1, keepdims=True) + EPS)
        xn = xv * r
        c0_ref[...] += _group_rows(dh)
        c1_ref[...] += _group_rows(dh * xn)
        dxn_ = dh * (g_ref[...] * s1_ref[...])
        dx_ref[...] = dxn_ref[...] + r * (dxn_ - xn * jnp.mean(xn * dxn_, axis=-1, keepdims=True))

    from_prev, from_next = _halo_in_specs(tile, nt)
    vec = _full_spec((1, d))
    return pl.pallas_call(
        body,
        name=name,
        grid=(nt,),
        in_specs=[
            _row_spec(tile, d),
            _row_spec(tile, d),
            _row_spec(tile, D_ATTN),
            _row_spec(tile, 2 * D_KV),
            from_prev,
            from_next,
            _row_spec(tile, D_REST),
            _layer_spec(layer, D_IN, d),
            vec,
            vec,
        ],
        out_specs=[_row_spec(tile, d), _row_spec(tile, 2 * D_KV), _full_spec((SUBLANES, d)), _full_spec((SUBLANES, d))],
        out_shape=[
            jax.ShapeDtypeStruct((seq, d), F32),
            jax.ShapeDtypeStruct((seq, 2 * D_KV), MXU_DTYPE),
            jax.ShapeDtypeStruct((SUBLANES, d), F32),
            jax.ShapeDtypeStruct((SUBLANES, d), F32),
        ],
        compiler_params=_params(("arbitrary",)),
    )(x, dxn, dq, dkv, halo_next, halo_prev, dpb, w_in_t, gain, scale1)


def _proj_bwd_dw(x, gain, scale1, shift, dq, dkvb, dpb, name):
    seq, d = x.shape
    tile = min(TOKEN_TILE, seq)

    def body(x_ref, g_ref, s1_ref, sh_ref, dq_ref, dkv_ref, dpb_ref, dw_ref):
        @pl.when(pl.program_id(0) == 0)
        def _():
            dw_ref[...] = jnp.zeros_like(dw_ref)

        xv = x_ref[...]
        r = lax.rsqrt(jnp.mean(xv * xv, axis=-1, keepdims=True) + EPS)
        h = _mx((xv * r) * g_ref[...] * s1_ref[...] + sh_ref[...])
        dw_ref[0:D_ATTN, :] += _dot_tn(dq_ref[...], h)
        dw_ref[D_ATTN:D_QKV, :] += _dot_tn(dkv_ref[...], h)
        dw_ref[D_QKV:D_IN, :] += _dot_tn(dpb_ref[...], h)

    vec = _full_spec((1, d))
    return pl.pallas_call(
        body,
        name=name,
        grid=(seq // tile,),
        in_specs=[_row_spec(tile, d), vec, vec, vec, _row_spec(tile, D_ATTN), _row_spec(tile, 2 * D_KV), _row_spec(tile, D_REST)],
        out_specs=_full_spec((D_IN, d)),
        out_shape=jax.ShapeDtypeStruct((D_IN, d), F32),
        compiler_params=_params(("arbitrary",)),
    )(x, gain, scale1, shift, dq, dkvb, dpb)


def _w_out_finish(g, w_out, layer, gate, name):
    d_mix, d = g.shape

    def body(g_ref, w_ref, gate_ref, dw_ref, dgate_ref):
        gv = g_ref[...]
        dw_ref[...] = gv * gate_ref[...]
        dgate_ref[...] = _group_rows(gv * w_ref[layer].astype(F32))

    return pl.pallas_call(
        body,
        name=name,
        in_specs=[VMEM_SPEC, VMEM_SPEC, VMEM_SPEC],
        out_specs=[VMEM_SPEC, VMEM_SPEC],
        out_shape=[jax.ShapeDtypeStruct((d_mix, d), F32), jax.ShapeDtypeStruct((SUBLANES, d), F32)],
        compiler_params=_params(),
    )(g, w_out, gate)


def _adamw_math(w, g, m, v):
    m = ADAM_B1 * m + (1.0 - ADAM_B1) * g
    v = ADAM_B2 * v + (1.0 - ADAM_B2) * (g * g)
    m_hat = m / (1.0 - ADAM_B1**ADAM_STEP)
    v_hat = v / (1.0 - ADAM_B2**ADAM_STEP)
    delta = -ADAM_LR * (m_hat / (jnp.sqrt(v_hat) + ADAM_EPS) + ADAM_WD * w)
    return delta, m, v


def _adamw(w, g, m, v, name):
    rows, cols = w.shape
    tile = min(TOKEN_TILE, rows)

    def body(w_ref, g_ref, m_ref, v_ref, d_ref, mo_ref, vo_ref):
        d_ref[...], mo_ref[...], vo_ref[...] = _adamw_math(w_ref[...], g_ref[...], m_ref[...], v_ref[...])

    spec = _row_spec(tile, cols)
    shape = jax.ShapeDtypeStruct((rows, cols), F32)
    return pl.pallas_call(
        body,
        name=name,
        grid=(rows // tile,),
        in_specs=[spec] * 4,
        out_specs=[spec] * 3,
        out_shape=[shape] * 3,
        compiler_params=_params(("parallel",)),
    )(w, g, m, v)


def _small_update(gathered, gathered_ws, w, m, v, ws, m_ws, v_ws):
    def body(ga_ref, gws_ref, w_ref, m_ref, v_ref, ws_ref, mws_ref, vws_ref, *outs):
        for src, refs, out in ((ga_ref, (w_ref, m_ref, v_ref), outs[0:4]), (gws_ref, (ws_ref, mws_ref, vws_ref), outs[4:8])):
            g = src[0].astype(F32)
            for j in range(1, N_DEV):
                g = g + src[j].astype(F32)
            out[0][...] = g
            out[1][...], out[2][...], out[3][...] = _adamw_math(refs[0][...], g, refs[1][...], refs[2][...])

    shapes = [jax.ShapeDtypeStruct(w.shape, F32)] * 4 + [jax.ShapeDtypeStruct(ws.shape, F32)] * 4
    return pl.pallas_call(
        body,
        name="small_update",
        in_specs=[VMEM_SPEC] * 8,
        out_specs=[VMEM_SPEC] * 8,
        out_shape=shapes,
        compiler_params=_params(),
    )(gathered, gathered_ws, w, m, v, ws, m_ws, v_ws)


def _ada_weight_grad(c_all, d_ada_cols):
    d = c_all.shape[-1]
    n_layers, _, width = d_ada_cols.shape

    def body(c_ref, da_ref, dw_ref):
        cv = c_ref[...]
        cond = cv * _sigmoid(cv)
        for l in range(n_layers):
            dw_ref[l] = lax.dot_general(
                cond, da_ref[l], (((0,), (0,)), ((), ())), preferred_element_type=F32, precision=lax.Precision.HIGHEST
            )

    return pl.pallas_call(
        body,
        name="ada_weight_grad",
        in_specs=[VMEM_SPEC, VMEM_SPEC],
        out_specs=VMEM_SPEC,
        out_shape=jax.ShapeDtypeStruct((n_layers, d, width), F32),
        compiler_params=_params(),
    )(c_all, d_ada_cols)


def _position():
    return lax.axis_index("x"), lax.axis_index("y"), lax.axis_index("c")


def _flip(pos, k):
    x, y, c = pos
    return (1 - x if k & 4 else x, 1 - y if k & 2 else y, 1 - c if k & 1 else c)


def _index(pos):
    x, y, c = pos
    return 4 * x + 2 * y + c


def _remote(src, dst, send_sem, recv_sem, to):
    return pltpu.make_async_remote_copy(
        src_ref=src, dst_ref=dst, send_sem=send_sem, recv_sem=recv_sem, device_id=to, device_id_type=MESH_ID
    )


def _two_level_all_gather(slots, send_sems, recv_sems, between=None):
    me = _position()
    sibling = _flip(me, 1)
    others = (4, 2, 6)

    def copy(t, k, block, to):
        slot = slots[t](_index(block))
        return _remote(slot, slot, send_sems.at[7 * t + k], recv_sems.at[7 * t + k], to)

    started = []
    for t in range(len(slots)):
        started.append(copy(t, 0, me, sibling))
        started += [copy(t, 1 + j, me, _flip(me, f)) for j, f in enumerate(others)]
    for cp in started:
        cp.start()
    if between is not None:
        between()
    for j, f in enumerate(others):
        for t in range(len(slots)):
            copy(t, 1 + j, _flip(me, f), me).wait_recv()
            passed = copy(t, 4 + j, _flip(me, f), sibling)
            passed.start()
            started.append(passed)
    for t in range(len(slots)):
        copy(t, 0, sibling, me).wait_recv()
        for j, f in enumerate(others):
            copy(t, 4 + j, _flip(sibling, f), me).wait_recv()
    for cp in started:
        cp.wait_send()


def _row_block(ref, rows):
    return lambda j: ref.at[:, pl.ds(pl.multiple_of(j * rows, 16), rows), :]


def _ada_exchange(c_ref, w_ref, call_ref, parts_ref, sbuf, sem_s1, sem_r1, sem_s2, sem_r2):
    d = c_ref.shape[-1]
    n_layers = w_ref.shape[0]
    me = _position()
    my = _index(me)
    call_ref[my] = jnp.broadcast_to(c_ref[...], (SUBLANES, d))
    mine = call_ref.at[my]
    first = [_remote(mine, mine, sem_s1.at[k - 1], sem_r1.at[k - 1], _flip(me, k)) for k in range(1, N_DEV)]
    for cp in first:
        cp.start()
    for k in range(1, N_DEV):
        theirs = call_ref.at[_index(_flip(me, k))]
        _remote(theirs, theirs, sem_s1.at[k - 1], sem_r1.at[k - 1], _flip(me, k)).wait_recv()
    for b in range(N_DEV):
        cv = call_ref[b]
        cond = cv * _sigmoid(cv)
        for l in range(n_layers):
            sbuf[b, l] = jnp.dot(cond, w_ref[l], preferred_element_type=F32, precision=lax.Precision.HIGHEST)
    parts_ref[my] = sbuf[my]
    second = []
    for k in range(1, N_DEV):
        to = _flip(me, k)
        second.append(_remote(sbuf.at[_index(to)], parts_ref.at[my], sem_s2.at[k - 1], sem_r2.at[k - 1], to))
    for cp in second:
        cp.start()
    for k in range(1, N_DEV):
        theirs = parts_ref.at[_index(_flip(me, k))]
        _remote(theirs, theirs, sem_s2.at[k - 1], sem_r2.at[k - 1], _flip(me, k)).wait_recv()
    for cp in first + second:
        cp.wait_send()


def _gather_weights(w_in_t, w_out, c_row, w_ada):
    n_layers, rows_in, d = w_in_t.shape
    rows_out = w_out.shape[1]
    width = w_ada.shape[2]

    def body(wi_ref, wo_ref, c_ref, wa_ref, gi_ref, go_ref, call_ref, parts_ref, sbuf, send_sems, recv_sems, *ada_sems):
        my = _index(_position())
        gi_ref[:, pl.ds(pl.multiple_of(my * rows_in, 16), rows_in), :] = wi_ref[...].astype(gi_ref.dtype)
        go_ref[:, pl.ds(pl.multiple_of(my * rows_out, 16), rows_out), :] = wo_ref[...].astype(go_ref.dtype)
        _two_level_all_gather(
            (_row_block(gi_ref, rows_in), _row_block(go_ref, rows_out)),
            send_sems,
            recv_sems,
            between=functools.partial(_ada_exchange, c_ref, wa_ref, call_ref, parts_ref, sbuf, *ada_sems),
        )

    return pl.pallas_call(
        body,
        name="gather_weights",
        in_specs=[VMEM_SPEC] * 4,
        out_specs=[VMEM_SPEC] * 4,
        out_shape=[
            jax.ShapeDtypeStruct((n_layers, N_DEV * rows_in, d), MXU_DTYPE),
            jax.ShapeDtypeStruct((n_layers, N_DEV * rows_out, d), MXU_DTYPE),
            jax.ShapeDtypeStruct((N_DEV, SUBLANES, d), F32),
            jax.ShapeDtypeStruct((N_DEV, n_layers, SUBLANES, width), F32),
        ],
        scratch_shapes=[
            pltpu.VMEM((N_DEV, n_layers, SUBLANES, width), F32),
            pltpu.SemaphoreType.DMA((14,)),
            pltpu.SemaphoreType.DMA((14,)),
        ]
        + [pltpu.SemaphoreType.DMA((N_DEV - 1,))] * 4,
        compiler_params=_params(),
    )(w_in_t, w_out, c_row, w_ada)


def _gather_small(packed, d_ws):
    def body(p_ref, ws_ref, g_ref, gws_ref, send_sems, recv_sems):
        my = _index(_position())
        g_ref[my] = p_ref[...]
        gws_ref[my] = ws_ref[...].astype(gws_ref.dtype)
        _two_level_all_gather((lambda j: g_ref.at[j], lambda j: gws_ref.at[j]), send_sems, recv_sems)

    return pl.pallas_call(
        body,
        name="gather_small",
        in_specs=[VMEM_SPEC, VMEM_SPEC],
        out_specs=[VMEM_SPEC, VMEM_SPEC],
        out_shape=[
            jax.ShapeDtypeStruct((N_DEV,) + packed.shape, F32),
            jax.ShapeDtypeStruct((N_DEV,) + d_ws.shape, jnp.bfloat16),
        ],
        scratch_shapes=[pltpu.SemaphoreType.DMA((14,)), pltpu.SemaphoreType.DMA((14,))],
        compiler_params=_params(),
    )(packed, d_ws)


def _reduce_scatter(blocks_a, blocks_b, name):
    wire = jnp.bfloat16

    def body(a_ref, b_ref, oa_ref, ob_ref, stage_a, stage_b, half_a, half_b, send_a, send_b, chips_a, chips_b, send_sems, recv_sems):
        me = _position()
        x, y, c = me
        sibling = _flip(me, 1)
        my_chip = 2 * x + y
        others = (4, 2, 6)
        arrays = ((a_ref, stage_a, half_a, send_a, chips_a, oa_ref), (b_ref, stage_b, half_b, send_b, chips_b, ob_ref))
        to_sibling = []
        for t, (src, stage, half, _, _, _) in enumerate(arrays):
            for chip in range(4):
                stage[chip] = src[chip, 1 - c].astype(wire)
            cp = _remote(stage, half, send_sems.at[4 * t], recv_sems.at[4 * t], sibling)
            cp.start()
            to_sibling.append(cp)
        to_chips = []
        for t, (src, _, half, send, chips, _) in enumerate(arrays):
            to_sibling[t].wait_recv()
            chips[my_chip] = (src[my_chip, c] + half[my_chip].astype(F32)).astype(wire)
            for j, f in enumerate(others):
                px, py, _ = _flip(me, f)
                chip = 2 * px + py
                send[j] = (src[chip, c] + half[chip].astype(F32)).astype(wire)
                cp = _remote(send.at[j], chips.at[my_chip], send_sems.at[4 * t + 1 + j], recv_sems.at[4 * t + 1 + j], _flip(me, f))
                cp.start()
                to_chips.append(cp)
        for t, (_, _, _, _, chips, out) in enumerate(arrays):
            for j, f in enumerate(others):
                px, py, _ = _flip(me, f)
                slot = chips.at[2 * px + py]
                _remote(slot, slot, send_sems.at[4 * t + 1 + j], recv_sems.at[4 * t + 1 + j], _flip(me, f)).wait_recv()
            out[...] = ((chips[0].astype(F32) + chips[1].astype(F32)) + chips[2].astype(F32)) + chips[3].astype(F32)
        for cp in to_sibling + to_chips:
            cp.wait_send()

    def scratch(blocks):
        blk = blocks.shape[2:]
        return [pltpu.VMEM((4,) + blk, wire), pltpu.VMEM((4,) + blk, wire), pltpu.VMEM((3,) + blk, wire), pltpu.VMEM((4,) + blk, wire)]

    sa, sb = scratch(blocks_a), scratch(blocks_b)
    return pl.pallas_call(
        body,
        name=name,
        in_specs=[VMEM_SPEC, VMEM_SPEC],
        out_specs=[VMEM_SPEC, VMEM_SPEC],
        out_shape=[jax.ShapeDtypeStruct(blocks_a.shape[2:], F32), jax.ShapeDtypeStruct(blocks_b.shape[2:], F32)],
        scratch_shapes=[sa[0], sb[0], sa[1], sb[1], sa[2], sb[2], sa[3], sb[3], pltpu.SemaphoreType.DMA((8,)), pltpu.SemaphoreType.DMA((8,))],
        compiler_params=_params(),
    )(blocks_a, blocks_b)


def _pack_rows(parts):
    rows, offsets, at = [], [], 0
    for p in parts:
        flat = p.reshape(-1)
        n = -(-flat.shape[0] // (SUBLANES * LANES)) * SUBLANES
        rows.append(jnp.pad(flat, (0, n * LANES - flat.shape[0])).reshape(n, LANES))
        offsets.append(at)
        at += n
    return jnp.concatenate(rows, axis=0), offsets


def _unpack_rows(packed, offsets, shapes):
    out = []
    for off, shape in zip(offsets, shapes):
        size = 1
        for s in shape:
            size *= s
        n = -(-size // (SUBLANES * LANES)) * SUBLANES
        out.append(packed[off : off + n].reshape(-1)[:size].reshape(shape))
    return out


def kernel(x, c, w_ada, b_ada, norm_gain, w_in, q_gain, k_gain, sink, w_s, b_s, w_out, loss_target, m_w_ada, m_b_ada, m_norm_gain, m_w_in, m_q_gain, m_k_gain, m_sink, m_w_s, m_b_s, m_w_out, v_w_ada, v_b_ada, v_norm_gain, v_w_in, v_q_gain, v_k_gain, v_sink, v_w_s, v_b_s, v_w_out):
    seq, d = x.shape[1], x.shape[2]
    n_layers = w_in.shape[0]
    w_cols = w_in.shape[2]
    ada_cols = w_ada.shape[2]
    my = _index(_position())
    xs = x.reshape(seq, d)
    target = loss_target.reshape(seq, d)

    w_in_t, w_out_full, c_all, ada_parts = _gather_weights(w_in.transpose(0, 2, 1), w_out, c, w_ada)
    ada = ada_parts[:, :, 0, :].transpose(1, 0, 2).reshape(n_layers, 3 * d) + b_ada
    shift, scale1, gate = ada[:, None, 0:d], 1.0 + ada[:, None, d : 2 * d], ada[:, None, 2 * d : 3 * d]
    gain = norm_gain[:, None, :]

    w_s_m = w_s.astype(MXU_DTYPE)
    w_s_t = w_s_m.transpose(0, 1, 3, 2)
    b_st = jnp.repeat(b_s.transpose(0, 2, 1), HEAD_DIM, axis=2)
    q_gain2 = jnp.tile(q_gain, (1, 2))[:, None, :]
    k_gain2 = jnp.tile(k_gain, (1, 2))[:, None, :]

    xl, saved = xs, []
    for l in range(n_layers):
        pa, pb = _ln_proj_fwd(xl, gain[l], scale1[l], shift[l], w_in_t, l, f"ln_proj_fwd_{l}")
        o = _attn_fwd(pa, q_gain2[l], k_gain2[l], sink[l], f"attn_fwd_{l}")
        saved.append((xl, pa, pb, o))
        last = l == n_layers - 1
        out = _mix_out_fwd(pb, o, xl, gate[l], w_out_full, l, w_s_m[l], b_st[l], f"mix_out_fwd_{l}", target if last else None)
        if last:
            dx, sq_err = out
        else:
            xl = out

    g_w_in, g_w_out, small, d_ada_rows = [None] * n_layers, [None] * n_layers, [None] * n_layers, [None] * n_layers
    for l in reversed(range(n_layers)):
        x_l, pa, pb, o = saved[l]
        dpb, do, g_acc, d_ws, d_bs = _mix_out_bwd(dx, pb, o, gate[l], w_out_full, l, w_s_m[l], w_s_t[l], b_st[l], f"mix_out_bwd_{l}")
        dq, dkv, halo_prev, halo_next, d_qg, d_kg, d_sk = _attn_bwd(pa, o, do, q_gain2[l], k_gain2[l], sink[l], f"attn_bwd_{l}")
        dx, dkvb, c0, c1 = _proj_bwd_dx(x_l, dx, dq, dkv, halo_prev, halo_next, dpb, w_in_t, l, gain[l], scale1[l], f"proj_bwd_dx_{l}")
        dw_in_t = _proj_bwd_dw(x_l, gain[l], scale1[l], shift[l], dq, dkvb, dpb, f"proj_bwd_dw_{l}")
        dw_out, d_gate8 = _w_out_finish(g_acc, w_out_full, l, gate[l], f"w_out_finish_{l}")
        r_in, r_out = _reduce_scatter(
            dw_in_t.reshape(4, 2, w_cols, d), dw_out.reshape(4, 2, D_MIX // N_DEV, d), f"reduce_scatter_{l}"
        )
        g_w_in[l] = r_in.transpose(1, 0)
        g_w_out[l] = r_out
        c0s, c1s = c0.sum(axis=0), c1.sum(axis=0)
        d_ada_rows[l] = jnp.concatenate([c0s, norm_gain[l] * c1s, d_gate8.sum(axis=0)])
        small[l] = (
            scale1[l, 0] * c1s,
            d_qg.sum(axis=0).reshape(N_HEADS, HEAD_DIM).sum(axis=0),
            d_kg.sum(axis=0).reshape(2, HEAD_DIM).sum(axis=0),
            d_sk[0, 0:N_HEADS],
            d_bs.reshape(BLK, N_GROUPS, HEAD_DIM).sum(axis=2).transpose(1, 0),
            d_ws,
        )

    names = ("norm_gain", "q_gain", "k_gain", "sink", "b_s")
    stacked = [jnp.stack([small[l][t] for l in range(n_layers)]) for t in range(len(names))]
    d_ada = jnp.stack(d_ada_rows)
    packed, offsets = _pack_rows(stacked + [d_ada, sq_err[0, 0:1]])
    d_ws = jnp.stack([small[l][len(names)] for l in range(n_layers)]).reshape(-1, LANES)
    gathered, gathered_ws = _gather_small(packed, d_ws)
    no_weight = jnp.zeros((1,), F32)
    weights = (norm_gain, q_gain, k_gain, sink, b_s, b_ada, no_weight)
    moments_m = (m_norm_gain, m_q_gain, m_k_gain, m_sink, m_b_s, m_b_ada, no_weight)
    moments_v = (v_norm_gain, v_q_gain, v_k_gain, v_sink, v_b_s, v_b_ada, no_weight)
    w_pack, _ = _pack_rows(weights)
    m_pack, _ = _pack_rows(moments_m)
    v_pack, _ = _pack_rows(moments_v)
    shapes = [w.shape for w in weights]
    flat_ws = lambda a: a.reshape(-1, LANES)
    updated = _small_update(gathered, gathered_ws, w_pack, m_pack, v_pack, flat_ws(w_s), flat_ws(m_w_s), flat_ws(v_w_s))
    g_small, d_small, m_small, v_small = (_unpack_rows(p, offsets, shapes) for p in updated[0:4])
    ws_small = [p.reshape(w_s.shape) for p in updated[4:8]]
    loss = g_small[-1][0] * (0.5 / d)

    ada_off = offsets[-2]
    ada_n = -(-n_layers * 3 * d // (SUBLANES * LANES)) * SUBLANES
    d_ada_all = gathered[:, ada_off : ada_off + ada_n].reshape(N_DEV, -1)[:, : n_layers * 3 * d].reshape(N_DEV, n_layers, 3 * d)
    d_ada_cols = lax.dynamic_slice_in_dim(d_ada_all, my * ada_cols, ada_cols, axis=2)
    g_w_ada = _ada_weight_grad(c_all[:, 0, :], d_ada_cols.transpose(1, 0, 2))

    def update(w, g, m, v, name):
        shape = w.shape
        flat = lambda a: a.reshape(-1, shape[-1])
        return tuple(a.reshape(shape) for a in _adamw(flat(w), flat(g), flat(m), flat(v), name))

    g_w_in, g_w_out = jnp.stack(g_w_in), jnp.stack(g_w_out)
    upd_ada = update(w_ada, g_w_ada, m_w_ada, v_w_ada, "adamw_w_ada")
    upd_in = update(w_in, g_w_in, m_w_in, v_w_in, "adamw_w_in")
    upd_out = update(w_out, g_w_out, m_w_out, v_w_out, "adamw_w_out")

    def ordered(ada_, in_, out_, small_, ws):
        ng, qg, kg, sk, bs, ba, _ = small_
        return (ada_, ba, ng, in_, qg, kg, sk, ws, bs, out_)

    grads = ordered(g_w_ada, g_w_in, g_w_out, g_small, ws_small[0])
    deltas = ordered(upd_ada[0], upd_in[0], upd_out[0], d_small, ws_small[1])
    new_m = ordered(upd_ada[1], upd_in[1], upd_out[1], m_small, ws_small[2])
    new_v = ordered(upd_ada[2], upd_in[2], upd_out[2], v_small, ws_small[3])
    return (loss, dx.reshape(x.shape), *grads, *deltas, *new_m, *new_v)
```

```python
import functools

import jax
import jax.numpy as jnp
from jax import lax
from jax.experimental import pallas as pl
from jax.experimental.pallas import tpu as pltpu

F32 = jnp.float32
MXU_DTYPE = jnp.bfloat16
MESH_ID = pl.DeviceIdType.MESH

N_DEV = 8
HEAD_DIM = 64
N_HEADS = 8
Q_PER_KV = 4
D_ATTN = 512
D_KV = 128
D_GM = 512
N_GROUPS = 8
D_MIX = D_ATTN + D_GM
BLK = 128
LANES = 128
SUBLANES = 8
N_PAIRS = D_ATTN // LANES
D_QKV = D_ATTN + 2 * D_KV
D_REST = D_ATTN + 3 * D_GM
D_IN = D_QKV + D_REST
EPS = 1e-6
NEG_INF = -1e30
ALIBI_SLOPES = tuple(2.0 ** (-8.0 * (h + 1) / N_HEADS) for h in range(N_HEADS))
Q_SCALE = 1.0 / 8.0

ADAM_LR = 0.001
ADAM_B1 = 0.9
ADAM_B2 = 0.999
ADAM_EPS = 1e-08
ADAM_WD = 0.01
ADAM_STEP = 10

TOKEN_TILE = 512
VMEM_LIMIT_BYTES = 56 * 1024 * 1024


def _params(semantics=None):
    return pltpu.CompilerParams(dimension_semantics=semantics, vmem_limit_bytes=VMEM_LIMIT_BYTES)


def _dot(a, b):
    return jnp.dot(a, b, preferred_element_type=F32)


def _dot_nt(a, b):
    return lax.dot_general(a, b, (((1,), (1,)), ((), ())), preferred_element_type=F32)


def _dot_tn(a, b):
    return lax.dot_general(a, b, (((0,), (0,)), ((), ())), preferred_element_type=F32)


def _mx(v):
    return v.astype(MXU_DTYPE)


def _lane_lo(rows):
    return lax.broadcasted_iota(jnp.int32, (rows, LANES), 1) < HEAD_DIM


def _half_ones():
    r = lax.broadcasted_iota(jnp.int32, (LANES, LANES), 0) < HEAD_DIM
    c = lax.broadcasted_iota(jnp.int32, (LANES, LANES), 1) < HEAD_DIM
    return jnp.where(r == c, 1.0, 0.0).astype(jnp.bfloat16)


def _half_sum(v, ones):
    p1 = v.astype(jnp.bfloat16)
    p2 = (v - p1.astype(F32)).astype(jnp.bfloat16)
    return _dot(p1, ones) + _dot(p2, ones)


def _half_rms(v, ones):
    r = lax.rsqrt(_half_sum(v * v, ones) * (1.0 / HEAD_DIM) + EPS)
    return v * r, r


def _half_rms_bwd(dy, vhat, r, ones):
    return r * (dy - vhat * (_half_sum(vhat * dy, ones) * (1.0 / HEAD_DIM)))


def _group_rows(v):
    rows, n = v.shape
    return v.reshape(rows // SUBLANES, SUBLANES, n).sum(axis=0)


def _sigmoid(v):
    return 1.0 / (1.0 + jnp.exp(-v))


ROW_CHUNK = 32
VARIANT_HEADS = ((0, 2, 5, 7), (1, 3, 4, 6))
HEAD_SLOT = {h: (v, s) for v, heads in enumerate(VARIANT_HEADS) for s, h in enumerate(heads)}
STACK = Q_PER_KV * BLK


def _fill_attn_bias(bias_s):
    qi = lax.broadcasted_iota(jnp.int32, (BLK, 3 * BLK), 0)
    ci = lax.broadcasted_iota(jnp.int32, (BLK, 3 * BLK), 1)
    dist = jnp.abs(ci - BLK - qi)
    distf = dist.astype(F32)
    for h in range(N_HEADS):
        bias_s[h] = jnp.where(dist <= BLK, -(ALIBI_SLOPES[h] * distf), NEG_INF)


def _edge_mask(block, seq):
    kpos = (block - 1) * BLK + lax.broadcasted_iota(jnp.int32, (1, 3 * BLK), 1)
    return jnp.where((kpos >= 0) & (kpos < seq), 0.0, NEG_INF)


def _stage_queries(qn, lo_t, j, nb, qs):
    for a in range(2):
        v, slot = HEAD_SLOT[2 * j + a]
        qm = _mx(jnp.where(lo_t, qn, 0.0) if a == 0 else jnp.where(lo_t, 0.0, qn))
        for n in range(nb):
            qs[n, v, slot * BLK : (slot + 1) * BLK, :] = qm[n * BLK : (n + 1) * BLK]


def _unstack_pair(stacked, j, lo):
    (v0, s0), (v1, s1) = HEAD_SLOT[2 * j], HEAD_SLOT[2 * j + 1]
    return jnp.where(lo, stacked[v0][s0 * BLK : (s0 + 1) * BLK], stacked[v1][s1 * BLK : (s1 + 1) * BLK])


def _stage_keys(kvp_ref, qkv_ref, kvn_ref, kg, ones, tile, ks, kr, vs, vr, khat_s=None, rk_s=None):
    pieces = (
        (0, BLK, kvp_ref[:, 0:D_KV], kvp_ref[:, D_KV : 2 * D_KV]),
        (BLK, tile, qkv_ref[:, D_ATTN : D_ATTN + D_KV], qkv_ref[:, D_ATTN + D_KV : D_QKV]),
        (BLK + tile, BLK, kvn_ref[:, 0:D_KV], kvn_ref[:, D_KV : 2 * D_KV]),
    )
    for r0, n, k, v in pieces:
        khat, rk = _half_rms(k, ones)
        kn = khat * kg
        ks[r0 : r0 + n, :] = _mx(kn)
        kr[r0 : r0 + n, :] = _mx(pltpu.roll(kn, HEAD_DIM, 1))
        vs[r0 : r0 + n, :] = _mx(v)
        vr[r0 : r0 + n, :] = _mx(pltpu.roll(v, HEAD_DIM, 1))
        if khat_s is not None:
            khat_s[r0 : r0 + n, :] = khat
            rk_s[r0 : r0 + n, :] = rk


def _halo_specs(tile, seq):
    nb = tile // BLK
    last = seq // BLK - 1
    kv_col = D_ATTN // (2 * D_KV)
    prev = pl.BlockSpec((BLK, 2 * D_KV), lambda i: (jnp.maximum(i * nb - 1, 0), kv_col))
    nxt = pl.BlockSpec((BLK, 2 * D_KV), lambda i: (jnp.minimum((i + 1) * nb, last), kv_col))
    return prev, nxt


def _row_spec(tile, width):
    return pl.BlockSpec((tile, width), lambda i: (i, 0))


def _full_spec(shape):
    nd = len(shape)
    return pl.BlockSpec(shape, lambda i: (0,) * nd)


SMEM_SPEC = pl.BlockSpec(memory_space=pltpu.SMEM)
VMEM_SPEC = pl.BlockSpec(memory_space=pltpu.VMEM)
HBM_SPEC = pl.BlockSpec(memory_space=pltpu.HBM)


def _ln_proj_fwd(x, gain, scale1, shift, w_in_t, name):
    seq, d = x.shape
    tile = min(TOKEN_TILE, seq)

    def body(x_ref, g_ref, s1_ref, sh_ref, wt_ref, pa_ref, pb_ref):
        xv = x_ref[...]
        r = lax.rsqrt(jnp.mean(xv * xv, axis=-1, keepdims=True) + EPS)
        h = _mx((xv * r) * g_ref[...] * s1_ref[...] + sh_ref[...])
        pa_ref[...] = _dot_nt(h, wt_ref[0:D_QKV, :])
        pb_ref[...] = _dot_nt(h, wt_ref[D_QKV:D_IN, :])

    vec = _full_spec((1, d))
    return pl.pallas_call(
        body,
        name=name,
        grid=(seq // tile,),
        in_specs=[_row_spec(tile, d), vec, vec, vec, _full_spec((D_IN, d))],
        out_specs=[_row_spec(tile, D_QKV), _row_spec(tile, D_REST)],
        out_shape=[jax.ShapeDtypeStruct((seq, D_QKV), F32), jax.ShapeDtypeStruct((seq, D_REST), F32)],
        compiler_params=_params(("parallel",)),
    )(x, gain, scale1, shift, w_in_t)


def _rider_steps(nt):
    return 0, (2 * nt) // 3, nt - 1


def _attn_fwd(pa, q_gain2, k_gain2, sink, name, gather=None):
    seq = pa.shape[0]
    tile = min(TOKEN_TILE, seq)
    nb = tile // BLK
    nt = seq // tile
    ext = tile + 2 * BLK
    riding = gather is not None

    def body(sink_ref, qkv_ref, kvp_ref, kvn_ref, qg_ref, kg_ref, *rest):
        i = pl.program_id(0)
        if riding:
            shard_in, shard_out, o_ref, full_in, full_out = rest[0:5]
            qs, ks, kr, vs, vr, bias_s, s_scr, p_scr, inv_scr, send_sems, recv_sems, local_sems = rest[5:]
            start, forward, finish = _all_gather_stages(
                (_row_block(full_in, shard_in.shape[0]), _row_block(full_out, shard_out.shape[0])),
                send_sems,
                recv_sems,
                sources=(shard_in, shard_out),
                local_sems=local_sems,
            )
            at_start, at_forward, at_finish = _rider_steps(nt)
            pl.when(i == at_start)(start)
        else:
            o_ref, qs, ks, kr, vs, vr, bias_s, s_scr, p_scr, inv_scr = rest

        @pl.when(i == 0)
        def _():
            _fill_attn_bias(bias_s)

        ones = _half_ones()
        lo = _lane_lo(BLK)
        lo_t = _lane_lo(tile)
        _stage_keys(kvp_ref, qkv_ref, kvn_ref, kg_ref[...], ones, tile, ks, kr, vs, vr)
        for j in range(N_PAIRS):
            qhat, _ = _half_rms(qkv_ref[:, j * LANES : (j + 1) * LANES], ones)
            _stage_queries(qhat * (qg_ref[...] * Q_SCALE), lo_t, j, nb, qs)

        def block(n, carry):
            r0 = pl.multiple_of(n * BLK, BLK)
            krows = pl.ds(r0, 3 * BLK)
            edge = _edge_mask(i * nb + n, seq)
            for v in range(2):
                s_scr[v] = _dot_nt(qs[n, v], (kr if v else ks)[krows, :])
            for h in range(N_HEADS):
                v, slot = HEAD_SLOT[h]
                sink_h = sink_ref[h]
                for rc in range(0, BLK, ROW_CHUNK):
                    rows = slice(slot * BLK + rc, slot * BLK + rc + ROW_CHUNK)
                    s = s_scr[v, rows, :] + bias_s[h, rc : rc + ROW_CHUNK, :] + edge
                    m = jnp.maximum(jnp.max(s, axis=-1, keepdims=True), sink_h)
                    p = jnp.exp(s - m)
                    total = jnp.sum(p, axis=-1, keepdims=True) + jnp.exp(sink_h - m)
                    p_scr[v, rows, :] = _mx(p)
                    inv_scr[v, rows, :] = jnp.broadcast_to(1.0 / total, (ROW_CHUNK, LANES))
            outs = [_dot(p_scr[v], (vr if v else vs)[krows, :]) * inv_scr[v] for v in range(2)]
            for j in range(N_PAIRS):
                o_ref[pl.ds(r0, BLK), j * LANES : (j + 1) * LANES] = _unstack_pair(outs, j, lo)
            return carry

        lax.fori_loop(0, nb, block, 0)
        if riding:
            pl.when(i == at_forward)(forward)
            pl.when(i == at_finish)(finish)

    prev, nxt = _halo_specs(tile, seq)
    vec = _full_spec((1, LANES))
    in_specs = [SMEM_SPEC, _row_spec(tile, D_QKV), prev, nxt, vec, vec]
    out_specs = [_row_spec(tile, D_ATTN)]
    out_shape = [jax.ShapeDtypeStruct((seq, D_ATTN), F32)]
    scratch = [
        pltpu.VMEM((nb, 2, STACK, LANES), MXU_DTYPE),
        pltpu.VMEM((ext, LANES), MXU_DTYPE),
        pltpu.VMEM((ext, LANES), MXU_DTYPE),
        pltpu.VMEM((ext, LANES), MXU_DTYPE),
        pltpu.VMEM((ext, LANES), MXU_DTYPE),
        pltpu.VMEM((N_HEADS, BLK, 3 * BLK), F32),
        pltpu.VMEM((2, STACK, 3 * BLK), F32),
        pltpu.VMEM((2, STACK, 3 * BLK), MXU_DTYPE),
        pltpu.VMEM((2, STACK, LANES), F32),
    ]
    extra = ()
    if riding:
        extra = tuple(gather)
        in_specs += [HBM_SPEC] * 2
        out_specs += [HBM_SPEC] * 2
        out_shape += [jax.ShapeDtypeStruct((N_DEV * g.shape[0], g.shape[1]), g.dtype) for g in gather]
        scratch += [pltpu.SemaphoreType.DMA((14,)), pltpu.SemaphoreType.DMA((14,)), pltpu.SemaphoreType.DMA((2,))]
    out = pl.pallas_call(
        body,
        name=name,
        grid=(nt,),
        in_specs=in_specs,
        out_specs=out_specs,
        out_shape=out_shape,
        scratch_shapes=scratch,
        compiler_params=_params(("arbitrary",)),
    )(sink, pa, pa, pa, q_gain2, k_gain2, *extra)
    return out if riding else out[0]


def _mix_out_fwd(pb, o, x, gate, w_out, w_s, b_st, name, target=None):
    seq, d = x.shape
    tile = min(TOKEN_TILE, seq)
    nb = tile // BLK
    with_loss = target is not None

    def body(pb_ref, o_ref, x_ref, gate_ref, wo_ref, ws_ref, bs_ref, *rest):
        if with_loss:
            t_ref, xo_ref, acc_ref, y_s, vn_s = rest

            @pl.when(pl.program_id(0) == 0)
            def _():
                acc_ref[...] = jnp.zeros_like(acc_ref)
        else:
            xo_ref, y_s, vn_s = rest
        ones = _half_ones()
        lo = _lane_lo(BLK)
        ga = pb_ref[:, 0:D_ATTN]
        y_s[:, 0:D_ATTN] = _mx(o_ref[...] * (ga * _sigmoid(ga)))
        for j in range(N_PAIRS):
            cols = slice(2 * D_GM + j * LANES, 2 * D_GM + (j + 1) * LANES)
            vhat, _ = _half_rms(pb_ref[:, cols], ones)
            vn_s[:, j * LANES : (j + 1) * LANES] = _mx(vhat)

        def chunk(n, carry):
            rows = pl.ds(pl.multiple_of(n * BLK, BLK), BLK)
            for j in range(N_PAIRS):
                cols = slice(j * LANES, (j + 1) * LANES)
                vn = vn_s[rows, cols]
                sv = jnp.where(lo, _dot(ws_ref[2 * j], vn), _dot(ws_ref[2 * j + 1], vn)) + bs_ref[:, cols]
                u = pb_ref[rows, D_ATTN + j * LANES : D_ATTN + (j + 1) * LANES]
                gg = pb_ref[rows, D_ATTN + 2 * D_GM + j * LANES : D_ATTN + 2 * D_GM + (j + 1) * LANES]
                y_s[rows, D_ATTN + j * LANES : D_ATTN + (j + 1) * LANES] = _mx((u * sv) * (gg * _sigmoid(gg)))
            return carry

        lax.fori_loop(0, nb, chunk, 0)
        y = x_ref[...] + gate_ref[...] * _dot(y_s[...], wo_ref[...])
        if with_loss:
            e = y - t_ref[...]
            xo_ref[...] = e * (1.0 / d)
            acc_ref[...] += jnp.sum(jnp.sum(e * e, axis=-1, keepdims=True), axis=0, keepdims=True)
        else:
            xo_ref[...] = y

    row = _row_spec(tile, d)
    acc_shape = (SUBLANES, LANES)
    return pl.pallas_call(
        body,
        name=name,
        grid=(seq // tile,),
        in_specs=[
            _row_spec(tile, D_REST),
            _row_spec(tile, D_ATTN),
            row,
            _full_spec((1, d)),
            _full_spec((D_MIX, d)),
            _full_spec((N_GROUPS, BLK, BLK)),
            _full_spec((BLK, D_GM)),
        ]
        + ([row] if with_loss else []),
        out_specs=[row, _full_spec(acc_shape)] if with_loss else row,
        out_shape=[jax.ShapeDtypeStruct((seq, d), F32), jax.ShapeDtypeStruct(acc_shape, F32)]
        if with_loss
        else jax.ShapeDtypeStruct((seq, d), F32),
        scratch_shapes=[pltpu.VMEM((tile, D_MIX), MXU_DTYPE), pltpu.VMEM((tile, D_GM), MXU_DTYPE)],
        compiler_params=_params(("arbitrary",) if with_loss else ("parallel",)),
    )(pb, o, x, gate, w_out, w_s, b_st, *([target] if with_loss else []))


def _mix_out_bwd(dxn, pb, o, gate, w_out, w_s, w_s_t, b_st, name):
    seq, d = dxn.shape
    tile = min(TOKEN_TILE, seq)
    nb = tile // BLK

    def body(dxn_ref, pb_ref, o_ref, gate_ref, wo_ref, ws_ref, wst_ref, bs_ref,
             dpb_ref, do_ref, g_ref, dws_ref, dbs_ref, y_s, dy_s, vn_s, rv_s, vnb_s, sv_s, dsv_s, dvn_s):
        @pl.when(pl.program_id(0) == 0)
        def _():
            g_ref[...] = jnp.zeros_like(g_ref)
            dws_ref[...] = jnp.zeros_like(dws_ref)
            dbs_ref[...] = jnp.zeros_like(dbs_ref)

        ones = _half_ones()
        lo = _lane_lo(BLK)
        c_u = slice(D_ATTN, D_ATTN + D_GM)
        c_vg = slice(D_ATTN + D_GM, D_ATTN + 2 * D_GM)
        c_gg = slice(D_ATTN + 2 * D_GM, D_REST)
        dxv = dxn_ref[...]
        dy_s[...] = _dot_nt(_mx(dxv * gate_ref[...]), wo_ref[...])
        ga = pb_ref[:, 0:D_ATTN]
        sig = _sigmoid(ga)
        sil = ga * sig
        ov = o_ref[...]
        y_s[:, 0:D_ATTN] = _mx(ov * sil)
        da = dy_s[:, 0:D_ATTN]
        do_ref[...] = da * sil
        dpb_ref[:, 0:D_ATTN] = (da * ov * (sig * (1.0 + ga * (1.0 - sig)))).astype(dpb_ref.dtype)
        for j in range(N_PAIRS):
            cols = slice(j * LANES, (j + 1) * LANES)
            vhat, rv = _half_rms(pb_ref[:, 2 * D_GM + j * LANES : 2 * D_GM + (j + 1) * LANES], ones)
            vn_s[:, cols] = vhat
            rv_s[:, cols] = rv
            vnb_s[:, cols] = _mx(vhat)

        def spatial_fwd(n, carry):
            rows = pl.ds(pl.multiple_of(n * BLK, BLK), BLK)
            for j in range(N_PAIRS):
                cols = slice(j * LANES, (j + 1) * LANES)
                vn = vnb_s[rows, cols]
                sv_s[rows, cols] = jnp.where(lo, _dot(ws_ref[2 * j], vn), _dot(ws_ref[2 * j + 1], vn)) + bs_ref[:, cols]
            return carry

        lax.fori_loop(0, nb, spatial_fwd, 0)

        def gating(n, carry):
            rows = pl.ds(pl.multiple_of(n * BLK, BLK), BLK)
            sv = sv_s[rows, :]
            u = pb_ref[rows, c_u]
            gg = pb_ref[rows, c_gg]
            sg = _sigmoid(gg)
            silg = gg * sg
            m0 = u * sv
            y_s[rows, D_ATTN:D_MIX] = _mx(m0 * silg)
            dm = dy_s[rows, D_ATTN:D_MIX]
            dm0 = dm * silg
            dpb_ref[rows, c_gg] = (dm * m0 * (sg * (1.0 + gg * (1.0 - sg)))).astype(dpb_ref.dtype)
            dpb_ref[rows, c_u] = (dm0 * sv).astype(dpb_ref.dtype)
            dsv = dm0 * u
            dsv_s[rows, :] = _mx(dsv)
            dbs_ref[...] += dsv
            return carry

        lax.fori_loop(0, nb, gating, 0)

        def spatial_bwd(n, carry):
            rows = pl.ds(pl.multiple_of(n * BLK, BLK), BLK)
            for j in range(N_PAIRS):
                cols = slice(j * LANES, (j + 1) * LANES)
                dsv = dsv_s[rows, cols]
                dvn_s[rows, cols] = jnp.where(lo, _dot(wst_ref[2 * j], dsv), _dot(wst_ref[2 * j + 1], dsv))
            return carry

        lax.fori_loop(0, nb, spatial_bwd, 0)
        zero = jnp.zeros((BLK, LANES), MXU_DTYPE)
        for j in range(N_PAIRS):
            cols = slice(j * LANES, (j + 1) * LANES)
            chunks = [dsv_s[n * BLK : (n + 1) * BLK, cols] for n in range(nb)]
            vn_all = jnp.concatenate([vnb_s[n * BLK : (n + 1) * BLK, cols] for n in range(nb)], axis=1)
            dws_ref[2 * j] += _dot_nt(jnp.concatenate([jnp.where(lo, c, zero) for c in chunks], axis=1), vn_all)
            dws_ref[2 * j + 1] += _dot_nt(jnp.concatenate([jnp.where(lo, zero, c) for c in chunks], axis=1), vn_all)
            dpb_ref[:, D_ATTN + D_GM + j * LANES : D_ATTN + D_GM + (j + 1) * LANES] = _half_rms_bwd(
                dvn_s[:, cols], vn_s[:, cols], rv_s[:, cols], ones
            ).astype(dpb_ref.dtype)
        g_ref[...] += _dot_tn(y_s[...], _mx(dxv))

    return pl.pallas_call(
        body,
        name=name,
        grid=(seq // tile,),
        in_specs=[
            _row_spec(tile, d),
            _row_spec(tile, D_REST),
            _row_spec(tile, D_ATTN),
            _full_spec((1, d)),
            _full_spec((D_MIX, d)),
            _full_spec((N_GROUPS, BLK, BLK)),
            _full_spec((N_GROUPS, BLK, BLK)),
            _full_spec((BLK, D_GM)),
        ],
        out_specs=[
            _row_spec(tile, D_REST),
            _row_spec(tile, D_ATTN),
            _full_spec((D_MIX, d)),
            _full_spec((N_GROUPS, BLK, BLK)),
            _full_spec((BLK, D_GM)),
        ],
        out_shape=[
            jax.ShapeDtypeStruct((seq, D_REST), MXU_DTYPE),
            jax.ShapeDtypeStruct((seq, D_ATTN), F32),
            jax.ShapeDtypeStruct((D_MIX, d), F32),
            jax.ShapeDtypeStruct((N_GROUPS, BLK, BLK), F32),
            jax.ShapeDtypeStruct((BLK, D_GM), F32),
        ],
        scratch_shapes=[
            pltpu.VMEM((tile, D_MIX), MXU_DTYPE),
            pltpu.VMEM((tile, D_MIX), F32),
            pltpu.VMEM((tile, D_GM), F32),
            pltpu.VMEM((tile, D_GM), F32),
            pltpu.VMEM((tile, D_GM), MXU_DTYPE),
            pltpu.VMEM((tile, D_GM), F32),
            pltpu.VMEM((tile, D_GM), MXU_DTYPE),
            pltpu.VMEM((tile, D_GM), F32),
        ],
        compiler_params=_params(("arbitrary",)),
    )(dxn, pb, o, gate, w_out, w_s, w_s_t, b_st)


def _attn_bwd(pa, o, do, q_gain2, k_gain2, sink, name, scatter=None):
    seq = pa.shape[0]
    tile = min(TOKEN_TILE, seq)
    nb = tile // BLK
    nt = seq // tile
    ext = tile + 2 * BLK
    riding = scatter is not None
    n_ride = len(scatter) if riding else 0

    def body(sink_ref, qkv_ref, kvp_ref, kvn_ref, o_ref, do_ref, qg_ref, kg_ref, *rest):
        i = pl.program_id(0)
        blocks, rest = rest[:n_ride], rest[n_ride:]
        dq_ref, dkv_ref, hp_ref, hn_ref, dqg_ref, dkg_ref, dsk_ref = rest[:7]
        landing, rest = rest[7 : 7 + n_ride], rest[7 + n_ride :]
        (qs, dos, qhat_s, rq_s, ks, kr, vs, vr, khat_s, rk_s, dqn_s, dka, dva, bias_s, s_scr, dp_scr, p_scr, ds_scr) = rest[:18]
        if riding:
            start, finish = _scatter_stages(blocks, landing, *rest[18:])
            at_start, _, at_finish = _rider_steps(nt)
            pl.when(i == at_start)(start)

        @pl.when(i == 0)
        def _():
            dqg_ref[...] = jnp.zeros_like(dqg_ref)
            dkg_ref[...] = jnp.zeros_like(dkg_ref)
            dsk_ref[...] = jnp.zeros_like(dsk_ref)
            _fill_attn_bias(bias_s)

        ones = _half_ones()
        lo = _lane_lo(BLK)
        lo_t = _lane_lo(tile)
        lo_c = _lane_lo(ROW_CHUNK)
        qg = qg_ref[...] * Q_SCALE
        kg = kg_ref[...]
        _stage_keys(kvp_ref, qkv_ref, kvn_ref, kg, ones, tile, ks, kr, vs, vr, khat_s, rk_s)
        for j in range(N_PAIRS):
            cols = slice(j * LANES, (j + 1) * LANES)
            qhat, rq = _half_rms(qkv_ref[:, cols], ones)
            qhat_s[:, cols] = qhat
            rq_s[:, cols] = rq
            _stage_queries(qhat * qg, lo_t, j, nb, qs)
            _stage_queries(do_ref[:, cols], lo_t, j, nb, dos)
        dka[...] = jnp.zeros_like(dka)
        dva[...] = jnp.zeros_like(dva)
        head_lane = lax.broadcasted_iota(jnp.int32, (1, LANES), 1)

        def block(n, dsink):
            r0 = pl.multiple_of(n * BLK, BLK)
            krows = pl.ds(r0, 3 * BLK)
            edge = _edge_mask(i * nb + n, seq)
            for v in range(2):
                s_scr[v] = _dot_nt(qs[n, v], (kr if v else ks)[krows, :])
                dp_scr[v] = _dot_nt(dos[n, v], (vr if v else vs)[krows, :])
            for h in range(N_HEADS):
                v, slot = HEAD_SLOT[h]
                j, a = divmod(h, 2)
                cols = slice(j * LANES, (j + 1) * LANES)
                sink_h = sink_ref[h]
                sink_part = jnp.zeros((ROW_CHUNK, 1), F32)
                for rc in range(0, BLK, ROW_CHUNK):
                    rows = slice(slot * BLK + rc, slot * BLK + rc + ROW_CHUNK)
                    trows = pl.ds(pl.multiple_of(r0 + rc, ROW_CHUNK), ROW_CHUNK)
                    s = s_scr[v, rows, :] + bias_s[h, rc : rc + ROW_CHUNK, :] + edge
                    m = jnp.maximum(jnp.max(s, axis=-1, keepdims=True), sink_h)
                    p = jnp.exp(s - m)
                    e_sink = jnp.exp(sink_h - m)
                    inv = 1.0 / (jnp.sum(p, axis=-1, keepdims=True) + e_sink)
                    pn = p * inv
                    prod = do_ref[trows, cols] * o_ref[trows, cols]
                    prod = jnp.where(lo_c, prod, 0.0) if a == 0 else jnp.where(lo_c, 0.0, prod)
                    dcol = jnp.sum(prod, axis=-1, keepdims=True)
                    ds_scr[v, rows, :] = _mx(pn * (dp_scr[v, rows, :] - dcol))
                    p_scr[v, rows, :] = _mx(pn)
                    sink_part = sink_part + (e_sink * inv) * dcol
                dsink = dsink - jnp.where(head_lane == h, jnp.sum(sink_part, axis=0, keepdims=True), 0.0)
            dqv = []
            for v in range(2):
                dqv.append(_dot(ds_scr[v], (kr if v else ks)[krows, :]))
                dka[v, krows, :] += _dot_tn(ds_scr[v], qs[n, v])
                dva[v, krows, :] += _dot_tn(p_scr[v], dos[n, v])
            for j in range(N_PAIRS):
                dqn_s[pl.ds(r0, BLK), j * LANES : (j + 1) * LANES] = _unstack_pair(dqv, j, lo)
            return dsink

        dsink = lax.fori_loop(0, nb, block, jnp.zeros((1, LANES), F32))
        dsk_ref[...] += jnp.broadcast_to(dsink, (SUBLANES, LANES))
        for j in range(N_PAIRS):
            cols = slice(j * LANES, (j + 1) * LANES)
            dqn = dqn_s[:, cols]
            qhat = qhat_s[:, cols]
            dqg_ref[:, cols] += _group_rows(dqn * qhat) * Q_SCALE
            dq_ref[:, cols] = _half_rms_bwd(dqn * qg, qhat, rq_s[:, cols], ones).astype(dq_ref.dtype)
        dkn = dka[0] + pltpu.roll(dka[1], HEAD_DIM, 1)
        khat = khat_s[...]
        dkg_ref[...] += _group_rows(dkn * khat)
        dk = _half_rms_bwd(dkn * kg, khat, rk_s[...], ones)
        dv = dva[0] + pltpu.roll(dva[1], HEAD_DIM, 1)
        hp_ref[:, 0:D_KV] = dk[0:BLK]
        hp_ref[:, D_KV : 2 * D_KV] = dv[0:BLK]
        dkv_ref[:, 0:D_KV] = dk[BLK : BLK + tile]
        dkv_ref[:, D_KV : 2 * D_KV] = dv[BLK : BLK + tile]
        hn_ref[:, 0:D_KV] = dk[BLK + tile : ext]
        hn_ref[:, D_KV : 2 * D_KV] = dv[BLK + tile : ext]
        if riding:
            pl.when(i == at_finish)(finish)

    prev, nxt = _halo_specs(tile, seq)
    vec = _full_spec((1, LANES))
    halo = pl.BlockSpec((None, BLK, 2 * D_KV), lambda i: (i, 0, 0))
    ride_in = list(scatter) if riding else []
    ride_out = [jax.ShapeDtypeStruct((N_DEV,) + b.shape[2:], b.dtype) for b in ride_in]
    ride_sems = [pltpu.SemaphoreType.DMA((7 * n_ride,)), pltpu.SemaphoreType.DMA((7 * n_ride,)), pltpu.SemaphoreType.DMA((n_ride,))]
    return pl.pallas_call(
        body,
        name=name,
        grid=(nt,),
        in_specs=[SMEM_SPEC, _row_spec(tile, D_QKV), prev, nxt, _row_spec(tile, D_ATTN), _row_spec(tile, D_ATTN), vec, vec]
        + [HBM_SPEC] * n_ride,
        out_specs=[
            _row_spec(tile, D_ATTN),
            _row_spec(tile, 2 * D_KV),
            halo,
            halo,
            _full_spec((SUBLANES, D_ATTN)),
            _full_spec((SUBLANES, LANES)),
            _full_spec((SUBLANES, LANES)),
        ]
        + [HBM_SPEC] * n_ride,
        out_shape=[
            jax.ShapeDtypeStruct((seq, D_ATTN), MXU_DTYPE),
            jax.ShapeDtypeStruct((seq, 2 * D_KV), F32),
            jax.ShapeDtypeStruct((nt, BLK, 2 * D_KV), F32),
            jax.ShapeDtypeStruct((nt, BLK, 2 * D_KV), F32),
            jax.ShapeDtypeStruct((SUBLANES, D_ATTN), F32),
            jax.ShapeDtypeStruct((SUBLANES, LANES), F32),
            jax.ShapeDtypeStruct((SUBLANES, LANES), F32),
        ]
        + ride_out,
        scratch_shapes=[
            pltpu.VMEM((nb, 2, STACK, LANES), MXU_DTYPE),
            pltpu.VMEM((nb, 2, STACK, LANES), MXU_DTYPE),
            pltpu.VMEM((tile, D_ATTN), F32),
            pltpu.VMEM((tile, D_ATTN), F32),
            pltpu.VMEM((ext, LANES), MXU_DTYPE),
            pltpu.VMEM((ext, LANES), MXU_DTYPE),
            pltpu.VMEM((ext, LANES), MXU_DTYPE),
            pltpu.VMEM((ext, LANES), MXU_DTYPE),
            pltpu.VMEM((ext, LANES), F32),
            pltpu.VMEM((ext, LANES), F32),
            pltpu.VMEM((tile, D_ATTN), F32),
            pltpu.VMEM((2, ext, LANES), F32),
            pltpu.VMEM((2, ext, LANES), F32),
            pltpu.VMEM((N_HEADS, BLK, 3 * BLK), F32),
            pltpu.VMEM((2, STACK, 3 * BLK), F32),
            pltpu.VMEM((2, STACK, 3 * BLK), F32),
            pltpu.VMEM((2, STACK, 3 * BLK), MXU_DTYPE),
            pltpu.VMEM((2, STACK, 3 * BLK), MXU_DTYPE),
        ]
        + (ride_sems if riding else []),
        compiler_params=_params(("arbitrary",)),
    )(sink, pa, pa, pa, o, do, q_gain2, k_gain2, *ride_in)


def _halo_in_specs(tile, nt):
    from_prev = pl.BlockSpec((None, BLK, 2 * D_KV), lambda i: (jnp.maximum(i - 1, 0), 0, 0))
    from_next = pl.BlockSpec((None, BLK, 2 * D_KV), lambda i: (jnp.minimum(i + 1, nt - 1), 0, 0))
    return from_prev, from_next


def _proj_bwd_dx(x, dxn, dq, dkv, halo_prev, halo_next, dpb, w_in_t, gain, scale1, name):
    seq, d = x.shape
    tile = min(TOKEN_TILE, seq)
    nt = seq // tile

    def body(x_ref, dxn_ref, dq_ref, dkv_ref, hn_ref, hp_ref, dpb_ref, wt_ref, g_ref, s1_ref,
             dx_ref, dkvb_ref, c0_ref, c1_ref):
        i = pl.program_id(0)

        @pl.when(i == 0)
        def _():
            c0_ref[...] = jnp.zeros_like(c0_ref)
            c1_ref[...] = jnp.zeros_like(c1_ref)

        top = dkv_ref[0:BLK, :] + jnp.where(i > 0, hn_ref[...], 0.0)
        bot = dkv_ref[tile - BLK : tile, :] + jnp.where(i < nt - 1, hp_ref[...], 0.0)
        if tile == BLK:
            dkvb_ref[...] = (top + bot - dkv_ref[...]).astype(dkvb_ref.dtype)
        else:
            dkvb_ref[0:BLK, :] = top.astype(dkvb_ref.dtype)
            dkvb_ref[tile - BLK : tile, :] = bot.astype(dkvb_ref.dtype)
            if tile > 2 * BLK:
                dkvb_ref[BLK : tile - BLK, :] = dkv_ref[BLK : tile - BLK, :].astype(dkvb_ref.dtype)
        dh = (
            _dot(dq_ref[...], wt_ref[0:D_ATTN, :])
            + _dot(dkvb_ref[...], wt_ref[D_ATTN:D_QKV, :])
            + _dot(dpb_ref[...], wt_ref[D_QKV:D_IN, :])
        )
        xv = x_ref[...]
        r = lax.rsqrt(jnp.mean(xv * xv, axis=-1, keepdims=True) + EPS)
        xn = xv * r
        c0_ref[...] += _group_rows(dh)
        c1_ref[...] += _group_rows(dh * xn)
        dxn_ = dh * (g_ref[...] * s1_ref[...])
        dx_ref[...] = dxn_ref[...] + r * (dxn_ - xn * jnp.mean(xn * dxn_, axis=-1, keepdims=True))

    from_prev, from_next = _halo_in_specs(tile, nt)
    vec = _full_spec((1, d))
    return pl.pallas_call(
        body,
        name=name,
        grid=(nt,),
        in_specs=[
            _row_spec(tile, d),
            _row_spec(tile, d),
            _row_spec(tile, D_ATTN),
            _row_spec(tile, 2 * D_KV),
            from_prev,
            from_next,
            _row_spec(tile, D_REST),
            _full_spec((D_IN, d)),
            vec,
            vec,
        ],
        out_specs=[_row_spec(tile, d), _row_spec(tile, 2 * D_KV), _full_spec((SUBLANES, d)), _full_spec((SUBLANES, d))],
        out_shape=[
            jax.ShapeDtypeStruct((seq, d), F32),
            jax.ShapeDtypeStruct((seq, 2 * D_KV), MXU_DTYPE),
            jax.ShapeDtypeStruct((SUBLANES, d), F32),
            jax.ShapeDtypeStruct((SUBLANES, d), F32),
        ],
        compiler_params=_params(("arbitrary",)),
    )(x, dxn, dq, dkv, halo_next, halo_prev, dpb, w_in_t, gain, scale1)


def _proj_bwd_dw(x, gain, scale1, shift, dq, dkvb, dpb, out_dtype, name):
    seq, d = x.shape
    tile = min(TOKEN_TILE, seq)
    nt = seq // tile
    direct = out_dtype == F32

    def body(x_ref, g_ref, s1_ref, sh_ref, dq_ref, dkv_ref, dpb_ref, dw_ref, *scratch):
        acc = dw_ref if direct else scratch[0]

        @pl.when(pl.program_id(0) == 0)
        def _():
            acc[...] = jnp.zeros_like(acc)

        xv = x_ref[...]
        r = lax.rsqrt(jnp.mean(xv * xv, axis=-1, keepdims=True) + EPS)
        h = _mx((xv * r) * g_ref[...] * s1_ref[...] + sh_ref[...])
        acc[0:D_ATTN, :] += _dot_tn(dq_ref[...], h)
        acc[D_ATTN:D_QKV, :] += _dot_tn(dkv_ref[...], h)
        acc[D_QKV:D_IN, :] += _dot_tn(dpb_ref[...], h)
        if not direct:

            @pl.when(pl.program_id(0) == nt - 1)
            def _():
                dw_ref[...] = acc[...].astype(out_dtype)

    vec = _full_spec((1, d))
    return pl.pallas_call(
        body,
        name=name,
        grid=(nt,),
        in_specs=[_row_spec(tile, d), vec, vec, vec, _row_spec(tile, D_ATTN), _row_spec(tile, 2 * D_KV), _row_spec(tile, D_REST)],
        out_specs=_full_spec((D_IN, d)),
        out_shape=jax.ShapeDtypeStruct((D_IN, d), out_dtype),
        scratch_shapes=[] if direct else [pltpu.VMEM((D_IN, d), F32)],
        compiler_params=_params(("arbitrary",)),
    )(x, gain, scale1, shift, dq, dkvb, dpb)


def _w_out_finish(g, w_out, gate, out_dtype, name):
    d_mix, d = g.shape

    def body(g_ref, w_ref, gate_ref, dw_ref, dgate_ref):
        gv = g_ref[...]
        dw_ref[...] = (gv * gate_ref[...]).astype(out_dtype)
        dgate_ref[...] = _group_rows(gv * w_ref[...].astype(F32))

    return pl.pallas_call(
        body,
        name=name,
        in_specs=[VMEM_SPEC, VMEM_SPEC, VMEM_SPEC],
        out_specs=[VMEM_SPEC, VMEM_SPEC],
        out_shape=[jax.ShapeDtypeStruct((d_mix, d), out_dtype), jax.ShapeDtypeStruct((SUBLANES, d), F32)],
        compiler_params=_params(),
    )(g, w_out, gate)


def _adamw_math(w, g, m, v):
    m = ADAM_B1 * m + (1.0 - ADAM_B1) * g
    v = ADAM_B2 * v + (1.0 - ADAM_B2) * (g * g)
    m_hat = m / (1.0 - ADAM_B1**ADAM_STEP)
    v_hat = v / (1.0 - ADAM_B2**ADAM_STEP)
    delta = -ADAM_LR * (m_hat / (jnp.sqrt(v_hat) + ADAM_EPS) + ADAM_WD * w)
    return delta, m, v


def _adamw(w, g, m, v, name):
    rows, cols = w.shape
    tile = min(TOKEN_TILE, rows)

    def body(w_ref, g_ref, m_ref, v_ref, d_ref, mo_ref, vo_ref):
        d_ref[...], mo_ref[...], vo_ref[...] = _adamw_math(w_ref[...], g_ref[...], m_ref[...], v_ref[...])

    spec = _row_spec(tile, cols)
    shape = jax.ShapeDtypeStruct((rows, cols), F32)
    return pl.pallas_call(
        body,
        name=name,
        grid=(rows // tile,),
        in_specs=[spec] * 4,
        out_specs=[spec] * 3,
        out_shape=[shape] * 3,
        compiler_params=_params(("parallel",)),
    )(w, g, m, v)


def _small_update(gathered, gathered_ws, w, m, v, ws, m_ws, v_ws):
    def body(ga_ref, gws_ref, w_ref, m_ref, v_ref, ws_ref, mws_ref, vws_ref, *outs):
        for src, refs, out in ((ga_ref, (w_ref, m_ref, v_ref), outs[0:4]), (gws_ref, (ws_ref, mws_ref, vws_ref), outs[4:8])):
            g = src[0].astype(F32)
            for j in range(1, N_DEV):
                g = g + src[j].astype(F32)
            out[0][...] = g
            out[1][...], out[2][...], out[3][...] = _adamw_math(refs[0][...], g, refs[1][...], refs[2][...])

    shapes = [jax.ShapeDtypeStruct(w.shape, F32)] * 4 + [jax.ShapeDtypeStruct(ws.shape, F32)] * 4
    return pl.pallas_call(
        body,
        name="small_update",
        in_specs=[VMEM_SPEC] * 8,
        out_specs=[VMEM_SPEC] * 8,
        out_shape=shapes,
        compiler_params=_params(),
    )(gathered, gathered_ws, w, m, v, ws, m_ws, v_ws)


def _ada_weight_grad(c_all, d_ada_cols):
    d = c_all.shape[-1]
    n_layers, _, width = d_ada_cols.shape

    def body(c_ref, da_ref, dw_ref):
        cv = c_ref[...]
        cond = cv * _sigmoid(cv)
        for l in range(n_layers):
            dw_ref[l] = lax.dot_general(
                cond, da_ref[l], (((0,), (0,)), ((), ())), preferred_element_type=F32, precision=lax.Precision.HIGHEST
            )

    return pl.pallas_call(
        body,
        name="ada_weight_grad",
        in_specs=[VMEM_SPEC, VMEM_SPEC],
        out_specs=VMEM_SPEC,
        out_shape=jax.ShapeDtypeStruct((n_layers, d, width), F32),
        compiler_params=_params(),
    )(c_all, d_ada_cols)


def _position():
    return lax.axis_index("x"), lax.axis_index("y"), lax.axis_index("c")


def _flip(pos, k):
    x, y, c = pos
    return (1 - x if k & 4 else x, 1 - y if k & 2 else y, 1 - c if k & 1 else c)


def _index(pos):
    x, y, c = pos
    return 4 * x + 2 * y + c


def _remote(src, dst, send_sem, recv_sem, to):
    return pltpu.make_async_remote_copy(
        src_ref=src, dst_ref=dst, send_sem=send_sem, recv_sem=recv_sem, device_id=to, device_id_type=MESH_ID
    )


def _all_gather_stages(slots, send_sems, recv_sems, sources=None, local_sems=None):
    me = _position()
    sibling = _flip(me, 1)
    others = (4, 2, 6)
    arrays = range(len(slots))

    def copy(t, k, block, to, own=False):
        slot = slots[t](_index(block))
        src = sources[t] if own and sources is not None else slot
        return _remote(src, slot, send_sems.at[7 * t + k], recv_sems.at[7 * t + k], to)

    def first(t):
        return [copy(t, 0, me, sibling, own=True)] + [copy(t, 1 + j, me, _flip(me, f), own=True) for j, f in enumerate(others)]

    def passed(t, j):
        return copy(t, 4 + j, _flip(me, others[j]), sibling)

    def local(t):
        return pltpu.make_async_copy(sources[t], slots[t](_index(me)), local_sems.at[t])

    def start():
        for t in arrays:
            if sources is not None:
                local(t).start()
            for cp in first(t):
                cp.start()

    def forward():
        for j, f in enumerate(others):
            for t in arrays:
                copy(t, 1 + j, _flip(me, f), me).wait_recv()
                passed(t, j).start()

    def finish():
        for t in arrays:
            copy(t, 0, sibling, me).wait_recv()
            for j, f in enumerate(others):
                copy(t, 4 + j, _flip(sibling, f), me).wait_recv()
        for t in arrays:
            for cp in first(t) + [passed(t, j) for j in range(len(others))]:
                cp.wait_send()
            if sources is not None:
                local(t).wait()

    return start, forward, finish


def _two_level_all_gather(slots, send_sems, recv_sems, between=None):
    start, forward, finish = _all_gather_stages(slots, send_sems, recv_sems)
    start()
    if between is not None:
        between()
    forward()
    finish()


def _row_block(ref, rows):
    return lambda j: ref.at[pl.ds(pl.multiple_of(j * rows, 16), rows), :]


def _scatter_stages(blocks, landing, send_sems, recv_sems, local_sems):
    me = _position()
    my = _index(me)
    arrays = range(len(blocks))

    def copy(t, k):
        px, py, pc = to = _flip(me, k)
        return _remote(blocks[t].at[2 * px + py, pc], landing[t].at[my], send_sems.at[7 * t + k - 1], recv_sems.at[7 * t + k - 1], to)

    def arrival(t, k):
        slot = landing[t].at[_index(_flip(me, k))]
        return _remote(slot, slot, send_sems.at[7 * t + k - 1], recv_sems.at[7 * t + k - 1], _flip(me, k))

    def local(t):
        x, y, c = me
        return pltpu.make_async_copy(blocks[t].at[2 * x + y, c], landing[t].at[my], local_sems.at[t])

    def start():
        for t in arrays:
            local(t).start()
            for k in range(1, N_DEV):
                copy(t, k).start()

    def finish():
        for t in arrays:
            for k in range(1, N_DEV):
                arrival(t, k).wait_recv()
        for t in arrays:
            for k in range(1, N_DEV):
                copy(t, k).wait_send()
            local(t).wait()

    return start, finish


def _ada_exchange(c_ref, w_ref, call_ref, parts_ref, sbuf, sem_s1, sem_r1, sem_s2, sem_r2):
    d = c_ref.shape[-1]
    n_layers = w_ref.shape[0]
    me = _position()
    my = _index(me)
    call_ref[my] = jnp.broadcast_to(c_ref[...], (SUBLANES, d))
    mine = call_ref.at[my]
    first = [_remote(mine, mine, sem_s1.at[k - 1], sem_r1.at[k - 1], _flip(me, k)) for k in range(1, N_DEV)]
    for cp in first:
        cp.start()
    for k in range(1, N_DEV):
        theirs = call_ref.at[_index(_flip(me, k))]
        _remote(theirs, theirs, sem_s1.at[k - 1], sem_r1.at[k - 1], _flip(me, k)).wait_recv()
    for b in range(N_DEV):
        cv = call_ref[b]
        cond = cv * _sigmoid(cv)
        for l in range(n_layers):
            sbuf[b, l] = jnp.dot(cond, w_ref[l], preferred_element_type=F32, precision=lax.Precision.HIGHEST)
    parts_ref[my] = sbuf[my]
    second = []
    for k in range(1, N_DEV):
        to = _flip(me, k)
        second.append(_remote(sbuf.at[_index(to)], parts_ref.at[my], sem_s2.at[k - 1], sem_r2.at[k - 1], to))
    for cp in second:
        cp.start()
    for k in range(1, N_DEV):
        theirs = parts_ref.at[_index(_flip(me, k))]
        _remote(theirs, theirs, sem_s2.at[k - 1], sem_r2.at[k - 1], _flip(me, k)).wait_recv()
    for cp in first + second:
        cp.wait_send()


def _gather_weights(w_in_t, w_out, c_row, w_ada):
    n_layers, rows_in, d = w_in_t.shape
    rows_out = w_out.shape[1]
    width = w_ada.shape[2]

    def body(wi_ref, wo_ref, c_ref, wa_ref, gi_ref, go_ref, si_ref, so_ref, call_ref, parts_ref, sbuf, send_sems, recv_sems, *ada_sems):
        my = _index(_position())
        si_ref[...] = wi_ref[...].astype(si_ref.dtype)
        so_ref[...] = wo_ref[...].astype(so_ref.dtype)
        gi_ref[pl.ds(pl.multiple_of(my * rows_in, 16), rows_in), :] = si_ref[0]
        go_ref[pl.ds(pl.multiple_of(my * rows_out, 16), rows_out), :] = so_ref[0]
        _two_level_all_gather(
            (_row_block(gi_ref, rows_in), _row_block(go_ref, rows_out)),
            send_sems,
            recv_sems,
            between=functools.partial(_ada_exchange, c_ref, wa_ref, call_ref, parts_ref, sbuf, *ada_sems),
        )

    return pl.pallas_call(
        body,
        name="gather_weights",
        in_specs=[VMEM_SPEC] * 4,
        out_specs=[VMEM_SPEC] * 6,
        out_shape=[
            jax.ShapeDtypeStruct((N_DEV * rows_in, d), MXU_DTYPE),
            jax.ShapeDtypeStruct((N_DEV * rows_out, d), MXU_DTYPE),
            jax.ShapeDtypeStruct(w_in_t.shape, MXU_DTYPE),
            jax.ShapeDtypeStruct(w_out.shape, MXU_DTYPE),
            jax.ShapeDtypeStruct((N_DEV, SUBLANES, d), F32),
            jax.ShapeDtypeStruct((N_DEV, n_layers, SUBLANES, width), F32),
        ],
        scratch_shapes=[
            pltpu.VMEM((N_DEV, n_layers, SUBLANES, width), F32),
            pltpu.SemaphoreType.DMA((14,)),
            pltpu.SemaphoreType.DMA((14,)),
        ]
        + [pltpu.SemaphoreType.DMA((N_DEV - 1,))] * 4,
        compiler_params=_params(),
    )(w_in_t, w_out, c_row, w_ada)


def _gather_small(packed, d_ws):
    def body(p_ref, ws_ref, g_ref, gws_ref, send_sems, recv_sems):
        my = _index(_position())
        g_ref[my] = p_ref[...]
        gws_ref[my] = ws_ref[...].astype(gws_ref.dtype)
        _two_level_all_gather((lambda j: g_ref.at[j], lambda j: gws_ref.at[j]), send_sems, recv_sems)

    return pl.pallas_call(
        body,
        name="gather_small",
        in_specs=[VMEM_SPEC, VMEM_SPEC],
        out_specs=[VMEM_SPEC, VMEM_SPEC],
        out_shape=[
            jax.ShapeDtypeStruct((N_DEV,) + packed.shape, F32),
            jax.ShapeDtypeStruct((N_DEV,) + d_ws.shape, jnp.bfloat16),
        ],
        scratch_shapes=[pltpu.SemaphoreType.DMA((14,)), pltpu.SemaphoreType.DMA((14,))],
        compiler_params=_params(),
    )(packed, d_ws)


def _reduce_scatter(blocks_a, blocks_b, name):
    wire = jnp.bfloat16

    def body(a_ref, b_ref, oa_ref, ob_ref, stage_a, stage_b, half_a, half_b, send_a, send_b, chips_a, chips_b, send_sems, recv_sems):
        me = _position()
        x, y, c = me
        sibling = _flip(me, 1)
        my_chip = 2 * x + y
        others = (4, 2, 6)
        arrays = ((a_ref, stage_a, half_a, send_a, chips_a, oa_ref), (b_ref, stage_b, half_b, send_b, chips_b, ob_ref))
        to_sibling = []
        for t, (src, stage, half, _, _, _) in enumerate(arrays):
            for chip in range(4):
                stage[chip] = src[chip, 1 - c].astype(wire)
            cp = _remote(stage, half, send_sems.at[4 * t], recv_sems.at[4 * t], sibling)
            cp.start()
            to_sibling.append(cp)
        to_chips = []
        for t, (src, _, half, send, chips, _) in enumerate(arrays):
            to_sibling[t].wait_recv()
            chips[my_chip] = (src[my_chip, c] + half[my_chip].astype(F32)).astype(wire)
            for j, f in enumerate(others):
                px, py, _ = _flip(me, f)
                chip = 2 * px + py
                send[j] = (src[chip, c] + half[chip].astype(F32)).astype(wire)
                cp = _remote(send.at[j], chips.at[my_chip], send_sems.at[4 * t + 1 + j], recv_sems.at[4 * t + 1 + j], _flip(me, f))
                cp.start()
                to_chips.append(cp)
        for t, (_, _, _, _, chips, out) in enumerate(arrays):
            for j, f in enumerate(others):
                px, py, _ = _flip(me, f)
                slot = chips.at[2 * px + py]
                _remote(slot, slot, send_sems.at[4 * t + 1 + j], recv_sems.at[4 * t + 1 + j], _flip(me, f)).wait_recv()
            out[...] = ((chips[0].astype(F32) + chips[1].astype(F32)) + chips[2].astype(F32)) + chips[3].astype(F32)
        for cp in to_sibling + to_chips:
            cp.wait_send()

    def scratch(blocks):
        blk = blocks.shape[2:]
        return [pltpu.VMEM((4,) + blk, wire), pltpu.VMEM((4,) + blk, wire), pltpu.VMEM((3,) + blk, wire), pltpu.VMEM((4,) + blk, wire)]

    sa, sb = scratch(blocks_a), scratch(blocks_b)
    return pl.pallas_call(
        body,
        name=name,
        in_specs=[VMEM_SPEC, VMEM_SPEC],
        out_specs=[VMEM_SPEC, VMEM_SPEC],
        out_shape=[jax.ShapeDtypeStruct(blocks_a.shape[2:], F32), jax.ShapeDtypeStruct(blocks_b.shape[2:], F32)],
        scratch_shapes=[sa[0], sb[0], sa[1], sb[1], sa[2], sb[2], sa[3], sb[3], pltpu.SemaphoreType.DMA((8,)), pltpu.SemaphoreType.DMA((8,))],
        compiler_params=_params(),
    )(blocks_a, blocks_b)


def _scatter_finish(landed_a, landed_b, name):
    def body(a_ref, b_ref, oa_ref, ob_ref):
        for src, out in ((a_ref, oa_ref), (b_ref, ob_ref)):
            g = src[0].astype(F32)
            for j in range(1, N_DEV):
                g = g + src[j].astype(F32)
            out[...] = g

    return pl.pallas_call(
        body,
        name=name,
        in_specs=[VMEM_SPEC, VMEM_SPEC],
        out_specs=[VMEM_SPEC, VMEM_SPEC],
        out_shape=[jax.ShapeDtypeStruct(landed_a.shape[1:], F32), jax.ShapeDtypeStruct(landed_b.shape[1:], F32)],
        compiler_params=_params(),
    )(landed_a, landed_b)


def _pack_rows(parts):
    rows, offsets, at = [], [], 0
    for p in parts:
        flat = p.reshape(-1)
        n = -(-flat.shape[0] // (SUBLANES * LANES)) * SUBLANES
        rows.append(jnp.pad(flat, (0, n * LANES - flat.shape[0])).reshape(n, LANES))
        offsets.append(at)
        at += n
    return jnp.concatenate(rows, axis=0), offsets


def _unpack_rows(packed, offsets, shapes):
    out = []
    for off, shape in zip(offsets, shapes):
        size = 1
        for s in shape:
            size *= s
        n = -(-size // (SUBLANES * LANES)) * SUBLANES
        out.append(packed[off : off + n].reshape(-1)[:size].reshape(shape))
    return out


def kernel(x, c, w_ada, b_ada, norm_gain, w_in, q_gain, k_gain, sink, w_s, b_s, w_out, loss_target, m_w_ada, m_b_ada, m_norm_gain, m_w_in, m_q_gain, m_k_gain, m_sink, m_w_s, m_b_s, m_w_out, v_w_ada, v_b_ada, v_norm_gain, v_w_in, v_q_gain, v_k_gain, v_sink, v_w_s, v_b_s, v_w_out):
    seq, d = x.shape[1], x.shape[2]
    n_layers = w_in.shape[0]
    w_cols = w_in.shape[2]
    ada_cols = w_ada.shape[2]
    my = _index(_position())
    xs = x.reshape(seq, d)
    target = loss_target.reshape(seq, d)

    w_in_t0, w_out0, shard_in, shard_out, c_all, ada_parts = _gather_weights(w_in.transpose(0, 2, 1), w_out, c, w_ada)
    w_in_ts, w_outs = [w_in_t0], [w_out0]
    ada = ada_parts[:, :, 0, :].transpose(1, 0, 2).reshape(n_layers, 3 * d) + b_ada
    shift, scale1, gate = ada[:, None, 0:d], 1.0 + ada[:, None, d : 2 * d], ada[:, None, 2 * d : 3 * d]
    gain = norm_gain[:, None, :]

    w_s_m = w_s.astype(MXU_DTYPE)
    w_s_t = w_s_m.transpose(0, 1, 3, 2)
    b_st = jnp.repeat(b_s.transpose(0, 2, 1), HEAD_DIM, axis=2)
    q_gain2 = jnp.tile(q_gain, (1, 2))[:, None, :]
    k_gain2 = jnp.tile(k_gain, (1, 2))[:, None, :]

    xl, saved = xs, []
    for l in range(n_layers):
        last = l == n_layers - 1
        pa, pb = _ln_proj_fwd(xl, gain[l], scale1[l], shift[l], w_in_ts[l], f"ln_proj_fwd_{l}")
        if last:
            o = _attn_fwd(pa, q_gain2[l], k_gain2[l], sink[l], f"attn_fwd_{l}")
        else:
            o, w_in_next, w_out_next = _attn_fwd(
                pa, q_gain2[l], k_gain2[l], sink[l], f"attn_fwd_{l}", gather=(shard_in[l + 1], shard_out[l + 1])
            )
            w_in_ts.append(w_in_next)
            w_outs.append(w_out_next)
        saved.append((xl, pa, pb, o))
        out = _mix_out_fwd(pb, o, xl, gate[l], w_outs[l], w_s_m[l], b_st[l], f"mix_out_fwd_{l}", target if last else None)
        if last:
            dx, sq_err = out
        else:
            xl = out

    g_w_in, g_w_out, small, d_ada_rows = [None] * n_layers, [None] * n_layers, [None] * n_layers, [None] * n_layers
    pending = None
    for l in reversed(range(n_layers)):
        x_l, pa, pb, o = saved[l]
        dpb, do, g_acc, d_ws, d_bs = _mix_out_bwd(dx, pb, o, gate[l], w_outs[l], w_s_m[l], w_s_t[l], b_st[l], f"mix_out_bwd_{l}")
        attn = _attn_bwd(pa, o, do, q_gain2[l], k_gain2[l], sink[l], f"attn_bwd_{l}", scatter=pending)
        dq, dkv, halo_prev, halo_next, d_qg, d_kg, d_sk = attn[:7]
        if pending is not None:
            r_in, r_out = _scatter_finish(attn[7], attn[8], f"scatter_finish_{l + 1}")
            g_w_in[l + 1], g_w_out[l + 1] = r_in.transpose(1, 0), r_out
        dx, dkvb, c0, c1 = _proj_bwd_dx(x_l, dx, dq, dkv, halo_prev, halo_next, dpb, w_in_ts[l], gain[l], scale1[l], f"proj_bwd_dx_{l}")
        wire = jnp.bfloat16 if l > 0 else F32
        dw_in_t = _proj_bwd_dw(x_l, gain[l], scale1[l], shift[l], dq, dkvb, dpb, wire, f"proj_bwd_dw_{l}")
        dw_out, d_gate8 = _w_out_finish(g_acc, w_outs[l], gate[l], wire, f"w_out_finish_{l}")
        blocks = (dw_in_t.reshape(4, 2, w_cols, d), dw_out.reshape(4, 2, D_MIX // N_DEV, d))
        if l > 0:
            pending = blocks
        else:
            r_in, r_out = _reduce_scatter(*blocks, f"reduce_scatter_{l}")
            g_w_in[l], g_w_out[l] = r_in.transpose(1, 0), r_out
        c0s, c1s = c0.sum(axis=0), c1.sum(axis=0)
        d_ada_rows[l] = jnp.concatenate([c0s, norm_gain[l] * c1s, d_gate8.sum(axis=0)])
        small[l] = (
            scale1[l, 0] * c1s,
            d_qg.sum(axis=0).reshape(N_HEADS, HEAD_DIM).sum(axis=0),
            d_kg.sum(axis=0).reshape(2, HEAD_DIM).sum(axis=0),
            d_sk[0, 0:N_HEADS],
            d_bs.reshape(BLK, N_GROUPS, HEAD_DIM).sum(axis=2).transpose(1, 0),
            d_ws,
        )

    names = ("norm_gain", "q_gain", "k_gain", "sink", "b_s")
    stacked = [jnp.stack([small[l][t] for l in range(n_layers)]) for t in range(len(names))]
    d_ada = jnp.stack(d_ada_rows)
    packed, offsets = _pack_rows(stacked + [d_ada, sq_err[0, 0:1]])
    d_ws = jnp.stack([small[l][len(names)] for l in range(n_layers)]).reshape(-1, LANES)
    gathered, gathered_ws = _gather_small(packed, d_ws)
    no_weight = jnp.zeros((1,), F32)
    weights = (norm_gain, q_gain, k_gain, sink, b_s, b_ada, no_weight)
    moments_m = (m_norm_gain, m_q_gain, m_k_gain, m_sink, m_b_s, m_b_ada, no_weight)
    moments_v = (v_norm_gain, v_q_gain, v_k_gain, v_sink, v_b_s, v_b_ada, no_weight)
    w_pack, _ = _pack_rows(weights)
    m_pack, _ = _pack_rows(moments_m)
    v_pack, _ = _pack_rows(moments_v)
    shapes = [w.shape for w in weights]
    flat_ws = lambda a: a.reshape(-1, LANES)
    updated = _small_update(gathered, gathered_ws, w_pack, m_pack, v_pack, flat_ws(w_s), flat_ws(m_w_s), flat_ws(v_w_s))
    g_small, d_small, m_small, v_small = (_unpack_rows(p, offsets, shapes) for p in updated[0:4])
    ws_small = [p.reshape(w_s.shape) for p in updated[4:8]]
    loss = g_small[-1][0] * (0.5 / d)

    ada_off = offsets[-2]
    ada_n = -(-n_layers * 3 * d // (SUBLANES * LANES)) * SUBLANES
    d_ada_all = gathered[:, ada_off : ada_off + ada_n].reshape(N_DEV, -1)[:, : n_layers * 3 * d].reshape(N_DEV, n_layers, 3 * d)
    d_ada_cols = lax.dynamic_slice_in_dim(d_ada_all, my * ada_cols, ada_cols, axis=2)
    g_w_ada = _ada_weight_grad(c_all[:, 0, :], d_ada_cols.transpose(1, 0, 2))

    def update(w, g, m, v, name):
        shape = w.shape
        flat = lambda a: a.reshape(-1, shape[-1])
        return tuple(a.reshape(shape) for a in _adamw(flat(w), flat(g), flat(m), flat(v), name))

    g_w_in, g_w_out = jnp.stack(g_w_in), jnp.stack(g_w_out)
    upd_ada = update(w_ada, g_w_ada, m_w_ada, v_w_ada, "adamw_w_ada")
    upd_in = update(w_in, g_w_in, m_w_in, v_w_in, "adamw_w_in")
    upd_out = update(w_out, g_w_out, m_w_out, v_w_out, "adamw_w_out")

    def ordered(ada_, in_, out_, small_, ws):
        ng, qg, kg, sk, bs, ba, _ = small_
        return (ada_, ba, ng, in_, qg, kg, sk, ws, bs, out_)

    grads = ordered(g_w_ada, g_w_in, g_w_out, g_small, ws_small[0])
    deltas = ordered(upd_ada[0], upd_in[0], upd_out[0], d_small, ws_small[1])
    new_m = ordered(upd_ada[1], upd_in[1], upd_out[1], m_small, ws_small[2])
    new_v = ordered(upd_ada[2], upd_in[2], upd_out[2], v_small, ws_small[3])
    return (loss, dx.reshape(x.shape), *grads, *deltas, *new_m, *new_v)
```

```python
import functools

import jax
import jax.numpy as jnp
from jax import lax
from jax.experimental import pallas as pl
from jax.experimental.pallas import tpu as pltpu

F32 = jnp.float32
MXU_DTYPE = jnp.bfloat16
MESH_ID = pl.DeviceIdType.MESH

N_DEV = 8
HEAD_DIM = 64
N_HEADS = 8
Q_PER_KV = 4
D_ATTN = 512
D_KV = 128
D_GM = 512
N_GROUPS = 8
D_MIX = D_ATTN + D_GM
BLK = 128
LANES = 128
SUBLANES = 8
N_PAIRS = D_ATTN // LANES
D_QKV = D_ATTN + 2 * D_KV
D_REST = D_ATTN + 3 * D_GM
D_IN = D_QKV + D_REST
EPS = 1e-6
NEG_INF = -1e30
ALIBI_SLOPES = tuple(2.0 ** (-8.0 * (h + 1) / N_HEADS) for h in range(N_HEADS))
Q_SCALE = 1.0 / 8.0

ADAM_LR = 0.001
ADAM_B1 = 0.9
ADAM_B2 = 0.999
ADAM_EPS = 1e-08
ADAM_WD = 0.01
ADAM_STEP = 10

TOKEN_TILE = 512
VMEM_LIMIT_BYTES = 56 * 1024 * 1024


def _params(semantics=None):
    return pltpu.CompilerParams(dimension_semantics=semantics, vmem_limit_bytes=VMEM_LIMIT_BYTES)


def _dot(a, b):
    return jnp.dot(a, b, preferred_element_type=F32)


def _dot_nt(a, b):
    return lax.dot_general(a, b, (((1,), (1,)), ((), ())), preferred_element_type=F32)


def _dot_tn(a, b):
    return lax.dot_general(a, b, (((0,), (0,)), ((), ())), preferred_element_type=F32)


def _mx(v):
    return v.astype(MXU_DTYPE)


def _lane_lo(rows):
    return lax.broadcasted_iota(jnp.int32, (rows, LANES), 1) < HEAD_DIM


def _half_ones():
    r = lax.broadcasted_iota(jnp.int32, (LANES, LANES), 0) < HEAD_DIM
    c = lax.broadcasted_iota(jnp.int32, (LANES, LANES), 1) < HEAD_DIM
    return jnp.where(r == c, 1.0, 0.0).astype(jnp.bfloat16)


def _half_sum(v, ones):
    p1 = v.astype(jnp.bfloat16)
    p2 = (v - p1.astype(F32)).astype(jnp.bfloat16)
    return _dot(p1, ones) + _dot(p2, ones)


def _half_rms(v, ones):
    r = lax.rsqrt(_half_sum(v * v, ones) * (1.0 / HEAD_DIM) + EPS)
    return v * r, r


def _half_rms_bwd(dy, vhat, r, ones):
    return r * (dy - vhat * (_half_sum(vhat * dy, ones) * (1.0 / HEAD_DIM)))


def _group_rows(v):
    rows, n = v.shape
    return v.reshape(rows // SUBLANES, SUBLANES, n).sum(axis=0)


def _sigmoid(v):
    return 1.0 / (1.0 + jnp.exp(-v))


ROW_CHUNK = 32
VARIANT_HEADS = ((0, 2, 5, 7), (1, 3, 4, 6))
HEAD_SLOT = {h: (v, s) for v, heads in enumerate(VARIANT_HEADS) for s, h in enumerate(heads)}
STACK = Q_PER_KV * BLK


def _fill_attn_bias(bias_s):
    qi = lax.broadcasted_iota(jnp.int32, (BLK, 3 * BLK), 0)
    ci = lax.broadcasted_iota(jnp.int32, (BLK, 3 * BLK), 1)
    dist = jnp.abs(ci - BLK - qi)
    distf = dist.astype(F32)
    for h in range(N_HEADS):
        bias_s[h] = jnp.where(dist <= BLK, -(ALIBI_SLOPES[h] * distf), NEG_INF)


def _edge_mask(block, seq):
    kpos = (block - 1) * BLK + lax.broadcasted_iota(jnp.int32, (1, 3 * BLK), 1)
    return jnp.where((kpos >= 0) & (kpos < seq), 0.0, NEG_INF)


def _stage_queries(qn, lo_t, j, nb, qs):
    for a in range(2):
        v, slot = HEAD_SLOT[2 * j + a]
        qm = _mx(jnp.where(lo_t, qn, 0.0) if a == 0 else jnp.where(lo_t, 0.0, qn))
        for n in range(nb):
            qs[n, v, slot * BLK : (slot + 1) * BLK, :] = qm[n * BLK : (n + 1) * BLK]


def _unstack_pair(stacked, j, lo):
    (v0, s0), (v1, s1) = HEAD_SLOT[2 * j], HEAD_SLOT[2 * j + 1]
    return jnp.where(lo, stacked[v0][s0 * BLK : (s0 + 1) * BLK], stacked[v1][s1 * BLK : (s1 + 1) * BLK])


def _stage_keys(kvp_ref, qkv_ref, kvn_ref, kg, ones, tile, ks, kr, vs, vr, khat_s=None, rk_s=None):
    pieces = (
        (0, BLK, kvp_ref[:, 0:D_KV], kvp_ref[:, D_KV : 2 * D_KV]),
        (BLK, tile, qkv_ref[:, D_ATTN : D_ATTN + D_KV], qkv_ref[:, D_ATTN + D_KV : D_QKV]),
        (BLK + tile, BLK, kvn_ref[:, 0:D_KV], kvn_ref[:, D_KV : 2 * D_KV]),
    )
    for r0, n, k, v in pieces:
        khat, rk = _half_rms(k, ones)
        kn = khat * kg
        ks[r0 : r0 + n, :] = _mx(kn)
        kr[r0 : r0 + n, :] = _mx(pltpu.roll(kn, HEAD_DIM, 1))
        vs[r0 : r0 + n, :] = _mx(v)
        vr[r0 : r0 + n, :] = _mx(pltpu.roll(v, HEAD_DIM, 1))
        if khat_s is not None:
            khat_s[r0 : r0 + n, :] = khat
            rk_s[r0 : r0 + n, :] = rk


def _halo_specs(tile, seq):
    nb = tile // BLK
    last = seq // BLK - 1
    kv_col = D_ATTN // (2 * D_KV)
    prev = pl.BlockSpec((BLK, 2 * D_KV), lambda i: (jnp.maximum(i * nb - 1, 0), kv_col))
    nxt = pl.BlockSpec((BLK, 2 * D_KV), lambda i: (jnp.minimum((i + 1) * nb, last), kv_col))
    return prev, nxt


def _row_spec(tile, width):
    return pl.BlockSpec((tile, width), lambda i: (i, 0))


def _full_spec(shape):
    nd = len(shape)
    return pl.BlockSpec(shape, lambda i: (0,) * nd)


SMEM_SPEC = pl.BlockSpec(memory_space=pltpu.SMEM)
VMEM_SPEC = pl.BlockSpec(memory_space=pltpu.VMEM)
HBM_SPEC = pl.BlockSpec(memory_space=pltpu.HBM)


def _ln_proj_fwd(x, gain, scale1, shift, w_in_t, name):
    seq, d = x.shape
    tile = min(TOKEN_TILE, seq)

    def body(x_ref, g_ref, s1_ref, sh_ref, wt_ref, pa_ref, pb_ref):
        xv = x_ref[...]
        r = lax.rsqrt(jnp.mean(xv * xv, axis=-1, keepdims=True) + EPS)
        h = _mx((xv * r) * g_ref[...] * s1_ref[...] + sh_ref[...])
        pa_ref[...] = _dot_nt(h, wt_ref[0:D_QKV, :])
        pb_ref[...] = _dot_nt(h, wt_ref[D_QKV:D_IN, :])

    vec = _full_spec((1, d))
    return pl.pallas_call(
        body,
        name=name,
        grid=(seq // tile,),
        in_specs=[_row_spec(tile, d), vec, vec, vec, _full_spec((D_IN, d))],
        out_specs=[_row_spec(tile, D_QKV), _row_spec(tile, D_REST)],
        out_shape=[jax.ShapeDtypeStruct((seq, D_QKV), F32), jax.ShapeDtypeStruct((seq, D_REST), F32)],
        compiler_params=_params(("parallel",)),
    )(x, gain, scale1, shift, w_in_t)


def _rider_steps(nt):
    return 0, (2 * nt) // 3, nt - 1


def _attn_fwd(pa, q_gain2, k_gain2, sink, name, gather=None):
    seq = pa.shape[0]
    tile = min(TOKEN_TILE, seq)
    nb = tile // BLK
    nt = seq // tile
    ext = tile + 2 * BLK
    riding = gather is not None

    def body(sink_ref, qkv_ref, kvp_ref, kvn_ref, qg_ref, kg_ref, *rest):
        i = pl.program_id(0)
        if riding:
            shard_in, shard_out, o_ref, full_in, full_out = rest[0:5]
            qs, ks, kr, vs, vr, bias_s, s_scr, p_scr, inv_scr, send_sems, recv_sems, local_sems = rest[5:]
            start, forward, finish = _all_gather_stages(
                (_row_block(full_in, shard_in.shape[0]), _row_block(full_out, shard_out.shape[0])),
                send_sems,
                recv_sems,
                sources=(shard_in, shard_out),
                local_sems=local_sems,
            )
            at_start, at_forward, at_finish = _rider_steps(nt)
            pl.when(i == at_start)(start)
        else:
            o_ref, qs, ks, kr, vs, vr, bias_s, s_scr, p_scr, inv_scr = rest

        @pl.when(i == 0)
        def _():
            _fill_attn_bias(bias_s)

        ones = _half_ones()
        lo = _lane_lo(BLK)
        lo_t = _lane_lo(tile)
        _stage_keys(kvp_ref, qkv_ref, kvn_ref, kg_ref[...], ones, tile, ks, kr, vs, vr)
        for j in range(N_PAIRS):
            qhat, _ = _half_rms(qkv_ref[:, j * LANES : (j + 1) * LANES], ones)
            _stage_queries(qhat * (qg_ref[...] * Q_SCALE), lo_t, j, nb, qs)

        def block(n, carry):
            r0 = pl.multiple_of(n * BLK, BLK)
            krows = pl.ds(r0, 3 * BLK)
            edge = _edge_mask(i * nb + n, seq)
            for v in range(2):
                s_scr[v] = _dot_nt(qs[n, v], (kr if v else ks)[krows, :])
            for h in range(N_HEADS):
                v, slot = HEAD_SLOT[h]
                sink_h = sink_ref[h]
                for rc in range(0, BLK, ROW_CHUNK):
                    rows = slice(slot * BLK + rc, slot * BLK + rc + ROW_CHUNK)
                    s = s_scr[v, rows, :] + bias_s[h, rc : rc + ROW_CHUNK, :] + edge
                    m = jnp.maximum(jnp.max(s, axis=-1, keepdims=True), sink_h)
                    p = jnp.exp(s - m)
                    total = jnp.sum(p, axis=-1, keepdims=True) + jnp.exp(sink_h - m)
                    p_scr[v, rows, :] = _mx(p)
                    inv_scr[v, rows, :] = jnp.broadcast_to(1.0 / total, (ROW_CHUNK, LANES))
            outs = [_dot(p_scr[v], (vr if v else vs)[krows, :]) * inv_scr[v] for v in range(2)]
            for j in range(N_PAIRS):
                o_ref[pl.ds(r0, BLK), j * LANES : (j + 1) * LANES] = _unstack_pair(outs, j, lo)
            return carry

        lax.fori_loop(0, nb, block, 0)
        if riding:
            pl.when(i == at_forward)(forward)
            pl.when(i == at_finish)(finish)

    prev, nxt = _halo_specs(tile, seq)
    vec = _full_spec((1, LANES))
    in_specs = [SMEM_SPEC, _row_spec(tile, D_QKV), prev, nxt, vec, vec]
    out_specs = [_row_spec(tile, D_ATTN)]
    out_shape = [jax.ShapeDtypeStruct((seq, D_ATTN), F32)]
    scratch = [
        pltpu.VMEM((nb, 2, STACK, LANES), MXU_DTYPE),
        pltpu.VMEM((ext, LANES), MXU_DTYPE),
        pltpu.VMEM((ext, LANES), MXU_DTYPE),
        pltpu.VMEM((ext, LANES), MXU_DTYPE),
        pltpu.VMEM((ext, LANES), MXU_DTYPE),
        pltpu.VMEM((N_HEADS, BLK, 3 * BLK), F32),
        pltpu.VMEM((2, STACK, 3 * BLK), F32),
        pltpu.VMEM((2, STACK, 3 * BLK), MXU_DTYPE),
        pltpu.VMEM((2, STACK, LANES), F32),
    ]
    extra = ()
    if riding:
        extra = tuple(gather)
        in_specs += [HBM_SPEC] * 2
        out_specs += [HBM_SPEC] * 2
        out_shape += [jax.ShapeDtypeStruct((N_DEV * g.shape[0], g.shape[1]), g.dtype) for g in gather]
        scratch += [pltpu.SemaphoreType.DMA((14,)), pltpu.SemaphoreType.DMA((14,)), pltpu.SemaphoreType.DMA((2,))]
    out = pl.pallas_call(
        body,
        name=name,
        grid=(nt,),
        in_specs=in_specs,
        out_specs=out_specs,
        out_shape=out_shape,
        scratch_shapes=scratch,
        compiler_params=_params(("arbitrary",)),
    )(sink, pa, pa, pa, q_gain2, k_gain2, *extra)
    return out if riding else out[0]


def _mix_out_fwd(pb, o, x, gate, w_out, w_s, b_st, name, target=None):
    seq, d = x.shape
    tile = min(TOKEN_TILE, seq)
    nb = tile // BLK
    with_loss = target is not None

    def body(pb_ref, o_ref, x_ref, gate_ref, wo_ref, ws_ref, bs_ref, *rest):
        if with_loss:
            t_ref, xo_ref, acc_ref, y_s, vn_s = rest

            @pl.when(pl.program_id(0) == 0)
            def _():
                acc_ref[...] = jnp.zeros_like(acc_ref)
        else:
            xo_ref, y_s, vn_s = rest
        ones = _half_ones()
        lo = _lane_lo(BLK)
        ga = pb_ref[:, 0:D_ATTN]
        y_s[:, 0:D_ATTN] = _mx(o_ref[...] * (ga * _sigmoid(ga)))
        for j in range(N_PAIRS):
            cols = slice(2 * D_GM + j * LANES, 2 * D_GM + (j + 1) * LANES)
            vhat, _ = _half_rms(pb_ref[:, cols], ones)
            vn_s[:, j * LANES : (j + 1) * LANES] = _mx(vhat)

        def chunk(n, carry):
            rows = pl.ds(pl.multiple_of(n * BLK, BLK), BLK)
            for j in range(N_PAIRS):
                cols = slice(j * LANES, (j + 1) * LANES)
                vn = vn_s[rows, cols]
                sv = jnp.where(lo, _dot(ws_ref[2 * j], vn), _dot(ws_ref[2 * j + 1], vn)) + bs_ref[:, cols]
                u = pb_ref[rows, D_ATTN + j * LANES : D_ATTN + (j + 1) * LANES]
                gg = pb_ref[rows, D_ATTN + 2 * D_GM + j * LANES : D_ATTN + 2 * D_GM + (j + 1) * LANES]
                y_s[rows, D_ATTN + j * LANES : D_ATTN + (j + 1) * LANES] = _mx((u * sv) * (gg * _sigmoid(gg)))
            return carry

        lax.fori_loop(0, nb, chunk, 0)
        y = x_ref[...] + gate_ref[...] * _dot(y_s[...], wo_ref[...])
        if with_loss:
            e = y - t_ref[...]
            xo_ref[...] = e * (1.0 / d)
            acc_ref[...] += jnp.sum(jnp.sum(e * e, axis=-1, keepdims=True), axis=0, keepdims=True)
        else:
            xo_ref[...] = y

    row = _row_spec(tile, d)
    acc_shape = (SUBLANES, LANES)
    return pl.pallas_call(
        body,
        name=name,
        grid=(seq // tile,),
        in_specs=[
            _row_spec(tile, D_REST),
            _row_spec(tile, D_ATTN),
            row,
            _full_spec((1, d)),
            _full_spec((D_MIX, d)),
            _full_spec((N_GROUPS, BLK, BLK)),
            _full_spec((BLK, D_GM)),
        ]
        + ([row] if with_loss else []),
        out_specs=[row, _full_spec(acc_shape)] if with_loss else row,
        out_shape=[jax.ShapeDtypeStruct((seq, d), F32), jax.ShapeDtypeStruct(acc_shape, F32)]
        if with_loss
        else jax.ShapeDtypeStruct((seq, d), F32),
        scratch_shapes=[pltpu.VMEM((tile, D_MIX), MXU_DTYPE), pltpu.VMEM((tile, D_GM), MXU_DTYPE)],
        compiler_params=_params(("arbitrary",) if with_loss else ("parallel",)),
    )(pb, o, x, gate, w_out, w_s, b_st, *([target] if with_loss else []))


def _mix_out_bwd(dxn, pb, o, gate, w_out, w_s, w_s_t, b_st, name):
    seq, d = dxn.shape
    tile = min(TOKEN_TILE, seq)
    nb = tile // BLK

    def body(dxn_ref, pb_ref, o_ref, gate_ref, wo_ref, ws_ref, wst_ref, bs_ref,
             dpb_ref, do_ref, g_ref, dws_ref, dbs_ref, y_s, dy_s, vn_s, rv_s, vnb_s, sv_s, dsv_s, dvn_s):
        @pl.when(pl.program_id(0) == 0)
        def _():
            g_ref[...] = jnp.zeros_like(g_ref)
            dws_ref[...] = jnp.zeros_like(dws_ref)
            dbs_ref[...] = jnp.zeros_like(dbs_ref)

        ones = _half_ones()
        lo = _lane_lo(BLK)
        c_u = slice(D_ATTN, D_ATTN + D_GM)
        c_vg = slice(D_ATTN + D_GM, D_ATTN + 2 * D_GM)
        c_gg = slice(D_ATTN + 2 * D_GM, D_REST)
        dxv = dxn_ref[...]
        dy_s[...] = _dot_nt(_mx(dxv * gate_ref[...]), wo_ref[...])
        ga = pb_ref[:, 0:D_ATTN]
        sig = _sigmoid(ga)
        sil = ga * sig
        ov = o_ref[...]
        y_s[:, 0:D_ATTN] = _mx(ov * sil)
        da = dy_s[:, 0:D_ATTN]
        do_ref[...] = da * sil
        dpb_ref[:, 0:D_ATTN] = (da * ov * (sig * (1.0 + ga * (1.0 - sig)))).astype(dpb_ref.dtype)
        for j in range(N_PAIRS):
            cols = slice(j * LANES, (j + 1) * LANES)
            vhat, rv = _half_rms(pb_ref[:, 2 * D_GM + j * LANES : 2 * D_GM + (j + 1) * LANES], ones)
            vn_s[:, cols] = vhat
            rv_s[:, cols] = rv
            vnb_s[:, cols] = _mx(vhat)

        def spatial_fwd(n, carry):
            rows = pl.ds(pl.multiple_of(n * BLK, BLK), BLK)
            for j in range(N_PAIRS):
                cols = slice(j * LANES, (j + 1) * LANES)
                vn = vnb_s[rows, cols]
                sv_s[rows, cols] = jnp.where(lo, _dot(ws_ref[2 * j], vn), _dot(ws_ref[2 * j + 1], vn)) + bs_ref[:, cols]
            return carry

        lax.fori_loop(0, nb, spatial_fwd, 0)

        def gating(n, carry):
            rows = pl.ds(pl.multiple_of(n * BLK, BLK), BLK)
            sv = sv_s[rows, :]
            u = pb_ref[rows, c_u]
            gg = pb_ref[rows, c_gg]
            sg = _sigmoid(gg)
            silg = gg * sg
            m0 = u * sv
            y_s[rows, D_ATTN:D_MIX] = _mx(m0 * silg)
            dm = dy_s[rows, D_ATTN:D_MIX]
            dm0 = dm * silg
            dpb_ref[rows, c_gg] = (dm * m0 * (sg * (1.0 + gg * (1.0 - sg)))).astype(dpb_ref.dtype)
            dpb_ref[rows, c_u] = (dm0 * sv).astype(dpb_ref.dtype)
            dsv = dm0 * u
            dsv_s[rows, :] = _mx(dsv)
            dbs_ref[...] += dsv
            return carry

        lax.fori_loop(0, nb, gating, 0)

        def spatial_bwd(n, carry):
            rows = pl.ds(pl.multiple_of(n * BLK, BLK), BLK)
            for j in range(N_PAIRS):
                cols = slice(j * LANES, (j + 1) * LANES)
                dsv = dsv_s[rows, cols]
                dvn_s[rows, cols] = jnp.where(lo, _dot(wst_ref[2 * j], dsv), _dot(wst_ref[2 * j + 1], dsv))
            return carry

        lax.fori_loop(0, nb, spatial_bwd, 0)
        zero = jnp.zeros((BLK, LANES), MXU_DTYPE)
        for j in range(N_PAIRS):
            cols = slice(j * LANES, (j + 1) * LANES)
            chunks = [dsv_s[n * BLK : (n + 1) * BLK, cols] for n in range(nb)]
            vn_all = jnp.concatenate([vnb_s[n * BLK : (n + 1) * BLK, cols] for n in range(nb)], axis=1)
            dws_ref[2 * j] += _dot_nt(jnp.concatenate([jnp.where(lo, c, zero) for c in chunks], axis=1), vn_all)
            dws_ref[2 * j + 1] += _dot_nt(jnp.concatenate([jnp.where(lo, zero, c) for c in chunks], axis=1), vn_all)
            dpb_ref[:, D_ATTN + D_GM + j * LANES : D_ATTN + D_GM + (j + 1) * LANES] = _half_rms_bwd(
                dvn_s[:, cols], vn_s[:, cols], rv_s[:, cols], ones
            ).astype(dpb_ref.dtype)
        g_ref[...] += _dot_tn(y_s[...], _mx(dxv))

    return pl.pallas_call(
        body,
        name=name,
        grid=(seq // tile,),
        in_specs=[
            _row_spec(tile, d),
            _row_spec(tile, D_REST),
            _row_spec(tile, D_ATTN),
            _full_spec((1, d)),
            _full_spec((D_MIX, d)),
            _full_spec((N_GROUPS, BLK, BLK)),
            _full_spec((N_GROUPS, BLK, BLK)),
            _full_spec((BLK, D_GM)),
        ],
        out_specs=[
            _row_spec(tile, D_REST),
            _row_spec(tile, D_ATTN),
            _full_spec((D_MIX, d)),
            _full_spec((N_GROUPS, BLK, BLK)),
            _full_spec((BLK, D_GM)),
        ],
        out_shape=[
            jax.ShapeDtypeStruct((seq, D_REST), MXU_DTYPE),
            jax.ShapeDtypeStruct((seq, D_ATTN), F32),
            jax.ShapeDtypeStruct((D_MIX, d), F32),
            jax.ShapeDtypeStruct((N_GROUPS, BLK, BLK), F32),
            jax.ShapeDtypeStruct((BLK, D_GM), F32),
        ],
        scratch_shapes=[
            pltpu.VMEM((tile, D_MIX), MXU_DTYPE),
            pltpu.VMEM((tile, D_MIX), F32),
            pltpu.VMEM((tile, D_GM), F32),
            pltpu.VMEM((tile, D_GM), F32),
            pltpu.VMEM((tile, D_GM), MXU_DTYPE),
            pltpu.VMEM((tile, D_GM), F32),
            pltpu.VMEM((tile, D_GM), MXU_DTYPE),
            pltpu.VMEM((tile, D_GM), F32),
        ],
        compiler_params=_params(("arbitrary",)),
    )(dxn, pb, o, gate, w_out, w_s, w_s_t, b_st)


def _attn_bwd(pa, o, do, q_gain2, k_gain2, sink, name, scatter=()):
    seq = pa.shape[0]
    tile = min(TOKEN_TILE, seq)
    nb = tile // BLK
    nt = seq // tile
    ext = tile + 2 * BLK
    n_ride = len(scatter)
    riding = n_ride > 0

    def body(sink_ref, qkv_ref, kvp_ref, kvn_ref, o_ref, do_ref, qg_ref, kg_ref, *rest):
        i = pl.program_id(0)
        blocks, rest = rest[:n_ride], rest[n_ride:]
        dq_ref, dkv_ref, hp_ref, hn_ref, dqg_ref, dkg_ref, dsk_ref = rest[:7]
        landing, rest = rest[7 : 7 + n_ride], rest[7 + n_ride :]
        (qs, dos, qhat_s, rq_s, ks, kr, vs, vr, khat_s, rk_s, dqn_s, dka, dva, bias_s, s_scr, dp_scr, p_scr, ds_scr) = rest[:18]
        if riding:
            start, finish = _scatter_stages(blocks, landing, *rest[18:])
            at_start, _, at_finish = _rider_steps(nt)
            pl.when(i == at_start)(start)

        @pl.when(i == 0)
        def _():
            dqg_ref[...] = jnp.zeros_like(dqg_ref)
            dkg_ref[...] = jnp.zeros_like(dkg_ref)
            dsk_ref[...] = jnp.zeros_like(dsk_ref)
            _fill_attn_bias(bias_s)

        ones = _half_ones()
        lo = _lane_lo(BLK)
        lo_t = _lane_lo(tile)
        lo_c = _lane_lo(ROW_CHUNK)
        qg = qg_ref[...] * Q_SCALE
        kg = kg_ref[...]
        _stage_keys(kvp_ref, qkv_ref, kvn_ref, kg, ones, tile, ks, kr, vs, vr, khat_s, rk_s)
        for j in range(N_PAIRS):
            cols = slice(j * LANES, (j + 1) * LANES)
            qhat, rq = _half_rms(qkv_ref[:, cols], ones)
            qhat_s[:, cols] = qhat
            rq_s[:, cols] = rq
            _stage_queries(qhat * qg, lo_t, j, nb, qs)
            _stage_queries(do_ref[:, cols], lo_t, j, nb, dos)
        dka[...] = jnp.zeros_like(dka)
        dva[...] = jnp.zeros_like(dva)
        head_lane = lax.broadcasted_iota(jnp.int32, (1, LANES), 1)

        def block(n, dsink):
            r0 = pl.multiple_of(n * BLK, BLK)
            krows = pl.ds(r0, 3 * BLK)
            edge = _edge_mask(i * nb + n, seq)
            for v in range(2):
                s_scr[v] = _dot_nt(qs[n, v], (kr if v else ks)[krows, :])
                dp_scr[v] = _dot_nt(dos[n, v], (vr if v else vs)[krows, :])
            for h in range(N_HEADS):
                v, slot = HEAD_SLOT[h]
                j, a = divmod(h, 2)
                cols = slice(j * LANES, (j + 1) * LANES)
                sink_h = sink_ref[h]
                sink_part = jnp.zeros((ROW_CHUNK, 1), F32)
                for rc in range(0, BLK, ROW_CHUNK):
                    rows = slice(slot * BLK + rc, slot * BLK + rc + ROW_CHUNK)
                    trows = pl.ds(pl.multiple_of(r0 + rc, ROW_CHUNK), ROW_CHUNK)
                    s = s_scr[v, rows, :] + bias_s[h, rc : rc + ROW_CHUNK, :] + edge
                    m = jnp.maximum(jnp.max(s, axis=-1, keepdims=True), sink_h)
                    p = jnp.exp(s - m)
                    e_sink = jnp.exp(sink_h - m)
                    inv = 1.0 / (jnp.sum(p, axis=-1, keepdims=True) + e_sink)
                    pn = p * inv
                    prod = do_ref[trows, cols] * o_ref[trows, cols]
                    prod = jnp.where(lo_c, prod, 0.0) if a == 0 else jnp.where(lo_c, 0.0, prod)
                    dcol = jnp.sum(prod, axis=-1, keepdims=True)
                    ds_scr[v, rows, :] = _mx(pn * (dp_scr[v, rows, :] - dcol))
                    p_scr[v, rows, :] = _mx(pn)
                    sink_part = sink_part + (e_sink * inv) * dcol
                dsink = dsink - jnp.where(head_lane == h, jnp.sum(sink_part, axis=0, keepdims=True), 0.0)
            dqv = []
            for v in range(2):
                dqv.append(_dot(ds_scr[v], (kr if v else ks)[krows, :]))
                dka[v, krows, :] += _dot_tn(ds_scr[v], qs[n, v])
                dva[v, krows, :] += _dot_tn(p_scr[v], dos[n, v])
            for j in range(N_PAIRS):
                dqn_s[pl.ds(r0, BLK), j * LANES : (j + 1) * LANES] = _unstack_pair(dqv, j, lo)
            return dsink

        dsink = lax.fori_loop(0, nb, block, jnp.zeros((1, LANES), F32))
        dsk_ref[...] += jnp.broadcast_to(dsink, (SUBLANES, LANES))
        for j in range(N_PAIRS):
            cols = slice(j * LANES, (j + 1) * LANES)
            dqn = dqn_s[:, cols]
            qhat = qhat_s[:, cols]
            dqg_ref[:, cols] += _group_rows(dqn * qhat) * Q_SCALE
            dq_ref[:, cols] = _half_rms_bwd(dqn * qg, qhat, rq_s[:, cols], ones).astype(dq_ref.dtype)
        dkn = dka[0] + pltpu.roll(dka[1], HEAD_DIM, 1)
        khat = khat_s[...]
        dkg_ref[...] += _group_rows(dkn * khat)
        dk = _half_rms_bwd(dkn * kg, khat, rk_s[...], ones)
        dv = dva[0] + pltpu.roll(dva[1], HEAD_DIM, 1)
        hp_ref[:, 0:D_KV] = dk[0:BLK]
        hp_ref[:, D_KV : 2 * D_KV] = dv[0:BLK]
        dkv_ref[:, 0:D_KV] = dk[BLK : BLK + tile]
        dkv_ref[:, D_KV : 2 * D_KV] = dv[BLK : BLK + tile]
        hn_ref[:, 0:D_KV] = dk[BLK + tile : ext]
        hn_ref[:, D_KV : 2 * D_KV] = dv[BLK + tile : ext]
        if riding:
            pl.when(i == at_finish)(finish)

    prev, nxt = _halo_specs(tile, seq)
    vec = _full_spec((1, LANES))
    halo = pl.BlockSpec((None, BLK, 2 * D_KV), lambda i: (i, 0, 0))
    return pl.pallas_call(
        body,
        name=name,
        grid=(nt,),
        in_specs=[SMEM_SPEC, _row_spec(tile, D_QKV), prev, nxt, _row_spec(tile, D_ATTN), _row_spec(tile, D_ATTN), vec, vec]
        + [HBM_SPEC] * n_ride,
        out_specs=[
            _row_spec(tile, D_ATTN),
            _row_spec(tile, 2 * D_KV),
            halo,
            halo,
            _full_spec((SUBLANES, D_ATTN)),
            _full_spec((SUBLANES, LANES)),
            _full_spec((SUBLANES, LANES)),
        ]
        + [HBM_SPEC] * n_ride,
        out_shape=[
            jax.ShapeDtypeStruct((seq, D_ATTN), MXU_DTYPE),
            jax.ShapeDtypeStruct((seq, 2 * D_KV), F32),
            jax.ShapeDtypeStruct((nt, BLK, 2 * D_KV), F32),
            jax.ShapeDtypeStruct((nt, BLK, 2 * D_KV), F32),
            jax.ShapeDtypeStruct((SUBLANES, D_ATTN), F32),
            jax.ShapeDtypeStruct((SUBLANES, LANES), F32),
            jax.ShapeDtypeStruct((SUBLANES, LANES), F32),
        ]
        + _landing_shapes(scatter),
        scratch_shapes=[
            pltpu.VMEM((nb, 2, STACK, LANES), MXU_DTYPE),
            pltpu.VMEM((nb, 2, STACK, LANES), MXU_DTYPE),
            pltpu.VMEM((tile, D_ATTN), F32),
            pltpu.VMEM((tile, D_ATTN), F32),
            pltpu.VMEM((ext, LANES), MXU_DTYPE),
            pltpu.VMEM((ext, LANES), MXU_DTYPE),
            pltpu.VMEM((ext, LANES), MXU_DTYPE),
            pltpu.VMEM((ext, LANES), MXU_DTYPE),
            pltpu.VMEM((ext, LANES), F32),
            pltpu.VMEM((ext, LANES), F32),
            pltpu.VMEM((tile, D_ATTN), F32),
            pltpu.VMEM((2, ext, LANES), F32),
            pltpu.VMEM((2, ext, LANES), F32),
            pltpu.VMEM((N_HEADS, BLK, 3 * BLK), F32),
            pltpu.VMEM((2, STACK, 3 * BLK), F32),
            pltpu.VMEM((2, STACK, 3 * BLK), F32),
            pltpu.VMEM((2, STACK, 3 * BLK), MXU_DTYPE),
            pltpu.VMEM((2, STACK, 3 * BLK), MXU_DTYPE),
        ]
        + _rider_sems(n_ride),
        compiler_params=_params(("arbitrary",)),
    )(sink, pa, pa, pa, o, do, q_gain2, k_gain2, *scatter)


def _halo_in_specs(tile, nt):
    from_prev = pl.BlockSpec((None, BLK, 2 * D_KV), lambda i: (jnp.maximum(i - 1, 0), 0, 0))
    from_next = pl.BlockSpec((None, BLK, 2 * D_KV), lambda i: (jnp.minimum(i + 1, nt - 1), 0, 0))
    return from_prev, from_next


def _landing_shapes(scatter):
    return [jax.ShapeDtypeStruct((N_DEV,) + b.shape[2:], b.dtype) for b in scatter]


def _rider_sems(n_ride):
    if not n_ride:
        return []
    return [pltpu.SemaphoreType.DMA((7 * n_ride,)), pltpu.SemaphoreType.DMA((7 * n_ride,)), pltpu.SemaphoreType.DMA((n_ride,))]


def _proj_bwd_dx(x, dxn, dq, dkvb, dpb, w_in_t, gain, scale1, name, scatter=()):
    seq, d = x.shape
    tile = min(TOKEN_TILE, seq)
    nt = seq // tile
    n_ride = len(scatter)

    def body(x_ref, dxn_ref, dq_ref, dkvb_ref, dpb_ref, wt_ref, g_ref, s1_ref, *rest):
        i = pl.program_id(0)
        blocks, rest = rest[:n_ride], rest[n_ride:]
        dx_ref, c0_ref, c1_ref = rest[:3]
        landing, sems = rest[3 : 3 + n_ride], rest[3 + n_ride :]
        if n_ride:
            start, finish = _scatter_stages(blocks, landing, *sems)
            at_start, _, at_finish = _rider_steps(nt)
            pl.when(i == at_start)(start)

        @pl.when(i == 0)
        def _():
            c0_ref[...] = jnp.zeros_like(c0_ref)
            c1_ref[...] = jnp.zeros_like(c1_ref)

        dh = (
            _dot(dq_ref[...], wt_ref[0:D_ATTN, :])
            + _dot(dkvb_ref[...], wt_ref[D_ATTN:D_QKV, :])
            + _dot(dpb_ref[...], wt_ref[D_QKV:D_IN, :])
        )
        xv = x_ref[...]
        r = lax.rsqrt(jnp.mean(xv * xv, axis=-1, keepdims=True) + EPS)
        xn = xv * r
        c0_ref[...] += _group_rows(dh)
        c1_ref[...] += _group_rows(dh * xn)
        dxn_ = dh * (g_ref[...] * s1_ref[...])
        dx_ref[...] = dxn_ref[...] + r * (dxn_ - xn * jnp.mean(xn * dxn_, axis=-1, keepdims=True))
        if n_ride:
            pl.when(i == at_finish)(finish)

    vec = _full_spec((1, d))
    return pl.pallas_call(
        body,
        name=name,
        grid=(nt,),
        in_specs=[
            _row_spec(tile, d),
            _row_spec(tile, d),
            _row_spec(tile, D_ATTN),
            _row_spec(tile, 2 * D_KV),
            _row_spec(tile, D_REST),
            _full_spec((D_IN, d)),
            vec,
            vec,
        ]
        + [HBM_SPEC] * n_ride,
        out_specs=[_row_spec(tile, d), _full_spec((SUBLANES, d)), _full_spec((SUBLANES, d))] + [HBM_SPEC] * n_ride,
        out_shape=[
            jax.ShapeDtypeStruct((seq, d), F32),
            jax.ShapeDtypeStruct((SUBLANES, d), F32),
            jax.ShapeDtypeStruct((SUBLANES, d), F32),
        ]
        + _landing_shapes(scatter),
        scratch_shapes=_rider_sems(n_ride),
        compiler_params=_params(("arbitrary",)),
    )(x, dxn, dq, dkvb, dpb, w_in_t, gain, scale1, *scatter)


def _proj_bwd_dw(x, gain, scale1, shift, dq, dkv, halo_prev, halo_next, dpb, name):
    seq, d = x.shape
    tile = min(TOKEN_TILE, seq)
    nt = seq // tile
    assert tile >= 2 * BLK

    def body(x_ref, g_ref, s1_ref, sh_ref, dq_ref, dkv_ref, hn_ref, hp_ref, dpb_ref, dw_ref, dkvb_ref, acc):
        i = pl.program_id(0)

        @pl.when(i == 0)
        def _():
            acc[...] = jnp.zeros_like(acc)

        top = dkv_ref[0:BLK, :] + jnp.where(i > 0, hn_ref[...], 0.0)
        bot = dkv_ref[tile - BLK : tile, :] + jnp.where(i < nt - 1, hp_ref[...], 0.0)
        dkvb_ref[0:BLK, :] = top.astype(dkvb_ref.dtype)
        dkvb_ref[tile - BLK : tile, :] = bot.astype(dkvb_ref.dtype)
        if tile > 2 * BLK:
            dkvb_ref[BLK : tile - BLK, :] = dkv_ref[BLK : tile - BLK, :].astype(dkvb_ref.dtype)
        xv = x_ref[...]
        r = lax.rsqrt(jnp.mean(xv * xv, axis=-1, keepdims=True) + EPS)
        h = _mx((xv * r) * g_ref[...] * s1_ref[...] + sh_ref[...])
        acc[0:D_ATTN, :] += _dot_tn(dq_ref[...], h)
        acc[D_ATTN:D_QKV, :] += _dot_tn(dkvb_ref[...], h)
        acc[D_QKV:D_IN, :] += _dot_tn(dpb_ref[...], h)

        @pl.when(i == nt - 1)
        def _():
            dw_ref[...] = acc[...].astype(dw_ref.dtype)

    from_prev, from_next = _halo_in_specs(tile, nt)
    vec = _full_spec((1, d))
    return pl.pallas_call(
        body,
        name=name,
        grid=(nt,),
        in_specs=[
            _row_spec(tile, d),
            vec,
            vec,
            vec,
            _row_spec(tile, D_ATTN),
            _row_spec(tile, 2 * D_KV),
            from_prev,
            from_next,
            _row_spec(tile, D_REST),
        ],
        out_specs=[_full_spec((D_IN, d)), _row_spec(tile, 2 * D_KV)],
        out_shape=[jax.ShapeDtypeStruct((D_IN, d), jnp.bfloat16), jax.ShapeDtypeStruct((seq, 2 * D_KV), MXU_DTYPE)],
        scratch_shapes=[pltpu.VMEM((D_IN, d), F32)],
        compiler_params=_params(("arbitrary",)),
    )(x, gain, scale1, shift, dq, dkv, halo_next, halo_prev, dpb)


def _w_out_finish(g, w_out, gate, name):
    d_mix, d = g.shape

    def body(g_ref, w_ref, gate_ref, dw_ref, dgate_ref):
        gv = g_ref[...]
        dw_ref[...] = (gv * gate_ref[...]).astype(dw_ref.dtype)
        dgate_ref[...] = _group_rows(gv * w_ref[...].astype(F32))

    return pl.pallas_call(
        body,
        name=name,
        in_specs=[VMEM_SPEC, VMEM_SPEC, VMEM_SPEC],
        out_specs=[VMEM_SPEC, VMEM_SPEC],
        out_shape=[jax.ShapeDtypeStruct((d_mix, d), jnp.bfloat16), jax.ShapeDtypeStruct((SUBLANES, d), F32)],
        compiler_params=_params(),
    )(g, w_out, gate)


def _adamw_math(w, g, m, v):
    m = ADAM_B1 * m + (1.0 - ADAM_B1) * g
    v = ADAM_B2 * v + (1.0 - ADAM_B2) * (g * g)
    m_hat = m / (1.0 - ADAM_B1**ADAM_STEP)
    v_hat = v / (1.0 - ADAM_B2**ADAM_STEP)
    delta = -ADAM_LR * (m_hat / (jnp.sqrt(v_hat) + ADAM_EPS) + ADAM_WD * w)
    return delta, m, v


def _adamw(w, g, m, v, name):
    rows, cols = w.shape
    tile = min(TOKEN_TILE, rows)

    def body(w_ref, g_ref, m_ref, v_ref, d_ref, mo_ref, vo_ref):
        d_ref[...], mo_ref[...], vo_ref[...] = _adamw_math(w_ref[...], g_ref[...], m_ref[...], v_ref[...])

    spec = _row_spec(tile, cols)
    shape = jax.ShapeDtypeStruct((rows, cols), F32)
    return pl.pallas_call(
        body,
        name=name,
        grid=(rows // tile,),
        in_specs=[spec] * 4,
        out_specs=[spec] * 3,
        out_shape=[shape] * 3,
        compiler_params=_params(("parallel",)),
    )(w, g, m, v)


def _small_update(gathered, gathered_ws, w, m, v, ws, m_ws, v_ws):
    def body(ga_ref, gws_ref, w_ref, m_ref, v_ref, ws_ref, mws_ref, vws_ref, *outs):
        for src, refs, out in ((ga_ref, (w_ref, m_ref, v_ref), outs[0:4]), (gws_ref, (ws_ref, mws_ref, vws_ref), outs[4:8])):
            g = src[0].astype(F32)
            for j in range(1, N_DEV):
                g = g + src[j].astype(F32)
            out[0][...] = g
            out[1][...], out[2][...], out[3][...] = _adamw_math(refs[0][...], g, refs[1][...], refs[2][...])

    shapes = [jax.ShapeDtypeStruct(w.shape, F32)] * 4 + [jax.ShapeDtypeStruct(ws.shape, F32)] * 4
    return pl.pallas_call(
        body,
        name="small_update",
        in_specs=[VMEM_SPEC] * 8,
        out_specs=[VMEM_SPEC] * 8,
        out_shape=shapes,
        compiler_params=_params(),
    )(gathered, gathered_ws, w, m, v, ws, m_ws, v_ws)


def _ada_weight_grad(c_all, d_ada_cols):
    d = c_all.shape[-1]
    n_layers, _, width = d_ada_cols.shape

    def body(c_ref, da_ref, dw_ref):
        cv = c_ref[...]
        cond = cv * _sigmoid(cv)
        for l in range(n_layers):
            dw_ref[l] = lax.dot_general(
                cond, da_ref[l], (((0,), (0,)), ((), ())), preferred_element_type=F32, precision=lax.Precision.HIGHEST
            )

    return pl.pallas_call(
        body,
        name="ada_weight_grad",
        in_specs=[VMEM_SPEC, VMEM_SPEC],
        out_specs=VMEM_SPEC,
        out_shape=jax.ShapeDtypeStruct((n_layers, d, width), F32),
        compiler_params=_params(),
    )(c_all, d_ada_cols)


def _position():
    return lax.axis_index("x"), lax.axis_index("y"), lax.axis_index("c")


def _flip(pos, k):
    x, y, c = pos
    return (1 - x if k & 4 else x, 1 - y if k & 2 else y, 1 - c if k & 1 else c)


def _index(pos):
    x, y, c = pos
    return 4 * x + 2 * y + c


def _remote(src, dst, send_sem, recv_sem, to):
    return pltpu.make_async_remote_copy(
        src_ref=src, dst_ref=dst, send_sem=send_sem, recv_sem=recv_sem, device_id=to, device_id_type=MESH_ID
    )


def _all_gather_stages(slots, send_sems, recv_sems, sources=None, local_sems=None):
    me = _position()
    sibling = _flip(me, 1)
    others = (4, 2, 6)
    arrays = range(len(slots))

    def copy(t, k, block, to, own=False):
        slot = slots[t](_index(block))
        src = sources[t] if own and sources is not None else slot
        return _remote(src, slot, send_sems.at[7 * t + k], recv_sems.at[7 * t + k], to)

    def first(t):
        return [copy(t, 0, me, sibling, own=True)] + [copy(t, 1 + j, me, _flip(me, f), own=True) for j, f in enumerate(others)]

    def passed(t, j):
        return copy(t, 4 + j, _flip(me, others[j]), sibling)

    def local(t):
        return pltpu.make_async_copy(sources[t], slots[t](_index(me)), local_sems.at[t])

    def start():
        for t in arrays:
            if sources is not None:
                local(t).start()
            for cp in first(t):
                cp.start()

    def forward():
        for j, f in enumerate(others):
            for t in arrays:
                copy(t, 1 + j, _flip(me, f), me).wait_recv()
                passed(t, j).start()

    def finish():
        for t in arrays:
            copy(t, 0, sibling, me).wait_recv()
            for j, f in enumerate(others):
                copy(t, 4 + j, _flip(sibling, f), me).wait_recv()
        for t in arrays:
            for cp in first(t) + [passed(t, j) for j in range(len(others))]:
                cp.wait_send()
            if sources is not None:
                local(t).wait()

    return start, forward, finish


def _two_level_all_gather(slots, send_sems, recv_sems, between=None):
    start, forward, finish = _all_gather_stages(slots, send_sems, recv_sems)
    start()
    if between is not None:
        between()
    forward()
    finish()


def _row_block(ref, rows):
    return lambda j: ref.at[pl.ds(pl.multiple_of(j * rows, 16), rows), :]


def _scatter_stages(blocks, landing, send_sems, recv_sems, local_sems):
    me = _position()
    my = _index(me)
    arrays = range(len(blocks))

    def copy(t, k):
        px, py, pc = to = _flip(me, k)
        return _remote(blocks[t].at[2 * px + py, pc], landing[t].at[my], send_sems.at[7 * t + k - 1], recv_sems.at[7 * t + k - 1], to)

    def arrival(t, k):
        slot = landing[t].at[_index(_flip(me, k))]
        return _remote(slot, slot, send_sems.at[7 * t + k - 1], recv_sems.at[7 * t + k - 1], _flip(me, k))

    def local(t):
        x, y, c = me
        return pltpu.make_async_copy(blocks[t].at[2 * x + y, c], landing[t].at[my], local_sems.at[t])

    def start():
        for t in arrays:
            local(t).start()
            for k in range(1, N_DEV):
                copy(t, k).start()

    def finish():
        for t in arrays:
            for k in range(1, N_DEV):
                arrival(t, k).wait_recv()
        for t in arrays:
            for k in range(1, N_DEV):
                copy(t, k).wait_send()
            local(t).wait()

    return start, finish


def _ada_exchange(c_ref, w_ref, call_ref, parts_ref, sbuf, sem_s1, sem_r1, sem_s2, sem_r2):
    d = c_ref.shape[-1]
    n_layers = w_ref.shape[0]
    me = _position()
    my = _index(me)
    call_ref[my] = jnp.broadcast_to(c_ref[...], (SUBLANES, d))
    mine = call_ref.at[my]
    first = [_remote(mine, mine, sem_s1.at[k - 1], sem_r1.at[k - 1], _flip(me, k)) for k in range(1, N_DEV)]
    for cp in first:
        cp.start()
    for k in range(1, N_DEV):
        theirs = call_ref.at[_index(_flip(me, k))]
        _remote(theirs, theirs, sem_s1.at[k - 1], sem_r1.at[k - 1], _flip(me, k)).wait_recv()
    cv = call_ref[...].reshape(N_DEV * SUBLANES, d)
    cond = cv * _sigmoid(cv)
    for l in range(n_layers):
        rows = jnp.dot(cond, w_ref[l], preferred_element_type=F32, precision=lax.Precision.HIGHEST)
        for b in range(N_DEV):
            sbuf[b, l] = rows[b * SUBLANES : (b + 1) * SUBLANES]
    parts_ref[my] = sbuf[my]
    second = []
    for k in range(1, N_DEV):
        to = _flip(me, k)
        second.append(_remote(sbuf.at[_index(to)], parts_ref.at[my], sem_s2.at[k - 1], sem_r2.at[k - 1], to))
    for cp in second:
        cp.start()
    for k in range(1, N_DEV):
        theirs = parts_ref.at[_index(_flip(me, k))]
        _remote(theirs, theirs, sem_s2.at[k - 1], sem_r2.at[k - 1], _flip(me, k)).wait_recv()
    for cp in first + second:
        cp.wait_send()


def _gather_weights(w_in_t, w_out, c_row, w_ada):
    n_layers, rows_in, d = w_in_t.shape
    rows_out = w_out.shape[1]
    width = w_ada.shape[2]

    def body(wi_ref, wo_ref, c_ref, wa_ref, gi_ref, go_ref, si_ref, so_ref, call_ref, parts_ref, sbuf, send_sems, recv_sems, *ada_sems):
        my = _index(_position())
        si_ref[...] = wi_ref[...].astype(si_ref.dtype)
        so_ref[...] = wo_ref[...].astype(so_ref.dtype)
        gi_ref[pl.ds(pl.multiple_of(my * rows_in, 16), rows_in), :] = si_ref[0]
        go_ref[pl.ds(pl.multiple_of(my * rows_out, 16), rows_out), :] = so_ref[0]
        _two_level_all_gather(
            (_row_block(gi_ref, rows_in), _row_block(go_ref, rows_out)),
            send_sems,
            recv_sems,
            between=functools.partial(_ada_exchange, c_ref, wa_ref, call_ref, parts_ref, sbuf, *ada_sems),
        )

    return pl.pallas_call(
        body,
        name="gather_weights",
        in_specs=[VMEM_SPEC] * 4,
        out_specs=[VMEM_SPEC] * 6,
        out_shape=[
            jax.ShapeDtypeStruct((N_DEV * rows_in, d), MXU_DTYPE),
            jax.ShapeDtypeStruct((N_DEV * rows_out, d), MXU_DTYPE),
            jax.ShapeDtypeStruct(w_in_t.shape, MXU_DTYPE),
            jax.ShapeDtypeStruct(w_out.shape, MXU_DTYPE),
            jax.ShapeDtypeStruct((N_DEV, SUBLANES, d), F32),
            jax.ShapeDtypeStruct((N_DEV, n_layers, SUBLANES, width), F32),
        ],
        scratch_shapes=[
            pltpu.VMEM((N_DEV, n_layers, SUBLANES, width), F32),
            pltpu.SemaphoreType.DMA((14,)),
            pltpu.SemaphoreType.DMA((14,)),
        ]
        + [pltpu.SemaphoreType.DMA((N_DEV - 1,))] * 4,
        compiler_params=_params(),
    )(w_in_t, w_out, c_row, w_ada)


def _gather_small(packed, d_ws):
    def body(p_ref, ws_ref, g_ref, gws_ref, send_sems, recv_sems):
        my = _index(_position())
        g_ref[my] = p_ref[...]
        gws_ref[my] = ws_ref[...].astype(gws_ref.dtype)
        _two_level_all_gather((lambda j: g_ref.at[j], lambda j: gws_ref.at[j]), send_sems, recv_sems)

    return pl.pallas_call(
        body,
        name="gather_small",
        in_specs=[VMEM_SPEC, VMEM_SPEC],
        out_specs=[VMEM_SPEC, VMEM_SPEC],
        out_shape=[
            jax.ShapeDtypeStruct((N_DEV,) + packed.shape, F32),
            jax.ShapeDtypeStruct((N_DEV,) + d_ws.shape, jnp.bfloat16),
        ],
        scratch_shapes=[pltpu.SemaphoreType.DMA((14,)), pltpu.SemaphoreType.DMA((14,))],
        compiler_params=_params(),
    )(packed, d_ws)


def _scatter_finish(landed, name):
    n = len(landed)

    def body(*refs):
        for src, out in zip(refs[:n], refs[n:]):
            g = src[0].astype(F32)
            for j in range(1, N_DEV):
                g = g + src[j].astype(F32)
            out[...] = g

    return pl.pallas_call(
        body,
        name=name,
        in_specs=[VMEM_SPEC] * n,
        out_specs=[VMEM_SPEC] * n,
        out_shape=[jax.ShapeDtypeStruct(a.shape[1:], F32) for a in landed],
        compiler_params=_params(),
    )(*landed)


def _pack_rows(parts):
    rows, offsets, at = [], [], 0
    for p in parts:
        flat = p.reshape(-1)
        n = -(-flat.shape[0] // (SUBLANES * LANES)) * SUBLANES
        rows.append(jnp.pad(flat, (0, n * LANES - flat.shape[0])).reshape(n, LANES))
        offsets.append(at)
        at += n
    return jnp.concatenate(rows, axis=0), offsets


def _unpack_rows(packed, offsets, shapes):
    out = []
    for off, shape in zip(offsets, shapes):
        size = 1
        for s in shape:
            size *= s
        n = -(-size // (SUBLANES * LANES)) * SUBLANES
        out.append(packed[off : off + n].reshape(-1)[:size].reshape(shape))
    return out


def kernel(x, c, w_ada, b_ada, norm_gain, w_in, q_gain, k_gain, sink, w_s, b_s, w_out, loss_target, m_w_ada, m_b_ada, m_norm_gain, m_w_in, m_q_gain, m_k_gain, m_sink, m_w_s, m_b_s, m_w_out, v_w_ada, v_b_ada, v_norm_gain, v_w_in, v_q_gain, v_k_gain, v_sink, v_w_s, v_b_s, v_w_out):
    seq, d = x.shape[1], x.shape[2]
    n_layers = w_in.shape[0]
    w_cols = w_in.shape[2]
    ada_cols = w_ada.shape[2]
    my = _index(_position())
    xs = x.reshape(seq, d)
    target = loss_target.reshape(seq, d)

    w_in_t0, w_out0, shard_in, shard_out, c_all, ada_parts = _gather_weights(w_in.transpose(0, 2, 1), w_out, c, w_ada)
    w_in_ts, w_outs = [w_in_t0], [w_out0]
    ada = ada_parts[:, :, 0, :].transpose(1, 0, 2).reshape(n_layers, 3 * d) + b_ada
    shift, scale1, gate = ada[:, None, 0:d], 1.0 + ada[:, None, d : 2 * d], ada[:, None, 2 * d : 3 * d]
    gain = norm_gain[:, None, :]

    w_s_m = w_s.astype(MXU_DTYPE)
    w_s_t = w_s_m.transpose(0, 1, 3, 2)
    b_st = jnp.repeat(b_s.transpose(0, 2, 1), HEAD_DIM, axis=2)
    q_gain2 = jnp.tile(q_gain, (1, 2))[:, None, :]
    k_gain2 = jnp.tile(k_gain, (1, 2))[:, None, :]

    xl, saved = xs, []
    for l in range(n_layers):
        last = l == n_layers - 1
        pa, pb = _ln_proj_fwd(xl, gain[l], scale1[l], shift[l], w_in_ts[l], f"ln_proj_fwd_{l}")
        if last:
            o = _attn_fwd(pa, q_gain2[l], k_gain2[l], sink[l], f"attn_fwd_{l}")
        else:
            o, w_in_next, w_out_next = _attn_fwd(
                pa, q_gain2[l], k_gain2[l], sink[l], f"attn_fwd_{l}", gather=(shard_in[l + 1], shard_out[l + 1])
            )
            w_in_ts.append(w_in_next)
            w_outs.append(w_out_next)
        saved.append((xl, pa, pb, o))
        out = _mix_out_fwd(pb, o, xl, gate[l], w_outs[l], w_s_m[l], b_st[l], f"mix_out_fwd_{l}", target if last else None)
        if last:
            dx, sq_err = out
        else:
            xl = out

    g_w_in, g_w_out, small, d_ada_rows = [None] * n_layers, [None] * n_layers, [None] * n_layers, [None] * n_layers
    waiting = []
    for l in reversed(range(n_layers)):
        x_l, pa, pb, o = saved[l]
        dpb, do, g_acc, d_ws, d_bs = _mix_out_bwd(dx, pb, o, gate[l], w_outs[l], w_s_m[l], w_s_t[l], b_st[l], f"mix_out_bwd_{l}")
        dw_out, d_gate8 = _w_out_finish(g_acc, w_outs[l], gate[l], f"w_out_finish_{l}")
        riding = waiting + [(g_w_out, l, dw_out.reshape(4, 2, D_MIX // N_DEV, d))]
        attn = _attn_bwd(pa, o, do, q_gain2[l], k_gain2[l], sink[l], f"attn_bwd_{l}", scatter=tuple(b for _, _, b in riding))
        dq, dkv, halo_prev, halo_next, d_qg, d_kg, d_sk = attn[:7]
        for (dest, layer, _), total in zip(riding, _scatter_finish(attn[7:], f"scatter_finish_{l}")):
            dest[layer] = total.transpose(1, 0) if dest is g_w_in else total
        dw_in_t, dkvb = _proj_bwd_dw(x_l, gain[l], scale1[l], shift[l], dq, dkv, halo_prev, halo_next, dpb, f"proj_bwd_dw_{l}")
        blocks_in = dw_in_t.reshape(4, 2, w_cols, d)
        waiting = [(g_w_in, l, blocks_in)] if l > 0 else []
        dxs = _proj_bwd_dx(
            x_l, dx, dq, dkvb, dpb, w_in_ts[l], gain[l], scale1[l], f"proj_bwd_dx_{l}", scatter=() if l > 0 else (blocks_in,)
        )
        dx, c0, c1 = dxs[:3]
        if l == 0:
            g_w_in[l] = _scatter_finish(dxs[3:], "scatter_finish_in_0")[0].transpose(1, 0)
        c0s, c1s = c0.sum(axis=0), c1.sum(axis=0)
        d_ada_rows[l] = jnp.concatenate([c0s, norm_gain[l] * c1s, d_gate8.sum(axis=0)])
        small[l] = (
            scale1[l, 0] * c1s,
            d_qg.sum(axis=0).reshape(N_HEADS, HEAD_DIM).sum(axis=0),
            d_kg.sum(axis=0).reshape(2, HEAD_DIM).sum(axis=0),
            d_sk[0, 0:N_HEADS],
            d_bs.reshape(BLK, N_GROUPS, HEAD_DIM).sum(axis=2).transpose(1, 0),
            d_ws,
        )

    names = ("norm_gain", "q_gain", "k_gain", "sink", "b_s")
    stacked = [jnp.stack([small[l][t] for l in range(n_layers)]) for t in range(len(names))]
    d_ada = jnp.stack(d_ada_rows)
    packed, offsets = _pack_rows(stacked + [d_ada, sq_err[0, 0:1]])
    d_ws = jnp.stack([small[l][len(names)] for l in range(n_layers)]).reshape(-1, LANES)
    gathered, gathered_ws = _gather_small(packed, d_ws)
    no_weight = jnp.zeros((1,), F32)
    weights = (norm_gain, q_gain, k_gain, sink, b_s, b_ada, no_weight)
    moments_m = (m_norm_gain, m_q_gain, m_k_gain, m_sink, m_b_s, m_b_ada, no_weight)
    moments_v = (v_norm_gain, v_q_gain, v_k_gain, v_sink, v_b_s, v_b_ada, no_weight)
    w_pack, _ = _pack_rows(weights)
    m_pack, _ = _pack_rows(moments_m)
    v_pack, _ = _pack_rows(moments_v)
    shapes = [w.shape for w in weights]
    flat_ws = lambda a: a.reshape(-1, LANES)
    updated = _small_update(gathered, gathered_ws, w_pack, m_pack, v_pack, flat_ws(w_s), flat_ws(m_w_s), flat_ws(v_w_s))
    g_small, d_small, m_small, v_small = (_unpack_rows(p, offsets, shapes) for p in updated[0:4])
    ws_small = [p.reshape(w_s.shape) for p in updated[4:8]]
    loss = g_small[-1][0] * (0.5 / d)

    ada_off = offsets[-2]
    ada_n = -(-n_layers * 3 * d // (SUBLANES * LANES)) * SUBLANES
    d_ada_all = gathered[:, ada_off : ada_off + ada_n].reshape(N_DEV, -1)[:, : n_layers * 3 * d].reshape(N_DEV, n_layers, 3 * d)
    d_ada_cols = lax.dynamic_slice_in_dim(d_ada_all, my * ada_cols, ada_cols, axis=2)
    g_w_ada = _ada_weight_grad(c_all[:, 0, :], d_ada_cols.transpose(1, 0, 2))

    def update(w, g, m, v, name):
        shape = w.shape
        flat = lambda a: a.reshape(-1, shape[-1])
        return tuple(a.reshape(shape) for a in _adamw(flat(w), flat(g), flat(m), flat(v), name))

    g_w_in, g_w_out = jnp.stack(g_w_in), jnp.stack(g_w_out)
    upd_ada = update(w_ada, g_w_ada, m_w_ada, v_w_ada, "adamw_w_ada")
    upd_in = update(w_in, g_w_in, m_w_in, v_w_in, "adamw_w_in")
    upd_out = update(w_out, g_w_out, m_w_out, v_w_out, "adamw_w_out")

    def ordered(ada_, in_, out_, small_, ws):
        ng, qg, kg, sk, bs, ba, _ = small_
        return (ada_, ba, ng, in_, qg, kg, sk, ws, bs, out_)

    grads = ordered(g_w_ada, g_w_in, g_w_out, g_small, ws_small[0])
    deltas = ordered(upd_ada[0], upd_in[0], upd_out[0], d_small, ws_small[1])
    new_m = ordered(upd_ada[1], upd_in[1], upd_out[1], m_small, ws_small[2])
    new_v = ordered(upd_ada[2], upd_in[2], upd_out[2], v_small, ws_small[3])
    return (loss, dx.reshape(x.shape), *grads, *deltas, *new_m, *new_v)
```

```python
import functools

import jax
import jax.numpy as jnp
from jax import lax
from jax.experimental import pallas as pl
from jax.experimental.pallas import tpu as pltpu

F32 = jnp.float32
MXU_DTYPE = jnp.bfloat16
MESH_ID = pl.DeviceIdType.MESH

N_DEV = 8
HEAD_DIM = 64
N_HEADS = 8
Q_PER_KV = 4
D_ATTN = 512
D_KV = 128
D_GM = 512
N_GROUPS = 8
D_MIX = D_ATTN + D_GM
BLK = 128
LANES = 128
SUBLANES = 8
N_PAIRS = D_ATTN // LANES
D_QKV = D_ATTN + 2 * D_KV
D_REST = D_ATTN + 3 * D_GM
D_IN = D_QKV + D_REST
EPS = 1e-6
NEG_INF = -1e30
ALIBI_SLOPES = tuple(2.0 ** (-8.0 * (h + 1) / N_HEADS) for h in range(N_HEADS))
Q_SCALE = 1.0 / 8.0

ADAM_LR = 0.001
ADAM_B1 = 0.9
ADAM_B2 = 0.999
ADAM_EPS = 1e-08
ADAM_WD = 0.01
ADAM_STEP = 10

TOKEN_TILE = 512
VMEM_LIMIT_BYTES = 56 * 1024 * 1024


def _params(semantics=None):
    return pltpu.CompilerParams(dimension_semantics=semantics, vmem_limit_bytes=VMEM_LIMIT_BYTES)


def _dot(a, b):
    return jnp.dot(a, b, preferred_element_type=F32)


def _dot_nt(a, b):
    return lax.dot_general(a, b, (((1,), (1,)), ((), ())), preferred_element_type=F32)


def _dot_tn(a, b):
    return lax.dot_general(a, b, (((0,), (0,)), ((), ())), preferred_element_type=F32)


def _mx(v):
    return v.astype(MXU_DTYPE)


def _lane_lo(rows):
    return lax.broadcasted_iota(jnp.int32, (rows, LANES), 1) < HEAD_DIM


def _half_ones(width=LANES):
    r = jnp.right_shift(lax.broadcasted_iota(jnp.int32, (width, width), 0), 6)
    c = jnp.right_shift(lax.broadcasted_iota(jnp.int32, (width, width), 1), 6)
    return jnp.where(r == c, 1.0, 0.0).astype(jnp.bfloat16)


WIDE = 2 * LANES


def _half_sum(v, ones):
    p1 = v.astype(jnp.bfloat16)
    p2 = (v - p1.astype(F32)).astype(jnp.bfloat16)
    return _dot(p1, ones) + _dot(p2, ones)


def _half_rms(v, ones):
    r = lax.rsqrt(_half_sum(v * v, ones) * (1.0 / HEAD_DIM) + EPS)
    return v * r, r


def _half_rms_bwd(dy, vhat, r, ones):
    return r * (dy - vhat * (_half_sum(vhat * dy, ones) * (1.0 / HEAD_DIM)))


def _group_rows(v):
    rows, n = v.shape
    return v.reshape(rows // SUBLANES, SUBLANES, n).sum(axis=0)


def _sigmoid(v):
    return 1.0 / (1.0 + jnp.exp(-v))


ROW_CHUNK = 32
VARIANT_HEADS = ((0, 2, 5, 7), (1, 3, 4, 6))
HEAD_SLOT = {h: (v, s) for v, heads in enumerate(VARIANT_HEADS) for s, h in enumerate(heads)}
STACK = Q_PER_KV * BLK


def _fill_attn_bias(bias_s):
    qi = lax.broadcasted_iota(jnp.int32, (BLK, 3 * BLK), 0)
    ci = lax.broadcasted_iota(jnp.int32, (BLK, 3 * BLK), 1)
    dist = jnp.abs(ci - BLK - qi)
    distf = dist.astype(F32)
    for h in range(N_HEADS):
        bias_s[h] = jnp.where(dist <= BLK, -(ALIBI_SLOPES[h] * distf), NEG_INF)


def _edge_mask(block, seq):
    kpos = (block - 1) * BLK + lax.broadcasted_iota(jnp.int32, (1, 3 * BLK), 1)
    return jnp.where((kpos >= 0) & (kpos < seq), 0.0, NEG_INF)


def _stage_queries(qn, lo_t, j, nb, qs):
    for a in range(2):
        v, slot = HEAD_SLOT[2 * j + a]
        qm = _mx(jnp.where(lo_t, qn, 0.0) if a == 0 else jnp.where(lo_t, 0.0, qn))
        for n in range(nb):
            qs[n, v, slot * BLK : (slot + 1) * BLK, :] = qm[n * BLK : (n + 1) * BLK]


def _unstack_pair(stacked, j, lo):
    (v0, s0), (v1, s1) = HEAD_SLOT[2 * j], HEAD_SLOT[2 * j + 1]
    return jnp.where(lo, stacked[v0][s0 * BLK : (s0 + 1) * BLK], stacked[v1][s1 * BLK : (s1 + 1) * BLK])


def _stage_keys(kvp_ref, qkv_ref, kvn_ref, kg, ones, tile, ks, kr, vs, vr, khat_s=None, rk_s=None):
    pieces = (
        (0, BLK, kvp_ref[:, 0:D_KV], kvp_ref[:, D_KV : 2 * D_KV]),
        (BLK, tile, qkv_ref[:, D_ATTN : D_ATTN + D_KV], qkv_ref[:, D_ATTN + D_KV : D_QKV]),
        (BLK + tile, BLK, kvn_ref[:, 0:D_KV], kvn_ref[:, D_KV : 2 * D_KV]),
    )
    for r0, n, k, v in pieces:
        khat, rk = _half_rms(k, ones)
        kn = khat * kg
        ks[r0 : r0 + n, :] = _mx(kn)
        kr[r0 : r0 + n, :] = _mx(pltpu.roll(kn, HEAD_DIM, 1))
        vs[r0 : r0 + n, :] = _mx(v)
        vr[r0 : r0 + n, :] = _mx(pltpu.roll(v, HEAD_DIM, 1))
        if khat_s is not None:
            khat_s[r0 : r0 + n, :] = khat
            rk_s[r0 : r0 + n, :] = rk


def _halo_specs(tile, seq):
    nb = tile // BLK
    last = seq // BLK - 1
    kv_col = D_ATTN // (2 * D_KV)
    prev = pl.BlockSpec((BLK, 2 * D_KV), lambda i: (jnp.maximum(i * nb - 1, 0), kv_col))
    nxt = pl.BlockSpec((BLK, 2 * D_KV), lambda i: (jnp.minimum((i + 1) * nb, last), kv_col))
    return prev, nxt


def _row_spec(tile, width):
    return pl.BlockSpec((tile, width), lambda i: (i, 0))


def _full_spec(shape):
    nd = len(shape)
    return pl.BlockSpec(shape, lambda i: (0,) * nd)


SMEM_SPEC = pl.BlockSpec(memory_space=pltpu.SMEM)
VMEM_SPEC = pl.BlockSpec(memory_space=pltpu.VMEM)
HBM_SPEC = pl.BlockSpec(memory_space=pltpu.HBM)


def _ln_proj_fwd(x, gain, scale1, shift, w_in_t, name):
    seq, d = x.shape
    tile = min(TOKEN_TILE, seq)

    def body(x_ref, g_ref, s1_ref, sh_ref, wt_ref, pa_ref, pb_ref):
        xv = x_ref[...]
        r = lax.rsqrt(jnp.mean(xv * xv, axis=-1, keepdims=True) + EPS)
        h = _mx((xv * r) * g_ref[...] * s1_ref[...] + sh_ref[...])
        pa_ref[...] = _dot_nt(h, wt_ref[0:D_QKV, :])
        pb_ref[...] = _dot_nt(h, wt_ref[D_QKV:D_IN, :])

    vec = _full_spec((1, d))
    return pl.pallas_call(
        body,
        name=name,
        grid=(seq // tile,),
        in_specs=[_row_spec(tile, d), vec, vec, vec, _full_spec((D_IN, d))],
        out_specs=[_row_spec(tile, D_QKV), _row_spec(tile, D_REST)],
        out_shape=[jax.ShapeDtypeStruct((seq, D_QKV), F32), jax.ShapeDtypeStruct((seq, D_REST), F32)],
        compiler_params=_params(("parallel",)),
    )(x, gain, scale1, shift, w_in_t)


def _rider_steps(nt):
    return 0, (2 * nt) // 3, nt - 1


def _attn_fwd(pa, q_gain2, k_gain2, sink, name, gather=None):
    seq = pa.shape[0]
    tile = min(TOKEN_TILE, seq)
    nb = tile // BLK
    nt = seq // tile
    ext = tile + 2 * BLK
    riding = gather is not None

    def body(sink_ref, qkv_ref, kvp_ref, kvn_ref, qg_ref, kg_ref, *rest):
        i = pl.program_id(0)
        if riding:
            shard_in, shard_out, o_ref, full_in, full_out = rest[0:5]
            qs, ks, kr, vs, vr, bias_s, s_scr, p_scr, inv_scr, send_sems, recv_sems, local_sems = rest[5:]
            start, forward, finish = _all_gather_stages(
                (_row_block(full_in, shard_in.shape[0]), _row_block(full_out, shard_out.shape[0])),
                send_sems,
                recv_sems,
                sources=(shard_in, shard_out),
                local_sems=local_sems,
            )
            at_start, at_forward, at_finish = _rider_steps(nt)
            pl.when(i == at_start)(start)
        else:
            o_ref, qs, ks, kr, vs, vr, bias_s, s_scr, p_scr, inv_scr = rest

        @pl.when(i == 0)
        def _():
            _fill_attn_bias(bias_s)

        ones = _half_ones()
        lo = _lane_lo(BLK)
        lo_t = _lane_lo(tile)
        _stage_keys(kvp_ref, qkv_ref, kvn_ref, kg_ref[...], ones, tile, ks, kr, vs, vr)
        for j in range(N_PAIRS):
            qhat, _ = _half_rms(qkv_ref[:, j * LANES : (j + 1) * LANES], ones)
            _stage_queries(qhat * (qg_ref[...] * Q_SCALE), lo_t, j, nb, qs)

        def block(n, carry):
            r0 = pl.multiple_of(n * BLK, BLK)
            krows = pl.ds(r0, 3 * BLK)
            edge = _edge_mask(i * nb + n, seq)
            for v in range(2):
                s_scr[v] = _dot_nt(qs[n, v], (kr if v else ks)[krows, :])
            for h in range(N_HEADS):
                v, slot = HEAD_SLOT[h]
                sink_h = sink_ref[h]
                for rc in range(0, BLK, ROW_CHUNK):
                    rows = slice(slot * BLK + rc, slot * BLK + rc + ROW_CHUNK)
                    s = s_scr[v, rows, :] + bias_s[h, rc : rc + ROW_CHUNK, :] + edge
                    m = jnp.maximum(jnp.max(s, axis=-1, keepdims=True), sink_h)
                    p = jnp.exp(s - m)
                    total = jnp.sum(p, axis=-1, keepdims=True) + jnp.exp(sink_h - m)
                    p_scr[v, rows, :] = _mx(p)
                    inv_scr[v, rows, :] = jnp.broadcast_to(1.0 / total, (ROW_CHUNK, LANES))
            outs = [_dot(p_scr[v], (vr if v else vs)[krows, :]) * inv_scr[v] for v in range(2)]
            for j in range(N_PAIRS):
                o_ref[pl.ds(r0, BLK), j * LANES : (j + 1) * LANES] = _unstack_pair(outs, j, lo)
            return carry

        lax.fori_loop(0, nb, block, 0)
        if riding:
            pl.when(i == at_forward)(forward)
            pl.when(i == at_finish)(finish)

    prev, nxt = _halo_specs(tile, seq)
    vec = _full_spec((1, LANES))
    in_specs = [SMEM_SPEC, _row_spec(tile, D_QKV), prev, nxt, vec, vec]
    out_specs = [_row_spec(tile, D_ATTN)]
    out_shape = [jax.ShapeDtypeStruct((seq, D_ATTN), F32)]
    scratch = [
        pltpu.VMEM((nb, 2, STACK, LANES), MXU_DTYPE),
        pltpu.VMEM((ext, LANES), MXU_DTYPE),
        pltpu.VMEM((ext, LANES), MXU_DTYPE),
        pltpu.VMEM((ext, LANES), MXU_DTYPE),
        pltpu.VMEM((ext, LANES), MXU_DTYPE),
        pltpu.VMEM((N_HEADS, BLK, 3 * BLK), F32),
        pltpu.VMEM((2, STACK, 3 * BLK), F32),
        pltpu.VMEM((2, STACK, 3 * BLK), MXU_DTYPE),
        pltpu.VMEM((2, STACK, LANES), F32),
    ]
    extra = ()
    if riding:
        extra = tuple(gather)
        in_specs += [HBM_SPEC] * 2
        out_specs += [HBM_SPEC] * 2
        out_shape += [jax.ShapeDtypeStruct((N_DEV * g.shape[0], g.shape[1]), g.dtype) for g in gather]
        scratch += [pltpu.SemaphoreType.DMA((14,)), pltpu.SemaphoreType.DMA((14,)), pltpu.SemaphoreType.DMA((2,))]
    out = pl.pallas_call(
        body,
        name=name,
        grid=(nt,),
        in_specs=in_specs,
        out_specs=out_specs,
        out_shape=out_shape,
        scratch_shapes=scratch,
        compiler_params=_params(("arbitrary",)),
    )(sink, pa, pa, pa, q_gain2, k_gain2, *extra)
    return out if riding else out[0]


def _mix_out_fwd(pb, o, x, gate, w_out, w_s, b_st, name, target=None):
    seq, d = x.shape
    tile = min(TOKEN_TILE, seq)
    nb = tile // BLK
    with_loss = target is not None

    def body(pb_ref, o_ref, x_ref, gate_ref, wo_ref, ws_ref, bs_ref, *rest):
        if with_loss:
            t_ref, xo_ref, acc_ref, y_s, vn_s = rest

            @pl.when(pl.program_id(0) == 0)
            def _():
                acc_ref[...] = jnp.zeros_like(acc_ref)
        else:
            xo_ref, y_s, vn_s = rest
        ones = _half_ones(WIDE)
        lo = _lane_lo(BLK)
        ga = pb_ref[:, 0:D_ATTN]
        y_s[:, 0:D_ATTN] = _mx(o_ref[...] * (ga * _sigmoid(ga)))
        for j in range(D_GM // WIDE):
            vhat, _ = _half_rms(pb_ref[:, 2 * D_GM + j * WIDE : 2 * D_GM + (j + 1) * WIDE], ones)
            vn_s[:, j * WIDE : (j + 1) * WIDE] = _mx(vhat)

        def chunk(n, carry):
            rows = pl.ds(pl.multiple_of(n * BLK, BLK), BLK)
            for j in range(N_PAIRS):
                cols = slice(j * LANES, (j + 1) * LANES)
                vn = vn_s[rows, cols]
                sv = jnp.where(lo, _dot(ws_ref[2 * j], vn), _dot(ws_ref[2 * j + 1], vn)) + bs_ref[:, cols]
                u = pb_ref[rows, D_ATTN + j * LANES : D_ATTN + (j + 1) * LANES]
                gg = pb_ref[rows, D_ATTN + 2 * D_GM + j * LANES : D_ATTN + 2 * D_GM + (j + 1) * LANES]
                y_s[rows, D_ATTN + j * LANES : D_ATTN + (j + 1) * LANES] = _mx((u * sv) * (gg * _sigmoid(gg)))
            return carry

        lax.fori_loop(0, nb, chunk, 0)
        y = x_ref[...] + gate_ref[...] * _dot(y_s[...], wo_ref[...])
        if with_loss:
            e = y - t_ref[...]
            xo_ref[...] = e * (1.0 / d)
            acc_ref[...] += jnp.sum(jnp.sum(e * e, axis=-1, keepdims=True), axis=0, keepdims=True)
        else:
            xo_ref[...] = y

    row = _row_spec(tile, d)
    acc_shape = (SUBLANES, LANES)
    return pl.pallas_call(
        body,
        name=name,
        grid=(seq // tile,),
        in_specs=[
            _row_spec(tile, D_REST),
            _row_spec(tile, D_ATTN),
            row,
            _full_spec((1, d)),
            _full_spec((D_MIX, d)),
            _full_spec((N_GROUPS, BLK, BLK)),
            _full_spec((BLK, D_GM)),
        ]
        + ([row] if with_loss else []),
        out_specs=[row, _full_spec(acc_shape)] if with_loss else row,
        out_shape=[jax.ShapeDtypeStruct((seq, d), F32), jax.ShapeDtypeStruct(acc_shape, F32)]
        if with_loss
        else jax.ShapeDtypeStruct((seq, d), F32),
        scratch_shapes=[pltpu.VMEM((tile, D_MIX), MXU_DTYPE), pltpu.VMEM((tile, D_GM), MXU_DTYPE)],
        compiler_params=_params(("arbitrary",) if with_loss else ("parallel",)),
    )(pb, o, x, gate, w_out, w_s, b_st, *([target] if with_loss else []))


def _mix_out_bwd(dxn, pb, o, gate, w_out, w_s, w_s_t, b_st, name):
    seq, d = dxn.shape
    tile = min(TOKEN_TILE, seq)
    nb = tile // BLK

    def body(dxn_ref, pb_ref, o_ref, gate_ref, wo_ref, ws_ref, wst_ref, bs_ref,
             dpb_ref, do_ref, g_ref, dws_ref, dbs_ref, y_s, dy_s, vn_s, rv_s, vnb_s, sv_s, dsv_s, dvn_s):
        @pl.when(pl.program_id(0) == 0)
        def _():
            g_ref[...] = jnp.zeros_like(g_ref)
            dws_ref[...] = jnp.zeros_like(dws_ref)
            dbs_ref[...] = jnp.zeros_like(dbs_ref)

        ones = _half_ones(WIDE)
        lo = _lane_lo(BLK)
        c_u = slice(D_ATTN, D_ATTN + D_GM)
        c_vg = slice(D_ATTN + D_GM, D_ATTN + 2 * D_GM)
        c_gg = slice(D_ATTN + 2 * D_GM, D_REST)
        dxv = dxn_ref[...]
        dy_s[...] = _dot_nt(_mx(dxv * gate_ref[...]), wo_ref[...])
        ga = pb_ref[:, 0:D_ATTN]
        sig = _sigmoid(ga)
        sil = ga * sig
        ov = o_ref[...]
        y_s[:, 0:D_ATTN] = _mx(ov * sil)
        da = dy_s[:, 0:D_ATTN]
        do_ref[...] = da * sil
        dpb_ref[:, 0:D_ATTN] = (da * ov * (sig * (1.0 + ga * (1.0 - sig)))).astype(dpb_ref.dtype)
        for j in range(D_GM // WIDE):
            cols = slice(j * WIDE, (j + 1) * WIDE)
            vhat, rv = _half_rms(pb_ref[:, 2 * D_GM + j * WIDE : 2 * D_GM + (j + 1) * WIDE], ones)
            vn_s[:, cols] = vhat
            rv_s[:, cols] = rv
            vnb_s[:, cols] = _mx(vhat)

        def spatial_fwd(n, carry):
            rows = pl.ds(pl.multiple_of(n * BLK, BLK), BLK)
            for j in range(N_PAIRS):
                cols = slice(j * LANES, (j + 1) * LANES)
                vn = vnb_s[rows, cols]
                sv_s[rows, cols] = jnp.where(lo, _dot(ws_ref[2 * j], vn), _dot(ws_ref[2 * j + 1], vn)) + bs_ref[:, cols]
            return carry

        lax.fori_loop(0, nb, spatial_fwd, 0)

        def gating(n, carry):
            rows = pl.ds(pl.multiple_of(n * BLK, BLK), BLK)
            sv = sv_s[rows, :]
            u = pb_ref[rows, c_u]
            gg = pb_ref[rows, c_gg]
            sg = _sigmoid(gg)
            silg = gg * sg
            m0 = u * sv
            y_s[rows, D_ATTN:D_MIX] = _mx(m0 * silg)
            dm = dy_s[rows, D_ATTN:D_MIX]
            dm0 = dm * silg
            dpb_ref[rows, c_gg] = (dm * m0 * (sg * (1.0 + gg * (1.0 - sg)))).astype(dpb_ref.dtype)
            dpb_ref[rows, c_u] = (dm0 * sv).astype(dpb_ref.dtype)
            dsv = dm0 * u
            dsv_s[rows, :] = _mx(dsv)
            dbs_ref[...] += dsv
            return carry

        lax.fori_loop(0, nb, gating, 0)

        def spatial_bwd(n, carry):
            rows = pl.ds(pl.multiple_of(n * BLK, BLK), BLK)
            for j in range(N_PAIRS):
                cols = slice(j * LANES, (j + 1) * LANES)
                dsv = dsv_s[rows, cols]
                dvn_s[rows, cols] = jnp.where(lo, _dot(wst_ref[2 * j], dsv), _dot(wst_ref[2 * j + 1], dsv))
            return carry

        lax.fori_loop(0, nb, spatial_bwd, 0)
        zero = jnp.zeros((BLK, LANES), MXU_DTYPE)
        for j in range(N_PAIRS):
            cols = slice(j * LANES, (j + 1) * LANES)
            chunks = [dsv_s[n * BLK : (n + 1) * BLK, cols] for n in range(nb)]
            vn_all = jnp.concatenate([vnb_s[n * BLK : (n + 1) * BLK, cols] for n in range(nb)], axis=1)
            dws_ref[2 * j] += _dot_nt(jnp.concatenate([jnp.where(lo, c, zero) for c in chunks], axis=1), vn_all)
            dws_ref[2 * j + 1] += _dot_nt(jnp.concatenate([jnp.where(lo, zero, c) for c in chunks], axis=1), vn_all)
        for j in range(D_GM // WIDE):
            cols = slice(j * WIDE, (j + 1) * WIDE)
            dpb_ref[:, D_ATTN + D_GM + j * WIDE : D_ATTN + D_GM + (j + 1) * WIDE] = _half_rms_bwd(
                dvn_s[:, cols], vn_s[:, cols], rv_s[:, cols], ones
            ).astype(dpb_ref.dtype)
        g_ref[...] += _dot_tn(y_s[...], _mx(dxv))

    return pl.pallas_call(
        body,
        name=name,
        grid=(seq // tile,),
        in_specs=[
            _row_spec(tile, d),
            _row_spec(tile, D_REST),
            _row_spec(tile, D_ATTN),
            _full_spec((1, d)),
            _full_spec((D_MIX, d)),
            _full_spec((N_GROUPS, BLK, BLK)),
            _full_spec((N_GROUPS, BLK, BLK)),
            _full_spec((BLK, D_GM)),
        ],
        out_specs=[
            _row_spec(tile, D_REST),
            _row_spec(tile, D_ATTN),
            _full_spec((D_MIX, d)),
            _full_spec((N_GROUPS, BLK, BLK)),
            _full_spec((BLK, D_GM)),
        ],
        out_shape=[
            jax.ShapeDtypeStruct((seq, D_REST), MXU_DTYPE),
            jax.ShapeDtypeStruct((seq, D_ATTN), F32),
            jax.ShapeDtypeStruct((D_MIX, d), F32),
            jax.ShapeDtypeStruct((N_GROUPS, BLK, BLK), F32),
            jax.ShapeDtypeStruct((BLK, D_GM), F32),
        ],
        scratch_shapes=[
            pltpu.VMEM((tile, D_MIX), MXU_DTYPE),
            pltpu.VMEM((tile, D_MIX), F32),
            pltpu.VMEM((tile, D_GM), F32),
            pltpu.VMEM((tile, D_GM), F32),
            pltpu.VMEM((tile, D_GM), MXU_DTYPE),
            pltpu.VMEM((tile, D_GM), F32),
            pltpu.VMEM((tile, D_GM), MXU_DTYPE),
            pltpu.VMEM((tile, D_GM), F32),
        ],
        compiler_params=_params(("arbitrary",)),
    )(dxn, pb, o, gate, w_out, w_s, w_s_t, b_st)


def _attn_bwd(pa, o, do, q_gain2, k_gain2, sink, name, scatter=()):
    seq = pa.shape[0]
    tile = min(TOKEN_TILE, seq)
    nb = tile // BLK
    nt = seq // tile
    ext = tile + 2 * BLK
    n_ride = len(scatter)
    riding = n_ride > 0

    def body(sink_ref, qkv_ref, kvp_ref, kvn_ref, o_ref, do_ref, qg_ref, kg_ref, *rest):
        i = pl.program_id(0)
        blocks, rest = rest[:n_ride], rest[n_ride:]
        dq_ref, dkv_ref, hp_ref, hn_ref, dqg_ref, dkg_ref, dsk_ref = rest[:7]
        landing, rest = rest[7 : 7 + n_ride], rest[7 + n_ride :]
        (qs, dos, qhat_s, rq_s, ks, kr, vs, vr, khat_s, rk_s, dqn_s, dka, dva, bias_s, s_scr, dp_scr, p_scr, ds_scr) = rest[:18]
        if riding:
            start, finish = _scatter_stages(blocks, landing, *rest[18:])
            at_start, _, at_finish = _rider_steps(nt)
            pl.when(i == at_start)(start)

        @pl.when(i == 0)
        def _():
            dqg_ref[...] = jnp.zeros_like(dqg_ref)
            dkg_ref[...] = jnp.zeros_like(dkg_ref)
            dsk_ref[...] = jnp.zeros_like(dsk_ref)
            _fill_attn_bias(bias_s)

        ones = _half_ones()
        lo = _lane_lo(BLK)
        lo_t = _lane_lo(tile)
        lo_c = _lane_lo(ROW_CHUNK)
        qg = qg_ref[...] * Q_SCALE
        kg = kg_ref[...]
        _stage_keys(kvp_ref, qkv_ref, kvn_ref, kg, ones, tile, ks, kr, vs, vr, khat_s, rk_s)
        for j in range(N_PAIRS):
            cols = slice(j * LANES, (j + 1) * LANES)
            qhat, rq = _half_rms(qkv_ref[:, cols], ones)
            qhat_s[:, cols] = qhat
            rq_s[:, cols] = rq
            _stage_queries(qhat * qg, lo_t, j, nb, qs)
            _stage_queries(do_ref[:, cols], lo_t, j, nb, dos)
        dka[...] = jnp.zeros_like(dka)
        dva[...] = jnp.zeros_like(dva)
        head_lane = lax.broadcasted_iota(jnp.int32, (1, LANES), 1)

        def block(n, dsink):
            r0 = pl.multiple_of(n * BLK, BLK)
            krows = pl.ds(r0, 3 * BLK)
            edge = _edge_mask(i * nb + n, seq)
            for v in range(2):
                s_scr[v] = _dot_nt(qs[n, v], (kr if v else ks)[krows, :])
                dp_scr[v] = _dot_nt(dos[n, v], (vr if v else vs)[krows, :])
            for h in range(N_HEADS):
                v, slot = HEAD_SLOT[h]
                j, a = divmod(h, 2)
                cols = slice(j * LANES, (j + 1) * LANES)
                sink_h = sink_ref[h]
                sink_part = jnp.zeros((ROW_CHUNK, 1), F32)
                for rc in range(0, BLK, ROW_CHUNK):
                    rows = slice(slot * BLK + rc, slot * BLK + rc + ROW_CHUNK)
                    trows = pl.ds(pl.multiple_of(r0 + rc, ROW_CHUNK), ROW_CHUNK)
                    s = s_scr[v, rows, :] + bias_s[h, rc : rc + ROW_CHUNK, :] + edge
                    m = jnp.maximum(jnp.max(s, axis=-1, keepdims=True), sink_h)
                    p = jnp.exp(s - m)
                    e_sink = jnp.exp(sink_h - m)
                    inv = 1.0 / (jnp.sum(p, axis=-1, keepdims=True) + e_sink)
                    pn = p * inv
                    prod = do_ref[trows, cols] * o_ref[trows, cols]
                    prod = jnp.where(lo_c, prod, 0.0) if a == 0 else jnp.where(lo_c, 0.0, prod)
                    dcol = jnp.sum(prod, axis=-1, keepdims=True)
                    ds_scr[v, rows, :] = _mx(pn * (dp_scr[v, rows, :] - dcol))
                    p_scr[v, rows, :] = _mx(pn)
                    sink_part = sink_part + (e_sink * inv) * dcol
                dsink = dsink - jnp.where(head_lane == h, jnp.sum(sink_part, axis=0, keepdims=True), 0.0)
            dqv = []
            for v in range(2):
                dqv.append(_dot(ds_scr[v], (kr if v else ks)[krows, :]))
                dka[v, krows, :] += _dot_tn(ds_scr[v], qs[n, v])
                dva[v, krows, :] += _dot_tn(p_scr[v], dos[n, v])
            for j in range(N_PAIRS):
                dqn_s[pl.ds(r0, BLK), j * LANES : (j + 1) * LANES] = _unstack_pair(dqv, j, lo)
            return dsink

        dsink = lax.fori_loop(0, nb, block, jnp.zeros((1, LANES), F32))
        dsk_ref[...] += jnp.broadcast_to(dsink, (SUBLANES, LANES))
        for j in range(N_PAIRS):
            cols = slice(j * LANES, (j + 1) * LANES)
            dqn = dqn_s[:, cols]
            qhat = qhat_s[:, cols]
            dqg_ref[:, cols] += _group_rows(dqn * qhat) * Q_SCALE
            dq_ref[:, cols] = _half_rms_bwd(dqn * qg, qhat, rq_s[:, cols], ones).astype(dq_ref.dtype)
        dkn = dka[0] + pltpu.roll(dka[1], HEAD_DIM, 1)
        khat = khat_s[...]
        dkg_ref[...] += _group_rows(dkn * khat)
        dk = _half_rms_bwd(dkn * kg, khat, rk_s[...], ones)
        dv = dva[0] + pltpu.roll(dva[1], HEAD_DIM, 1)
        hp_ref[:, 0:D_KV] = dk[0:BLK]
        hp_ref[:, D_KV : 2 * D_KV] = dv[0:BLK]
        dkv_ref[:, 0:D_KV] = dk[BLK : BLK + tile]
        dkv_ref[:, D_KV : 2 * D_KV] = dv[BLK : BLK + tile]
        hn_ref[:, 0:D_KV] = dk[BLK + tile : ext]
        hn_ref[:, D_KV : 2 * D_KV] = dv[BLK + tile : ext]
        if riding:
            pl.when(i == at_finish)(finish)

    prev, nxt = _halo_specs(tile, seq)
    vec = _full_spec((1, LANES))
    halo = pl.BlockSpec((None, BLK, 2 * D_KV), lambda i: (i, 0, 0))
    return pl.pallas_call(
        body,
        name=name,
        grid=(nt,),
        in_specs=[SMEM_SPEC, _row_spec(tile, D_QKV), prev, nxt, _row_spec(tile, D_ATTN), _row_spec(tile, D_ATTN), vec, vec]
        + [HBM_SPEC] * n_ride,
        out_specs=[
            _row_spec(tile, D_ATTN),
            _row_spec(tile, 2 * D_KV),
            halo,
            halo,
            _full_spec((SUBLANES, D_ATTN)),
            _full_spec((SUBLANES, LANES)),
            _full_spec((SUBLANES, LANES)),
        ]
        + [HBM_SPEC] * n_ride,
        out_shape=[
            jax.ShapeDtypeStruct((seq, D_ATTN), MXU_DTYPE),
            jax.ShapeDtypeStruct((seq, 2 * D_KV), F32),
            jax.ShapeDtypeStruct((nt, BLK, 2 * D_KV), F32),
            jax.ShapeDtypeStruct((nt, BLK, 2 * D_KV), F32),
            jax.ShapeDtypeStruct((SUBLANES, D_ATTN), F32),
            jax.ShapeDtypeStruct((SUBLANES, LANES), F32),
            jax.ShapeDtypeStruct((SUBLANES, LANES), F32),
        ]
        + _landing_shapes(scatter),
        scratch_shapes=[
            pltpu.VMEM((nb, 2, STACK, LANES), MXU_DTYPE),
            pltpu.VMEM((nb, 2, STACK, LANES), MXU_DTYPE),
            pltpu.VMEM((tile, D_ATTN), F32),
            pltpu.VMEM((tile, D_ATTN), F32),
            pltpu.VMEM((ext, LANES), MXU_DTYPE),
            pltpu.VMEM((ext, LANES), MXU_DTYPE),
            pltpu.VMEM((ext, LANES), MXU_DTYPE),
            pltpu.VMEM((ext, LANES), MXU_DTYPE),
            pltpu.VMEM((ext, LANES), F32),
            pltpu.VMEM((ext, LANES), F32),
            pltpu.VMEM((tile, D_ATTN), F32),
            pltpu.VMEM((2, ext, LANES), F32),
            pltpu.VMEM((2, ext, LANES), F32),
            pltpu.VMEM((N_HEADS, BLK, 3 * BLK), F32),
            pltpu.VMEM((2, STACK, 3 * BLK), F32),
            pltpu.VMEM((2, STACK, 3 * BLK), F32),
            pltpu.VMEM((2, STACK, 3 * BLK), MXU_DTYPE),
            pltpu.VMEM((2, STACK, 3 * BLK), MXU_DTYPE),
        ]
        + _rider_sems(n_ride),
        compiler_params=_params(("arbitrary",)),
    )(sink, pa, pa, pa, o, do, q_gain2, k_gain2, *scatter)


def _halo_in_specs(tile, nt):
    from_prev = pl.BlockSpec((None, BLK, 2 * D_KV), lambda i: (jnp.maximum(i - 1, 0), 0, 0))
    from_next = pl.BlockSpec((None, BLK, 2 * D_KV), lambda i: (jnp.minimum(i + 1, nt - 1), 0, 0))
    return from_prev, from_next


def _landing_shapes(scatter):
    return [jax.ShapeDtypeStruct((N_DEV,) + b.shape[2:], b.dtype) for b in scatter]


def _rider_sems(n_ride):
    if not n_ride:
        return []
    return [pltpu.SemaphoreType.DMA((7 * n_ride,)), pltpu.SemaphoreType.DMA((7 * n_ride,)), pltpu.SemaphoreType.DMA((n_ride,))]


def _proj_bwd_dx(x, dxn, dq, dkvb, dpb, w_in_t, gain, scale1, name, scatter=(), tiles=None, into=None):
    seq, d = x.shape
    tile = min(TOKEN_TILE, seq)
    first, nt = tiles if tiles is not None else (0, seq // tile)
    n_ride = len(scatter)
    n_into = 0 if into is None else 1

    def row(width):
        return pl.BlockSpec((tile, width), lambda i: (i + first, 0))

    def body(x_ref, dxn_ref, dq_ref, dkvb_ref, dpb_ref, wt_ref, g_ref, s1_ref, *rest):
        i = pl.program_id(0)
        blocks, rest = rest[:n_ride], rest[n_ride + n_into :]
        dx_ref, c0_ref, c1_ref = rest[:3]
        landing, sems = rest[3 : 3 + n_ride], rest[3 + n_ride :]
        if n_ride:
            start, finish = _scatter_stages(blocks, landing, *sems)
            at_start, _, at_finish = _rider_steps(nt)
            pl.when(i == at_start)(start)

        @pl.when(i == 0)
        def _():
            c0_ref[...] = jnp.zeros_like(c0_ref)
            c1_ref[...] = jnp.zeros_like(c1_ref)

        dh = (
            _dot(dq_ref[...], wt_ref[0:D_ATTN, :])
            + _dot(dkvb_ref[...], wt_ref[D_ATTN:D_QKV, :])
            + _dot(dpb_ref[...], wt_ref[D_QKV:D_IN, :])
        )
        xv = x_ref[...]
        r = lax.rsqrt(jnp.mean(xv * xv, axis=-1, keepdims=True) + EPS)
        xn = xv * r
        c0_ref[...] += _group_rows(dh)
        c1_ref[...] += _group_rows(dh * xn)
        dxn_ = dh * (g_ref[...] * s1_ref[...])
        dx_ref[...] = dxn_ref[...] + r * (dxn_ - xn * jnp.mean(xn * dxn_, axis=-1, keepdims=True))
        if n_ride:
            pl.when(i == at_finish)(finish)

    vec = _full_spec((1, d))
    n_fixed = 8
    return pl.pallas_call(
        body,
        name=name,
        grid=(nt,),
        in_specs=[row(d), row(d), row(D_ATTN), row(2 * D_KV), row(D_REST), _full_spec((D_IN, d)), vec, vec]
        + [HBM_SPEC] * (n_ride + n_into),
        out_specs=[row(d), _full_spec((SUBLANES, d)), _full_spec((SUBLANES, d))] + [HBM_SPEC] * n_ride,
        out_shape=[
            jax.ShapeDtypeStruct((seq, d), F32),
            jax.ShapeDtypeStruct((SUBLANES, d), F32),
            jax.ShapeDtypeStruct((SUBLANES, d), F32),
        ]
        + _landing_shapes(scatter),
        scratch_shapes=_rider_sems(n_ride),
        input_output_aliases={n_fixed + n_ride: 0} if n_into else {},
        compiler_params=_params(("arbitrary",)),
    )(x, dxn, dq, dkvb, dpb, w_in_t, gain, scale1, *scatter, *([into] if n_into else []))


def _proj_bwd_dw(x, gain, scale1, shift, dq, dkv, halo_prev, halo_next, dpb, name):
    seq, d = x.shape
    tile = min(TOKEN_TILE, seq)
    nt = seq // tile
    assert tile >= 2 * BLK

    def body(x_ref, g_ref, s1_ref, sh_ref, dq_ref, dkv_ref, hn_ref, hp_ref, dpb_ref, dw_ref, dkvb_ref, acc):
        i = pl.program_id(0)

        @pl.when(i == 0)
        def _():
            acc[...] = jnp.zeros_like(acc)

        top = dkv_ref[0:BLK, :] + jnp.where(i > 0, hn_ref[...], 0.0)
        bot = dkv_ref[tile - BLK : tile, :] + jnp.where(i < nt - 1, hp_ref[...], 0.0)
        dkvb_ref[0:BLK, :] = top.astype(dkvb_ref.dtype)
        dkvb_ref[tile - BLK : tile, :] = bot.astype(dkvb_ref.dtype)
        if tile > 2 * BLK:
            dkvb_ref[BLK : tile - BLK, :] = dkv_ref[BLK : tile - BLK, :].astype(dkvb_ref.dtype)
        xv = x_ref[...]
        r = lax.rsqrt(jnp.mean(xv * xv, axis=-1, keepdims=True) + EPS)
        h = _mx((xv * r) * g_ref[...] * s1_ref[...] + sh_ref[...])
        acc[0:D_ATTN, :] += _dot_tn(dq_ref[...], h)
        acc[D_ATTN:D_QKV, :] += _dot_tn(dkvb_ref[...], h)
        acc[D_QKV:D_IN, :] += _dot_tn(dpb_ref[...], h)

        @pl.when(i == nt - 1)
        def _():
            dw_ref[...] = acc[...].astype(dw_ref.dtype)

    from_prev, from_next = _halo_in_specs(tile, nt)
    vec = _full_spec((1, d))
    return pl.pallas_call(
        body,
        name=name,
        grid=(nt,),
        in_specs=[
            _row_spec(tile, d),
            vec,
            vec,
            vec,
            _row_spec(tile, D_ATTN),
            _row_spec(tile, 2 * D_KV),
            from_prev,
            from_next,
            _row_spec(tile, D_REST),
        ],
        out_specs=[_full_spec((D_IN, d)), _row_spec(tile, 2 * D_KV)],
        out_shape=[jax.ShapeDtypeStruct((D_IN, d), jnp.bfloat16), jax.ShapeDtypeStruct((seq, 2 * D_KV), MXU_DTYPE)],
        scratch_shapes=[pltpu.VMEM((D_IN, d), F32)],
        compiler_params=_params(("arbitrary",)),
    )(x, gain, scale1, shift, dq, dkv, halo_next, halo_prev, dpb)


def _w_out_finish(g, w_out, gate, name):
    d_mix, d = g.shape

    def body(g_ref, w_ref, gate_ref, dw_ref, dgate_ref):
        gv = g_ref[...]
        dw_ref[...] = (gv * gate_ref[...]).astype(dw_ref.dtype)
        dgate_ref[...] = _group_rows(gv * w_ref[...].astype(F32))

    return pl.pallas_call(
        body,
        name=name,
        in_specs=[VMEM_SPEC, VMEM_SPEC, VMEM_SPEC],
        out_specs=[VMEM_SPEC, VMEM_SPEC],
        out_shape=[jax.ShapeDtypeStruct((d_mix, d), jnp.bfloat16), jax.ShapeDtypeStruct((SUBLANES, d), F32)],
        compiler_params=_params(),
    )(g, w_out, gate)


def _adamw_math(w, g, m, v):
    m = ADAM_B1 * m + (1.0 - ADAM_B1) * g
    v = ADAM_B2 * v + (1.0 - ADAM_B2) * (g * g)
    m_hat = m / (1.0 - ADAM_B1**ADAM_STEP)
    v_hat = v / (1.0 - ADAM_B2**ADAM_STEP)
    delta = -ADAM_LR * (m_hat / (jnp.sqrt(v_hat) + ADAM_EPS) + ADAM_WD * w)
    return delta, m, v


def _adamw(w, g, m, v, name):
    rows, cols = w.shape
    tile = min(TOKEN_TILE, rows)

    def body(w_ref, g_ref, m_ref, v_ref, d_ref, mo_ref, vo_ref):
        d_ref[...], mo_ref[...], vo_ref[...] = _adamw_math(w_ref[...], g_ref[...], m_ref[...], v_ref[...])

    spec = _row_spec(tile, cols)
    shape = jax.ShapeDtypeStruct((rows, cols), F32)
    return pl.pallas_call(
        body,
        name=name,
        grid=(rows // tile,),
        in_specs=[spec] * 4,
        out_specs=[spec] * 3,
        out_shape=[shape] * 3,
        compiler_params=_params(("parallel",)),
    )(w, g, m, v)


def _small_update(gathered, gathered_ws, w, m, v, ws, m_ws, v_ws):
    def body(ga_ref, gws_ref, w_ref, m_ref, v_ref, ws_ref, mws_ref, vws_ref, *outs):
        for src, refs, out in ((ga_ref, (w_ref, m_ref, v_ref), outs[0:4]), (gws_ref, (ws_ref, mws_ref, vws_ref), outs[4:8])):
            g = src[0].astype(F32)
            for j in range(1, N_DEV):
                g = g + src[j].astype(F32)
            out[0][...] = g
            out[1][...], out[2][...], out[3][...] = _adamw_math(refs[0][...], g, refs[1][...], refs[2][...])

    shapes = [jax.ShapeDtypeStruct(w.shape, F32)] * 4 + [jax.ShapeDtypeStruct(ws.shape, F32)] * 4
    return pl.pallas_call(
        body,
        name="small_update",
        in_specs=[VMEM_SPEC] * 8,
        out_specs=[VMEM_SPEC] * 8,
        out_shape=shapes,
        compiler_params=_params(),
    )(gathered, gathered_ws, w, m, v, ws, m_ws, v_ws)


def _ada_weight_grad(c_all, d_ada_cols):
    d = c_all.shape[-1]
    n_layers, _, width = d_ada_cols.shape

    def body(c_ref, da_ref, dw_ref):
        cv = c_ref[...]
        cond = cv * _sigmoid(cv)
        for l in range(n_layers):
            dw_ref[l] = lax.dot_general(
                cond, da_ref[l], (((0,), (0,)), ((), ())), preferred_element_type=F32, precision=lax.Precision.HIGHEST
            )

    return pl.pallas_call(
        body,
        name="ada_weight_grad",
        in_specs=[VMEM_SPEC, VMEM_SPEC],
        out_specs=VMEM_SPEC,
        out_shape=jax.ShapeDtypeStruct((n_layers, d, width), F32),
        compiler_params=_params(),
    )(c_all, d_ada_cols)


def _position():
    return lax.axis_index("x"), lax.axis_index("y"), lax.axis_index("c")


def _flip(pos, k):
    x, y, c = pos
    return (1 - x if k & 4 else x, 1 - y if k & 2 else y, 1 - c if k & 1 else c)


def _index(pos):
    x, y, c = pos
    return 4 * x + 2 * y + c


def _remote(src, dst, send_sem, recv_sem, to):
    return pltpu.make_async_remote_copy(
        src_ref=src, dst_ref=dst, send_sem=send_sem, recv_sem=recv_sem, device_id=to, device_id_type=MESH_ID
    )


def _all_gather_stages(slots, send_sems, recv_sems, sources=None, local_sems=None):
    me = _position()
    sibling = _flip(me, 1)
    others = (4, 2, 6)
    arrays = range(len(slots))

    def copy(t, k, block, to, own=False):
        slot = slots[t](_index(block))
        src = sources[t] if own and sources is not None else slot
        return _remote(src, slot, send_sems.at[7 * t + k], recv_sems.at[7 * t + k], to)

    def first(t):
        return [copy(t, 0, me, sibling, own=True)] + [copy(t, 1 + j, me, _flip(me, f), own=True) for j, f in enumerate(others)]

    def passed(t, j):
        return copy(t, 4 + j, _flip(me, others[j]), sibling)

    def local(t):
        return pltpu.make_async_copy(sources[t], slots[t](_index(me)), local_sems.at[t])

    def start():
        for t in arrays:
            if sources is not None:
                local(t).start()
            for cp in first(t):
                cp.start()

    def forward():
        for j, f in enumerate(others):
            for t in arrays:
                copy(t, 1 + j, _flip(me, f), me).wait_recv()
                passed(t, j).start()

    def finish():
        for t in arrays:
            copy(t, 0, sibling, me).wait_recv()
            for j, f in enumerate(others):
                copy(t, 4 + j, _flip(sibling, f), me).wait_recv()
        for t in arrays:
            for cp in first(t) + [passed(t, j) for j in range(len(others))]:
                cp.wait_send()
            if sources is not None:
                local(t).wait()

    return start, forward, finish


def _two_level_all_gather(slots, send_sems, recv_sems, between=None):
    start, forward, finish = _all_gather_stages(slots, send_sems, recv_sems)
    start()
    if between is not None:
        between()
    forward()
    finish()


def _row_block(ref, rows):
    return lambda j: ref.at[pl.ds(pl.multiple_of(j * rows, 16), rows), :]


def _scatter_stages(blocks, landing, send_sems, recv_sems, local_sems):
    me = _position()
    my = _index(me)
    arrays = range(len(blocks))

    def copy(t, k):
        px, py, pc = to = _flip(me, k)
        return _remote(blocks[t].at[2 * px + py, pc], landing[t].at[my], send_sems.at[7 * t + k - 1], recv_sems.at[7 * t + k - 1], to)

    def arrival(t, k):
        slot = landing[t].at[_index(_flip(me, k))]
        return _remote(slot, slot, send_sems.at[7 * t + k - 1], recv_sems.at[7 * t + k - 1], _flip(me, k))

    def local(t):
        x, y, c = me
        return pltpu.make_async_copy(blocks[t].at[2 * x + y, c], landing[t].at[my], local_sems.at[t])

    def start():
        for t in arrays:
            local(t).start()
            for k in range(1, N_DEV):
                copy(t, k).start()

    def finish():
        for t in arrays:
            for k in range(1, N_DEV):
                arrival(t, k).wait_recv()
        for t in arrays:
            for k in range(1, N_DEV):
                copy(t, k).wait_send()
            local(t).wait()

    return start, finish


def _ada_exchange(c_ref, w_ref, call_ref, parts_ref, sbuf, sem_s1, sem_r1, sem_s2, sem_r2):
    d = c_ref.shape[-1]
    n_layers = w_ref.shape[0]
    me = _position()
    my = _index(me)
    call_ref[my] = jnp.broadcast_to(c_ref[...], (SUBLANES, d))
    mine = call_ref.at[my]
    first = [_remote(mine, mine, sem_s1.at[k - 1], sem_r1.at[k - 1], _flip(me, k)) for k in range(1, N_DEV)]
    for cp in first:
        cp.start()
    for k in range(1, N_DEV):
        theirs = call_ref.at[_index(_flip(me, k))]
        _remote(theirs, theirs, sem_s1.at[k - 1], sem_r1.at[k - 1], _flip(me, k)).wait_recv()
    cv = call_ref[...].reshape(N_DEV * SUBLANES, d)
    cond = cv * _sigmoid(cv)
    for l in range(n_layers):
        rows = jnp.dot(cond, w_ref[l], preferred_element_type=F32, precision=lax.Precision.HIGHEST)
        for b in range(N_DEV):
            sbuf[b, l] = rows[b * SUBLANES : (b + 1) * SUBLANES]
    parts_ref[my] = sbuf[my]
    second = []
    for k in range(1, N_DEV):
        to = _flip(me, k)
        second.append(_remote(sbuf.at[_index(to)], parts_ref.at[my], sem_s2.at[k - 1], sem_r2.at[k - 1], to))
    for cp in second:
        cp.start()
    for k in range(1, N_DEV):
        theirs = parts_ref.at[_index(_flip(me, k))]
        _remote(theirs, theirs, sem_s2.at[k - 1], sem_r2.at[k - 1], _flip(me, k)).wait_recv()
    for cp in first + second:
        cp.wait_send()


def _gather_weights(w_in_t, w_out, c_row, w_ada):
    n_layers, rows_in, d = w_in_t.shape
    rows_out = w_out.shape[1]
    width = w_ada.shape[2]

    def body(wi_ref, wo_ref, c_ref, wa_ref, gi_ref, go_ref, si_ref, so_ref, call_ref, parts_ref, sbuf, send_sems, recv_sems, *ada_sems):
        my = _index(_position())
        si_ref[...] = wi_ref[...].astype(si_ref.dtype)
        so_ref[...] = wo_ref[...].astype(so_ref.dtype)
        gi_ref[pl.ds(pl.multiple_of(my * rows_in, 16), rows_in), :] = si_ref[0]
        go_ref[pl.ds(pl.multiple_of(my * rows_out, 16), rows_out), :] = so_ref[0]
        _two_level_all_gather(
            (_row_block(gi_ref, rows_in), _row_block(go_ref, rows_out)),
            send_sems,
            recv_sems,
            between=functools.partial(_ada_exchange, c_ref, wa_ref, call_ref, parts_ref, sbuf, *ada_sems),
        )

    return pl.pallas_call(
        body,
        name="gather_weights",
        in_specs=[VMEM_SPEC] * 4,
        out_specs=[VMEM_SPEC] * 6,
        out_shape=[
            jax.ShapeDtypeStruct((N_DEV * rows_in, d), MXU_DTYPE),
            jax.ShapeDtypeStruct((N_DEV * rows_out, d), MXU_DTYPE),
            jax.ShapeDtypeStruct(w_in_t.shape, MXU_DTYPE),
            jax.ShapeDtypeStruct(w_out.shape, MXU_DTYPE),
            jax.ShapeDtypeStruct((N_DEV, SUBLANES, d), F32),
            jax.ShapeDtypeStruct((N_DEV, n_layers, SUBLANES, width), F32),
        ],
        scratch_shapes=[
            pltpu.VMEM((N_DEV, n_layers, SUBLANES, width), F32),
            pltpu.SemaphoreType.DMA((14,)),
            pltpu.SemaphoreType.DMA((14,)),
        ]
        + [pltpu.SemaphoreType.DMA((N_DEV - 1,))] * 4,
        compiler_params=_params(),
    )(w_in_t, w_out, c_row, w_ada)


def _gather_small(packed, d_ws):
    def body(p_ref, ws_ref, g_ref, gws_ref, send_sems, recv_sems):
        my = _index(_position())
        g_ref[my] = p_ref[...]
        gws_ref[my] = ws_ref[...].astype(gws_ref.dtype)
        _two_level_all_gather((lambda j: g_ref.at[j], lambda j: gws_ref.at[j]), send_sems, recv_sems)

    return pl.pallas_call(
        body,
        name="gather_small",
        in_specs=[VMEM_SPEC, VMEM_SPEC],
        out_specs=[VMEM_SPEC, VMEM_SPEC],
        out_shape=[
            jax.ShapeDtypeStruct((N_DEV,) + packed.shape, F32),
            jax.ShapeDtypeStruct((N_DEV,) + d_ws.shape, jnp.bfloat16),
        ],
        scratch_shapes=[pltpu.SemaphoreType.DMA((14,)), pltpu.SemaphoreType.DMA((14,))],
        compiler_params=_params(),
    )(packed, d_ws)


def _scatter_finish(landed, name):
    n = len(landed)

    def body(*refs):
        for src, out in zip(refs[:n], refs[n:]):
            g = src[0].astype(F32)
            for j in range(1, N_DEV):
                g = g + src[j].astype(F32)
            out[...] = g

    return pl.pallas_call(
        body,
        name=name,
        in_specs=[VMEM_SPEC] * n,
        out_specs=[VMEM_SPEC] * n,
        out_shape=[jax.ShapeDtypeStruct(a.shape[1:], F32) for a in landed],
        compiler_params=_params(),
    )(*landed)


def _pack_rows(parts):
    rows, offsets, at = [], [], 0
    for p in parts:
        flat = p.reshape(-1)
        n = -(-flat.shape[0] // (SUBLANES * LANES)) * SUBLANES
        rows.append(jnp.pad(flat, (0, n * LANES - flat.shape[0])).reshape(n, LANES))
        offsets.append(at)
        at += n
    return jnp.concatenate(rows, axis=0), offsets


def _unpack_rows(packed, offsets, shapes):
    out = []
    for off, shape in zip(offsets, shapes):
        size = 1
        for s in shape:
            size *= s
        n = -(-size // (SUBLANES * LANES)) * SUBLANES
        out.append(packed[off : off + n].reshape(-1)[:size].reshape(shape))
    return out


def kernel(x, c, w_ada, b_ada, norm_gain, w_in, q_gain, k_gain, sink, w_s, b_s, w_out, loss_target, m_w_ada, m_b_ada, m_norm_gain, m_w_in, m_q_gain, m_k_gain, m_sink, m_w_s, m_b_s, m_w_out, v_w_ada, v_b_ada, v_norm_gain, v_w_in, v_q_gain, v_k_gain, v_sink, v_w_s, v_b_s, v_w_out):
    seq, d = x.shape[1], x.shape[2]
    n_layers = w_in.shape[0]
    w_cols = w_in.shape[2]
    ada_cols = w_ada.shape[2]
    my = _index(_position())
    xs = x.reshape(seq, d)
    target = loss_target.reshape(seq, d)

    w_in_t0, w_out0, shard_in, shard_out, c_all, ada_parts = _gather_weights(w_in.transpose(0, 2, 1), w_out, c, w_ada)
    w_in_ts, w_outs = [w_in_t0], [w_out0]
    ada = ada_parts[:, :, 0, :].transpose(1, 0, 2).reshape(n_layers, 3 * d) + b_ada
    shift, scale1, gate = ada[:, None, 0:d], 1.0 + ada[:, None, d : 2 * d], ada[:, None, 2 * d : 3 * d]
    gain = norm_gain[:, None, :]

    w_s_m = w_s.astype(MXU_DTYPE)
    w_s_t = w_s_m.transpose(0, 1, 3, 2)
    b_st = jnp.repeat(b_s.transpose(0, 2, 1), HEAD_DIM, axis=2)
    q_gain2 = jnp.tile(q_gain, (1, 2))[:, None, :]
    k_gain2 = jnp.tile(k_gain, (1, 2))[:, None, :]

    xl, saved = xs, []
    for l in range(n_layers):
        last = l == n_layers - 1
        pa, pb = _ln_proj_fwd(xl, gain[l], scale1[l], shift[l], w_in_ts[l], f"ln_proj_fwd_{l}")
        if last:
            o = _attn_fwd(pa, q_gain2[l], k_gain2[l], sink[l], f"attn_fwd_{l}")
        else:
            o, w_in_next, w_out_next = _attn_fwd(
                pa, q_gain2[l], k_gain2[l], sink[l], f"attn_fwd_{l}", gather=(shard_in[l + 1], shard_out[l + 1])
            )
            w_in_ts.append(w_in_next)
            w_outs.append(w_out_next)
        saved.append((xl, pa, pb, o))
        out = _mix_out_fwd(pb, o, xl, gate[l], w_outs[l], w_s_m[l], b_st[l], f"mix_out_fwd_{l}", target if last else None)
        if last:
            dx, sq_err = out
        else:
            xl = out

    g_w_in, g_w_out, small, d_ada_rows = [None] * n_layers, [None] * n_layers, [None] * n_layers, [None] * n_layers
    waiting = []
    for l in reversed(range(n_layers)):
        x_l, pa, pb, o = saved[l]
        dpb, do, g_acc, d_ws, d_bs = _mix_out_bwd(dx, pb, o, gate[l], w_outs[l], w_s_m[l], w_s_t[l], b_st[l], f"mix_out_bwd_{l}")
        dw_out, d_gate8 = _w_out_finish(g_acc, w_outs[l], gate[l], f"w_out_finish_{l}")
        riding = waiting + [(g_w_out, l, dw_out.reshape(4, 2, D_MIX // N_DEV, d))]
        attn = _attn_bwd(pa, o, do, q_gain2[l], k_gain2[l], sink[l], f"attn_bwd_{l}", scatter=tuple(b for _, _, b in riding))
        dq, dkv, halo_prev, halo_next, d_qg, d_kg, d_sk = attn[:7]
        for (dest, layer, _), total in zip(riding, _scatter_finish(attn[7:], f"scatter_finish_{l}")):
            dest[layer] = total.transpose(1, 0) if dest is g_w_in else total
        dw_in_t, dkvb = _proj_bwd_dw(x_l, gain[l], scale1[l], shift[l], dq, dkv, halo_prev, halo_next, dpb, f"proj_bwd_dw_{l}")
        blocks_in = dw_in_t.reshape(4, 2, w_cols, d)
        waiting = [(g_w_in, l, blocks_in)] if l > 0 else []
        dx_args = (x_l, dx, dq, dkvb, dpb, w_in_ts[l], gain[l], scale1[l])
        if l > 0:
            dx, c0, c1 = _proj_bwd_dx(*dx_args, f"proj_bwd_dx_{l}")
        else:
            n_tiles = seq // min(TOKEN_TILE, seq)
            n_tail = max(1, n_tiles // 8)
            assert n_tiles > n_tail
            part, c0_head, c1_head, landed = _proj_bwd_dx(
                *dx_args, f"proj_bwd_dx_{l}", scatter=(blocks_in,), tiles=(0, n_tiles - n_tail)
            )
            dx, c0_tail, c1_tail = _proj_bwd_dx(*dx_args, f"proj_bwd_dx_{l}_tail", tiles=(n_tiles - n_tail, n_tail), into=part)
            c0, c1 = c0_head + c0_tail, c1_head + c1_tail
            g_w_in[l] = _scatter_finish((landed,), "scatter_finish_in_0")[0].transpose(1, 0)
        c0s, c1s = c0.sum(axis=0), c1.sum(axis=0)
        d_ada_rows[l] = jnp.concatenate([c0s, norm_gain[l] * c1s, d_gate8.sum(axis=0)])
        small[l] = (
            scale1[l, 0] * c1s,
            d_qg.sum(axis=0).reshape(N_HEADS, HEAD_DIM).sum(axis=0),
            d_kg.sum(axis=0).reshape(2, HEAD_DIM).sum(axis=0),
            d_sk[0, 0:N_HEADS],
            d_bs.reshape(BLK, N_GROUPS, HEAD_DIM).sum(axis=2).transpose(1, 0),
            d_ws,
        )

    names = ("norm_gain", "q_gain", "k_gain", "sink", "b_s")
    stacked = [jnp.stack([small[l][t] for l in range(n_layers)]) for t in range(len(names))]
    d_ada = jnp.stack(d_ada_rows)
    packed, offsets = _pack_rows(stacked + [d_ada, sq_err[0, 0:1]])
    d_ws = jnp.stack([small[l][len(names)] for l in range(n_layers)]).reshape(-1, LANES)
    gathered, gathered_ws = _gather_small(packed, d_ws)
    no_weight = jnp.zeros((1,), F32)
    weights = (norm_gain, q_gain, k_gain, sink, b_s, b_ada, no_weight)
    moments_m = (m_norm_gain, m_q_gain, m_k_gain, m_sink, m_b_s, m_b_ada, no_weight)
    moments_v = (v_norm_gain, v_q_gain, v_k_gain, v_sink, v_b_s, v_b_ada, no_weight)
    w_pack, _ = _pack_rows(weights)
    m_pack, _ = _pack_rows(moments_m)
    v_pack, _ = _pack_rows(moments_v)
    shapes = [w.shape for w in weights]
    flat_ws = lambda a: a.reshape(-1, LANES)
    updated = _small_update(gathered, gathered_ws, w_pack, m_pack, v_pack, flat_ws(w_s), flat_ws(m_w_s), flat_ws(v_w_s))
    g_small, d_small, m_small, v_small = (_unpack_rows(p, offsets, shapes) for p in updated[0:4])
    ws_small = [p.reshape(w_s.shape) for p in updated[4:8]]
    loss = g_small[-1][0] * (0.5 / d)

    ada_off = offsets[-2]
    ada_n = -(-n_layers * 3 * d // (SUBLANES * LANES)) * SUBLANES
    d_ada_all = gathered[:, ada_off : ada_off + ada_n].reshape(N_DEV, -1)[:, : n_layers * 3 * d].reshape(N_DEV, n_layers, 3 * d)
    d_ada_cols = lax.dynamic_slice_in_dim(d_ada_all, my * ada_cols, ada_cols, axis=2)
    g_w_ada = _ada_weight_grad(c_all[:, 0, :], d_ada_cols.transpose(1, 0, 2))

    def update(w, g, m, v, name):
        shape = w.shape
        flat = lambda a: a.reshape(-1, shape[-1])
        return tuple(a.reshape(shape) for a in _adamw(flat(w), flat(g), flat(m), flat(v), name))

    g_w_in, g_w_out = jnp.stack(g_w_in), jnp.stack(g_w_out)
    upd_ada = update(w_ada, g_w_ada, m_w_ada, v_w_ada, "adamw_w_ada")
    upd_in = update(w_in, g_w_in, m_w_in, v_w_in, "adamw_w_in")
    upd_out = update(w_out, g_w_out, m_w_out, v_w_out, "adamw_w_out")

    def ordered(ada_, in_, out_, small_, ws):
        ng, qg, kg, sk, bs, ba, _ = small_
        return (ada_, ba, ng, in_, qg, kg, sk, ws, bs, out_)

    grads = ordered(g_w_ada, g_w_in, g_w_out, g_small, ws_small[0])
    deltas = ordered(upd_ada[0], upd_in[0], upd_out[0], d_small, ws_small[1])
    new_m = ordered(upd_ada[1], upd_in[1], upd_out[1], m_small, ws_small[2])
    new_v = ordered(upd_ada[2], upd_in[2], upd_out[2], v_small, ws_small[3])
    return (loss, dx.reshape(x.shape), *grads, *deltas, *new_m, *new_v)
```

```python
import functools

import jax
import jax.numpy as jnp
from jax import lax
from jax.experimental import pallas as pl
from jax.experimental.pallas import tpu as pltpu

F32 = jnp.float32
MXU_DTYPE = jnp.bfloat16
MESH_ID = pl.DeviceIdType.MESH

N_DEV = 8
HEAD_DIM = 64
N_HEADS = 8
Q_PER_KV = 4
D_ATTN = 512
D_KV = 128
D_GM = 512
N_GROUPS = 8
D_MIX = D_ATTN + D_GM
BLK = 128
LANES = 128
SUBLANES = 8
N_PAIRS = D_ATTN // LANES
D_QKV = D_ATTN + 2 * D_KV
D_REST = D_ATTN + 3 * D_GM
D_IN = D_QKV + D_REST
EPS = 1e-6
NEG_INF = -1e30
ALIBI_SLOPES = tuple(2.0 ** (-8.0 * (h + 1) / N_HEADS) for h in range(N_HEADS))
Q_SCALE = 1.0 / 8.0

ADAM_LR = 0.001
ADAM_B1 = 0.9
ADAM_B2 = 0.999
ADAM_EPS = 1e-08
ADAM_WD = 0.01
ADAM_STEP = 10

TOKEN_TILE = 512
VMEM_LIMIT_BYTES = 56 * 1024 * 1024


def _params(semantics=None):
    return pltpu.CompilerParams(dimension_semantics=semantics, vmem_limit_bytes=VMEM_LIMIT_BYTES)


def _dot(a, b):
    return jnp.dot(a, b, preferred_element_type=F32)


def _dot_nt(a, b):
    return lax.dot_general(a, b, (((1,), (1,)), ((), ())), preferred_element_type=F32)


def _dot_tn(a, b):
    return lax.dot_general(a, b, (((0,), (0,)), ((), ())), preferred_element_type=F32)


def _mx(v):
    return v.astype(MXU_DTYPE)


def _lane_lo(rows):
    return lax.broadcasted_iota(jnp.int32, (rows, LANES), 1) < HEAD_DIM


def _half_ones(width=LANES):
    r = jnp.right_shift(lax.broadcasted_iota(jnp.int32, (width, width), 0), 6)
    c = jnp.right_shift(lax.broadcasted_iota(jnp.int32, (width, width), 1), 6)
    return jnp.where(r == c, 1.0, 0.0).astype(jnp.bfloat16)


WIDE = 2 * LANES


def _half_sum(v, ones):
    p1 = v.astype(jnp.bfloat16)
    p2 = (v - p1.astype(F32)).astype(jnp.bfloat16)
    return _dot(p1, ones) + _dot(p2, ones)


def _half_rms(v, ones):
    r = lax.rsqrt(_half_sum(v * v, ones) * (1.0 / HEAD_DIM) + EPS)
    return v * r, r


def _half_rms_bwd(dy, vhat, r, ones):
    return r * (dy - vhat * (_half_sum(vhat * dy, ones) * (1.0 / HEAD_DIM)))


def _group_rows(v):
    rows, n = v.shape
    return v.reshape(rows // SUBLANES, SUBLANES, n).sum(axis=0)


def _sigmoid(v):
    return 1.0 / (1.0 + jnp.exp(-v))


ROW_CHUNK = 32
VARIANT_HEADS = ((0, 2, 5, 7), (1, 3, 4, 6))
HEAD_SLOT = {h: (v, s) for v, heads in enumerate(VARIANT_HEADS) for s, h in enumerate(heads)}
STACK = Q_PER_KV * BLK


def _fill_attn_bias(bias_s):
    qi = lax.broadcasted_iota(jnp.int32, (BLK, 3 * BLK), 0)
    ci = lax.broadcasted_iota(jnp.int32, (BLK, 3 * BLK), 1)
    dist = jnp.abs(ci - BLK - qi)
    distf = dist.astype(F32)
    window = dist <= BLK
    for kind, seen in enumerate((window & (ci >= BLK), window, window & (ci < 2 * BLK))):
        for h in range(N_HEADS):
            bias_s[kind, h] = jnp.where(seen, -(ALIBI_SLOPES[h] * distf), NEG_INF)


def _block_kind(block, seq):
    assert seq >= 2 * BLK
    return jnp.where(block == 0, 0, jnp.where(block == seq // BLK - 1, 2, 1))


def _stage_queries(qn, lo_t, j, nb, qs):
    for a in range(2):
        v, slot = HEAD_SLOT[2 * j + a]
        qm = _mx(jnp.where(lo_t, qn, 0.0) if a == 0 else jnp.where(lo_t, 0.0, qn))
        for n in range(nb):
            qs[n, v, slot * BLK : (slot + 1) * BLK, :] = qm[n * BLK : (n + 1) * BLK]


def _unstack_pair(stacked, j, lo):
    (v0, s0), (v1, s1) = HEAD_SLOT[2 * j], HEAD_SLOT[2 * j + 1]
    return jnp.where(lo, stacked[v0][s0 * BLK : (s0 + 1) * BLK], stacked[v1][s1 * BLK : (s1 + 1) * BLK])


def _stage_keys(kvp_ref, qkv_ref, kvn_ref, kg, ones, tile, ks, kr, vs, vr, khat_s=None, rk_s=None):
    pieces = (
        (0, BLK, kvp_ref[:, 0:D_KV], kvp_ref[:, D_KV : 2 * D_KV]),
        (BLK, tile, qkv_ref[:, D_ATTN : D_ATTN + D_KV], qkv_ref[:, D_ATTN + D_KV : D_QKV]),
        (BLK + tile, BLK, kvn_ref[:, 0:D_KV], kvn_ref[:, D_KV : 2 * D_KV]),
    )
    for r0, n, k, v in pieces:
        khat, rk = _half_rms(k, ones)
        kn = khat * kg
        ks[r0 : r0 + n, :] = _mx(kn)
        kr[r0 : r0 + n, :] = _mx(pltpu.roll(kn, HEAD_DIM, 1))
        vs[r0 : r0 + n, :] = _mx(v)
        vr[r0 : r0 + n, :] = _mx(pltpu.roll(v, HEAD_DIM, 1))
        if khat_s is not None:
            khat_s[r0 : r0 + n, :] = khat
            rk_s[r0 : r0 + n, :] = rk


def _halo_specs(tile, seq):
    nb = tile // BLK
    last = seq // BLK - 1
    kv_col = D_ATTN // (2 * D_KV)
    prev = pl.BlockSpec((BLK, 2 * D_KV), lambda i: (jnp.maximum(i * nb - 1, 0), kv_col))
    nxt = pl.BlockSpec((BLK, 2 * D_KV), lambda i: (jnp.minimum((i + 1) * nb, last), kv_col))
    return prev, nxt


def _row_spec(tile, width):
    return pl.BlockSpec((tile, width), lambda i: (i, 0))


def _full_spec(shape):
    nd = len(shape)
    return pl.BlockSpec(shape, lambda i: (0,) * nd)


SMEM_SPEC = pl.BlockSpec(memory_space=pltpu.SMEM)
VMEM_SPEC = pl.BlockSpec(memory_space=pltpu.VMEM)
HBM_SPEC = pl.BlockSpec(memory_space=pltpu.HBM)


def _ln_proj_fwd(x, gain, scale1, shift, w_in_t, name):
    seq, d = x.shape
    tile = min(TOKEN_TILE, seq)

    def body(x_ref, g_ref, s1_ref, sh_ref, wt_ref, pa_ref, pb_ref):
        xv = x_ref[...]
        r = lax.rsqrt(jnp.mean(xv * xv, axis=-1, keepdims=True) + EPS)
        h = _mx((xv * r) * g_ref[...] * s1_ref[...] + sh_ref[...])
        pa_ref[...] = _dot_nt(h, wt_ref[0:D_QKV, :])
        pb_ref[...] = _dot_nt(h, wt_ref[D_QKV:D_IN, :])

    vec = _full_spec((1, d))
    return pl.pallas_call(
        body,
        name=name,
        grid=(seq // tile,),
        in_specs=[_row_spec(tile, d), vec, vec, vec, _full_spec((D_IN, d))],
        out_specs=[_row_spec(tile, D_QKV), _row_spec(tile, D_REST)],
        out_shape=[jax.ShapeDtypeStruct((seq, D_QKV), F32), jax.ShapeDtypeStruct((seq, D_REST), F32)],
        compiler_params=_params(("parallel",)),
    )(x, gain, scale1, shift, w_in_t)


def _rider_steps(nt):
    return 0, (2 * nt) // 3, nt - 1


def _attn_fwd(pa, q_gain2, k_gain2, sink, name, gather=None):
    seq = pa.shape[0]
    tile = min(TOKEN_TILE, seq)
    nb = tile // BLK
    nt = seq // tile
    ext = tile + 2 * BLK
    riding = gather is not None

    def body(sink_ref, qkv_ref, kvp_ref, kvn_ref, qg_ref, kg_ref, *rest):
        i = pl.program_id(0)
        if riding:
            shard_in, shard_out, o_ref, full_in, full_out = rest[0:5]
            qs, ks, kr, vs, vr, bias_s, s_scr, p_scr, inv_scr, send_sems, recv_sems, local_sems = rest[5:]
            start, forward, finish = _all_gather_stages(
                (_row_block(full_in, shard_in.shape[0]), _row_block(full_out, shard_out.shape[0])),
                send_sems,
                recv_sems,
                sources=(shard_in, shard_out),
                local_sems=local_sems,
            )
            at_start, at_forward, at_finish = _rider_steps(nt)
            pl.when(i == at_start)(start)
        else:
            o_ref, qs, ks, kr, vs, vr, bias_s, s_scr, p_scr, inv_scr = rest

        @pl.when(i == 0)
        def _():
            _fill_attn_bias(bias_s)

        ones = _half_ones()
        lo = _lane_lo(BLK)
        lo_t = _lane_lo(tile)
        _stage_keys(kvp_ref, qkv_ref, kvn_ref, kg_ref[...], ones, tile, ks, kr, vs, vr)
        for j in range(N_PAIRS):
            qhat, _ = _half_rms(qkv_ref[:, j * LANES : (j + 1) * LANES], ones)
            _stage_queries(qhat * (qg_ref[...] * Q_SCALE), lo_t, j, nb, qs)

        def block(n, carry):
            r0 = pl.multiple_of(n * BLK, BLK)
            krows = pl.ds(r0, 3 * BLK)
            kind = _block_kind(i * nb + n, seq)
            for v in range(2):
                s_scr[v] = _dot_nt(qs[n, v], (kr if v else ks)[krows, :])
            for h in range(N_HEADS):
                v, slot = HEAD_SLOT[h]
                sink_h = sink_ref[h]
                for rc in range(0, BLK, ROW_CHUNK):
                    rows = slice(slot * BLK + rc, slot * BLK + rc + ROW_CHUNK)
                    s = s_scr[v, rows, :] + bias_s[kind, h, rc : rc + ROW_CHUNK, :]
                    m = jnp.maximum(jnp.max(s, axis=-1, keepdims=True), sink_h)
                    p = jnp.exp(s - m)
                    total = jnp.sum(p, axis=-1, keepdims=True) + jnp.exp(sink_h - m)
                    p_scr[v, rows, :] = _mx(p)
                    inv_scr[v, rows, :] = jnp.broadcast_to(1.0 / total, (ROW_CHUNK, LANES))
            outs = [_dot(p_scr[v], (vr if v else vs)[krows, :]) * inv_scr[v] for v in range(2)]
            for j in range(N_PAIRS):
                o_ref[pl.ds(r0, BLK), j * LANES : (j + 1) * LANES] = _unstack_pair(outs, j, lo)
            return carry

        lax.fori_loop(0, nb, block, 0)
        if riding:
            pl.when(i == at_forward)(forward)
            pl.when(i == at_finish)(finish)

    prev, nxt = _halo_specs(tile, seq)
    vec = _full_spec((1, LANES))
    in_specs = [SMEM_SPEC, _row_spec(tile, D_QKV), prev, nxt, vec, vec]
    out_specs = [_row_spec(tile, D_ATTN)]
    out_shape = [jax.ShapeDtypeStruct((seq, D_ATTN), F32)]
    scratch = [
        pltpu.VMEM((nb, 2, STACK, LANES), MXU_DTYPE),
        pltpu.VMEM((ext, LANES), MXU_DTYPE),
        pltpu.VMEM((ext, LANES), MXU_DTYPE),
        pltpu.VMEM((ext, LANES), MXU_DTYPE),
        pltpu.VMEM((ext, LANES), MXU_DTYPE),
        pltpu.VMEM((3, N_HEADS, BLK, 3 * BLK), F32),
        pltpu.VMEM((2, STACK, 3 * BLK), F32),
        pltpu.VMEM((2, STACK, 3 * BLK), MXU_DTYPE),
        pltpu.VMEM((2, STACK, LANES), F32),
    ]
    extra = ()
    if riding:
        extra = tuple(gather)
        in_specs += [HBM_SPEC] * 2
        out_specs += [HBM_SPEC] * 2
        out_shape += [jax.ShapeDtypeStruct((N_DEV * g.shape[0], g.shape[1]), g.dtype) for g in gather]
        scratch += [pltpu.SemaphoreType.DMA((14,)), pltpu.SemaphoreType.DMA((14,)), pltpu.SemaphoreType.DMA((2,))]
    out = pl.pallas_call(
        body,
        name=name,
        grid=(nt,),
        in_specs=in_specs,
        out_specs=out_specs,
        out_shape=out_shape,
        scratch_shapes=scratch,
        compiler_params=_params(("arbitrary",)),
    )(sink, pa, pa, pa, q_gain2, k_gain2, *extra)
    return out if riding else out[0]


def _mix_out_fwd(pb, o, x, gate, w_out, w_s, b_st, name, target=None):
    seq, d = x.shape
    tile = min(TOKEN_TILE, seq)
    nb = tile // BLK
    with_loss = target is not None

    def body(pb_ref, o_ref, x_ref, gate_ref, wo_ref, ws_ref, bs_ref, *rest):
        if with_loss:
            t_ref, xo_ref, acc_ref, y_s, vn_s = rest

            @pl.when(pl.program_id(0) == 0)
            def _():
                acc_ref[...] = jnp.zeros_like(acc_ref)
        else:
            xo_ref, y_s, vn_s = rest
        ones = _half_ones(WIDE)
        lo = _lane_lo(BLK)
        ga = pb_ref[:, 0:D_ATTN]
        y_s[:, 0:D_ATTN] = _mx(o_ref[...] * (ga * _sigmoid(ga)))
        for j in range(D_GM // WIDE):
            vhat, _ = _half_rms(pb_ref[:, 2 * D_GM + j * WIDE : 2 * D_GM + (j + 1) * WIDE], ones)
            vn_s[:, j * WIDE : (j + 1) * WIDE] = _mx(vhat)

        def chunk(n, carry):
            rows = pl.ds(pl.multiple_of(n * BLK, BLK), BLK)
            for j in range(N_PAIRS):
                cols = slice(j * LANES, (j + 1) * LANES)
                vn = vn_s[rows, cols]
                sv = jnp.where(lo, _dot(ws_ref[2 * j], vn), _dot(ws_ref[2 * j + 1], vn)) + bs_ref[:, cols]
                u = pb_ref[rows, D_ATTN + j * LANES : D_ATTN + (j + 1) * LANES]
                gg = pb_ref[rows, D_ATTN + 2 * D_GM + j * LANES : D_ATTN + 2 * D_GM + (j + 1) * LANES]
                y_s[rows, D_ATTN + j * LANES : D_ATTN + (j + 1) * LANES] = _mx((u * sv) * (gg * _sigmoid(gg)))
            return carry

        lax.fori_loop(0, nb, chunk, 0)
        y = x_ref[...] + gate_ref[...] * _dot(y_s[...], wo_ref[...])
        if with_loss:
            e = y - t_ref[...]
            xo_ref[...] = e * (1.0 / d)
            acc_ref[...] += jnp.sum(jnp.sum(e * e, axis=-1, keepdims=True), axis=0, keepdims=True)
        else:
            xo_ref[...] = y

    row = _row_spec(tile, d)
    acc_shape = (SUBLANES, LANES)
    return pl.pallas_call(
        body,
        name=name,
        grid=(seq // tile,),
        in_specs=[
            _row_spec(tile, D_REST),
            _row_spec(tile, D_ATTN),
            row,
            _full_spec((1, d)),
            _full_spec((D_MIX, d)),
            _full_spec((N_GROUPS, BLK, BLK)),
            _full_spec((BLK, D_GM)),
        ]
        + ([row] if with_loss else []),
        out_specs=[row, _full_spec(acc_shape)] if with_loss else row,
        out_shape=[jax.ShapeDtypeStruct((seq, d), F32), jax.ShapeDtypeStruct(acc_shape, F32)]
        if with_loss
        else jax.ShapeDtypeStruct((seq, d), F32),
        scratch_shapes=[pltpu.VMEM((tile, D_MIX), MXU_DTYPE), pltpu.VMEM((tile, D_GM), MXU_DTYPE)],
        compiler_params=_params(("arbitrary",) if with_loss else ("parallel",)),
    )(pb, o, x, gate, w_out, w_s, b_st, *([target] if with_loss else []))


def _mix_out_bwd(dxn, pb, o, gate, w_out, w_s, w_s_t, b_st, name):
    seq, d = dxn.shape
    tile = min(TOKEN_TILE, seq)
    nb = tile // BLK

    def body(dxn_ref, pb_ref, o_ref, gate_ref, wo_ref, ws_ref, wst_ref, bs_ref,
             dpb_ref, do_ref, g_ref, dws_ref, dbs_ref, y_s, dy_s, vn_s, rv_s, vnb_s, sv_s, dsv_s, dvn_s):
        @pl.when(pl.program_id(0) == 0)
        def _():
            g_ref[...] = jnp.zeros_like(g_ref)
            dws_ref[...] = jnp.zeros_like(dws_ref)
            dbs_ref[...] = jnp.zeros_like(dbs_ref)

        ones = _half_ones(WIDE)
        lo = _lane_lo(BLK)
        c_u = slice(D_ATTN, D_ATTN + D_GM)
        c_vg = slice(D_ATTN + D_GM, D_ATTN + 2 * D_GM)
        c_gg = slice(D_ATTN + 2 * D_GM, D_REST)
        dxv = dxn_ref[...]
        dy_s[...] = _dot_nt(_mx(dxv * gate_ref[...]), wo_ref[...])
        ga = pb_ref[:, 0:D_ATTN]
        sig = _sigmoid(ga)
        sil = ga * sig
        ov = o_ref[...]
        y_s[:, 0:D_ATTN] = _mx(ov * sil)
        da = dy_s[:, 0:D_ATTN]
        do_ref[...] = da * sil
        dpb_ref[:, 0:D_ATTN] = (da * ov * (sig * (1.0 + ga * (1.0 - sig)))).astype(dpb_ref.dtype)
        for j in range(D_GM // WIDE):
            cols = slice(j * WIDE, (j + 1) * WIDE)
            vhat, rv = _half_rms(pb_ref[:, 2 * D_GM + j * WIDE : 2 * D_GM + (j + 1) * WIDE], ones)
            vn_s[:, cols] = vhat
            rv_s[:, cols] = rv
            vnb_s[:, cols] = _mx(vhat)

        def spatial_fwd(n, carry):
            rows = pl.ds(pl.multiple_of(n * BLK, BLK), BLK)
            for j in range(N_PAIRS):
                cols = slice(j * LANES, (j + 1) * LANES)
                vn = vnb_s[rows, cols]
                sv_s[rows, cols] = jnp.where(lo, _dot(ws_ref[2 * j], vn), _dot(ws_ref[2 * j + 1], vn)) + bs_ref[:, cols]
            return carry

        lax.fori_loop(0, nb, spatial_fwd, 0)

        def gating(n, carry):
            rows = pl.ds(pl.multiple_of(n * BLK, BLK), BLK)
            sv = sv_s[rows, :]
            u = pb_ref[rows, c_u]
            gg = pb_ref[rows, c_gg]
            sg = _sigmoid(gg)
            silg = gg * sg
            m0 = u * sv
            y_s[rows, D_ATTN:D_MIX] = _mx(m0 * silg)
            dm = dy_s[rows, D_ATTN:D_MIX]
            dm0 = dm * silg
            dpb_ref[rows, c_gg] = (dm * m0 * (sg * (1.0 + gg * (1.0 - sg)))).astype(dpb_ref.dtype)
            dpb_ref[rows, c_u] = (dm0 * sv).astype(dpb_ref.dtype)
            dsv = dm0 * u
            dsv_s[rows, :] = _mx(dsv)
            dbs_ref[...] += dsv
            return carry

        lax.fori_loop(0, nb, gating, 0)

        def spatial_bwd(n, carry):
            rows = pl.ds(pl.multiple_of(n * BLK, BLK), BLK)
            for j in range(N_PAIRS):
                cols = slice(j * LANES, (j + 1) * LANES)
                dsv = dsv_s[rows, cols]
                dvn_s[rows, cols] = jnp.where(lo, _dot(wst_ref[2 * j], dsv), _dot(wst_ref[2 * j + 1], dsv))
            return carry

        lax.fori_loop(0, nb, spatial_bwd, 0)
        zero = jnp.zeros((BLK, LANES), MXU_DTYPE)
        for j in range(N_PAIRS):
            cols = slice(j * LANES, (j + 1) * LANES)
            chunks = [dsv_s[n * BLK : (n + 1) * BLK, cols] for n in range(nb)]
            vn_all = jnp.concatenate([vnb_s[n * BLK : (n + 1) * BLK, cols] for n in range(nb)], axis=1)
            dws_ref[2 * j] += _dot_nt(jnp.concatenate([jnp.where(lo, c, zero) for c in chunks], axis=1), vn_all)
            dws_ref[2 * j + 1] += _dot_nt(jnp.concatenate([jnp.where(lo, zero, c) for c in chunks], axis=1), vn_all)
        for j in range(D_GM // WIDE):
            cols = slice(j * WIDE, (j + 1) * WIDE)
            dpb_ref[:, D_ATTN + D_GM + j * WIDE : D_ATTN + D_GM + (j + 1) * WIDE] = _half_rms_bwd(
                dvn_s[:, cols], vn_s[:, cols], rv_s[:, cols], ones
            ).astype(dpb_ref.dtype)
        g_ref[...] += _dot_tn(y_s[...], _mx(dxv))

    return pl.pallas_call(
        body,
        name=name,
        grid=(seq // tile,),
        in_specs=[
            _row_spec(tile, d),
            _row_spec(tile, D_REST),
            _row_spec(tile, D_ATTN),
            _full_spec((1, d)),
            _full_spec((D_MIX, d)),
            _full_spec((N_GROUPS, BLK, BLK)),
            _full_spec((N_GROUPS, BLK, BLK)),
            _full_spec((BLK, D_GM)),
        ],
        out_specs=[
            _row_spec(tile, D_REST),
            _row_spec(tile, D_ATTN),
            _full_spec((D_MIX, d)),
            _full_spec((N_GROUPS, BLK, BLK)),
            _full_spec((BLK, D_GM)),
        ],
        out_shape=[
            jax.ShapeDtypeStruct((seq, D_REST), MXU_DTYPE),
            jax.ShapeDtypeStruct((seq, D_ATTN), F32),
            jax.ShapeDtypeStruct((D_MIX, d), F32),
            jax.ShapeDtypeStruct((N_GROUPS, BLK, BLK), F32),
            jax.ShapeDtypeStruct((BLK, D_GM), F32),
        ],
        scratch_shapes=[
            pltpu.VMEM((tile, D_MIX), MXU_DTYPE),
            pltpu.VMEM((tile, D_MIX), F32),
            pltpu.VMEM((tile, D_GM), F32),
            pltpu.VMEM((tile, D_GM), F32),
            pltpu.VMEM((tile, D_GM), MXU_DTYPE),
            pltpu.VMEM((tile, D_GM), F32),
            pltpu.VMEM((tile, D_GM), MXU_DTYPE),
            pltpu.VMEM((tile, D_GM), F32),
        ],
        compiler_params=_params(("arbitrary",)),
    )(dxn, pb, o, gate, w_out, w_s, w_s_t, b_st)


def _attn_bwd(pa, o, do, q_gain2, k_gain2, sink, name, scatter=()):
    seq = pa.shape[0]
    tile = min(TOKEN_TILE, seq)
    nb = tile // BLK
    nt = seq // tile
    ext = tile + 2 * BLK
    n_ride = len(scatter)
    riding = n_ride > 0

    def body(sink_ref, qkv_ref, kvp_ref, kvn_ref, o_ref, do_ref, qg_ref, kg_ref, *rest):
        i = pl.program_id(0)
        blocks, rest = rest[:n_ride], rest[n_ride:]
        dq_ref, dkv_ref, hp_ref, hn_ref, dqg_ref, dkg_ref, dsk_ref = rest[:7]
        landing, rest = rest[7 : 7 + n_ride], rest[7 + n_ride :]
        (qs, dos, qhat_s, rq_s, ks, kr, vs, vr, khat_s, rk_s, dqn_s, dka, dva, bias_s, s_scr, dp_scr, p_scr, ds_scr) = rest[:18]
        if riding:
            start, finish = _scatter_stages(blocks, landing, *rest[18:])
            at_start, _, at_finish = _rider_steps(nt)
            pl.when(i == at_start)(start)

        @pl.when(i == 0)
        def _():
            dqg_ref[...] = jnp.zeros_like(dqg_ref)
            dkg_ref[...] = jnp.zeros_like(dkg_ref)
            dsk_ref[...] = jnp.zeros_like(dsk_ref)
            _fill_attn_bias(bias_s)

        ones = _half_ones()
        lo = _lane_lo(BLK)
        lo_t = _lane_lo(tile)
        lo_c = _lane_lo(ROW_CHUNK)
        qg = qg_ref[...] * Q_SCALE
        kg = kg_ref[...]
        _stage_keys(kvp_ref, qkv_ref, kvn_ref, kg, ones, tile, ks, kr, vs, vr, khat_s, rk_s)
        for j in range(N_PAIRS):
            cols = slice(j * LANES, (j + 1) * LANES)
            qhat, rq = _half_rms(qkv_ref[:, cols], ones)
            qhat_s[:, cols] = qhat
            rq_s[:, cols] = rq
            _stage_queries(qhat * qg, lo_t, j, nb, qs)
            _stage_queries(do_ref[:, cols], lo_t, j, nb, dos)
        dka[...] = jnp.zeros_like(dka)
        dva[...] = jnp.zeros_like(dva)
        head_lane = lax.broadcasted_iota(jnp.int32, (1, LANES), 1)

        def block(n, dsink):
            r0 = pl.multiple_of(n * BLK, BLK)
            krows = pl.ds(r0, 3 * BLK)
            kind = _block_kind(i * nb + n, seq)
            for v in range(2):
                s_scr[v] = _dot_nt(qs[n, v], (kr if v else ks)[krows, :])
                dp_scr[v] = _dot_nt(dos[n, v], (vr if v else vs)[krows, :])
            for h in range(N_HEADS):
                v, slot = HEAD_SLOT[h]
                j, a = divmod(h, 2)
                cols = slice(j * LANES, (j + 1) * LANES)
                sink_h = sink_ref[h]
                sink_part = jnp.zeros((ROW_CHUNK, 1), F32)
                for rc in range(0, BLK, ROW_CHUNK):
                    rows = slice(slot * BLK + rc, slot * BLK + rc + ROW_CHUNK)
                    trows = pl.ds(pl.multiple_of(r0 + rc, ROW_CHUNK), ROW_CHUNK)
                    s = s_scr[v, rows, :] + bias_s[kind, h, rc : rc + ROW_CHUNK, :]
                    m = jnp.maximum(jnp.max(s, axis=-1, keepdims=True), sink_h)
                    p = jnp.exp(s - m)
                    e_sink = jnp.exp(sink_h - m)
                    inv = 1.0 / (jnp.sum(p, axis=-1, keepdims=True) + e_sink)
                    pn = p * inv
                    prod = do_ref[trows, cols] * o_ref[trows, cols]
                    prod = jnp.where(lo_c, prod, 0.0) if a == 0 else jnp.where(lo_c, 0.0, prod)
                    dcol = jnp.sum(prod, axis=-1, keepdims=True)
                    ds_scr[v, rows, :] = _mx(pn * (dp_scr[v, rows, :] - dcol))
                    p_scr[v, rows, :] = _mx(pn)
                    sink_part = sink_part + (e_sink * inv) * dcol
                dsink = dsink - jnp.where(head_lane == h, jnp.sum(sink_part, axis=0, keepdims=True), 0.0)
            dqv = []
            for v in range(2):
                dqv.append(_dot(ds_scr[v], (kr if v else ks)[krows, :]))
                dka[v, krows, :] += _dot_tn(ds_scr[v], qs[n, v])
                dva[v, krows, :] += _dot_tn(p_scr[v], dos[n, v])
            for j in range(N_PAIRS):
                dqn_s[pl.ds(r0, BLK), j * LANES : (j + 1) * LANES] = _unstack_pair(dqv, j, lo)
            return dsink

        dsink = lax.fori_loop(0, nb, block, jnp.zeros((1, LANES), F32))
        dsk_ref[...] += jnp.broadcast_to(dsink, (SUBLANES, LANES))
        for j in range(N_PAIRS):
            cols = slice(j * LANES, (j + 1) * LANES)
            dqn = dqn_s[:, cols]
            qhat = qhat_s[:, cols]
            dqg_ref[:, cols] += _group_rows(dqn * qhat) * Q_SCALE
            dq_ref[:, cols] = _half_rms_bwd(dqn * qg, qhat, rq_s[:, cols], ones).astype(dq_ref.dtype)
        dkn = dka[0] + pltpu.roll(dka[1], HEAD_DIM, 1)
        khat = khat_s[...]
        dkg_ref[...] += _group_rows(dkn * khat)
        dk = _half_rms_bwd(dkn * kg, khat, rk_s[...], ones)
        dv = dva[0] + pltpu.roll(dva[1], HEAD_DIM, 1)
        hp_ref[:, 0:D_KV] = dk[0:BLK]
        hp_ref[:, D_KV : 2 * D_KV] = dv[0:BLK]
        dkv_ref[:, 0:D_KV] = dk[BLK : BLK + tile]
        dkv_ref[:, D_KV : 2 * D_KV] = dv[BLK : BLK + tile]
        hn_ref[:, 0:D_KV] = dk[BLK + tile : ext]
        hn_ref[:, D_KV : 2 * D_KV] = dv[BLK + tile : ext]
        if riding:
            pl.when(i == at_finish)(finish)

    prev, nxt = _halo_specs(tile, seq)
    vec = _full_spec((1, LANES))
    halo = pl.BlockSpec((None, BLK, 2 * D_KV), lambda i: (i, 0, 0))
    return pl.pallas_call(
        body,
        name=name,
        grid=(nt,),
        in_specs=[SMEM_SPEC, _row_spec(tile, D_QKV), prev, nxt, _row_spec(tile, D_ATTN), _row_spec(tile, D_ATTN), vec, vec]
        + [HBM_SPEC] * n_ride,
        out_specs=[
            _row_spec(tile, D_ATTN),
            _row_spec(tile, 2 * D_KV),
            halo,
            halo,
            _full_spec((SUBLANES, D_ATTN)),
            _full_spec((SUBLANES, LANES)),
            _full_spec((SUBLANES, LANES)),
        ]
        + [HBM_SPEC] * n_ride,
        out_shape=[
            jax.ShapeDtypeStruct((seq, D_ATTN), MXU_DTYPE),
            jax.ShapeDtypeStruct((seq, 2 * D_KV), F32),
            jax.ShapeDtypeStruct((nt, BLK, 2 * D_KV), F32),
            jax.ShapeDtypeStruct((nt, BLK, 2 * D_KV), F32),
            jax.ShapeDtypeStruct((SUBLANES, D_ATTN), F32),
            jax.ShapeDtypeStruct((SUBLANES, LANES), F32),
            jax.ShapeDtypeStruct((SUBLANES, LANES), F32),
        ]
        + _landing_shapes(scatter),
        scratch_shapes=[
            pltpu.VMEM((nb, 2, STACK, LANES), MXU_DTYPE),
            pltpu.VMEM((nb, 2, STACK, LANES), MXU_DTYPE),
            pltpu.VMEM((tile, D_ATTN), F32),
            pltpu.VMEM((tile, D_ATTN), F32),
            pltpu.VMEM((ext, LANES), MXU_DTYPE),
            pltpu.VMEM((ext, LANES), MXU_DTYPE),
            pltpu.VMEM((ext, LANES), MXU_DTYPE),
            pltpu.VMEM((ext, LANES), MXU_DTYPE),
            pltpu.VMEM((ext, LANES), F32),
            pltpu.VMEM((ext, LANES), F32),
            pltpu.VMEM((tile, D_ATTN), F32),
            pltpu.VMEM((2, ext, LANES), F32),
            pltpu.VMEM((2, ext, LANES), F32),
            pltpu.VMEM((3, N_HEADS, BLK, 3 * BLK), F32),
            pltpu.VMEM((2, STACK, 3 * BLK), F32),
            pltpu.VMEM((2, STACK, 3 * BLK), F32),
            pltpu.VMEM((2, STACK, 3 * BLK), MXU_DTYPE),
            pltpu.VMEM((2, STACK, 3 * BLK), MXU_DTYPE),
        ]
        + _rider_sems(n_ride),
        compiler_params=_params(("arbitrary",)),
    )(sink, pa, pa, pa, o, do, q_gain2, k_gain2, *scatter)


def _halo_in_specs(tile, nt):
    from_prev = pl.BlockSpec((None, BLK, 2 * D_KV), lambda i: (jnp.maximum(i - 1, 0), 0, 0))
    from_next = pl.BlockSpec((None, BLK, 2 * D_KV), lambda i: (jnp.minimum(i + 1, nt - 1), 0, 0))
    return from_prev, from_next


def _landing_shapes(scatter):
    return [jax.ShapeDtypeStruct((N_DEV,) + b.shape[2:], b.dtype) for b in scatter]


def _rider_sems(n_ride):
    if not n_ride:
        return []
    return [pltpu.SemaphoreType.DMA((7 * n_ride,)), pltpu.SemaphoreType.DMA((7 * n_ride,)), pltpu.SemaphoreType.DMA((n_ride,))]


def _proj_bwd_dx(x, dxn, dq, dkvb, dpb, w_in_t, gain, scale1, name, scatter=()):
    seq, d = x.shape
    tile = min(TOKEN_TILE, seq)
    nt = seq // tile
    n_ride = len(scatter)

    def row(width):
        return _row_spec(tile, width)

    def body(x_ref, dxn_ref, dq_ref, dkvb_ref, dpb_ref, wt_ref, g_ref, s1_ref, *rest):
        i = pl.program_id(0)
        blocks, rest = rest[:n_ride], rest[n_ride:]
        dx_ref, c0_ref, c1_ref = rest[:3]
        landing, sems = rest[3 : 3 + n_ride], rest[3 + n_ride :]
        if n_ride:
            start, finish = _scatter_stages(blocks, landing, *sems)
            at_start, _, at_finish = _rider_steps(nt)
            pl.when(i == at_start)(start)

        @pl.when(i == 0)
        def _():
            c0_ref[...] = jnp.zeros_like(c0_ref)
            c1_ref[...] = jnp.zeros_like(c1_ref)

        dh = (
            _dot(dq_ref[...], wt_ref[0:D_ATTN, :])
            + _dot(dkvb_ref[...], wt_ref[D_ATTN:D_QKV, :])
            + _dot(dpb_ref[...], wt_ref[D_QKV:D_IN, :])
        )
        xv = x_ref[...]
        r = lax.rsqrt(jnp.mean(xv * xv, axis=-1, keepdims=True) + EPS)
        xn = xv * r
        c0_ref[...] += _group_rows(dh)
        c1_ref[...] += _group_rows(dh * xn)
        dxn_ = dh * (g_ref[...] * s1_ref[...])
        dx_ref[...] = dxn_ref[...] + r * (dxn_ - xn * jnp.mean(xn * dxn_, axis=-1, keepdims=True))
        if n_ride:
            pl.when(i == at_finish)(finish)

    vec = _full_spec((1, d))
    return pl.pallas_call(
        body,
        name=name,
        grid=(nt,),
        in_specs=[row(d), row(d), row(D_ATTN), row(2 * D_KV), row(D_REST), _full_spec((D_IN, d)), vec, vec]
        + [HBM_SPEC] * n_ride,
        out_specs=[row(d), _full_spec((SUBLANES, d)), _full_spec((SUBLANES, d))] + [HBM_SPEC] * n_ride,
        out_shape=[
            jax.ShapeDtypeStruct((seq, d), F32),
            jax.ShapeDtypeStruct((SUBLANES, d), F32),
            jax.ShapeDtypeStruct((SUBLANES, d), F32),
        ]
        + _landing_shapes(scatter),
        scratch_shapes=_rider_sems(n_ride),
        compiler_params=_params(("arbitrary",)),
    )(x, dxn, dq, dkvb, dpb, w_in_t, gain, scale1, *scatter)


def _proj_bwd_dw(x, gain, scale1, shift, dq, dkv, halo_prev, halo_next, dpb, name):
    seq, d = x.shape
    tile = min(TOKEN_TILE, seq)
    nt = seq // tile
    assert tile >= 2 * BLK

    def body(x_ref, g_ref, s1_ref, sh_ref, dq_ref, dkv_ref, hn_ref, hp_ref, dpb_ref, dw_ref, dkvb_ref, acc):
        i = pl.program_id(0)

        @pl.when(i == 0)
        def _():
            acc[...] = jnp.zeros_like(acc)

        top = dkv_ref[0:BLK, :] + jnp.where(i > 0, hn_ref[...], 0.0)
        bot = dkv_ref[tile - BLK : tile, :] + jnp.where(i < nt - 1, hp_ref[...], 0.0)
        dkvb_ref[0:BLK, :] = top.astype(dkvb_ref.dtype)
        dkvb_ref[tile - BLK : tile, :] = bot.astype(dkvb_ref.dtype)
        if tile > 2 * BLK:
            dkvb_ref[BLK : tile - BLK, :] = dkv_ref[BLK : tile - BLK, :].astype(dkvb_ref.dtype)
        xv = x_ref[...]
        r = lax.rsqrt(jnp.mean(xv * xv, axis=-1, keepdims=True) + EPS)
        h = _mx((xv * r) * g_ref[...] * s1_ref[...] + sh_ref[...])
        acc[0:D_ATTN, :] += _dot_tn(dq_ref[...], h)
        acc[D_ATTN:D_QKV, :] += _dot_tn(dkvb_ref[...], h)
        acc[D_QKV:D_IN, :] += _dot_tn(dpb_ref[...], h)

        @pl.when(i == nt - 1)
        def _():
            dw_ref[...] = acc[...].astype(dw_ref.dtype)

    from_prev, from_next = _halo_in_specs(tile, nt)
    vec = _full_spec((1, d))
    return pl.pallas_call(
        body,
        name=name,
        grid=(nt,),
        in_specs=[
            _row_spec(tile, d),
            vec,
            vec,
            vec,
            _row_spec(tile, D_ATTN),
            _row_spec(tile, 2 * D_KV),
            from_prev,
            from_next,
            _row_spec(tile, D_REST),
        ],
        out_specs=[_full_spec((D_IN, d)), _row_spec(tile, 2 * D_KV)],
        out_shape=[jax.ShapeDtypeStruct((D_IN, d), jnp.bfloat16), jax.ShapeDtypeStruct((seq, 2 * D_KV), MXU_DTYPE)],
        scratch_shapes=[pltpu.VMEM((D_IN, d), F32)],
        compiler_params=_params(("arbitrary",)),
    )(x, gain, scale1, shift, dq, dkv, halo_next, halo_prev, dpb)


def _w_out_finish(g, w_out, gate, name):
    d_mix, d = g.shape

    def body(g_ref, w_ref, gate_ref, dw_ref, dgate_ref):
        gv = g_ref[...]
        dw_ref[...] = (gv * gate_ref[...]).astype(dw_ref.dtype)
        dgate_ref[...] = _group_rows(gv * w_ref[...].astype(F32))

    return pl.pallas_call(
        body,
        name=name,
        in_specs=[VMEM_SPEC, VMEM_SPEC, VMEM_SPEC],
        out_specs=[VMEM_SPEC, VMEM_SPEC],
        out_shape=[jax.ShapeDtypeStruct((d_mix, d), jnp.bfloat16), jax.ShapeDtypeStruct((SUBLANES, d), F32)],
        compiler_params=_params(),
    )(g, w_out, gate)


def _adamw_math(w, g, m, v):
    m = ADAM_B1 * m + (1.0 - ADAM_B1) * g
    v = ADAM_B2 * v + (1.0 - ADAM_B2) * (g * g)
    m_hat = m / (1.0 - ADAM_B1**ADAM_STEP)
    v_hat = v / (1.0 - ADAM_B2**ADAM_STEP)
    delta = -ADAM_LR * (m_hat / (jnp.sqrt(v_hat) + ADAM_EPS) + ADAM_WD * w)
    return delta, m, v


def _adamw(w, g, m, v, name):
    rows, cols = w.shape
    tile = min(TOKEN_TILE, rows)

    def body(w_ref, g_ref, m_ref, v_ref, d_ref, mo_ref, vo_ref):
        d_ref[...], mo_ref[...], vo_ref[...] = _adamw_math(w_ref[...], g_ref[...], m_ref[...], v_ref[...])

    spec = _row_spec(tile, cols)
    shape = jax.ShapeDtypeStruct((rows, cols), F32)
    return pl.pallas_call(
        body,
        name=name,
        grid=(rows // tile,),
        in_specs=[spec] * 4,
        out_specs=[spec] * 3,
        out_shape=[shape] * 3,
        compiler_params=_params(("parallel",)),
    )(w, g, m, v)


def _small_update(gathered, gathered_ws, w, m, v, ws, m_ws, v_ws):
    def body(ga_ref, gws_ref, w_ref, m_ref, v_ref, ws_ref, mws_ref, vws_ref, *outs):
        for src, refs, out in ((ga_ref, (w_ref, m_ref, v_ref), outs[0:4]), (gws_ref, (ws_ref, mws_ref, vws_ref), outs[4:8])):
            g = src[0].astype(F32)
            for j in range(1, N_DEV):
                g = g + src[j].astype(F32)
            out[0][...] = g
            out[1][...], out[2][...], out[3][...] = _adamw_math(refs[0][...], g, refs[1][...], refs[2][...])

    shapes = [jax.ShapeDtypeStruct(w.shape, F32)] * 4 + [jax.ShapeDtypeStruct(ws.shape, F32)] * 4
    return pl.pallas_call(
        body,
        name="small_update",
        in_specs=[VMEM_SPEC] * 8,
        out_specs=[VMEM_SPEC] * 8,
        out_shape=shapes,
        compiler_params=_params(),
    )(gathered, gathered_ws, w, m, v, ws, m_ws, v_ws)


def _ada_weight_grad(c_all, d_ada_cols):
    d = c_all.shape[-1]
    n_layers, _, width = d_ada_cols.shape

    def body(c_ref, da_ref, dw_ref):
        cv = c_ref[...]
        cond = cv * _sigmoid(cv)
        for l in range(n_layers):
            dw_ref[l] = lax.dot_general(
                cond, da_ref[l], (((0,), (0,)), ((), ())), preferred_element_type=F32, precision=lax.Precision.HIGHEST
            )

    return pl.pallas_call(
        body,
        name="ada_weight_grad",
        in_specs=[VMEM_SPEC, VMEM_SPEC],
        out_specs=VMEM_SPEC,
        out_shape=jax.ShapeDtypeStruct((n_layers, d, width), F32),
        compiler_params=_params(),
    )(c_all, d_ada_cols)


def _position():
    return lax.axis_index("x"), lax.axis_index("y"), lax.axis_index("c")


def _flip(pos, k):
    x, y, c = pos
    return (1 - x if k & 4 else x, 1 - y if k & 2 else y, 1 - c if k & 1 else c)


def _index(pos):
    x, y, c = pos
    return 4 * x + 2 * y + c


def _remote(src, dst, send_sem, recv_sem, to):
    return pltpu.make_async_remote_copy(
        src_ref=src, dst_ref=dst, send_sem=send_sem, recv_sem=recv_sem, device_id=to, device_id_type=MESH_ID
    )


def _all_gather_stages(slots, send_sems, recv_sems, sources=None, local_sems=None):
    me = _position()
    sibling = _flip(me, 1)
    others = (4, 2, 6)
    arrays = range(len(slots))

    def copy(t, k, block, to, own=False):
        slot = slots[t](_index(block))
        src = sources[t] if own and sources is not None else slot
        return _remote(src, slot, send_sems.at[7 * t + k], recv_sems.at[7 * t + k], to)

    def first(t):
        return [copy(t, 0, me, sibling, own=True)] + [copy(t, 1 + j, me, _flip(me, f), own=True) for j, f in enumerate(others)]

    def passed(t, j):
        return copy(t, 4 + j, _flip(me, others[j]), sibling)

    def local(t):
        return pltpu.make_async_copy(sources[t], slots[t](_index(me)), local_sems.at[t])

    def start():
        for t in arrays:
            if sources is not None:
                local(t).start()
            for cp in first(t):
                cp.start()

    def forward():
        for j, f in enumerate(others):
            for t in arrays:
                copy(t, 1 + j, _flip(me, f), me).wait_recv()
                passed(t, j).start()

    def finish():
        for t in arrays:
            copy(t, 0, sibling, me).wait_recv()
            for j, f in enumerate(others):
                copy(t, 4 + j, _flip(sibling, f), me).wait_recv()
        for t in arrays:
            for cp in first(t) + [passed(t, j) for j in range(len(others))]:
                cp.wait_send()
            if sources is not None:
                local(t).wait()

    return start, forward, finish


def _two_level_all_gather(slots, send_sems, recv_sems, between=None):
    start, forward, finish = _all_gather_stages(slots, send_sems, recv_sems)
    start()
    if between is not None:
        between()
    forward()
    finish()


def _row_block(ref, rows):
    return lambda j: ref.at[pl.ds(pl.multiple_of(j * rows, 16), rows), :]


def _scatter_stages(blocks, landing, send_sems, recv_sems, local_sems):
    me = _position()
    my = _index(me)
    arrays = range(len(blocks))

    def copy(t, k):
        px, py, pc = to = _flip(me, k)
        return _remote(blocks[t].at[2 * px + py, pc], landing[t].at[my], send_sems.at[7 * t + k - 1], recv_sems.at[7 * t + k - 1], to)

    def arrival(t, k):
        slot = landing[t].at[_index(_flip(me, k))]
        return _remote(slot, slot, send_sems.at[7 * t + k - 1], recv_sems.at[7 * t + k - 1], _flip(me, k))

    def local(t):
        x, y, c = me
        return pltpu.make_async_copy(blocks[t].at[2 * x + y, c], landing[t].at[my], local_sems.at[t])

    def start():
        for t in arrays:
            local(t).start()
            for k in range(1, N_DEV):
                copy(t, k).start()

    def finish():
        for t in arrays:
            for k in range(1, N_DEV):
                arrival(t, k).wait_recv()
        for t in arrays:
            for k in range(1, N_DEV):
                copy(t, k).wait_send()
            local(t).wait()

    return start, finish


def _ada_exchange(c_ref, w_ref, call_ref, parts_ref, sbuf, sem_s1, sem_r1, sem_s2, sem_r2):
    d = c_ref.shape[-1]
    n_layers = w_ref.shape[0]
    me = _position()
    my = _index(me)
    call_ref[my] = jnp.broadcast_to(c_ref[...], (SUBLANES, d))
    mine = call_ref.at[my]
    first = [_remote(mine, mine, sem_s1.at[k - 1], sem_r1.at[k - 1], _flip(me, k)) for k in range(1, N_DEV)]
    for cp in first:
        cp.start()
    for k in range(1, N_DEV):
        theirs = call_ref.at[_index(_flip(me, k))]
        _remote(theirs, theirs, sem_s1.at[k - 1], sem_r1.at[k - 1], _flip(me, k)).wait_recv()
    cv = call_ref[...].reshape(N_DEV * SUBLANES, d)
    cond = cv * _sigmoid(cv)
    for l in range(n_layers):
        rows = jnp.dot(cond, w_ref[l], preferred_element_type=F32, precision=lax.Precision.HIGHEST)
        for b in range(N_DEV):
            sbuf[b, l] = rows[b * SUBLANES : (b + 1) * SUBLANES]
    parts_ref[my] = sbuf[my]
    second = []
    for k in range(1, N_DEV):
        to = _flip(me, k)
        second.append(_remote(sbuf.at[_index(to)], parts_ref.at[my], sem_s2.at[k - 1], sem_r2.at[k - 1], to))
    for cp in second:
        cp.start()
    for k in range(1, N_DEV):
        theirs = parts_ref.at[_index(_flip(me, k))]
        _remote(theirs, theirs, sem_s2.at[k - 1], sem_r2.at[k - 1], _flip(me, k)).wait_recv()
    for cp in first + second:
        cp.wait_send()


def _gather_weights(w_in_t, w_out, c_row, w_ada):
    n_layers, rows_in, d = w_in_t.shape
    rows_out = w_out.shape[1]
    width = w_ada.shape[2]

    def body(wi_ref, wo_ref, c_ref, wa_ref, gi_ref, go_ref, si_ref, so_ref, call_ref, parts_ref, sbuf, send_sems, recv_sems, *ada_sems):
        my = _index(_position())
        si_ref[...] = wi_ref[...].astype(si_ref.dtype)
        so_ref[...] = wo_ref[...].astype(so_ref.dtype)
        gi_ref[pl.ds(pl.multiple_of(my * rows_in, 16), rows_in), :] = si_ref[0]
        go_ref[pl.ds(pl.multiple_of(my * rows_out, 16), rows_out), :] = so_ref[0]
        _two_level_all_gather(
            (_row_block(gi_ref, rows_in), _row_block(go_ref, rows_out)),
            send_sems,
            recv_sems,
            between=functools.partial(_ada_exchange, c_ref, wa_ref, call_ref, parts_ref, sbuf, *ada_sems),
        )

    return pl.pallas_call(
        body,
        name="gather_weights",
        in_specs=[VMEM_SPEC] * 4,
        out_specs=[VMEM_SPEC] * 6,
        out_shape=[
            jax.ShapeDtypeStruct((N_DEV * rows_in, d), MXU_DTYPE),
            jax.ShapeDtypeStruct((N_DEV * rows_out, d), MXU_DTYPE),
            jax.ShapeDtypeStruct(w_in_t.shape, MXU_DTYPE),
            jax.ShapeDtypeStruct(w_out.shape, MXU_DTYPE),
            jax.ShapeDtypeStruct((N_DEV, SUBLANES, d), F32),
            jax.ShapeDtypeStruct((N_DEV, n_layers, SUBLANES, width), F32),
        ],
        scratch_shapes=[
            pltpu.VMEM((N_DEV, n_layers, SUBLANES, width), F32),
            pltpu.SemaphoreType.DMA((14,)),
            pltpu.SemaphoreType.DMA((14,)),
        ]
        + [pltpu.SemaphoreType.DMA((N_DEV - 1,))] * 4,
        compiler_params=_params(),
    )(w_in_t, w_out, c_row, w_ada)


def _gather_small(packed, d_ws):
    def body(p_ref, ws_ref, g_ref, gws_ref, send_sems, recv_sems):
        my = _index(_position())
        g_ref[my] = p_ref[...]
        gws_ref[my] = ws_ref[...].astype(gws_ref.dtype)
        _two_level_all_gather((lambda j: g_ref.at[j], lambda j: gws_ref.at[j]), send_sems, recv_sems)

    return pl.pallas_call(
        body,
        name="gather_small",
        in_specs=[VMEM_SPEC, VMEM_SPEC],
        out_specs=[VMEM_SPEC, VMEM_SPEC],
        out_shape=[
            jax.ShapeDtypeStruct((N_DEV,) + packed.shape, F32),
            jax.ShapeDtypeStruct((N_DEV,) + d_ws.shape, jnp.bfloat16),
        ],
        scratch_shapes=[pltpu.SemaphoreType.DMA((14,)), pltpu.SemaphoreType.DMA((14,))],
        compiler_params=_params(),
    )(packed, d_ws)


def _scatter_finish(landed, name):
    n = len(landed)

    def body(*refs):
        for src, out in zip(refs[:n], refs[n:]):
            g = src[0].astype(F32)
            for j in range(1, N_DEV):
                g = g + src[j].astype(F32)
            out[...] = g

    return pl.pallas_call(
        body,
        name=name,
        in_specs=[VMEM_SPEC] * n,
        out_specs=[VMEM_SPEC] * n,
        out_shape=[jax.ShapeDtypeStruct(a.shape[1:], F32) for a in landed],
        compiler_params=_params(),
    )(*landed)


def _pack_rows(parts):
    rows, offsets, at = [], [], 0
    for p in parts:
        flat = p.reshape(-1)
        n = -(-flat.shape[0] // (SUBLANES * LANES)) * SUBLANES
        rows.append(jnp.pad(flat, (0, n * LANES - flat.shape[0])).reshape(n, LANES))
        offsets.append(at)
        at += n
    return jnp.concatenate(rows, axis=0), offsets


def _unpack_rows(packed, offsets, shapes):
    out = []
    for off, shape in zip(offsets, shapes):
        size = 1
        for s in shape:
            size *= s
        n = -(-size // (SUBLANES * LANES)) * SUBLANES
        out.append(packed[off : off + n].reshape(-1)[:size].reshape(shape))
    return out


def kernel(x, c, w_ada, b_ada, norm_gain, w_in, q_gain, k_gain, sink, w_s, b_s, w_out, loss_target, m_w_ada, m_b_ada, m_norm_gain, m_w_in, m_q_gain, m_k_gain, m_sink, m_w_s, m_b_s, m_w_out, v_w_ada, v_b_ada, v_norm_gain, v_w_in, v_q_gain, v_k_gain, v_sink, v_w_s, v_b_s, v_w_out):
    seq, d = x.shape[1], x.shape[2]
    n_layers = w_in.shape[0]
    w_cols = w_in.shape[2]
    ada_cols = w_ada.shape[2]
    my = _index(_position())
    xs = x.reshape(seq, d)
    target = loss_target.reshape(seq, d)

    w_in_t0, w_out0, shard_in, shard_out, c_all, ada_parts = _gather_weights(w_in.transpose(0, 2, 1), w_out, c, w_ada)
    w_in_ts, w_outs = [w_in_t0], [w_out0]
    ada = ada_parts[:, :, 0, :].transpose(1, 0, 2).reshape(n_layers, 3 * d) + b_ada
    shift, scale1, gate = ada[:, None, 0:d], 1.0 + ada[:, None, d : 2 * d], ada[:, None, 2 * d : 3 * d]
    gain = norm_gain[:, None, :]

    w_s_m = w_s.astype(MXU_DTYPE)
    w_s_t = w_s_m.transpose(0, 1, 3, 2)
    b_st = jnp.repeat(b_s.transpose(0, 2, 1), HEAD_DIM, axis=2)
    q_gain2 = jnp.tile(q_gain, (1, 2))[:, None, :]
    k_gain2 = jnp.tile(k_gain, (1, 2))[:, None, :]

    xl, saved = xs, []
    for l in range(n_layers):
        last = l == n_layers - 1
        pa, pb = _ln_proj_fwd(xl, gain[l], scale1[l], shift[l], w_in_ts[l], f"ln_proj_fwd_{l}")
        if last:
            o = _attn_fwd(pa, q_gain2[l], k_gain2[l], sink[l], f"attn_fwd_{l}")
        else:
            o, w_in_next, w_out_next = _attn_fwd(
                pa, q_gain2[l], k_gain2[l], sink[l], f"attn_fwd_{l}", gather=(shard_in[l + 1], shard_out[l + 1])
            )
            w_in_ts.append(w_in_next)
            w_outs.append(w_out_next)
        saved.append((xl, pa, pb, o))
        out = _mix_out_fwd(pb, o, xl, gate[l], w_outs[l], w_s_m[l], b_st[l], f"mix_out_fwd_{l}", target if last else None)
        if last:
            dx, sq_err = out
        else:
            xl = out

    g_w_in, g_w_out, small, d_ada_rows = [None] * n_layers, [None] * n_layers, [None] * n_layers, [None] * n_layers
    waiting = []
    for l in reversed(range(n_layers)):
        x_l, pa, pb, o = saved[l]
        dpb, do, g_acc, d_ws, d_bs = _mix_out_bwd(dx, pb, o, gate[l], w_outs[l], w_s_m[l], w_s_t[l], b_st[l], f"mix_out_bwd_{l}")
        dw_out, d_gate8 = _w_out_finish(g_acc, w_outs[l], gate[l], f"w_out_finish_{l}")
        riding = waiting + [(g_w_out, l, dw_out.reshape(4, 2, D_MIX // N_DEV, d))]
        attn = _attn_bwd(pa, o, do, q_gain2[l], k_gain2[l], sink[l], f"attn_bwd_{l}", scatter=tuple(b for _, _, b in riding))
        dq, dkv, halo_prev, halo_next, d_qg, d_kg, d_sk = attn[:7]
        for (dest, layer, _), total in zip(riding, _scatter_finish(attn[7:], f"scatter_finish_{l}")):
            dest[layer] = total.transpose(1, 0) if dest is g_w_in else total
        dw_in_t, dkvb = _proj_bwd_dw(x_l, gain[l], scale1[l], shift[l], dq, dkv, halo_prev, halo_next, dpb, f"proj_bwd_dw_{l}")
        blocks_in = dw_in_t.reshape(4, 2, w_cols, d)
        waiting = [(g_w_in, l, blocks_in)] if l > 0 else []
        dxs = _proj_bwd_dx(
            x_l, dx, dq, dkvb, dpb, w_in_ts[l], gain[l], scale1[l], f"proj_bwd_dx_{l}", scatter=() if l > 0 else (blocks_in,)
        )
        dx, c0, c1 = dxs[:3]
        if l == 0:
            g_w_in[l] = _scatter_finish(dxs[3:], "scatter_finish_in_0")[0].transpose(1, 0)
        c0s, c1s = c0.sum(axis=0), c1.sum(axis=0)
        d_ada_rows[l] = jnp.concatenate([c0s, norm_gain[l] * c1s, d_gate8.sum(axis=0)])
        small[l] = (
            scale1[l, 0] * c1s,
            d_qg.sum(axis=0).reshape(N_HEADS, HEAD_DIM).sum(axis=0),
            d_kg.sum(axis=0).reshape(2, HEAD_DIM).sum(axis=0),
            d_sk[0, 0:N_HEADS],
            d_bs.reshape(BLK, N_GROUPS, HEAD_DIM).sum(axis=2).transpose(1, 0),
            d_ws,
        )

    names = ("norm_gain", "q_gain", "k_gain", "sink", "b_s")
    stacked = [jnp.stack([small[l][t] for l in range(n_layers)]) for t in range(len(names))]
    d_ada = jnp.stack(d_ada_rows)
    packed, offsets = _pack_rows(stacked + [d_ada, sq_err[0, 0:1]])
    d_ws = jnp.stack([small[l][len(names)] for l in range(n_layers)]).reshape(-1, LANES)
    gathered, gathered_ws = _gather_small(packed, d_ws)
    no_weight = jnp.zeros((1,), F32)
    weights = (norm_gain, q_gain, k_gain, sink, b_s, b_ada, no_weight)
    moments_m = (m_norm_gain, m_q_gain, m_k_gain, m_sink, m_b_s, m_b_ada, no_weight)
    moments_v = (v_norm_gain, v_q_gain, v_k_gain, v_sink, v_b_s, v_b_ada, no_weight)
    w_pack, _ = _pack_rows(weights)
    m_pack, _ = _pack_rows(moments_m)
    v_pack, _ = _pack_rows(moments_v)
    shapes = [w.shape for w in weights]
    flat_ws = lambda a: a.reshape(-1, LANES)
    updated = _small_update(gathered, gathered_ws, w_pack, m_pack, v_pack, flat_ws(w_s), flat_ws(m_w_s), flat_ws(v_w_s))
    g_small, d_small, m_small, v_small = (_unpack_rows(p, offsets, shapes) for p in updated[0:4])
    ws_small = [p.reshape(w_s.shape) for p in updated[4:8]]
    loss = g_small[-1][0] * (0.5 / d)

    ada_off = offsets[-2]
    ada_n = -(-n_layers * 3 * d // (SUBLANES * LANES)) * SUBLANES
    d_ada_all = gathered[:, ada_off : ada_off + ada_n].reshape(N_DEV, -1)[:, : n_layers * 3 * d].reshape(N_DEV, n_layers, 3 * d)
    d_ada_cols = lax.dynamic_slice_in_dim(d_ada_all, my * ada_cols, ada_cols, axis=2)
    g_w_ada = _ada_weight_grad(c_all[:, 0, :], d_ada_cols.transpose(1, 0, 2))

    def update(w, g, m, v, name):
        shape = w.shape
        flat = lambda a: a.reshape(-1, shape[-1])
        return tuple(a.reshape(shape) for a in _adamw(flat(w), flat(g), flat(m), flat(v), name))

    g_w_in, g_w_out = jnp.stack(g_w_in), jnp.stack(g_w_out)
    upd_ada = update(w_ada, g_w_ada, m_w_ada, v_w_ada, "adamw_w_ada")
    upd_in = update(w_in, g_w_in, m_w_in, v_w_in, "adamw_w_in")
    upd_out = update(w_out, g_w_out, m_w_out, v_w_out, "adamw_w_out")

    def ordered(ada_, in_, out_, small_, ws):
        ng, qg, kg, sk, bs, ba, _ = small_
        return (ada_, ba, ng, in_, qg, kg, sk, ws, bs, out_)

    grads = ordered(g_w_ada, g_w_in, g_w_out, g_small, ws_small[0])
    deltas = ordered(upd_ada[0], upd_in[0], upd_out[0], d_small, ws_small[1])
    new_m = ordered(upd_ada[1], upd_in[1], upd_out[1], m_small, ws_small[2])
    new_v = ordered(upd_ada[2], upd_in[2], upd_out[2], v_small, ws_small[3])
    return (loss, dx.reshape(x.shape), *grads, *deltas, *new_m, *new_v)
```

```python
import functools

import jax
import jax.numpy as jnp
from jax import lax
from jax.experimental import pallas as pl
from jax.experimental.pallas import tpu as pltpu

F32 = jnp.float32
MXU_DTYPE = jnp.bfloat16
MESH_ID = pl.DeviceIdType.MESH

N_DEV = 8
HEAD_DIM = 64
N_HEADS = 8
Q_PER_KV = 4
D_ATTN = 512
D_KV = 128
D_GM = 512
N_GROUPS = 8
D_MIX = D_ATTN + D_GM
BLK = 128
LANES = 128
SUBLANES = 8
N_PAIRS = D_ATTN // LANES
D_QKV = D_ATTN + 2 * D_KV
D_REST = D_ATTN + 3 * D_GM
D_IN = D_QKV + D_REST
EPS = 1e-6
NEG_INF = -1e30
ALIBI_SLOPES = tuple(2.0 ** (-8.0 * (h + 1) / N_HEADS) for h in range(N_HEADS))
Q_SCALE = 1.0 / 8.0

ADAM_LR = 0.001
ADAM_B1 = 0.9
ADAM_B2 = 0.999
ADAM_EPS = 1e-08
ADAM_WD = 0.01
ADAM_STEP = 10

TOKEN_TILE = 512
VMEM_LIMIT_BYTES = 56 * 1024 * 1024


def _params(semantics=None):
    return pltpu.CompilerParams(dimension_semantics=semantics, vmem_limit_bytes=VMEM_LIMIT_BYTES)


def _dot(a, b):
    return jnp.dot(a, b, preferred_element_type=F32)


def _dot_nt(a, b):
    return lax.dot_general(a, b, (((1,), (1,)), ((), ())), preferred_element_type=F32)


def _dot_tn(a, b):
    return lax.dot_general(a, b, (((0,), (0,)), ((), ())), preferred_element_type=F32)


def _mx(v):
    return v.astype(MXU_DTYPE)


def _lane_lo(rows):
    return lax.broadcasted_iota(jnp.int32, (rows, LANES), 1) < HEAD_DIM


def _half_ones(width=LANES):
    r = jnp.right_shift(lax.broadcasted_iota(jnp.int32, (width, width), 0), 6)
    c = jnp.right_shift(lax.broadcasted_iota(jnp.int32, (width, width), 1), 6)
    return jnp.where(r == c, 1.0, 0.0).astype(jnp.bfloat16)


WIDE = 2 * LANES


def _half_sum(v, ones):
    p1 = v.astype(jnp.bfloat16)
    p2 = (v - p1.astype(F32)).astype(jnp.bfloat16)
    return _dot(p1, ones) + _dot(p2, ones)


def _half_rms(v, ones):
    r = lax.rsqrt(_half_sum(v * v, ones) * (1.0 / HEAD_DIM) + EPS)
    return v * r, r


def _half_rms_bwd(dy, vhat, r, ones):
    return r * (dy - vhat * (_half_sum(vhat * dy, ones) * (1.0 / HEAD_DIM)))


def _group_rows(v):
    rows, n = v.shape
    return v.reshape(rows // SUBLANES, SUBLANES, n).sum(axis=0)


def _sigmoid(v):
    return 1.0 / (1.0 + jnp.exp(-v))


ROW_CHUNK = 32
VARIANT_HEADS = ((0, 2, 5, 7), (1, 3, 4, 6))
HEAD_SLOT = {h: (v, s) for v, heads in enumerate(VARIANT_HEADS) for s, h in enumerate(heads)}
STACK = Q_PER_KV * BLK


def _fill_attn_bias(bias_s):
    qi = lax.broadcasted_iota(jnp.int32, (BLK, 3 * BLK), 0)
    ci = lax.broadcasted_iota(jnp.int32, (BLK, 3 * BLK), 1)
    dist = jnp.abs(ci - BLK - qi)
    distf = dist.astype(F32)
    window = dist <= BLK
    for kind, seen in enumerate((window & (ci >= BLK), window, window & (ci < 2 * BLK))):
        for h in range(N_HEADS):
            bias_s[kind, h] = jnp.where(seen, -(ALIBI_SLOPES[h] * distf), NEG_INF)


def _block_kind(block, seq):
    assert seq >= 2 * BLK
    return jnp.where(block == 0, 0, jnp.where(block == seq // BLK - 1, 2, 1))


def _stage_queries(qn, lo_t, j, nb, qs):
    for a in range(2):
        v, slot = HEAD_SLOT[2 * j + a]
        qm = _mx(jnp.where(lo_t, qn, 0.0) if a == 0 else jnp.where(lo_t, 0.0, qn))
        for n in range(nb):
            qs[n, v, slot * BLK : (slot + 1) * BLK, :] = qm[n * BLK : (n + 1) * BLK]


def _unstack_pair(stacked, j, lo):
    (v0, s0), (v1, s1) = HEAD_SLOT[2 * j], HEAD_SLOT[2 * j + 1]
    return jnp.where(lo, stacked[v0][s0 * BLK : (s0 + 1) * BLK], stacked[v1][s1 * BLK : (s1 + 1) * BLK])


def _stage_keys(kvp_ref, qkv_ref, kvn_ref, kg, ones, tile, ks, kr, vs, vr, khat_s=None, rk_s=None):
    pieces = (
        (0, BLK, kvp_ref[:, 0:D_KV], kvp_ref[:, D_KV : 2 * D_KV]),
        (BLK, tile, qkv_ref[:, D_ATTN : D_ATTN + D_KV], qkv_ref[:, D_ATTN + D_KV : D_QKV]),
        (BLK + tile, BLK, kvn_ref[:, 0:D_KV], kvn_ref[:, D_KV : 2 * D_KV]),
    )
    for r0, n, k, v in pieces:
        khat, rk = _half_rms(k, ones)
        kn = khat * kg
        ks[r0 : r0 + n, :] = _mx(kn)
        kr[r0 : r0 + n, :] = _mx(pltpu.roll(kn, HEAD_DIM, 1))
        vs[r0 : r0 + n, :] = _mx(v)
        vr[r0 : r0 + n, :] = _mx(pltpu.roll(v, HEAD_DIM, 1))
        if khat_s is not None:
            khat_s[r0 : r0 + n, :] = khat
            rk_s[r0 : r0 + n, :] = rk


def _halo_specs(tile, seq):
    nb = tile // BLK
    last = seq // BLK - 1
    kv_col = D_ATTN // (2 * D_KV)
    prev = pl.BlockSpec((BLK, 2 * D_KV), lambda i: (jnp.maximum(i * nb - 1, 0), kv_col))
    nxt = pl.BlockSpec((BLK, 2 * D_KV), lambda i: (jnp.minimum((i + 1) * nb, last), kv_col))
    return prev, nxt


def _row_spec(tile, width):
    return pl.BlockSpec((tile, width), lambda i: (i, 0))


def _full_spec(shape):
    nd = len(shape)
    return pl.BlockSpec(shape, lambda i: (0,) * nd)


SMEM_SPEC = pl.BlockSpec(memory_space=pltpu.SMEM)
VMEM_SPEC = pl.BlockSpec(memory_space=pltpu.VMEM)
HBM_SPEC = pl.BlockSpec(memory_space=pltpu.HBM)


def _ln_proj_fwd(x, gain, scale1, shift, w_in_t, name):
    seq, d = x.shape
    tile = min(TOKEN_TILE, seq)

    def body(x_ref, g_ref, s1_ref, sh_ref, wt_ref, pa_ref, pb_ref):
        xv = x_ref[...]
        r = lax.rsqrt(jnp.mean(xv * xv, axis=-1, keepdims=True) + EPS)
        h = _mx((xv * r) * g_ref[...] * s1_ref[...] + sh_ref[...])
        pa_ref[...] = _dot_nt(h, wt_ref[0:D_QKV, :])
        pb_ref[...] = _dot_nt(h, wt_ref[D_QKV:D_IN, :])

    vec = _full_spec((1, d))
    return pl.pallas_call(
        body,
        name=name,
        grid=(seq // tile,),
        in_specs=[_row_spec(tile, d), vec, vec, vec, _full_spec((D_IN, d))],
        out_specs=[_row_spec(tile, D_QKV), _row_spec(tile, D_REST)],
        out_shape=[jax.ShapeDtypeStruct((seq, D_QKV), F32), jax.ShapeDtypeStruct((seq, D_REST), F32)],
        compiler_params=_params(("parallel",)),
    )(x, gain, scale1, shift, w_in_t)


def _rider_steps(nt):
    return 0, (2 * nt) // 3, nt - 1


def _attn_fwd(pa, q_gain2, k_gain2, sink, name, gather=None):
    seq = pa.shape[0]
    tile = min(TOKEN_TILE, seq)
    nb = tile // BLK
    nt = seq // tile
    ext = tile + 2 * BLK
    riding = gather is not None

    def body(sink_ref, qkv_ref, kvp_ref, kvn_ref, qg_ref, kg_ref, *rest):
        i = pl.program_id(0)
        if riding:
            shard_in, shard_out, o_ref, full_in, full_out = rest[0:5]
            qs, ks, kr, vs, vr, bias_s, s_scr, p_scr, inv_scr, send_sems, recv_sems, local_sems = rest[5:]
            start, forward, finish = _all_gather_stages(
                (_row_block(full_in, shard_in.shape[0]), _row_block(full_out, shard_out.shape[0])),
                send_sems,
                recv_sems,
                sources=(shard_in, shard_out),
                local_sems=local_sems,
            )
            at_start, at_forward, at_finish = _rider_steps(nt)
            pl.when(i == at_start)(start)
        else:
            o_ref, qs, ks, kr, vs, vr, bias_s, s_scr, p_scr, inv_scr = rest

        @pl.when(i == 0)
        def _():
            _fill_attn_bias(bias_s)

        ones = _half_ones()
        lo = _lane_lo(BLK)
        lo_t = _lane_lo(tile)
        _stage_keys(kvp_ref, qkv_ref, kvn_ref, kg_ref[...], ones, tile, ks, kr, vs, vr)
        for j in range(N_PAIRS):
            qhat, _ = _half_rms(qkv_ref[:, j * LANES : (j + 1) * LANES], ones)
            _stage_queries(qhat * (qg_ref[...] * Q_SCALE), lo_t, j, nb, qs)

        def block(n, carry):
            r0 = pl.multiple_of(n * BLK, BLK)
            krows = pl.ds(r0, 3 * BLK)
            kind = _block_kind(i * nb + n, seq)
            for v in range(2):
                s_scr[v] = _dot_nt(qs[n, v], (kr if v else ks)[krows, :])
            for h in range(N_HEADS):
                v, slot = HEAD_SLOT[h]
                sink_h = sink_ref[h]
                for rc in range(0, BLK, ROW_CHUNK):
                    rows = slice(slot * BLK + rc, slot * BLK + rc + ROW_CHUNK)
                    s = s_scr[v, rows, :] + bias_s[kind, h, rc : rc + ROW_CHUNK, :]
                    m = jnp.maximum(jnp.max(s, axis=-1, keepdims=True), sink_h)
                    p = jnp.exp(s - m)
                    total = jnp.sum(p, axis=-1, keepdims=True) + jnp.exp(sink_h - m)
                    p_scr[v, rows, :] = _mx(p)
                    inv_scr[v, rows, :] = jnp.broadcast_to(1.0 / total, (ROW_CHUNK, LANES))
            outs = [_dot(p_scr[v], (vr if v else vs)[krows, :]) * inv_scr[v] for v in range(2)]
            for j in range(N_PAIRS):
                o_ref[pl.ds(r0, BLK), j * LANES : (j + 1) * LANES] = _unstack_pair(outs, j, lo)
            return carry

        lax.fori_loop(0, nb, block, 0)
        if riding:
            pl.when(i == at_forward)(forward)
            pl.when(i == at_finish)(finish)

    prev, nxt = _halo_specs(tile, seq)
    vec = _full_spec((1, LANES))
    in_specs = [SMEM_SPEC, _row_spec(tile, D_QKV), prev, nxt, vec, vec]
    out_specs = [_row_spec(tile, D_ATTN)]
    out_shape = [jax.ShapeDtypeStruct((seq, D_ATTN), F32)]
    scratch = [
        pltpu.VMEM((nb, 2, STACK, LANES), MXU_DTYPE),
        pltpu.VMEM((ext, LANES), MXU_DTYPE),
        pltpu.VMEM((ext, LANES), MXU_DTYPE),
        pltpu.VMEM((ext, LANES), MXU_DTYPE),
        pltpu.VMEM((ext, LANES), MXU_DTYPE),
        pltpu.VMEM((3, N_HEADS, BLK, 3 * BLK), F32),
        pltpu.VMEM((2, STACK, 3 * BLK), F32),
        pltpu.VMEM((2, STACK, 3 * BLK), MXU_DTYPE),
        pltpu.VMEM((2, STACK, LANES), F32),
    ]
    extra = ()
    if riding:
        extra = tuple(gather)
        in_specs += [HBM_SPEC] * 2
        out_specs += [HBM_SPEC] * 2
        out_shape += [jax.ShapeDtypeStruct((N_DEV * g.shape[0], g.shape[1]), g.dtype) for g in gather]
        scratch += [pltpu.SemaphoreType.DMA((14,)), pltpu.SemaphoreType.DMA((14,)), pltpu.SemaphoreType.DMA((2,))]
    out = pl.pallas_call(
        body,
        name=name,
        grid=(nt,),
        in_specs=in_specs,
        out_specs=out_specs,
        out_shape=out_shape,
        scratch_shapes=scratch,
        compiler_params=_params(("arbitrary",)),
    )(sink, pa, pa, pa, q_gain2, k_gain2, *extra)
    return out if riding else out[0]


def _mix_out_fwd(pb, o, x, gate, w_out, w_s, b_st, name, target=None):
    seq, d = x.shape
    tile = min(TOKEN_TILE, seq)
    nb = tile // BLK
    with_loss = target is not None

    def body(pb_ref, o_ref, x_ref, gate_ref, wo_ref, ws_ref, bs_ref, *rest):
        if with_loss:
            t_ref, xo_ref, acc_ref, y_s, vn_s = rest

            @pl.when(pl.program_id(0) == 0)
            def _():
                acc_ref[...] = jnp.zeros_like(acc_ref)
        else:
            xo_ref, y_s, vn_s = rest
        ones = _half_ones(WIDE)
        lo = _lane_lo(BLK)
        ga = pb_ref[:, 0:D_ATTN]
        y_s[:, 0:D_ATTN] = _mx(o_ref[...] * (ga * _sigmoid(ga)))
        for j in range(D_GM // WIDE):
            vhat, _ = _half_rms(pb_ref[:, 2 * D_GM + j * WIDE : 2 * D_GM + (j + 1) * WIDE], ones)
            vn_s[:, j * WIDE : (j + 1) * WIDE] = _mx(vhat)

        def chunk(n, carry):
            rows = pl.ds(pl.multiple_of(n * BLK, BLK), BLK)
            for j in range(N_PAIRS):
                cols = slice(j * LANES, (j + 1) * LANES)
                vn = vn_s[rows, cols]
                sv = jnp.where(lo, _dot(ws_ref[2 * j], vn), _dot(ws_ref[2 * j + 1], vn)) + bs_ref[:, cols]
                u = pb_ref[rows, D_ATTN + j * LANES : D_ATTN + (j + 1) * LANES]
                gg = pb_ref[rows, D_ATTN + 2 * D_GM + j * LANES : D_ATTN + 2 * D_GM + (j + 1) * LANES]
                y_s[rows, D_ATTN + j * LANES : D_ATTN + (j + 1) * LANES] = _mx((u * sv) * (gg * _sigmoid(gg)))
            return carry

        lax.fori_loop(0, nb, chunk, 0)
        y = x_ref[...] + gate_ref[...] * _dot(y_s[...], wo_ref[...])
        if with_loss:
            e = y - t_ref[...]
            xo_ref[...] = e * (1.0 / d)
            acc_ref[...] += jnp.sum(jnp.sum(e * e, axis=-1, keepdims=True), axis=0, keepdims=True)
        else:
            xo_ref[...] = y

    row = _row_spec(tile, d)
    acc_shape = (SUBLANES, LANES)
    return pl.pallas_call(
        body,
        name=name,
        grid=(seq // tile,),
        in_specs=[
            _row_spec(tile, D_REST),
            _row_spec(tile, D_ATTN),
            row,
            _full_spec((1, d)),
            _full_spec((D_MIX, d)),
            _full_spec((N_GROUPS, BLK, BLK)),
            _full_spec((BLK, D_GM)),
        ]
        + ([row] if with_loss else []),
        out_specs=[row, _full_spec(acc_shape)] if with_loss else row,
        out_shape=[jax.ShapeDtypeStruct((seq, d), F32), jax.ShapeDtypeStruct(acc_shape, F32)]
        if with_loss
        else jax.ShapeDtypeStruct((seq, d), F32),
        scratch_shapes=[pltpu.VMEM((tile, D_MIX), MXU_DTYPE), pltpu.VMEM((tile, D_GM), MXU_DTYPE)],
        compiler_params=_params(("arbitrary",) if with_loss else ("parallel",)),
    )(pb, o, x, gate, w_out, w_s, b_st, *([target] if with_loss else []))


def _mix_out_bwd(dxn, pb, o, gate, w_out, w_s, w_s_t, b_st, name):
    seq, d = dxn.shape
    tile = min(TOKEN_TILE, seq)
    nb = tile // BLK
    nt = seq // tile

    def body(dxn_ref, pb_ref, o_ref, gate_ref, wo_ref, ws_ref, wst_ref, bs_ref,
             dpb_ref, do_ref, dwo_ref, dgate_ref, dws_ref, dbs_ref, g_ref, y_s, dy_s, vn_s, rv_s, vnb_s, sv_s, dsv_s, dvn_s):
        @pl.when(pl.program_id(0) == 0)
        def _():
            g_ref[...] = jnp.zeros_like(g_ref)
            dws_ref[...] = jnp.zeros_like(dws_ref)
            dbs_ref[...] = jnp.zeros_like(dbs_ref)

        ones = _half_ones(WIDE)
        lo = _lane_lo(BLK)
        c_u = slice(D_ATTN, D_ATTN + D_GM)
        c_vg = slice(D_ATTN + D_GM, D_ATTN + 2 * D_GM)
        c_gg = slice(D_ATTN + 2 * D_GM, D_REST)
        dxv = dxn_ref[...]
        dy_s[...] = _dot_nt(_mx(dxv * gate_ref[...]), wo_ref[...])
        ga = pb_ref[:, 0:D_ATTN]
        sig = _sigmoid(ga)
        sil = ga * sig
        ov = o_ref[...]
        y_s[:, 0:D_ATTN] = _mx(ov * sil)
        da = dy_s[:, 0:D_ATTN]
        do_ref[...] = da * sil
        dpb_ref[:, 0:D_ATTN] = (da * ov * (sig * (1.0 + ga * (1.0 - sig)))).astype(dpb_ref.dtype)
        for j in range(D_GM // WIDE):
            cols = slice(j * WIDE, (j + 1) * WIDE)
            vhat, rv = _half_rms(pb_ref[:, 2 * D_GM + j * WIDE : 2 * D_GM + (j + 1) * WIDE], ones)
            vn_s[:, cols] = vhat
            rv_s[:, cols] = rv
            vnb_s[:, cols] = _mx(vhat)

        def spatial_fwd(n, carry):
            rows = pl.ds(pl.multiple_of(n * BLK, BLK), BLK)
            for j in range(N_PAIRS):
                cols = slice(j * LANES, (j + 1) * LANES)
                vn = vnb_s[rows, cols]
                sv_s[rows, cols] = jnp.where(lo, _dot(ws_ref[2 * j], vn), _dot(ws_ref[2 * j + 1], vn)) + bs_ref[:, cols]
            return carry

        lax.fori_loop(0, nb, spatial_fwd, 0)

        def gating(n, carry):
            rows = pl.ds(pl.multiple_of(n * BLK, BLK), BLK)
            sv = sv_s[rows, :]
            u = pb_ref[rows, c_u]
            gg = pb_ref[rows, c_gg]
            sg = _sigmoid(gg)
            silg = gg * sg
            m0 = u * sv
            y_s[rows, D_ATTN:D_MIX] = _mx(m0 * silg)
            dm = dy_s[rows, D_ATTN:D_MIX]
            dm0 = dm * silg
            dpb_ref[rows, c_gg] = (dm * m0 * (sg * (1.0 + gg * (1.0 - sg)))).astype(dpb_ref.dtype)
            dpb_ref[rows, c_u] = (dm0 * sv).astype(dpb_ref.dtype)
            dsv = dm0 * u
            dsv_s[rows, :] = _mx(dsv)
            dbs_ref[...] += dsv
            return carry

        lax.fori_loop(0, nb, gating, 0)

        def spatial_bwd(n, carry):
            rows = pl.ds(pl.multiple_of(n * BLK, BLK), BLK)
            for j in range(N_PAIRS):
                cols = slice(j * LANES, (j + 1) * LANES)
                dsv = dsv_s[rows, cols]
                dvn_s[rows, cols] = jnp.where(lo, _dot(wst_ref[2 * j], dsv), _dot(wst_ref[2 * j + 1], dsv))
            return carry

        lax.fori_loop(0, nb, spatial_bwd, 0)
        zero = jnp.zeros((BLK, LANES), MXU_DTYPE)
        for j in range(N_PAIRS):
            cols = slice(j * LANES, (j + 1) * LANES)
            chunks = [dsv_s[n * BLK : (n + 1) * BLK, cols] for n in range(nb)]
            vn_all = jnp.concatenate([vnb_s[n * BLK : (n + 1) * BLK, cols] for n in range(nb)], axis=1)
            dws_ref[2 * j] += _dot_nt(jnp.concatenate([jnp.where(lo, c, zero) for c in chunks], axis=1), vn_all)
            dws_ref[2 * j + 1] += _dot_nt(jnp.concatenate([jnp.where(lo, zero, c) for c in chunks], axis=1), vn_all)
        for j in range(D_GM // WIDE):
            cols = slice(j * WIDE, (j + 1) * WIDE)
            dpb_ref[:, D_ATTN + D_GM + j * WIDE : D_ATTN + D_GM + (j + 1) * WIDE] = _half_rms_bwd(
                dvn_s[:, cols], vn_s[:, cols], rv_s[:, cols], ones
            ).astype(dpb_ref.dtype)
        g_ref[...] += _dot_tn(y_s[...], _mx(dxv))

        @pl.when(pl.program_id(0) == nt - 1)
        def _():
            gv = g_ref[...]
            dwo_ref[...] = (gv * gate_ref[...]).astype(dwo_ref.dtype)
            dgate_ref[...] = _group_rows(gv * wo_ref[...].astype(F32))

    return pl.pallas_call(
        body,
        name=name,
        grid=(seq // tile,),
        in_specs=[
            _row_spec(tile, d),
            _row_spec(tile, D_REST),
            _row_spec(tile, D_ATTN),
            _full_spec((1, d)),
            _full_spec((D_MIX, d)),
            _full_spec((N_GROUPS, BLK, BLK)),
            _full_spec((N_GROUPS, BLK, BLK)),
            _full_spec((BLK, D_GM)),
        ],
        out_specs=[
            _row_spec(tile, D_REST),
            _row_spec(tile, D_ATTN),
            _full_spec((D_MIX, d)),
            _full_spec((SUBLANES, d)),
            _full_spec((N_GROUPS, BLK, BLK)),
            _full_spec((BLK, D_GM)),
        ],
        out_shape=[
            jax.ShapeDtypeStruct((seq, D_REST), MXU_DTYPE),
            jax.ShapeDtypeStruct((seq, D_ATTN), F32),
            jax.ShapeDtypeStruct((D_MIX, d), jnp.bfloat16),
            jax.ShapeDtypeStruct((SUBLANES, d), F32),
            jax.ShapeDtypeStruct((N_GROUPS, BLK, BLK), F32),
            jax.ShapeDtypeStruct((BLK, D_GM), F32),
        ],
        scratch_shapes=[
            pltpu.VMEM((D_MIX, d), F32),
            pltpu.VMEM((tile, D_MIX), MXU_DTYPE),
            pltpu.VMEM((tile, D_MIX), F32),
            pltpu.VMEM((tile, D_GM), F32),
            pltpu.VMEM((tile, D_GM), F32),
            pltpu.VMEM((tile, D_GM), MXU_DTYPE),
            pltpu.VMEM((tile, D_GM), F32),
            pltpu.VMEM((tile, D_GM), MXU_DTYPE),
            pltpu.VMEM((tile, D_GM), F32),
        ],
        compiler_params=_params(("arbitrary",)),
    )(dxn, pb, o, gate, w_out, w_s, w_s_t, b_st)


def _attn_bwd(pa, o, do, q_gain2, k_gain2, sink, name, scatter=()):
    seq = pa.shape[0]
    tile = min(TOKEN_TILE, seq)
    nb = tile // BLK
    nt = seq // tile
    ext = tile + 2 * BLK
    n_ride = len(scatter)
    riding = n_ride > 0

    def body(sink_ref, qkv_ref, kvp_ref, kvn_ref, o_ref, do_ref, qg_ref, kg_ref, *rest):
        i = pl.program_id(0)
        blocks, rest = rest[:n_ride], rest[n_ride:]
        dq_ref, dkv_ref, hp_ref, hn_ref, dqg_ref, dkg_ref, dsk_ref = rest[:7]
        landing, rest = rest[7 : 7 + n_ride], rest[7 + n_ride :]
        (qs, dos, qhat_s, rq_s, ks, kr, vs, vr, khat_s, rk_s, dqn_s, dka, dva, bias_s, s_scr, dp_scr, p_scr, ds_scr) = rest[:18]
        if riding:
            start, finish = _scatter_stages(blocks, landing, *rest[18:])
            at_start, _, at_finish = _rider_steps(nt)
            pl.when(i == at_start)(start)

        @pl.when(i == 0)
        def _():
            dqg_ref[...] = jnp.zeros_like(dqg_ref)
            dkg_ref[...] = jnp.zeros_like(dkg_ref)
            dsk_ref[...] = jnp.zeros_like(dsk_ref)
            _fill_attn_bias(bias_s)

        ones = _half_ones()
        lo = _lane_lo(BLK)
        lo_t = _lane_lo(tile)
        lo_c = _lane_lo(ROW_CHUNK)
        qg = qg_ref[...] * Q_SCALE
        kg = kg_ref[...]
        _stage_keys(kvp_ref, qkv_ref, kvn_ref, kg, ones, tile, ks, kr, vs, vr, khat_s, rk_s)
        for j in range(N_PAIRS):
            cols = slice(j * LANES, (j + 1) * LANES)
            qhat, rq = _half_rms(qkv_ref[:, cols], ones)
            qhat_s[:, cols] = qhat
            rq_s[:, cols] = rq
            _stage_queries(qhat * qg, lo_t, j, nb, qs)
            _stage_queries(do_ref[:, cols], lo_t, j, nb, dos)
        dka[...] = jnp.zeros_like(dka)
        dva[...] = jnp.zeros_like(dva)
        head_lane = lax.broadcasted_iota(jnp.int32, (1, LANES), 1)

        def block(n, dsink):
            r0 = pl.multiple_of(n * BLK, BLK)
            krows = pl.ds(r0, 3 * BLK)
            kind = _block_kind(i * nb + n, seq)
            for v in range(2):
                s_scr[v] = _dot_nt(qs[n, v], (kr if v else ks)[krows, :])
                dp_scr[v] = _dot_nt(dos[n, v], (vr if v else vs)[krows, :])
            for h in range(N_HEADS):
                v, slot = HEAD_SLOT[h]
                j, a = divmod(h, 2)
                cols = slice(j * LANES, (j + 1) * LANES)
                sink_h = sink_ref[h]
                sink_part = jnp.zeros((ROW_CHUNK, 1), F32)
                for rc in range(0, BLK, ROW_CHUNK):
                    rows = slice(slot * BLK + rc, slot * BLK + rc + ROW_CHUNK)
                    trows = pl.ds(pl.multiple_of(r0 + rc, ROW_CHUNK), ROW_CHUNK)
                    s = s_scr[v, rows, :] + bias_s[kind, h, rc : rc + ROW_CHUNK, :]
                    m = jnp.maximum(jnp.max(s, axis=-1, keepdims=True), sink_h)
                    p = jnp.exp(s - m)
                    e_sink = jnp.exp(sink_h - m)
                    inv = 1.0 / (jnp.sum(p, axis=-1, keepdims=True) + e_sink)
                    pn = p * inv
                    prod = do_ref[trows, cols] * o_ref[trows, cols]
                    prod = jnp.where(lo_c, prod, 0.0) if a == 0 else jnp.where(lo_c, 0.0, prod)
                    dcol = jnp.sum(prod, axis=-1, keepdims=True)
                    ds_scr[v, rows, :] = _mx(pn * (dp_scr[v, rows, :] - dcol))
                    p_scr[v, rows, :] = _mx(pn)
                    sink_part = sink_part + (e_sink * inv) * dcol
                dsink = dsink - jnp.where(head_lane == h, jnp.sum(sink_part, axis=0, keepdims=True), 0.0)
            dqv = []
            for v in range(2):
                dqv.append(_dot(ds_scr[v], (kr if v else ks)[krows, :]))
                dka[v, krows, :] += _dot_tn(ds_scr[v], qs[n, v])
                dva[v, krows, :] += _dot_tn(p_scr[v], dos[n, v])
            for j in range(N_PAIRS):
                dqn_s[pl.ds(r0, BLK), j * LANES : (j + 1) * LANES] = _unstack_pair(dqv, j, lo)
            return dsink

        dsink = lax.fori_loop(0, nb, block, jnp.zeros((1, LANES), F32))
        dsk_ref[...] += jnp.broadcast_to(dsink, (SUBLANES, LANES))
        for j in range(N_PAIRS):
            cols = slice(j * LANES, (j + 1) * LANES)
            dqn = dqn_s[:, cols]
            qhat = qhat_s[:, cols]
            dqg_ref[:, cols] += _group_rows(dqn * qhat) * Q_SCALE
            dq_ref[:, cols] = _half_rms_bwd(dqn * qg, qhat, rq_s[:, cols], ones).astype(dq_ref.dtype)
        dkn = dka[0] + pltpu.roll(dka[1], HEAD_DIM, 1)
        khat = khat_s[...]
        dkg_ref[...] += _group_rows(dkn * khat)
        dk = _half_rms_bwd(dkn * kg, khat, rk_s[...], ones)
        dv = dva[0] + pltpu.roll(dva[1], HEAD_DIM, 1)
        hp_ref[:, 0:D_KV] = dk[0:BLK]
        hp_ref[:, D_KV : 2 * D_KV] = dv[0:BLK]
        dkv_ref[:, 0:D_KV] = dk[BLK : BLK + tile]
        dkv_ref[:, D_KV : 2 * D_KV] = dv[BLK : BLK + tile]
        hn_ref[:, 0:D_KV] = dk[BLK + tile : ext]
        hn_ref[:, D_KV : 2 * D_KV] = dv[BLK + tile : ext]
        if riding:
            pl.when(i == at_finish)(finish)

    prev, nxt = _halo_specs(tile, seq)
    vec = _full_spec((1, LANES))
    halo = pl.BlockSpec((None, BLK, 2 * D_KV), lambda i: (i, 0, 0))
    return pl.pallas_call(
        body,
        name=name,
        grid=(nt,),
        in_specs=[SMEM_SPEC, _row_spec(tile, D_QKV), prev, nxt, _row_spec(tile, D_ATTN), _row_spec(tile, D_ATTN), vec, vec]
        + [HBM_SPEC] * n_ride,
        out_specs=[
            _row_spec(tile, D_ATTN),
            _row_spec(tile, 2 * D_KV),
            halo,
            halo,
            _full_spec((SUBLANES, D_ATTN)),
            _full_spec((SUBLANES, LANES)),
            _full_spec((SUBLANES, LANES)),
        ]
        + [HBM_SPEC] * n_ride,
        out_shape=[
            jax.ShapeDtypeStruct((seq, D_ATTN), MXU_DTYPE),
            jax.ShapeDtypeStruct((seq, 2 * D_KV), F32),
            jax.ShapeDtypeStruct((nt, BLK, 2 * D_KV), F32),
            jax.ShapeDtypeStruct((nt, BLK, 2 * D_KV), F32),
            jax.ShapeDtypeStruct((SUBLANES, D_ATTN), F32),
            jax.ShapeDtypeStruct((SUBLANES, LANES), F32),
            jax.ShapeDtypeStruct((SUBLANES, LANES), F32),
        ]
        + _landing_shapes(scatter),
        scratch_shapes=[
            pltpu.VMEM((nb, 2, STACK, LANES), MXU_DTYPE),
            pltpu.VMEM((nb, 2, STACK, LANES), MXU_DTYPE),
            pltpu.VMEM((tile, D_ATTN), F32),
            pltpu.VMEM((tile, D_ATTN), F32),
            pltpu.VMEM((ext, LANES), MXU_DTYPE),
            pltpu.VMEM((ext, LANES), MXU_DTYPE),
            pltpu.VMEM((ext, LANES), MXU_DTYPE),
            pltpu.VMEM((ext, LANES), MXU_DTYPE),
            pltpu.VMEM((ext, LANES), F32),
            pltpu.VMEM((ext, LANES), F32),
            pltpu.VMEM((tile, D_ATTN), F32),
            pltpu.VMEM((2, ext, LANES), F32),
            pltpu.VMEM((2, ext, LANES), F32),
            pltpu.VMEM((3, N_HEADS, BLK, 3 * BLK), F32),
            pltpu.VMEM((2, STACK, 3 * BLK), F32),
            pltpu.VMEM((2, STACK, 3 * BLK), F32),
            pltpu.VMEM((2, STACK, 3 * BLK), MXU_DTYPE),
            pltpu.VMEM((2, STACK, 3 * BLK), MXU_DTYPE),
        ]
        + _rider_sems(n_ride),
        compiler_params=_params(("arbitrary",)),
    )(sink, pa, pa, pa, o, do, q_gain2, k_gain2, *scatter)


def _halo_in_specs(tile, nt):
    from_prev = pl.BlockSpec((None, BLK, 2 * D_KV), lambda i: (jnp.maximum(i - 1, 0), 0, 0))
    from_next = pl.BlockSpec((None, BLK, 2 * D_KV), lambda i: (jnp.minimum(i + 1, nt - 1), 0, 0))
    return from_prev, from_next


def _landing_shapes(scatter):
    return [jax.ShapeDtypeStruct((N_DEV,) + b.shape[2:], b.dtype) for b in scatter]


def _rider_sems(n_ride):
    if not n_ride:
        return []
    return [pltpu.SemaphoreType.DMA((7 * n_ride,)), pltpu.SemaphoreType.DMA((7 * n_ride,)), pltpu.SemaphoreType.DMA((n_ride,))]


def _proj_bwd_dx(x, dxn, dq, dkvb, dpb, w_in_t, gain, scale1, name, scatter=()):
    seq, d = x.shape
    tile = min(TOKEN_TILE, seq)
    nt = seq // tile
    n_ride = len(scatter)

    def row(width):
        return _row_spec(tile, width)

    def body(x_ref, dxn_ref, dq_ref, dkvb_ref, dpb_ref, wt_ref, g_ref, s1_ref, *rest):
        i = pl.program_id(0)
        blocks, rest = rest[:n_ride], rest[n_ride:]
        dx_ref, c0_ref, c1_ref = rest[:3]
        landing, sems = rest[3 : 3 + n_ride], rest[3 + n_ride :]
        if n_ride:
            start, finish = _scatter_stages(blocks, landing, *sems)
            at_start, _, at_finish = _rider_steps(nt)
            pl.when(i == at_start)(start)

        @pl.when(i == 0)
        def _():
            c0_ref[...] = jnp.zeros_like(c0_ref)
            c1_ref[...] = jnp.zeros_like(c1_ref)

        dh = (
            _dot(dq_ref[...], wt_ref[0:D_ATTN, :])
            + _dot(dkvb_ref[...], wt_ref[D_ATTN:D_QKV, :])
            + _dot(dpb_ref[...], wt_ref[D_QKV:D_IN, :])
        )
        xv = x_ref[...]
        r = lax.rsqrt(jnp.mean(xv * xv, axis=-1, keepdims=True) + EPS)
        xn = xv * r
        c0_ref[...] += _group_rows(dh)
        c1_ref[...] += _group_rows(dh * xn)
        dxn_ = dh * (g_ref[...] * s1_ref[...])
        dx_ref[...] = dxn_ref[...] + r * (dxn_ - xn * jnp.mean(xn * dxn_, axis=-1, keepdims=True))
        if n_ride:
            pl.when(i == at_finish)(finish)

    vec = _full_spec((1, d))
    return pl.pallas_call(
        body,
        name=name,
        grid=(nt,),
        in_specs=[row(d), row(d), row(D_ATTN), row(2 * D_KV), row(D_REST), _full_spec((D_IN, d)), vec, vec]
        + [HBM_SPEC] * n_ride,
        out_specs=[row(d), _full_spec((SUBLANES, d)), _full_spec((SUBLANES, d))] + [HBM_SPEC] * n_ride,
        out_shape=[
            jax.ShapeDtypeStruct((seq, d), F32),
            jax.ShapeDtypeStruct((SUBLANES, d), F32),
            jax.ShapeDtypeStruct((SUBLANES, d), F32),
        ]
        + _landing_shapes(scatter),
        scratch_shapes=_rider_sems(n_ride),
        compiler_params=_params(("arbitrary",)),
    )(x, dxn, dq, dkvb, dpb, w_in_t, gain, scale1, *scatter)


def _proj_bwd_dw(x, gain, scale1, shift, dq, dkv, halo_prev, halo_next, dpb, name):
    seq, d = x.shape
    tile = min(TOKEN_TILE, seq)
    nt = seq // tile
    assert tile >= 2 * BLK

    def body(x_ref, g_ref, s1_ref, sh_ref, dq_ref, dkv_ref, hn_ref, hp_ref, dpb_ref, dw_ref, dkvb_ref, acc):
        i = pl.program_id(0)

        @pl.when(i == 0)
        def _():
            acc[...] = jnp.zeros_like(acc)

        top = dkv_ref[0:BLK, :] + jnp.where(i > 0, hn_ref[...], 0.0)
        bot = dkv_ref[tile - BLK : tile, :] + jnp.where(i < nt - 1, hp_ref[...], 0.0)
        dkvb_ref[0:BLK, :] = top.astype(dkvb_ref.dtype)
        dkvb_ref[tile - BLK : tile, :] = bot.astype(dkvb_ref.dtype)
        if tile > 2 * BLK:
            dkvb_ref[BLK : tile - BLK, :] = dkv_ref[BLK : tile - BLK, :].astype(dkvb_ref.dtype)
        xv = x_ref[...]
        r = lax.rsqrt(jnp.mean(xv * xv, axis=-1, keepdims=True) + EPS)
        h = _mx((xv * r) * g_ref[...] * s1_ref[...] + sh_ref[...])
        acc[0:D_ATTN, :] += _dot_tn(dq_ref[...], h)
        acc[D_ATTN:D_QKV, :] += _dot_tn(dkvb_ref[...], h)
        acc[D_QKV:D_IN, :] += _dot_tn(dpb_ref[...], h)

        @pl.when(i == nt - 1)
        def _():
            dw_ref[...] = acc[...].astype(dw_ref.dtype)

    from_prev, from_next = _halo_in_specs(tile, nt)
    vec = _full_spec((1, d))
    return pl.pallas_call(
        body,
        name=name,
        grid=(nt,),
        in_specs=[
            _row_spec(tile, d),
            vec,
            vec,
            vec,
            _row_spec(tile, D_ATTN),
            _row_spec(tile, 2 * D_KV),
            from_prev,
            from_next,
            _row_spec(tile, D_REST),
        ],
        out_specs=[_full_spec((D_IN, d)), _row_spec(tile, 2 * D_KV)],
        out_shape=[jax.ShapeDtypeStruct((D_IN, d), jnp.bfloat16), jax.ShapeDtypeStruct((seq, 2 * D_KV), MXU_DTYPE)],
        scratch_shapes=[pltpu.VMEM((D_IN, d), F32)],
        compiler_params=_params(("arbitrary",)),
    )(x, gain, scale1, shift, dq, dkv, halo_next, halo_prev, dpb)


def _adamw_math(w, g, m, v):
    m = ADAM_B1 * m + (1.0 - ADAM_B1) * g
    v = ADAM_B2 * v + (1.0 - ADAM_B2) * (g * g)
    m_hat = m / (1.0 - ADAM_B1**ADAM_STEP)
    v_hat = v / (1.0 - ADAM_B2**ADAM_STEP)
    delta = -ADAM_LR * (m_hat / (jnp.sqrt(v_hat) + ADAM_EPS) + ADAM_WD * w)
    return delta, m, v


def _adamw(w, g, m, v, name):
    rows, cols = w.shape
    tile = min(TOKEN_TILE, rows)

    def body(w_ref, g_ref, m_ref, v_ref, d_ref, mo_ref, vo_ref):
        d_ref[...], mo_ref[...], vo_ref[...] = _adamw_math(w_ref[...], g_ref[...], m_ref[...], v_ref[...])

    spec = _row_spec(tile, cols)
    shape = jax.ShapeDtypeStruct((rows, cols), F32)
    return pl.pallas_call(
        body,
        name=name,
        grid=(rows // tile,),
        in_specs=[spec] * 4,
        out_specs=[spec] * 3,
        out_shape=[shape] * 3,
        compiler_params=_params(("parallel",)),
    )(w, g, m, v)


def _small_update(gathered, gathered_ws, w, m, v, ws, m_ws, v_ws):
    def body(ga_ref, gws_ref, w_ref, m_ref, v_ref, ws_ref, mws_ref, vws_ref, *outs):
        for src, refs, out in ((ga_ref, (w_ref, m_ref, v_ref), outs[0:4]), (gws_ref, (ws_ref, mws_ref, vws_ref), outs[4:8])):
            g = src[0].astype(F32)
            for j in range(1, N_DEV):
                g = g + src[j].astype(F32)
            out[0][...] = g
            out[1][...], out[2][...], out[3][...] = _adamw_math(refs[0][...], g, refs[1][...], refs[2][...])

    shapes = [jax.ShapeDtypeStruct(w.shape, F32)] * 4 + [jax.ShapeDtypeStruct(ws.shape, F32)] * 4
    return pl.pallas_call(
        body,
        name="small_update",
        in_specs=[VMEM_SPEC] * 8,
        out_specs=[VMEM_SPEC] * 8,
        out_shape=shapes,
        compiler_params=_params(),
    )(gathered, gathered_ws, w, m, v, ws, m_ws, v_ws)


def _ada_update(c_all, d_ada_cols, w, m, v):
    n_layers = w.shape[0]

    def body(c_ref, da_ref, w_ref, m_ref, v_ref, g_ref, d_ref, mo_ref, vo_ref):
        cv = c_ref[...]
        cond = cv * _sigmoid(cv)
        for l in range(n_layers):
            g = lax.dot_general(
                cond, da_ref[l], (((0,), (0,)), ((), ())), preferred_element_type=F32, precision=lax.Precision.HIGHEST
            )
            g_ref[l] = g
            d_ref[l], mo_ref[l], vo_ref[l] = _adamw_math(w_ref[l], g, m_ref[l], v_ref[l])

    return pl.pallas_call(
        body,
        name="ada_update",
        in_specs=[VMEM_SPEC] * 5,
        out_specs=[VMEM_SPEC] * 4,
        out_shape=[jax.ShapeDtypeStruct(w.shape, F32)] * 4,
        compiler_params=_params(),
    )(c_all, d_ada_cols, w, m, v)


def _position():
    return lax.axis_index("x"), lax.axis_index("y"), lax.axis_index("c")


def _flip(pos, k):
    x, y, c = pos
    return (1 - x if k & 4 else x, 1 - y if k & 2 else y, 1 - c if k & 1 else c)


def _index(pos):
    x, y, c = pos
    return 4 * x + 2 * y + c


def _remote(src, dst, send_sem, recv_sem, to):
    return pltpu.make_async_remote_copy(
        src_ref=src, dst_ref=dst, send_sem=send_sem, recv_sem=recv_sem, device_id=to, device_id_type=MESH_ID
    )


def _all_gather_stages(slots, send_sems, recv_sems, sources=None, local_sems=None):
    me = _position()
    sibling = _flip(me, 1)
    others = (4, 2, 6)
    arrays = range(len(slots))

    def copy(t, k, block, to, own=False):
        slot = slots[t](_index(block))
        src = sources[t] if own and sources is not None else slot
        return _remote(src, slot, send_sems.at[7 * t + k], recv_sems.at[7 * t + k], to)

    def first(t):
        return [copy(t, 0, me, sibling, own=True)] + [copy(t, 1 + j, me, _flip(me, f), own=True) for j, f in enumerate(others)]

    def passed(t, j):
        return copy(t, 4 + j, _flip(me, others[j]), sibling)

    def local(t):
        return pltpu.make_async_copy(sources[t], slots[t](_index(me)), local_sems.at[t])

    def start():
        for t in arrays:
            if sources is not None:
                local(t).start()
            for cp in first(t):
                cp.start()

    def forward():
        for j, f in enumerate(others):
            for t in arrays:
                copy(t, 1 + j, _flip(me, f), me).wait_recv()
                passed(t, j).start()

    def finish():
        for t in arrays:
            copy(t, 0, sibling, me).wait_recv()
            for j, f in enumerate(others):
                copy(t, 4 + j, _flip(sibling, f), me).wait_recv()
        for t in arrays:
            for cp in first(t) + [passed(t, j) for j in range(len(others))]:
                cp.wait_send()
            if sources is not None:
                local(t).wait()

    return start, forward, finish


def _two_level_all_gather(slots, send_sems, recv_sems, between=None):
    start, forward, finish = _all_gather_stages(slots, send_sems, recv_sems)
    start()
    if between is not None:
        between()
    forward()
    finish()


def _row_block(ref, rows):
    return lambda j: ref.at[pl.ds(pl.multiple_of(j * rows, 16), rows), :]


def _scatter_stages(blocks, landing, send_sems, recv_sems, local_sems):
    me = _position()
    my = _index(me)
    arrays = range(len(blocks))

    def copy(t, k):
        px, py, pc = to = _flip(me, k)
        return _remote(blocks[t].at[2 * px + py, pc], landing[t].at[my], send_sems.at[7 * t + k - 1], recv_sems.at[7 * t + k - 1], to)

    def arrival(t, k):
        slot = landing[t].at[_index(_flip(me, k))]
        return _remote(slot, slot, send_sems.at[7 * t + k - 1], recv_sems.at[7 * t + k - 1], _flip(me, k))

    def local(t):
        x, y, c = me
        return pltpu.make_async_copy(blocks[t].at[2 * x + y, c], landing[t].at[my], local_sems.at[t])

    def start():
        for t in arrays:
            local(t).start()
            for k in range(1, N_DEV):
                copy(t, k).start()

    def finish():
        for t in arrays:
            for k in range(1, N_DEV):
                arrival(t, k).wait_recv()
        for t in arrays:
            for k in range(1, N_DEV):
                copy(t, k).wait_send()
            local(t).wait()

    return start, finish


def _ada_exchange(c_ref, w_ref, call_ref, parts_ref, sbuf, sem_s1, sem_r1, sem_s2, sem_r2):
    d = c_ref.shape[-1]
    n_layers = w_ref.shape[0]
    me = _position()
    my = _index(me)
    call_ref[my] = jnp.broadcast_to(c_ref[...], (SUBLANES, d))
    mine = call_ref.at[my]
    first = [_remote(mine, mine, sem_s1.at[k - 1], sem_r1.at[k - 1], _flip(me, k)) for k in range(1, N_DEV)]
    for cp in first:
        cp.start()
    for k in range(1, N_DEV):
        theirs = call_ref.at[_index(_flip(me, k))]
        _remote(theirs, theirs, sem_s1.at[k - 1], sem_r1.at[k - 1], _flip(me, k)).wait_recv()
    cv = call_ref[...].reshape(N_DEV * SUBLANES, d)
    cond = cv * _sigmoid(cv)
    for l in range(n_layers):
        rows = jnp.dot(cond, w_ref[l], preferred_element_type=F32, precision=lax.Precision.HIGHEST)
        for b in range(N_DEV):
            sbuf[b, l] = rows[b * SUBLANES : (b + 1) * SUBLANES]
    parts_ref[my] = sbuf[my]
    second = []
    for k in range(1, N_DEV):
        to = _flip(me, k)
        second.append(_remote(sbuf.at[_index(to)], parts_ref.at[my], sem_s2.at[k - 1], sem_r2.at[k - 1], to))
    for cp in second:
        cp.start()
    for k in range(1, N_DEV):
        theirs = parts_ref.at[_index(_flip(me, k))]
        _remote(theirs, theirs, sem_s2.at[k - 1], sem_r2.at[k - 1], _flip(me, k)).wait_recv()
    for cp in first + second:
        cp.wait_send()


def _gather_weights(w_in_t, w_out, c_row, w_ada):
    n_layers, rows_in, d = w_in_t.shape
    rows_out = w_out.shape[1]
    width = w_ada.shape[2]

    def body(wi_ref, wo_ref, c_ref, wa_ref, gi_ref, go_ref, si_ref, so_ref, call_ref, parts_ref, sbuf, send_sems, recv_sems, *ada_sems):
        my = _index(_position())
        si_ref[...] = wi_ref[...].astype(si_ref.dtype)
        so_ref[...] = wo_ref[...].astype(so_ref.dtype)
        gi_ref[pl.ds(pl.multiple_of(my * rows_in, 16), rows_in), :] = si_ref[0]
        go_ref[pl.ds(pl.multiple_of(my * rows_out, 16), rows_out), :] = so_ref[0]
        _two_level_all_gather(
            (_row_block(gi_ref, rows_in), _row_block(go_ref, rows_out)),
            send_sems,
            recv_sems,
            between=functools.partial(_ada_exchange, c_ref, wa_ref, call_ref, parts_ref, sbuf, *ada_sems),
        )

    return pl.pallas_call(
        body,
        name="gather_weights",
        in_specs=[VMEM_SPEC] * 4,
        out_specs=[VMEM_SPEC] * 6,
        out_shape=[
            jax.ShapeDtypeStruct((N_DEV * rows_in, d), MXU_DTYPE),
            jax.ShapeDtypeStruct((N_DEV * rows_out, d), MXU_DTYPE),
            jax.ShapeDtypeStruct(w_in_t.shape, MXU_DTYPE),
            jax.ShapeDtypeStruct(w_out.shape, MXU_DTYPE),
            jax.ShapeDtypeStruct((N_DEV, SUBLANES, d), F32),
            jax.ShapeDtypeStruct((N_DEV, n_layers, SUBLANES, width), F32),
        ],
        scratch_shapes=[
            pltpu.VMEM((N_DEV, n_layers, SUBLANES, width), F32),
            pltpu.SemaphoreType.DMA((14,)),
            pltpu.SemaphoreType.DMA((14,)),
        ]
        + [pltpu.SemaphoreType.DMA((N_DEV - 1,))] * 4,
        compiler_params=_params(),
    )(w_in_t, w_out, c_row, w_ada)


def _gather_small(packed, d_ws):
    def body(p_ref, ws_ref, g_ref, gws_ref, send_sems, recv_sems):
        my = _index(_position())
        g_ref[my] = p_ref[...]
        gws_ref[my] = ws_ref[...].astype(gws_ref.dtype)
        _two_level_all_gather((lambda j: g_ref.at[j], lambda j: gws_ref.at[j]), send_sems, recv_sems)

    return pl.pallas_call(
        body,
        name="gather_small",
        in_specs=[VMEM_SPEC, VMEM_SPEC],
        out_specs=[VMEM_SPEC, VMEM_SPEC],
        out_shape=[
            jax.ShapeDtypeStruct((N_DEV,) + packed.shape, F32),
            jax.ShapeDtypeStruct((N_DEV,) + d_ws.shape, jnp.bfloat16),
        ],
        scratch_shapes=[pltpu.SemaphoreType.DMA((14,)), pltpu.SemaphoreType.DMA((14,))],
        compiler_params=_params(),
    )(packed, d_ws)


def _scatter_finish(landed, name):
    n = len(landed)

    def body(*refs):
        for src, out in zip(refs[:n], refs[n:]):
            g = src[0].astype(F32)
            for j in range(1, N_DEV):
                g = g + src[j].astype(F32)
            out[...] = g

    return pl.pallas_call(
        body,
        name=name,
        in_specs=[VMEM_SPEC] * n,
        out_specs=[VMEM_SPEC] * n,
        out_shape=[jax.ShapeDtypeStruct(a.shape[1:], F32) for a in landed],
        compiler_params=_params(),
    )(*landed)


def _pack_rows(parts):
    rows, offsets, at = [], [], 0
    for p in parts:
        flat = p.reshape(-1)
        n = -(-flat.shape[0] // (SUBLANES * LANES)) * SUBLANES
        rows.append(jnp.pad(flat, (0, n * LANES - flat.shape[0])).reshape(n, LANES))
        offsets.append(at)
        at += n
    return jnp.concatenate(rows, axis=0), offsets


def _unpack_rows(packed, offsets, shapes):
    out = []
    for off, shape in zip(offsets, shapes):
        size = 1
        for s in shape:
            size *= s
        n = -(-size // (SUBLANES * LANES)) * SUBLANES
        out.append(packed[off : off + n].reshape(-1)[:size].reshape(shape))
    return out


def kernel(x, c, w_ada, b_ada, norm_gain, w_in, q_gain, k_gain, sink, w_s, b_s, w_out, loss_target, m_w_ada, m_b_ada, m_norm_gain, m_w_in, m_q_gain, m_k_gain, m_sink, m_w_s, m_b_s, m_w_out, v_w_ada, v_b_ada, v_norm_gain, v_w_in, v_q_gain, v_k_gain, v_sink, v_w_s, v_b_s, v_w_out):
    seq, d = x.shape[1], x.shape[2]
    n_layers = w_in.shape[0]
    w_cols = w_in.shape[2]
    ada_cols = w_ada.shape[2]
    my = _index(_position())
    xs = x.reshape(seq, d)
    target = loss_target.reshape(seq, d)

    w_in_t0, w_out0, shard_in, shard_out, c_all, ada_parts = _gather_weights(w_in.transpose(0, 2, 1), w_out, c, w_ada)
    w_in_ts, w_outs = [w_in_t0], [w_out0]
    ada = ada_parts[:, :, 0, :].transpose(1, 0, 2).reshape(n_layers, 3 * d) + b_ada
    shift, scale1, gate = ada[:, None, 0:d], 1.0 + ada[:, None, d : 2 * d], ada[:, None, 2 * d : 3 * d]
    gain = norm_gain[:, None, :]

    w_s_m = w_s.astype(MXU_DTYPE)
    w_s_t = w_s_m.transpose(0, 1, 3, 2)
    b_st = jnp.repeat(b_s.transpose(0, 2, 1), HEAD_DIM, axis=2)
    q_gain2 = jnp.tile(q_gain, (1, 2))[:, None, :]
    k_gain2 = jnp.tile(k_gain, (1, 2))[:, None, :]

    xl, saved = xs, []
    for l in range(n_layers):
        last = l == n_layers - 1
        pa, pb = _ln_proj_fwd(xl, gain[l], scale1[l], shift[l], w_in_ts[l], f"ln_proj_fwd_{l}")
        if last:
            o = _attn_fwd(pa, q_gain2[l], k_gain2[l], sink[l], f"attn_fwd_{l}")
        else:
            o, w_in_next, w_out_next = _attn_fwd(
                pa, q_gain2[l], k_gain2[l], sink[l], f"attn_fwd_{l}", gather=(shard_in[l + 1], shard_out[l + 1])
            )
            w_in_ts.append(w_in_next)
            w_outs.append(w_out_next)
        saved.append((xl, pa, pb, o))
        out = _mix_out_fwd(pb, o, xl, gate[l], w_outs[l], w_s_m[l], b_st[l], f"mix_out_fwd_{l}", target if last else None)
        if last:
            dx, sq_err = out
        else:
            xl = out

    g_w_in, g_w_out, small, d_ada_rows = [None] * n_layers, [None] * n_layers, [None] * n_layers, [None] * n_layers
    waiting = []
    for l in reversed(range(n_layers)):
        x_l, pa, pb, o = saved[l]
        dpb, do, dw_out, d_gate8, d_ws, d_bs = _mix_out_bwd(
            dx, pb, o, gate[l], w_outs[l], w_s_m[l], w_s_t[l], b_st[l], f"mix_out_bwd_{l}"
        )
        riding = waiting + [(g_w_out, l, dw_out.reshape(4, 2, D_MIX // N_DEV, d))]
        attn = _attn_bwd(pa, o, do, q_gain2[l], k_gain2[l], sink[l], f"attn_bwd_{l}", scatter=tuple(b for _, _, b in riding))
        dq, dkv, halo_prev, halo_next, d_qg, d_kg, d_sk = attn[:7]
        for (dest, layer, _), total in zip(riding, _scatter_finish(attn[7:], f"scatter_finish_{l}")):
            dest[layer] = total.transpose(1, 0) if dest is g_w_in else total
        dw_in_t, dkvb = _proj_bwd_dw(x_l, gain[l], scale1[l], shift[l], dq, dkv, halo_prev, halo_next, dpb, f"proj_bwd_dw_{l}")
        blocks_in = dw_in_t.reshape(4, 2, w_cols, d)
        waiting = [(g_w_in, l, blocks_in)] if l > 0 else []
        dxs = _proj_bwd_dx(
            x_l, dx, dq, dkvb, dpb, w_in_ts[l], gain[l], scale1[l], f"proj_bwd_dx_{l}", scatter=() if l > 0 else (blocks_in,)
        )
        dx, c0, c1 = dxs[:3]
        if l == 0:
            g_w_in[l] = _scatter_finish(dxs[3:], "scatter_finish_in_0")[0].transpose(1, 0)
        c0s, c1s = c0.sum(axis=0), c1.sum(axis=0)
        d_ada_rows[l] = jnp.concatenate([c0s, norm_gain[l] * c1s, d_gate8.sum(axis=0)])
        small[l] = (
            scale1[l, 0] * c1s,
            d_qg.sum(axis=0).reshape(N_HEADS, HEAD_DIM).sum(axis=0),
            d_kg.sum(axis=0).reshape(2, HEAD_DIM).sum(axis=0),
            d_sk[0, 0:N_HEADS],
            d_bs.reshape(BLK, N_GROUPS, HEAD_DIM).sum(axis=2).transpose(1, 0),
            d_ws,
        )

    names = ("norm_gain", "q_gain", "k_gain", "sink", "b_s")
    stacked = [jnp.stack([small[l][t] for l in range(n_layers)]) for t in range(len(names))]
    d_ada = jnp.stack(d_ada_rows)
    packed, offsets = _pack_rows(stacked + [d_ada, sq_err[0, 0:1]])
    d_ws = jnp.stack([small[l][len(names)] for l in range(n_layers)]).reshape(-1, LANES)
    gathered, gathered_ws = _gather_small(packed, d_ws)
    no_weight = jnp.zeros((1,), F32)
    weights = (norm_gain, q_gain, k_gain, sink, b_s, b_ada, no_weight)
    moments_m = (m_norm_gain, m_q_gain, m_k_gain, m_sink, m_b_s, m_b_ada, no_weight)
    moments_v = (v_norm_gain, v_q_gain, v_k_gain, v_sink, v_b_s, v_b_ada, no_weight)
    w_pack, _ = _pack_rows(weights)
    m_pack, _ = _pack_rows(moments_m)
    v_pack, _ = _pack_rows(moments_v)
    shapes = [w.shape for w in weights]
    flat_ws = lambda a: a.reshape(-1, LANES)
    updated = _small_update(gathered, gathered_ws, w_pack, m_pack, v_pack, flat_ws(w_s), flat_ws(m_w_s), flat_ws(v_w_s))
    g_small, d_small, m_small, v_small = (_unpack_rows(p, offsets, shapes) for p in updated[0:4])
    ws_small = [p.reshape(w_s.shape) for p in updated[4:8]]
    loss = g_small[-1][0] * (0.5 / d)

    ada_off = offsets[-2]
    ada_n = -(-n_layers * 3 * d // (SUBLANES * LANES)) * SUBLANES
    d_ada_all = gathered[:, ada_off : ada_off + ada_n].reshape(N_DEV, -1)[:, : n_layers * 3 * d].reshape(N_DEV, n_layers, 3 * d)
    d_ada_cols = lax.dynamic_slice_in_dim(d_ada_all, my * ada_cols, ada_cols, axis=2)
    g_w_ada, *upd_ada = _ada_update(c_all[:, 0, :], d_ada_cols.transpose(1, 0, 2), w_ada, m_w_ada, v_w_ada)

    def update(w, g, m, v, name):
        shape = w.shape
        flat = lambda a: a.reshape(-1, shape[-1])
        return tuple(a.reshape(shape) for a in _adamw(flat(w), flat(g), flat(m), flat(v), name))

    g_w_in, g_w_out = jnp.stack(g_w_in), jnp.stack(g_w_out)
    upd_in = update(w_in, g_w_in, m_w_in, v_w_in, "adamw_w_in")
    upd_out = update(w_out, g_w_out, m_w_out, v_w_out, "adamw_w_out")

    def ordered(ada_, in_, out_, small_, ws):
        ng, qg, kg, sk, bs, ba, _ = small_
        return (ada_, ba, ng, in_, qg, kg, sk, ws, bs, out_)

    grads = ordered(g_w_ada, g_w_in, g_w_out, g_small, ws_small[0])
    deltas = ordered(upd_ada[0], upd_in[0], upd_out[0], d_small, ws_small[1])
    new_m = ordered(upd_ada[1], upd_in[1], upd_out[1], m_small, ws_small[2])
    new_v = ordered(upd_ada[2], upd_in[2], upd_out[2], v_small, ws_small[3])
    return (loss, dx.reshape(x.shape), *grads, *deltas, *new_m, *new_v)
```

```python
import functools

import jax
import jax.numpy as jnp
from jax import lax
from jax.experimental import pallas as pl
from jax.experimental.pallas import tpu as pltpu

F32 = jnp.float32
MXU_DTYPE = jnp.bfloat16
MESH_ID = pl.DeviceIdType.MESH

N_DEV = 8
HEAD_DIM = 64
N_HEADS = 8
Q_PER_KV = 4
D_ATTN = 512
D_KV = 128
D_GM = 512
N_GROUPS = 8
D_MIX = D_ATTN + D_GM
BLK = 128
LANES = 128
SUBLANES = 8
N_PAIRS = D_ATTN // LANES
D_QKV = D_ATTN + 2 * D_KV
D_REST = D_ATTN + 3 * D_GM
D_IN = D_QKV + D_REST
EPS = 1e-6
NEG_INF = -1e30
ALIBI_SLOPES = tuple(2.0 ** (-8.0 * (h + 1) / N_HEADS) for h in range(N_HEADS))
Q_SCALE = 1.0 / 8.0

ADAM_LR = 0.001
ADAM_B1 = 0.9
ADAM_B2 = 0.999
ADAM_EPS = 1e-08
ADAM_WD = 0.01
ADAM_STEP = 10

TOKEN_TILE = 512
VMEM_LIMIT_BYTES = 56 * 1024 * 1024


def _params(semantics=None):
    return pltpu.CompilerParams(dimension_semantics=semantics, vmem_limit_bytes=VMEM_LIMIT_BYTES)


def _dot(a, b):
    return jnp.dot(a, b, preferred_element_type=F32)


def _dot_nt(a, b):
    return lax.dot_general(a, b, (((1,), (1,)), ((), ())), preferred_element_type=F32)


def _dot_tn(a, b):
    return lax.dot_general(a, b, (((0,), (0,)), ((), ())), preferred_element_type=F32)


def _mx(v):
    return v.astype(MXU_DTYPE)


def _lane_lo(rows):
    return lax.broadcasted_iota(jnp.int32, (rows, LANES), 1) < HEAD_DIM


def _half_ones(width=LANES):
    r = jnp.right_shift(lax.broadcasted_iota(jnp.int32, (width, width), 0), 6)
    c = jnp.right_shift(lax.broadcasted_iota(jnp.int32, (width, width), 1), 6)
    return jnp.where(r == c, 1.0, 0.0).astype(jnp.bfloat16)


WIDE = 2 * LANES


def _half_sum(v, ones):
    p1 = v.astype(jnp.bfloat16)
    p2 = (v - p1.astype(F32)).astype(jnp.bfloat16)
    return _dot(p1, ones) + _dot(p2, ones)


def _half_rms(v, ones):
    r = lax.rsqrt(_half_sum(v * v, ones) * (1.0 / HEAD_DIM) + EPS)
    return v * r, r


def _half_rms_bwd(dy, vhat, r, ones):
    return r * (dy - vhat * (_half_sum(vhat * dy, ones) * (1.0 / HEAD_DIM)))


def _group_rows(v):
    rows, n = v.shape
    return v.reshape(rows // SUBLANES, SUBLANES, n).sum(axis=0)


def _sigmoid(v):
    return 1.0 / (1.0 + jnp.exp(-v))


ROW_CHUNK = 32
VARIANT_HEADS = ((0, 2, 5, 7), (1, 3, 4, 6))
HEAD_SLOT = {h: (v, s) for v, heads in enumerate(VARIANT_HEADS) for s, h in enumerate(heads)}
STACK = Q_PER_KV * BLK


def _fill_attn_bias(bias_s):
    qi = lax.broadcasted_iota(jnp.int32, (BLK, 3 * BLK), 0)
    ci = lax.broadcasted_iota(jnp.int32, (BLK, 3 * BLK), 1)
    dist = jnp.abs(ci - BLK - qi)
    distf = dist.astype(F32)
    window = dist <= BLK
    for kind, seen in enumerate((window & (ci >= BLK), window, window & (ci < 2 * BLK))):
        for h in range(N_HEADS):
            bias_s[kind, h] = jnp.where(seen, -(ALIBI_SLOPES[h] * distf), NEG_INF)


def _block_kind(block, seq):
    assert seq >= 2 * BLK
    return jnp.where(block == 0, 0, jnp.where(block == seq // BLK - 1, 2, 1))


def _stage_queries(qn, lo_t, j, nb, qs):
    for a in range(2):
        v, slot = HEAD_SLOT[2 * j + a]
        qm = _mx(jnp.where(lo_t, qn, 0.0) if a == 0 else jnp.where(lo_t, 0.0, qn))
        for n in range(nb):
            qs[n, v, slot * BLK : (slot + 1) * BLK, :] = qm[n * BLK : (n + 1) * BLK]


def _unstack_pair(stacked, j, lo):
    (v0, s0), (v1, s1) = HEAD_SLOT[2 * j], HEAD_SLOT[2 * j + 1]
    return jnp.where(lo, stacked[v0][s0 * BLK : (s0 + 1) * BLK], stacked[v1][s1 * BLK : (s1 + 1) * BLK])


def _stage_keys(kvp_ref, qkv_ref, kvn_ref, kg, ones, tile, ks, kr, vs, vr, khat_s=None, rk_s=None):
    pieces = (
        (0, BLK, kvp_ref[:, 0:D_KV], kvp_ref[:, D_KV : 2 * D_KV]),
        (BLK, tile, qkv_ref[:, D_ATTN : D_ATTN + D_KV], qkv_ref[:, D_ATTN + D_KV : D_QKV]),
        (BLK + tile, BLK, kvn_ref[:, 0:D_KV], kvn_ref[:, D_KV : 2 * D_KV]),
    )
    for r0, n, k, v in pieces:
        khat, rk = _half_rms(k, ones)
        kn = khat * kg
        ks[r0 : r0 + n, :] = _mx(kn)
        kr[r0 : r0 + n, :] = _mx(pltpu.roll(kn, HEAD_DIM, 1))
        vs[r0 : r0 + n, :] = _mx(v)
        vr[r0 : r0 + n, :] = _mx(pltpu.roll(v, HEAD_DIM, 1))
        if khat_s is not None:
            khat_s[r0 : r0 + n, :] = khat
            rk_s[r0 : r0 + n, :] = rk


def _halo_specs(tile, seq):
    nb = tile // BLK
    last = seq // BLK - 1
    kv_col = D_ATTN // (2 * D_KV)
    prev = pl.BlockSpec((BLK, 2 * D_KV), lambda i: (jnp.maximum(i * nb - 1, 0), kv_col))
    nxt = pl.BlockSpec((BLK, 2 * D_KV), lambda i: (jnp.minimum((i + 1) * nb, last), kv_col))
    return prev, nxt


def _row_spec(tile, width):
    return pl.BlockSpec((tile, width), lambda i: (i, 0))


def _full_spec(shape):
    nd = len(shape)
    return pl.BlockSpec(shape, lambda i: (0,) * nd)


SMEM_SPEC = pl.BlockSpec(memory_space=pltpu.SMEM)
VMEM_SPEC = pl.BlockSpec(memory_space=pltpu.VMEM)
HBM_SPEC = pl.BlockSpec(memory_space=pltpu.HBM)


def _ln_proj_fwd(x, gain, scale1, shift, w_in_t, name):
    seq, d = x.shape
    tile = min(TOKEN_TILE, seq)

    def body(x_ref, g_ref, s1_ref, sh_ref, wt_ref, pa_ref, pb_ref):
        xv = x_ref[...]
        r = lax.rsqrt(jnp.mean(xv * xv, axis=-1, keepdims=True) + EPS)
        h = _mx((xv * r) * g_ref[...] * s1_ref[...] + sh_ref[...])
        pa_ref[...] = _dot_nt(h, wt_ref[0:D_QKV, :])
        pb_ref[...] = _dot_nt(h, wt_ref[D_QKV:D_IN, :])

    vec = _full_spec((1, d))
    return pl.pallas_call(
        body,
        name=name,
        grid=(seq // tile,),
        in_specs=[_row_spec(tile, d), vec, vec, vec, _full_spec((D_IN, d))],
        out_specs=[_row_spec(tile, D_QKV), _row_spec(tile, D_REST)],
        out_shape=[jax.ShapeDtypeStruct((seq, D_QKV), F32), jax.ShapeDtypeStruct((seq, D_REST), F32)],
        compiler_params=_params(("parallel",)),
    )(x, gain, scale1, shift, w_in_t)


def _rider_steps(nt):
    return 0, (2 * nt) // 3, nt - 1


def _attn_fwd(pa, q_gain2, k_gain2, sink, name, gather=None):
    seq = pa.shape[0]
    tile = min(TOKEN_TILE, seq)
    nb = tile // BLK
    nt = seq // tile
    ext = tile + 2 * BLK
    riding = gather is not None

    def body(sink_ref, qkv_ref, kvp_ref, kvn_ref, qg_ref, kg_ref, *rest):
        i = pl.program_id(0)
        if riding:
            shard_in, shard_out, o_ref, full_in, full_out = rest[0:5]
            qs, ks, kr, vs, vr, bias_s, s_scr, p_scr, inv_scr, send_sems, recv_sems, local_sems = rest[5:]
            start, forward, finish = _all_gather_stages(
                (_row_block(full_in, shard_in.shape[0]), _row_block(full_out, shard_out.shape[0])),
                send_sems,
                recv_sems,
                sources=(shard_in, shard_out),
                local_sems=local_sems,
            )
            at_start, at_forward, at_finish = _rider_steps(nt)
            pl.when(i == at_start)(start)
        else:
            o_ref, qs, ks, kr, vs, vr, bias_s, s_scr, p_scr, inv_scr = rest

        @pl.when(i == 0)
        def _():
            _fill_attn_bias(bias_s)

        ones = _half_ones()
        lo = _lane_lo(BLK)
        lo_t = _lane_lo(tile)
        _stage_keys(kvp_ref, qkv_ref, kvn_ref, kg_ref[...], ones, tile, ks, kr, vs, vr)
        for j in range(N_PAIRS):
            qhat, _ = _half_rms(qkv_ref[:, j * LANES : (j + 1) * LANES], ones)
            _stage_queries(qhat * (qg_ref[...] * Q_SCALE), lo_t, j, nb, qs)

        def block(n, carry):
            r0 = pl.multiple_of(n * BLK, BLK)
            krows = pl.ds(r0, 3 * BLK)
            kind = _block_kind(i * nb + n, seq)
            for v in range(2):
                s_scr[v] = _dot_nt(qs[n, v], (kr if v else ks)[krows, :])
            for h in range(N_HEADS):
                v, slot = HEAD_SLOT[h]
                sink_h = sink_ref[h]
                for rc in range(0, BLK, ROW_CHUNK):
                    rows = slice(slot * BLK + rc, slot * BLK + rc + ROW_CHUNK)
                    s = s_scr[v, rows, :] + bias_s[kind, h, rc : rc + ROW_CHUNK, :]
                    m = jnp.maximum(jnp.max(s, axis=-1, keepdims=True), sink_h)
                    p = jnp.exp(s - m)
                    total = jnp.sum(p, axis=-1, keepdims=True) + jnp.exp(sink_h - m)
                    p_scr[v, rows, :] = _mx(p)
                    inv_scr[v, rows, :] = jnp.broadcast_to(1.0 / total, (ROW_CHUNK, LANES))
            outs = [_dot(p_scr[v], (vr if v else vs)[krows, :]) * inv_scr[v] for v in range(2)]
            for j in range(N_PAIRS):
                o_ref[pl.ds(r0, BLK), j * LANES : (j + 1) * LANES] = _unstack_pair(outs, j, lo)
            return carry

        lax.fori_loop(0, nb, block, 0)
        if riding:
            pl.when(i == at_forward)(forward)
            pl.when(i == at_finish)(finish)

    prev, nxt = _halo_specs(tile, seq)
    vec = _full_spec((1, LANES))
    in_specs = [SMEM_SPEC, _row_spec(tile, D_QKV), prev, nxt, vec, vec]
    out_specs = [_row_spec(tile, D_ATTN)]
    out_shape = [jax.ShapeDtypeStruct((seq, D_ATTN), F32)]
    scratch = [
        pltpu.VMEM((nb, 2, STACK, LANES), MXU_DTYPE),
        pltpu.VMEM((ext, LANES), MXU_DTYPE),
        pltpu.VMEM((ext, LANES), MXU_DTYPE),
        pltpu.VMEM((ext, LANES), MXU_DTYPE),
        pltpu.VMEM((ext, LANES), MXU_DTYPE),
        pltpu.VMEM((3, N_HEADS, BLK, 3 * BLK), F32),
        pltpu.VMEM((2, STACK, 3 * BLK), F32),
        pltpu.VMEM((2, STACK, 3 * BLK), MXU_DTYPE),
        pltpu.VMEM((2, STACK, LANES), F32),
    ]
    extra = ()
    if riding:
        extra = tuple(gather)
        in_specs += [HBM_SPEC] * 2
        out_specs += [HBM_SPEC] * 2
        out_shape += [jax.ShapeDtypeStruct((N_DEV * g.shape[0], g.shape[1]), g.dtype) for g in gather]
        scratch += [pltpu.SemaphoreType.DMA((14,)), pltpu.SemaphoreType.DMA((14,)), pltpu.SemaphoreType.DMA((2,))]
    out = pl.pallas_call(
        body,
        name=name,
        grid=(nt,),
        in_specs=in_specs,
        out_specs=out_specs,
        out_shape=out_shape,
        scratch_shapes=scratch,
        compiler_params=_params(("arbitrary",)),
    )(sink, pa, pa, pa, q_gain2, k_gain2, *extra)
    return out if riding else out[0]


def _mix_out_fwd(pb, o, x, gate, w_out, w_s, b_st, name, target=None):
    seq, d = x.shape
    tile = min(TOKEN_TILE, seq)
    nb = tile // BLK
    with_loss = target is not None

    def body(pb_ref, o_ref, x_ref, gate_ref, wo_ref, ws_ref, bs_ref, *rest):
        if with_loss:
            t_ref, xo_ref, acc_ref, y_s, vn_s = rest

            @pl.when(pl.program_id(0) == 0)
            def _():
                acc_ref[...] = jnp.zeros_like(acc_ref)
        else:
            xo_ref, y_s, vn_s = rest
        ones = _half_ones(WIDE)
        lo = _lane_lo(BLK)
        ga = pb_ref[:, 0:D_ATTN]
        y_s[:, 0:D_ATTN] = _mx(o_ref[...] * (ga * _sigmoid(ga)))
        for j in range(D_GM // WIDE):
            vhat, _ = _half_rms(pb_ref[:, 2 * D_GM + j * WIDE : 2 * D_GM + (j + 1) * WIDE], ones)
            vn_s[:, j * WIDE : (j + 1) * WIDE] = _mx(vhat)

        def chunk(n, carry):
            rows = pl.ds(pl.multiple_of(n * BLK, BLK), BLK)
            for j in range(N_PAIRS):
                cols = slice(j * LANES, (j + 1) * LANES)
                vn = vn_s[rows, cols]
                sv = jnp.where(lo, _dot(ws_ref[2 * j], vn), _dot(ws_ref[2 * j + 1], vn)) + bs_ref[:, cols]
                u = pb_ref[rows, D_ATTN + j * LANES : D_ATTN + (j + 1) * LANES]
                gg = pb_ref[rows, D_ATTN + 2 * D_GM + j * LANES : D_ATTN + 2 * D_GM + (j + 1) * LANES]
                y_s[rows, D_ATTN + j * LANES : D_ATTN + (j + 1) * LANES] = _mx((u * sv) * (gg * _sigmoid(gg)))
            return carry

        lax.fori_loop(0, nb, chunk, 0)
        y = x_ref[...] + gate_ref[...] * _dot(y_s[...], wo_ref[...])
        if with_loss:
            e = y - t_ref[...]
            xo_ref[...] = e * (1.0 / d)
            acc_ref[...] += jnp.sum(jnp.sum(e * e, axis=-1, keepdims=True), axis=0, keepdims=True)
        else:
            xo_ref[...] = y

    row = _row_spec(tile, d)
    acc_shape = (SUBLANES, LANES)
    return pl.pallas_call(
        body,
        name=name,
        grid=(seq // tile,),
        in_specs=[
            _row_spec(tile, D_REST),
            _row_spec(tile, D_ATTN),
            row,
            _full_spec((1, d)),
            _full_spec((D_MIX, d)),
            _full_spec((N_GROUPS, BLK, BLK)),
            _full_spec((BLK, D_GM)),
        ]
        + ([row] if with_loss else []),
        out_specs=[row, _full_spec(acc_shape)] if with_loss else row,
        out_shape=[jax.ShapeDtypeStruct((seq, d), F32), jax.ShapeDtypeStruct(acc_shape, F32)]
        if with_loss
        else jax.ShapeDtypeStruct((seq, d), F32),
        scratch_shapes=[pltpu.VMEM((tile, D_MIX), MXU_DTYPE), pltpu.VMEM((tile, D_GM), MXU_DTYPE)],
        compiler_params=_params(("arbitrary",) if with_loss else ("parallel",)),
    )(pb, o, x, gate, w_out, w_s, b_st, *([target] if with_loss else []))


def _mix_out_bwd(dxn, pb, o, gate, w_out, w_s, w_s_t, b_st, name):
    seq, d = dxn.shape
    tile = min(TOKEN_TILE, seq)
    nb = tile // BLK
    nt = seq // tile

    def body(dxn_ref, pb_ref, o_ref, gate_ref, wo_ref, ws_ref, wst_ref, bs_ref,
             dpb_ref, do_ref, dwo_ref, dgate_ref, dws_ref, dbs_ref, g_ref, y_s, dy_s, vn_s, rv_s, vnb_s, sv_s, dsv_s, dvn_s):
        @pl.when(pl.program_id(0) == 0)
        def _():
            g_ref[...] = jnp.zeros_like(g_ref)
            dws_ref[...] = jnp.zeros_like(dws_ref)
            dbs_ref[...] = jnp.zeros_like(dbs_ref)

        ones = _half_ones(WIDE)
        lo = _lane_lo(BLK)
        c_u = slice(D_ATTN, D_ATTN + D_GM)
        c_vg = slice(D_ATTN + D_GM, D_ATTN + 2 * D_GM)
        c_gg = slice(D_ATTN + 2 * D_GM, D_REST)
        dxv = dxn_ref[...]
        dy_s[...] = _dot_nt(_mx(dxv * gate_ref[...]), wo_ref[...])
        ga = pb_ref[:, 0:D_ATTN]
        sig = _sigmoid(ga)
        sil = ga * sig
        ov = o_ref[...]
        y_s[:, 0:D_ATTN] = _mx(ov * sil)
        da = dy_s[:, 0:D_ATTN]
        do_ref[...] = da * sil
        dpb_ref[:, 0:D_ATTN] = (da * ov * (sig * (1.0 + ga * (1.0 - sig)))).astype(dpb_ref.dtype)
        for j in range(D_GM // WIDE):
            cols = slice(j * WIDE, (j + 1) * WIDE)
            vhat, rv = _half_rms(pb_ref[:, 2 * D_GM + j * WIDE : 2 * D_GM + (j + 1) * WIDE], ones)
            vn_s[:, cols] = vhat
            rv_s[:, cols] = rv
            vnb_s[:, cols] = _mx(vhat)

        def spatial_fwd(n, carry):
            rows = pl.ds(pl.multiple_of(n * BLK, BLK), BLK)
            for j in range(N_PAIRS):
                cols = slice(j * LANES, (j + 1) * LANES)
                vn = vnb_s[rows, cols]
                sv_s[rows, cols] = jnp.where(lo, _dot(ws_ref[2 * j], vn), _dot(ws_ref[2 * j + 1], vn)) + bs_ref[:, cols]
            return carry

        lax.fori_loop(0, nb, spatial_fwd, 0)

        def gating(n, carry):
            rows = pl.ds(pl.multiple_of(n * BLK, BLK), BLK)
            sv = sv_s[rows, :]
            u = pb_ref[rows, c_u]
            gg = pb_ref[rows, c_gg]
            sg = _sigmoid(gg)
            silg = gg * sg
            m0 = u * sv
            y_s[rows, D_ATTN:D_MIX] = _mx(m0 * silg)
            dm = dy_s[rows, D_ATTN:D_MIX]
            dm0 = dm * silg
            dpb_ref[rows, c_gg] = (dm * m0 * (sg * (1.0 + gg * (1.0 - sg)))).astype(dpb_ref.dtype)
            dpb_ref[rows, c_u] = (dm0 * sv).astype(dpb_ref.dtype)
            dsv = dm0 * u
            dsv_s[rows, :] = _mx(dsv)
            dbs_ref[...] += dsv
            return carry

        lax.fori_loop(0, nb, gating, 0)

        def spatial_bwd(n, carry):
            rows = pl.ds(pl.multiple_of(n * BLK, BLK), BLK)
            for j in range(N_PAIRS):
                cols = slice(j * LANES, (j + 1) * LANES)
                dsv = dsv_s[rows, cols]
                dvn_s[rows, cols] = jnp.where(lo, _dot(wst_ref[2 * j], dsv), _dot(wst_ref[2 * j + 1], dsv))
            return carry

        lax.fori_loop(0, nb, spatial_bwd, 0)
        zero = jnp.zeros((BLK, LANES), MXU_DTYPE)
        for j in range(N_PAIRS):
            cols = slice(j * LANES, (j + 1) * LANES)
            chunks = [dsv_s[n * BLK : (n + 1) * BLK, cols] for n in range(nb)]
            vn_all = jnp.concatenate([vnb_s[n * BLK : (n + 1) * BLK, cols] for n in range(nb)], axis=1)
            dws_ref[2 * j] += _dot_nt(jnp.concatenate([jnp.where(lo, c, zero) for c in chunks], axis=1), vn_all)
            dws_ref[2 * j + 1] += _dot_nt(jnp.concatenate([jnp.where(lo, zero, c) for c in chunks], axis=1), vn_all)
        for j in range(D_GM // WIDE):
            cols = slice(j * WIDE, (j + 1) * WIDE)
            dpb_ref[:, D_ATTN + D_GM + j * WIDE : D_ATTN + D_GM + (j + 1) * WIDE] = _half_rms_bwd(
                dvn_s[:, cols], vn_s[:, cols], rv_s[:, cols], ones
            ).astype(dpb_ref.dtype)
        g_ref[...] += _dot_tn(y_s[...], _mx(dxv))

        @pl.when(pl.program_id(0) == nt - 1)
        def _():
            gv = g_ref[...]
            dwo_ref[...] = (gv * gate_ref[...]).astype(dwo_ref.dtype)
            dgate_ref[...] = _group_rows(gv * wo_ref[...].astype(F32))

    return pl.pallas_call(
        body,
        name=name,
        grid=(seq // tile,),
        in_specs=[
            _row_spec(tile, d),
            _row_spec(tile, D_REST),
            _row_spec(tile, D_ATTN),
            _full_spec((1, d)),
            _full_spec((D_MIX, d)),
            _full_spec((N_GROUPS, BLK, BLK)),
            _full_spec((N_GROUPS, BLK, BLK)),
            _full_spec((BLK, D_GM)),
        ],
        out_specs=[
            _row_spec(tile, D_REST),
            _row_spec(tile, D_ATTN),
            _full_spec((D_MIX, d)),
            _full_spec((SUBLANES, d)),
            _full_spec((N_GROUPS, BLK, BLK)),
            _full_spec((BLK, D_GM)),
        ],
        out_shape=[
            jax.ShapeDtypeStruct((seq, D_REST), MXU_DTYPE),
            jax.ShapeDtypeStruct((seq, D_ATTN), F32),
            jax.ShapeDtypeStruct((D_MIX, d), jnp.bfloat16),
            jax.ShapeDtypeStruct((SUBLANES, d), F32),
            jax.ShapeDtypeStruct((N_GROUPS, BLK, BLK), F32),
            jax.ShapeDtypeStruct((BLK, D_GM), F32),
        ],
        scratch_shapes=[
            pltpu.VMEM((D_MIX, d), F32),
            pltpu.VMEM((tile, D_MIX), MXU_DTYPE),
            pltpu.VMEM((tile, D_MIX), F32),
            pltpu.VMEM((tile, D_GM), F32),
            pltpu.VMEM((tile, D_GM), F32),
            pltpu.VMEM((tile, D_GM), MXU_DTYPE),
            pltpu.VMEM((tile, D_GM), F32),
            pltpu.VMEM((tile, D_GM), MXU_DTYPE),
            pltpu.VMEM((tile, D_GM), F32),
        ],
        compiler_params=_params(("arbitrary",)),
    )(dxn, pb, o, gate, w_out, w_s, w_s_t, b_st)


def _attn_bwd(pa, o, do, q_gain2, k_gain2, sink, name, scatter=()):
    seq = pa.shape[0]
    tile = min(TOKEN_TILE, seq)
    nb = tile // BLK
    nt = seq // tile
    ext = tile + 2 * BLK
    n_ride = len(scatter)
    riding = n_ride > 0

    def body(sink_ref, qkv_ref, kvp_ref, kvn_ref, o_ref, do_ref, qg_ref, kg_ref, *rest):
        i = pl.program_id(0)
        blocks, rest = rest[:n_ride], rest[n_ride:]
        dq_ref, dkv_ref, hp_ref, hn_ref, dqg_ref, dkg_ref, dsk_ref = rest[:7]
        landing, rest = rest[7 : 7 + n_ride], rest[7 + n_ride :]
        (qs, dos, qhat_s, rq_s, ks, kr, vs, vr, khat_s, rk_s, dqn_s, dka, dva, bias_s, s_scr, dp_scr, p_scr, ds_scr) = rest[:18]
        if riding:
            start, finish = _scatter_stages(blocks, landing, *rest[18:])
            at_start, _, at_finish = _rider_steps(nt)
            pl.when(i == at_start)(start)

        @pl.when(i == 0)
        def _():
            dqg_ref[...] = jnp.zeros_like(dqg_ref)
            dkg_ref[...] = jnp.zeros_like(dkg_ref)
            dsk_ref[...] = jnp.zeros_like(dsk_ref)
            _fill_attn_bias(bias_s)

        ones = _half_ones()
        lo = _lane_lo(BLK)
        lo_t = _lane_lo(tile)
        lo_c = _lane_lo(ROW_CHUNK)
        qg = qg_ref[...] * Q_SCALE
        kg = kg_ref[...]
        _stage_keys(kvp_ref, qkv_ref, kvn_ref, kg, ones, tile, ks, kr, vs, vr, khat_s, rk_s)
        for j in range(N_PAIRS):
            cols = slice(j * LANES, (j + 1) * LANES)
            qhat, rq = _half_rms(qkv_ref[:, cols], ones)
            qhat_s[:, cols] = qhat
            rq_s[:, cols] = rq
            _stage_queries(qhat * qg, lo_t, j, nb, qs)
            _stage_queries(do_ref[:, cols], lo_t, j, nb, dos)
        dka[...] = jnp.zeros_like(dka)
        dva[...] = jnp.zeros_like(dva)
        head_lane = lax.broadcasted_iota(jnp.int32, (1, LANES), 1)

        def block(n, dsink):
            r0 = pl.multiple_of(n * BLK, BLK)
            krows = pl.ds(r0, 3 * BLK)
            kind = _block_kind(i * nb + n, seq)
            for v in range(2):
                s_scr[v] = _dot_nt(qs[n, v], (kr if v else ks)[krows, :])
                dp_scr[v] = _dot_nt(dos[n, v], (vr if v else vs)[krows, :])
            for h in range(N_HEADS):
                v, slot = HEAD_SLOT[h]
                j, a = divmod(h, 2)
                cols = slice(j * LANES, (j + 1) * LANES)
                sink_h = sink_ref[h]
                sink_part = jnp.zeros((ROW_CHUNK, 1), F32)
                for rc in range(0, BLK, ROW_CHUNK):
                    rows = slice(slot * BLK + rc, slot * BLK + rc + ROW_CHUNK)
                    trows = pl.ds(pl.multiple_of(r0 + rc, ROW_CHUNK), ROW_CHUNK)
                    s = s_scr[v, rows, :] + bias_s[kind, h, rc : rc + ROW_CHUNK, :]
                    m = jnp.maximum(jnp.max(s, axis=-1, keepdims=True), sink_h)
                    p = jnp.exp(s - m)
                    e_sink = jnp.exp(sink_h - m)
                    inv = 1.0 / (jnp.sum(p, axis=-1, keepdims=True) + e_sink)
                    pn = p * inv
                    prod = do_ref[trows, cols] * o_ref[trows, cols]
                    prod = jnp.where(lo_c, prod, 0.0) if a == 0 else jnp.where(lo_c, 0.0, prod)
                    dcol = jnp.sum(prod, axis=-1, keepdims=True)
                    ds_scr[v, rows, :] = _mx(pn * (dp_scr[v, rows, :] - dcol))
                    p_scr[v, rows, :] = _mx(pn)
                    sink_part = sink_part + (e_sink * inv) * dcol
                dsink = dsink - jnp.where(head_lane == h, jnp.sum(sink_part, axis=0, keepdims=True), 0.0)
            dqv = []
            for v in range(2):
                dqv.append(_dot(ds_scr[v], (kr if v else ks)[krows, :]))
                dka[v, krows, :] += _dot_tn(ds_scr[v], qs[n, v])
                dva[v, krows, :] += _dot_tn(p_scr[v], dos[n, v])
            for j in range(N_PAIRS):
                dqn_s[pl.ds(r0, BLK), j * LANES : (j + 1) * LANES] = _unstack_pair(dqv, j, lo)
            return dsink

        dsink = lax.fori_loop(0, nb, block, jnp.zeros((1, LANES), F32))
        dsk_ref[...] += jnp.broadcast_to(dsink, (SUBLANES, LANES))
        for j in range(N_PAIRS):
            cols = slice(j * LANES, (j + 1) * LANES)
            dqn = dqn_s[:, cols]
            qhat = qhat_s[:, cols]
            dqg_ref[:, cols] += _group_rows(dqn * qhat) * Q_SCALE
            dq_ref[:, cols] = _half_rms_bwd(dqn * qg, qhat, rq_s[:, cols], ones).astype(dq_ref.dtype)
        dkn = dka[0] + pltpu.roll(dka[1], HEAD_DIM, 1)
        khat = khat_s[...]
        dkg_ref[...] += _group_rows(dkn * khat)
        dk = _half_rms_bwd(dkn * kg, khat, rk_s[...], ones)
        dv = dva[0] + pltpu.roll(dva[1], HEAD_DIM, 1)
        hp_ref[:, 0:D_KV] = dk[0:BLK]
        hp_ref[:, D_KV : 2 * D_KV] = dv[0:BLK]
        dkv_ref[:, 0:D_KV] = dk[BLK : BLK + tile]
        dkv_ref[:, D_KV : 2 * D_KV] = dv[BLK : BLK + tile]
        hn_ref[:, 0:D_KV] = dk[BLK + tile : ext]
        hn_ref[:, D_KV : 2 * D_KV] = dv[BLK + tile : ext]
        if riding:
            pl.when(i == at_finish)(finish)

    prev, nxt = _halo_specs(tile, seq)
    vec = _full_spec((1, LANES))
    halo = pl.BlockSpec((None, BLK, 2 * D_KV), lambda i: (i, 0, 0))
    return pl.pallas_call(
        body,
        name=name,
        grid=(nt,),
        in_specs=[SMEM_SPEC, _row_spec(tile, D_QKV), prev, nxt, _row_spec(tile, D_ATTN), _row_spec(tile, D_ATTN), vec, vec]
        + [HBM_SPEC] * n_ride,
        out_specs=[
            _row_spec(tile, D_ATTN),
            _row_spec(tile, 2 * D_KV),
            halo,
            halo,
            _full_spec((SUBLANES, D_ATTN)),
            _full_spec((SUBLANES, LANES)),
            _full_spec((SUBLANES, LANES)),
        ]
        + [HBM_SPEC] * n_ride,
        out_shape=[
            jax.ShapeDtypeStruct((seq, D_ATTN), MXU_DTYPE),
            jax.ShapeDtypeStruct((seq, 2 * D_KV), F32),
            jax.ShapeDtypeStruct((nt, BLK, 2 * D_KV), F32),
            jax.ShapeDtypeStruct((nt, BLK, 2 * D_KV), F32),
            jax.ShapeDtypeStruct((SUBLANES, D_ATTN), F32),
            jax.ShapeDtypeStruct((SUBLANES, LANES), F32),
            jax.ShapeDtypeStruct((SUBLANES, LANES), F32),
        ]
        + _landing_shapes(scatter),
        scratch_shapes=[
            pltpu.VMEM((nb, 2, STACK, LANES), MXU_DTYPE),
            pltpu.VMEM((nb, 2, STACK, LANES), MXU_DTYPE),
            pltpu.VMEM((tile, D_ATTN), F32),
            pltpu.VMEM((tile, D_ATTN), F32),
            pltpu.VMEM((ext, LANES), MXU_DTYPE),
            pltpu.VMEM((ext, LANES), MXU_DTYPE),
            pltpu.VMEM((ext, LANES), MXU_DTYPE),
            pltpu.VMEM((ext, LANES), MXU_DTYPE),
            pltpu.VMEM((ext, LANES), F32),
            pltpu.VMEM((ext, LANES), F32),
            pltpu.VMEM((tile, D_ATTN), F32),
            pltpu.VMEM((2, ext, LANES), F32),
            pltpu.VMEM((2, ext, LANES), F32),
            pltpu.VMEM((3, N_HEADS, BLK, 3 * BLK), F32),
            pltpu.VMEM((2, STACK, 3 * BLK), F32),
            pltpu.VMEM((2, STACK, 3 * BLK), F32),
            pltpu.VMEM((2, STACK, 3 * BLK), MXU_DTYPE),
            pltpu.VMEM((2, STACK, 3 * BLK), MXU_DTYPE),
        ]
        + _rider_sems(n_ride),
        compiler_params=_params(("arbitrary",)),
    )(sink, pa, pa, pa, o, do, q_gain2, k_gain2, *scatter)


def _halo_in_specs(tile, nt):
    from_prev = pl.BlockSpec((None, BLK, 2 * D_KV), lambda i: (jnp.maximum(i - 1, 0), 0, 0))
    from_next = pl.BlockSpec((None, BLK, 2 * D_KV), lambda i: (jnp.minimum(i + 1, nt - 1), 0, 0))
    return from_prev, from_next


def _landing_shapes(scatter):
    return [jax.ShapeDtypeStruct((N_DEV,) + b.shape[2:], b.dtype) for b in scatter]


def _rider_sems(n_ride):
    if not n_ride:
        return []
    return [pltpu.SemaphoreType.DMA((7 * n_ride,)), pltpu.SemaphoreType.DMA((7 * n_ride,)), pltpu.SemaphoreType.DMA((n_ride,))]


def _proj_bwd_dx(x, dxn, dq, dkvb, dpb, w_in_t, gain, scale1, name, scatter=()):
    seq, d = x.shape
    tile = min(TOKEN_TILE, seq)
    nt = seq // tile
    n_ride = len(scatter)

    def row(width):
        return _row_spec(tile, width)

    def body(x_ref, dxn_ref, dq_ref, dkvb_ref, dpb_ref, wt_ref, g_ref, s1_ref, *rest):
        i = pl.program_id(0)
        blocks, rest = rest[:n_ride], rest[n_ride:]
        dx_ref, c0_ref, c1_ref = rest[:3]
        landing, sems = rest[3 : 3 + n_ride], rest[3 + n_ride :]
        if n_ride:
            start, finish = _scatter_stages(blocks, landing, *sems)
            at_start, _, at_finish = _rider_steps(nt)
            pl.when(i == at_start)(start)

        @pl.when(i == 0)
        def _():
            c0_ref[...] = jnp.zeros_like(c0_ref)
            c1_ref[...] = jnp.zeros_like(c1_ref)

        dh = (
            _dot(dq_ref[...], wt_ref[0:D_ATTN, :])
            + _dot(dkvb_ref[...], wt_ref[D_ATTN:D_QKV, :])
            + _dot(dpb_ref[...], wt_ref[D_QKV:D_IN, :])
        )
        xv = x_ref[...]
        r = lax.rsqrt(jnp.mean(xv * xv, axis=-1, keepdims=True) + EPS)
        xn = xv * r
        c0_ref[...] += _group_rows(dh)
        c1_ref[...] += _group_rows(dh * xn)
        dxn_ = dh * (g_ref[...] * s1_ref[...])
        dx_ref[...] = dxn_ref[...] + r * (dxn_ - xn * jnp.mean(xn * dxn_, axis=-1, keepdims=True))
        if n_ride:
            pl.when(i == at_finish)(finish)

    vec = _full_spec((1, d))
    return pl.pallas_call(
        body,
        name=name,
        grid=(nt,),
        in_specs=[row(d), row(d), row(D_ATTN), row(2 * D_KV), row(D_REST), _full_spec((D_IN, d)), vec, vec]
        + [HBM_SPEC] * n_ride,
        out_specs=[row(d), _full_spec((SUBLANES, d)), _full_spec((SUBLANES, d))] + [HBM_SPEC] * n_ride,
        out_shape=[
            jax.ShapeDtypeStruct((seq, d), F32),
            jax.ShapeDtypeStruct((SUBLANES, d), F32),
            jax.ShapeDtypeStruct((SUBLANES, d), F32),
        ]
        + _landing_shapes(scatter),
        scratch_shapes=_rider_sems(n_ride),
        compiler_params=_params(("arbitrary",)),
    )(x, dxn, dq, dkvb, dpb, w_in_t, gain, scale1, *scatter)


def _proj_bwd_dw(x, gain, scale1, shift, dq, dkv, halo_prev, halo_next, dpb, name):
    seq, d = x.shape
    tile = min(TOKEN_TILE, seq)
    nt = seq // tile
    assert tile >= 2 * BLK

    def body(x_ref, g_ref, s1_ref, sh_ref, dq_ref, dkv_ref, hn_ref, hp_ref, dpb_ref, dw_ref, dkvb_ref, acc):
        i = pl.program_id(0)

        @pl.when(i == 0)
        def _():
            acc[...] = jnp.zeros_like(acc)

        top = dkv_ref[0:BLK, :] + jnp.where(i > 0, hn_ref[...], 0.0)
        bot = dkv_ref[tile - BLK : tile, :] + jnp.where(i < nt - 1, hp_ref[...], 0.0)
        dkvb_ref[0:BLK, :] = top.astype(dkvb_ref.dtype)
        dkvb_ref[tile - BLK : tile, :] = bot.astype(dkvb_ref.dtype)
        if tile > 2 * BLK:
            dkvb_ref[BLK : tile - BLK, :] = dkv_ref[BLK : tile - BLK, :].astype(dkvb_ref.dtype)
        xv = x_ref[...]
        r = lax.rsqrt(jnp.mean(xv * xv, axis=-1, keepdims=True) + EPS)
        h = _mx((xv * r) * g_ref[...] * s1_ref[...] + sh_ref[...])
        acc[0:D_ATTN, :] += _dot_tn(dq_ref[...], h)
        acc[D_ATTN:D_QKV, :] += _dot_tn(dkvb_ref[...], h)
        acc[D_QKV:D_IN, :] += _dot_tn(dpb_ref[...], h)

        @pl.when(i == nt - 1)
        def _():
            dw_ref[...] = acc[...].astype(dw_ref.dtype)

    from_prev, from_next = _halo_in_specs(tile, nt)
    vec = _full_spec((1, d))
    return pl.pallas_call(
        body,
        name=name,
        grid=(nt,),
        in_specs=[
            _row_spec(tile, d),
            vec,
            vec,
            vec,
            _row_spec(tile, D_ATTN),
            _row_spec(tile, 2 * D_KV),
            from_prev,
            from_next,
            _row_spec(tile, D_REST),
        ],
        out_specs=[_full_spec((D_IN, d)), _row_spec(tile, 2 * D_KV)],
        out_shape=[jax.ShapeDtypeStruct((D_IN, d), jnp.bfloat16), jax.ShapeDtypeStruct((seq, 2 * D_KV), MXU_DTYPE)],
        scratch_shapes=[pltpu.VMEM((D_IN, d), F32)],
        compiler_params=_params(("arbitrary",)),
    )(x, gain, scale1, shift, dq, dkv, halo_next, halo_prev, dpb)


def _adamw_math(w, g, m, v):
    m = ADAM_B1 * m + (1.0 - ADAM_B1) * g
    v = ADAM_B2 * v + (1.0 - ADAM_B2) * (g * g)
    m_hat = m / (1.0 - ADAM_B1**ADAM_STEP)
    v_hat = v / (1.0 - ADAM_B2**ADAM_STEP)
    delta = -ADAM_LR * (m_hat / (jnp.sqrt(v_hat) + ADAM_EPS) + ADAM_WD * w)
    return delta, m, v


def _small_update(gathered, gathered_ws, w, m, v, ws, m_ws, v_ws):
    def body(ga_ref, gws_ref, w_ref, m_ref, v_ref, ws_ref, mws_ref, vws_ref, *outs):
        for src, refs, out in ((ga_ref, (w_ref, m_ref, v_ref), outs[0:4]), (gws_ref, (ws_ref, mws_ref, vws_ref), outs[4:8])):
            g = src[0].astype(F32)
            for j in range(1, N_DEV):
                g = g + src[j].astype(F32)
            out[0][...] = g
            out[1][...], out[2][...], out[3][...] = _adamw_math(refs[0][...], g, refs[1][...], refs[2][...])

    shapes = [jax.ShapeDtypeStruct(w.shape, F32)] * 4 + [jax.ShapeDtypeStruct(ws.shape, F32)] * 4
    return pl.pallas_call(
        body,
        name="small_update",
        in_specs=[VMEM_SPEC] * 8,
        out_specs=[VMEM_SPEC] * 8,
        out_shape=shapes,
        compiler_params=_params(),
    )(gathered, gathered_ws, w, m, v, ws, m_ws, v_ws)


def _ada_update(c_all, d_ada_cols, w, m, v):
    n_layers = w.shape[0]

    def body(c_ref, da_ref, w_ref, m_ref, v_ref, g_ref, d_ref, mo_ref, vo_ref):
        cv = c_ref[...]
        cond = cv * _sigmoid(cv)
        for l in range(n_layers):
            g = lax.dot_general(
                cond, da_ref[l], (((0,), (0,)), ((), ())), preferred_element_type=F32, precision=lax.Precision.HIGHEST
            )
            g_ref[l] = g
            d_ref[l], mo_ref[l], vo_ref[l] = _adamw_math(w_ref[l], g, m_ref[l], v_ref[l])

    return pl.pallas_call(
        body,
        name="ada_update",
        in_specs=[VMEM_SPEC] * 5,
        out_specs=[VMEM_SPEC] * 4,
        out_shape=[jax.ShapeDtypeStruct(w.shape, F32)] * 4,
        compiler_params=_params(),
    )(c_all, d_ada_cols, w, m, v)


def _position():
    return lax.axis_index("x"), lax.axis_index("y"), lax.axis_index("c")


def _flip(pos, k):
    x, y, c = pos
    return (1 - x if k & 4 else x, 1 - y if k & 2 else y, 1 - c if k & 1 else c)


def _index(pos):
    x, y, c = pos
    return 4 * x + 2 * y + c


def _remote(src, dst, send_sem, recv_sem, to):
    return pltpu.make_async_remote_copy(
        src_ref=src, dst_ref=dst, send_sem=send_sem, recv_sem=recv_sem, device_id=to, device_id_type=MESH_ID
    )


def _all_gather_stages(slots, send_sems, recv_sems, sources=None, local_sems=None):
    me = _position()
    sibling = _flip(me, 1)
    others = (4, 2, 6)
    arrays = range(len(slots))

    def copy(t, k, block, to, own=False):
        slot = slots[t](_index(block))
        src = sources[t] if own and sources is not None else slot
        return _remote(src, slot, send_sems.at[7 * t + k], recv_sems.at[7 * t + k], to)

    def first(t):
        return [copy(t, 0, me, sibling, own=True)] + [copy(t, 1 + j, me, _flip(me, f), own=True) for j, f in enumerate(others)]

    def passed(t, j):
        return copy(t, 4 + j, _flip(me, others[j]), sibling)

    def local(t):
        return pltpu.make_async_copy(sources[t], slots[t](_index(me)), local_sems.at[t])

    def start():
        for t in arrays:
            if sources is not None:
                local(t).start()
            for cp in first(t):
                cp.start()

    def forward():
        for j, f in enumerate(others):
            for t in arrays:
                copy(t, 1 + j, _flip(me, f), me).wait_recv()
                passed(t, j).start()

    def finish():
        for t in arrays:
            copy(t, 0, sibling, me).wait_recv()
            for j, f in enumerate(others):
                copy(t, 4 + j, _flip(sibling, f), me).wait_recv()
        for t in arrays:
            for cp in first(t) + [passed(t, j) for j in range(len(others))]:
                cp.wait_send()
            if sources is not None:
                local(t).wait()

    return start, forward, finish


def _two_level_all_gather(slots, send_sems, recv_sems, between=None):
    start, forward, finish = _all_gather_stages(slots, send_sems, recv_sems)
    start()
    if between is not None:
        between()
    forward()
    finish()


def _row_block(ref, rows):
    return lambda j: ref.at[pl.ds(pl.multiple_of(j * rows, 16), rows), :]


def _scatter_stages(blocks, landing, send_sems, recv_sems, local_sems):
    me = _position()
    my = _index(me)
    arrays = range(len(blocks))

    def copy(t, k):
        px, py, pc = to = _flip(me, k)
        return _remote(blocks[t].at[2 * px + py, pc], landing[t].at[my], send_sems.at[7 * t + k - 1], recv_sems.at[7 * t + k - 1], to)

    def arrival(t, k):
        slot = landing[t].at[_index(_flip(me, k))]
        return _remote(slot, slot, send_sems.at[7 * t + k - 1], recv_sems.at[7 * t + k - 1], _flip(me, k))

    def local(t):
        x, y, c = me
        return pltpu.make_async_copy(blocks[t].at[2 * x + y, c], landing[t].at[my], local_sems.at[t])

    def start():
        for t in arrays:
            local(t).start()
            for k in range(1, N_DEV):
                copy(t, k).start()

    def finish():
        for t in arrays:
            for k in range(1, N_DEV):
                arrival(t, k).wait_recv()
        for t in arrays:
            for k in range(1, N_DEV):
                copy(t, k).wait_send()
            local(t).wait()

    return start, finish


def _ada_exchange(c_ref, w_ref, call_ref, parts_ref, sbuf, sem_s1, sem_r1, sem_s2, sem_r2):
    d = c_ref.shape[-1]
    n_layers = w_ref.shape[0]
    me = _position()
    my = _index(me)
    call_ref[my] = jnp.broadcast_to(c_ref[...], (SUBLANES, d))
    mine = call_ref.at[my]
    first = [_remote(mine, mine, sem_s1.at[k - 1], sem_r1.at[k - 1], _flip(me, k)) for k in range(1, N_DEV)]
    for cp in first:
        cp.start()
    for k in range(1, N_DEV):
        theirs = call_ref.at[_index(_flip(me, k))]
        _remote(theirs, theirs, sem_s1.at[k - 1], sem_r1.at[k - 1], _flip(me, k)).wait_recv()
    cv = call_ref[...].reshape(N_DEV * SUBLANES, d)
    cond = cv * _sigmoid(cv)
    for l in range(n_layers):
        rows = jnp.dot(cond, w_ref[l], preferred_element_type=F32, precision=lax.Precision.HIGHEST)
        for b in range(N_DEV):
            sbuf[b, l] = rows[b * SUBLANES : (b + 1) * SUBLANES]
    parts_ref[my] = sbuf[my]
    second = []
    for k in range(1, N_DEV):
        to = _flip(me, k)
        second.append(_remote(sbuf.at[_index(to)], parts_ref.at[my], sem_s2.at[k - 1], sem_r2.at[k - 1], to))
    for cp in second:
        cp.start()
    for k in range(1, N_DEV):
        theirs = parts_ref.at[_index(_flip(me, k))]
        _remote(theirs, theirs, sem_s2.at[k - 1], sem_r2.at[k - 1], _flip(me, k)).wait_recv()
    for cp in first + second:
        cp.wait_send()


def _gather_weights(w_in_t, w_out, c_row, w_ada):
    n_layers, rows_in, d = w_in_t.shape
    rows_out = w_out.shape[1]
    width = w_ada.shape[2]

    def body(wi_ref, wo_ref, c_ref, wa_ref, gi_ref, go_ref, si_ref, so_ref, call_ref, parts_ref, sbuf, send_sems, recv_sems, *ada_sems):
        my = _index(_position())
        si_ref[...] = wi_ref[...].astype(si_ref.dtype)
        so_ref[...] = wo_ref[...].astype(so_ref.dtype)
        gi_ref[pl.ds(pl.multiple_of(my * rows_in, 16), rows_in), :] = si_ref[0]
        go_ref[pl.ds(pl.multiple_of(my * rows_out, 16), rows_out), :] = so_ref[0]
        _two_level_all_gather(
            (_row_block(gi_ref, rows_in), _row_block(go_ref, rows_out)),
            send_sems,
            recv_sems,
            between=functools.partial(_ada_exchange, c_ref, wa_ref, call_ref, parts_ref, sbuf, *ada_sems),
        )

    return pl.pallas_call(
        body,
        name="gather_weights",
        in_specs=[VMEM_SPEC] * 4,
        out_specs=[VMEM_SPEC] * 6,
        out_shape=[
            jax.ShapeDtypeStruct((N_DEV * rows_in, d), MXU_DTYPE),
            jax.ShapeDtypeStruct((N_DEV * rows_out, d), MXU_DTYPE),
            jax.ShapeDtypeStruct(w_in_t.shape, MXU_DTYPE),
            jax.ShapeDtypeStruct(w_out.shape, MXU_DTYPE),
            jax.ShapeDtypeStruct((N_DEV, SUBLANES, d), F32),
            jax.ShapeDtypeStruct((N_DEV, n_layers, SUBLANES, width), F32),
        ],
        scratch_shapes=[
            pltpu.VMEM((N_DEV, n_layers, SUBLANES, width), F32),
            pltpu.SemaphoreType.DMA((14,)),
            pltpu.SemaphoreType.DMA((14,)),
        ]
        + [pltpu.SemaphoreType.DMA((N_DEV - 1,))] * 4,
        compiler_params=_params(),
    )(w_in_t, w_out, c_row, w_ada)


def _gather_small(packed, d_ws, adam=()):
    n_adam = len(adam)

    def body(p_ref, ws_ref, *rest):
        quads = [rest[4 * t : 4 * t + 4] for t in range(n_adam)]
        rest = rest[4 * n_adam :]
        g_ref, gws_ref = rest[:2]
        results = [rest[2 + 3 * t : 5 + 3 * t] for t in range(n_adam)]
        send_sems, recv_sems = rest[2 + 3 * n_adam :]
        my = _index(_position())
        g_ref[my] = p_ref[...]
        gws_ref[my] = ws_ref[...].astype(gws_ref.dtype)

        def updates():
            for (w_ref, gr_ref, m_ref, v_ref), (d_ref, mo_ref, vo_ref) in zip(quads, results):
                d_ref[...], mo_ref[...], vo_ref[...] = _adamw_math(w_ref[...], gr_ref[...], m_ref[...], v_ref[...])

        _two_level_all_gather((lambda j: g_ref.at[j], lambda j: gws_ref.at[j]), send_sems, recv_sems, between=updates)

    return pl.pallas_call(
        body,
        name="gather_small",
        in_specs=[VMEM_SPEC] * (2 + 4 * n_adam),
        out_specs=[VMEM_SPEC] * (2 + 3 * n_adam),
        out_shape=[
            jax.ShapeDtypeStruct((N_DEV,) + packed.shape, F32),
            jax.ShapeDtypeStruct((N_DEV,) + d_ws.shape, jnp.bfloat16),
        ]
        + [jax.ShapeDtypeStruct(q[0].shape, F32) for q in adam for _ in range(3)],
        scratch_shapes=[pltpu.SemaphoreType.DMA((14,)), pltpu.SemaphoreType.DMA((14,))],
        compiler_params=_params(),
    )(packed, d_ws, *[a for q in adam for a in q])


def _scatter_finish(landed, name):
    n = len(landed)

    def body(*refs):
        for src, out in zip(refs[:n], refs[n:]):
            g = src[0].astype(F32)
            for j in range(1, N_DEV):
                g = g + src[j].astype(F32)
            out[...] = g

    return pl.pallas_call(
        body,
        name=name,
        in_specs=[VMEM_SPEC] * n,
        out_specs=[VMEM_SPEC] * n,
        out_shape=[jax.ShapeDtypeStruct(a.shape[1:], F32) for a in landed],
        compiler_params=_params(),
    )(*landed)


def _pack_rows(parts):
    rows, offsets, at = [], [], 0
    for p in parts:
        flat = p.reshape(-1)
        n = -(-flat.shape[0] // (SUBLANES * LANES)) * SUBLANES
        rows.append(jnp.pad(flat, (0, n * LANES - flat.shape[0])).reshape(n, LANES))
        offsets.append(at)
        at += n
    return jnp.concatenate(rows, axis=0), offsets


def _unpack_rows(packed, offsets, shapes):
    out = []
    for off, shape in zip(offsets, shapes):
        size = 1
        for s in shape:
            size *= s
        n = -(-size // (SUBLANES * LANES)) * SUBLANES
        out.append(packed[off : off + n].reshape(-1)[:size].reshape(shape))
    return out


def kernel(x, c, w_ada, b_ada, norm_gain, w_in, q_gain, k_gain, sink, w_s, b_s, w_out, loss_target, m_w_ada, m_b_ada, m_norm_gain, m_w_in, m_q_gain, m_k_gain, m_sink, m_w_s, m_b_s, m_w_out, v_w_ada, v_b_ada, v_norm_gain, v_w_in, v_q_gain, v_k_gain, v_sink, v_w_s, v_b_s, v_w_out):
    seq, d = x.shape[1], x.shape[2]
    n_layers = w_in.shape[0]
    w_cols = w_in.shape[2]
    ada_cols = w_ada.shape[2]
    my = _index(_position())
    xs = x.reshape(seq, d)
    target = loss_target.reshape(seq, d)

    w_in_t0, w_out0, shard_in, shard_out, c_all, ada_parts = _gather_weights(w_in.transpose(0, 2, 1), w_out, c, w_ada)
    w_in_ts, w_outs = [w_in_t0], [w_out0]
    ada = ada_parts[:, :, 0, :].transpose(1, 0, 2).reshape(n_layers, 3 * d) + b_ada
    shift, scale1, gate = ada[:, None, 0:d], 1.0 + ada[:, None, d : 2 * d], ada[:, None, 2 * d : 3 * d]
    gain = norm_gain[:, None, :]

    w_s_m = w_s.astype(MXU_DTYPE)
    w_s_t = w_s_m.transpose(0, 1, 3, 2)
    b_st = jnp.repeat(b_s.transpose(0, 2, 1), HEAD_DIM, axis=2)
    q_gain2 = jnp.tile(q_gain, (1, 2))[:, None, :]
    k_gain2 = jnp.tile(k_gain, (1, 2))[:, None, :]

    xl, saved = xs, []
    for l in range(n_layers):
        last = l == n_layers - 1
        pa, pb = _ln_proj_fwd(xl, gain[l], scale1[l], shift[l], w_in_ts[l], f"ln_proj_fwd_{l}")
        if last:
            o = _attn_fwd(pa, q_gain2[l], k_gain2[l], sink[l], f"attn_fwd_{l}")
        else:
            o, w_in_next, w_out_next = _attn_fwd(
                pa, q_gain2[l], k_gain2[l], sink[l], f"attn_fwd_{l}", gather=(shard_in[l + 1], shard_out[l + 1])
            )
            w_in_ts.append(w_in_next)
            w_outs.append(w_out_next)
        saved.append((xl, pa, pb, o))
        out = _mix_out_fwd(pb, o, xl, gate[l], w_outs[l], w_s_m[l], b_st[l], f"mix_out_fwd_{l}", target if last else None)
        if last:
            dx, sq_err = out
        else:
            xl = out

    g_w_in, g_w_out, small, d_ada_rows = [None] * n_layers, [None] * n_layers, [None] * n_layers, [None] * n_layers
    waiting = []
    for l in reversed(range(n_layers)):
        x_l, pa, pb, o = saved[l]
        dpb, do, dw_out, d_gate8, d_ws, d_bs = _mix_out_bwd(
            dx, pb, o, gate[l], w_outs[l], w_s_m[l], w_s_t[l], b_st[l], f"mix_out_bwd_{l}"
        )
        riding = waiting + [(g_w_out, l, dw_out.reshape(4, 2, D_MIX // N_DEV, d))]
        attn = _attn_bwd(pa, o, do, q_gain2[l], k_gain2[l], sink[l], f"attn_bwd_{l}", scatter=tuple(b for _, _, b in riding))
        dq, dkv, halo_prev, halo_next, d_qg, d_kg, d_sk = attn[:7]
        for (dest, layer, _), total in zip(riding, _scatter_finish(attn[7:], f"scatter_finish_{l}")):
            dest[layer] = total.transpose(1, 0) if dest is g_w_in else total
        dw_in_t, dkvb = _proj_bwd_dw(x_l, gain[l], scale1[l], shift[l], dq, dkv, halo_prev, halo_next, dpb, f"proj_bwd_dw_{l}")
        blocks_in = dw_in_t.reshape(4, 2, w_cols, d)
        waiting = [(g_w_in, l, blocks_in)] if l > 0 else []
        dxs = _proj_bwd_dx(
            x_l, dx, dq, dkvb, dpb, w_in_ts[l], gain[l], scale1[l], f"proj_bwd_dx_{l}", scatter=() if l > 0 else (blocks_in,)
        )
        dx, c0, c1 = dxs[:3]
        if l == 0:
            g_w_in[l] = _scatter_finish(dxs[3:], "scatter_finish_in_0")[0].transpose(1, 0)
        c0s, c1s = c0.sum(axis=0), c1.sum(axis=0)
        d_ada_rows[l] = jnp.concatenate([c0s, norm_gain[l] * c1s, d_gate8.sum(axis=0)])
        small[l] = (
            scale1[l, 0] * c1s,
            d_qg.sum(axis=0).reshape(N_HEADS, HEAD_DIM).sum(axis=0),
            d_kg.sum(axis=0).reshape(2, HEAD_DIM).sum(axis=0),
            d_sk[0, 0:N_HEADS],
            d_bs.reshape(BLK, N_GROUPS, HEAD_DIM).sum(axis=2).transpose(1, 0),
            d_ws,
        )

    names = ("norm_gain", "q_gain", "k_gain", "sink", "b_s")
    stacked = [jnp.stack([small[l][t] for l in range(n_layers)]) for t in range(len(names))]
    d_ada = jnp.stack(d_ada_rows)
    packed, offsets = _pack_rows(stacked + [d_ada, sq_err[0, 0:1]])
    d_ws = jnp.stack([small[l][len(names)] for l in range(n_layers)]).reshape(-1, LANES)
    g_w_in, g_w_out = jnp.stack(g_w_in), jnp.stack(g_w_out)
    gathered, gathered_ws, *upd = _gather_small(
        packed, d_ws, adam=((w_in, g_w_in, m_w_in, v_w_in), (w_out, g_w_out, m_w_out, v_w_out))
    )
    upd_in, upd_out = upd[0:3], upd[3:6]
    no_weight = jnp.zeros((1,), F32)
    weights = (norm_gain, q_gain, k_gain, sink, b_s, b_ada, no_weight)
    moments_m = (m_norm_gain, m_q_gain, m_k_gain, m_sink, m_b_s, m_b_ada, no_weight)
    moments_v = (v_norm_gain, v_q_gain, v_k_gain, v_sink, v_b_s, v_b_ada, no_weight)
    w_pack, _ = _pack_rows(weights)
    m_pack, _ = _pack_rows(moments_m)
    v_pack, _ = _pack_rows(moments_v)
    shapes = [w.shape for w in weights]
    flat_ws = lambda a: a.reshape(-1, LANES)
    updated = _small_update(gathered, gathered_ws, w_pack, m_pack, v_pack, flat_ws(w_s), flat_ws(m_w_s), flat_ws(v_w_s))
    g_small, d_small, m_small, v_small = (_unpack_rows(p, offsets, shapes) for p in updated[0:4])
    ws_small = [p.reshape(w_s.shape) for p in updated[4:8]]
    loss = g_small[-1][0] * (0.5 / d)

    ada_off = offsets[-2]
    ada_n = -(-n_layers * 3 * d // (SUBLANES * LANES)) * SUBLANES
    d_ada_all = gathered[:, ada_off : ada_off + ada_n].reshape(N_DEV, -1)[:, : n_layers * 3 * d].reshape(N_DEV, n_layers, 3 * d)
    d_ada_cols = lax.dynamic_slice_in_dim(d_ada_all, my * ada_cols, ada_cols, axis=2)
    g_w_ada, *upd_ada = _ada_update(c_all[:, 0, :], d_ada_cols.transpose(1, 0, 2), w_ada, m_w_ada, v_w_ada)

    def ordered(ada_, in_, out_, small_, ws):
        ng, qg, kg, sk, bs, ba, _ = small_
        return (ada_, ba, ng, in_, qg, kg, sk, ws, bs, out_)

    grads = ordered(g_w_ada, g_w_in, g_w_out, g_small, ws_small[0])
    deltas = ordered(upd_ada[0], upd_in[0], upd_out[0], d_small, ws_small[1])
    new_m = ordered(upd_ada[1], upd_in[1], upd_out[1], m_small, ws_small[2])
    new_v = ordered(upd_ada[2], upd_in[2], upd_out[2], v_small, ws_small[3])
    return (loss, dx.reshape(x.shape), *grads, *deltas, *new_m, *new_v)
```

```python
import functools

import jax
import jax.numpy as jnp
from jax import lax
from jax.experimental import pallas as pl
from jax.experimental.pallas import tpu as pltpu

F32 = jnp.float32
MXU_DTYPE = jnp.bfloat16
MESH_ID = pl.DeviceIdType.MESH

N_DEV = 8
HEAD_DIM = 64
N_HEADS = 8
Q_PER_KV = 4
D_ATTN = 512
D_KV = 128
D_GM = 512
N_GROUPS = 8
D_MIX = D_ATTN + D_GM
BLK = 128
LANES = 128
SUBLANES = 8
N_PAIRS = D_ATTN // LANES
D_QKV = D_ATTN + 2 * D_KV
D_REST = D_ATTN + 3 * D_GM
D_IN = D_QKV + D_REST
EPS = 1e-6
NEG_INF = -1e30
ALIBI_SLOPES = tuple(2.0 ** (-8.0 * (h + 1) / N_HEADS) for h in range(N_HEADS))
Q_SCALE = 1.0 / 8.0

ADAM_LR = 0.001
ADAM_B1 = 0.9
ADAM_B2 = 0.999
ADAM_EPS = 1e-08
ADAM_WD = 0.01
ADAM_STEP = 10

TOKEN_TILE = 512
VMEM_LIMIT_BYTES = 56 * 1024 * 1024


def _params(semantics=None):
    return pltpu.CompilerParams(dimension_semantics=semantics, vmem_limit_bytes=VMEM_LIMIT_BYTES)


def _dot(a, b):
    return jnp.dot(a, b, preferred_element_type=F32)


def _dot_nt(a, b):
    return lax.dot_general(a, b, (((1,), (1,)), ((), ())), preferred_element_type=F32)


def _dot_tn(a, b):
    return lax.dot_general(a, b, (((0,), (0,)), ((), ())), preferred_element_type=F32)


def _mx(v):
    return v.astype(MXU_DTYPE)


def _lane_lo(rows):
    return lax.broadcasted_iota(jnp.int32, (rows, LANES), 1) < HEAD_DIM


def _half_ones(width=LANES):
    group_bits = HEAD_DIM.bit_length() - 1
    r = jnp.right_shift(lax.broadcasted_iota(jnp.int32, (width, width), 0), group_bits)
    c = jnp.right_shift(lax.broadcasted_iota(jnp.int32, (width, width), 1), group_bits)
    return jnp.where(r == c, 1.0, 0.0).astype(jnp.bfloat16)


WIDE = 2 * LANES


def _half_sum(v, ones):
    p1 = v.astype(jnp.bfloat16)
    p2 = (v - p1.astype(F32)).astype(jnp.bfloat16)
    return _dot(p1, ones) + _dot(p2, ones)


def _half_rms(v, ones):
    r = lax.rsqrt(_half_sum(v * v, ones) * (1.0 / HEAD_DIM) + EPS)
    return v * r, r


def _half_rms_bwd(dy, vhat, r, ones):
    return r * (dy - vhat * (_half_sum(vhat * dy, ones) * (1.0 / HEAD_DIM)))


def _group_rows(v):
    rows, n = v.shape
    return v.reshape(rows // SUBLANES, SUBLANES, n).sum(axis=0)


def _sigmoid(v):
    return 1.0 / (1.0 + jnp.exp(-v))


ROW_CHUNK = 32
VARIANT_HEADS = ((0, 2, 5, 7), (1, 3, 4, 6))
HEAD_SLOT = {h: (v, s) for v, heads in enumerate(VARIANT_HEADS) for s, h in enumerate(heads)}
STACK = Q_PER_KV * BLK


def _fill_attn_bias(bias_s):
    qi = lax.broadcasted_iota(jnp.int32, (BLK, 3 * BLK), 0)
    ci = lax.broadcasted_iota(jnp.int32, (BLK, 3 * BLK), 1)
    dist = jnp.abs(ci - BLK - qi)
    distf = dist.astype(F32)
    window = dist <= BLK
    for kind, seen in enumerate((window & (ci >= BLK), window, window & (ci < 2 * BLK))):
        for h in range(N_HEADS):
            bias_s[kind, h] = jnp.where(seen, -(ALIBI_SLOPES[h] * distf), NEG_INF)


def _block_kind(block, seq):
    assert seq >= 2 * BLK
    return jnp.where(block == 0, 0, jnp.where(block == seq // BLK - 1, 2, 1))


def _stage_queries(qn, lo_t, j, nb, qs):
    for a in range(2):
        v, slot = HEAD_SLOT[2 * j + a]
        qm = _mx(jnp.where(lo_t, qn, 0.0) if a == 0 else jnp.where(lo_t, 0.0, qn))
        for n in range(nb):
            qs[n, v, slot * BLK : (slot + 1) * BLK, :] = qm[n * BLK : (n + 1) * BLK]


def _unstack_pair(stacked, j, lo):
    (v0, s0), (v1, s1) = HEAD_SLOT[2 * j], HEAD_SLOT[2 * j + 1]
    return jnp.where(lo, stacked[v0][s0 * BLK : (s0 + 1) * BLK], stacked[v1][s1 * BLK : (s1 + 1) * BLK])


def _stage_keys(kvp_ref, qkv_ref, kvn_ref, kg, ones, tile, ks, kr, vs, vr, khat_s=None, rk_s=None):
    pieces = (
        (0, BLK, kvp_ref[:, 0:D_KV], kvp_ref[:, D_KV : 2 * D_KV]),
        (BLK, tile, qkv_ref[:, D_ATTN : D_ATTN + D_KV], qkv_ref[:, D_ATTN + D_KV : D_QKV]),
        (BLK + tile, BLK, kvn_ref[:, 0:D_KV], kvn_ref[:, D_KV : 2 * D_KV]),
    )
    for r0, n, k, v in pieces:
        khat, rk = _half_rms(k, ones)
        kn = khat * kg
        ks[r0 : r0 + n, :] = _mx(kn)
        kr[r0 : r0 + n, :] = _mx(pltpu.roll(kn, HEAD_DIM, 1))
        vs[r0 : r0 + n, :] = _mx(v)
        vr[r0 : r0 + n, :] = _mx(pltpu.roll(v, HEAD_DIM, 1))
        if khat_s is not None:
            khat_s[r0 : r0 + n, :] = khat
            rk_s[r0 : r0 + n, :] = rk


def _halo_specs(tile, seq):
    nb = tile // BLK
    last = seq // BLK - 1
    kv_col = D_ATTN // (2 * D_KV)
    prev = pl.BlockSpec((BLK, 2 * D_KV), lambda i: (jnp.maximum(i * nb - 1, 0), kv_col))
    nxt = pl.BlockSpec((BLK, 2 * D_KV), lambda i: (jnp.minimum((i + 1) * nb, last), kv_col))
    return prev, nxt


def _row_spec(tile, width):
    return pl.BlockSpec((tile, width), lambda i: (i, 0))


def _full_spec(shape):
    nd = len(shape)
    return pl.BlockSpec(shape, lambda i: (0,) * nd)


SMEM_SPEC = pl.BlockSpec(memory_space=pltpu.SMEM)
VMEM_SPEC = pl.BlockSpec(memory_space=pltpu.VMEM)
HBM_SPEC = pl.BlockSpec(memory_space=pltpu.HBM)


def _ln_proj_fwd(x, gain, scale1, shift, w_in_t, name):
    seq, d = x.shape
    tile = min(TOKEN_TILE, seq)

    def body(x_ref, g_ref, s1_ref, sh_ref, wt_ref, pa_ref, pb_ref):
        xv = x_ref[...]
        r = lax.rsqrt(jnp.mean(xv * xv, axis=-1, keepdims=True) + EPS)
        h = _mx((xv * r) * g_ref[...] * s1_ref[...] + sh_ref[...])
        pa_ref[...] = _dot_nt(h, wt_ref[0:D_QKV, :])
        pb_ref[...] = _dot_nt(h, wt_ref[D_QKV:D_IN, :])

    vec = _full_spec((1, d))
    return pl.pallas_call(
        body,
        name=name,
        grid=(seq // tile,),
        in_specs=[_row_spec(tile, d), vec, vec, vec, _full_spec((D_IN, d))],
        out_specs=[_row_spec(tile, D_QKV), _row_spec(tile, D_REST)],
        out_shape=[jax.ShapeDtypeStruct((seq, D_QKV), F32), jax.ShapeDtypeStruct((seq, D_REST), F32)],
        compiler_params=_params(("parallel",)),
    )(x, gain, scale1, shift, w_in_t)


def _rider_steps(nt):
    return 0, (2 * nt) // 3, nt - 1


def _attn_fwd(pa, q_gain2, k_gain2, sink, name, gather=None):
    seq = pa.shape[0]
    tile = min(TOKEN_TILE, seq)
    nb = tile // BLK
    nt = seq // tile
    ext = tile + 2 * BLK
    riding = gather is not None

    def body(sink_ref, qkv_ref, kvp_ref, kvn_ref, qg_ref, kg_ref, *rest):
        i = pl.program_id(0)
        if riding:
            shard_in, shard_out, o_ref, full_in, full_out = rest[0:5]
            qs, ks, kr, vs, vr, bias_s, s_scr, p_scr, inv_scr, send_sems, recv_sems, local_sems = rest[5:]
            start, forward, finish = _all_gather_stages(
                (_row_block(full_in, shard_in.shape[0]), _row_block(full_out, shard_out.shape[0])),
                send_sems,
                recv_sems,
                sources=(shard_in, shard_out),
                local_sems=local_sems,
            )
            at_start, at_forward, at_finish = _rider_steps(nt)
            pl.when(i == at_start)(start)
        else:
            o_ref, qs, ks, kr, vs, vr, bias_s, s_scr, p_scr, inv_scr = rest

        @pl.when(i == 0)
        def _():
            _fill_attn_bias(bias_s)

        ones = _half_ones()
        lo = _lane_lo(BLK)
        lo_t = _lane_lo(tile)
        _stage_keys(kvp_ref, qkv_ref, kvn_ref, kg_ref[...], ones, tile, ks, kr, vs, vr)
        for j in range(N_PAIRS):
            qhat, _ = _half_rms(qkv_ref[:, j * LANES : (j + 1) * LANES], ones)
            _stage_queries(qhat * (qg_ref[...] * Q_SCALE), lo_t, j, nb, qs)

        def block(n, carry):
            r0 = pl.multiple_of(n * BLK, BLK)
            krows = pl.ds(r0, 3 * BLK)
            kind = _block_kind(i * nb + n, seq)
            for v in range(2):
                s_scr[v] = _dot_nt(qs[n, v], (kr if v else ks)[krows, :])
            for h in range(N_HEADS):
                v, slot = HEAD_SLOT[h]
                sink_h = sink_ref[h]
                for rc in range(0, BLK, ROW_CHUNK):
                    rows = slice(slot * BLK + rc, slot * BLK + rc + ROW_CHUNK)
                    s = s_scr[v, rows, :] + bias_s[kind, h, rc : rc + ROW_CHUNK, :]
                    m = jnp.maximum(jnp.max(s, axis=-1, keepdims=True), sink_h)
                    p = jnp.exp(s - m)
                    total = jnp.sum(p, axis=-1, keepdims=True) + jnp.exp(sink_h - m)
                    p_scr[v, rows, :] = _mx(p)
                    inv_scr[v, rows, :] = jnp.broadcast_to(1.0 / total, (ROW_CHUNK, LANES))
            outs = [_dot(p_scr[v], (vr if v else vs)[krows, :]) * inv_scr[v] for v in range(2)]
            for j in range(N_PAIRS):
                o_ref[pl.ds(r0, BLK), j * LANES : (j + 1) * LANES] = _unstack_pair(outs, j, lo)
            return carry

        lax.fori_loop(0, nb, block, 0)
        if riding:
            pl.when(i == at_forward)(forward)
            pl.when(i == at_finish)(finish)

    prev, nxt = _halo_specs(tile, seq)
    vec = _full_spec((1, LANES))
    in_specs = [SMEM_SPEC, _row_spec(tile, D_QKV), prev, nxt, vec, vec]
    out_specs = [_row_spec(tile, D_ATTN)]
    out_shape = [jax.ShapeDtypeStruct((seq, D_ATTN), F32)]
    scratch = [
        pltpu.VMEM((nb, 2, STACK, LANES), MXU_DTYPE),
        pltpu.VMEM((ext, LANES), MXU_DTYPE),
        pltpu.VMEM((ext, LANES), MXU_DTYPE),
        pltpu.VMEM((ext, LANES), MXU_DTYPE),
        pltpu.VMEM((ext, LANES), MXU_DTYPE),
        pltpu.VMEM((3, N_HEADS, BLK, 3 * BLK), F32),
        pltpu.VMEM((2, STACK, 3 * BLK), F32),
        pltpu.VMEM((2, STACK, 3 * BLK), MXU_DTYPE),
        pltpu.VMEM((2, STACK, LANES), F32),
    ]
    extra = ()
    if riding:
        extra = tuple(gather)
        in_specs += [HBM_SPEC] * 2
        out_specs += [HBM_SPEC] * 2
        out_shape += [jax.ShapeDtypeStruct((N_DEV * g.shape[0], g.shape[1]), g.dtype) for g in gather]
        scratch += [pltpu.SemaphoreType.DMA((14,)), pltpu.SemaphoreType.DMA((14,)), pltpu.SemaphoreType.DMA((2,))]
    out = pl.pallas_call(
        body,
        name=name,
        grid=(nt,),
        in_specs=in_specs,
        out_specs=out_specs,
        out_shape=out_shape,
        scratch_shapes=scratch,
        compiler_params=_params(("arbitrary",)),
    )(sink, pa, pa, pa, q_gain2, k_gain2, *extra)
    return out if riding else out[0]


def _mix_out_fwd(pb, o, x, gate, w_out, w_s, b_st, name, target=None):
    seq, d = x.shape
    tile = min(TOKEN_TILE, seq)
    nb = tile // BLK
    with_loss = target is not None

    def body(pb_ref, o_ref, x_ref, gate_ref, wo_ref, ws_ref, bs_ref, *rest):
        if with_loss:
            t_ref, xo_ref, acc_ref, y_s, vn_s = rest

            @pl.when(pl.program_id(0) == 0)
            def _():
                acc_ref[...] = jnp.zeros_like(acc_ref)
        else:
            xo_ref, y_s, vn_s = rest
        ones = _half_ones(WIDE)
        lo = _lane_lo(BLK)
        ga = pb_ref[:, 0:D_ATTN]
        y_s[:, 0:D_ATTN] = _mx(o_ref[...] * (ga * _sigmoid(ga)))
        for j in range(D_GM // WIDE):
            vhat, _ = _half_rms(pb_ref[:, 2 * D_GM + j * WIDE : 2 * D_GM + (j + 1) * WIDE], ones)
            vn_s[:, j * WIDE : (j + 1) * WIDE] = _mx(vhat)

        def chunk(n, carry):
            rows = pl.ds(pl.multiple_of(n * BLK, BLK), BLK)
            for j in range(N_PAIRS):
                cols = slice(j * LANES, (j + 1) * LANES)
                vn = vn_s[rows, cols]
                sv = jnp.where(lo, _dot(ws_ref[2 * j], vn), _dot(ws_ref[2 * j + 1], vn)) + bs_ref[:, cols]
                u = pb_ref[rows, D_ATTN + j * LANES : D_ATTN + (j + 1) * LANES]
                gg = pb_ref[rows, D_ATTN + 2 * D_GM + j * LANES : D_ATTN + 2 * D_GM + (j + 1) * LANES]
                y_s[rows, D_ATTN + j * LANES : D_ATTN + (j + 1) * LANES] = _mx((u * sv) * (gg * _sigmoid(gg)))
            return carry

        lax.fori_loop(0, nb, chunk, 0)
        y = x_ref[...] + gate_ref[...] * _dot(y_s[...], wo_ref[...])
        if with_loss:
            e = y - t_ref[...]
            xo_ref[...] = e * (1.0 / d)
            acc_ref[...] += jnp.sum(jnp.sum(e * e, axis=-1, keepdims=True), axis=0, keepdims=True)
        else:
            xo_ref[...] = y

    row = _row_spec(tile, d)
    acc_shape = (SUBLANES, LANES)
    return pl.pallas_call(
        body,
        name=name,
        grid=(seq // tile,),
        in_specs=[
            _row_spec(tile, D_REST),
            _row_spec(tile, D_ATTN),
            row,
            _full_spec((1, d)),
            _full_spec((D_MIX, d)),
            _full_spec((N_GROUPS, BLK, BLK)),
            _full_spec((BLK, D_GM)),
        ]
        + ([row] if with_loss else []),
        out_specs=[row, _full_spec(acc_shape)] if with_loss else row,
        out_shape=[jax.ShapeDtypeStruct((seq, d), F32), jax.ShapeDtypeStruct(acc_shape, F32)]
        if with_loss
        else jax.ShapeDtypeStruct((seq, d), F32),
        scratch_shapes=[pltpu.VMEM((tile, D_MIX), MXU_DTYPE), pltpu.VMEM((tile, D_GM), MXU_DTYPE)],
        compiler_params=_params(("arbitrary",) if with_loss else ("parallel",)),
    )(pb, o, x, gate, w_out, w_s, b_st, *([target] if with_loss else []))


def _mix_out_bwd(dxn, pb, o, gate, w_out, w_s, w_s_t, b_st, name):
    seq, d = dxn.shape
    tile = min(TOKEN_TILE, seq)
    nb = tile // BLK
    nt = seq // tile

    def body(dxn_ref, pb_ref, o_ref, gate_ref, wo_ref, ws_ref, wst_ref, bs_ref,
             dpb_ref, do_ref, dwo_ref, dgate_ref, dws_ref, dbs_ref, g_ref, y_s, dy_s, vn_s, rv_s, vnb_s, sv_s, dsv_s, dvn_s):
        @pl.when(pl.program_id(0) == 0)
        def _():
            g_ref[...] = jnp.zeros_like(g_ref)
            dws_ref[...] = jnp.zeros_like(dws_ref)
            dbs_ref[...] = jnp.zeros_like(dbs_ref)

        ones = _half_ones(WIDE)
        lo = _lane_lo(BLK)
        c_u = slice(D_ATTN, D_ATTN + D_GM)
        c_vg = slice(D_ATTN + D_GM, D_ATTN + 2 * D_GM)
        c_gg = slice(D_ATTN + 2 * D_GM, D_REST)
        dxv = dxn_ref[...]
        dy_s[...] = _dot_nt(_mx(dxv * gate_ref[...]), wo_ref[...])
        ga = pb_ref[:, 0:D_ATTN]
        sig = _sigmoid(ga)
        sil = ga * sig
        ov = o_ref[...]
        y_s[:, 0:D_ATTN] = _mx(ov * sil)
        da = dy_s[:, 0:D_ATTN]
        do_ref[...] = da * sil
        dpb_ref[:, 0:D_ATTN] = (da * ov * (sig * (1.0 + ga * (1.0 - sig)))).astype(dpb_ref.dtype)
        for j in range(D_GM // WIDE):
            cols = slice(j * WIDE, (j + 1) * WIDE)
            vhat, rv = _half_rms(pb_ref[:, 2 * D_GM + j * WIDE : 2 * D_GM + (j + 1) * WIDE], ones)
            vn_s[:, cols] = vhat
            rv_s[:, cols] = rv
            vnb_s[:, cols] = _mx(vhat)

        def spatial_fwd(n, carry):
            rows = pl.ds(pl.multiple_of(n * BLK, BLK), BLK)
            for j in range(N_PAIRS):
                cols = slice(j * LANES, (j + 1) * LANES)
                vn = vnb_s[rows, cols]
                sv_s[rows, cols] = jnp.where(lo, _dot(ws_ref[2 * j], vn), _dot(ws_ref[2 * j + 1], vn)) + bs_ref[:, cols]
            return carry

        lax.fori_loop(0, nb, spatial_fwd, 0)

        def gating(n, carry):
            rows = pl.ds(pl.multiple_of(n * BLK, BLK), BLK)
            sv = sv_s[rows, :]
            u = pb_ref[rows, c_u]
            gg = pb_ref[rows, c_gg]
            sg = _sigmoid(gg)
            silg = gg * sg
            m0 = u * sv
            y_s[rows, D_ATTN:D_MIX] = _mx(m0 * silg)
            dm = dy_s[rows, D_ATTN:D_MIX]
            dm0 = dm * silg
            dpb_ref[rows, c_gg] = (dm * m0 * (sg * (1.0 + gg * (1.0 - sg)))).astype(dpb_ref.dtype)
            dpb_ref[rows, c_u] = (dm0 * sv).astype(dpb_ref.dtype)
            dsv = dm0 * u
            dsv_s[rows, :] = _mx(dsv)
            dbs_ref[...] += dsv
            return carry

        lax.fori_loop(0, nb, gating, 0)

        def spatial_bwd(n, carry):
            rows = pl.ds(pl.multiple_of(n * BLK, BLK), BLK)
            for j in range(N_PAIRS):
                cols = slice(j * LANES, (j + 1) * LANES)
                dsv = dsv_s[rows, cols]
                dvn_s[rows, cols] = jnp.where(lo, _dot(wst_ref[2 * j], dsv), _dot(wst_ref[2 * j + 1], dsv))
            return carry

        lax.fori_loop(0, nb, spatial_bwd, 0)
        zero = jnp.zeros((BLK, LANES), MXU_DTYPE)
        for j in range(N_PAIRS):
            cols = slice(j * LANES, (j + 1) * LANES)
            chunks = [dsv_s[n * BLK : (n + 1) * BLK, cols] for n in range(nb)]
            vn_all = jnp.concatenate([vnb_s[n * BLK : (n + 1) * BLK, cols] for n in range(nb)], axis=1)
            dws_ref[2 * j] += _dot_nt(jnp.concatenate([jnp.where(lo, c, zero) for c in chunks], axis=1), vn_all)
            dws_ref[2 * j + 1] += _dot_nt(jnp.concatenate([jnp.where(lo, zero, c) for c in chunks], axis=1), vn_all)
        for j in range(D_GM // WIDE):
            cols = slice(j * WIDE, (j + 1) * WIDE)
            dpb_ref[:, D_ATTN + D_GM + j * WIDE : D_ATTN + D_GM + (j + 1) * WIDE] = _half_rms_bwd(
                dvn_s[:, cols], vn_s[:, cols], rv_s[:, cols], ones
            ).astype(dpb_ref.dtype)
        g_ref[...] += _dot_tn(y_s[...], _mx(dxv))

        @pl.when(pl.program_id(0) == nt - 1)
        def _():
            gv = g_ref[...]
            dwo_ref[...] = (gv * gate_ref[...]).astype(dwo_ref.dtype)
            dgate_ref[...] = _group_rows(gv * wo_ref[...].astype(F32))

    return pl.pallas_call(
        body,
        name=name,
        grid=(seq // tile,),
        in_specs=[
            _row_spec(tile, d),
            _row_spec(tile, D_REST),
            _row_spec(tile, D_ATTN),
            _full_spec((1, d)),
            _full_spec((D_MIX, d)),
            _full_spec((N_GROUPS, BLK, BLK)),
            _full_spec((N_GROUPS, BLK, BLK)),
            _full_spec((BLK, D_GM)),
        ],
        out_specs=[
            _row_spec(tile, D_REST),
            _row_spec(tile, D_ATTN),
            _full_spec((D_MIX, d)),
            _full_spec((SUBLANES, d)),
            _full_spec((N_GROUPS, BLK, BLK)),
            _full_spec((BLK, D_GM)),
        ],
        out_shape=[
            jax.ShapeDtypeStruct((seq, D_REST), MXU_DTYPE),
            jax.ShapeDtypeStruct((seq, D_ATTN), F32),
            jax.ShapeDtypeStruct((D_MIX, d), jnp.bfloat16),
            jax.ShapeDtypeStruct((SUBLANES, d), F32),
            jax.ShapeDtypeStruct((N_GROUPS, BLK, BLK), F32),
            jax.ShapeDtypeStruct((BLK, D_GM), F32),
        ],
        scratch_shapes=[
            pltpu.VMEM((D_MIX, d), F32),
            pltpu.VMEM((tile, D_MIX), MXU_DTYPE),
            pltpu.VMEM((tile, D_MIX), F32),
            pltpu.VMEM((tile, D_GM), F32),
            pltpu.VMEM((tile, D_GM), F32),
            pltpu.VMEM((tile, D_GM), MXU_DTYPE),
            pltpu.VMEM((tile, D_GM), F32),
            pltpu.VMEM((tile, D_GM), MXU_DTYPE),
            pltpu.VMEM((tile, D_GM), F32),
        ],
        compiler_params=_params(("arbitrary",)),
    )(dxn, pb, o, gate, w_out, w_s, w_s_t, b_st)


def _attn_bwd(pa, o, do, q_gain2, k_gain2, sink, name, scatter=()):
    seq = pa.shape[0]
    tile = min(TOKEN_TILE, seq)
    nb = tile // BLK
    nt = seq // tile
    ext = tile + 2 * BLK
    n_ride = len(scatter)
    riding = n_ride > 0

    def body(sink_ref, qkv_ref, kvp_ref, kvn_ref, o_ref, do_ref, qg_ref, kg_ref, *rest):
        i = pl.program_id(0)
        blocks, rest = rest[:n_ride], rest[n_ride:]
        dq_ref, dkv_ref, hp_ref, hn_ref, dqg_ref, dkg_ref, dsk_ref = rest[:7]
        landing, rest = rest[7 : 7 + n_ride], rest[7 + n_ride :]
        (qs, dos, qhat_s, rq_s, ks, kr, vs, vr, khat_s, rk_s, dqn_s, dka, dva, bias_s, s_scr, dp_scr, p_scr, ds_scr) = rest[:18]
        if riding:
            start, finish = _scatter_stages(blocks, landing, *rest[18:])
            at_start, _, at_finish = _rider_steps(nt)
            pl.when(i == at_start)(start)

        @pl.when(i == 0)
        def _():
            dqg_ref[...] = jnp.zeros_like(dqg_ref)
            dkg_ref[...] = jnp.zeros_like(dkg_ref)
            dsk_ref[...] = jnp.zeros_like(dsk_ref)
            _fill_attn_bias(bias_s)

        ones = _half_ones()
        lo = _lane_lo(BLK)
        lo_t = _lane_lo(tile)
        lo_c = _lane_lo(ROW_CHUNK)
        qg = qg_ref[...] * Q_SCALE
        kg = kg_ref[...]
        _stage_keys(kvp_ref, qkv_ref, kvn_ref, kg, ones, tile, ks, kr, vs, vr, khat_s, rk_s)
        for j in range(N_PAIRS):
            cols = slice(j * LANES, (j + 1) * LANES)
            qhat, rq = _half_rms(qkv_ref[:, cols], ones)
            qhat_s[:, cols] = qhat
            rq_s[:, cols] = rq
            _stage_queries(qhat * qg, lo_t, j, nb, qs)
            _stage_queries(do_ref[:, cols], lo_t, j, nb, dos)
        dka[...] = jnp.zeros_like(dka)
        dva[...] = jnp.zeros_like(dva)
        head_lane = lax.broadcasted_iota(jnp.int32, (1, LANES), 1)

        def block(n, dsink):
            r0 = pl.multiple_of(n * BLK, BLK)
            krows = pl.ds(r0, 3 * BLK)
            kind = _block_kind(i * nb + n, seq)
            for v in range(2):
                s_scr[v] = _dot_nt(qs[n, v], (kr if v else ks)[krows, :])
                dp_scr[v] = _dot_nt(dos[n, v], (vr if v else vs)[krows, :])
            for h in range(N_HEADS):
                v, slot = HEAD_SLOT[h]
                j, a = divmod(h, 2)
                cols = slice(j * LANES, (j + 1) * LANES)
                sink_h = sink_ref[h]
                sink_part = jnp.zeros((ROW_CHUNK, 1), F32)
                for rc in range(0, BLK, ROW_CHUNK):
                    rows = slice(slot * BLK + rc, slot * BLK + rc + ROW_CHUNK)
                    trows = pl.ds(pl.multiple_of(r0 + rc, ROW_CHUNK), ROW_CHUNK)
                    s = s_scr[v, rows, :] + bias_s[kind, h, rc : rc + ROW_CHUNK, :]
                    m = jnp.maximum(jnp.max(s, axis=-1, keepdims=True), sink_h)
                    p = jnp.exp(s - m)
                    e_sink = jnp.exp(sink_h - m)
                    inv = 1.0 / (jnp.sum(p, axis=-1, keepdims=True) + e_sink)
                    pn = p * inv
                    prod = do_ref[trows, cols] * o_ref[trows, cols]
                    prod = jnp.where(lo_c, prod, 0.0) if a == 0 else jnp.where(lo_c, 0.0, prod)
                    dcol = jnp.sum(prod, axis=-1, keepdims=True)
                    ds_scr[v, rows, :] = _mx(pn * (dp_scr[v, rows, :] - dcol))
                    p_scr[v, rows, :] = _mx(pn)
                    sink_part = sink_part + (e_sink * inv) * dcol
                dsink = dsink - jnp.where(head_lane == h, jnp.sum(sink_part, axis=0, keepdims=True), 0.0)
            dqv = []
            for v in range(2):
                dqv.append(_dot(ds_scr[v], (kr if v else ks)[krows, :]))
                dka[v, krows, :] += _dot_tn(ds_scr[v], qs[n, v])
                dva[v, krows, :] += _dot_tn(p_scr[v], dos[n, v])
            for j in range(N_PAIRS):
                dqn_s[pl.ds(r0, BLK), j * LANES : (j + 1) * LANES] = _unstack_pair(dqv, j, lo)
            return dsink

        dsink = lax.fori_loop(0, nb, block, jnp.zeros((1, LANES), F32))
        dsk_ref[...] += jnp.broadcast_to(dsink, (SUBLANES, LANES))
        for j in range(N_PAIRS):
            cols = slice(j * LANES, (j + 1) * LANES)
            dqn = dqn_s[:, cols]
            qhat = qhat_s[:, cols]
            dqg_ref[:, cols] += _group_rows(dqn * qhat) * Q_SCALE
            dq_ref[:, cols] = _half_rms_bwd(dqn * qg, qhat, rq_s[:, cols], ones).astype(dq_ref.dtype)
        dkn = dka[0] + pltpu.roll(dka[1], HEAD_DIM, 1)
        khat = khat_s[...]
        dkg_ref[...] += _group_rows(dkn * khat)
        dk = _half_rms_bwd(dkn * kg, khat, rk_s[...], ones)
        dv = dva[0] + pltpu.roll(dva[1], HEAD_DIM, 1)
        hp_ref[:, 0:D_KV] = dk[0:BLK]
        hp_ref[:, D_KV : 2 * D_KV] = dv[0:BLK]
        dkv_ref[:, 0:D_KV] = dk[BLK : BLK + tile]
        dkv_ref[:, D_KV : 2 * D_KV] = dv[BLK : BLK + tile]
        hn_ref[:, 0:D_KV] = dk[BLK + tile : ext]
        hn_ref[:, D_KV : 2 * D_KV] = dv[BLK + tile : ext]
        if riding:
            pl.when(i == at_finish)(finish)

    prev, nxt = _halo_specs(tile, seq)
    vec = _full_spec((1, LANES))
    halo = pl.BlockSpec((None, BLK, 2 * D_KV), lambda i: (i, 0, 0))
    return pl.pallas_call(
        body,
        name=name,
        grid=(nt,),
        in_specs=[SMEM_SPEC, _row_spec(tile, D_QKV), prev, nxt, _row_spec(tile, D_ATTN), _row_spec(tile, D_ATTN), vec, vec]
        + [HBM_SPEC] * n_ride,
        out_specs=[
            _row_spec(tile, D_ATTN),
            _row_spec(tile, 2 * D_KV),
            halo,
            halo,
            _full_spec((SUBLANES, D_ATTN)),
            _full_spec((SUBLANES, LANES)),
            _full_spec((SUBLANES, LANES)),
        ]
        + [HBM_SPEC] * n_ride,
        out_shape=[
            jax.ShapeDtypeStruct((seq, D_ATTN), MXU_DTYPE),
            jax.ShapeDtypeStruct((seq, 2 * D_KV), F32),
            jax.ShapeDtypeStruct((nt, BLK, 2 * D_KV), F32),
            jax.ShapeDtypeStruct((nt, BLK, 2 * D_KV), F32),
            jax.ShapeDtypeStruct((SUBLANES, D_ATTN), F32),
            jax.ShapeDtypeStruct((SUBLANES, LANES), F32),
            jax.ShapeDtypeStruct((SUBLANES, LANES), F32),
        ]
        + _landing_shapes(scatter),
        scratch_shapes=[
            pltpu.VMEM((nb, 2, STACK, LANES), MXU_DTYPE),
            pltpu.VMEM((nb, 2, STACK, LANES), MXU_DTYPE),
            pltpu.VMEM((tile, D_ATTN), F32),
            pltpu.VMEM((tile, D_ATTN), F32),
            pltpu.VMEM((ext, LANES), MXU_DTYPE),
            pltpu.VMEM((ext, LANES), MXU_DTYPE),
            pltpu.VMEM((ext, LANES), MXU_DTYPE),
            pltpu.VMEM((ext, LANES), MXU_DTYPE),
            pltpu.VMEM((ext, LANES), F32),
            pltpu.VMEM((ext, LANES), F32),
            pltpu.VMEM((tile, D_ATTN), F32),
            pltpu.VMEM((2, ext, LANES), F32),
            pltpu.VMEM((2, ext, LANES), F32),
            pltpu.VMEM((3, N_HEADS, BLK, 3 * BLK), F32),
            pltpu.VMEM((2, STACK, 3 * BLK), F32),
            pltpu.VMEM((2, STACK, 3 * BLK), F32),
            pltpu.VMEM((2, STACK, 3 * BLK), MXU_DTYPE),
            pltpu.VMEM((2, STACK, 3 * BLK), MXU_DTYPE),
        ]
        + _rider_sems(n_ride),
        compiler_params=_params(("arbitrary",)),
    )(sink, pa, pa, pa, o, do, q_gain2, k_gain2, *scatter)


def _halo_in_specs(tile, nt):
    from_prev = pl.BlockSpec((None, BLK, 2 * D_KV), lambda i: (jnp.maximum(i - 1, 0), 0, 0))
    from_next = pl.BlockSpec((None, BLK, 2 * D_KV), lambda i: (jnp.minimum(i + 1, nt - 1), 0, 0))
    return from_prev, from_next


def _landing_shapes(scatter):
    return [jax.ShapeDtypeStruct((N_DEV,) + b.shape[2:], b.dtype) for b in scatter]


def _rider_sems(n_ride):
    if not n_ride:
        return []
    return [pltpu.SemaphoreType.DMA((7 * n_ride,)), pltpu.SemaphoreType.DMA((7 * n_ride,)), pltpu.SemaphoreType.DMA((n_ride,))]


def _proj_bwd_dx(x, dxn, dq, dkvb, dpb, w_in_t, gain, scale1, name, scatter=()):
    seq, d = x.shape
    tile = min(TOKEN_TILE, seq)
    nt = seq // tile
    n_ride = len(scatter)

    def row(width):
        return _row_spec(tile, width)

    def body(x_ref, dxn_ref, dq_ref, dkvb_ref, dpb_ref, wt_ref, g_ref, s1_ref, *rest):
        i = pl.program_id(0)
        blocks, rest = rest[:n_ride], rest[n_ride:]
        dx_ref, c0_ref, c1_ref = rest[:3]
        landing, sems = rest[3 : 3 + n_ride], rest[3 + n_ride :]
        if n_ride:
            start, finish = _scatter_stages(blocks, landing, *sems)
            at_start, _, at_finish = _rider_steps(nt)
            pl.when(i == at_start)(start)

        @pl.when(i == 0)
        def _():
            c0_ref[...] = jnp.zeros_like(c0_ref)
            c1_ref[...] = jnp.zeros_like(c1_ref)

        dh = (
            _dot(dq_ref[...], wt_ref[0:D_ATTN, :])
            + _dot(dkvb_ref[...], wt_ref[D_ATTN:D_QKV, :])
            + _dot(dpb_ref[...], wt_ref[D_QKV:D_IN, :])
        )
        xv = x_ref[...]
        r = lax.rsqrt(jnp.mean(xv * xv, axis=-1, keepdims=True) + EPS)
        xn = xv * r
        c0_ref[...] += _group_rows(dh)
        c1_ref[...] += _group_rows(dh * xn)
        dxn_ = dh * (g_ref[...] * s1_ref[...])
        dx_ref[...] = dxn_ref[...] + r * (dxn_ - xn * jnp.mean(xn * dxn_, axis=-1, keepdims=True))
        if n_ride:
            pl.when(i == at_finish)(finish)

    vec = _full_spec((1, d))
    return pl.pallas_call(
        body,
        name=name,
        grid=(nt,),
        in_specs=[row(d), row(d), row(D_ATTN), row(2 * D_KV), row(D_REST), _full_spec((D_IN, d)), vec, vec]
        + [HBM_SPEC] * n_ride,
        out_specs=[row(d), _full_spec((SUBLANES, d)), _full_spec((SUBLANES, d))] + [HBM_SPEC] * n_ride,
        out_shape=[
            jax.ShapeDtypeStruct((seq, d), F32),
            jax.ShapeDtypeStruct((SUBLANES, d), F32),
            jax.ShapeDtypeStruct((SUBLANES, d), F32),
        ]
        + _landing_shapes(scatter),
        scratch_shapes=_rider_sems(n_ride),
        compiler_params=_params(("arbitrary",)),
    )(x, dxn, dq, dkvb, dpb, w_in_t, gain, scale1, *scatter)


def _proj_bwd_dw(x, gain, scale1, shift, dq, dkv, halo_prev, halo_next, dpb, name, gather=()):
    seq, d = x.shape
    tile = min(TOKEN_TILE, seq)
    nt = seq // tile
    assert tile >= 2 * BLK
    n_ride = len(gather)

    def body(x_ref, g_ref, s1_ref, sh_ref, dq_ref, dkv_ref, hn_ref, hp_ref, dpb_ref, *rest):
        i = pl.program_id(0)
        sources, rest = rest[:n_ride], rest[n_ride:]
        dw_ref, dkvb_ref = rest[:2]
        gathered, (acc, *sems) = rest[2 : 2 + n_ride], rest[2 + n_ride :]
        if n_ride:
            start, forward, finish = _all_gather_stages(
                [_row_block(g, s.shape[0]) for g, s in zip(gathered, sources)], sems[0], sems[1], sources=sources, local_sems=sems[2]
            )
            at_start, at_forward, at_finish = _rider_steps(nt)
            pl.when(i == at_start)(start)

        @pl.when(i == 0)
        def _():
            acc[...] = jnp.zeros_like(acc)

        top = dkv_ref[0:BLK, :] + jnp.where(i > 0, hn_ref[...], 0.0)
        bot = dkv_ref[tile - BLK : tile, :] + jnp.where(i < nt - 1, hp_ref[...], 0.0)
        dkvb_ref[0:BLK, :] = top.astype(dkvb_ref.dtype)
        dkvb_ref[tile - BLK : tile, :] = bot.astype(dkvb_ref.dtype)
        if tile > 2 * BLK:
            dkvb_ref[BLK : tile - BLK, :] = dkv_ref[BLK : tile - BLK, :].astype(dkvb_ref.dtype)
        xv = x_ref[...]
        r = lax.rsqrt(jnp.mean(xv * xv, axis=-1, keepdims=True) + EPS)
        h = _mx((xv * r) * g_ref[...] * s1_ref[...] + sh_ref[...])
        acc[0:D_ATTN, :] += _dot_tn(dq_ref[...], h)
        acc[D_ATTN:D_QKV, :] += _dot_tn(dkvb_ref[...], h)
        acc[D_QKV:D_IN, :] += _dot_tn(dpb_ref[...], h)

        @pl.when(i == nt - 1)
        def _():
            dw_ref[...] = acc[...].astype(dw_ref.dtype)

        if n_ride:
            pl.when(i == at_forward)(forward)
            pl.when(i == at_finish)(finish)

    from_prev, from_next = _halo_in_specs(tile, nt)
    vec = _full_spec((1, d))
    return pl.pallas_call(
        body,
        name=name,
        grid=(nt,),
        in_specs=[
            _row_spec(tile, d),
            vec,
            vec,
            vec,
            _row_spec(tile, D_ATTN),
            _row_spec(tile, 2 * D_KV),
            from_prev,
            from_next,
            _row_spec(tile, D_REST),
        ]
        + [HBM_SPEC] * n_ride,
        out_specs=[_full_spec((D_IN, d)), _row_spec(tile, 2 * D_KV)] + [HBM_SPEC] * n_ride,
        out_shape=[jax.ShapeDtypeStruct((D_IN, d), jnp.bfloat16), jax.ShapeDtypeStruct((seq, 2 * D_KV), MXU_DTYPE)]
        + [jax.ShapeDtypeStruct((N_DEV * g.shape[0], g.shape[1]), g.dtype) for g in gather],
        scratch_shapes=[pltpu.VMEM((D_IN, d), F32)] + _rider_sems(n_ride),
        compiler_params=_params(("arbitrary",)),
    )(x, gain, scale1, shift, dq, dkv, halo_next, halo_prev, dpb, *gather)


def _adamw_math(w, g, m, v):
    m = ADAM_B1 * m + (1.0 - ADAM_B1) * g
    v = ADAM_B2 * v + (1.0 - ADAM_B2) * (g * g)
    m_hat = m / (1.0 - ADAM_B1**ADAM_STEP)
    v_hat = v / (1.0 - ADAM_B2**ADAM_STEP)
    delta = -ADAM_LR * (m_hat / (jnp.sqrt(v_hat) + ADAM_EPS) + ADAM_WD * w)
    return delta, m, v


def _small_update(gathered, gathered_ws, w, m, v, ws, m_ws, v_ws):
    def body(ga_ref, gws_ref, w_ref, m_ref, v_ref, ws_ref, mws_ref, vws_ref, *outs):
        for src, refs, out in ((ga_ref, (w_ref, m_ref, v_ref), outs[0:4]), (gws_ref, (ws_ref, mws_ref, vws_ref), outs[4:8])):
            g = src[0].astype(F32)
            for j in range(1, N_DEV):
                g = g + src[j].astype(F32)
            out[0][...] = g
            out[1][...], out[2][...], out[3][...] = _adamw_math(refs[0][...], g, refs[1][...], refs[2][...])

    shapes = [jax.ShapeDtypeStruct(w.shape, F32)] * 4 + [jax.ShapeDtypeStruct(ws.shape, F32)] * 4
    return pl.pallas_call(
        body,
        name="small_update",
        in_specs=[VMEM_SPEC] * 8,
        out_specs=[VMEM_SPEC] * 8,
        out_shape=shapes,
        compiler_params=_params(),
    )(gathered, gathered_ws, w, m, v, ws, m_ws, v_ws)


def _ada_update(c_all, d_ada_cols, w, m, v):
    n_layers = w.shape[0]

    def body(c_ref, da_ref, w_ref, m_ref, v_ref, g_ref, d_ref, mo_ref, vo_ref):
        cv = c_ref[...]
        cond = cv * _sigmoid(cv)
        for l in range(n_layers):
            g = lax.dot_general(
                cond, da_ref[l], (((0,), (0,)), ((), ())), preferred_element_type=F32, precision=lax.Precision.HIGHEST
            )
            g_ref[l] = g
            d_ref[l], mo_ref[l], vo_ref[l] = _adamw_math(w_ref[l], g, m_ref[l], v_ref[l])

    return pl.pallas_call(
        body,
        name="ada_update",
        in_specs=[VMEM_SPEC] * 5,
        out_specs=[VMEM_SPEC] * 4,
        out_shape=[jax.ShapeDtypeStruct(w.shape, F32)] * 4,
        compiler_params=_params(),
    )(c_all, d_ada_cols, w, m, v)


def _position():
    return lax.axis_index("x"), lax.axis_index("y"), lax.axis_index("c")


def _flip(pos, k):
    x, y, c = pos
    return (1 - x if k & 4 else x, 1 - y if k & 2 else y, 1 - c if k & 1 else c)


def _index(pos):
    x, y, c = pos
    return 4 * x + 2 * y + c


def _remote(src, dst, send_sem, recv_sem, to):
    return pltpu.make_async_remote_copy(
        src_ref=src, dst_ref=dst, send_sem=send_sem, recv_sem=recv_sem, device_id=to, device_id_type=MESH_ID
    )


def _all_gather_stages(slots, send_sems, recv_sems, sources=None, local_sems=None):
    me = _position()
    sibling = _flip(me, 1)
    others = (4, 2, 6)
    arrays = range(len(slots))

    def copy(t, k, block, to, own=False):
        slot = slots[t](_index(block))
        src = sources[t] if own and sources is not None else slot
        return _remote(src, slot, send_sems.at[7 * t + k], recv_sems.at[7 * t + k], to)

    def first(t):
        return [copy(t, 0, me, sibling, own=True)] + [copy(t, 1 + j, me, _flip(me, f), own=True) for j, f in enumerate(others)]

    def passed(t, j):
        return copy(t, 4 + j, _flip(me, others[j]), sibling)

    def local(t):
        return pltpu.make_async_copy(sources[t], slots[t](_index(me)), local_sems.at[t])

    def start():
        for t in arrays:
            if sources is not None:
                local(t).start()
            for cp in first(t):
                cp.start()

    def forward():
        for j, f in enumerate(others):
            for t in arrays:
                copy(t, 1 + j, _flip(me, f), me).wait_recv()
                passed(t, j).start()

    def finish():
        for t in arrays:
            copy(t, 0, sibling, me).wait_recv()
            for j, f in enumerate(others):
                copy(t, 4 + j, _flip(sibling, f), me).wait_recv()
        for t in arrays:
            for cp in first(t) + [passed(t, j) for j in range(len(others))]:
                cp.wait_send()
            if sources is not None:
                local(t).wait()

    return start, forward, finish


def _two_level_all_gather(slots, send_sems, recv_sems, between=None):
    start, forward, finish = _all_gather_stages(slots, send_sems, recv_sems)
    start()
    if between is not None:
        between()
    forward()
    finish()


def _row_block(ref, rows):
    return lambda j: ref.at[pl.ds(pl.multiple_of(j * rows, 16), rows), :]


def _scatter_stages(blocks, landing, send_sems, recv_sems, local_sems):
    me = _position()
    my = _index(me)
    arrays = range(len(blocks))

    def copy(t, k):
        px, py, pc = to = _flip(me, k)
        return _remote(blocks[t].at[2 * px + py, pc], landing[t].at[my], send_sems.at[7 * t + k - 1], recv_sems.at[7 * t + k - 1], to)

    def arrival(t, k):
        slot = landing[t].at[_index(_flip(me, k))]
        return _remote(slot, slot, send_sems.at[7 * t + k - 1], recv_sems.at[7 * t + k - 1], _flip(me, k))

    def local(t):
        x, y, c = me
        return pltpu.make_async_copy(blocks[t].at[2 * x + y, c], landing[t].at[my], local_sems.at[t])

    def start():
        for t in arrays:
            local(t).start()
            for k in range(1, N_DEV):
                copy(t, k).start()

    def finish():
        for t in arrays:
            for k in range(1, N_DEV):
                arrival(t, k).wait_recv()
        for t in arrays:
            for k in range(1, N_DEV):
                copy(t, k).wait_send()
            local(t).wait()

    return start, finish


def _ada_exchange(c_ref, w_ref, call_ref, parts_ref, sbuf, sem_s1, sem_r1, sem_s2, sem_r2):
    d = c_ref.shape[-1]
    n_layers = w_ref.shape[0]
    me = _position()
    my = _index(me)
    call_ref[my] = jnp.broadcast_to(c_ref[...], (SUBLANES, d))
    mine = call_ref.at[my]
    first = [_remote(mine, mine, sem_s1.at[k - 1], sem_r1.at[k - 1], _flip(me, k)) for k in range(1, N_DEV)]
    for cp in first:
        cp.start()
    for k in range(1, N_DEV):
        theirs = call_ref.at[_index(_flip(me, k))]
        _remote(theirs, theirs, sem_s1.at[k - 1], sem_r1.at[k - 1], _flip(me, k)).wait_recv()
    cv = call_ref[...].reshape(N_DEV * SUBLANES, d)
    cond = cv * _sigmoid(cv)
    for l in range(n_layers):
        rows = jnp.dot(cond, w_ref[l], preferred_element_type=F32, precision=lax.Precision.HIGHEST)
        for b in range(N_DEV):
            sbuf[b, l] = rows[b * SUBLANES : (b + 1) * SUBLANES]
    parts_ref[my] = sbuf[my]
    second = []
    for k in range(1, N_DEV):
        to = _flip(me, k)
        second.append(_remote(sbuf.at[_index(to)], parts_ref.at[my], sem_s2.at[k - 1], sem_r2.at[k - 1], to))
    for cp in second:
        cp.start()
    for k in range(1, N_DEV):
        theirs = parts_ref.at[_index(_flip(me, k))]
        _remote(theirs, theirs, sem_s2.at[k - 1], sem_r2.at[k - 1], _flip(me, k)).wait_recv()
    for cp in first + second:
        cp.wait_send()


def _gather_weights(w_in_t, w_out, c_row, w_ada):
    n_layers, rows_in, d = w_in_t.shape
    rows_out = w_out.shape[1]
    width = w_ada.shape[2]

    def body(wi_ref, wo_ref, c_ref, wa_ref, gi_ref, go_ref, si_ref, so_ref, call_ref, parts_ref, sbuf, send_sems, recv_sems, *ada_sems):
        my = _index(_position())
        si_ref[...] = wi_ref[...].astype(si_ref.dtype)
        so_ref[...] = wo_ref[...].astype(so_ref.dtype)
        gi_ref[pl.ds(pl.multiple_of(my * rows_in, 16), rows_in), :] = si_ref[0]
        go_ref[pl.ds(pl.multiple_of(my * rows_out, 16), rows_out), :] = so_ref[0]
        _two_level_all_gather(
            (_row_block(gi_ref, rows_in), _row_block(go_ref, rows_out)),
            send_sems,
            recv_sems,
            between=functools.partial(_ada_exchange, c_ref, wa_ref, call_ref, parts_ref, sbuf, *ada_sems),
        )

    return pl.pallas_call(
        body,
        name="gather_weights",
        in_specs=[VMEM_SPEC] * 4,
        out_specs=[VMEM_SPEC] * 6,
        out_shape=[
            jax.ShapeDtypeStruct((N_DEV * rows_in, d), MXU_DTYPE),
            jax.ShapeDtypeStruct((N_DEV * rows_out, d), MXU_DTYPE),
            jax.ShapeDtypeStruct(w_in_t.shape, MXU_DTYPE),
            jax.ShapeDtypeStruct(w_out.shape, MXU_DTYPE),
            jax.ShapeDtypeStruct((N_DEV, SUBLANES, d), F32),
            jax.ShapeDtypeStruct((N_DEV, n_layers, SUBLANES, width), F32),
        ],
        scratch_shapes=[
            pltpu.VMEM((N_DEV, n_layers, SUBLANES, width), F32),
            pltpu.SemaphoreType.DMA((14,)),
            pltpu.SemaphoreType.DMA((14,)),
        ]
        + [pltpu.SemaphoreType.DMA((N_DEV - 1,))] * 4,
        compiler_params=_params(),
    )(w_in_t, w_out, c_row, w_ada)


def _gather_small(packed, adam=()):
    n_adam = len(adam)

    def body(p_ref, *rest):
        quads = [rest[4 * t : 4 * t + 4] for t in range(n_adam)]
        rest = rest[4 * n_adam :]
        g_ref = rest[0]
        results = [rest[1 + 3 * t : 4 + 3 * t] for t in range(n_adam)]
        send_sems, recv_sems = rest[1 + 3 * n_adam :]
        g_ref[_index(_position())] = p_ref[...]

        def updates():
            for (w_ref, gr_ref, m_ref, v_ref), (d_ref, mo_ref, vo_ref) in zip(quads, results):
                d_ref[...], mo_ref[...], vo_ref[...] = _adamw_math(w_ref[...], gr_ref[...], m_ref[...], v_ref[...])

        _two_level_all_gather((lambda j: g_ref.at[j],), send_sems, recv_sems, between=updates)

    return pl.pallas_call(
        body,
        name="gather_small",
        in_specs=[VMEM_SPEC] * (1 + 4 * n_adam),
        out_specs=[VMEM_SPEC] * (1 + 3 * n_adam),
        out_shape=[jax.ShapeDtypeStruct((N_DEV,) + packed.shape, F32)]
        + [jax.ShapeDtypeStruct(q[0].shape, F32) for q in adam for _ in range(3)],
        scratch_shapes=[pltpu.SemaphoreType.DMA((7,)), pltpu.SemaphoreType.DMA((7,))],
        compiler_params=_params(),
    )(packed, *[a for q in adam for a in q])


def _scatter_finish(landed, name):
    n = len(landed)

    def body(*refs):
        for src, out in zip(refs[:n], refs[n:]):
            g = src[0].astype(F32)
            for j in range(1, N_DEV):
                g = g + src[j].astype(F32)
            out[...] = g

    return pl.pallas_call(
        body,
        name=name,
        in_specs=[VMEM_SPEC] * n,
        out_specs=[VMEM_SPEC] * n,
        out_shape=[jax.ShapeDtypeStruct(a.shape[1:], F32) for a in landed],
        compiler_params=_params(),
    )(*landed)


def _pack_rows(parts):
    rows, offsets, at = [], [], 0
    for p in parts:
        flat = p.reshape(-1)
        n = -(-flat.shape[0] // (SUBLANES * LANES)) * SUBLANES
        rows.append(jnp.pad(flat, (0, n * LANES - flat.shape[0])).reshape(n, LANES))
        offsets.append(at)
        at += n
    return jnp.concatenate(rows, axis=0), offsets


def _unpack_rows(packed, offsets, shapes):
    out = []
    for off, shape in zip(offsets, shapes):
        size = 1
        for s in shape:
            size *= s
        n = -(-size // (SUBLANES * LANES)) * SUBLANES
        out.append(packed[off : off + n].reshape(-1)[:size].reshape(shape))
    return out


def kernel(x, c, w_ada, b_ada, norm_gain, w_in, q_gain, k_gain, sink, w_s, b_s, w_out, loss_target, m_w_ada, m_b_ada, m_norm_gain, m_w_in, m_q_gain, m_k_gain, m_sink, m_w_s, m_b_s, m_w_out, v_w_ada, v_b_ada, v_norm_gain, v_w_in, v_q_gain, v_k_gain, v_sink, v_w_s, v_b_s, v_w_out):
    seq, d = x.shape[1], x.shape[2]
    n_layers = w_in.shape[0]
    w_cols = w_in.shape[2]
    ada_cols = w_ada.shape[2]
    my = _index(_position())
    xs = x.reshape(seq, d)
    target = loss_target.reshape(seq, d)

    w_in_t0, w_out0, shard_in, shard_out, c_all, ada_parts = _gather_weights(w_in.transpose(0, 2, 1), w_out, c, w_ada)
    w_in_ts, w_outs = [w_in_t0], [w_out0]
    ada = ada_parts[:, :, 0, :].transpose(1, 0, 2).reshape(n_layers, 3 * d) + b_ada
    shift, scale1, gate = ada[:, None, 0:d], 1.0 + ada[:, None, d : 2 * d], ada[:, None, 2 * d : 3 * d]
    gain = norm_gain[:, None, :]

    w_s_m = w_s.astype(MXU_DTYPE)
    w_s_t = w_s_m.transpose(0, 1, 3, 2)
    b_st = jnp.repeat(b_s.transpose(0, 2, 1), HEAD_DIM, axis=2)
    q_gain2 = jnp.tile(q_gain, (1, 2))[:, None, :]
    k_gain2 = jnp.tile(k_gain, (1, 2))[:, None, :]

    xl, saved = xs, []
    for l in range(n_layers):
        last = l == n_layers - 1
        pa, pb = _ln_proj_fwd(xl, gain[l], scale1[l], shift[l], w_in_ts[l], f"ln_proj_fwd_{l}")
        if last:
            o = _attn_fwd(pa, q_gain2[l], k_gain2[l], sink[l], f"attn_fwd_{l}")
        else:
            o, w_in_next, w_out_next = _attn_fwd(
                pa, q_gain2[l], k_gain2[l], sink[l], f"attn_fwd_{l}", gather=(shard_in[l + 1], shard_out[l + 1])
            )
            w_in_ts.append(w_in_next)
            w_outs.append(w_out_next)
        saved.append((xl, pa, pb, o))
        out = _mix_out_fwd(pb, o, xl, gate[l], w_outs[l], w_s_m[l], b_st[l], f"mix_out_fwd_{l}", target if last else None)
        if last:
            dx, sq_err = out
        else:
            xl = out

    g_w_in, g_w_out, small, d_ada_rows = [None] * n_layers, [None] * n_layers, [None] * n_layers, [None] * n_layers
    waiting = []
    d_ws_all = [None] * n_layers
    for l in reversed(range(n_layers)):
        x_l, pa, pb, o = saved[l]
        dpb, do, dw_out, d_gate8, d_ws, d_bs = _mix_out_bwd(
            dx, pb, o, gate[l], w_outs[l], w_s_m[l], w_s_t[l], b_st[l], f"mix_out_bwd_{l}"
        )
        riding = waiting + [(g_w_out, l, dw_out.reshape(4, 2, D_MIX // N_DEV, d))]
        attn = _attn_bwd(pa, o, do, q_gain2[l], k_gain2[l], sink[l], f"attn_bwd_{l}", scatter=tuple(b for _, _, b in riding))
        dq, dkv, halo_prev, halo_next, d_qg, d_kg, d_sk = attn[:7]
        for (dest, layer, _), total in zip(riding, _scatter_finish(attn[7:], f"scatter_finish_{l}")):
            dest[layer] = total.transpose(1, 0) if dest is g_w_in else total
        d_ws_all[l] = d_ws
        dw_args = (x_l, gain[l], scale1[l], shift[l], dq, dkv, halo_prev, halo_next, dpb, f"proj_bwd_dw_{l}")
        if l > 0:
            dw_in_t, dkvb = _proj_bwd_dw(*dw_args)
        else:
            d_ws_wire = jnp.stack(d_ws_all).reshape(-1, LANES).astype(jnp.bfloat16)
            dw_in_t, dkvb, gathered_ws = _proj_bwd_dw(*dw_args, gather=(d_ws_wire,))
        blocks_in = dw_in_t.reshape(4, 2, w_cols, d)
        waiting = [(g_w_in, l, blocks_in)] if l > 0 else []
        dxs = _proj_bwd_dx(
            x_l, dx, dq, dkvb, dpb, w_in_ts[l], gain[l], scale1[l], f"proj_bwd_dx_{l}", scatter=() if l > 0 else (blocks_in,)
        )
        dx, c0, c1 = dxs[:3]
        if l == 0:
            g_w_in[l] = _scatter_finish(dxs[3:], "scatter_finish_in_0")[0].transpose(1, 0)
        c0s, c1s = c0.sum(axis=0), c1.sum(axis=0)
        d_ada_rows[l] = jnp.concatenate([c0s, norm_gain[l] * c1s, d_gate8.sum(axis=0)])
        small[l] = (
            scale1[l, 0] * c1s,
            d_qg.sum(axis=0).reshape(N_HEADS, HEAD_DIM).sum(axis=0),
            d_kg.sum(axis=0).reshape(2, HEAD_DIM).sum(axis=0),
            d_sk[0, 0:N_HEADS],
            d_bs.reshape(BLK, N_GROUPS, HEAD_DIM).sum(axis=2).transpose(1, 0),
        )

    names = ("norm_gain", "q_gain", "k_gain", "sink", "b_s")
    stacked = [jnp.stack([small[l][t] for l in range(n_layers)]) for t in range(len(names))]
    d_ada = jnp.stack(d_ada_rows)
    packed, offsets = _pack_rows(stacked + [d_ada, sq_err[0, 0:1]])
    g_w_in, g_w_out = jnp.stack(g_w_in), jnp.stack(g_w_out)
    gathered, *upd = _gather_small(packed, adam=((w_in, g_w_in, m_w_in, v_w_in), (w_out, g_w_out, m_w_out, v_w_out)))
    gathered_ws = gathered_ws.reshape(N_DEV, -1, LANES)
    upd_in, upd_out = upd[0:3], upd[3:6]
    no_weight = jnp.zeros((1,), F32)
    weights = (norm_gain, q_gain, k_gain, sink, b_s, b_ada, no_weight)
    moments_m = (m_norm_gain, m_q_gain, m_k_gain, m_sink, m_b_s, m_b_ada, no_weight)
    moments_v = (v_norm_gain, v_q_gain, v_k_gain, v_sink, v_b_s, v_b_ada, no_weight)
    w_pack, _ = _pack_rows(weights)
    m_pack, _ = _pack_rows(moments_m)
    v_pack, _ = _pack_rows(moments_v)
    shapes = [w.shape for w in weights]
    flat_ws = lambda a: a.reshape(-1, LANES)
    updated = _small_update(gathered, gathered_ws, w_pack, m_pack, v_pack, flat_ws(w_s), flat_ws(m_w_s), flat_ws(v_w_s))
    g_small, d_small, m_small, v_small = (_unpack_rows(p, offsets, shapes) for p in updated[0:4])
    ws_small = [p.reshape(w_s.shape) for p in updated[4:8]]
    loss = g_small[-1][0] * (0.5 / d)

    ada_off = offsets[-2]
    ada_n = -(-n_layers * 3 * d // (SUBLANES * LANES)) * SUBLANES
    d_ada_all = gathered[:, ada_off : ada_off + ada_n].reshape(N_DEV, -1)[:, : n_layers * 3 * d].reshape(N_DEV, n_layers, 3 * d)
    d_ada_cols = lax.dynamic_slice_in_dim(d_ada_all, my * ada_cols, ada_cols, axis=2)
    g_w_ada, *upd_ada = _ada_update(c_all[:, 0, :], d_ada_cols.transpose(1, 0, 2), w_ada, m_w_ada, v_w_ada)

    def ordered(ada_, in_, out_, small_, ws):
        ng, qg, kg, sk, bs, ba, _ = small_
        return (ada_, ba, ng, in_, qg, kg, sk, ws, bs, out_)

    grads = ordered(g_w_ada, g_w_in, g_w_out, g_small, ws_small[0])
    deltas = ordered(upd_ada[0], upd_in[0], upd_out[0], d_small, ws_small[1])
    new_m = ordered(upd_ada[1], upd_in[1], upd_out[1], m_small, ws_small[2])
    new_v = ordered(upd_ada[2], upd_in[2], upd_out[2], v_small, ws_small[3])
    return (loss, dx.reshape(x.shape), *grads, *deltas, *new_m, *new_v)
```

```python
import functools

import jax
import jax.numpy as jnp
from jax import lax
from jax.experimental import pallas as pl
from jax.experimental.pallas import tpu as pltpu

F32 = jnp.float32
MXU_DTYPE = jnp.bfloat16
MESH_ID = pl.DeviceIdType.MESH

N_DEV = 8
HEAD_DIM = 64
N_HEADS = 8
Q_PER_KV = 4
D_ATTN = 512
D_KV = 128
D_GM = 512
N_GROUPS = 8
D_MIX = D_ATTN + D_GM
BLK = 128
LANES = 128
SUBLANES = 8
N_PAIRS = D_ATTN // LANES
D_QKV = D_ATTN + 2 * D_KV
D_REST = D_ATTN + 3 * D_GM
D_IN = D_QKV + D_REST
EPS = 1e-6
NEG_INF = -1e30
ALIBI_SLOPES = tuple(2.0 ** (-8.0 * (h + 1) / N_HEADS) for h in range(N_HEADS))
Q_SCALE = 1.0 / 8.0

ADAM_LR = 0.001
ADAM_B1 = 0.9
ADAM_B2 = 0.999
ADAM_EPS = 1e-08
ADAM_WD = 0.01
ADAM_STEP = 10

TOKEN_TILE = 512
VMEM_LIMIT_BYTES = 56 * 1024 * 1024


def _params(semantics=None):
    return pltpu.CompilerParams(dimension_semantics=semantics, vmem_limit_bytes=VMEM_LIMIT_BYTES)


def _dot(a, b):
    return jnp.dot(a, b, preferred_element_type=F32)


def _dot_nt(a, b):
    return lax.dot_general(a, b, (((1,), (1,)), ((), ())), preferred_element_type=F32)


def _dot_tn(a, b):
    return lax.dot_general(a, b, (((0,), (0,)), ((), ())), preferred_element_type=F32)


def _mx(v):
    return v.astype(MXU_DTYPE)


def _lane_lo(rows):
    return lax.broadcasted_iota(jnp.int32, (rows, LANES), 1) < HEAD_DIM


def _half_ones(width=LANES):
    group_bits = HEAD_DIM.bit_length() - 1
    r = jnp.right_shift(lax.broadcasted_iota(jnp.int32, (width, width), 0), group_bits)
    c = jnp.right_shift(lax.broadcasted_iota(jnp.int32, (width, width), 1), group_bits)
    return jnp.where(r == c, 1.0, 0.0).astype(jnp.bfloat16)


WIDE = 2 * LANES


def _half_sum(v, ones):
    p1 = v.astype(jnp.bfloat16)
    p2 = (v - p1.astype(F32)).astype(jnp.bfloat16)
    return _dot(p1, ones) + _dot(p2, ones)


def _half_rms(v, ones):
    r = lax.rsqrt(_half_sum(v * v, ones) * (1.0 / HEAD_DIM) + EPS)
    return v * r, r


def _half_rms_bwd(dy, vhat, r, ones):
    return r * (dy - vhat * (_half_sum(vhat * dy, ones) * (1.0 / HEAD_DIM)))


def _group_rows(v):
    rows, n = v.shape
    return v.reshape(rows // SUBLANES, SUBLANES, n).sum(axis=0)


def _sigmoid(v):
    return 1.0 / (1.0 + jnp.exp(-v))


ROW_CHUNK = 32
VARIANT_HEADS = ((0, 2, 5, 7), (1, 3, 4, 6))
HEAD_SLOT = {h: (v, s) for v, heads in enumerate(VARIANT_HEADS) for s, h in enumerate(heads)}
STACK = Q_PER_KV * BLK


def _fill_attn_bias(bias_s):
    qi = lax.broadcasted_iota(jnp.int32, (BLK, 3 * BLK), 0)
    ci = lax.broadcasted_iota(jnp.int32, (BLK, 3 * BLK), 1)
    dist = jnp.abs(ci - BLK - qi)
    distf = dist.astype(F32)
    window = dist <= BLK
    for kind, seen in enumerate((window & (ci >= BLK), window, window & (ci < 2 * BLK))):
        for h in range(N_HEADS):
            bias_s[kind, h] = jnp.where(seen, -(ALIBI_SLOPES[h] * distf), NEG_INF)


def _block_kind(block, seq):
    assert seq >= 2 * BLK
    return jnp.where(block == 0, 0, jnp.where(block == seq // BLK - 1, 2, 1))


def _stage_queries(qn, lo_t, j, nb, qs):
    for a in range(2):
        v, slot = HEAD_SLOT[2 * j + a]
        qm = _mx(jnp.where(lo_t, qn, 0.0) if a == 0 else jnp.where(lo_t, 0.0, qn))
        for n in range(nb):
            qs[n, v, slot * BLK : (slot + 1) * BLK, :] = qm[n * BLK : (n + 1) * BLK]


def _unstack_pair(stacked, j, lo):
    (v0, s0), (v1, s1) = HEAD_SLOT[2 * j], HEAD_SLOT[2 * j + 1]
    return jnp.where(lo, stacked[v0][s0 * BLK : (s0 + 1) * BLK], stacked[v1][s1 * BLK : (s1 + 1) * BLK])


def _stage_keys(kvp_ref, qkv_ref, kvn_ref, kg, ones, tile, ks, kr, vs, vr, khat_s=None, rk_s=None):
    pieces = (
        (0, BLK, kvp_ref[:, 0:D_KV], kvp_ref[:, D_KV : 2 * D_KV]),
        (BLK, tile, qkv_ref[:, D_ATTN : D_ATTN + D_KV], qkv_ref[:, D_ATTN + D_KV : D_QKV]),
        (BLK + tile, BLK, kvn_ref[:, 0:D_KV], kvn_ref[:, D_KV : 2 * D_KV]),
    )
    for r0, n, k, v in pieces:
        khat, rk = _half_rms(k, ones)
        kn = khat * kg
        ks[r0 : r0 + n, :] = _mx(kn)
        kr[r0 : r0 + n, :] = _mx(pltpu.roll(kn, HEAD_DIM, 1))
        vs[r0 : r0 + n, :] = _mx(v)
        vr[r0 : r0 + n, :] = _mx(pltpu.roll(v, HEAD_DIM, 1))
        if khat_s is not None:
            khat_s[r0 : r0 + n, :] = khat
            rk_s[r0 : r0 + n, :] = rk


def _halo_specs(tile, seq):
    nb = tile // BLK
    last = seq // BLK - 1
    kv_col = D_ATTN // (2 * D_KV)
    prev = pl.BlockSpec((BLK, 2 * D_KV), lambda i: (jnp.maximum(i * nb - 1, 0), kv_col))
    nxt = pl.BlockSpec((BLK, 2 * D_KV), lambda i: (jnp.minimum((i + 1) * nb, last), kv_col))
    return prev, nxt


def _row_spec(tile, width):
    return pl.BlockSpec((tile, width), lambda i: (i, 0))


def _full_spec(shape):
    nd = len(shape)
    return pl.BlockSpec(shape, lambda i: (0,) * nd)


SMEM_SPEC = pl.BlockSpec(memory_space=pltpu.SMEM)
VMEM_SPEC = pl.BlockSpec(memory_space=pltpu.VMEM)
HBM_SPEC = pl.BlockSpec(memory_space=pltpu.HBM)


def _rider_steps(nt):
    return 0, (2 * nt) // 3, nt - 1


def _gather_rider(sources, gathered, sems, step, nt):
    start, forward, finish = _all_gather_stages(
        [_row_block(g, s.shape[0]) for g, s in zip(gathered, sources)], sems[0], sems[1], sources=sources, local_sems=sems[2]
    )
    at_start, at_forward, at_finish = _rider_steps(nt)
    pl.when(step == at_start)(start)

    def after_compute():
        pl.when(step == at_forward)(forward)
        pl.when(step == at_finish)(finish)

    return after_compute


def _gathered_shapes(gather):
    return [jax.ShapeDtypeStruct((N_DEV * g.shape[0], g.shape[1]), g.dtype) for g in gather]


def _ln_proj_fwd(x, gain, scale1, shift, w_in_t, name, gather=()):
    seq, d = x.shape
    tile = min(TOKEN_TILE, seq)
    nt = seq // tile
    n_ride = len(gather)

    def body(x_ref, g_ref, s1_ref, sh_ref, wt_ref, *rest):
        sources, (pa_ref, pb_ref), rest = rest[:n_ride], rest[n_ride : n_ride + 2], rest[n_ride + 2 :]
        after_compute = _gather_rider(sources, rest[:n_ride], rest[n_ride:], pl.program_id(0), nt) if n_ride else None
        xv = x_ref[...]
        r = lax.rsqrt(jnp.mean(xv * xv, axis=-1, keepdims=True) + EPS)
        h = _mx((xv * r) * g_ref[...] * s1_ref[...] + sh_ref[...])
        pa_ref[...] = _dot_nt(h, wt_ref[0:D_QKV, :])
        pb_ref[...] = _dot_nt(h, wt_ref[D_QKV:D_IN, :])
        if n_ride:
            after_compute()

    vec = _full_spec((1, d))
    return pl.pallas_call(
        body,
        name=name,
        grid=(nt,),
        in_specs=[_row_spec(tile, d), vec, vec, vec, _full_spec((D_IN, d))] + [HBM_SPEC] * n_ride,
        out_specs=[_row_spec(tile, D_QKV), _row_spec(tile, D_REST)] + [HBM_SPEC] * n_ride,
        out_shape=[jax.ShapeDtypeStruct((seq, D_QKV), F32), jax.ShapeDtypeStruct((seq, D_REST), F32)] + _gathered_shapes(gather),
        scratch_shapes=_rider_sems(n_ride),
        compiler_params=_params(("arbitrary",) if n_ride else ("parallel",)),
    )(x, gain, scale1, shift, w_in_t, *gather)


def _attn_fwd(pa, q_gain2, k_gain2, sink, name, gather=None):
    seq = pa.shape[0]
    tile = min(TOKEN_TILE, seq)
    nb = tile // BLK
    nt = seq // tile
    ext = tile + 2 * BLK
    riding = gather is not None

    def body(sink_ref, qkv_ref, kvp_ref, kvn_ref, qg_ref, kg_ref, *rest):
        i = pl.program_id(0)
        if riding:
            sources, o_ref, gathered = rest[0:2], rest[2], rest[3:5]
            qs, ks, kr, vs, vr, bias_s, s_scr, p_scr, inv_scr, *sems = rest[5:]
            after_compute = _gather_rider(sources, gathered, sems, i, nt)
        else:
            o_ref, qs, ks, kr, vs, vr, bias_s, s_scr, p_scr, inv_scr = rest

        @pl.when(i == 0)
        def _():
            _fill_attn_bias(bias_s)

        ones = _half_ones()
        lo = _lane_lo(BLK)
        lo_t = _lane_lo(tile)
        _stage_keys(kvp_ref, qkv_ref, kvn_ref, kg_ref[...], ones, tile, ks, kr, vs, vr)
        for j in range(N_PAIRS):
            qhat, _ = _half_rms(qkv_ref[:, j * LANES : (j + 1) * LANES], ones)
            _stage_queries(qhat * (qg_ref[...] * Q_SCALE), lo_t, j, nb, qs)

        def block(n, carry):
            r0 = pl.multiple_of(n * BLK, BLK)
            krows = pl.ds(r0, 3 * BLK)
            kind = _block_kind(i * nb + n, seq)
            for v in range(2):
                s_scr[v] = _dot_nt(qs[n, v], (kr if v else ks)[krows, :])
            for h in range(N_HEADS):
                v, slot = HEAD_SLOT[h]
                sink_h = sink_ref[h]
                for rc in range(0, BLK, ROW_CHUNK):
                    rows = slice(slot * BLK + rc, slot * BLK + rc + ROW_CHUNK)
                    s = s_scr[v, rows, :] + bias_s[kind, h, rc : rc + ROW_CHUNK, :]
                    m = jnp.maximum(jnp.max(s, axis=-1, keepdims=True), sink_h)
                    p = jnp.exp(s - m)
                    total = jnp.sum(p, axis=-1, keepdims=True) + jnp.exp(sink_h - m)
                    p_scr[v, rows, :] = _mx(p)
                    inv_scr[v, rows, :] = jnp.broadcast_to(1.0 / total, (ROW_CHUNK, LANES))
            outs = [_dot(p_scr[v], (vr if v else vs)[krows, :]) * inv_scr[v] for v in range(2)]
            for j in range(N_PAIRS):
                o_ref[pl.ds(r0, BLK), j * LANES : (j + 1) * LANES] = _unstack_pair(outs, j, lo)
            return carry

        lax.fori_loop(0, nb, block, 0)
        if riding:
            after_compute()

    prev, nxt = _halo_specs(tile, seq)
    vec = _full_spec((1, LANES))
    in_specs = [SMEM_SPEC, _row_spec(tile, D_QKV), prev, nxt, vec, vec]
    out_specs = [_row_spec(tile, D_ATTN)]
    out_shape = [jax.ShapeDtypeStruct((seq, D_ATTN), F32)]
    scratch = [
        pltpu.VMEM((nb, 2, STACK, LANES), MXU_DTYPE),
        pltpu.VMEM((ext, LANES), MXU_DTYPE),
        pltpu.VMEM((ext, LANES), MXU_DTYPE),
        pltpu.VMEM((ext, LANES), MXU_DTYPE),
        pltpu.VMEM((ext, LANES), MXU_DTYPE),
        pltpu.VMEM((3, N_HEADS, BLK, 3 * BLK), F32),
        pltpu.VMEM((2, STACK, 3 * BLK), F32),
        pltpu.VMEM((2, STACK, 3 * BLK), MXU_DTYPE),
        pltpu.VMEM((2, STACK, LANES), F32),
    ]
    extra = ()
    if riding:
        extra = tuple(gather)
        in_specs += [HBM_SPEC] * 2
        out_specs += [HBM_SPEC] * 2
        out_shape += [jax.ShapeDtypeStruct((N_DEV * g.shape[0], g.shape[1]), g.dtype) for g in gather]
        scratch += [pltpu.SemaphoreType.DMA((14,)), pltpu.SemaphoreType.DMA((14,)), pltpu.SemaphoreType.DMA((2,))]
    out = pl.pallas_call(
        body,
        name=name,
        grid=(nt,),
        in_specs=in_specs,
        out_specs=out_specs,
        out_shape=out_shape,
        scratch_shapes=scratch,
        compiler_params=_params(("arbitrary",)),
    )(sink, pa, pa, pa, q_gain2, k_gain2, *extra)
    return out if riding else out[0]


def _mix_out_fwd(pb, o, x, gate, w_out, w_s, b_st, name, target=None):
    seq, d = x.shape
    tile = min(TOKEN_TILE, seq)
    nb = tile // BLK
    with_loss = target is not None

    def body(pb_ref, o_ref, x_ref, gate_ref, wo_ref, ws_ref, bs_ref, *rest):
        if with_loss:
            t_ref, xo_ref, acc_ref, y_s, vn_s = rest

            @pl.when(pl.program_id(0) == 0)
            def _():
                acc_ref[...] = jnp.zeros_like(acc_ref)
        else:
            xo_ref, y_s, vn_s = rest
        ones = _half_ones(WIDE)
        lo = _lane_lo(BLK)
        ga = pb_ref[:, 0:D_ATTN]
        y_s[:, 0:D_ATTN] = _mx(o_ref[...] * (ga * _sigmoid(ga)))
        for j in range(D_GM // WIDE):
            vhat, _ = _half_rms(pb_ref[:, 2 * D_GM + j * WIDE : 2 * D_GM + (j + 1) * WIDE], ones)
            vn_s[:, j * WIDE : (j + 1) * WIDE] = _mx(vhat)

        def chunk(n, carry):
            rows = pl.ds(pl.multiple_of(n * BLK, BLK), BLK)
            for j in range(N_PAIRS):
                cols = slice(j * LANES, (j + 1) * LANES)
                vn = vn_s[rows, cols]
                sv = jnp.where(lo, _dot(ws_ref[2 * j], vn), _dot(ws_ref[2 * j + 1], vn)) + bs_ref[:, cols]
                u = pb_ref[rows, D_ATTN + j * LANES : D_ATTN + (j + 1) * LANES]
                gg = pb_ref[rows, D_ATTN + 2 * D_GM + j * LANES : D_ATTN + 2 * D_GM + (j + 1) * LANES]
                y_s[rows, D_ATTN + j * LANES : D_ATTN + (j + 1) * LANES] = _mx((u * sv) * (gg * _sigmoid(gg)))
            return carry

        lax.fori_loop(0, nb, chunk, 0)
        y = x_ref[...] + gate_ref[...] * _dot(y_s[...], wo_ref[...])
        if with_loss:
            e = y - t_ref[...]
            xo_ref[...] = e * (1.0 / d)
            acc_ref[...] += jnp.sum(jnp.sum(e * e, axis=-1, keepdims=True), axis=0, keepdims=True)
        else:
            xo_ref[...] = y

    row = _row_spec(tile, d)
    acc_shape = (SUBLANES, LANES)
    return pl.pallas_call(
        body,
        name=name,
        grid=(seq // tile,),
        in_specs=[
            _row_spec(tile, D_REST),
            _row_spec(tile, D_ATTN),
            row,
            _full_spec((1, d)),
            _full_spec((D_MIX, d)),
            _full_spec((N_GROUPS, BLK, BLK)),
            _full_spec((BLK, D_GM)),
        ]
        + ([row] if with_loss else []),
        out_specs=[row, _full_spec(acc_shape)] if with_loss else row,
        out_shape=[jax.ShapeDtypeStruct((seq, d), F32), jax.ShapeDtypeStruct(acc_shape, F32)]
        if with_loss
        else jax.ShapeDtypeStruct((seq, d), F32),
        scratch_shapes=[pltpu.VMEM((tile, D_MIX), MXU_DTYPE), pltpu.VMEM((tile, D_GM), MXU_DTYPE)],
        compiler_params=_params(("arbitrary",) if with_loss else ("parallel",)),
    )(pb, o, x, gate, w_out, w_s, b_st, *([target] if with_loss else []))


def _mix_out_bwd(dxn, pb, o, gate, w_out, w_s, w_s_t, b_st, name):
    seq, d = dxn.shape
    tile = min(TOKEN_TILE, seq)
    nb = tile // BLK
    nt = seq // tile

    def body(dxn_ref, pb_ref, o_ref, gate_ref, wo_ref, ws_ref, wst_ref, bs_ref,
             dpb_ref, do_ref, dwo_ref, dgate_ref, dws_ref, dbs_ref, g_ref, y_s, dy_s, vn_s, rv_s, vnb_s, sv_s, dsv_s, dvn_s):
        @pl.when(pl.program_id(0) == 0)
        def _():
            g_ref[...] = jnp.zeros_like(g_ref)
            dws_ref[...] = jnp.zeros_like(dws_ref)
            dbs_ref[...] = jnp.zeros_like(dbs_ref)

        ones = _half_ones(WIDE)
        lo = _lane_lo(BLK)
        c_u = slice(D_ATTN, D_ATTN + D_GM)
        c_vg = slice(D_ATTN + D_GM, D_ATTN + 2 * D_GM)
        c_gg = slice(D_ATTN + 2 * D_GM, D_REST)
        dxv = dxn_ref[...]
        dy_s[...] = _dot_nt(_mx(dxv * gate_ref[...]), wo_ref[...])
        ga = pb_ref[:, 0:D_ATTN]
        sig = _sigmoid(ga)
        sil = ga * sig
        ov = o_ref[...]
        y_s[:, 0:D_ATTN] = _mx(ov * sil)
        da = dy_s[:, 0:D_ATTN]
        do_ref[...] = da * sil
        dpb_ref[:, 0:D_ATTN] = (da * ov * (sig * (1.0 + ga * (1.0 - sig)))).astype(dpb_ref.dtype)
        for j in range(D_GM // WIDE):
            cols = slice(j * WIDE, (j + 1) * WIDE)
            vhat, rv = _half_rms(pb_ref[:, 2 * D_GM + j * WIDE : 2 * D_GM + (j + 1) * WIDE], ones)
            vn_s[:, cols] = vhat
            rv_s[:, cols] = rv
            vnb_s[:, cols] = _mx(vhat)

        def spatial_fwd(n, carry):
            rows = pl.ds(pl.multiple_of(n * BLK, BLK), BLK)
            for j in range(N_PAIRS):
                cols = slice(j * LANES, (j + 1) * LANES)
                vn = vnb_s[rows, cols]
                sv_s[rows, cols] = jnp.where(lo, _dot(ws_ref[2 * j], vn), _dot(ws_ref[2 * j + 1], vn)) + bs_ref[:, cols]
            return carry

        lax.fori_loop(0, nb, spatial_fwd, 0)

        def gating(n, carry):
            rows = pl.ds(pl.multiple_of(n * BLK, BLK), BLK)
            sv = sv_s[rows, :]
            u = pb_ref[rows, c_u]
            gg = pb_ref[rows, c_gg]
            sg = _sigmoid(gg)
            silg = gg * sg
            m0 = u * sv
            y_s[rows, D_ATTN:D_MIX] = _mx(m0 * silg)
            dm = dy_s[rows, D_ATTN:D_MIX]
            dm0 = dm * silg
            dpb_ref[rows, c_gg] = (dm * m0 * (sg * (1.0 + gg * (1.0 - sg)))).astype(dpb_ref.dtype)
            dpb_ref[rows, c_u] = (dm0 * sv).astype(dpb_ref.dtype)
            dsv = dm0 * u
            dsv_s[rows, :] = _mx(dsv)
            dbs_ref[...] += dsv
            return carry

        lax.fori_loop(0, nb, gating, 0)

        def spatial_bwd(n, carry):
            rows = pl.ds(pl.multiple_of(n * BLK, BLK), BLK)
            for j in range(N_PAIRS):
                cols = slice(j * LANES, (j + 1) * LANES)
                dsv = dsv_s[rows, cols]
                dvn_s[rows, cols] = jnp.where(lo, _dot(wst_ref[2 * j], dsv), _dot(wst_ref[2 * j + 1], dsv))
            return carry

        lax.fori_loop(0, nb, spatial_bwd, 0)
        zero = jnp.zeros((BLK, LANES), MXU_DTYPE)
        for j in range(N_PAIRS):
            cols = slice(j * LANES, (j + 1) * LANES)
            chunks = [dsv_s[n * BLK : (n + 1) * BLK, cols] for n in range(nb)]
            vn_all = jnp.concatenate([vnb_s[n * BLK : (n + 1) * BLK, cols] for n in range(nb)], axis=1)
            dws_ref[2 * j] += _dot_nt(jnp.concatenate([jnp.where(lo, c, zero) for c in chunks], axis=1), vn_all)
            dws_ref[2 * j + 1] += _dot_nt(jnp.concatenate([jnp.where(lo, zero, c) for c in chunks], axis=1), vn_all)
        for j in range(D_GM // WIDE):
            cols = slice(j * WIDE, (j + 1) * WIDE)
            dpb_ref[:, D_ATTN + D_GM + j * WIDE : D_ATTN + D_GM + (j + 1) * WIDE] = _half_rms_bwd(
                dvn_s[:, cols], vn_s[:, cols], rv_s[:, cols], ones
            ).astype(dpb_ref.dtype)
        g_ref[...] += _dot_tn(y_s[...], _mx(dxv))

        @pl.when(pl.program_id(0) == nt - 1)
        def _():
            gv = g_ref[...]
            dwo_ref[...] = (gv * gate_ref[...]).astype(dwo_ref.dtype)
            dgate_ref[...] = _group_rows(gv * wo_ref[...].astype(F32))

    return pl.pallas_call(
        body,
        name=name,
        grid=(seq // tile,),
        in_specs=[
            _row_spec(tile, d),
            _row_spec(tile, D_REST),
            _row_spec(tile, D_ATTN),
            _full_spec((1, d)),
            _full_spec((D_MIX, d)),
            _full_spec((N_GROUPS, BLK, BLK)),
            _full_spec((N_GROUPS, BLK, BLK)),
            _full_spec((BLK, D_GM)),
        ],
        out_specs=[
            _row_spec(tile, D_REST),
            _row_spec(tile, D_ATTN),
            _full_spec((D_MIX, d)),
            _full_spec((SUBLANES, d)),
            _full_spec((N_GROUPS, BLK, BLK)),
            _full_spec((BLK, D_GM)),
        ],
        out_shape=[
            jax.ShapeDtypeStruct((seq, D_REST), MXU_DTYPE),
            jax.ShapeDtypeStruct((seq, D_ATTN), F32),
            jax.ShapeDtypeStruct((D_MIX, d), jnp.bfloat16),
            jax.ShapeDtypeStruct((SUBLANES, d), F32),
            jax.ShapeDtypeStruct((N_GROUPS, BLK, BLK), F32),
            jax.ShapeDtypeStruct((BLK, D_GM), F32),
        ],
        scratch_shapes=[
            pltpu.VMEM((D_MIX, d), F32),
            pltpu.VMEM((tile, D_MIX), MXU_DTYPE),
            pltpu.VMEM((tile, D_MIX), F32),
            pltpu.VMEM((tile, D_GM), F32),
            pltpu.VMEM((tile, D_GM), F32),
            pltpu.VMEM((tile, D_GM), MXU_DTYPE),
            pltpu.VMEM((tile, D_GM), F32),
            pltpu.VMEM((tile, D_GM), MXU_DTYPE),
            pltpu.VMEM((tile, D_GM), F32),
        ],
        compiler_params=_params(("arbitrary",)),
    )(dxn, pb, o, gate, w_out, w_s, w_s_t, b_st)


def _attn_bwd(pa, o, do, q_gain2, k_gain2, sink, name, scatter=()):
    seq = pa.shape[0]
    tile = min(TOKEN_TILE, seq)
    nb = tile // BLK
    nt = seq // tile
    ext = tile + 2 * BLK
    n_ride = len(scatter)
    riding = n_ride > 0

    def body(sink_ref, qkv_ref, kvp_ref, kvn_ref, o_ref, do_ref, qg_ref, kg_ref, *rest):
        i = pl.program_id(0)
        blocks, rest = rest[:n_ride], rest[n_ride:]
        dq_ref, dkv_ref, hp_ref, hn_ref, dqg_ref, dkg_ref, dsk_ref = rest[:7]
        landing, rest = rest[7 : 7 + n_ride], rest[7 + n_ride :]
        (qs, dos, qhat_s, rq_s, ks, kr, vs, vr, khat_s, rk_s, dqn_s, dka, dva, bias_s, s_scr, dp_scr, p_scr, ds_scr) = rest[:18]
        if riding:
            start, finish = _scatter_stages(blocks, landing, *rest[18:])
            at_start, _, at_finish = _rider_steps(nt)
            pl.when(i == at_start)(start)

        @pl.when(i == 0)
        def _():
            dqg_ref[...] = jnp.zeros_like(dqg_ref)
            dkg_ref[...] = jnp.zeros_like(dkg_ref)
            dsk_ref[...] = jnp.zeros_like(dsk_ref)
            _fill_attn_bias(bias_s)

        ones = _half_ones()
        lo = _lane_lo(BLK)
        lo_t = _lane_lo(tile)
        lo_c = _lane_lo(ROW_CHUNK)
        qg = qg_ref[...] * Q_SCALE
        kg = kg_ref[...]
        _stage_keys(kvp_ref, qkv_ref, kvn_ref, kg, ones, tile, ks, kr, vs, vr, khat_s, rk_s)
        for j in range(N_PAIRS):
            cols = slice(j * LANES, (j + 1) * LANES)
            qhat, rq = _half_rms(qkv_ref[:, cols], ones)
            qhat_s[:, cols] = qhat
            rq_s[:, cols] = rq
            _stage_queries(qhat * qg, lo_t, j, nb, qs)
            _stage_queries(do_ref[:, cols], lo_t, j, nb, dos)
        dka[...] = jnp.zeros_like(dka)
        dva[...] = jnp.zeros_like(dva)
        head_lane = lax.broadcasted_iota(jnp.int32, (1, LANES), 1)

        def block(n, dsink):
            r0 = pl.multiple_of(n * BLK, BLK)
            krows = pl.ds(r0, 3 * BLK)
            kind = _block_kind(i * nb + n, seq)
            for v in range(2):
                s_scr[v] = _dot_nt(qs[n, v], (kr if v else ks)[krows, :])
                dp_scr[v] = _dot_nt(dos[n, v], (vr if v else vs)[krows, :])
            for h in range(N_HEADS):
                v, slot = HEAD_SLOT[h]
                j, a = divmod(h, 2)
                cols = slice(j * LANES, (j + 1) * LANES)
                sink_h = sink_ref[h]
                sink_part = jnp.zeros((ROW_CHUNK, 1), F32)
                for rc in range(0, BLK, ROW_CHUNK):
                    rows = slice(slot * BLK + rc, slot * BLK + rc + ROW_CHUNK)
                    trows = pl.ds(pl.multiple_of(r0 + rc, ROW_CHUNK), ROW_CHUNK)
                    s = s_scr[v, rows, :] + bias_s[kind, h, rc : rc + ROW_CHUNK, :]
                    m = jnp.maximum(jnp.max(s, axis=-1, keepdims=True), sink_h)
                    p = jnp.exp(s - m)
                    e_sink = jnp.exp(sink_h - m)
                    inv = 1.0 / (jnp.sum(p, axis=-1, keepdims=True) + e_sink)
                    pn = p * inv
                    prod = do_ref[trows, cols] * o_ref[trows, cols]
                    prod = jnp.where(lo_c, prod, 0.0) if a == 0 else jnp.where(lo_c, 0.0, prod)
                    dcol = jnp.sum(prod, axis=-1, keepdims=True)
                    ds_scr[v, rows, :] = _mx(pn * (dp_scr[v, rows, :] - dcol))
                    p_scr[v, rows, :] = _mx(pn)
                    sink_part = sink_part + (e_sink * inv) * dcol
                dsink = dsink - jnp.where(head_lane == h, jnp.sum(sink_part, axis=0, keepdims=True), 0.0)
            dqv = []
            for v in range(2):
                dqv.append(_dot(ds_scr[v], (kr if v else ks)[krows, :]))
                dka[v, krows, :] += _dot_tn(ds_scr[v], qs[n, v])
                dva[v, krows, :] += _dot_tn(p_scr[v], dos[n, v])
            for j in range(N_PAIRS):
                dqn_s[pl.ds(r0, BLK), j * LANES : (j + 1) * LANES] = _unstack_pair(dqv, j, lo)
            return dsink

        dsink = lax.fori_loop(0, nb, block, jnp.zeros((1, LANES), F32))
        dsk_ref[...] += jnp.broadcast_to(dsink, (SUBLANES, LANES))
        for j in range(N_PAIRS):
            cols = slice(j * LANES, (j + 1) * LANES)
            dqn = dqn_s[:, cols]
            qhat = qhat_s[:, cols]
            dqg_ref[:, cols] += _group_rows(dqn * qhat) * Q_SCALE
            dq_ref[:, cols] = _half_rms_bwd(dqn * qg, qhat, rq_s[:, cols], ones).astype(dq_ref.dtype)
        dkn = dka[0] + pltpu.roll(dka[1], HEAD_DIM, 1)
        khat = khat_s[...]
        dkg_ref[...] += _group_rows(dkn * khat)
        dk = _half_rms_bwd(dkn * kg, khat, rk_s[...], ones)
        dv = dva[0] + pltpu.roll(dva[1], HEAD_DIM, 1)
        hp_ref[:, 0:D_KV] = dk[0:BLK]
        hp_ref[:, D_KV : 2 * D_KV] = dv[0:BLK]
        dkv_ref[:, 0:D_KV] = dk[BLK : BLK + tile]
        dkv_ref[:, D_KV : 2 * D_KV] = dv[BLK : BLK + tile]
        hn_ref[:, 0:D_KV] = dk[BLK + tile : ext]
        hn_ref[:, D_KV : 2 * D_KV] = dv[BLK + tile : ext]
        if riding:
            pl.when(i == at_finish)(finish)

    prev, nxt = _halo_specs(tile, seq)
    vec = _full_spec((1, LANES))
    halo = pl.BlockSpec((None, BLK, 2 * D_KV), lambda i: (i, 0, 0))
    return pl.pallas_call(
        body,
        name=name,
        grid=(nt,),
        in_specs=[SMEM_SPEC, _row_spec(tile, D_QKV), prev, nxt, _row_spec(tile, D_ATTN), _row_spec(tile, D_ATTN), vec, vec]
        + [HBM_SPEC] * n_ride,
        out_specs=[
            _row_spec(tile, D_ATTN),
            _row_spec(tile, 2 * D_KV),
            halo,
            halo,
            _full_spec((SUBLANES, D_ATTN)),
            _full_spec((SUBLANES, LANES)),
            _full_spec((SUBLANES, LANES)),
        ]
        + [HBM_SPEC] * n_ride,
        out_shape=[
            jax.ShapeDtypeStruct((seq, D_ATTN), MXU_DTYPE),
            jax.ShapeDtypeStruct((seq, 2 * D_KV), F32),
            jax.ShapeDtypeStruct((nt, BLK, 2 * D_KV), F32),
            jax.ShapeDtypeStruct((nt, BLK, 2 * D_KV), F32),
            jax.ShapeDtypeStruct((SUBLANES, D_ATTN), F32),
            jax.ShapeDtypeStruct((SUBLANES, LANES), F32),
            jax.ShapeDtypeStruct((SUBLANES, LANES), F32),
        ]
        + _landing_shapes(scatter),
        scratch_shapes=[
            pltpu.VMEM((nb, 2, STACK, LANES), MXU_DTYPE),
            pltpu.VMEM((nb, 2, STACK, LANES), MXU_DTYPE),
            pltpu.VMEM((tile, D_ATTN), F32),
            pltpu.VMEM((tile, D_ATTN), F32),
            pltpu.VMEM((ext, LANES), MXU_DTYPE),
            pltpu.VMEM((ext, LANES), MXU_DTYPE),
            pltpu.VMEM((ext, LANES), MXU_DTYPE),
            pltpu.VMEM((ext, LANES), MXU_DTYPE),
            pltpu.VMEM((ext, LANES), F32),
            pltpu.VMEM((ext, LANES), F32),
            pltpu.VMEM((tile, D_ATTN), F32),
            pltpu.VMEM((2, ext, LANES), F32),
            pltpu.VMEM((2, ext, LANES), F32),
            pltpu.VMEM((3, N_HEADS, BLK, 3 * BLK), F32),
            pltpu.VMEM((2, STACK, 3 * BLK), F32),
            pltpu.VMEM((2, STACK, 3 * BLK), F32),
            pltpu.VMEM((2, STACK, 3 * BLK), MXU_DTYPE),
            pltpu.VMEM((2, STACK, 3 * BLK), MXU_DTYPE),
        ]
        + _rider_sems(n_ride),
        compiler_params=_params(("arbitrary",)),
    )(sink, pa, pa, pa, o, do, q_gain2, k_gain2, *scatter)


def _halo_in_specs(tile, nt):
    from_prev = pl.BlockSpec((None, BLK, 2 * D_KV), lambda i: (jnp.maximum(i - 1, 0), 0, 0))
    from_next = pl.BlockSpec((None, BLK, 2 * D_KV), lambda i: (jnp.minimum(i + 1, nt - 1), 0, 0))
    return from_prev, from_next


def _landing_shapes(scatter):
    return [jax.ShapeDtypeStruct((N_DEV,) + b.shape[2:], b.dtype) for b in scatter]


def _rider_sems(n_ride):
    if not n_ride:
        return []
    return [pltpu.SemaphoreType.DMA((7 * n_ride,)), pltpu.SemaphoreType.DMA((7 * n_ride,)), pltpu.SemaphoreType.DMA((n_ride,))]


def _proj_bwd_dx(x, dxn, dq, dkvb, dpb, w_in_t, gain, scale1, name, scatter=()):
    seq, d = x.shape
    tile = min(TOKEN_TILE, seq)
    nt = seq // tile
    n_ride = len(scatter)

    def row(width):
        return _row_spec(tile, width)

    def body(x_ref, dxn_ref, dq_ref, dkvb_ref, dpb_ref, wt_ref, g_ref, s1_ref, *rest):
        i = pl.program_id(0)
        blocks, rest = rest[:n_ride], rest[n_ride:]
        dx_ref, c0_ref, c1_ref = rest[:3]
        landing, sems = rest[3 : 3 + n_ride], rest[3 + n_ride :]
        if n_ride:
            start, finish = _scatter_stages(blocks, landing, *sems)
            at_start, _, at_finish = _rider_steps(nt)
            pl.when(i == at_start)(start)

        @pl.when(i == 0)
        def _():
            c0_ref[...] = jnp.zeros_like(c0_ref)
            c1_ref[...] = jnp.zeros_like(c1_ref)

        dh = (
            _dot(dq_ref[...], wt_ref[0:D_ATTN, :])
            + _dot(dkvb_ref[...], wt_ref[D_ATTN:D_QKV, :])
            + _dot(dpb_ref[...], wt_ref[D_QKV:D_IN, :])
        )
        xv = x_ref[...]
        r = lax.rsqrt(jnp.mean(xv * xv, axis=-1, keepdims=True) + EPS)
        xn = xv * r
        c0_ref[...] += _group_rows(dh)
        c1_ref[...] += _group_rows(dh * xn)
        dxn_ = dh * (g_ref[...] * s1_ref[...])
        dx_ref[...] = dxn_ref[...] + r * (dxn_ - xn * jnp.mean(xn * dxn_, axis=-1, keepdims=True))
        if n_ride:
            pl.when(i == at_finish)(finish)

    vec = _full_spec((1, d))
    return pl.pallas_call(
        body,
        name=name,
        grid=(nt,),
        in_specs=[row(d), row(d), row(D_ATTN), row(2 * D_KV), row(D_REST), _full_spec((D_IN, d)), vec, vec]
        + [HBM_SPEC] * n_ride,
        out_specs=[row(d), _full_spec((SUBLANES, d)), _full_spec((SUBLANES, d))] + [HBM_SPEC] * n_ride,
        out_shape=[
            jax.ShapeDtypeStruct((seq, d), F32),
            jax.ShapeDtypeStruct((SUBLANES, d), F32),
            jax.ShapeDtypeStruct((SUBLANES, d), F32),
        ]
        + _landing_shapes(scatter),
        scratch_shapes=_rider_sems(n_ride),
        compiler_params=_params(("arbitrary",)),
    )(x, dxn, dq, dkvb, dpb, w_in_t, gain, scale1, *scatter)


def _proj_bwd_dw(x, gain, scale1, shift, dq, dkv, halo_prev, halo_next, dpb, name, gather=()):
    seq, d = x.shape
    tile = min(TOKEN_TILE, seq)
    nt = seq // tile
    assert tile >= 2 * BLK
    n_ride = len(gather)

    def body(x_ref, g_ref, s1_ref, sh_ref, dq_ref, dkv_ref, hn_ref, hp_ref, dpb_ref, *rest):
        i = pl.program_id(0)
        sources, rest = rest[:n_ride], rest[n_ride:]
        dw_ref, dkvb_ref = rest[:2]
        gathered, (acc, *sems) = rest[2 : 2 + n_ride], rest[2 + n_ride :]
        after_compute = _gather_rider(sources, gathered, sems, i, nt) if n_ride else None

        @pl.when(i == 0)
        def _():
            acc[...] = jnp.zeros_like(acc)

        top = dkv_ref[0:BLK, :] + jnp.where(i > 0, hn_ref[...], 0.0)
        bot = dkv_ref[tile - BLK : tile, :] + jnp.where(i < nt - 1, hp_ref[...], 0.0)
        dkvb_ref[0:BLK, :] = top.astype(dkvb_ref.dtype)
        dkvb_ref[tile - BLK : tile, :] = bot.astype(dkvb_ref.dtype)
        if tile > 2 * BLK:
            dkvb_ref[BLK : tile - BLK, :] = dkv_ref[BLK : tile - BLK, :].astype(dkvb_ref.dtype)
        xv = x_ref[...]
        r = lax.rsqrt(jnp.mean(xv * xv, axis=-1, keepdims=True) + EPS)
        h = _mx((xv * r) * g_ref[...] * s1_ref[...] + sh_ref[...])
        acc[0:D_ATTN, :] += _dot_tn(dq_ref[...], h)
        acc[D_ATTN:D_QKV, :] += _dot_tn(dkvb_ref[...], h)
        acc[D_QKV:D_IN, :] += _dot_tn(dpb_ref[...], h)

        @pl.when(i == nt - 1)
        def _():
            dw_ref[...] = acc[...].astype(dw_ref.dtype)

        if n_ride:
            after_compute()

    from_prev, from_next = _halo_in_specs(tile, nt)
    vec = _full_spec((1, d))
    return pl.pallas_call(
        body,
        name=name,
        grid=(nt,),
        in_specs=[
            _row_spec(tile, d),
            vec,
            vec,
            vec,
            _row_spec(tile, D_ATTN),
            _row_spec(tile, 2 * D_KV),
            from_prev,
            from_next,
            _row_spec(tile, D_REST),
        ]
        + [HBM_SPEC] * n_ride,
        out_specs=[_full_spec((D_IN, d)), _row_spec(tile, 2 * D_KV)] + [HBM_SPEC] * n_ride,
        out_shape=[jax.ShapeDtypeStruct((D_IN, d), jnp.bfloat16), jax.ShapeDtypeStruct((seq, 2 * D_KV), MXU_DTYPE)]
        + _gathered_shapes(gather),
        scratch_shapes=[pltpu.VMEM((D_IN, d), F32)] + _rider_sems(n_ride),
        compiler_params=_params(("arbitrary",)),
    )(x, gain, scale1, shift, dq, dkv, halo_next, halo_prev, dpb, *gather)


def _adamw_math(w, g, m, v):
    m = ADAM_B1 * m + (1.0 - ADAM_B1) * g
    v = ADAM_B2 * v + (1.0 - ADAM_B2) * (g * g)
    m_hat = m / (1.0 - ADAM_B1**ADAM_STEP)
    v_hat = v / (1.0 - ADAM_B2**ADAM_STEP)
    delta = -ADAM_LR * (m_hat / (jnp.sqrt(v_hat) + ADAM_EPS) + ADAM_WD * w)
    return delta, m, v


def _small_update(gathered, gathered_ws, w, m, v, ws, m_ws, v_ws):
    def body(ga_ref, gws_ref, w_ref, m_ref, v_ref, ws_ref, mws_ref, vws_ref, *outs):
        for src, refs, out in ((ga_ref, (w_ref, m_ref, v_ref), outs[0:4]), (gws_ref, (ws_ref, mws_ref, vws_ref), outs[4:8])):
            g = src[0].astype(F32)
            for j in range(1, N_DEV):
                g = g + src[j].astype(F32)
            out[0][...] = g
            out[1][...], out[2][...], out[3][...] = _adamw_math(refs[0][...], g, refs[1][...], refs[2][...])

    shapes = [jax.ShapeDtypeStruct(w.shape, F32)] * 4 + [jax.ShapeDtypeStruct(ws.shape, F32)] * 4
    return pl.pallas_call(
        body,
        name="small_update",
        in_specs=[VMEM_SPEC] * 8,
        out_specs=[VMEM_SPEC] * 8,
        out_shape=shapes,
        compiler_params=_params(),
    )(gathered, gathered_ws, w, m, v, ws, m_ws, v_ws)


def _ada_update(c_all, d_ada_cols, w, m, v):
    n_layers = w.shape[0]

    def body(c_ref, da_ref, w_ref, m_ref, v_ref, g_ref, d_ref, mo_ref, vo_ref):
        cv = c_ref[...]
        cond = cv * _sigmoid(cv)
        for l in range(n_layers):
            g = lax.dot_general(
                cond, da_ref[l], (((0,), (0,)), ((), ())), preferred_element_type=F32, precision=lax.Precision.HIGHEST
            )
            g_ref[l] = g
            d_ref[l], mo_ref[l], vo_ref[l] = _adamw_math(w_ref[l], g, m_ref[l], v_ref[l])

    return pl.pallas_call(
        body,
        name="ada_update",
        in_specs=[VMEM_SPEC] * 5,
        out_specs=[VMEM_SPEC] * 4,
        out_shape=[jax.ShapeDtypeStruct(w.shape, F32)] * 4,
        compiler_params=_params(),
    )(c_all, d_ada_cols, w, m, v)


def _position():
    return lax.axis_index("x"), lax.axis_index("y"), lax.axis_index("c")


def _flip(pos, k):
    x, y, c = pos
    return (1 - x if k & 4 else x, 1 - y if k & 2 else y, 1 - c if k & 1 else c)


def _index(pos):
    x, y, c = pos
    return 4 * x + 2 * y + c


def _remote(src, dst, send_sem, recv_sem, to):
    return pltpu.make_async_remote_copy(
        src_ref=src, dst_ref=dst, send_sem=send_sem, recv_sem=recv_sem, device_id=to, device_id_type=MESH_ID
    )


def _all_gather_stages(slots, send_sems, recv_sems, sources=None, local_sems=None):
    me = _position()
    sibling = _flip(me, 1)
    others = (4, 2, 6)
    arrays = range(len(slots))

    def copy(t, k, block, to, own=False):
        slot = slots[t](_index(block))
        src = sources[t] if own and sources is not None else slot
        return _remote(src, slot, send_sems.at[7 * t + k], recv_sems.at[7 * t + k], to)

    def first(t):
        return [copy(t, 0, me, sibling, own=True)] + [copy(t, 1 + j, me, _flip(me, f), own=True) for j, f in enumerate(others)]

    def passed(t, j):
        return copy(t, 4 + j, _flip(me, others[j]), sibling)

    def local(t):
        return pltpu.make_async_copy(sources[t], slots[t](_index(me)), local_sems.at[t])

    def start():
        for t in arrays:
            if sources is not None:
                local(t).start()
            for cp in first(t):
                cp.start()

    def forward():
        for j, f in enumerate(others):
            for t in arrays:
                copy(t, 1 + j, _flip(me, f), me).wait_recv()
                passed(t, j).start()

    def finish():
        for t in arrays:
            copy(t, 0, sibling, me).wait_recv()
            for j, f in enumerate(others):
                copy(t, 4 + j, _flip(sibling, f), me).wait_recv()
        for t in arrays:
            for cp in first(t) + [passed(t, j) for j in range(len(others))]:
                cp.wait_send()
            if sources is not None:
                local(t).wait()

    return start, forward, finish


def _two_level_all_gather(slots, send_sems, recv_sems, between=None):
    start, forward, finish = _all_gather_stages(slots, send_sems, recv_sems)
    start()
    if between is not None:
        between()
    forward()
    finish()


def _row_block(ref, rows):
    return lambda j: ref.at[pl.ds(pl.multiple_of(j * rows, 16), rows), :]


def _scatter_stages(blocks, landing, send_sems, recv_sems, local_sems):
    me = _position()
    my = _index(me)
    arrays = range(len(blocks))

    def copy(t, k):
        px, py, pc = to = _flip(me, k)
        return _remote(blocks[t].at[2 * px + py, pc], landing[t].at[my], send_sems.at[7 * t + k - 1], recv_sems.at[7 * t + k - 1], to)

    def arrival(t, k):
        slot = landing[t].at[_index(_flip(me, k))]
        return _remote(slot, slot, send_sems.at[7 * t + k - 1], recv_sems.at[7 * t + k - 1], _flip(me, k))

    def local(t):
        x, y, c = me
        return pltpu.make_async_copy(blocks[t].at[2 * x + y, c], landing[t].at[my], local_sems.at[t])

    def start():
        for t in arrays:
            local(t).start()
            for k in range(1, N_DEV):
                copy(t, k).start()

    def finish():
        for t in arrays:
            for k in range(1, N_DEV):
                arrival(t, k).wait_recv()
        for t in arrays:
            for k in range(1, N_DEV):
                copy(t, k).wait_send()
            local(t).wait()

    return start, finish


def _ada_exchange(c_ref, w_ref, call_ref, parts_ref, sbuf, sem_s1, sem_r1, sem_s2, sem_r2):
    d = c_ref.shape[-1]
    n_layers = w_ref.shape[0]
    me = _position()
    my = _index(me)
    call_ref[my] = jnp.broadcast_to(c_ref[...], (SUBLANES, d))
    mine = call_ref.at[my]
    first = [_remote(mine, mine, sem_s1.at[k - 1], sem_r1.at[k - 1], _flip(me, k)) for k in range(1, N_DEV)]
    for cp in first:
        cp.start()
    for k in range(1, N_DEV):
        theirs = call_ref.at[_index(_flip(me, k))]
        _remote(theirs, theirs, sem_s1.at[k - 1], sem_r1.at[k - 1], _flip(me, k)).wait_recv()
    cv = call_ref[...].reshape(N_DEV * SUBLANES, d)
    cond = cv * _sigmoid(cv)
    for l in range(n_layers):
        rows = jnp.dot(cond, w_ref[l], preferred_element_type=F32, precision=lax.Precision.HIGHEST)
        for b in range(N_DEV):
            sbuf[b, l] = rows[b * SUBLANES : (b + 1) * SUBLANES]
    parts_ref[my] = sbuf[my]
    second = []
    for k in range(1, N_DEV):
        to = _flip(me, k)
        second.append(_remote(sbuf.at[_index(to)], parts_ref.at[my], sem_s2.at[k - 1], sem_r2.at[k - 1], to))
    for cp in second:
        cp.start()
    for k in range(1, N_DEV):
        theirs = parts_ref.at[_index(_flip(me, k))]
        _remote(theirs, theirs, sem_s2.at[k - 1], sem_r2.at[k - 1], _flip(me, k)).wait_recv()
    for cp in first + second:
        cp.wait_send()


def _gather_weights(w_in_t, w_out, c_row, w_ada):
    n_layers, rows_in, d = w_in_t.shape
    width = w_ada.shape[2]

    def body(wi_ref, wo_ref, c_ref, wa_ref, gi_ref, si_ref, so_ref, call_ref, parts_ref, sbuf, send_sems, recv_sems, *ada_sems):
        my = _index(_position())
        si_ref[...] = wi_ref[...].astype(si_ref.dtype)
        so_ref[...] = wo_ref[...].astype(so_ref.dtype)
        gi_ref[pl.ds(pl.multiple_of(my * rows_in, 16), rows_in), :] = si_ref[0]
        _two_level_all_gather(
            (_row_block(gi_ref, rows_in),),
            send_sems,
            recv_sems,
            between=functools.partial(_ada_exchange, c_ref, wa_ref, call_ref, parts_ref, sbuf, *ada_sems),
        )

    return pl.pallas_call(
        body,
        name="gather_weights",
        in_specs=[VMEM_SPEC] * 4,
        out_specs=[VMEM_SPEC] * 5,
        out_shape=[
            jax.ShapeDtypeStruct((N_DEV * rows_in, d), MXU_DTYPE),
            jax.ShapeDtypeStruct(w_in_t.shape, MXU_DTYPE),
            jax.ShapeDtypeStruct(w_out.shape, MXU_DTYPE),
            jax.ShapeDtypeStruct((N_DEV, SUBLANES, d), F32),
            jax.ShapeDtypeStruct((N_DEV, n_layers, SUBLANES, width), F32),
        ],
        scratch_shapes=[
            pltpu.VMEM((N_DEV, n_layers, SUBLANES, width), F32),
            pltpu.SemaphoreType.DMA((7,)),
            pltpu.SemaphoreType.DMA((7,)),
        ]
        + [pltpu.SemaphoreType.DMA((N_DEV - 1,))] * 4,
        compiler_params=_params(),
    )(w_in_t, w_out, c_row, w_ada)


def _gather_small(packed, adam=()):
    n_adam = len(adam)

    def body(p_ref, *rest):
        quads = [rest[4 * t : 4 * t + 4] for t in range(n_adam)]
        rest = rest[4 * n_adam :]
        g_ref = rest[0]
        results = [rest[1 + 3 * t : 4 + 3 * t] for t in range(n_adam)]
        send_sems, recv_sems = rest[1 + 3 * n_adam :]
        g_ref[_index(_position())] = p_ref[...]

        def updates():
            for (w_ref, gr_ref, m_ref, v_ref), (d_ref, mo_ref, vo_ref) in zip(quads, results):
                d_ref[...], mo_ref[...], vo_ref[...] = _adamw_math(w_ref[...], gr_ref[...], m_ref[...], v_ref[...])

        _two_level_all_gather((lambda j: g_ref.at[j],), send_sems, recv_sems, between=updates)

    return pl.pallas_call(
        body,
        name="gather_small",
        in_specs=[VMEM_SPEC] * (1 + 4 * n_adam),
        out_specs=[VMEM_SPEC] * (1 + 3 * n_adam),
        out_shape=[jax.ShapeDtypeStruct((N_DEV,) + packed.shape, F32)]
        + [jax.ShapeDtypeStruct(q[0].shape, F32) for q in adam for _ in range(3)],
        scratch_shapes=[pltpu.SemaphoreType.DMA((7,)), pltpu.SemaphoreType.DMA((7,))],
        compiler_params=_params(),
    )(packed, *[a for q in adam for a in q])


def _scatter_finish(landed, name):
    n = len(landed)

    def body(*refs):
        for src, out in zip(refs[:n], refs[n:]):
            g = src[0].astype(F32)
            for j in range(1, N_DEV):
                g = g + src[j].astype(F32)
            out[...] = g

    return pl.pallas_call(
        body,
        name=name,
        in_specs=[VMEM_SPEC] * n,
        out_specs=[VMEM_SPEC] * n,
        out_shape=[jax.ShapeDtypeStruct(a.shape[1:], F32) for a in landed],
        compiler_params=_params(),
    )(*landed)


def _pack_rows(parts):
    rows, offsets, at = [], [], 0
    for p in parts:
        flat = p.reshape(-1)
        n = -(-flat.shape[0] // (SUBLANES * LANES)) * SUBLANES
        rows.append(jnp.pad(flat, (0, n * LANES - flat.shape[0])).reshape(n, LANES))
        offsets.append(at)
        at += n
    return jnp.concatenate(rows, axis=0), offsets


def _unpack_rows(packed, offsets, shapes):
    out = []
    for off, shape in zip(offsets, shapes):
        size = 1
        for s in shape:
            size *= s
        n = -(-size // (SUBLANES * LANES)) * SUBLANES
        out.append(packed[off : off + n].reshape(-1)[:size].reshape(shape))
    return out


def kernel(x, c, w_ada, b_ada, norm_gain, w_in, q_gain, k_gain, sink, w_s, b_s, w_out, loss_target, m_w_ada, m_b_ada, m_norm_gain, m_w_in, m_q_gain, m_k_gain, m_sink, m_w_s, m_b_s, m_w_out, v_w_ada, v_b_ada, v_norm_gain, v_w_in, v_q_gain, v_k_gain, v_sink, v_w_s, v_b_s, v_w_out):
    seq, d = x.shape[1], x.shape[2]
    n_layers = w_in.shape[0]
    w_cols = w_in.shape[2]
    ada_cols = w_ada.shape[2]
    my = _index(_position())
    xs = x.reshape(seq, d)
    target = loss_target.reshape(seq, d)

    w_in_t0, shard_in, shard_out, c_all, ada_parts = _gather_weights(w_in.transpose(0, 2, 1), w_out, c, w_ada)
    w_in_ts, w_outs = [w_in_t0], []
    ada = ada_parts[:, :, 0, :].transpose(1, 0, 2).reshape(n_layers, 3 * d) + b_ada
    shift, scale1, gate = ada[:, None, 0:d], 1.0 + ada[:, None, d : 2 * d], ada[:, None, 2 * d : 3 * d]
    gain = norm_gain[:, None, :]

    w_s_m = w_s.astype(MXU_DTYPE)
    w_s_t = w_s_m.transpose(0, 1, 3, 2)
    b_st = jnp.repeat(b_s.transpose(0, 2, 1), HEAD_DIM, axis=2)
    q_gain2 = jnp.tile(q_gain, (1, 2))[:, None, :]
    k_gain2 = jnp.tile(k_gain, (1, 2))[:, None, :]

    xl, saved = xs, []
    for l in range(n_layers):
        last = l == n_layers - 1
        pa, pb, *first_w_out = _ln_proj_fwd(
            xl, gain[l], scale1[l], shift[l], w_in_ts[l], f"ln_proj_fwd_{l}", gather=(shard_out[0],) if l == 0 else ()
        )
        w_outs += first_w_out
        if last:
            o = _attn_fwd(pa, q_gain2[l], k_gain2[l], sink[l], f"attn_fwd_{l}")
        else:
            o, w_in_next, w_out_next = _attn_fwd(
                pa, q_gain2[l], k_gain2[l], sink[l], f"attn_fwd_{l}", gather=(shard_in[l + 1], shard_out[l + 1])
            )
            w_in_ts.append(w_in_next)
            w_outs.append(w_out_next)
        saved.append((xl, pa, pb, o))
        out = _mix_out_fwd(pb, o, xl, gate[l], w_outs[l], w_s_m[l], b_st[l], f"mix_out_fwd_{l}", target if last else None)
        if last:
            dx, sq_err = out
        else:
            xl = out

    g_w_in, g_w_out, small, d_ada_rows = [None] * n_layers, [None] * n_layers, [None] * n_layers, [None] * n_layers
    waiting = []
    d_ws_all = [None] * n_layers
    for l in reversed(range(n_layers)):
        x_l, pa, pb, o = saved[l]
        dpb, do, dw_out, d_gate8, d_ws, d_bs = _mix_out_bwd(
            dx, pb, o, gate[l], w_outs[l], w_s_m[l], w_s_t[l], b_st[l], f"mix_out_bwd_{l}"
        )
        riding = waiting + [(g_w_out, l, dw_out.reshape(4, 2, D_MIX // N_DEV, d))]
        attn = _attn_bwd(pa, o, do, q_gain2[l], k_gain2[l], sink[l], f"attn_bwd_{l}", scatter=tuple(b for _, _, b in riding))
        dq, dkv, halo_prev, halo_next, d_qg, d_kg, d_sk = attn[:7]
        for (dest, layer, _), total in zip(riding, _scatter_finish(attn[7:], f"scatter_finish_{l}")):
            dest[layer] = total.transpose(1, 0) if dest is g_w_in else total
        d_ws_all[l] = d_ws
        dw_args = (x_l, gain[l], scale1[l], shift[l], dq, dkv, halo_prev, halo_next, dpb, f"proj_bwd_dw_{l}")
        if l > 0:
            dw_in_t, dkvb = _proj_bwd_dw(*dw_args)
        else:
            d_ws_wire = jnp.stack(d_ws_all).reshape(-1, LANES).astype(jnp.bfloat16)
            dw_in_t, dkvb, gathered_ws = _proj_bwd_dw(*dw_args, gather=(d_ws_wire,))
        blocks_in = dw_in_t.reshape(4, 2, w_cols, d)
        waiting = [(g_w_in, l, blocks_in)] if l > 0 else []
        dxs = _proj_bwd_dx(
            x_l, dx, dq, dkvb, dpb, w_in_ts[l], gain[l], scale1[l], f"proj_bwd_dx_{l}", scatter=() if l > 0 else (blocks_in,)
        )
        dx, c0, c1 = dxs[:3]
        if l == 0:
            g_w_in[l] = _scatter_finish(dxs[3:], "scatter_finish_in_0")[0].transpose(1, 0)
        c0s, c1s = c0.sum(axis=0), c1.sum(axis=0)
        d_ada_rows[l] = jnp.concatenate([c0s, norm_gain[l] * c1s, d_gate8.sum(axis=0)])
        small[l] = (
            scale1[l, 0] * c1s,
            d_qg.sum(axis=0).reshape(N_HEADS, HEAD_DIM).sum(axis=0),
            d_kg.sum(axis=0).reshape(2, HEAD_DIM).sum(axis=0),
            d_sk[0, 0:N_HEADS],
            d_bs.reshape(BLK, N_GROUPS, HEAD_DIM).sum(axis=2).transpose(1, 0),
        )

    names = ("norm_gain", "q_gain", "k_gain", "sink", "b_s")
    stacked = [jnp.stack([small[l][t] for l in range(n_layers)]) for t in range(len(names))]
    d_ada = jnp.stack(d_ada_rows)
    packed, offsets = _pack_rows(stacked + [d_ada, sq_err[0, 0:1]])
    g_w_in, g_w_out = jnp.stack(g_w_in), jnp.stack(g_w_out)
    gathered, *upd = _gather_small(packed, adam=((w_in, g_w_in, m_w_in, v_w_in), (w_out, g_w_out, m_w_out, v_w_out)))
    gathered_ws = gathered_ws.reshape(N_DEV, -1, LANES)
    upd_in, upd_out = upd[0:3], upd[3:6]
    no_weight = jnp.zeros((1,), F32)
    weights = (norm_gain, q_gain, k_gain, sink, b_s, b_ada, no_weight)
    moments_m = (m_norm_gain, m_q_gain, m_k_gain, m_sink, m_b_s, m_b_ada, no_weight)
    moments_v = (v_norm_gain, v_q_gain, v_k_gain, v_sink, v_b_s, v_b_ada, no_weight)
    w_pack, _ = _pack_rows(weights)
    m_pack, _ = _pack_rows(moments_m)
    v_pack, _ = _pack_rows(moments_v)
    shapes = [w.shape for w in weights]
    flat_ws = lambda a: a.reshape(-1, LANES)
    updated = _small_update(gathered, gathered_ws, w_pack, m_pack, v_pack, flat_ws(w_s), flat_ws(m_w_s), flat_ws(v_w_s))
    g_small, d_small, m_small, v_small = (_unpack_rows(p, offsets, shapes) for p in updated[0:4])
    ws_small = [p.reshape(w_s.shape) for p in updated[4:8]]
    loss = g_small[-1][0] * (0.5 / d)

    ada_off = offsets[-2]
    ada_n = -(-n_layers * 3 * d // (SUBLANES * LANES)) * SUBLANES
    d_ada_all = gathered[:, ada_off : ada_off + ada_n].reshape(N_DEV, -1)[:, : n_layers * 3 * d].reshape(N_DEV, n_layers, 3 * d)
    d_ada_cols = lax.dynamic_slice_in_dim(d_ada_all, my * ada_cols, ada_cols, axis=2)
    g_w_ada, *upd_ada = _ada_update(c_all[:, 0, :], d_ada_cols.transpose(1, 0, 2), w_ada, m_w_ada, v_w_ada)

    def ordered(ada_, in_, out_, small_, ws):
        ng, qg, kg, sk, bs, ba, _ = small_
        return (ada_, ba, ng, in_, qg, kg, sk, ws, bs, out_)

    grads = ordered(g_w_ada, g_w_in, g_w_out, g_small, ws_small[0])
    deltas = ordered(upd_ada[0], upd_in[0], upd_out[0], d_small, ws_small[1])
    new_m = ordered(upd_ada[1], upd_in[1], upd_out[1], m_small, ws_small[2])
    new_v = ordered(upd_ada[2], upd_in[2], upd_out[2], v_small, ws_small[3])
    return (loss, dx.reshape(x.shape), *grads, *deltas, *new_m, *new_v)
```

```python
import functools

import jax
import jax.numpy as jnp
from jax import lax
from jax.experimental import pallas as pl
from jax.experimental.pallas import tpu as pltpu

F32 = jnp.float32
MXU_DTYPE = jnp.bfloat16
MESH_ID = pl.DeviceIdType.MESH

N_DEV = 8
HEAD_DIM = 64
N_HEADS = 8
Q_PER_KV = 4
D_ATTN = 512
D_KV = 128
D_GM = 512
N_GROUPS = 8
D_MIX = D_ATTN + D_GM
BLK = 128
LANES = 128
SUBLANES = 8
N_PAIRS = D_ATTN // LANES
D_QKV = D_ATTN + 2 * D_KV
D_REST = D_ATTN + 3 * D_GM
D_IN = D_QKV + D_REST
EPS = 1e-6
NEG_INF = -1e30
ALIBI_SLOPES = tuple(2.0 ** (-8.0 * (h + 1) / N_HEADS) for h in range(N_HEADS))
Q_SCALE = 1.0 / 8.0

ADAM_LR = 0.001
ADAM_B1 = 0.9
ADAM_B2 = 0.999
ADAM_EPS = 1e-08
ADAM_WD = 0.01
ADAM_STEP = 10

TOKEN_TILE = 512
VMEM_LIMIT_BYTES = 56 * 1024 * 1024


def _params(semantics=None):
    return pltpu.CompilerParams(dimension_semantics=semantics, vmem_limit_bytes=VMEM_LIMIT_BYTES)


def _dot(a, b):
    return jnp.dot(a, b, preferred_element_type=F32)


def _dot_nt(a, b):
    return lax.dot_general(a, b, (((1,), (1,)), ((), ())), preferred_element_type=F32)


def _dot_tn(a, b):
    return lax.dot_general(a, b, (((0,), (0,)), ((), ())), preferred_element_type=F32)


def _mx(v):
    return v.astype(MXU_DTYPE)


def _lane_lo(rows):
    return lax.broadcasted_iota(jnp.int32, (rows, LANES), 1) < HEAD_DIM


def _half_ones(width=LANES):
    group_bits = HEAD_DIM.bit_length() - 1
    r = jnp.right_shift(lax.broadcasted_iota(jnp.int32, (width, width), 0), group_bits)
    c = jnp.right_shift(lax.broadcasted_iota(jnp.int32, (width, width), 1), group_bits)
    return jnp.where(r == c, 1.0, 0.0).astype(jnp.bfloat16)


WIDE = 2 * LANES


def _half_sum(v, ones):
    p1 = v.astype(jnp.bfloat16)
    p2 = (v - p1.astype(F32)).astype(jnp.bfloat16)
    return _dot(p1, ones) + _dot(p2, ones)


def _half_rms(v, ones):
    r = lax.rsqrt(_half_sum(v * v, ones) * (1.0 / HEAD_DIM) + EPS)
    return v * r, r


def _half_rms_bwd(dy, vhat, r, ones):
    return r * (dy - vhat * (_half_sum(vhat * dy, ones) * (1.0 / HEAD_DIM)))


def _group_rows(v):
    rows, n = v.shape
    return v.reshape(rows // SUBLANES, SUBLANES, n).sum(axis=0)


def _sigmoid(v):
    return 1.0 / (1.0 + jnp.exp(-v))


ROW_CHUNK = 32
VARIANT_HEADS = ((0, 2, 5, 7), (1, 3, 4, 6))
HEAD_SLOT = {h: (v, s) for v, heads in enumerate(VARIANT_HEADS) for s, h in enumerate(heads)}
STACK = Q_PER_KV * BLK


def _fill_attn_bias(bias_s):
    qi = lax.broadcasted_iota(jnp.int32, (BLK, 3 * BLK), 0)
    ci = lax.broadcasted_iota(jnp.int32, (BLK, 3 * BLK), 1)
    dist = jnp.abs(ci - BLK - qi)
    distf = dist.astype(F32)
    window = dist <= BLK
    for kind, seen in enumerate((window & (ci >= BLK), window, window & (ci < 2 * BLK))):
        for h in range(N_HEADS):
            bias_s[kind, h] = jnp.where(seen, -(ALIBI_SLOPES[h] * distf), NEG_INF)


def _block_kind(block, seq):
    assert seq >= 2 * BLK
    return jnp.where(block == 0, 0, jnp.where(block == seq // BLK - 1, 2, 1))


def _stage_queries(qn, lo_t, j, nb, qs):
    for a in range(2):
        v, slot = HEAD_SLOT[2 * j + a]
        qm = _mx(jnp.where(lo_t, qn, 0.0) if a == 0 else jnp.where(lo_t, 0.0, qn))
        for n in range(nb):
            qs[n, v, slot * BLK : (slot + 1) * BLK, :] = qm[n * BLK : (n + 1) * BLK]


def _unstack_pair(stacked, j, lo):
    (v0, s0), (v1, s1) = HEAD_SLOT[2 * j], HEAD_SLOT[2 * j + 1]
    return jnp.where(lo, stacked[v0][s0 * BLK : (s0 + 1) * BLK], stacked[v1][s1 * BLK : (s1 + 1) * BLK])


def _stage_keys(kvp_ref, qkv_ref, kvn_ref, kg, ones, tile, ks, kr, vs, vr, khat_s=None, rk_s=None):
    pieces = (
        (0, BLK, kvp_ref[:, 0:D_KV], kvp_ref[:, D_KV : 2 * D_KV]),
        (BLK, tile, qkv_ref[:, D_ATTN : D_ATTN + D_KV], qkv_ref[:, D_ATTN + D_KV : D_QKV]),
        (BLK + tile, BLK, kvn_ref[:, 0:D_KV], kvn_ref[:, D_KV : 2 * D_KV]),
    )
    for r0, n, k, v in pieces:
        khat, rk = _half_rms(k, ones)
        kn = khat * kg
        ks[r0 : r0 + n, :] = _mx(kn)
        kr[r0 : r0 + n, :] = _mx(pltpu.roll(kn, HEAD_DIM, 1))
        vs[r0 : r0 + n, :] = _mx(v)
        vr[r0 : r0 + n, :] = _mx(pltpu.roll(v, HEAD_DIM, 1))
        if khat_s is not None:
            khat_s[r0 : r0 + n, :] = khat
            rk_s[r0 : r0 + n, :] = rk


def _halo_specs(tile, seq):
    nb = tile // BLK
    last = seq // BLK - 1
    kv_col = D_ATTN // (2 * D_KV)
    prev = pl.BlockSpec((BLK, 2 * D_KV), lambda i: (jnp.maximum(i * nb - 1, 0), kv_col))
    nxt = pl.BlockSpec((BLK, 2 * D_KV), lambda i: (jnp.minimum((i + 1) * nb, last), kv_col))
    return prev, nxt


def _row_spec(tile, width):
    return pl.BlockSpec((tile, width), lambda i: (i, 0))


def _full_spec(shape):
    nd = len(shape)
    return pl.BlockSpec(shape, lambda i: (0,) * nd)


SMEM_SPEC = pl.BlockSpec(memory_space=pltpu.SMEM)
VMEM_SPEC = pl.BlockSpec(memory_space=pltpu.VMEM)
HBM_SPEC = pl.BlockSpec(memory_space=pltpu.HBM)


def _rider_steps(nt):
    return 0, (2 * nt) // 3, nt - 1


def _gather_rider(sources, gathered, sems, step, nt):
    start, forward, finish = _all_gather_stages(
        [_row_block(g, s.shape[0]) for g, s in zip(gathered, sources)], sems[0], sems[1], sources=sources, local_sems=sems[2]
    )
    at_start, at_forward, at_finish = _rider_steps(nt)
    pl.when(step == at_start)(start)

    def after_compute():
        pl.when(step == at_forward)(forward)
        pl.when(step == at_finish)(finish)

    return after_compute


def _gathered_shapes(gather):
    return [jax.ShapeDtypeStruct((N_DEV * g.shape[0], g.shape[1]), g.dtype) for g in gather]


def _ln_proj_fwd(x, gain, scale1, shift, w_in_t, name, gather=()):
    seq, d = x.shape
    tile = min(TOKEN_TILE, seq)
    nt = seq // tile
    n_ride = len(gather)

    def body(x_ref, g_ref, s1_ref, sh_ref, wt_ref, *rest):
        sources, (pa_ref, pb_ref), rest = rest[:n_ride], rest[n_ride : n_ride + 2], rest[n_ride + 2 :]
        after_compute = _gather_rider(sources, rest[:n_ride], rest[n_ride:], pl.program_id(0), nt) if n_ride else None
        xv = x_ref[...]
        r = lax.rsqrt(jnp.mean(xv * xv, axis=-1, keepdims=True) + EPS)
        h = _mx((xv * r) * g_ref[...] * s1_ref[...] + sh_ref[...])
        pa_ref[...] = _dot_nt(h, wt_ref[0:D_QKV, :])
        pb_ref[...] = _dot_nt(h, wt_ref[D_QKV:D_IN, :])
        if n_ride:
            after_compute()

    vec = _full_spec((1, d))
    return pl.pallas_call(
        body,
        name=name,
        grid=(nt,),
        in_specs=[_row_spec(tile, d), vec, vec, vec, _full_spec((D_IN, d))] + [HBM_SPEC] * n_ride,
        out_specs=[_row_spec(tile, D_QKV), _row_spec(tile, D_REST)] + [HBM_SPEC] * n_ride,
        out_shape=[jax.ShapeDtypeStruct((seq, D_QKV), F32), jax.ShapeDtypeStruct((seq, D_REST), F32)] + _gathered_shapes(gather),
        scratch_shapes=_rider_sems(n_ride),
        compiler_params=_params(("arbitrary",) if n_ride else ("parallel",)),
    )(x, gain, scale1, shift, w_in_t, *gather)


def _attn_fwd(pa, q_gain2, k_gain2, sink, name, gather=None):
    seq = pa.shape[0]
    tile = min(TOKEN_TILE, seq)
    nb = tile // BLK
    nt = seq // tile
    ext = tile + 2 * BLK
    riding = gather is not None

    def body(sink_ref, qkv_ref, kvp_ref, kvn_ref, qg_ref, kg_ref, *rest):
        i = pl.program_id(0)
        if riding:
            sources, o_ref, gathered = rest[0:2], rest[2], rest[3:5]
            qs, ks, kr, vs, vr, bias_s, s_scr, p_scr, inv_scr, *sems = rest[5:]
            after_compute = _gather_rider(sources, gathered, sems, i, nt)
        else:
            o_ref, qs, ks, kr, vs, vr, bias_s, s_scr, p_scr, inv_scr = rest

        @pl.when(i == 0)
        def _():
            _fill_attn_bias(bias_s)

        ones = _half_ones()
        lo = _lane_lo(BLK)
        lo_t = _lane_lo(tile)
        _stage_keys(kvp_ref, qkv_ref, kvn_ref, kg_ref[...], ones, tile, ks, kr, vs, vr)
        for j in range(N_PAIRS):
            qhat, _ = _half_rms(qkv_ref[:, j * LANES : (j + 1) * LANES], ones)
            _stage_queries(qhat * (qg_ref[...] * Q_SCALE), lo_t, j, nb, qs)

        def block(n, carry):
            r0 = pl.multiple_of(n * BLK, BLK)
            krows = pl.ds(r0, 3 * BLK)
            kind = _block_kind(i * nb + n, seq)
            for v in range(2):
                s_scr[v] = _dot_nt(qs[n, v], (kr if v else ks)[krows, :])
            for h in range(N_HEADS):
                v, slot = HEAD_SLOT[h]
                sink_h = sink_ref[h]
                for rc in range(0, BLK, ROW_CHUNK):
                    rows = slice(slot * BLK + rc, slot * BLK + rc + ROW_CHUNK)
                    s = s_scr[v, rows, :] + bias_s[kind, h, rc : rc + ROW_CHUNK, :]
                    m = jnp.maximum(jnp.max(s, axis=-1, keepdims=True), sink_h)
                    p = jnp.exp(s - m)
                    total = jnp.sum(p, axis=-1, keepdims=True) + jnp.exp(sink_h - m)
                    p_scr[v, rows, :] = _mx(p)
                    inv_scr[v, rows, :] = jnp.broadcast_to(1.0 / total, (ROW_CHUNK, LANES))
            outs = [_dot(p_scr[v], (vr if v else vs)[krows, :]) * inv_scr[v] for v in range(2)]
            for j in range(N_PAIRS):
                o_ref[pl.ds(r0, BLK), j * LANES : (j + 1) * LANES] = _unstack_pair(outs, j, lo)
            return carry

        lax.fori_loop(0, nb, block, 0)
        if riding:
            after_compute()

    prev, nxt = _halo_specs(tile, seq)
    vec = _full_spec((1, LANES))
    in_specs = [SMEM_SPEC, _row_spec(tile, D_QKV), prev, nxt, vec, vec]
    out_specs = [_row_spec(tile, D_ATTN)]
    out_shape = [jax.ShapeDtypeStruct((seq, D_ATTN), F32)]
    scratch = [
        pltpu.VMEM((nb, 2, STACK, LANES), MXU_DTYPE),
        pltpu.VMEM((ext, LANES), MXU_DTYPE),
        pltpu.VMEM((ext, LANES), MXU_DTYPE),
        pltpu.VMEM((ext, LANES), MXU_DTYPE),
        pltpu.VMEM((ext, LANES), MXU_DTYPE),
        pltpu.VMEM((3, N_HEADS, BLK, 3 * BLK), F32),
        pltpu.VMEM((2, STACK, 3 * BLK), F32),
        pltpu.VMEM((2, STACK, 3 * BLK), MXU_DTYPE),
        pltpu.VMEM((2, STACK, LANES), F32),
    ]
    extra = ()
    if riding:
        extra = tuple(gather)
        in_specs += [HBM_SPEC] * 2
        out_specs += [HBM_SPEC] * 2
        out_shape += [jax.ShapeDtypeStruct((N_DEV * g.shape[0], g.shape[1]), g.dtype) for g in gather]
        scratch += [pltpu.SemaphoreType.DMA((14,)), pltpu.SemaphoreType.DMA((14,)), pltpu.SemaphoreType.DMA((2,))]
    out = pl.pallas_call(
        body,
        name=name,
        grid=(nt,),
        in_specs=in_specs,
        out_specs=out_specs,
        out_shape=out_shape,
        scratch_shapes=scratch,
        compiler_params=_params(("arbitrary",)),
    )(sink, pa, pa, pa, q_gain2, k_gain2, *extra)
    return out if riding else out[0]


def _mix_out_fwd(pb, o, x, gate, w_out, w_s, b_st, name, target=None):
    seq, d = x.shape
    tile = min(TOKEN_TILE, seq)
    nb = tile // BLK
    with_loss = target is not None

    def body(pb_ref, o_ref, x_ref, gate_ref, wo_ref, ws_ref, bs_ref, *rest):
        if with_loss:
            t_ref, xo_ref, acc_ref, y_s, vn_s = rest

            @pl.when(pl.program_id(0) == 0)
            def _():
                acc_ref[...] = jnp.zeros_like(acc_ref)
        else:
            xo_ref, y_s, vn_s = rest
        ones = _half_ones(WIDE)
        lo = _lane_lo(BLK)
        ga = pb_ref[:, 0:D_ATTN]
        y_s[:, 0:D_ATTN] = _mx(o_ref[...] * (ga * _sigmoid(ga)))
        for j in range(D_GM // WIDE):
            vhat, _ = _half_rms(pb_ref[:, 2 * D_GM + j * WIDE : 2 * D_GM + (j + 1) * WIDE], ones)
            vn_s[:, j * WIDE : (j + 1) * WIDE] = _mx(vhat)

        def chunk(n, carry):
            rows = pl.ds(pl.multiple_of(n * BLK, BLK), BLK)
            for j in range(N_PAIRS):
                cols = slice(j * LANES, (j + 1) * LANES)
                vn = vn_s[rows, cols]
                sv = jnp.where(lo, _dot(ws_ref[2 * j], vn), _dot(ws_ref[2 * j + 1], vn)) + bs_ref[:, cols]
                u = pb_ref[rows, D_ATTN + j * LANES : D_ATTN + (j + 1) * LANES]
                gg = pb_ref[rows, D_ATTN + 2 * D_GM + j * LANES : D_ATTN + 2 * D_GM + (j + 1) * LANES]
                y_s[rows, D_ATTN + j * LANES : D_ATTN + (j + 1) * LANES] = _mx((u * sv) * (gg * _sigmoid(gg)))
            return carry

        lax.fori_loop(0, nb, chunk, 0)
        y = x_ref[...] + gate_ref[...] * _dot(y_s[...], wo_ref[...])
        if with_loss:
            e = y - t_ref[...]
            xo_ref[...] = e * (1.0 / d)
            acc_ref[...] += jnp.sum(jnp.sum(e * e, axis=-1, keepdims=True), axis=0, keepdims=True)
        else:
            xo_ref[...] = y

    row = _row_spec(tile, d)
    acc_shape = (SUBLANES, LANES)
    return pl.pallas_call(
        body,
        name=name,
        grid=(seq // tile,),
        in_specs=[
            _row_spec(tile, D_REST),
            _row_spec(tile, D_ATTN),
            row,
            _full_spec((1, d)),
            _full_spec((D_MIX, d)),
            _full_spec((N_GROUPS, BLK, BLK)),
            _full_spec((BLK, D_GM)),
        ]
        + ([row] if with_loss else []),
        out_specs=[row, _full_spec(acc_shape)] if with_loss else row,
        out_shape=[jax.ShapeDtypeStruct((seq, d), F32), jax.ShapeDtypeStruct(acc_shape, F32)]
        if with_loss
        else jax.ShapeDtypeStruct((seq, d), F32),
        scratch_shapes=[pltpu.VMEM((tile, D_MIX), MXU_DTYPE), pltpu.VMEM((tile, D_GM), MXU_DTYPE)],
        compiler_params=_params(("arbitrary",) if with_loss else ("parallel",)),
    )(pb, o, x, gate, w_out, w_s, b_st, *([target] if with_loss else []))


def _mix_out_bwd(dxn, pb, o, gate, w_out, w_s, w_s_t, b_st, name):
    seq, d = dxn.shape
    tile = min(TOKEN_TILE, seq)
    nb = tile // BLK
    nt = seq // tile

    def body(dxn_ref, pb_ref, o_ref, gate_ref, wo_ref, ws_ref, wst_ref, bs_ref,
             dpb_ref, do_ref, dwo_ref, dgate_ref, dws_ref, dbs_ref, g_ref, y_s, dy_s, vn_s, rv_s, vnb_s, sv_s, dsv_s, dvn_s):
        @pl.when(pl.program_id(0) == 0)
        def _():
            g_ref[...] = jnp.zeros_like(g_ref)
            dws_ref[...] = jnp.zeros_like(dws_ref)
            dbs_ref[...] = jnp.zeros_like(dbs_ref)

        ones = _half_ones(WIDE)
        lo = _lane_lo(BLK)
        c_u = slice(D_ATTN, D_ATTN + D_GM)
        c_vg = slice(D_ATTN + D_GM, D_ATTN + 2 * D_GM)
        c_gg = slice(D_ATTN + 2 * D_GM, D_REST)
        dxv = dxn_ref[...]
        dy_s[...] = _dot_nt(_mx(dxv * gate_ref[...]), wo_ref[...])
        ga = pb_ref[:, 0:D_ATTN]
        sig = _sigmoid(ga)
        sil = ga * sig
        ov = o_ref[...]
        y_s[:, 0:D_ATTN] = _mx(ov * sil)
        da = dy_s[:, 0:D_ATTN]
        do_ref[...] = da * sil
        dpb_ref[:, 0:D_ATTN] = (da * ov * (sig * (1.0 + ga * (1.0 - sig)))).astype(dpb_ref.dtype)
        for j in range(D_GM // WIDE):
            cols = slice(j * WIDE, (j + 1) * WIDE)
            vhat, rv = _half_rms(pb_ref[:, 2 * D_GM + j * WIDE : 2 * D_GM + (j + 1) * WIDE], ones)
            vn_s[:, cols] = vhat
            rv_s[:, cols] = rv
            vnb_s[:, cols] = _mx(vhat)

        def spatial_fwd(n, carry):
            rows = pl.ds(pl.multiple_of(n * BLK, BLK), BLK)
            for j in range(N_PAIRS):
                cols = slice(j * LANES, (j + 1) * LANES)
                vn = vnb_s[rows, cols]
                sv_s[rows, cols] = jnp.where(lo, _dot(ws_ref[2 * j], vn), _dot(ws_ref[2 * j + 1], vn)) + bs_ref[:, cols]
            return carry

        lax.fori_loop(0, nb, spatial_fwd, 0)

        def gating(n, carry):
            rows = pl.ds(pl.multiple_of(n * BLK, BLK), BLK)
            sv = sv_s[rows, :]
            u = pb_ref[rows, c_u]
            gg = pb_ref[rows, c_gg]
            sg = _sigmoid(gg)
            silg = gg * sg
            m0 = u * sv
            y_s[rows, D_ATTN:D_MIX] = _mx(m0 * silg)
            dm = dy_s[rows, D_ATTN:D_MIX]
            dm0 = dm * silg
            dpb_ref[rows, c_gg] = (dm * m0 * (sg * (1.0 + gg * (1.0 - sg)))).astype(dpb_ref.dtype)
            dpb_ref[rows, c_u] = (dm0 * sv).astype(dpb_ref.dtype)
            dsv = dm0 * u
            dsv_s[rows, :] = _mx(dsv)
            dbs_ref[...] += dsv
            return carry

        lax.fori_loop(0, nb, gating, 0)

        def spatial_bwd(n, carry):
            rows = pl.ds(pl.multiple_of(n * BLK, BLK), BLK)
            for j in range(N_PAIRS):
                cols = slice(j * LANES, (j + 1) * LANES)
                dsv = dsv_s[rows, cols]
                dvn_s[rows, cols] = jnp.where(lo, _dot(wst_ref[2 * j], dsv), _dot(wst_ref[2 * j + 1], dsv))
            return carry

        lax.fori_loop(0, nb, spatial_bwd, 0)
        zero = jnp.zeros((BLK, LANES), MXU_DTYPE)
        for j in range(N_PAIRS):
            cols = slice(j * LANES, (j + 1) * LANES)
            chunks = [dsv_s[n * BLK : (n + 1) * BLK, cols] for n in range(nb)]
            vn_all = jnp.concatenate([vnb_s[n * BLK : (n + 1) * BLK, cols] for n in range(nb)], axis=1)
            dws_ref[2 * j] += _dot_nt(jnp.concatenate([jnp.where(lo, c, zero) for c in chunks], axis=1), vn_all)
            dws_ref[2 * j + 1] += _dot_nt(jnp.concatenate([jnp.where(lo, zero, c) for c in chunks], axis=1), vn_all)
        for j in range(D_GM // WIDE):
            cols = slice(j * WIDE, (j + 1) * WIDE)
            dpb_ref[:, D_ATTN + D_GM + j * WIDE : D_ATTN + D_GM + (j + 1) * WIDE] = _half_rms_bwd(
                dvn_s[:, cols], vn_s[:, cols], rv_s[:, cols], ones
            ).astype(dpb_ref.dtype)
        g_ref[...] += _dot_tn(y_s[...], _mx(dxv))

        @pl.when(pl.program_id(0) == nt - 1)
        def _():
            gv = g_ref[...]
            dwo_ref[...] = (gv * gate_ref[...]).astype(dwo_ref.dtype)
            dgate_ref[...] = _group_rows(gv * wo_ref[...].astype(F32))

    return pl.pallas_call(
        body,
        name=name,
        grid=(seq // tile,),
        in_specs=[
            _row_spec(tile, d),
            _row_spec(tile, D_REST),
            _row_spec(tile, D_ATTN),
            _full_spec((1, d)),
            _full_spec((D_MIX, d)),
            _full_spec((N_GROUPS, BLK, BLK)),
            _full_spec((N_GROUPS, BLK, BLK)),
            _full_spec((BLK, D_GM)),
        ],
        out_specs=[
            _row_spec(tile, D_REST),
            _row_spec(tile, D_ATTN),
            _full_spec((D_MIX, d)),
            _full_spec((SUBLANES, d)),
            _full_spec((N_GROUPS, BLK, BLK)),
            _full_spec((BLK, D_GM)),
        ],
        out_shape=[
            jax.ShapeDtypeStruct((seq, D_REST), MXU_DTYPE),
            jax.ShapeDtypeStruct((seq, D_ATTN), F32),
            jax.ShapeDtypeStruct((D_MIX, d), jnp.bfloat16),
            jax.ShapeDtypeStruct((SUBLANES, d), F32),
            jax.ShapeDtypeStruct((N_GROUPS, BLK, BLK), F32),
            jax.ShapeDtypeStruct((BLK, D_GM), F32),
        ],
        scratch_shapes=[
            pltpu.VMEM((D_MIX, d), F32),
            pltpu.VMEM((tile, D_MIX), MXU_DTYPE),
            pltpu.VMEM((tile, D_MIX), F32),
            pltpu.VMEM((tile, D_GM), F32),
            pltpu.VMEM((tile, D_GM), F32),
            pltpu.VMEM((tile, D_GM), MXU_DTYPE),
            pltpu.VMEM((tile, D_GM), F32),
            pltpu.VMEM((tile, D_GM), MXU_DTYPE),
            pltpu.VMEM((tile, D_GM), F32),
        ],
        compiler_params=_params(("arbitrary",)),
    )(dxn, pb, o, gate, w_out, w_s, w_s_t, b_st)


def _attn_bwd(pa, o, do, q_gain2, k_gain2, sink, name, scatter=()):
    seq = pa.shape[0]
    tile = min(TOKEN_TILE, seq)
    nb = tile // BLK
    nt = seq // tile
    ext = tile + 2 * BLK
    n_ride = len(scatter)
    riding = n_ride > 0

    def body(sink_ref, qkv_ref, kvp_ref, kvn_ref, o_ref, do_ref, qg_ref, kg_ref, *rest):
        i = pl.program_id(0)
        blocks, rest = rest[:n_ride], rest[n_ride:]
        dq_ref, dkv_ref, hp_ref, hn_ref, dqg_ref, dkg_ref, dsk_ref = rest[:7]
        landing, rest = rest[7 : 7 + n_ride], rest[7 + n_ride :]
        (qs, dos, qhat_s, rq_s, ks, kr, vs, vr, khat_s, rk_s, dqn_s, dka, dva, bias_s, s_scr, dp_scr, p_scr, ds_scr) = rest[:18]
        if riding:
            start, finish = _scatter_stages(blocks, landing, *rest[18:])
            at_start, _, at_finish = _rider_steps(nt)
            pl.when(i == at_start)(start)

        @pl.when(i == 0)
        def _():
            dqg_ref[...] = jnp.zeros_like(dqg_ref)
            dkg_ref[...] = jnp.zeros_like(dkg_ref)
            dsk_ref[...] = jnp.zeros_like(dsk_ref)
            _fill_attn_bias(bias_s)

        ones = _half_ones()
        lo = _lane_lo(BLK)
        lo_t = _lane_lo(tile)
        lo_c = _lane_lo(ROW_CHUNK)
        qg = qg_ref[...] * Q_SCALE
        kg = kg_ref[...]
        _stage_keys(kvp_ref, qkv_ref, kvn_ref, kg, ones, tile, ks, kr, vs, vr, khat_s, rk_s)
        for j in range(N_PAIRS):
            cols = slice(j * LANES, (j + 1) * LANES)
            qhat, rq = _half_rms(qkv_ref[:, cols], ones)
            qhat_s[:, cols] = qhat
            rq_s[:, cols] = rq
            _stage_queries(qhat * qg, lo_t, j, nb, qs)
            _stage_queries(do_ref[:, cols], lo_t, j, nb, dos)
        dka[...] = jnp.zeros_like(dka)
        dva[...] = jnp.zeros_like(dva)
        head_lane = lax.broadcasted_iota(jnp.int32, (1, LANES), 1)

        def block(n, dsink):
            r0 = pl.multiple_of(n * BLK, BLK)
            krows = pl.ds(r0, 3 * BLK)
            kind = _block_kind(i * nb + n, seq)
            for v in range(2):
                s_scr[v] = _dot_nt(qs[n, v], (kr if v else ks)[krows, :])
                dp_scr[v] = _dot_nt(dos[n, v], (vr if v else vs)[krows, :])
            for h in range(N_HEADS):
                v, slot = HEAD_SLOT[h]
                j, a = divmod(h, 2)
                cols = slice(j * LANES, (j + 1) * LANES)
                sink_h = sink_ref[h]
                sink_part = jnp.zeros((ROW_CHUNK, 1), F32)
                for rc in range(0, BLK, ROW_CHUNK):
                    rows = slice(slot * BLK + rc, slot * BLK + rc + ROW_CHUNK)
                    trows = pl.ds(pl.multiple_of(r0 + rc, ROW_CHUNK), ROW_CHUNK)
                    s = s_scr[v, rows, :] + bias_s[kind, h, rc : rc + ROW_CHUNK, :]
                    m = jnp.maximum(jnp.max(s, axis=-1, keepdims=True), sink_h)
                    p = jnp.exp(s - m)
                    e_sink = jnp.exp(sink_h - m)
                    inv = 1.0 / (jnp.sum(p, axis=-1, keepdims=True) + e_sink)
                    pn = p * inv
                    prod = do_ref[trows, cols] * o_ref[trows, cols]
                    prod = jnp.where(lo_c, prod, 0.0) if a == 0 else jnp.where(lo_c, 0.0, prod)
                    dcol = jnp.sum(prod, axis=-1, keepdims=True)
                    ds_scr[v, rows, :] = _mx(pn * (dp_scr[v, rows, :] - dcol))
                    p_scr[v, rows, :] = _mx(pn)
                    sink_part = sink_part + (e_sink * inv) * dcol
                dsink = dsink - jnp.where(head_lane == h, jnp.sum(sink_part, axis=0, keepdims=True), 0.0)
            dqv = []
            for v in range(2):
                dqv.append(_dot(ds_scr[v], (kr if v else ks)[krows, :]))
                dka[v, krows, :] += _dot_tn(ds_scr[v], qs[n, v])
                dva[v, krows, :] += _dot_tn(p_scr[v], dos[n, v])
            for j in range(N_PAIRS):
                dqn_s[pl.ds(r0, BLK), j * LANES : (j + 1) * LANES] = _unstack_pair(dqv, j, lo)
            return dsink

        dsink = lax.fori_loop(0, nb, block, jnp.zeros((1, LANES), F32))
        dsk_ref[...] += jnp.broadcast_to(dsink, (SUBLANES, LANES))
        for j in range(N_PAIRS):
            cols = slice(j * LANES, (j + 1) * LANES)
            dqn = dqn_s[:, cols]
            qhat = qhat_s[:, cols]
            dqg_ref[:, cols] += _group_rows(dqn * qhat) * Q_SCALE
            dq_ref[:, cols] = _half_rms_bwd(dqn * qg, qhat, rq_s[:, cols], ones).astype(dq_ref.dtype)
        dkn = dka[0] + pltpu.roll(dka[1], HEAD_DIM, 1)
        khat = khat_s[...]
        dkg_ref[...] += _group_rows(dkn * khat)
        dk = _half_rms_bwd(dkn * kg, khat, rk_s[...], ones)
        dv = dva[0] + pltpu.roll(dva[1], HEAD_DIM, 1)
        hp_ref[:, 0:D_KV] = dk[0:BLK]
        hp_ref[:, D_KV : 2 * D_KV] = dv[0:BLK]
        dkv_ref[:, 0:D_KV] = dk[BLK : BLK + tile]
        dkv_ref[:, D_KV : 2 * D_KV] = dv[BLK : BLK + tile]
        hn_ref[:, 0:D_KV] = dk[BLK + tile : ext]
        hn_ref[:, D_KV : 2 * D_KV] = dv[BLK + tile : ext]
        if riding:
            pl.when(i == at_finish)(finish)

    prev, nxt = _halo_specs(tile, seq)
    vec = _full_spec((1, LANES))
    halo = pl.BlockSpec((None, BLK, 2 * D_KV), lambda i: (i, 0, 0))
    return pl.pallas_call(
        body,
        name=name,
        grid=(nt,),
        in_specs=[SMEM_SPEC, _row_spec(tile, D_QKV), prev, nxt, _row_spec(tile, D_ATTN), _row_spec(tile, D_ATTN), vec, vec]
        + [HBM_SPEC] * n_ride,
        out_specs=[
            _row_spec(tile, D_ATTN),
            _row_spec(tile, 2 * D_KV),
            halo,
            halo,
            _full_spec((SUBLANES, D_ATTN)),
            _full_spec((SUBLANES, LANES)),
            _full_spec((SUBLANES, LANES)),
        ]
        + [HBM_SPEC] * n_ride,
        out_shape=[
            jax.ShapeDtypeStruct((seq, D_ATTN), MXU_DTYPE),
            jax.ShapeDtypeStruct((seq, 2 * D_KV), F32),
            jax.ShapeDtypeStruct((nt, BLK, 2 * D_KV), F32),
            jax.ShapeDtypeStruct((nt, BLK, 2 * D_KV), F32),
            jax.ShapeDtypeStruct((SUBLANES, D_ATTN), F32),
            jax.ShapeDtypeStruct((SUBLANES, LANES), F32),
            jax.ShapeDtypeStruct((SUBLANES, LANES), F32),
        ]
        + _landing_shapes(scatter),
        scratch_shapes=[
            pltpu.VMEM((nb, 2, STACK, LANES), MXU_DTYPE),
            pltpu.VMEM((nb, 2, STACK, LANES), MXU_DTYPE),
            pltpu.VMEM((tile, D_ATTN), F32),
            pltpu.VMEM((tile, D_ATTN), F32),
            pltpu.VMEM((ext, LANES), MXU_DTYPE),
            pltpu.VMEM((ext, LANES), MXU_DTYPE),
            pltpu.VMEM((ext, LANES), MXU_DTYPE),
            pltpu.VMEM((ext, LANES), MXU_DTYPE),
            pltpu.VMEM((ext, LANES), F32),
            pltpu.VMEM((ext, LANES), F32),
            pltpu.VMEM((tile, D_ATTN), F32),
            pltpu.VMEM((2, ext, LANES), F32),
            pltpu.VMEM((2, ext, LANES), F32),
            pltpu.VMEM((3, N_HEADS, BLK, 3 * BLK), F32),
            pltpu.VMEM((2, STACK, 3 * BLK), F32),
            pltpu.VMEM((2, STACK, 3 * BLK), F32),
            pltpu.VMEM((2, STACK, 3 * BLK), MXU_DTYPE),
            pltpu.VMEM((2, STACK, 3 * BLK), MXU_DTYPE),
        ]
        + _rider_sems(n_ride),
        compiler_params=_params(("arbitrary",)),
    )(sink, pa, pa, pa, o, do, q_gain2, k_gain2, *scatter)


def _halo_in_specs(tile, nt):
    from_prev = pl.BlockSpec((None, BLK, 2 * D_KV), lambda i: (jnp.maximum(i - 1, 0), 0, 0))
    from_next = pl.BlockSpec((None, BLK, 2 * D_KV), lambda i: (jnp.minimum(i + 1, nt - 1), 0, 0))
    return from_prev, from_next


def _landing_shapes(scatter):
    return [jax.ShapeDtypeStruct((N_DEV,) + b.shape[2:], b.dtype) for b in scatter]


def _rider_sems(n_ride):
    if not n_ride:
        return []
    return [pltpu.SemaphoreType.DMA((7 * n_ride,)), pltpu.SemaphoreType.DMA((7 * n_ride,)), pltpu.SemaphoreType.DMA((n_ride,))]


def _proj_bwd_dx(x, dxn, dq, dkvb, dpb, w_in_t, gain, scale1, name, scatter=()):
    seq, d = x.shape
    tile = min(TOKEN_TILE, seq)
    nt = seq // tile
    n_ride = len(scatter)

    def row(width):
        return _row_spec(tile, width)

    def body(x_ref, dxn_ref, dq_ref, dkvb_ref, dpb_ref, wt_ref, g_ref, s1_ref, *rest):
        i = pl.program_id(0)
        blocks, rest = rest[:n_ride], rest[n_ride:]
        dx_ref, c0_ref, c1_ref = rest[:3]
        landing, sems = rest[3 : 3 + n_ride], rest[3 + n_ride :]
        if n_ride:
            start, finish = _scatter_stages(blocks, landing, *sems)
            at_start, _, at_finish = _rider_steps(nt)
            pl.when(i == at_start)(start)

        @pl.when(i == 0)
        def _():
            c0_ref[...] = jnp.zeros_like(c0_ref)
            c1_ref[...] = jnp.zeros_like(c1_ref)

        dh = (
            _dot(dq_ref[...], wt_ref[0:D_ATTN, :])
            + _dot(dkvb_ref[...], wt_ref[D_ATTN:D_QKV, :])
            + _dot(dpb_ref[...], wt_ref[D_QKV:D_IN, :])
        )
        xv = x_ref[...]
        r = lax.rsqrt(jnp.mean(xv * xv, axis=-1, keepdims=True) + EPS)
        xn = xv * r
        c0_ref[...] += _group_rows(dh)
        c1_ref[...] += _group_rows(dh * xn)
        dxn_ = dh * (g_ref[...] * s1_ref[...])
        dx_ref[...] = dxn_ref[...] + r * (dxn_ - xn * jnp.mean(xn * dxn_, axis=-1, keepdims=True))
        if n_ride:
            pl.when(i == at_finish)(finish)

    vec = _full_spec((1, d))
    return pl.pallas_call(
        body,
        name=name,
        grid=(nt,),
        in_specs=[row(d), row(d), row(D_ATTN), row(2 * D_KV), row(D_REST), _full_spec((D_IN, d)), vec, vec]
        + [HBM_SPEC] * n_ride,
        out_specs=[row(d), _full_spec((SUBLANES, d)), _full_spec((SUBLANES, d))] + [HBM_SPEC] * n_ride,
        out_shape=[
            jax.ShapeDtypeStruct((seq, d), F32),
            jax.ShapeDtypeStruct((SUBLANES, d), F32),
            jax.ShapeDtypeStruct((SUBLANES, d), F32),
        ]
        + _landing_shapes(scatter),
        scratch_shapes=_rider_sems(n_ride),
        compiler_params=_params(("arbitrary",)),
    )(x, dxn, dq, dkvb, dpb, w_in_t, gain, scale1, *scatter)


def _proj_bwd_dw(x, gain, scale1, shift, dq, dkv, halo_prev, halo_next, dpb, name, gather=()):
    seq, d = x.shape
    tile = min(TOKEN_TILE, seq)
    nt = seq // tile
    assert tile >= 2 * BLK
    n_ride = len(gather)

    def body(x_ref, g_ref, s1_ref, sh_ref, dq_ref, dkv_ref, hn_ref, hp_ref, dpb_ref, *rest):
        i = pl.program_id(0)
        sources, rest = rest[:n_ride], rest[n_ride:]
        dw_ref, dkvb_ref = rest[:2]
        gathered, (acc, *sems) = rest[2 : 2 + n_ride], rest[2 + n_ride :]
        after_compute = _gather_rider(sources, gathered, sems, i, nt) if n_ride else None

        @pl.when(i == 0)
        def _():
            acc[...] = jnp.zeros_like(acc)

        top = dkv_ref[0:BLK, :] + jnp.where(i > 0, hn_ref[...], 0.0)
        bot = dkv_ref[tile - BLK : tile, :] + jnp.where(i < nt - 1, hp_ref[...], 0.0)
        dkvb_ref[0:BLK, :] = top.astype(dkvb_ref.dtype)
        dkvb_ref[tile - BLK : tile, :] = bot.astype(dkvb_ref.dtype)
        if tile > 2 * BLK:
            dkvb_ref[BLK : tile - BLK, :] = dkv_ref[BLK : tile - BLK, :].astype(dkvb_ref.dtype)
        xv = x_ref[...]
        r = lax.rsqrt(jnp.mean(xv * xv, axis=-1, keepdims=True) + EPS)
        h = _mx((xv * r) * g_ref[...] * s1_ref[...] + sh_ref[...])
        acc[0:D_ATTN, :] += _dot_tn(dq_ref[...], h)
        acc[D_ATTN:D_QKV, :] += _dot_tn(dkvb_ref[...], h)
        acc[D_QKV:D_IN, :] += _dot_tn(dpb_ref[...], h)

        @pl.when(i == nt - 1)
        def _():
            dw_ref[...] = acc[...].astype(dw_ref.dtype)

        if n_ride:
            after_compute()

    from_prev, from_next = _halo_in_specs(tile, nt)
    vec = _full_spec((1, d))
    return pl.pallas_call(
        body,
        name=name,
        grid=(nt,),
        in_specs=[
            _row_spec(tile, d),
            vec,
            vec,
            vec,
            _row_spec(tile, D_ATTN),
            _row_spec(tile, 2 * D_KV),
            from_prev,
            from_next,
            _row_spec(tile, D_REST),
        ]
        + [HBM_SPEC] * n_ride,
        out_specs=[_full_spec((D_IN, d)), _row_spec(tile, 2 * D_KV)] + [HBM_SPEC] * n_ride,
        out_shape=[jax.ShapeDtypeStruct((D_IN, d), jnp.bfloat16), jax.ShapeDtypeStruct((seq, 2 * D_KV), MXU_DTYPE)]
        + _gathered_shapes(gather),
        scratch_shapes=[pltpu.VMEM((D_IN, d), F32)] + _rider_sems(n_ride),
        compiler_params=_params(("arbitrary",)),
    )(x, gain, scale1, shift, dq, dkv, halo_next, halo_prev, dpb, *gather)


def _adamw_math(w, g, m, v):
    m = ADAM_B1 * m + (1.0 - ADAM_B1) * g
    v = ADAM_B2 * v + (1.0 - ADAM_B2) * (g * g)
    m_hat = m / (1.0 - ADAM_B1**ADAM_STEP)
    v_hat = v / (1.0 - ADAM_B2**ADAM_STEP)
    delta = -ADAM_LR * (m_hat / (jnp.sqrt(v_hat) + ADAM_EPS) + ADAM_WD * w)
    return delta, m, v


def _small_update(gathered, gathered_ws, w, m, v, ws, m_ws, v_ws):
    def body(ga_ref, gws_ref, w_ref, m_ref, v_ref, ws_ref, mws_ref, vws_ref, *outs):
        for src, refs, out in ((ga_ref, (w_ref, m_ref, v_ref), outs[0:4]), (gws_ref, (ws_ref, mws_ref, vws_ref), outs[4:8])):
            g = src[0].astype(F32)
            for j in range(1, N_DEV):
                g = g + src[j].astype(F32)
            out[0][...] = g
            out[1][...], out[2][...], out[3][...] = _adamw_math(refs[0][...], g, refs[1][...], refs[2][...])

    shapes = [jax.ShapeDtypeStruct(w.shape, F32)] * 4 + [jax.ShapeDtypeStruct(ws.shape, F32)] * 4
    return pl.pallas_call(
        body,
        name="small_update",
        in_specs=[VMEM_SPEC] * 8,
        out_specs=[VMEM_SPEC] * 8,
        out_shape=shapes,
        compiler_params=_params(),
    )(gathered, gathered_ws, w, m, v, ws, m_ws, v_ws)


def _ada_update(c_all, d_ada_cols, w, m, v):
    n_layers = w.shape[0]

    def body(c_ref, da_ref, w_ref, m_ref, v_ref, g_ref, d_ref, mo_ref, vo_ref):
        cv = c_ref[...]
        cond = cv * _sigmoid(cv)
        for l in range(n_layers):
            g = lax.dot_general(
                cond, da_ref[l], (((0,), (0,)), ((), ())), preferred_element_type=F32, precision=lax.Precision.HIGHEST
            )
            g_ref[l] = g
            d_ref[l], mo_ref[l], vo_ref[l] = _adamw_math(w_ref[l], g, m_ref[l], v_ref[l])

    return pl.pallas_call(
        body,
        name="ada_update",
        in_specs=[VMEM_SPEC] * 5,
        out_specs=[VMEM_SPEC] * 4,
        out_shape=[jax.ShapeDtypeStruct(w.shape, F32)] * 4,
        compiler_params=_params(),
    )(c_all, d_ada_cols, w, m, v)


def _position():
    return lax.axis_index("x"), lax.axis_index("y"), lax.axis_index("c")


def _flip(pos, k):
    x, y, c = pos
    return (1 - x if k & 4 else x, 1 - y if k & 2 else y, 1 - c if k & 1 else c)


def _index(pos):
    x, y, c = pos
    return 4 * x + 2 * y + c


def _remote(src, dst, send_sem, recv_sem, to):
    return pltpu.make_async_remote_copy(
        src_ref=src, dst_ref=dst, send_sem=send_sem, recv_sem=recv_sem, device_id=to, device_id_type=MESH_ID
    )


def _all_gather_stages(slots, send_sems, recv_sems, sources=None, local_sems=None):
    me = _position()
    sibling = _flip(me, 1)
    others = (4, 2, 6)
    arrays = range(len(slots))

    def copy(t, k, block, to, own=False):
        slot = slots[t](_index(block))
        src = sources[t] if own and sources is not None else slot
        return _remote(src, slot, send_sems.at[7 * t + k], recv_sems.at[7 * t + k], to)

    def first(t):
        return [copy(t, 0, me, sibling, own=True)] + [copy(t, 1 + j, me, _flip(me, f), own=True) for j, f in enumerate(others)]

    def passed(t, j):
        return copy(t, 4 + j, _flip(me, others[j]), sibling)

    def local(t):
        return pltpu.make_async_copy(sources[t], slots[t](_index(me)), local_sems.at[t])

    def start():
        for t in arrays:
            if sources is not None:
                local(t).start()
            for cp in first(t):
                cp.start()

    def forward():
        for j, f in enumerate(others):
            for t in arrays:
                copy(t, 1 + j, _flip(me, f), me).wait_recv()
                passed(t, j).start()

    def finish():
        for t in arrays:
            copy(t, 0, sibling, me).wait_recv()
            for j, f in enumerate(others):
                copy(t, 4 + j, _flip(sibling, f), me).wait_recv()
        for t in arrays:
            for cp in first(t) + [passed(t, j) for j in range(len(others))]:
                cp.wait_send()
            if sources is not None:
                local(t).wait()

    return start, forward, finish


def _two_level_all_gather(slots, send_sems, recv_sems, between=None):
    start, forward, finish = _all_gather_stages(slots, send_sems, recv_sems)
    start()
    if between is not None:
        between()
    forward()
    finish()


def _row_block(ref, rows):
    return lambda j: ref.at[pl.ds(pl.multiple_of(j * rows, 16), rows), :]


def _scatter_stages(blocks, landing, send_sems, recv_sems, local_sems):
    me = _position()
    my = _index(me)
    arrays = range(len(blocks))

    def copy(t, k):
        px, py, pc = to = _flip(me, k)
        return _remote(blocks[t].at[2 * px + py, pc], landing[t].at[my], send_sems.at[7 * t + k - 1], recv_sems.at[7 * t + k - 1], to)

    def arrival(t, k):
        slot = landing[t].at[_index(_flip(me, k))]
        return _remote(slot, slot, send_sems.at[7 * t + k - 1], recv_sems.at[7 * t + k - 1], _flip(me, k))

    def local(t):
        x, y, c = me
        return pltpu.make_async_copy(blocks[t].at[2 * x + y, c], landing[t].at[my], local_sems.at[t])

    def start():
        for t in arrays:
            local(t).start()
            for k in range(1, N_DEV):
                copy(t, k).start()

    def finish():
        for t in arrays:
            for k in range(1, N_DEV):
                arrival(t, k).wait_recv()
        for t in arrays:
            for k in range(1, N_DEV):
                copy(t, k).wait_send()
            local(t).wait()

    return start, finish


def _ada_exchange(c_ref, w_ref, call_ref, parts_ref, sbuf, sem_s1, sem_r1, sem_s2, sem_r2):
    d = c_ref.shape[-1]
    n_layers = w_ref.shape[0]
    me = _position()
    my = _index(me)
    call_ref[my] = jnp.broadcast_to(c_ref[...], (SUBLANES, d))
    mine = call_ref.at[my]
    first = [_remote(mine, mine, sem_s1.at[k - 1], sem_r1.at[k - 1], _flip(me, k)) for k in range(1, N_DEV)]
    for cp in first:
        cp.start()
    for k in range(1, N_DEV):
        theirs = call_ref.at[_index(_flip(me, k))]
        _remote(theirs, theirs, sem_s1.at[k - 1], sem_r1.at[k - 1], _flip(me, k)).wait_recv()
    cv = call_ref[...].reshape(N_DEV * SUBLANES, d)
    cond = cv * _sigmoid(cv)
    for l in range(n_layers):
        rows = jnp.dot(cond, w_ref[l], preferred_element_type=F32, precision=lax.Precision.HIGHEST)
        for b in range(N_DEV):
            sbuf[b, l] = rows[b * SUBLANES : (b + 1) * SUBLANES]
    parts_ref[my] = sbuf[my]
    second = []
    for k in range(1, N_DEV):
        to = _flip(me, k)
        second.append(_remote(sbuf.at[_index(to)], parts_ref.at[my], sem_s2.at[k - 1], sem_r2.at[k - 1], to))
    for cp in second:
        cp.start()
    for k in range(1, N_DEV):
        theirs = parts_ref.at[_index(_flip(me, k))]
        _remote(theirs, theirs, sem_s2.at[k - 1], sem_r2.at[k - 1], _flip(me, k)).wait_recv()
    for cp in first + second:
        cp.wait_send()


def _gather_weights(w_in_t, w_out, c_row, w_ada):
    n_layers, rows_in, d = w_in_t.shape
    width = w_ada.shape[2]

    def body(wi_ref, wo_ref, c_ref, wa_ref, gi_ref, si_ref, so_ref, call_ref, parts_ref, sbuf, send_sems, recv_sems, *ada_sems):
        my = _index(_position())
        si_ref[...] = wi_ref[...].astype(si_ref.dtype)
        so_ref[...] = wo_ref[...].astype(so_ref.dtype)
        gi_ref[pl.ds(pl.multiple_of(my * rows_in, 16), rows_in), :] = si_ref[0]
        _two_level_all_gather(
            (_row_block(gi_ref, rows_in),),
            send_sems,
            recv_sems,
            between=functools.partial(_ada_exchange, c_ref, wa_ref, call_ref, parts_ref, sbuf, *ada_sems),
        )

    return pl.pallas_call(
        body,
        name="gather_weights",
        in_specs=[VMEM_SPEC] * 4,
        out_specs=[VMEM_SPEC] * 5,
        out_shape=[
            jax.ShapeDtypeStruct((N_DEV * rows_in, d), MXU_DTYPE),
            jax.ShapeDtypeStruct(w_in_t.shape, MXU_DTYPE),
            jax.ShapeDtypeStruct(w_out.shape, MXU_DTYPE),
            jax.ShapeDtypeStruct((N_DEV, SUBLANES, d), F32),
            jax.ShapeDtypeStruct((N_DEV, n_layers, SUBLANES, width), F32),
        ],
        scratch_shapes=[
            pltpu.VMEM((N_DEV, n_layers, SUBLANES, width), F32),
            pltpu.SemaphoreType.DMA((7,)),
            pltpu.SemaphoreType.DMA((7,)),
        ]
        + [pltpu.SemaphoreType.DMA((N_DEV - 1,))] * 4,
        compiler_params=_params(),
    )(w_in_t, w_out, c_row, w_ada)


def _gather_small(packed, adam=()):
    n_adam = len(adam)

    def body(p_ref, *rest):
        quads = [rest[4 * t : 4 * t + 4] for t in range(n_adam)]
        rest = rest[4 * n_adam :]
        g_ref = rest[0]
        results = [rest[1 + 3 * t : 4 + 3 * t] for t in range(n_adam)]
        send_sems, recv_sems = rest[1 + 3 * n_adam :]
        g_ref[_index(_position())] = p_ref[...]

        def updates():
            for (w_ref, gr_ref, m_ref, v_ref), (d_ref, mo_ref, vo_ref) in zip(quads, results):
                d_ref[...], mo_ref[...], vo_ref[...] = _adamw_math(w_ref[...], gr_ref[...], m_ref[...], v_ref[...])

        _two_level_all_gather((lambda j: g_ref.at[j],), send_sems, recv_sems, between=updates)

    return pl.pallas_call(
        body,
        name="gather_small",
        in_specs=[VMEM_SPEC] * (1 + 4 * n_adam),
        out_specs=[VMEM_SPEC] * (1 + 3 * n_adam),
        out_shape=[jax.ShapeDtypeStruct((N_DEV,) + packed.shape, F32)]
        + [jax.ShapeDtypeStruct(q[0].shape, F32) for q in adam for _ in range(3)],
        scratch_shapes=[pltpu.SemaphoreType.DMA((7,)), pltpu.SemaphoreType.DMA((7,))],
        compiler_params=_params(),
    )(packed, *[a for q in adam for a in q])


def _scatter_finish(landed, name, own=()):
    n = len(landed)

    def body(*refs):
        if own:
            x, y, c = _position()
            my = _index((x, y, c))
        for t, (src, out) in enumerate(zip(refs[:n], refs[n + len(own) :])):
            g = None
            for j in range(N_DEV):
                part = src[j].astype(F32)
                if own:
                    part = jnp.where(j == my, refs[n + t][2 * x + y, c].astype(F32), part)
                g = part if g is None else g + part
            out[...] = g

    return pl.pallas_call(
        body,
        name=name,
        in_specs=[VMEM_SPEC] * (n + len(own)),
        out_specs=[VMEM_SPEC] * n,
        out_shape=[jax.ShapeDtypeStruct(a.shape[1:], F32) for a in landed],
        compiler_params=_params(),
    )(*landed, *own)


SEM_SPEC = pl.BlockSpec(memory_space=pltpu.SEMAPHORE)
SPLIT_COPY = pltpu.SideEffectType.DATAFLOW_SIDE_EFFECTING


def _scatter_start(blocks, name):
    land_shape = (N_DEV,) + blocks.shape[2:]

    def body(blocks_ref, land_ref, send_sems, recv_sems, blocks_thru, land_thru, token):
        me = _position()
        my = _index(me)
        for k in range(1, N_DEV):
            px, py, pc = to = _flip(me, k)
            _remote(blocks_ref.at[2 * px + py, pc], land_ref.at[my], send_sems.at[k - 1], recv_sems.at[k - 1], to).start()
        token[...] = jnp.zeros_like(token)

    return pl.pallas_call(
        body,
        name=name,
        in_specs=(HBM_SPEC, HBM_SPEC),
        out_specs=(SEM_SPEC, SEM_SPEC, HBM_SPEC, HBM_SPEC, VMEM_SPEC),
        out_shape=(
            pltpu.SemaphoreType.DMA((N_DEV - 1,)),
            pltpu.SemaphoreType.DMA((N_DEV - 1,)),
            pltpu.HBM(blocks.shape, blocks.dtype),
            pltpu.HBM(land_shape, blocks.dtype),
            jax.ShapeDtypeStruct((SUBLANES, LANES), F32),
        ),
        input_output_aliases={0: 2, 1: 3},
        compiler_params=pltpu.CompilerParams(has_side_effects=SPLIT_COPY),
    )(pltpu.with_memory_space_constraint(blocks, pltpu.HBM), pltpu.with_memory_space_constraint(lax.empty(land_shape, blocks.dtype), pltpu.HBM))


def _scatter_wait(send_sems, recv_sems, blocks_thru, land_thru, after, name):
    def body(blocks_ref, land_ref, send_sems, recv_sems, after_ref, blocks_dead, got_ref):
        me = _position()
        my = _index(me)
        for k in range(1, N_DEV):
            px, py, pc = to = _flip(me, k)
            _remote(blocks_ref.at[2 * px + py, pc], land_ref.at[my], send_sems.at[k - 1], recv_sems.at[k - 1], to).wait_send()
        for k in range(1, N_DEV):
            slot = land_ref.at[_index(_flip(me, k))]
            _remote(slot, slot, send_sems.at[k - 1], recv_sems.at[k - 1], _flip(me, k)).wait_recv()

    return pl.pallas_call(
        body,
        name=name,
        in_specs=(HBM_SPEC, HBM_SPEC, SEM_SPEC, SEM_SPEC, pl.BlockSpec(memory_space=pl.ANY)),
        out_specs=(HBM_SPEC, HBM_SPEC),
        out_shape=(pltpu.HBM(blocks_thru.shape, blocks_thru.dtype), pltpu.HBM(land_thru.shape, land_thru.dtype)),
        input_output_aliases={0: 0, 1: 1},
        compiler_params=pltpu.CompilerParams(has_side_effects=SPLIT_COPY),
    )(blocks_thru, land_thru, send_sems, recv_sems, after)


def _pack_rows(parts):
    rows, offsets, at = [], [], 0
    for p in parts:
        flat = p.reshape(-1)
        n = -(-flat.shape[0] // (SUBLANES * LANES)) * SUBLANES
        rows.append(jnp.pad(flat, (0, n * LANES - flat.shape[0])).reshape(n, LANES))
        offsets.append(at)
        at += n
    return jnp.concatenate(rows, axis=0), offsets


def _unpack_rows(packed, offsets, shapes):
    out = []
    for off, shape in zip(offsets, shapes):
        size = 1
        for s in shape:
            size *= s
        n = -(-size // (SUBLANES * LANES)) * SUBLANES
        out.append(packed[off : off + n].reshape(-1)[:size].reshape(shape))
    return out


def kernel(x, c, w_ada, b_ada, norm_gain, w_in, q_gain, k_gain, sink, w_s, b_s, w_out, loss_target, m_w_ada, m_b_ada, m_norm_gain, m_w_in, m_q_gain, m_k_gain, m_sink, m_w_s, m_b_s, m_w_out, v_w_ada, v_b_ada, v_norm_gain, v_w_in, v_q_gain, v_k_gain, v_sink, v_w_s, v_b_s, v_w_out):
    seq, d = x.shape[1], x.shape[2]
    n_layers = w_in.shape[0]
    w_cols = w_in.shape[2]
    ada_cols = w_ada.shape[2]
    my = _index(_position())
    xs = x.reshape(seq, d)
    target = loss_target.reshape(seq, d)

    w_in_t0, shard_in, shard_out, c_all, ada_parts = _gather_weights(w_in.transpose(0, 2, 1), w_out, c, w_ada)
    w_in_ts, w_outs = [w_in_t0], []
    ada = ada_parts[:, :, 0, :].transpose(1, 0, 2).reshape(n_layers, 3 * d) + b_ada
    shift, scale1, gate = ada[:, None, 0:d], 1.0 + ada[:, None, d : 2 * d], ada[:, None, 2 * d : 3 * d]
    gain = norm_gain[:, None, :]

    w_s_m = w_s.astype(MXU_DTYPE)
    w_s_t = w_s_m.transpose(0, 1, 3, 2)
    b_st = jnp.repeat(b_s.transpose(0, 2, 1), HEAD_DIM, axis=2)
    q_gain2 = jnp.tile(q_gain, (1, 2))[:, None, :]
    k_gain2 = jnp.tile(k_gain, (1, 2))[:, None, :]

    xl, saved = xs, []
    for l in range(n_layers):
        last = l == n_layers - 1
        pa, pb, *first_w_out = _ln_proj_fwd(
            xl, gain[l], scale1[l], shift[l], w_in_ts[l], f"ln_proj_fwd_{l}", gather=(shard_out[0],) if l == 0 else ()
        )
        w_outs += first_w_out
        if last:
            o = _attn_fwd(pa, q_gain2[l], k_gain2[l], sink[l], f"attn_fwd_{l}")
        else:
            o, w_in_next, w_out_next = _attn_fwd(
                pa, q_gain2[l], k_gain2[l], sink[l], f"attn_fwd_{l}", gather=(shard_in[l + 1], shard_out[l + 1])
            )
            w_in_ts.append(w_in_next)
            w_outs.append(w_out_next)
        saved.append((xl, pa, pb, o))
        out = _mix_out_fwd(pb, o, xl, gate[l], w_outs[l], w_s_m[l], b_st[l], f"mix_out_fwd_{l}", target if last else None)
        if last:
            dx, sq_err = out
        else:
            xl = out

    g_w_in, g_w_out, small, d_ada_rows = [None] * n_layers, [None] * n_layers, [None] * n_layers, [None] * n_layers
    waiting = []
    d_ws_all = [None] * n_layers
    for l in reversed(range(n_layers)):
        x_l, pa, pb, o = saved[l]
        dpb, do, dw_out, d_gate8, d_ws, d_bs = _mix_out_bwd(
            dx, pb, o, gate[l], w_outs[l], w_s_m[l], w_s_t[l], b_st[l], f"mix_out_bwd_{l}"
        )
        riding = waiting + [(g_w_out, l, dw_out.reshape(4, 2, D_MIX // N_DEV, d))]
        attn = _attn_bwd(pa, o, do, q_gain2[l], k_gain2[l], sink[l], f"attn_bwd_{l}", scatter=tuple(b for _, _, b in riding))
        dq, dkv, halo_prev, halo_next, d_qg, d_kg, d_sk = attn[:7]
        for (dest, layer, _), total in zip(riding, _scatter_finish(attn[7:], f"scatter_finish_{l}")):
            dest[layer] = total.transpose(1, 0) if dest is g_w_in else total
        d_ws_all[l] = d_ws
        dw_args = (x_l, gain[l], scale1[l], shift[l], dq, dkv, halo_prev, halo_next, dpb, f"proj_bwd_dw_{l}")
        if l > 0:
            dw_in_t, dkvb = _proj_bwd_dw(*dw_args)
        else:
            d_ws_wire = jnp.stack(d_ws_all).reshape(-1, LANES).astype(jnp.bfloat16)
            dw_in_t, dkvb, gathered_ws = _proj_bwd_dw(*dw_args, gather=(d_ws_wire,))
        blocks_in = dw_in_t.reshape(4, 2, w_cols, d)
        waiting = [(g_w_in, l, blocks_in)] if l > 0 else []
        if l > 0:
            dx, c0, c1 = _proj_bwd_dx(x_l, dx, dq, dkvb, dpb, w_in_ts[l], gain[l], scale1[l], f"proj_bwd_dx_{l}")
        else:
            *in_flight, token = _scatter_start(blocks_in, "scatter_start_in_0")
            dx, c0, c1 = _proj_bwd_dx(
                x_l, dx, dq, dkvb, dpb, w_in_ts[l], gain[l] + token[0, 0], scale1[l], f"proj_bwd_dx_{l}"
            )
            sent, landed = _scatter_wait(*in_flight, dx, "scatter_wait_in_0")
            g_w_in[l] = _scatter_finish((landed,), "scatter_finish_in_0", own=(sent,))[0].transpose(1, 0)
        c0s, c1s = c0.sum(axis=0), c1.sum(axis=0)
        d_ada_rows[l] = jnp.concatenate([c0s, norm_gain[l] * c1s, d_gate8.sum(axis=0)])
        small[l] = (
            scale1[l, 0] * c1s,
            d_qg.sum(axis=0).reshape(N_HEADS, HEAD_DIM).sum(axis=0),
            d_kg.sum(axis=0).reshape(2, HEAD_DIM).sum(axis=0),
            d_sk[0, 0:N_HEADS],
            d_bs.reshape(BLK, N_GROUPS, HEAD_DIM).sum(axis=2).transpose(1, 0),
        )

    names = ("norm_gain", "q_gain", "k_gain", "sink", "b_s")
    stacked = [jnp.stack([small[l][t] for l in range(n_layers)]) for t in range(len(names))]
    d_ada = jnp.stack(d_ada_rows)
    packed, offsets = _pack_rows(stacked + [d_ada, sq_err[0, 0:1]])
    g_w_in, g_w_out = jnp.stack(g_w_in), jnp.stack(g_w_out)
    gathered, *upd = _gather_small(packed, adam=((w_in, g_w_in, m_w_in, v_w_in), (w_out, g_w_out, m_w_out, v_w_out)))
    gathered_ws = gathered_ws.reshape(N_DEV, -1, LANES)
    upd_in, upd_out = upd[0:3], upd[3:6]
    no_weight = jnp.zeros((1,), F32)
    weights = (norm_gain, q_gain, k_gain, sink, b_s, b_ada, no_weight)
    moments_m = (m_norm_gain, m_q_gain, m_k_gain, m_sink, m_b_s, m_b_ada, no_weight)
    moments_v = (v_norm_gain, v_q_gain, v_k_gain, v_sink, v_b_s, v_b_ada, no_weight)
    w_pack, _ = _pack_rows(weights)
    m_pack, _ = _pack_rows(moments_m)
    v_pack, _ = _pack_rows(moments_v)
    shapes = [w.shape for w in weights]
    flat_ws = lambda a: a.reshape(-1, LANES)
    updated = _small_update(gathered, gathered_ws, w_pack, m_pack, v_pack, flat_ws(w_s), flat_ws(m_w_s), flat_ws(v_w_s))
    g_small, d_small, m_small, v_small = (_unpack_rows(p, offsets, shapes) for p in updated[0:4])
    ws_small = [p.reshape(w_s.shape) for p in updated[4:8]]
    loss = g_small[-1][0] * (0.5 / d)

    ada_off = offsets[-2]
    ada_n = -(-n_layers * 3 * d // (SUBLANES * LANES)) * SUBLANES
    d_ada_all = gathered[:, ada_off : ada_off + ada_n].reshape(N_DEV, -1)[:, : n_layers * 3 * d].reshape(N_DEV, n_layers, 3 * d)
    d_ada_cols = lax.dynamic_slice_in_dim(d_ada_all, my * ada_cols, ada_cols, axis=2)
    g_w_ada, *upd_ada = _ada_update(c_all[:, 0, :], d_ada_cols.transpose(1, 0, 2), w_ada, m_w_ada, v_w_ada)

    def ordered(ada_, in_, out_, small_, ws):
        ng, qg, kg, sk, bs, ba, _ = small_
        return (ada_, ba, ng, in_, qg, kg, sk, ws, bs, out_)

    grads = ordered(g_w_ada, g_w_in, g_w_out, g_small, ws_small[0])
    deltas = ordered(upd_ada[0], upd_in[0], upd_out[0], d_small, ws_small[1])
    new_m = ordered(upd_ada[1], upd_in[1], upd_out[1], m_small, ws_small[2])
    new_v = ordered(upd_ada[2], upd_in[2], upd_out[2], v_small, ws_small[3])
    return (loss, dx.reshape(x.shape), *grads, *deltas, *new_m, *new_v)
```

```python
import functools

import jax
import jax.numpy as jnp
from jax import lax
from jax.experimental import pallas as pl
from jax.experimental.pallas import tpu as pltpu

F32 = jnp.float32
MXU_DTYPE = jnp.bfloat16
MESH_ID = pl.DeviceIdType.MESH

N_DEV = 8
HEAD_DIM = 64
N_HEADS = 8
Q_PER_KV = 4
D_ATTN = 512
D_KV = 128
D_GM = 512
N_GROUPS = 8
D_MIX = D_ATTN + D_GM
BLK = 128
LANES = 128
SUBLANES = 8
N_PAIRS = D_ATTN // LANES
D_QKV = D_ATTN + 2 * D_KV
D_REST = D_ATTN + 3 * D_GM
D_IN = D_QKV + D_REST
EPS = 1e-6
NEG_INF = -1e30
ALIBI_SLOPES = tuple(2.0 ** (-8.0 * (h + 1) / N_HEADS) for h in range(N_HEADS))
Q_SCALE = 1.0 / 8.0

ADAM_LR = 0.001
ADAM_B1 = 0.9
ADAM_B2 = 0.999
ADAM_EPS = 1e-08
ADAM_WD = 0.01
ADAM_STEP = 10

TOKEN_TILE = 512
VMEM_LIMIT_BYTES = 56 * 1024 * 1024


def _params(semantics=None):
    return pltpu.CompilerParams(dimension_semantics=semantics, vmem_limit_bytes=VMEM_LIMIT_BYTES)


def _dot(a, b):
    return jnp.dot(a, b, preferred_element_type=F32)


def _dot_nt(a, b):
    return lax.dot_general(a, b, (((1,), (1,)), ((), ())), preferred_element_type=F32)


def _dot_tn(a, b):
    return lax.dot_general(a, b, (((0,), (0,)), ((), ())), preferred_element_type=F32)


def _mx(v):
    return v.astype(MXU_DTYPE)


def _lane_lo(rows):
    return lax.broadcasted_iota(jnp.int32, (rows, LANES), 1) < HEAD_DIM


def _half_ones(width=LANES):
    group_bits = HEAD_DIM.bit_length() - 1
    r = jnp.right_shift(lax.broadcasted_iota(jnp.int32, (width, width), 0), group_bits)
    c = jnp.right_shift(lax.broadcasted_iota(jnp.int32, (width, width), 1), group_bits)
    return jnp.where(r == c, 1.0, 0.0).astype(jnp.bfloat16)


WIDE = 2 * LANES


def _half_sum(v, ones):
    p1 = v.astype(jnp.bfloat16)
    p2 = (v - p1.astype(F32)).astype(jnp.bfloat16)
    return _dot(p1, ones) + _dot(p2, ones)


def _half_rms(v, ones):
    r = lax.rsqrt(_half_sum(v * v, ones) * (1.0 / HEAD_DIM) + EPS)
    return v * r, r


def _half_rms_bwd(dy, vhat, r, ones):
    return r * (dy - vhat * (_half_sum(vhat * dy, ones) * (1.0 / HEAD_DIM)))


def _group_rows(v):
    rows, n = v.shape
    return v.reshape(rows // SUBLANES, SUBLANES, n).sum(axis=0)


def _sigmoid(v):
    return 1.0 / (1.0 + jnp.exp(-v))


ROW_CHUNK = 32
VARIANT_HEADS = ((0, 2, 5, 7), (1, 3, 4, 6))
HEAD_SLOT = {h: (v, s) for v, heads in enumerate(VARIANT_HEADS) for s, h in enumerate(heads)}
STACK = Q_PER_KV * BLK


def _fill_attn_bias(bias_s):
    qi = lax.broadcasted_iota(jnp.int32, (BLK, 3 * BLK), 0)
    ci = lax.broadcasted_iota(jnp.int32, (BLK, 3 * BLK), 1)
    dist = jnp.abs(ci - BLK - qi)
    distf = dist.astype(F32)
    window = dist <= BLK
    for kind, seen in enumerate((window & (ci >= BLK), window, window & (ci < 2 * BLK))):
        for h in range(N_HEADS):
            bias_s[kind, h] = jnp.where(seen, -(ALIBI_SLOPES[h] * distf), NEG_INF)


def _block_kind(block, seq):
    assert seq >= 2 * BLK
    return jnp.where(block == 0, 0, jnp.where(block == seq // BLK - 1, 2, 1))


def _stage_queries(qn, lo_t, j, nb, qs):
    for a in range(2):
        v, slot = HEAD_SLOT[2 * j + a]
        qm = _mx(jnp.where(lo_t, qn, 0.0) if a == 0 else jnp.where(lo_t, 0.0, qn))
        for n in range(nb):
            qs[n, v, slot * BLK : (slot + 1) * BLK, :] = qm[n * BLK : (n + 1) * BLK]


def _unstack_pair(stacked, j, lo):
    (v0, s0), (v1, s1) = HEAD_SLOT[2 * j], HEAD_SLOT[2 * j + 1]
    return jnp.where(lo, stacked[v0][s0 * BLK : (s0 + 1) * BLK], stacked[v1][s1 * BLK : (s1 + 1) * BLK])


def _stage_keys(kvp_ref, qkv_ref, kvn_ref, kg, ones, tile, ks, kr, vs, vr, khat_s=None, rk_s=None):
    pieces = (
        (0, BLK, kvp_ref[:, 0:D_KV], kvp_ref[:, D_KV : 2 * D_KV]),
        (BLK, tile, qkv_ref[:, D_ATTN : D_ATTN + D_KV], qkv_ref[:, D_ATTN + D_KV : D_QKV]),
        (BLK + tile, BLK, kvn_ref[:, 0:D_KV], kvn_ref[:, D_KV : 2 * D_KV]),
    )
    for r0, n, k, v in pieces:
        khat, rk = _half_rms(k, ones)
        kn = khat * kg
        ks[r0 : r0 + n, :] = _mx(kn)
        kr[r0 : r0 + n, :] = _mx(pltpu.roll(kn, HEAD_DIM, 1))
        vs[r0 : r0 + n, :] = _mx(v)
        vr[r0 : r0 + n, :] = _mx(pltpu.roll(v, HEAD_DIM, 1))
        if khat_s is not None:
            khat_s[r0 : r0 + n, :] = khat
            rk_s[r0 : r0 + n, :] = rk


def _halo_specs(tile, seq):
    nb = tile // BLK
    last = seq // BLK - 1
    kv_col = D_ATTN // (2 * D_KV)
    prev = pl.BlockSpec((BLK, 2 * D_KV), lambda i: (jnp.maximum(i * nb - 1, 0), kv_col))
    nxt = pl.BlockSpec((BLK, 2 * D_KV), lambda i: (jnp.minimum((i + 1) * nb, last), kv_col))
    return prev, nxt


def _row_spec(tile, width):
    return pl.BlockSpec((tile, width), lambda i: (i, 0))


def _full_spec(shape):
    nd = len(shape)
    return pl.BlockSpec(shape, lambda i: (0,) * nd)


SMEM_SPEC = pl.BlockSpec(memory_space=pltpu.SMEM)
VMEM_SPEC = pl.BlockSpec(memory_space=pltpu.VMEM)
HBM_SPEC = pl.BlockSpec(memory_space=pltpu.HBM)


def _rider_steps(nt):
    return 0, (2 * nt) // 3, nt - 1


def _gather_rider(sources, gathered, sems, step, nt):
    start, forward, finish = _all_gather_stages(
        [_row_block(g, s.shape[0]) for g, s in zip(gathered, sources)], sems[0], sems[1], sources=sources, local_sems=sems[2]
    )
    at_start, at_forward, at_finish = _rider_steps(nt)
    pl.when(step == at_start)(start)

    def after_compute():
        pl.when(step == at_forward)(forward)
        pl.when(step == at_finish)(finish)

    return after_compute


def _gathered_shapes(gather):
    return [jax.ShapeDtypeStruct((N_DEV * g.shape[0], g.shape[1]), g.dtype) for g in gather]


def _ln_proj_fwd(x, gain, scale1, shift, w_in_t, name):
    seq, d = x.shape
    tile = min(TOKEN_TILE, seq)

    def body(x_ref, g_ref, s1_ref, sh_ref, wt_ref, pa_ref, pb_ref):
        xv = x_ref[...]
        r = lax.rsqrt(jnp.mean(xv * xv, axis=-1, keepdims=True) + EPS)
        h = _mx((xv * r) * g_ref[...] * s1_ref[...] + sh_ref[...])
        pa_ref[...] = _dot_nt(h, wt_ref[0:D_QKV, :])
        pb_ref[...] = _dot_nt(h, wt_ref[D_QKV:D_IN, :])

    vec = _full_spec((1, d))
    return pl.pallas_call(
        body,
        name=name,
        grid=(seq // tile,),
        in_specs=[_row_spec(tile, d), vec, vec, vec, _full_spec((D_IN, d))],
        out_specs=[_row_spec(tile, D_QKV), _row_spec(tile, D_REST)],
        out_shape=[jax.ShapeDtypeStruct((seq, D_QKV), F32), jax.ShapeDtypeStruct((seq, D_REST), F32)],
        compiler_params=_params(("parallel",)),
    )(x, gain, scale1, shift, w_in_t)


def _attn_fwd(pa, q_gain2, k_gain2, sink, name, gather=()):
    seq = pa.shape[0]
    tile = min(TOKEN_TILE, seq)
    nb = tile // BLK
    nt = seq // tile
    ext = tile + 2 * BLK
    n_ride = len(gather)
    riding = n_ride > 0

    def body(sink_ref, qkv_ref, kvp_ref, kvn_ref, qg_ref, kg_ref, *rest):
        i = pl.program_id(0)
        sources, o_ref, gathered = rest[:n_ride], rest[n_ride], rest[n_ride + 1 : 2 * n_ride + 1]
        qs, ks, kr, vs, vr, bias_s, s_scr, p_scr, inv_scr, *sems = rest[2 * n_ride + 1 :]
        if riding:
            after_compute = _gather_rider(sources, gathered, sems, i, nt)

        @pl.when(i == 0)
        def _():
            _fill_attn_bias(bias_s)

        ones = _half_ones()
        lo = _lane_lo(BLK)
        lo_t = _lane_lo(tile)
        _stage_keys(kvp_ref, qkv_ref, kvn_ref, kg_ref[...], ones, tile, ks, kr, vs, vr)
        for j in range(N_PAIRS):
            qhat, _ = _half_rms(qkv_ref[:, j * LANES : (j + 1) * LANES], ones)
            _stage_queries(qhat * (qg_ref[...] * Q_SCALE), lo_t, j, nb, qs)

        def block(n, carry):
            r0 = pl.multiple_of(n * BLK, BLK)
            krows = pl.ds(r0, 3 * BLK)
            kind = _block_kind(i * nb + n, seq)
            for v in range(2):
                s_scr[v] = _dot_nt(qs[n, v], (kr if v else ks)[krows, :])
            for h in range(N_HEADS):
                v, slot = HEAD_SLOT[h]
                sink_h = sink_ref[h]
                for rc in range(0, BLK, ROW_CHUNK):
                    rows = slice(slot * BLK + rc, slot * BLK + rc + ROW_CHUNK)
                    s = s_scr[v, rows, :] + bias_s[kind, h, rc : rc + ROW_CHUNK, :]
                    m = jnp.maximum(jnp.max(s, axis=-1, keepdims=True), sink_h)
                    p = jnp.exp(s - m)
                    total = jnp.sum(p, axis=-1, keepdims=True) + jnp.exp(sink_h - m)
                    p_scr[v, rows, :] = _mx(p)
                    inv_scr[v, rows, :] = jnp.broadcast_to(1.0 / total, (ROW_CHUNK, LANES))
            outs = [_dot(p_scr[v], (vr if v else vs)[krows, :]) * inv_scr[v] for v in range(2)]
            for j in range(N_PAIRS):
                o_ref[pl.ds(r0, BLK), j * LANES : (j + 1) * LANES] = _unstack_pair(outs, j, lo)
            return carry

        lax.fori_loop(0, nb, block, 0)
        if riding:
            after_compute()

    prev, nxt = _halo_specs(tile, seq)
    vec = _full_spec((1, LANES))
    in_specs = [SMEM_SPEC, _row_spec(tile, D_QKV), prev, nxt, vec, vec]
    out_specs = [_row_spec(tile, D_ATTN)]
    out_shape = [jax.ShapeDtypeStruct((seq, D_ATTN), F32)]
    scratch = [
        pltpu.VMEM((nb, 2, STACK, LANES), MXU_DTYPE),
        pltpu.VMEM((ext, LANES), MXU_DTYPE),
        pltpu.VMEM((ext, LANES), MXU_DTYPE),
        pltpu.VMEM((ext, LANES), MXU_DTYPE),
        pltpu.VMEM((ext, LANES), MXU_DTYPE),
        pltpu.VMEM((3, N_HEADS, BLK, 3 * BLK), F32),
        pltpu.VMEM((2, STACK, 3 * BLK), F32),
        pltpu.VMEM((2, STACK, 3 * BLK), MXU_DTYPE),
        pltpu.VMEM((2, STACK, LANES), F32),
    ]
    out = pl.pallas_call(
        body,
        name=name,
        grid=(nt,),
        in_specs=in_specs + [HBM_SPEC] * n_ride,
        out_specs=out_specs + [HBM_SPEC] * n_ride,
        out_shape=out_shape + _gathered_shapes(gather),
        scratch_shapes=scratch + _rider_sems(n_ride),
        compiler_params=_params(("arbitrary",)),
    )(sink, pa, pa, pa, q_gain2, k_gain2, *gather)
    return out if riding else out[0]


def _mix_out_fwd(pb, o, x, gate, w_out, w_s, b_st, name, target=None):
    seq, d = x.shape
    tile = min(TOKEN_TILE, seq)
    nb = tile // BLK
    with_loss = target is not None

    def body(pb_ref, o_ref, x_ref, gate_ref, wo_ref, ws_ref, bs_ref, *rest):
        if with_loss:
            t_ref, xo_ref, acc_ref, y_s, vn_s = rest

            @pl.when(pl.program_id(0) == 0)
            def _():
                acc_ref[...] = jnp.zeros_like(acc_ref)
        else:
            xo_ref, y_s, vn_s = rest
        ones = _half_ones(WIDE)
        lo = _lane_lo(BLK)
        ga = pb_ref[:, 0:D_ATTN]
        y_s[:, 0:D_ATTN] = _mx(o_ref[...] * (ga * _sigmoid(ga)))
        for j in range(D_GM // WIDE):
            vhat, _ = _half_rms(pb_ref[:, 2 * D_GM + j * WIDE : 2 * D_GM + (j + 1) * WIDE], ones)
            vn_s[:, j * WIDE : (j + 1) * WIDE] = _mx(vhat)

        def chunk(n, carry):
            rows = pl.ds(pl.multiple_of(n * BLK, BLK), BLK)
            for j in range(N_PAIRS):
                cols = slice(j * LANES, (j + 1) * LANES)
                vn = vn_s[rows, cols]
                sv = jnp.where(lo, _dot(ws_ref[2 * j], vn), _dot(ws_ref[2 * j + 1], vn)) + bs_ref[:, cols]
                u = pb_ref[rows, D_ATTN + j * LANES : D_ATTN + (j + 1) * LANES]
                gg = pb_ref[rows, D_ATTN + 2 * D_GM + j * LANES : D_ATTN + 2 * D_GM + (j + 1) * LANES]
                y_s[rows, D_ATTN + j * LANES : D_ATTN + (j + 1) * LANES] = _mx((u * sv) * (gg * _sigmoid(gg)))
            return carry

        lax.fori_loop(0, nb, chunk, 0)
        y = x_ref[...] + gate_ref[...] * _dot(y_s[...], wo_ref[...])
        if with_loss:
            e = y - t_ref[...]
            xo_ref[...] = e * (1.0 / d)
            acc_ref[...] += jnp.sum(jnp.sum(e * e, axis=-1, keepdims=True), axis=0, keepdims=True)
        else:
            xo_ref[...] = y

    row = _row_spec(tile, d)
    acc_shape = (SUBLANES, LANES)
    return pl.pallas_call(
        body,
        name=name,
        grid=(seq // tile,),
        in_specs=[
            _row_spec(tile, D_REST),
            _row_spec(tile, D_ATTN),
            row,
            _full_spec((1, d)),
            _full_spec((D_MIX, d)),
            _full_spec((N_GROUPS, BLK, BLK)),
            _full_spec((BLK, D_GM)),
        ]
        + ([row] if with_loss else []),
        out_specs=[row, _full_spec(acc_shape)] if with_loss else row,
        out_shape=[jax.ShapeDtypeStruct((seq, d), F32), jax.ShapeDtypeStruct(acc_shape, F32)]
        if with_loss
        else jax.ShapeDtypeStruct((seq, d), F32),
        scratch_shapes=[pltpu.VMEM((tile, D_MIX), MXU_DTYPE), pltpu.VMEM((tile, D_GM), MXU_DTYPE)],
        compiler_params=_params(("arbitrary",) if with_loss else ("parallel",)),
    )(pb, o, x, gate, w_out, w_s, b_st, *([target] if with_loss else []))


def _mix_out_bwd(dxn, pb, o, gate, w_out, w_s, w_s_t, b_st, name):
    seq, d = dxn.shape
    tile = min(TOKEN_TILE, seq)
    nb = tile // BLK
    nt = seq // tile

    def body(dxn_ref, pb_ref, o_ref, gate_ref, wo_ref, ws_ref, wst_ref, bs_ref,
             dpb_ref, do_ref, dwo_ref, dgate_ref, dws_ref, dbs_ref, g_ref, y_s, dy_s, vn_s, rv_s, vnb_s, sv_s, dsv_s, dvn_s):
        @pl.when(pl.program_id(0) == 0)
        def _():
            g_ref[...] = jnp.zeros_like(g_ref)
            dws_ref[...] = jnp.zeros_like(dws_ref)
            dbs_ref[...] = jnp.zeros_like(dbs_ref)

        ones = _half_ones(WIDE)
        lo = _lane_lo(BLK)
        c_u = slice(D_ATTN, D_ATTN + D_GM)
        c_vg = slice(D_ATTN + D_GM, D_ATTN + 2 * D_GM)
        c_gg = slice(D_ATTN + 2 * D_GM, D_REST)
        dxv = dxn_ref[...]
        dy_s[...] = _dot_nt(_mx(dxv * gate_ref[...]), wo_ref[...])
        ga = pb_ref[:, 0:D_ATTN]
        sig = _sigmoid(ga)
        sil = ga * sig
        ov = o_ref[...]
        y_s[:, 0:D_ATTN] = _mx(ov * sil)
        da = dy_s[:, 0:D_ATTN]
        do_ref[...] = da * sil
        dpb_ref[:, 0:D_ATTN] = (da * ov * (sig * (1.0 + ga * (1.0 - sig)))).astype(dpb_ref.dtype)
        for j in range(D_GM // WIDE):
            cols = slice(j * WIDE, (j + 1) * WIDE)
            vhat, rv = _half_rms(pb_ref[:, 2 * D_GM + j * WIDE : 2 * D_GM + (j + 1) * WIDE], ones)
            vn_s[:, cols] = vhat
            rv_s[:, cols] = rv
            vnb_s[:, cols] = _mx(vhat)

        def spatial_fwd(n, carry):
            rows = pl.ds(pl.multiple_of(n * BLK, BLK), BLK)
            for j in range(N_PAIRS):
                cols = slice(j * LANES, (j + 1) * LANES)
                vn = vnb_s[rows, cols]
                sv_s[rows, cols] = jnp.where(lo, _dot(ws_ref[2 * j], vn), _dot(ws_ref[2 * j + 1], vn)) + bs_ref[:, cols]
            return carry

        lax.fori_loop(0, nb, spatial_fwd, 0)

        def gating(n, carry):
            rows = pl.ds(pl.multiple_of(n * BLK, BLK), BLK)
            sv = sv_s[rows, :]
            u = pb_ref[rows, c_u]
            gg = pb_ref[rows, c_gg]
            sg = _sigmoid(gg)
            silg = gg * sg
            m0 = u * sv
            y_s[rows, D_ATTN:D_MIX] = _mx(m0 * silg)
            dm = dy_s[rows, D_ATTN:D_MIX]
            dm0 = dm * silg
            dpb_ref[rows, c_gg] = (dm * m0 * (sg * (1.0 + gg * (1.0 - sg)))).astype(dpb_ref.dtype)
            dpb_ref[rows, c_u] = (dm0 * sv).astype(dpb_ref.dtype)
            dsv = dm0 * u
            dsv_s[rows, :] = _mx(dsv)
            dbs_ref[...] += dsv
            return carry

        lax.fori_loop(0, nb, gating, 0)

        def spatial_bwd(n, carry):
            rows = pl.ds(pl.multiple_of(n * BLK, BLK), BLK)
            for j in range(N_PAIRS):
                cols = slice(j * LANES, (j + 1) * LANES)
                dsv = dsv_s[rows, cols]
                dvn_s[rows, cols] = jnp.where(lo, _dot(wst_ref[2 * j], dsv), _dot(wst_ref[2 * j + 1], dsv))
            return carry

        lax.fori_loop(0, nb, spatial_bwd, 0)
        zero = jnp.zeros((BLK, LANES), MXU_DTYPE)
        for j in range(N_PAIRS):
            cols = slice(j * LANES, (j + 1) * LANES)
            chunks = [dsv_s[n * BLK : (n + 1) * BLK, cols] for n in range(nb)]
            vn_all = jnp.concatenate([vnb_s[n * BLK : (n + 1) * BLK, cols] for n in range(nb)], axis=1)
            dws_ref[2 * j] += _dot_nt(jnp.concatenate([jnp.where(lo, c, zero) for c in chunks], axis=1), vn_all)
            dws_ref[2 * j + 1] += _dot_nt(jnp.concatenate([jnp.where(lo, zero, c) for c in chunks], axis=1), vn_all)
        for j in range(D_GM // WIDE):
            cols = slice(j * WIDE, (j + 1) * WIDE)
            dpb_ref[:, D_ATTN + D_GM + j * WIDE : D_ATTN + D_GM + (j + 1) * WIDE] = _half_rms_bwd(
                dvn_s[:, cols], vn_s[:, cols], rv_s[:, cols], ones
            ).astype(dpb_ref.dtype)
        g_ref[...] += _dot_tn(y_s[...], _mx(dxv))

        @pl.when(pl.program_id(0) == nt - 1)
        def _():
            gv = g_ref[...]
            dwo_ref[...] = (gv * gate_ref[...]).astype(dwo_ref.dtype)
            dgate_ref[...] = _group_rows(gv * wo_ref[...].astype(F32))

    return pl.pallas_call(
        body,
        name=name,
        grid=(seq // tile,),
        in_specs=[
            _row_spec(tile, d),
            _row_spec(tile, D_REST),
            _row_spec(tile, D_ATTN),
            _full_spec((1, d)),
            _full_spec((D_MIX, d)),
            _full_spec((N_GROUPS, BLK, BLK)),
            _full_spec((N_GROUPS, BLK, BLK)),
            _full_spec((BLK, D_GM)),
        ],
        out_specs=[
            _row_spec(tile, D_REST),
            _row_spec(tile, D_ATTN),
            _full_spec((D_MIX, d)),
            _full_spec((SUBLANES, d)),
            _full_spec((N_GROUPS, BLK, BLK)),
            _full_spec((BLK, D_GM)),
        ],
        out_shape=[
            jax.ShapeDtypeStruct((seq, D_REST), MXU_DTYPE),
            jax.ShapeDtypeStruct((seq, D_ATTN), F32),
            jax.ShapeDtypeStruct((D_MIX, d), jnp.bfloat16),
            jax.ShapeDtypeStruct((SUBLANES, d), F32),
            jax.ShapeDtypeStruct((N_GROUPS, BLK, BLK), F32),
            jax.ShapeDtypeStruct((BLK, D_GM), F32),
        ],
        scratch_shapes=[
            pltpu.VMEM((D_MIX, d), F32),
            pltpu.VMEM((tile, D_MIX), MXU_DTYPE),
            pltpu.VMEM((tile, D_MIX), F32),
            pltpu.VMEM((tile, D_GM), F32),
            pltpu.VMEM((tile, D_GM), F32),
            pltpu.VMEM((tile, D_GM), MXU_DTYPE),
            pltpu.VMEM((tile, D_GM), F32),
            pltpu.VMEM((tile, D_GM), MXU_DTYPE),
            pltpu.VMEM((tile, D_GM), F32),
        ],
        compiler_params=_params(("arbitrary",)),
    )(dxn, pb, o, gate, w_out, w_s, w_s_t, b_st)


def _attn_bwd(pa, o, do, q_gain2, k_gain2, sink, name, scatter=()):
    seq = pa.shape[0]
    tile = min(TOKEN_TILE, seq)
    nb = tile // BLK
    nt = seq // tile
    ext = tile + 2 * BLK
    n_ride = len(scatter)
    riding = n_ride > 0

    def body(sink_ref, qkv_ref, kvp_ref, kvn_ref, o_ref, do_ref, qg_ref, kg_ref, *rest):
        i = pl.program_id(0)
        blocks, rest = rest[:n_ride], rest[n_ride:]
        dq_ref, dkv_ref, hp_ref, hn_ref, dqg_ref, dkg_ref, dsk_ref = rest[:7]
        landing, rest = rest[7 : 7 + n_ride], rest[7 + n_ride :]
        (qs, dos, qhat_s, rq_s, ks, kr, vs, vr, khat_s, rk_s, dqn_s, dka, dva, bias_s, s_scr, dp_scr, p_scr, ds_scr) = rest[:18]
        if riding:
            start, finish = _scatter_stages(blocks, landing, *rest[18:])
            at_start, _, at_finish = _rider_steps(nt)
            pl.when(i == at_start)(start)

        @pl.when(i == 0)
        def _():
            dqg_ref[...] = jnp.zeros_like(dqg_ref)
            dkg_ref[...] = jnp.zeros_like(dkg_ref)
            dsk_ref[...] = jnp.zeros_like(dsk_ref)
            _fill_attn_bias(bias_s)

        ones = _half_ones()
        lo = _lane_lo(BLK)
        lo_t = _lane_lo(tile)
        lo_c = _lane_lo(ROW_CHUNK)
        qg = qg_ref[...] * Q_SCALE
        kg = kg_ref[...]
        _stage_keys(kvp_ref, qkv_ref, kvn_ref, kg, ones, tile, ks, kr, vs, vr, khat_s, rk_s)
        for j in range(N_PAIRS):
            cols = slice(j * LANES, (j + 1) * LANES)
            qhat, rq = _half_rms(qkv_ref[:, cols], ones)
            qhat_s[:, cols] = qhat
            rq_s[:, cols] = rq
            _stage_queries(qhat * qg, lo_t, j, nb, qs)
            _stage_queries(do_ref[:, cols], lo_t, j, nb, dos)
        dka[...] = jnp.zeros_like(dka)
        dva[...] = jnp.zeros_like(dva)
        head_lane = lax.broadcasted_iota(jnp.int32, (1, LANES), 1)

        def block(n, dsink):
            r0 = pl.multiple_of(n * BLK, BLK)
            krows = pl.ds(r0, 3 * BLK)
            kind = _block_kind(i * nb + n, seq)
            for v in range(2):
                s_scr[v] = _dot_nt(qs[n, v], (kr if v else ks)[krows, :])
                dp_scr[v] = _dot_nt(dos[n, v], (vr if v else vs)[krows, :])
            for h in range(N_HEADS):
                v, slot = HEAD_SLOT[h]
                j, a = divmod(h, 2)
                cols = slice(j * LANES, (j + 1) * LANES)
                sink_h = sink_ref[h]
                sink_part = jnp.zeros((ROW_CHUNK, 1), F32)
                for rc in range(0, BLK, ROW_CHUNK):
                    rows = slice(slot * BLK + rc, slot * BLK + rc + ROW_CHUNK)
                    trows = pl.ds(pl.multiple_of(r0 + rc, ROW_CHUNK), ROW_CHUNK)
                    s = s_scr[v, rows, :] + bias_s[kind, h, rc : rc + ROW_CHUNK, :]
                    m = jnp.maximum(jnp.max(s, axis=-1, keepdims=True), sink_h)
                    p = jnp.exp(s - m)
                    e_sink = jnp.exp(sink_h - m)
                    inv = 1.0 / (jnp.sum(p, axis=-1, keepdims=True) + e_sink)
                    pn = p * inv
                    prod = do_ref[trows, cols] * o_ref[trows, cols]
                    prod = jnp.where(lo_c, prod, 0.0) if a == 0 else jnp.where(lo_c, 0.0, prod)
                    dcol = jnp.sum(prod, axis=-1, keepdims=True)
                    ds_scr[v, rows, :] = _mx(pn * (dp_scr[v, rows, :] - dcol))
                    p_scr[v, rows, :] = _mx(pn)
                    sink_part = sink_part + (e_sink * inv) * dcol
                dsink = dsink - jnp.where(head_lane == h, jnp.sum(sink_part, axis=0, keepdims=True), 0.0)
            dqv = []
            for v in range(2):
                dqv.append(_dot(ds_scr[v], (kr if v else ks)[krows, :]))
                dka[v, krows, :] += _dot_tn(ds_scr[v], qs[n, v])
                dva[v, krows, :] += _dot_tn(p_scr[v], dos[n, v])
            for j in range(N_PAIRS):
                dqn_s[pl.ds(r0, BLK), j * LANES : (j + 1) * LANES] = _unstack_pair(dqv, j, lo)
            return dsink

        dsink = lax.fori_loop(0, nb, block, jnp.zeros((1, LANES), F32))
        dsk_ref[...] += jnp.broadcast_to(dsink, (SUBLANES, LANES))
        for j in range(N_PAIRS):
            cols = slice(j * LANES, (j + 1) * LANES)
            dqn = dqn_s[:, cols]
            qhat = qhat_s[:, cols]
            dqg_ref[:, cols] += _group_rows(dqn * qhat) * Q_SCALE
            dq_ref[:, cols] = _half_rms_bwd(dqn * qg, qhat, rq_s[:, cols], ones).astype(dq_ref.dtype)
        dkn = dka[0] + pltpu.roll(dka[1], HEAD_DIM, 1)
        khat = khat_s[...]
        dkg_ref[...] += _group_rows(dkn * khat)
        dk = _half_rms_bwd(dkn * kg, khat, rk_s[...], ones)
        dv = dva[0] + pltpu.roll(dva[1], HEAD_DIM, 1)
        hp_ref[:, 0:D_KV] = dk[0:BLK]
        hp_ref[:, D_KV : 2 * D_KV] = dv[0:BLK]
        dkv_ref[:, 0:D_KV] = dk[BLK : BLK + tile]
        dkv_ref[:, D_KV : 2 * D_KV] = dv[BLK : BLK + tile]
        hn_ref[:, 0:D_KV] = dk[BLK + tile : ext]
        hn_ref[:, D_KV : 2 * D_KV] = dv[BLK + tile : ext]
        if riding:
            pl.when(i == at_finish)(finish)

    prev, nxt = _halo_specs(tile, seq)
    vec = _full_spec((1, LANES))
    halo = pl.BlockSpec((None, BLK, 2 * D_KV), lambda i: (i, 0, 0))
    return pl.pallas_call(
        body,
        name=name,
        grid=(nt,),
        in_specs=[SMEM_SPEC, _row_spec(tile, D_QKV), prev, nxt, _row_spec(tile, D_ATTN), _row_spec(tile, D_ATTN), vec, vec]
        + [HBM_SPEC] * n_ride,
        out_specs=[
            _row_spec(tile, D_ATTN),
            _row_spec(tile, 2 * D_KV),
            halo,
            halo,
            _full_spec((SUBLANES, D_ATTN)),
            _full_spec((SUBLANES, LANES)),
            _full_spec((SUBLANES, LANES)),
        ]
        + [HBM_SPEC] * n_ride,
        out_shape=[
            jax.ShapeDtypeStruct((seq, D_ATTN), MXU_DTYPE),
            jax.ShapeDtypeStruct((seq, 2 * D_KV), F32),
            jax.ShapeDtypeStruct((nt, BLK, 2 * D_KV), F32),
            jax.ShapeDtypeStruct((nt, BLK, 2 * D_KV), F32),
            jax.ShapeDtypeStruct((SUBLANES, D_ATTN), F32),
            jax.ShapeDtypeStruct((SUBLANES, LANES), F32),
            jax.ShapeDtypeStruct((SUBLANES, LANES), F32),
        ]
        + _landing_shapes(scatter),
        scratch_shapes=[
            pltpu.VMEM((nb, 2, STACK, LANES), MXU_DTYPE),
            pltpu.VMEM((nb, 2, STACK, LANES), MXU_DTYPE),
            pltpu.VMEM((tile, D_ATTN), F32),
            pltpu.VMEM((tile, D_ATTN), F32),
            pltpu.VMEM((ext, LANES), MXU_DTYPE),
            pltpu.VMEM((ext, LANES), MXU_DTYPE),
            pltpu.VMEM((ext, LANES), MXU_DTYPE),
            pltpu.VMEM((ext, LANES), MXU_DTYPE),
            pltpu.VMEM((ext, LANES), F32),
            pltpu.VMEM((ext, LANES), F32),
            pltpu.VMEM((tile, D_ATTN), F32),
            pltpu.VMEM((2, ext, LANES), F32),
            pltpu.VMEM((2, ext, LANES), F32),
            pltpu.VMEM((3, N_HEADS, BLK, 3 * BLK), F32),
            pltpu.VMEM((2, STACK, 3 * BLK), F32),
            pltpu.VMEM((2, STACK, 3 * BLK), F32),
            pltpu.VMEM((2, STACK, 3 * BLK), MXU_DTYPE),
            pltpu.VMEM((2, STACK, 3 * BLK), MXU_DTYPE),
        ]
        + _rider_sems(n_ride),
        compiler_params=_params(("arbitrary",)),
    )(sink, pa, pa, pa, o, do, q_gain2, k_gain2, *scatter)


def _halo_in_specs(tile, nt):
    from_prev = pl.BlockSpec((None, BLK, 2 * D_KV), lambda i: (jnp.maximum(i - 1, 0), 0, 0))
    from_next = pl.BlockSpec((None, BLK, 2 * D_KV), lambda i: (jnp.minimum(i + 1, nt - 1), 0, 0))
    return from_prev, from_next


def _landing_shapes(scatter):
    return [jax.ShapeDtypeStruct((N_DEV,) + b.shape[2:], b.dtype) for b in scatter]


def _rider_sems(n_ride):
    if not n_ride:
        return []
    return [pltpu.SemaphoreType.DMA((7 * n_ride,)), pltpu.SemaphoreType.DMA((7 * n_ride,)), pltpu.SemaphoreType.DMA((n_ride,))]


def _proj_bwd_dx(x, dxn, dq, dkvb, dpb, w_in_t, gain, scale1, name):
    seq, d = x.shape
    tile = min(TOKEN_TILE, seq)

    def row(width):
        return _row_spec(tile, width)

    def body(x_ref, dxn_ref, dq_ref, dkvb_ref, dpb_ref, wt_ref, g_ref, s1_ref, dx_ref, c0_ref, c1_ref):
        @pl.when(pl.program_id(0) == 0)
        def _():
            c0_ref[...] = jnp.zeros_like(c0_ref)
            c1_ref[...] = jnp.zeros_like(c1_ref)

        dh = (
            _dot(dq_ref[...], wt_ref[0:D_ATTN, :])
            + _dot(dkvb_ref[...], wt_ref[D_ATTN:D_QKV, :])
            + _dot(dpb_ref[...], wt_ref[D_QKV:D_IN, :])
        )
        xv = x_ref[...]
        r = lax.rsqrt(jnp.mean(xv * xv, axis=-1, keepdims=True) + EPS)
        xn = xv * r
        c0_ref[...] += _group_rows(dh)
        c1_ref[...] += _group_rows(dh * xn)
        dxn_ = dh * (g_ref[...] * s1_ref[...])
        dx_ref[...] = dxn_ref[...] + r * (dxn_ - xn * jnp.mean(xn * dxn_, axis=-1, keepdims=True))

    vec = _full_spec((1, d))
    return pl.pallas_call(
        body,
        name=name,
        grid=(seq // tile,),
        in_specs=[row(d), row(d), row(D_ATTN), row(2 * D_KV), row(D_REST), _full_spec((D_IN, d)), vec, vec],
        out_specs=[row(d), _full_spec((SUBLANES, d)), _full_spec((SUBLANES, d))],
        out_shape=[
            jax.ShapeDtypeStruct((seq, d), F32),
            jax.ShapeDtypeStruct((SUBLANES, d), F32),
            jax.ShapeDtypeStruct((SUBLANES, d), F32),
        ],
        compiler_params=_params(("arbitrary",)),
    )(x, dxn, dq, dkvb, dpb, w_in_t, gain, scale1)


def _proj_bwd_dw(x, gain, scale1, shift, dq, dkv, halo_prev, halo_next, dpb, name, gather=()):
    seq, d = x.shape
    tile = min(TOKEN_TILE, seq)
    nt = seq // tile
    assert tile >= 2 * BLK
    n_ride = len(gather)

    def body(x_ref, g_ref, s1_ref, sh_ref, dq_ref, dkv_ref, hn_ref, hp_ref, dpb_ref, *rest):
        i = pl.program_id(0)
        sources, rest = rest[:n_ride], rest[n_ride:]
        dw_ref, dkvb_ref = rest[:2]
        gathered, (acc, *sems) = rest[2 : 2 + n_ride], rest[2 + n_ride :]
        after_compute = _gather_rider(sources, gathered, sems, i, nt) if n_ride else None

        @pl.when(i == 0)
        def _():
            acc[...] = jnp.zeros_like(acc)

        top = dkv_ref[0:BLK, :] + jnp.where(i > 0, hn_ref[...], 0.0)
        bot = dkv_ref[tile - BLK : tile, :] + jnp.where(i < nt - 1, hp_ref[...], 0.0)
        dkvb_ref[0:BLK, :] = top.astype(dkvb_ref.dtype)
        dkvb_ref[tile - BLK : tile, :] = bot.astype(dkvb_ref.dtype)
        if tile > 2 * BLK:
            dkvb_ref[BLK : tile - BLK, :] = dkv_ref[BLK : tile - BLK, :].astype(dkvb_ref.dtype)
        xv = x_ref[...]
        r = lax.rsqrt(jnp.mean(xv * xv, axis=-1, keepdims=True) + EPS)
        h = _mx((xv * r) * g_ref[...] * s1_ref[...] + sh_ref[...])
        acc[0:D_ATTN, :] += _dot_tn(dq_ref[...], h)
        acc[D_ATTN:D_QKV, :] += _dot_tn(dkvb_ref[...], h)
        acc[D_QKV:D_IN, :] += _dot_tn(dpb_ref[...], h)

        @pl.when(i == nt - 1)
        def _():
            dw_ref[...] = acc[...].astype(dw_ref.dtype)

        if n_ride:
            after_compute()

    from_prev, from_next = _halo_in_specs(tile, nt)
    vec = _full_spec((1, d))
    return pl.pallas_call(
        body,
        name=name,
        grid=(nt,),
        in_specs=[
            _row_spec(tile, d),
            vec,
            vec,
            vec,
            _row_spec(tile, D_ATTN),
            _row_spec(tile, 2 * D_KV),
            from_prev,
            from_next,
            _row_spec(tile, D_REST),
        ]
        + [HBM_SPEC] * n_ride,
        out_specs=[_full_spec((D_IN, d)), _row_spec(tile, 2 * D_KV)] + [HBM_SPEC] * n_ride,
        out_shape=[jax.ShapeDtypeStruct((D_IN, d), jnp.bfloat16), jax.ShapeDtypeStruct((seq, 2 * D_KV), MXU_DTYPE)]
        + _gathered_shapes(gather),
        scratch_shapes=[pltpu.VMEM((D_IN, d), F32)] + _rider_sems(n_ride),
        compiler_params=_params(("arbitrary",)),
    )(x, gain, scale1, shift, dq, dkv, halo_next, halo_prev, dpb, *gather)


def _adamw_math(w, g, m, v):
    m = ADAM_B1 * m + (1.0 - ADAM_B1) * g
    v = ADAM_B2 * v + (1.0 - ADAM_B2) * (g * g)
    m_hat = m / (1.0 - ADAM_B1**ADAM_STEP)
    v_hat = v / (1.0 - ADAM_B2**ADAM_STEP)
    delta = -ADAM_LR * (m_hat / (jnp.sqrt(v_hat) + ADAM_EPS) + ADAM_WD * w)
    return delta, m, v


def _small_update(gathered, gathered_ws, w, m, v, ws, m_ws, v_ws):
    def body(ga_ref, gws_ref, w_ref, m_ref, v_ref, ws_ref, mws_ref, vws_ref, *outs):
        for src, refs, out in ((ga_ref, (w_ref, m_ref, v_ref), outs[0:4]), (gws_ref, (ws_ref, mws_ref, vws_ref), outs[4:8])):
            g = src[0].astype(F32)
            for j in range(1, N_DEV):
                g = g + src[j].astype(F32)
            out[0][...] = g
            out[1][...], out[2][...], out[3][...] = _adamw_math(refs[0][...], g, refs[1][...], refs[2][...])

    shapes = [jax.ShapeDtypeStruct(w.shape, F32)] * 4 + [jax.ShapeDtypeStruct(ws.shape, F32)] * 4
    return pl.pallas_call(
        body,
        name="small_update",
        in_specs=[VMEM_SPEC] * 8,
        out_specs=[VMEM_SPEC] * 8,
        out_shape=shapes,
        compiler_params=_params(),
    )(gathered, gathered_ws, w, m, v, ws, m_ws, v_ws)


def _ada_update(c_all, d_ada_cols, w, m, v):
    n_layers = w.shape[0]

    def body(c_ref, da_ref, w_ref, m_ref, v_ref, g_ref, d_ref, mo_ref, vo_ref):
        cv = c_ref[...]
        cond = cv * _sigmoid(cv)
        for l in range(n_layers):
            g = lax.dot_general(
                cond, da_ref[l], (((0,), (0,)), ((), ())), preferred_element_type=F32, precision=lax.Precision.HIGHEST
            )
            g_ref[l] = g
            d_ref[l], mo_ref[l], vo_ref[l] = _adamw_math(w_ref[l], g, m_ref[l], v_ref[l])

    return pl.pallas_call(
        body,
        name="ada_update",
        in_specs=[VMEM_SPEC] * 5,
        out_specs=[VMEM_SPEC] * 4,
        out_shape=[jax.ShapeDtypeStruct(w.shape, F32)] * 4,
        compiler_params=_params(),
    )(c_all, d_ada_cols, w, m, v)


def _position():
    return lax.axis_index("x"), lax.axis_index("y"), lax.axis_index("c")


def _flip(pos, k):
    x, y, c = pos
    return (1 - x if k & 4 else x, 1 - y if k & 2 else y, 1 - c if k & 1 else c)


def _index(pos):
    x, y, c = pos
    return 4 * x + 2 * y + c


def _remote(src, dst, send_sem, recv_sem, to):
    return pltpu.make_async_remote_copy(
        src_ref=src, dst_ref=dst, send_sem=send_sem, recv_sem=recv_sem, device_id=to, device_id_type=MESH_ID
    )


def _all_gather_stages(slots, send_sems, recv_sems, sources=None, local_sems=None):
    me = _position()
    sibling = _flip(me, 1)
    others = (4, 2, 6)
    arrays = range(len(slots))

    def copy(t, k, block, to, own=False):
        slot = slots[t](_index(block))
        src = sources[t] if own and sources is not None else slot
        return _remote(src, slot, send_sems.at[7 * t + k], recv_sems.at[7 * t + k], to)

    def first(t):
        return [copy(t, 0, me, sibling, own=True)] + [copy(t, 1 + j, me, _flip(me, f), own=True) for j, f in enumerate(others)]

    def passed(t, j):
        return copy(t, 4 + j, _flip(me, others[j]), sibling)

    def local(t):
        return pltpu.make_async_copy(sources[t], slots[t](_index(me)), local_sems.at[t])

    def start():
        for t in arrays:
            if sources is not None:
                local(t).start()
            for cp in first(t):
                cp.start()

    def forward():
        for j, f in enumerate(others):
            for t in arrays:
                copy(t, 1 + j, _flip(me, f), me).wait_recv()
                passed(t, j).start()

    def finish():
        for t in arrays:
            copy(t, 0, sibling, me).wait_recv()
            for j, f in enumerate(others):
                copy(t, 4 + j, _flip(sibling, f), me).wait_recv()
        for t in arrays:
            for cp in first(t) + [passed(t, j) for j in range(len(others))]:
                cp.wait_send()
            if sources is not None:
                local(t).wait()

    return start, forward, finish


def _two_level_all_gather(slots, send_sems, recv_sems, between=None):
    start, forward, finish = _all_gather_stages(slots, send_sems, recv_sems)
    start()
    if between is not None:
        between()
    forward()
    finish()


def _row_block(ref, rows):
    return lambda j: ref.at[pl.ds(pl.multiple_of(j * rows, 16), rows), :]


def _scatter_stages(blocks, landing, send_sems, recv_sems, local_sems):
    me = _position()
    my = _index(me)
    arrays = range(len(blocks))

    def copy(t, k):
        px, py, pc = to = _flip(me, k)
        return _remote(blocks[t].at[2 * px + py, pc], landing[t].at[my], send_sems.at[7 * t + k - 1], recv_sems.at[7 * t + k - 1], to)

    def arrival(t, k):
        slot = landing[t].at[_index(_flip(me, k))]
        return _remote(slot, slot, send_sems.at[7 * t + k - 1], recv_sems.at[7 * t + k - 1], _flip(me, k))

    def local(t):
        x, y, c = me
        return pltpu.make_async_copy(blocks[t].at[2 * x + y, c], landing[t].at[my], local_sems.at[t])

    def start():
        for t in arrays:
            local(t).start()
            for k in range(1, N_DEV):
                copy(t, k).start()

    def finish():
        for t in arrays:
            for k in range(1, N_DEV):
                arrival(t, k).wait_recv()
        for t in arrays:
            for k in range(1, N_DEV):
                copy(t, k).wait_send()
            local(t).wait()

    return start, finish


def _ada_exchange(c_ref, w_ref, call_ref, parts_ref, sbuf, sem_s1, sem_r1, sem_s2, sem_r2):
    d = c_ref.shape[-1]
    n_layers = w_ref.shape[0]
    me = _position()
    my = _index(me)
    call_ref[my] = jnp.broadcast_to(c_ref[...], (SUBLANES, d))
    mine = call_ref.at[my]
    first = [_remote(mine, mine, sem_s1.at[k - 1], sem_r1.at[k - 1], _flip(me, k)) for k in range(1, N_DEV)]
    for cp in first:
        cp.start()
    for k in range(1, N_DEV):
        theirs = call_ref.at[_index(_flip(me, k))]
        _remote(theirs, theirs, sem_s1.at[k - 1], sem_r1.at[k - 1], _flip(me, k)).wait_recv()
    cv = call_ref[...].reshape(N_DEV * SUBLANES, d)
    cond = cv * _sigmoid(cv)
    for l in range(n_layers):
        rows = jnp.dot(cond, w_ref[l], preferred_element_type=F32, precision=lax.Precision.HIGHEST)
        for b in range(N_DEV):
            sbuf[b, l] = rows[b * SUBLANES : (b + 1) * SUBLANES]
    parts_ref[my] = sbuf[my]
    second = []
    for k in range(1, N_DEV):
        to = _flip(me, k)
        second.append(_remote(sbuf.at[_index(to)], parts_ref.at[my], sem_s2.at[k - 1], sem_r2.at[k - 1], to))
    for cp in second:
        cp.start()
    for k in range(1, N_DEV):
        theirs = parts_ref.at[_index(_flip(me, k))]
        _remote(theirs, theirs, sem_s2.at[k - 1], sem_r2.at[k - 1], _flip(me, k)).wait_recv()
    for cp in first + second:
        cp.wait_send()


def _gather_weights(w_in_t, w_out, c_row, w_ada):
    n_layers, rows_in, d = w_in_t.shape
    width = w_ada.shape[2]

    def body(wi_ref, wo_ref, c_ref, wa_ref, gi_ref, si_ref, so_ref, call_ref, parts_ref, sbuf, send_sems, recv_sems, *ada_sems):
        my = _index(_position())
        si_ref[...] = wi_ref[...].astype(si_ref.dtype)
        so_ref[...] = wo_ref[...].astype(so_ref.dtype)
        gi_ref[pl.ds(pl.multiple_of(my * rows_in, 16), rows_in), :] = si_ref[0]
        _two_level_all_gather(
            (_row_block(gi_ref, rows_in),),
            send_sems,
            recv_sems,
            between=functools.partial(_ada_exchange, c_ref, wa_ref, call_ref, parts_ref, sbuf, *ada_sems),
        )

    return pl.pallas_call(
        body,
        name="gather_weights",
        in_specs=[VMEM_SPEC] * 4,
        out_specs=[VMEM_SPEC] * 5,
        out_shape=[
            jax.ShapeDtypeStruct((N_DEV * rows_in, d), MXU_DTYPE),
            jax.ShapeDtypeStruct(w_in_t.shape, MXU_DTYPE),
            jax.ShapeDtypeStruct(w_out.shape, MXU_DTYPE),
            jax.ShapeDtypeStruct((N_DEV, SUBLANES, d), F32),
            jax.ShapeDtypeStruct((N_DEV, n_layers, SUBLANES, width), F32),
        ],
        scratch_shapes=[
            pltpu.VMEM((N_DEV, n_layers, SUBLANES, width), F32),
            pltpu.SemaphoreType.DMA((7,)),
            pltpu.SemaphoreType.DMA((7,)),
        ]
        + [pltpu.SemaphoreType.DMA((N_DEV - 1,))] * 4,
        compiler_params=_params(),
    )(w_in_t, w_out, c_row, w_ada)


def _gather_small(packed, adam=()):
    n_adam = len(adam)

    def body(p_ref, *rest):
        quads = [rest[4 * t : 4 * t + 4] for t in range(n_adam)]
        rest = rest[4 * n_adam :]
        g_ref = rest[0]
        results = [rest[1 + 3 * t : 4 + 3 * t] for t in range(n_adam)]
        send_sems, recv_sems = rest[1 + 3 * n_adam :]
        g_ref[_index(_position())] = p_ref[...]

        def updates():
            for (w_ref, gr_ref, m_ref, v_ref), (d_ref, mo_ref, vo_ref) in zip(quads, results):
                d_ref[...], mo_ref[...], vo_ref[...] = _adamw_math(w_ref[...], gr_ref[...], m_ref[...], v_ref[...])

        _two_level_all_gather((lambda j: g_ref.at[j],), send_sems, recv_sems, between=updates)

    return pl.pallas_call(
        body,
        name="gather_small",
        in_specs=[VMEM_SPEC] * (1 + 4 * n_adam),
        out_specs=[VMEM_SPEC] * (1 + 3 * n_adam),
        out_shape=[jax.ShapeDtypeStruct((N_DEV,) + packed.shape, F32)]
        + [jax.ShapeDtypeStruct(q[0].shape, F32) for q in adam for _ in range(3)],
        scratch_shapes=[pltpu.SemaphoreType.DMA((7,)), pltpu.SemaphoreType.DMA((7,))],
        compiler_params=_params(),
    )(packed, *[a for q in adam for a in q])


def _scatter_finish(landed, name, own=()):
    n = len(landed)

    def body(*refs):
        if own:
            x, y, c = _position()
            my = _index((x, y, c))
        for t, (src, out) in enumerate(zip(refs[:n], refs[n + len(own) :])):
            g = None
            for j in range(N_DEV):
                part = src[j].astype(F32)
                if own:
                    part = jnp.where(j == my, refs[n + t][2 * x + y, c].astype(F32), part)
                g = part if g is None else g + part
            out[...] = g

    return pl.pallas_call(
        body,
        name=name,
        in_specs=[VMEM_SPEC] * (n + len(own)),
        out_specs=[VMEM_SPEC] * n,
        out_shape=[jax.ShapeDtypeStruct(a.shape[1:], F32) for a in landed],
        compiler_params=_params(),
    )(*landed, *own)


SEM_SPEC = pl.BlockSpec(memory_space=pltpu.SEMAPHORE)
SPLIT_COPY = pltpu.SideEffectType.DATAFLOW_SIDE_EFFECTING


def _scatter_start(blocks, name):
    land_shape = (N_DEV,) + blocks.shape[2:]

    def body(blocks_ref, land_ref, send_sems, recv_sems, blocks_thru, land_thru, token):
        me = _position()
        my = _index(me)
        for k in range(1, N_DEV):
            px, py, pc = to = _flip(me, k)
            _remote(blocks_ref.at[2 * px + py, pc], land_ref.at[my], send_sems.at[k - 1], recv_sems.at[k - 1], to).start()
        token[...] = jnp.zeros_like(token)

    return pl.pallas_call(
        body,
        name=name,
        in_specs=(HBM_SPEC, HBM_SPEC),
        out_specs=(SEM_SPEC, SEM_SPEC, HBM_SPEC, HBM_SPEC, VMEM_SPEC),
        out_shape=(
            pltpu.SemaphoreType.DMA((N_DEV - 1,)),
            pltpu.SemaphoreType.DMA((N_DEV - 1,)),
            pltpu.HBM(blocks.shape, blocks.dtype),
            pltpu.HBM(land_shape, blocks.dtype),
            jax.ShapeDtypeStruct((SUBLANES, LANES), F32),
        ),
        input_output_aliases={0: 2, 1: 3},
        compiler_params=pltpu.CompilerParams(has_side_effects=SPLIT_COPY),
    )(pltpu.with_memory_space_constraint(blocks, pltpu.HBM), pltpu.with_memory_space_constraint(lax.empty(land_shape, blocks.dtype), pltpu.HBM))


def _scatter_wait(send_sems, recv_sems, blocks_thru, land_thru, after, name):
    def body(blocks_ref, land_ref, send_sems, recv_sems, after_ref, blocks_dead, got_ref):
        me = _position()
        my = _index(me)
        for k in range(1, N_DEV):
            px, py, pc = to = _flip(me, k)
            _remote(blocks_ref.at[2 * px + py, pc], land_ref.at[my], send_sems.at[k - 1], recv_sems.at[k - 1], to).wait_send()
        for k in range(1, N_DEV):
            slot = land_ref.at[_index(_flip(me, k))]
            _remote(slot, slot, send_sems.at[k - 1], recv_sems.at[k - 1], _flip(me, k)).wait_recv()

    return pl.pallas_call(
        body,
        name=name,
        in_specs=(HBM_SPEC, HBM_SPEC, SEM_SPEC, SEM_SPEC, pl.BlockSpec(memory_space=pl.ANY)),
        out_specs=(HBM_SPEC, HBM_SPEC),
        out_shape=(pltpu.HBM(blocks_thru.shape, blocks_thru.dtype), pltpu.HBM(land_thru.shape, land_thru.dtype)),
        input_output_aliases={0: 0, 1: 1},
        compiler_params=pltpu.CompilerParams(has_side_effects=SPLIT_COPY),
    )(blocks_thru, land_thru, send_sems, recv_sems, after)


def _pack_rows(parts):
    rows, offsets, at = [], [], 0
    for p in parts:
        flat = p.reshape(-1)
        n = -(-flat.shape[0] // (SUBLANES * LANES)) * SUBLANES
        rows.append(jnp.pad(flat, (0, n * LANES - flat.shape[0])).reshape(n, LANES))
        offsets.append(at)
        at += n
    return jnp.concatenate(rows, axis=0), offsets


def _unpack_rows(packed, offsets, shapes):
    out = []
    for off, shape in zip(offsets, shapes):
        size = 1
        for s in shape:
            size *= s
        n = -(-size // (SUBLANES * LANES)) * SUBLANES
        out.append(packed[off : off + n].reshape(-1)[:size].reshape(shape))
    return out


def kernel(x, c, w_ada, b_ada, norm_gain, w_in, q_gain, k_gain, sink, w_s, b_s, w_out, loss_target, m_w_ada, m_b_ada, m_norm_gain, m_w_in, m_q_gain, m_k_gain, m_sink, m_w_s, m_b_s, m_w_out, v_w_ada, v_b_ada, v_norm_gain, v_w_in, v_q_gain, v_k_gain, v_sink, v_w_s, v_b_s, v_w_out):
    seq, d = x.shape[1], x.shape[2]
    n_layers = w_in.shape[0]
    w_cols = w_in.shape[2]
    ada_cols = w_ada.shape[2]
    my = _index(_position())
    xs = x.reshape(seq, d)
    target = loss_target.reshape(seq, d)

    w_in_t0, shard_in, shard_out, c_all, ada_parts = _gather_weights(w_in.transpose(0, 2, 1), w_out, c, w_ada)
    w_in_ts, w_outs = [w_in_t0], []
    ada = ada_parts[:, :, 0, :].transpose(1, 0, 2).reshape(n_layers, 3 * d) + b_ada
    shift, scale1, gate = ada[:, None, 0:d], 1.0 + ada[:, None, d : 2 * d], ada[:, None, 2 * d : 3 * d]
    gain = norm_gain[:, None, :]

    w_s_m = w_s.astype(MXU_DTYPE)
    w_s_t = w_s_m.transpose(0, 1, 3, 2)
    b_st = jnp.repeat(b_s.transpose(0, 2, 1), HEAD_DIM, axis=2)
    q_gain2 = jnp.tile(q_gain, (1, 2))[:, None, :]
    k_gain2 = jnp.tile(k_gain, (1, 2))[:, None, :]

    xl, saved = xs, []
    for l in range(n_layers):
        last = l == n_layers - 1
        pa, pb = _ln_proj_fwd(xl, gain[l], scale1[l], shift[l], w_in_ts[l], f"ln_proj_fwd_{l}")
        wanted = ([shard_out[0]] if l == 0 else []) + ([] if last else [shard_out[l + 1], shard_in[l + 1]])
        if wanted:
            o, *arrived = _attn_fwd(pa, q_gain2[l], k_gain2[l], sink[l], f"attn_fwd_{l}", gather=tuple(wanted))
            if not last:
                w_in_ts.append(arrived.pop())
            w_outs += arrived
        else:
            o = _attn_fwd(pa, q_gain2[l], k_gain2[l], sink[l], f"attn_fwd_{l}")
        saved.append((xl, pa, pb, o))
        out = _mix_out_fwd(pb, o, xl, gate[l], w_outs[l], w_s_m[l], b_st[l], f"mix_out_fwd_{l}", target if last else None)
        if last:
            dx, sq_err = out
        else:
            xl = out

    g_w_in, g_w_out, small, d_ada_rows = [None] * n_layers, [None] * n_layers, [None] * n_layers, [None] * n_layers
    waiting = []
    d_ws_all = [None] * n_layers
    for l in reversed(range(n_layers)):
        x_l, pa, pb, o = saved[l]
        dpb, do, dw_out, d_gate8, d_ws, d_bs = _mix_out_bwd(
            dx, pb, o, gate[l], w_outs[l], w_s_m[l], w_s_t[l], b_st[l], f"mix_out_bwd_{l}"
        )
        waiting.append((g_w_out, l, dw_out.reshape(4, 2, D_MIX // N_DEV, d)))
        riding, waiting = ([], waiting) if 0 < l == n_layers - 1 else (waiting, [])
        attn = _attn_bwd(pa, o, do, q_gain2[l], k_gain2[l], sink[l], f"attn_bwd_{l}", scatter=tuple(b for _, _, b in riding))
        dq, dkv, halo_prev, halo_next, d_qg, d_kg, d_sk = attn[:7]
        if riding:
            for (dest, layer, _), total in zip(riding, _scatter_finish(attn[7:], f"scatter_finish_{l}")):
                dest[layer] = total.transpose(1, 0) if dest is g_w_in else total
        d_ws_all[l] = d_ws
        dw_args = (x_l, gain[l], scale1[l], shift[l], dq, dkv, halo_prev, halo_next, dpb, f"proj_bwd_dw_{l}")
        if l > 0:
            dw_in_t, dkvb = _proj_bwd_dw(*dw_args)
        else:
            d_ws_wire = jnp.stack(d_ws_all).reshape(-1, LANES).astype(jnp.bfloat16)
            dw_in_t, dkvb, gathered_ws = _proj_bwd_dw(*dw_args, gather=(d_ws_wire,))
        blocks_in = dw_in_t.reshape(4, 2, w_cols, d)
        if l > 0:
            waiting.append((g_w_in, l, blocks_in))
            dx, c0, c1 = _proj_bwd_dx(x_l, dx, dq, dkvb, dpb, w_in_ts[l], gain[l], scale1[l], f"proj_bwd_dx_{l}")
        else:
            *in_flight, token = _scatter_start(blocks_in, "scatter_start_in_0")
            dx, c0, c1 = _proj_bwd_dx(
                x_l, dx, dq, dkvb, dpb, w_in_ts[l], gain[l] + token[0, 0], scale1[l], f"proj_bwd_dx_{l}"
            )
            sent, landed = _scatter_wait(*in_flight, dx, "scatter_wait_in_0")
            g_w_in[l] = _scatter_finish((landed,), "scatter_finish_in_0", own=(sent,))[0].transpose(1, 0)
        c0s, c1s = c0.sum(axis=0), c1.sum(axis=0)
        d_ada_rows[l] = jnp.concatenate([c0s, norm_gain[l] * c1s, d_gate8.sum(axis=0)])
        small[l] = (
            scale1[l, 0] * c1s,
            d_qg.sum(axis=0).reshape(N_HEADS, HEAD_DIM).sum(axis=0),
            d_kg.sum(axis=0).reshape(2, HEAD_DIM).sum(axis=0),
            d_sk[0, 0:N_HEADS],
            d_bs.reshape(BLK, N_GROUPS, HEAD_DIM).sum(axis=2).transpose(1, 0),
        )

    names = ("norm_gain", "q_gain", "k_gain", "sink", "b_s")
    stacked = [jnp.stack([small[l][t] for l in range(n_layers)]) for t in range(len(names))]
    d_ada = jnp.stack(d_ada_rows)
    packed, offsets = _pack_rows(stacked + [d_ada, sq_err[0, 0:1]])
    g_w_in, g_w_out = jnp.stack(g_w_in), jnp.stack(g_w_out)
    gathered, *upd = _gather_small(packed, adam=((w_in, g_w_in, m_w_in, v_w_in), (w_out, g_w_out, m_w_out, v_w_out)))
    gathered_ws = gathered_ws.reshape(N_DEV, -1, LANES)
    upd_in, upd_out = upd[0:3], upd[3:6]
    no_weight = jnp.zeros((1,), F32)
    weights = (norm_gain, q_gain, k_gain, sink, b_s, b_ada, no_weight)
    moments_m = (m_norm_gain, m_q_gain, m_k_gain, m_sink, m_b_s, m_b_ada, no_weight)
    moments_v = (v_norm_gain, v_q_gain, v_k_gain, v_sink, v_b_s, v_b_ada, no_weight)
    w_pack, _ = _pack_rows(weights)
    m_pack, _ = _pack_rows(moments_m)
    v_pack, _ = _pack_rows(moments_v)
    shapes = [w.shape for w in weights]
    flat_ws = lambda a: a.reshape(-1, LANES)
    updated = _small_update(gathered, gathered_ws, w_pack, m_pack, v_pack, flat_ws(w_s), flat_ws(m_w_s), flat_ws(v_w_s))
    g_small, d_small, m_small, v_small = (_unpack_rows(p, offsets, shapes) for p in updated[0:4])
    ws_small = [p.reshape(w_s.shape) for p in updated[4:8]]
    loss = g_small[-1][0] * (0.5 / d)

    ada_off = offsets[-2]
    ada_n = -(-n_layers * 3 * d // (SUBLANES * LANES)) * SUBLANES
    d_ada_all = gathered[:, ada_off : ada_off + ada_n].reshape(N_DEV, -1)[:, : n_layers * 3 * d].reshape(N_DEV, n_layers, 3 * d)
    d_ada_cols = lax.dynamic_slice_in_dim(d_ada_all, my * ada_cols, ada_cols, axis=2)
    g_w_ada, *upd_ada = _ada_update(c_all[:, 0, :], d_ada_cols.transpose(1, 0, 2), w_ada, m_w_ada, v_w_ada)

    def ordered(ada_, in_, out_, small_, ws):
        ng, qg, kg, sk, bs, ba, _ = small_
        return (ada_, ba, ng, in_, qg, kg, sk, ws, bs, out_)

    grads = ordered(g_w_ada, g_w_in, g_w_out, g_small, ws_small[0])
    deltas = ordered(upd_ada[0], upd_in[0], upd_out[0], d_small, ws_small[1])
    new_m = ordered(upd_ada[1], upd_in[1], upd_out[1], m_small, ws_small[2])
    new_v = ordered(upd_ada[2], upd_in[2], upd_out[2], v_small, ws_small[3])
    return (loss, dx.reshape(x.shape), *grads, *deltas, *new_m, *new_v)
```

```python
import functools

import jax
import jax.numpy as jnp
from jax import lax
from jax.experimental import pallas as pl
from jax.experimental.pallas import tpu as pltpu

F32 = jnp.float32
MXU_DTYPE = jnp.bfloat16
MESH_ID = pl.DeviceIdType.MESH

N_DEV = 8
HEAD_DIM = 64
N_HEADS = 8
Q_PER_KV = 4
D_ATTN = 512
D_KV = 128
D_GM = 512
N_GROUPS = 8
D_MIX = D_ATTN + D_GM
BLK = 128
LANES = 128
SUBLANES = 8
N_PAIRS = D_ATTN // LANES
D_QKV = D_ATTN + 2 * D_KV
D_REST = D_ATTN + 3 * D_GM
D_IN = D_QKV + D_REST
EPS = 1e-6
NEG_INF = -1e30
ALIBI_SLOPES = tuple(2.0 ** (-8.0 * (h + 1) / N_HEADS) for h in range(N_HEADS))
Q_SCALE = 1.0 / 8.0

ADAM_LR = 0.001
ADAM_B1 = 0.9
ADAM_B2 = 0.999
ADAM_EPS = 1e-08
ADAM_WD = 0.01
ADAM_STEP = 10

TOKEN_TILE = 512
VMEM_LIMIT_BYTES = 56 * 1024 * 1024


def _params(semantics=None):
    return pltpu.CompilerParams(dimension_semantics=semantics, vmem_limit_bytes=VMEM_LIMIT_BYTES)


def _dot(a, b):
    return jnp.dot(a, b, preferred_element_type=F32)


def _dot_nt(a, b):
    return lax.dot_general(a, b, (((1,), (1,)), ((), ())), preferred_element_type=F32)


def _dot_tn(a, b):
    return lax.dot_general(a, b, (((0,), (0,)), ((), ())), preferred_element_type=F32)


def _mx(v):
    return v.astype(MXU_DTYPE)


def _lane_lo(rows):
    return lax.broadcasted_iota(jnp.int32, (rows, LANES), 1) < HEAD_DIM


def _half_ones(width=LANES):
    group_bits = HEAD_DIM.bit_length() - 1
    r = jnp.right_shift(lax.broadcasted_iota(jnp.int32, (width, width), 0), group_bits)
    c = jnp.right_shift(lax.broadcasted_iota(jnp.int32, (width, width), 1), group_bits)
    return jnp.where(r == c, 1.0, 0.0).astype(jnp.bfloat16)


WIDE = 2 * LANES


def _half_sum(v, ones):
    p1 = v.astype(jnp.bfloat16)
    p2 = (v - p1.astype(F32)).astype(jnp.bfloat16)
    return _dot(p1, ones) + _dot(p2, ones)


def _half_rms(v, ones):
    r = lax.rsqrt(_half_sum(v * v, ones) * (1.0 / HEAD_DIM) + EPS)
    return v * r, r


def _half_rms_bwd(dy, vhat, r, ones):
    return r * (dy - vhat * (_half_sum(vhat * dy, ones) * (1.0 / HEAD_DIM)))


def _group_rows(v):
    rows, n = v.shape
    return v.reshape(rows // SUBLANES, SUBLANES, n).sum(axis=0)


def _sigmoid(v):
    return 1.0 / (1.0 + jnp.exp(-v))


ROW_CHUNK = 32
VARIANT_HEADS = ((0, 2, 5, 7), (1, 3, 4, 6))
HEAD_SLOT = {h: (v, s) for v, heads in enumerate(VARIANT_HEADS) for s, h in enumerate(heads)}
STACK = Q_PER_KV * BLK


def _fill_attn_bias(bias_s):
    qi = lax.broadcasted_iota(jnp.int32, (BLK, 3 * BLK), 0)
    ci = lax.broadcasted_iota(jnp.int32, (BLK, 3 * BLK), 1)
    dist = jnp.abs(ci - BLK - qi)
    distf = dist.astype(F32)
    window = dist <= BLK
    for kind, seen in enumerate((window & (ci >= BLK), window, window & (ci < 2 * BLK))):
        for h in range(N_HEADS):
            bias_s[kind, h] = jnp.where(seen, -(ALIBI_SLOPES[h] * distf), NEG_INF)


def _block_kind(block, seq):
    assert seq >= 2 * BLK
    return jnp.where(block == 0, 0, jnp.where(block == seq // BLK - 1, 2, 1))


def _stage_queries(qn, lo_t, j, nb, qs):
    for a in range(2):
        v, slot = HEAD_SLOT[2 * j + a]
        qm = _mx(jnp.where(lo_t, qn, 0.0) if a == 0 else jnp.where(lo_t, 0.0, qn))
        for n in range(nb):
            qs[n, v, slot * BLK : (slot + 1) * BLK, :] = qm[n * BLK : (n + 1) * BLK]


def _unstack_pair(stacked, j, lo):
    (v0, s0), (v1, s1) = HEAD_SLOT[2 * j], HEAD_SLOT[2 * j + 1]
    return jnp.where(lo, stacked[v0][s0 * BLK : (s0 + 1) * BLK], stacked[v1][s1 * BLK : (s1 + 1) * BLK])


def _stage_keys(kvp_ref, qkv_ref, kvn_ref, kg, ones, tile, ks, kr, vs, vr, khat_s=None, rk_s=None):
    pieces = (
        (0, BLK, kvp_ref[:, 0:D_KV], kvp_ref[:, D_KV : 2 * D_KV]),
        (BLK, tile, qkv_ref[:, D_ATTN : D_ATTN + D_KV], qkv_ref[:, D_ATTN + D_KV : D_QKV]),
        (BLK + tile, BLK, kvn_ref[:, 0:D_KV], kvn_ref[:, D_KV : 2 * D_KV]),
    )
    for r0, n, k, v in pieces:
        khat, rk = _half_rms(k, ones)
        kn = khat * kg
        ks[r0 : r0 + n, :] = _mx(kn)
        kr[r0 : r0 + n, :] = _mx(pltpu.roll(kn, HEAD_DIM, 1))
        vs[r0 : r0 + n, :] = _mx(v)
        vr[r0 : r0 + n, :] = _mx(pltpu.roll(v, HEAD_DIM, 1))
        if khat_s is not None:
            khat_s[r0 : r0 + n, :] = khat
            rk_s[r0 : r0 + n, :] = rk


def _halo_specs(tile, seq):
    nb = tile // BLK
    last = seq // BLK - 1
    kv_col = D_ATTN // (2 * D_KV)
    prev = pl.BlockSpec((BLK, 2 * D_KV), lambda i: (jnp.maximum(i * nb - 1, 0), kv_col))
    nxt = pl.BlockSpec((BLK, 2 * D_KV), lambda i: (jnp.minimum((i + 1) * nb, last), kv_col))
    return prev, nxt


def _row_spec(tile, width):
    return pl.BlockSpec((tile, width), lambda i: (i, 0))


def _full_spec(shape):
    nd = len(shape)
    return pl.BlockSpec(shape, lambda i: (0,) * nd)


SMEM_SPEC = pl.BlockSpec(memory_space=pltpu.SMEM)
VMEM_SPEC = pl.BlockSpec(memory_space=pltpu.VMEM)
HBM_SPEC = pl.BlockSpec(memory_space=pltpu.HBM)


def _rider_steps(nt):
    return 0, (3 * nt) // 4, nt - 1


def _gather_rider(sources, gathered, sems, step, nt):
    start, forward, finish = _all_gather_stages(
        [_row_block(g, s.shape[0]) for g, s in zip(gathered, sources)], sems[0], sems[1], sources=sources, local_sems=sems[2]
    )
    at_start, at_forward, at_finish = _rider_steps(nt)
    pl.when(step == at_start)(start)

    def after_compute():
        pl.when(step == at_forward)(forward)
        pl.when(step == at_finish)(finish)

    return after_compute


def _gathered_shapes(gather):
    return [jax.ShapeDtypeStruct((N_DEV * g.shape[0], g.shape[1]), g.dtype) for g in gather]


def _ln_proj_fwd(x, gain, scale1, shift, w_in_t, name):
    seq, d = x.shape
    tile = min(TOKEN_TILE, seq)

    def body(x_ref, g_ref, s1_ref, sh_ref, wt_ref, pa_ref, pb_ref):
        xv = x_ref[...]
        r = lax.rsqrt(jnp.mean(xv * xv, axis=-1, keepdims=True) + EPS)
        h = _mx((xv * r) * g_ref[...] * s1_ref[...] + sh_ref[...])
        pa_ref[...] = _dot_nt(h, wt_ref[0:D_QKV, :])
        pb_ref[...] = _dot_nt(h, wt_ref[D_QKV:D_IN, :])

    vec = _full_spec((1, d))
    return pl.pallas_call(
        body,
        name=name,
        grid=(seq // tile,),
        in_specs=[_row_spec(tile, d), vec, vec, vec, _full_spec((D_IN, d))],
        out_specs=[_row_spec(tile, D_QKV), _row_spec(tile, D_REST)],
        out_shape=[jax.ShapeDtypeStruct((seq, D_QKV), F32), jax.ShapeDtypeStruct((seq, D_REST), F32)],
        compiler_params=_params(("parallel",)),
    )(x, gain, scale1, shift, w_in_t)


def _attn_fwd(pa, q_gain2, k_gain2, sink, name, gather=()):
    seq = pa.shape[0]
    tile = min(TOKEN_TILE, seq)
    nb = tile // BLK
    nt = seq // tile
    ext = tile + 2 * BLK
    n_ride = len(gather)
    riding = n_ride > 0

    def body(sink_ref, qkv_ref, kvp_ref, kvn_ref, qg_ref, kg_ref, *rest):
        i = pl.program_id(0)
        sources, o_ref, gathered = rest[:n_ride], rest[n_ride], rest[n_ride + 1 : 2 * n_ride + 1]
        qs, ks, kr, vs, vr, bias_s, s_scr, p_scr, inv_scr, *sems = rest[2 * n_ride + 1 :]
        if riding:
            after_compute = _gather_rider(sources, gathered, sems, i, nt)

        @pl.when(i == 0)
        def _():
            _fill_attn_bias(bias_s)

        ones = _half_ones()
        lo = _lane_lo(BLK)
        lo_t = _lane_lo(tile)
        _stage_keys(kvp_ref, qkv_ref, kvn_ref, kg_ref[...], ones, tile, ks, kr, vs, vr)
        for j in range(N_PAIRS):
            qhat, _ = _half_rms(qkv_ref[:, j * LANES : (j + 1) * LANES], ones)
            _stage_queries(qhat * (qg_ref[...] * Q_SCALE), lo_t, j, nb, qs)

        def block(n, carry):
            r0 = pl.multiple_of(n * BLK, BLK)
            krows = pl.ds(r0, 3 * BLK)
            kind = _block_kind(i * nb + n, seq)
            for v in range(2):
                s_scr[v] = _dot_nt(qs[n, v], (kr if v else ks)[krows, :])
            for h in range(N_HEADS):
                v, slot = HEAD_SLOT[h]
                sink_h = sink_ref[h]
                for rc in range(0, BLK, ROW_CHUNK):
                    rows = slice(slot * BLK + rc, slot * BLK + rc + ROW_CHUNK)
                    s = s_scr[v, rows, :] + bias_s[kind, h, rc : rc + ROW_CHUNK, :]
                    m = jnp.maximum(jnp.max(s, axis=-1, keepdims=True), sink_h)
                    p = jnp.exp(s - m)
                    total = jnp.sum(p, axis=-1, keepdims=True) + jnp.exp(sink_h - m)
                    p_scr[v, rows, :] = _mx(p)
                    inv_scr[v, rows, :] = jnp.broadcast_to(1.0 / total, (ROW_CHUNK, LANES))
            outs = [_dot(p_scr[v], (vr if v else vs)[krows, :]) * inv_scr[v] for v in range(2)]
            for j in range(N_PAIRS):
                o_ref[pl.ds(r0, BLK), j * LANES : (j + 1) * LANES] = _unstack_pair(outs, j, lo)
            return carry

        lax.fori_loop(0, nb, block, 0)
        if riding:
            after_compute()

    prev, nxt = _halo_specs(tile, seq)
    vec = _full_spec((1, LANES))
    in_specs = [SMEM_SPEC, _row_spec(tile, D_QKV), prev, nxt, vec, vec]
    out_specs = [_row_spec(tile, D_ATTN)]
    out_shape = [jax.ShapeDtypeStruct((seq, D_ATTN), F32)]
    scratch = [
        pltpu.VMEM((nb, 2, STACK, LANES), MXU_DTYPE),
        pltpu.VMEM((ext, LANES), MXU_DTYPE),
        pltpu.VMEM((ext, LANES), MXU_DTYPE),
        pltpu.VMEM((ext, LANES), MXU_DTYPE),
        pltpu.VMEM((ext, LANES), MXU_DTYPE),
        pltpu.VMEM((3, N_HEADS, BLK, 3 * BLK), F32),
        pltpu.VMEM((2, STACK, 3 * BLK), F32),
        pltpu.VMEM((2, STACK, 3 * BLK), MXU_DTYPE),
        pltpu.VMEM((2, STACK, LANES), F32),
    ]
    out = pl.pallas_call(
        body,
        name=name,
        grid=(nt,),
        in_specs=in_specs + [HBM_SPEC] * n_ride,
        out_specs=out_specs + [HBM_SPEC] * n_ride,
        out_shape=out_shape + _gathered_shapes(gather),
        scratch_shapes=scratch + _rider_sems(n_ride),
        compiler_params=_params(("arbitrary",)),
    )(sink, pa, pa, pa, q_gain2, k_gain2, *gather)
    return out if riding else out[0]


def _mix_out_fwd(pb, o, x, gate, w_out, w_s, b_st, name, target=None):
    seq, d = x.shape
    tile = min(TOKEN_TILE, seq)
    nb = tile // BLK
    with_loss = target is not None

    def body(pb_ref, o_ref, x_ref, gate_ref, wo_ref, ws_ref, bs_ref, *rest):
        if with_loss:
            t_ref, xo_ref, acc_ref, y_s, vn_s = rest

            @pl.when(pl.program_id(0) == 0)
            def _():
                acc_ref[...] = jnp.zeros_like(acc_ref)
        else:
            xo_ref, y_s, vn_s = rest
        ones = _half_ones(WIDE)
        lo = _lane_lo(BLK)
        ga = pb_ref[:, 0:D_ATTN]
        y_s[:, 0:D_ATTN] = _mx(o_ref[...] * (ga * _sigmoid(ga)))
        for j in range(D_GM // WIDE):
            vhat, _ = _half_rms(pb_ref[:, 2 * D_GM + j * WIDE : 2 * D_GM + (j + 1) * WIDE], ones)
            vn_s[:, j * WIDE : (j + 1) * WIDE] = _mx(vhat)

        def chunk(n, carry):
            rows = pl.ds(pl.multiple_of(n * BLK, BLK), BLK)
            for j in range(N_PAIRS):
                cols = slice(j * LANES, (j + 1) * LANES)
                vn = vn_s[rows, cols]
                sv = jnp.where(lo, _dot(ws_ref[2 * j], vn), _dot(ws_ref[2 * j + 1], vn)) + bs_ref[:, cols]
                u = pb_ref[rows, D_ATTN + j * LANES : D_ATTN + (j + 1) * LANES]
                gg = pb_ref[rows, D_ATTN + 2 * D_GM + j * LANES : D_ATTN + 2 * D_GM + (j + 1) * LANES]
                y_s[rows, D_ATTN + j * LANES : D_ATTN + (j + 1) * LANES] = _mx((u * sv) * (gg * _sigmoid(gg)))
            return carry

        lax.fori_loop(0, nb, chunk, 0)
        y = x_ref[...] + gate_ref[...] * _dot(y_s[...], wo_ref[...])
        if with_loss:
            e = y - t_ref[...]
            xo_ref[...] = e * (1.0 / d)
            acc_ref[...] += jnp.sum(jnp.sum(e * e, axis=-1, keepdims=True), axis=0, keepdims=True)
        else:
            xo_ref[...] = y

    row = _row_spec(tile, d)
    acc_shape = (SUBLANES, LANES)
    return pl.pallas_call(
        body,
        name=name,
        grid=(seq // tile,),
        in_specs=[
            _row_spec(tile, D_REST),
            _row_spec(tile, D_ATTN),
            row,
            _full_spec((1, d)),
            _full_spec((D_MIX, d)),
            _full_spec((N_GROUPS, BLK, BLK)),
            _full_spec((BLK, D_GM)),
        ]
        + ([row] if with_loss else []),
        out_specs=[row, _full_spec(acc_shape)] if with_loss else row,
        out_shape=[jax.ShapeDtypeStruct((seq, d), F32), jax.ShapeDtypeStruct(acc_shape, F32)]
        if with_loss
        else jax.ShapeDtypeStruct((seq, d), F32),
        scratch_shapes=[pltpu.VMEM((tile, D_MIX), MXU_DTYPE), pltpu.VMEM((tile, D_GM), MXU_DTYPE)],
        compiler_params=_params(("arbitrary",) if with_loss else ("parallel",)),
    )(pb, o, x, gate, w_out, w_s, b_st, *([target] if with_loss else []))


def _mix_out_bwd(dxn, pb, o, gate, w_out, w_s, w_s_t, b_st, name):
    seq, d = dxn.shape
    tile = min(TOKEN_TILE, seq)
    nb = tile // BLK
    nt = seq // tile

    def body(dxn_ref, pb_ref, o_ref, gate_ref, wo_ref, ws_ref, wst_ref, bs_ref,
             dpb_ref, do_ref, dwo_ref, dgate_ref, dws_ref, dbs_ref, g_ref, y_s, dy_s, vn_s, rv_s, vnb_s, sv_s, dsv_s, dvn_s):
        @pl.when(pl.program_id(0) == 0)
        def _():
            g_ref[...] = jnp.zeros_like(g_ref)
            dws_ref[...] = jnp.zeros_like(dws_ref)
            dbs_ref[...] = jnp.zeros_like(dbs_ref)

        ones = _half_ones(WIDE)
        lo = _lane_lo(BLK)
        c_u = slice(D_ATTN, D_ATTN + D_GM)
        c_vg = slice(D_ATTN + D_GM, D_ATTN + 2 * D_GM)
        c_gg = slice(D_ATTN + 2 * D_GM, D_REST)
        dxv = dxn_ref[...]
        dy_s[...] = _dot_nt(_mx(dxv * gate_ref[...]), wo_ref[...])
        ga = pb_ref[:, 0:D_ATTN]
        sig = _sigmoid(ga)
        sil = ga * sig
        ov = o_ref[...]
        y_s[:, 0:D_ATTN] = _mx(ov * sil)
        da = dy_s[:, 0:D_ATTN]
        do_ref[...] = da * sil
        dpb_ref[:, 0:D_ATTN] = (da * ov * (sig * (1.0 + ga * (1.0 - sig)))).astype(dpb_ref.dtype)
        for j in range(D_GM // WIDE):
            cols = slice(j * WIDE, (j + 1) * WIDE)
            vhat, rv = _half_rms(pb_ref[:, 2 * D_GM + j * WIDE : 2 * D_GM + (j + 1) * WIDE], ones)
            vn_s[:, cols] = vhat
            rv_s[:, cols] = rv
            vnb_s[:, cols] = _mx(vhat)

        def spatial_fwd(n, carry):
            rows = pl.ds(pl.multiple_of(n * BLK, BLK), BLK)
            for j in range(N_PAIRS):
                cols = slice(j * LANES, (j + 1) * LANES)
                vn = vnb_s[rows, cols]
                sv_s[rows, cols] = jnp.where(lo, _dot(ws_ref[2 * j], vn), _dot(ws_ref[2 * j + 1], vn)) + bs_ref[:, cols]
            return carry

        lax.fori_loop(0, nb, spatial_fwd, 0)

        def gating(n, carry):
            rows = pl.ds(pl.multiple_of(n * BLK, BLK), BLK)
            sv = sv_s[rows, :]
            u = pb_ref[rows, c_u]
            gg = pb_ref[rows, c_gg]
            sg = _sigmoid(gg)
            silg = gg * sg
            m0 = u * sv
            y_s[rows, D_ATTN:D_MIX] = _mx(m0 * silg)
            dm = dy_s[rows, D_ATTN:D_MIX]
            dm0 = dm * silg
            dpb_ref[rows, c_gg] = (dm * m0 * (sg * (1.0 + gg * (1.0 - sg)))).astype(dpb_ref.dtype)
            dpb_ref[rows, c_u] = (dm0 * sv).astype(dpb_ref.dtype)
            dsv = dm0 * u
            dsv_s[rows, :] = _mx(dsv)
            dbs_ref[...] += dsv
            return carry

        lax.fori_loop(0, nb, gating, 0)

        def spatial_bwd(n, carry):
            rows = pl.ds(pl.multiple_of(n * BLK, BLK), BLK)
            for j in range(N_PAIRS):
                cols = slice(j * LANES, (j + 1) * LANES)
                dsv = dsv_s[rows, cols]
                dvn_s[rows, cols] = jnp.where(lo, _dot(wst_ref[2 * j], dsv), _dot(wst_ref[2 * j + 1], dsv))
            return carry

        lax.fori_loop(0, nb, spatial_bwd, 0)
        zero = jnp.zeros((BLK, LANES), MXU_DTYPE)
        for j in range(N_PAIRS):
            cols = slice(j * LANES, (j + 1) * LANES)
            chunks = [dsv_s[n * BLK : (n + 1) * BLK, cols] for n in range(nb)]
            vn_all = jnp.concatenate([vnb_s[n * BLK : (n + 1) * BLK, cols] for n in range(nb)], axis=1)
            dws_ref[2 * j] += _dot_nt(jnp.concatenate([jnp.where(lo, c, zero) for c in chunks], axis=1), vn_all)
            dws_ref[2 * j + 1] += _dot_nt(jnp.concatenate([jnp.where(lo, zero, c) for c in chunks], axis=1), vn_all)
        for j in range(D_GM // WIDE):
            cols = slice(j * WIDE, (j + 1) * WIDE)
            dpb_ref[:, D_ATTN + D_GM + j * WIDE : D_ATTN + D_GM + (j + 1) * WIDE] = _half_rms_bwd(
                dvn_s[:, cols], vn_s[:, cols], rv_s[:, cols], ones
            ).astype(dpb_ref.dtype)
        g_ref[...] += _dot_tn(y_s[...], _mx(dxv))

        @pl.when(pl.program_id(0) == nt - 1)
        def _():
            gv = g_ref[...]
            dwo_ref[...] = (gv * gate_ref[...]).astype(dwo_ref.dtype)
            dgate_ref[...] = _group_rows(gv * wo_ref[...].astype(F32))

    return pl.pallas_call(
        body,
        name=name,
        grid=(seq // tile,),
        in_specs=[
            _row_spec(tile, d),
            _row_spec(tile, D_REST),
            _row_spec(tile, D_ATTN),
            _full_spec((1, d)),
            _full_spec((D_MIX, d)),
            _full_spec((N_GROUPS, BLK, BLK)),
            _full_spec((N_GROUPS, BLK, BLK)),
            _full_spec((BLK, D_GM)),
        ],
        out_specs=[
            _row_spec(tile, D_REST),
            _row_spec(tile, D_ATTN),
            _full_spec((D_MIX, d)),
            _full_spec((SUBLANES, d)),
            _full_spec((N_GROUPS, BLK, BLK)),
            _full_spec((BLK, D_GM)),
        ],
        out_shape=[
            jax.ShapeDtypeStruct((seq, D_REST), MXU_DTYPE),
            jax.ShapeDtypeStruct((seq, D_ATTN), F32),
            jax.ShapeDtypeStruct((D_MIX, d), jnp.bfloat16),
            jax.ShapeDtypeStruct((SUBLANES, d), F32),
            jax.ShapeDtypeStruct((N_GROUPS, BLK, BLK), F32),
            jax.ShapeDtypeStruct((BLK, D_GM), F32),
        ],
        scratch_shapes=[
            pltpu.VMEM((D_MIX, d), F32),
            pltpu.VMEM((tile, D_MIX), MXU_DTYPE),
            pltpu.VMEM((tile, D_MIX), F32),
            pltpu.VMEM((tile, D_GM), F32),
            pltpu.VMEM((tile, D_GM), F32),
            pltpu.VMEM((tile, D_GM), MXU_DTYPE),
            pltpu.VMEM((tile, D_GM), F32),
            pltpu.VMEM((tile, D_GM), MXU_DTYPE),
            pltpu.VMEM((tile, D_GM), F32),
        ],
        compiler_params=_params(("arbitrary",)),
    )(dxn, pb, o, gate, w_out, w_s, w_s_t, b_st)


def _attn_bwd(pa, o, do, q_gain2, k_gain2, sink, name, scatter=()):
    seq = pa.shape[0]
    tile = min(TOKEN_TILE, seq)
    nb = tile // BLK
    nt = seq // tile
    ext = tile + 2 * BLK
    n_ride = len(scatter)
    riding = n_ride > 0

    def body(sink_ref, qkv_ref, kvp_ref, kvn_ref, o_ref, do_ref, qg_ref, kg_ref, *rest):
        i = pl.program_id(0)
        blocks, rest = rest[:n_ride], rest[n_ride:]
        dq_ref, dkv_ref, hp_ref, hn_ref, dqg_ref, dkg_ref, dsk_ref = rest[:7]
        landing, rest = rest[7 : 7 + n_ride], rest[7 + n_ride :]
        (qs, dos, qhat_s, rq_s, ks, kr, vs, vr, khat_s, rk_s, dqn_s, dka, dva, bias_s, s_scr, dp_scr, p_scr, ds_scr) = rest[:18]
        if riding:
            start, finish = _scatter_stages(blocks, landing, *rest[18:])
            at_start, _, at_finish = _rider_steps(nt)
            pl.when(i == at_start)(start)

        @pl.when(i == 0)
        def _():
            dqg_ref[...] = jnp.zeros_like(dqg_ref)
            dkg_ref[...] = jnp.zeros_like(dkg_ref)
            dsk_ref[...] = jnp.zeros_like(dsk_ref)
            _fill_attn_bias(bias_s)

        ones = _half_ones()
        lo = _lane_lo(BLK)
        lo_t = _lane_lo(tile)
        lo_c = _lane_lo(ROW_CHUNK)
        qg = qg_ref[...] * Q_SCALE
        kg = kg_ref[...]
        _stage_keys(kvp_ref, qkv_ref, kvn_ref, kg, ones, tile, ks, kr, vs, vr, khat_s, rk_s)
        for j in range(N_PAIRS):
            cols = slice(j * LANES, (j + 1) * LANES)
            qhat, rq = _half_rms(qkv_ref[:, cols], ones)
            qhat_s[:, cols] = qhat
            rq_s[:, cols] = rq
            _stage_queries(qhat * qg, lo_t, j, nb, qs)
            _stage_queries(do_ref[:, cols], lo_t, j, nb, dos)
        dka[...] = jnp.zeros_like(dka)
        dva[...] = jnp.zeros_like(dva)
        head_lane = lax.broadcasted_iota(jnp.int32, (1, LANES), 1)

        def block(n, dsink):
            r0 = pl.multiple_of(n * BLK, BLK)
            krows = pl.ds(r0, 3 * BLK)
            kind = _block_kind(i * nb + n, seq)
            for v in range(2):
                s_scr[v] = _dot_nt(qs[n, v], (kr if v else ks)[krows, :])
                dp_scr[v] = _dot_nt(dos[n, v], (vr if v else vs)[krows, :])
            for h in range(N_HEADS):
                v, slot = HEAD_SLOT[h]
                j, a = divmod(h, 2)
                cols = slice(j * LANES, (j + 1) * LANES)
                sink_h = sink_ref[h]
                sink_part = jnp.zeros((ROW_CHUNK, 1), F32)
                for rc in range(0, BLK, ROW_CHUNK):
                    rows = slice(slot * BLK + rc, slot * BLK + rc + ROW_CHUNK)
                    trows = pl.ds(pl.multiple_of(r0 + rc, ROW_CHUNK), ROW_CHUNK)
                    s = s_scr[v, rows, :] + bias_s[kind, h, rc : rc + ROW_CHUNK, :]
                    m = jnp.maximum(jnp.max(s, axis=-1, keepdims=True), sink_h)
                    p = jnp.exp(s - m)
                    e_sink = jnp.exp(sink_h - m)
                    inv = 1.0 / (jnp.sum(p, axis=-1, keepdims=True) + e_sink)
                    pn = p * inv
                    prod = do_ref[trows, cols] * o_ref[trows, cols]
                    prod = jnp.where(lo_c, prod, 0.0) if a == 0 else jnp.where(lo_c, 0.0, prod)
                    dcol = jnp.sum(prod, axis=-1, keepdims=True)
                    ds_scr[v, rows, :] = _mx(pn * (dp_scr[v, rows, :] - dcol))
                    p_scr[v, rows, :] = _mx(pn)
                    sink_part = sink_part + (e_sink * inv) * dcol
                dsink = dsink - jnp.where(head_lane == h, jnp.sum(sink_part, axis=0, keepdims=True), 0.0)
            dqv = []
            for v in range(2):
                dqv.append(_dot(ds_scr[v], (kr if v else ks)[krows, :]))
                dka[v, krows, :] += _dot_tn(ds_scr[v], qs[n, v])
                dva[v, krows, :] += _dot_tn(p_scr[v], dos[n, v])
            for j in range(N_PAIRS):
                dqn_s[pl.ds(r0, BLK), j * LANES : (j + 1) * LANES] = _unstack_pair(dqv, j, lo)
            return dsink

        dsink = lax.fori_loop(0, nb, block, jnp.zeros((1, LANES), F32))
        dsk_ref[...] += jnp.broadcast_to(dsink, (SUBLANES, LANES))
        for j in range(N_PAIRS):
            cols = slice(j * LANES, (j + 1) * LANES)
            dqn = dqn_s[:, cols]
            qhat = qhat_s[:, cols]
            dqg_ref[:, cols] += _group_rows(dqn * qhat) * Q_SCALE
            dq_ref[:, cols] = _half_rms_bwd(dqn * qg, qhat, rq_s[:, cols], ones).astype(dq_ref.dtype)
        dkn = dka[0] + pltpu.roll(dka[1], HEAD_DIM, 1)
        khat = khat_s[...]
        dkg_ref[...] += _group_rows(dkn * khat)
        dk = _half_rms_bwd(dkn * kg, khat, rk_s[...], ones)
        dv = dva[0] + pltpu.roll(dva[1], HEAD_DIM, 1)
        hp_ref[:, 0:D_KV] = dk[0:BLK]
        hp_ref[:, D_KV : 2 * D_KV] = dv[0:BLK]
        dkv_ref[:, 0:D_KV] = dk[BLK : BLK + tile]
        dkv_ref[:, D_KV : 2 * D_KV] = dv[BLK : BLK + tile]
        hn_ref[:, 0:D_KV] = dk[BLK + tile : ext]
        hn_ref[:, D_KV : 2 * D_KV] = dv[BLK + tile : ext]
        if riding:
            pl.when(i == at_finish)(finish)

    prev, nxt = _halo_specs(tile, seq)
    vec = _full_spec((1, LANES))
    halo = pl.BlockSpec((None, BLK, 2 * D_KV), lambda i: (i, 0, 0))
    return pl.pallas_call(
        body,
        name=name,
        grid=(nt,),
        in_specs=[SMEM_SPEC, _row_spec(tile, D_QKV), prev, nxt, _row_spec(tile, D_ATTN), _row_spec(tile, D_ATTN), vec, vec]
        + [HBM_SPEC] * n_ride,
        out_specs=[
            _row_spec(tile, D_ATTN),
            _row_spec(tile, 2 * D_KV),
            halo,
            halo,
            _full_spec((SUBLANES, D_ATTN)),
            _full_spec((SUBLANES, LANES)),
            _full_spec((SUBLANES, LANES)),
        ]
        + [HBM_SPEC] * n_ride,
        out_shape=[
            jax.ShapeDtypeStruct((seq, D_ATTN), MXU_DTYPE),
            jax.ShapeDtypeStruct((seq, 2 * D_KV), F32),
            jax.ShapeDtypeStruct((nt, BLK, 2 * D_KV), F32),
            jax.ShapeDtypeStruct((nt, BLK, 2 * D_KV), F32),
            jax.ShapeDtypeStruct((SUBLANES, D_ATTN), F32),
            jax.ShapeDtypeStruct((SUBLANES, LANES), F32),
            jax.ShapeDtypeStruct((SUBLANES, LANES), F32),
        ]
        + _landing_shapes(scatter),
        scratch_shapes=[
            pltpu.VMEM((nb, 2, STACK, LANES), MXU_DTYPE),
            pltpu.VMEM((nb, 2, STACK, LANES), MXU_DTYPE),
            pltpu.VMEM((tile, D_ATTN), F32),
            pltpu.VMEM((tile, D_ATTN), F32),
            pltpu.VMEM((ext, LANES), MXU_DTYPE),
            pltpu.VMEM((ext, LANES), MXU_DTYPE),
            pltpu.VMEM((ext, LANES), MXU_DTYPE),
            pltpu.VMEM((ext, LANES), MXU_DTYPE),
            pltpu.VMEM((ext, LANES), F32),
            pltpu.VMEM((ext, LANES), F32),
            pltpu.VMEM((tile, D_ATTN), F32),
            pltpu.VMEM((2, ext, LANES), F32),
            pltpu.VMEM((2, ext, LANES), F32),
            pltpu.VMEM((3, N_HEADS, BLK, 3 * BLK), F32),
            pltpu.VMEM((2, STACK, 3 * BLK), F32),
            pltpu.VMEM((2, STACK, 3 * BLK), F32),
            pltpu.VMEM((2, STACK, 3 * BLK), MXU_DTYPE),
            pltpu.VMEM((2, STACK, 3 * BLK), MXU_DTYPE),
        ]
        + _rider_sems(n_ride),
        compiler_params=_params(("arbitrary",)),
    )(sink, pa, pa, pa, o, do, q_gain2, k_gain2, *scatter)


def _halo_in_specs(tile, nt):
    from_prev = pl.BlockSpec((None, BLK, 2 * D_KV), lambda i: (jnp.maximum(i - 1, 0), 0, 0))
    from_next = pl.BlockSpec((None, BLK, 2 * D_KV), lambda i: (jnp.minimum(i + 1, nt - 1), 0, 0))
    return from_prev, from_next


def _landing_shapes(scatter):
    return [jax.ShapeDtypeStruct((N_DEV,) + b.shape[2:], b.dtype) for b in scatter]


def _rider_sems(n_ride):
    if not n_ride:
        return []
    return [pltpu.SemaphoreType.DMA((7 * n_ride,)), pltpu.SemaphoreType.DMA((7 * n_ride,)), pltpu.SemaphoreType.DMA((n_ride,))]


def _proj_bwd_dx(x, dxn, dq, dkvb, dpb, w_in_t, gain, scale1, name):
    seq, d = x.shape
    tile = min(TOKEN_TILE, seq)

    def row(width):
        return _row_spec(tile, width)

    def body(x_ref, dxn_ref, dq_ref, dkvb_ref, dpb_ref, wt_ref, g_ref, s1_ref, dx_ref, c0_ref, c1_ref):
        @pl.when(pl.program_id(0) == 0)
        def _():
            c0_ref[...] = jnp.zeros_like(c0_ref)
            c1_ref[...] = jnp.zeros_like(c1_ref)

        dh = (
            _dot(dq_ref[...], wt_ref[0:D_ATTN, :])
            + _dot(dkvb_ref[...], wt_ref[D_ATTN:D_QKV, :])
            + _dot(dpb_ref[...], wt_ref[D_QKV:D_IN, :])
        )
        xv = x_ref[...]
        r = lax.rsqrt(jnp.mean(xv * xv, axis=-1, keepdims=True) + EPS)
        xn = xv * r
        c0_ref[...] += _group_rows(dh)
        c1_ref[...] += _group_rows(dh * xn)
        dxn_ = dh * (g_ref[...] * s1_ref[...])
        dx_ref[...] = dxn_ref[...] + r * (dxn_ - xn * jnp.mean(xn * dxn_, axis=-1, keepdims=True))

    vec = _full_spec((1, d))
    return pl.pallas_call(
        body,
        name=name,
        grid=(seq // tile,),
        in_specs=[row(d), row(d), row(D_ATTN), row(2 * D_KV), row(D_REST), _full_spec((D_IN, d)), vec, vec],
        out_specs=[row(d), _full_spec((SUBLANES, d)), _full_spec((SUBLANES, d))],
        out_shape=[
            jax.ShapeDtypeStruct((seq, d), F32),
            jax.ShapeDtypeStruct((SUBLANES, d), F32),
            jax.ShapeDtypeStruct((SUBLANES, d), F32),
        ],
        compiler_params=_params(("arbitrary",)),
    )(x, dxn, dq, dkvb, dpb, w_in_t, gain, scale1)


def _proj_bwd_dw(x, gain, scale1, shift, dq, dkv, halo_prev, halo_next, dpb, name, gather=()):
    seq, d = x.shape
    tile = min(TOKEN_TILE, seq)
    nt = seq // tile
    assert tile >= 2 * BLK
    n_ride = len(gather)

    def body(x_ref, g_ref, s1_ref, sh_ref, dq_ref, dkv_ref, hn_ref, hp_ref, dpb_ref, *rest):
        i = pl.program_id(0)
        sources, rest = rest[:n_ride], rest[n_ride:]
        dw_ref, dkvb_ref = rest[:2]
        gathered, (acc, *sems) = rest[2 : 2 + n_ride], rest[2 + n_ride :]
        after_compute = _gather_rider(sources, gathered, sems, i, nt) if n_ride else None

        @pl.when(i == 0)
        def _():
            acc[...] = jnp.zeros_like(acc)

        top = dkv_ref[0:BLK, :] + jnp.where(i > 0, hn_ref[...], 0.0)
        bot = dkv_ref[tile - BLK : tile, :] + jnp.where(i < nt - 1, hp_ref[...], 0.0)
        dkvb_ref[0:BLK, :] = top.astype(dkvb_ref.dtype)
        dkvb_ref[tile - BLK : tile, :] = bot.astype(dkvb_ref.dtype)
        if tile > 2 * BLK:
            dkvb_ref[BLK : tile - BLK, :] = dkv_ref[BLK : tile - BLK, :].astype(dkvb_ref.dtype)
        xv = x_ref[...]
        r = lax.rsqrt(jnp.mean(xv * xv, axis=-1, keepdims=True) + EPS)
        h = _mx((xv * r) * g_ref[...] * s1_ref[...] + sh_ref[...])
        acc[0:D_ATTN, :] += _dot_tn(dq_ref[...], h)
        acc[D_ATTN:D_QKV, :] += _dot_tn(dkvb_ref[...], h)
        acc[D_QKV:D_IN, :] += _dot_tn(dpb_ref[...], h)

        @pl.when(i == nt - 1)
        def _():
            dw_ref[...] = acc[...].astype(dw_ref.dtype)

        if n_ride:
            after_compute()

    from_prev, from_next = _halo_in_specs(tile, nt)
    vec = _full_spec((1, d))
    return pl.pallas_call(
        body,
        name=name,
        grid=(nt,),
        in_specs=[
            _row_spec(tile, d),
            vec,
            vec,
            vec,
            _row_spec(tile, D_ATTN),
            _row_spec(tile, 2 * D_KV),
            from_prev,
            from_next,
            _row_spec(tile, D_REST),
        ]
        + [HBM_SPEC] * n_ride,
        out_specs=[_full_spec((D_IN, d)), _row_spec(tile, 2 * D_KV)] + [HBM_SPEC] * n_ride,
        out_shape=[jax.ShapeDtypeStruct((D_IN, d), jnp.bfloat16), jax.ShapeDtypeStruct((seq, 2 * D_KV), MXU_DTYPE)]
        + _gathered_shapes(gather),
        scratch_shapes=[pltpu.VMEM((D_IN, d), F32)] + _rider_sems(n_ride),
        compiler_params=_params(("arbitrary",)),
    )(x, gain, scale1, shift, dq, dkv, halo_next, halo_prev, dpb, *gather)


def _adamw_math(w, g, m, v):
    m = ADAM_B1 * m + (1.0 - ADAM_B1) * g
    v = ADAM_B2 * v + (1.0 - ADAM_B2) * (g * g)
    m_hat = m / (1.0 - ADAM_B1**ADAM_STEP)
    v_hat = v / (1.0 - ADAM_B2**ADAM_STEP)
    delta = -ADAM_LR * (m_hat / (jnp.sqrt(v_hat) + ADAM_EPS) + ADAM_WD * w)
    return delta, m, v


def _small_update(gathered, gathered_ws, w, m, v, ws, m_ws, v_ws):
    def body(ga_ref, gws_ref, w_ref, m_ref, v_ref, ws_ref, mws_ref, vws_ref, *outs):
        for src, refs, out in ((ga_ref, (w_ref, m_ref, v_ref), outs[0:4]), (gws_ref, (ws_ref, mws_ref, vws_ref), outs[4:8])):
            g = src[0].astype(F32)
            for j in range(1, N_DEV):
                g = g + src[j].astype(F32)
            out[0][...] = g
            out[1][...], out[2][...], out[3][...] = _adamw_math(refs[0][...], g, refs[1][...], refs[2][...])

    shapes = [jax.ShapeDtypeStruct(w.shape, F32)] * 4 + [jax.ShapeDtypeStruct(ws.shape, F32)] * 4
    return pl.pallas_call(
        body,
        name="small_update",
        in_specs=[VMEM_SPEC] * 8,
        out_specs=[VMEM_SPEC] * 8,
        out_shape=shapes,
        compiler_params=_params(),
    )(gathered, gathered_ws, w, m, v, ws, m_ws, v_ws)


def _ada_update(c_all, d_ada_cols, w, m, v):
    n_layers = w.shape[0]

    def body(c_ref, da_ref, w_ref, m_ref, v_ref, g_ref, d_ref, mo_ref, vo_ref):
        cv = c_ref[...]
        cond = cv * _sigmoid(cv)
        for l in range(n_layers):
            g = lax.dot_general(
                cond, da_ref[l], (((0,), (0,)), ((), ())), preferred_element_type=F32, precision=lax.Precision.HIGHEST
            )
            g_ref[l] = g
            d_ref[l], mo_ref[l], vo_ref[l] = _adamw_math(w_ref[l], g, m_ref[l], v_ref[l])

    return pl.pallas_call(
        body,
        name="ada_update",
        in_specs=[VMEM_SPEC] * 5,
        out_specs=[VMEM_SPEC] * 4,
        out_shape=[jax.ShapeDtypeStruct(w.shape, F32)] * 4,
        compiler_params=_params(),
    )(c_all, d_ada_cols, w, m, v)


def _position():
    return lax.axis_index("x"), lax.axis_index("y"), lax.axis_index("c")


def _flip(pos, k):
    x, y, c = pos
    return (1 - x if k & 4 else x, 1 - y if k & 2 else y, 1 - c if k & 1 else c)


def _index(pos):
    x, y, c = pos
    return 4 * x + 2 * y + c


def _remote(src, dst, send_sem, recv_sem, to):
    return pltpu.make_async_remote_copy(
        src_ref=src, dst_ref=dst, send_sem=send_sem, recv_sem=recv_sem, device_id=to, device_id_type=MESH_ID
    )


def _all_gather_stages(slots, send_sems, recv_sems, sources=None, local_sems=None):
    me = _position()
    sibling = _flip(me, 1)
    others = (4, 2, 6)
    arrays = range(len(slots))

    def copy(t, k, block, to, own=False):
        slot = slots[t](_index(block))
        src = sources[t] if own and sources is not None else slot
        return _remote(src, slot, send_sems.at[7 * t + k], recv_sems.at[7 * t + k], to)

    def first(t):
        return [copy(t, 0, me, sibling, own=True)] + [copy(t, 1 + j, me, _flip(me, f), own=True) for j, f in enumerate(others)]

    def passed(t, j):
        return copy(t, 4 + j, _flip(me, others[j]), sibling)

    def local(t):
        return pltpu.make_async_copy(sources[t], slots[t](_index(me)), local_sems.at[t])

    def start():
        for t in arrays:
            if sources is not None:
                local(t).start()
            for cp in first(t):
                cp.start()

    def forward():
        for j, f in enumerate(others):
            for t in arrays:
                copy(t, 1 + j, _flip(me, f), me).wait_recv()
                passed(t, j).start()

    def finish():
        for t in arrays:
            copy(t, 0, sibling, me).wait_recv()
            for j, f in enumerate(others):
                copy(t, 4 + j, _flip(sibling, f), me).wait_recv()
        for t in arrays:
            for cp in first(t) + [passed(t, j) for j in range(len(others))]:
                cp.wait_send()
            if sources is not None:
                local(t).wait()

    return start, forward, finish


def _two_level_all_gather(slots, send_sems, recv_sems, between=None):
    start, forward, finish = _all_gather_stages(slots, send_sems, recv_sems)
    start()
    if between is not None:
        between()
    forward()
    finish()


def _row_block(ref, rows):
    return lambda j: ref.at[pl.ds(pl.multiple_of(j * rows, 16), rows), :]


def _scatter_stages(blocks, landing, send_sems, recv_sems, local_sems):
    me = _position()
    my = _index(me)
    arrays = range(len(blocks))

    def copy(t, k):
        px, py, pc = to = _flip(me, k)
        return _remote(blocks[t].at[2 * px + py, pc], landing[t].at[my], send_sems.at[7 * t + k - 1], recv_sems.at[7 * t + k - 1], to)

    def arrival(t, k):
        slot = landing[t].at[_index(_flip(me, k))]
        return _remote(slot, slot, send_sems.at[7 * t + k - 1], recv_sems.at[7 * t + k - 1], _flip(me, k))

    def local(t):
        x, y, c = me
        return pltpu.make_async_copy(blocks[t].at[2 * x + y, c], landing[t].at[my], local_sems.at[t])

    def start():
        for t in arrays:
            local(t).start()
            for k in range(1, N_DEV):
                copy(t, k).start()

    def finish():
        for t in arrays:
            for k in range(1, N_DEV):
                arrival(t, k).wait_recv()
        for t in arrays:
            for k in range(1, N_DEV):
                copy(t, k).wait_send()
            local(t).wait()

    return start, finish


def _ada_exchange(c_ref, w_ref, call_ref, parts_ref, sbuf, sem_s1, sem_r1, sem_s2, sem_r2):
    d = c_ref.shape[-1]
    n_layers = w_ref.shape[0]
    me = _position()
    my = _index(me)
    call_ref[my] = jnp.broadcast_to(c_ref[...], (SUBLANES, d))
    mine = call_ref.at[my]
    first = [_remote(mine, mine, sem_s1.at[k - 1], sem_r1.at[k - 1], _flip(me, k)) for k in range(1, N_DEV)]
    for cp in first:
        cp.start()
    for k in range(1, N_DEV):
        theirs = call_ref.at[_index(_flip(me, k))]
        _remote(theirs, theirs, sem_s1.at[k - 1], sem_r1.at[k - 1], _flip(me, k)).wait_recv()
    cv = call_ref[...].reshape(N_DEV * SUBLANES, d)
    cond = cv * _sigmoid(cv)
    for l in range(n_layers):
        rows = jnp.dot(cond, w_ref[l], preferred_element_type=F32, precision=lax.Precision.HIGHEST)
        for b in range(N_DEV):
            sbuf[b, l] = rows[b * SUBLANES : (b + 1) * SUBLANES]
    parts_ref[my] = sbuf[my]
    second = []
    for k in range(1, N_DEV):
        to = _flip(me, k)
        second.append(_remote(sbuf.at[_index(to)], parts_ref.at[my], sem_s2.at[k - 1], sem_r2.at[k - 1], to))
    for cp in second:
        cp.start()
    for k in range(1, N_DEV):
        theirs = parts_ref.at[_index(_flip(me, k))]
        _remote(theirs, theirs, sem_s2.at[k - 1], sem_r2.at[k - 1], _flip(me, k)).wait_recv()
    for cp in first + second:
        cp.wait_send()


def _gather_weights(w_in_t, w_out, c_row, w_ada):
    n_layers, rows_in, d = w_in_t.shape
    width = w_ada.shape[2]

    def body(wi_ref, wo_ref, c_ref, wa_ref, gi_ref, si_ref, so_ref, call_ref, parts_ref, sbuf, send_sems, recv_sems, *ada_sems):
        my = _index(_position())
        si_ref[...] = wi_ref[...].astype(si_ref.dtype)
        so_ref[...] = wo_ref[...].astype(so_ref.dtype)
        gi_ref[pl.ds(pl.multiple_of(my * rows_in, 16), rows_in), :] = si_ref[0]
        _two_level_all_gather(
            (_row_block(gi_ref, rows_in),),
            send_sems,
            recv_sems,
            between=functools.partial(_ada_exchange, c_ref, wa_ref, call_ref, parts_ref, sbuf, *ada_sems),
        )

    return pl.pallas_call(
        body,
        name="gather_weights",
        in_specs=[VMEM_SPEC] * 4,
        out_specs=[VMEM_SPEC] * 5,
        out_shape=[
            jax.ShapeDtypeStruct((N_DEV * rows_in, d), MXU_DTYPE),
            jax.ShapeDtypeStruct(w_in_t.shape, MXU_DTYPE),
            jax.ShapeDtypeStruct(w_out.shape, MXU_DTYPE),
            jax.ShapeDtypeStruct((N_DEV, SUBLANES, d), F32),
            jax.ShapeDtypeStruct((N_DEV, n_layers, SUBLANES, width), F32),
        ],
        scratch_shapes=[
            pltpu.VMEM((N_DEV, n_layers, SUBLANES, width), F32),
            pltpu.SemaphoreType.DMA((7,)),
            pltpu.SemaphoreType.DMA((7,)),
        ]
        + [pltpu.SemaphoreType.DMA((N_DEV - 1,))] * 4,
        compiler_params=_params(),
    )(w_in_t, w_out, c_row, w_ada)


def _gather_small(packed, adam=()):
    n_adam = len(adam)

    def body(p_ref, *rest):
        quads = [rest[4 * t : 4 * t + 4] for t in range(n_adam)]
        rest = rest[4 * n_adam :]
        g_ref = rest[0]
        results = [rest[1 + 3 * t : 4 + 3 * t] for t in range(n_adam)]
        send_sems, recv_sems = rest[1 + 3 * n_adam :]
        g_ref[_index(_position())] = p_ref[...]

        def updates():
            for (w_ref, gr_ref, m_ref, v_ref), (d_ref, mo_ref, vo_ref) in zip(quads, results):
                d_ref[...], mo_ref[...], vo_ref[...] = _adamw_math(w_ref[...], gr_ref[...], m_ref[...], v_ref[...])

        _two_level_all_gather((lambda j: g_ref.at[j],), send_sems, recv_sems, between=updates)

    return pl.pallas_call(
        body,
        name="gather_small",
        in_specs=[VMEM_SPEC] * (1 + 4 * n_adam),
        out_specs=[VMEM_SPEC] * (1 + 3 * n_adam),
        out_shape=[jax.ShapeDtypeStruct((N_DEV,) + packed.shape, F32)]
        + [jax.ShapeDtypeStruct(q[0].shape, F32) for q in adam for _ in range(3)],
        scratch_shapes=[pltpu.SemaphoreType.DMA((7,)), pltpu.SemaphoreType.DMA((7,))],
        compiler_params=_params(),
    )(packed, *[a for q in adam for a in q])


def _scatter_finish(landed, name, own=()):
    n = len(landed)

    def body(*refs):
        if own:
            x, y, c = _position()
            my = _index((x, y, c))
        for t, (src, out) in enumerate(zip(refs[:n], refs[n + len(own) :])):
            g = None
            for j in range(N_DEV):
                part = src[j].astype(F32)
                if own:
                    part = jnp.where(j == my, refs[n + t][2 * x + y, c].astype(F32), part)
                g = part if g is None else g + part
            out[...] = g

    return pl.pallas_call(
        body,
        name=name,
        in_specs=[VMEM_SPEC] * (n + len(own)),
        out_specs=[VMEM_SPEC] * n,
        out_shape=[jax.ShapeDtypeStruct(a.shape[1:], F32) for a in landed],
        compiler_params=_params(),
    )(*landed, *own)


SEM_SPEC = pl.BlockSpec(memory_space=pltpu.SEMAPHORE)
SPLIT_COPY = pltpu.SideEffectType.DATAFLOW_SIDE_EFFECTING


def _scatter_start(blocks, name):
    land_shape = (N_DEV,) + blocks.shape[2:]

    def body(blocks_ref, land_ref, send_sems, recv_sems, blocks_thru, land_thru, token):
        me = _position()
        my = _index(me)
        for k in range(1, N_DEV):
            px, py, pc = to = _flip(me, k)
            _remote(blocks_ref.at[2 * px + py, pc], land_ref.at[my], send_sems.at[k - 1], recv_sems.at[k - 1], to).start()
        token[...] = jnp.zeros_like(token)

    return pl.pallas_call(
        body,
        name=name,
        in_specs=(HBM_SPEC, HBM_SPEC),
        out_specs=(SEM_SPEC, SEM_SPEC, HBM_SPEC, HBM_SPEC, VMEM_SPEC),
        out_shape=(
            pltpu.SemaphoreType.DMA((N_DEV - 1,)),
            pltpu.SemaphoreType.DMA((N_DEV - 1,)),
            pltpu.HBM(blocks.shape, blocks.dtype),
            pltpu.HBM(land_shape, blocks.dtype),
            jax.ShapeDtypeStruct((SUBLANES, LANES), F32),
        ),
        input_output_aliases={0: 2, 1: 3},
        compiler_params=pltpu.CompilerParams(has_side_effects=SPLIT_COPY),
    )(pltpu.with_memory_space_constraint(blocks, pltpu.HBM), pltpu.with_memory_space_constraint(lax.empty(land_shape, blocks.dtype), pltpu.HBM))


def _scatter_wait(send_sems, recv_sems, blocks_thru, land_thru, after, name):
    def body(blocks_ref, land_ref, send_sems, recv_sems, after_ref, blocks_dead, got_ref):
        me = _position()
        my = _index(me)
        for k in range(1, N_DEV):
            px, py, pc = to = _flip(me, k)
            _remote(blocks_ref.at[2 * px + py, pc], land_ref.at[my], send_sems.at[k - 1], recv_sems.at[k - 1], to).wait_send()
        for k in range(1, N_DEV):
            slot = land_ref.at[_index(_flip(me, k))]
            _remote(slot, slot, send_sems.at[k - 1], recv_sems.at[k - 1], _flip(me, k)).wait_recv()

    return pl.pallas_call(
        body,
        name=name,
        in_specs=(HBM_SPEC, HBM_SPEC, SEM_SPEC, SEM_SPEC, pl.BlockSpec(memory_space=pl.ANY)),
        out_specs=(HBM_SPEC, HBM_SPEC),
        out_shape=(pltpu.HBM(blocks_thru.shape, blocks_thru.dtype), pltpu.HBM(land_thru.shape, land_thru.dtype)),
        input_output_aliases={0: 0, 1: 1},
        compiler_params=pltpu.CompilerParams(has_side_effects=SPLIT_COPY),
    )(blocks_thru, land_thru, send_sems, recv_sems, after)


def _pack_rows(parts):
    rows, offsets, at = [], [], 0
    for p in parts:
        flat = p.reshape(-1)
        n = -(-flat.shape[0] // (SUBLANES * LANES)) * SUBLANES
        rows.append(jnp.pad(flat, (0, n * LANES - flat.shape[0])).reshape(n, LANES))
        offsets.append(at)
        at += n
    return jnp.concatenate(rows, axis=0), offsets


def _unpack_rows(packed, offsets, shapes):
    out = []
    for off, shape in zip(offsets, shapes):
        size = 1
        for s in shape:
            size *= s
        n = -(-size // (SUBLANES * LANES)) * SUBLANES
        out.append(packed[off : off + n].reshape(-1)[:size].reshape(shape))
    return out


def kernel(x, c, w_ada, b_ada, norm_gain, w_in, q_gain, k_gain, sink, w_s, b_s, w_out, loss_target, m_w_ada, m_b_ada, m_norm_gain, m_w_in, m_q_gain, m_k_gain, m_sink, m_w_s, m_b_s, m_w_out, v_w_ada, v_b_ada, v_norm_gain, v_w_in, v_q_gain, v_k_gain, v_sink, v_w_s, v_b_s, v_w_out):
    seq, d = x.shape[1], x.shape[2]
    n_layers = w_in.shape[0]
    w_cols = w_in.shape[2]
    ada_cols = w_ada.shape[2]
    my = _index(_position())
    xs = x.reshape(seq, d)
    target = loss_target.reshape(seq, d)

    w_in_t0, shard_in, shard_out, c_all, ada_parts = _gather_weights(w_in.transpose(0, 2, 1), w_out, c, w_ada)
    w_in_ts, w_outs = [w_in_t0], []
    ada = ada_parts[:, :, 0, :].transpose(1, 0, 2).reshape(n_layers, 3 * d) + b_ada
    shift, scale1, gate = ada[:, None, 0:d], 1.0 + ada[:, None, d : 2 * d], ada[:, None, 2 * d : 3 * d]
    gain = norm_gain[:, None, :]

    w_s_m = w_s.astype(MXU_DTYPE)
    w_s_t = w_s_m.transpose(0, 1, 3, 2)
    b_st = jnp.repeat(b_s.transpose(0, 2, 1), HEAD_DIM, axis=2)
    q_gain2 = jnp.tile(q_gain, (1, 2))[:, None, :]
    k_gain2 = jnp.tile(k_gain, (1, 2))[:, None, :]

    xl, saved = xs, []
    for l in range(n_layers):
        last = l == n_layers - 1
        pa, pb = _ln_proj_fwd(xl, gain[l], scale1[l], shift[l], w_in_ts[l], f"ln_proj_fwd_{l}")
        wanted = ([shard_out[0]] if l == 0 else []) + ([] if last else [shard_out[l + 1], shard_in[l + 1]])
        if wanted:
            o, *arrived = _attn_fwd(pa, q_gain2[l], k_gain2[l], sink[l], f"attn_fwd_{l}", gather=tuple(wanted))
            if not last:
                w_in_ts.append(arrived.pop())
            w_outs += arrived
        else:
            o = _attn_fwd(pa, q_gain2[l], k_gain2[l], sink[l], f"attn_fwd_{l}")
        saved.append((xl, pa, pb, o))
        out = _mix_out_fwd(pb, o, xl, gate[l], w_outs[l], w_s_m[l], b_st[l], f"mix_out_fwd_{l}", target if last else None)
        if last:
            dx, sq_err = out
        else:
            xl = out

    g_w_in, g_w_out, small, d_ada_rows = [None] * n_layers, [None] * n_layers, [None] * n_layers, [None] * n_layers
    waiting = []
    d_ws_all = [None] * n_layers
    for l in reversed(range(n_layers)):
        x_l, pa, pb, o = saved[l]
        dpb, do, dw_out, d_gate8, d_ws, d_bs = _mix_out_bwd(
            dx, pb, o, gate[l], w_outs[l], w_s_m[l], w_s_t[l], b_st[l], f"mix_out_bwd_{l}"
        )
        waiting.append((g_w_out, l, dw_out.reshape(4, 2, D_MIX // N_DEV, d)))
        riding, waiting = ([], waiting) if 0 < l == n_layers - 1 else (waiting, [])
        attn = _attn_bwd(pa, o, do, q_gain2[l], k_gain2[l], sink[l], f"attn_bwd_{l}", scatter=tuple(b for _, _, b in riding))
        dq, dkv, halo_prev, halo_next, d_qg, d_kg, d_sk = attn[:7]
        if riding:
            for (dest, layer, _), total in zip(riding, _scatter_finish(attn[7:], f"scatter_finish_{l}")):
                dest[layer] = total.transpose(1, 0) if dest is g_w_in else total
        d_ws_all[l] = d_ws
        dw_args = (x_l, gain[l], scale1[l], shift[l], dq, dkv, halo_prev, halo_next, dpb, f"proj_bwd_dw_{l}")
        if l > 0:
            dw_in_t, dkvb = _proj_bwd_dw(*dw_args)
        else:
            d_ws_wire = jnp.stack(d_ws_all).reshape(-1, LANES).astype(jnp.bfloat16)
            dw_in_t, dkvb, gathered_ws = _proj_bwd_dw(*dw_args, gather=(d_ws_wire,))
        blocks_in = dw_in_t.reshape(4, 2, w_cols, d)
        if l > 0:
            waiting.append((g_w_in, l, blocks_in))
            dx, c0, c1 = _proj_bwd_dx(x_l, dx, dq, dkvb, dpb, w_in_ts[l], gain[l], scale1[l], f"proj_bwd_dx_{l}")
        else:
            *in_flight, token = _scatter_start(blocks_in, "scatter_start_in_0")
            dx, c0, c1 = _proj_bwd_dx(
                x_l, dx, dq, dkvb, dpb, w_in_ts[l], gain[l] + token[0, 0], scale1[l], f"proj_bwd_dx_{l}"
            )
            sent, landed = _scatter_wait(*in_flight, dx, "scatter_wait_in_0")
            g_w_in[l] = _scatter_finish((landed,), "scatter_finish_in_0", own=(sent,))[0].transpose(1, 0)
        c0s, c1s = c0.sum(axis=0), c1.sum(axis=0)
        d_ada_rows[l] = jnp.concatenate([c0s, norm_gain[l] * c1s, d_gate8.sum(axis=0)])
        small[l] = (
            scale1[l, 0] * c1s,
            d_qg.sum(axis=0).reshape(N_HEADS, HEAD_DIM).sum(axis=0),
            d_kg.sum(axis=0).reshape(2, HEAD_DIM).sum(axis=0),
            d_sk[0, 0:N_HEADS],
            d_bs.reshape(BLK, N_GROUPS, HEAD_DIM).sum(axis=2).transpose(1, 0),
        )

    names = ("norm_gain", "q_gain", "k_gain", "sink", "b_s")
    stacked = [jnp.stack([small[l][t] for l in range(n_layers)]) for t in range(len(names))]
    d_ada = jnp.stack(d_ada_rows)
    packed, offsets = _pack_rows(stacked + [d_ada, sq_err[0, 0:1]])
    g_w_in, g_w_out = jnp.stack(g_w_in), jnp.stack(g_w_out)
    gathered, *upd = _gather_small(packed, adam=((w_in, g_w_in, m_w_in, v_w_in), (w_out, g_w_out, m_w_out, v_w_out)))
    gathered_ws = gathered_ws.reshape(N_DEV, -1, LANES)
    upd_in, upd_out = upd[0:3], upd[3:6]
    no_weight = jnp.zeros((1,), F32)
    weights = (norm_gain, q_gain, k_gain, sink, b_s, b_ada, no_weight)
    moments_m = (m_norm_gain, m_q_gain, m_k_gain, m_sink, m_b_s, m_b_ada, no_weight)
    moments_v = (v_norm_gain, v_q_gain, v_k_gain, v_sink, v_b_s, v_b_ada, no_weight)
    w_pack, _ = _pack_rows(weights)
    m_pack, _ = _pack_rows(moments_m)
    v_pack, _ = _pack_rows(moments_v)
    shapes = [w.shape for w in weights]
    flat_ws = lambda a: a.reshape(-1, LANES)
    updated = _small_update(gathered, gathered_ws, w_pack, m_pack, v_pack, flat_ws(w_s), flat_ws(m_w_s), flat_ws(v_w_s))
    g_small, d_small, m_small, v_small = (_unpack_rows(p, offsets, shapes) for p in updated[0:4])
    ws_small = [p.reshape(w_s.shape) for p in updated[4:8]]
    loss = g_small[-1][0] * (0.5 / d)

    ada_off = offsets[-2]
    ada_n = -(-n_layers * 3 * d // (SUBLANES * LANES)) * SUBLANES
    d_ada_all = gathered[:, ada_off : ada_off + ada_n].reshape(N_DEV, -1)[:, : n_layers * 3 * d].reshape(N_DEV, n_layers, 3 * d)
    d_ada_cols = lax.dynamic_slice_in_dim(d_ada_all, my * ada_cols, ada_cols, axis=2)
    g_w_ada, *upd_ada = _ada_update(c_all[:, 0, :], d_ada_cols.transpose(1, 0, 2), w_ada, m_w_ada, v_w_ada)

    def ordered(ada_, in_, out_, small_, ws):
        ng, qg, kg, sk, bs, ba, _ = small_
        return (ada_, ba, ng, in_, qg, kg, sk, ws, bs, out_)

    grads = ordered(g_w_ada, g_w_in, g_w_out, g_small, ws_small[0])
    deltas = ordered(upd_ada[0], upd_in[0], upd_out[0], d_small, ws_small[1])
    new_m = ordered(upd_ada[1], upd_in[1], upd_out[1], m_small, ws_small[2])
    new_v = ordered(upd_ada[2], upd_in[2], upd_out[2], v_small, ws_small[3])
    return (loss, dx.reshape(x.shape), *grads, *deltas, *new_m, *new_v)
```

```python
import functools

import jax
import jax.numpy as jnp
from jax import lax
from jax.experimental import pallas as pl
from jax.experimental.pallas import tpu as pltpu

F32 = jnp.float32
MXU_DTYPE = jnp.bfloat16
MESH_ID = pl.DeviceIdType.MESH

N_DEV = 8
HEAD_DIM = 64
N_HEADS = 8
Q_PER_KV = 4
D_ATTN = 512
D_KV = 128
D_GM = 512
N_GROUPS = 8
D_MIX = D_ATTN + D_GM
BLK = 128
LANES = 128
SUBLANES = 8
N_PAIRS = D_ATTN // LANES
D_QKV = D_ATTN + 2 * D_KV
D_REST = D_ATTN + 3 * D_GM
D_IN = D_QKV + D_REST
EPS = 1e-6
NEG_INF = -1e30
ALIBI_SLOPES = tuple(2.0 ** (-8.0 * (h + 1) / N_HEADS) for h in range(N_HEADS))
Q_SCALE = 1.0 / 8.0

ADAM_LR = 0.001
ADAM_B1 = 0.9
ADAM_B2 = 0.999
ADAM_EPS = 1e-08
ADAM_WD = 0.01
ADAM_STEP = 10

TOKEN_TILE = 512
VMEM_LIMIT_BYTES = 56 * 1024 * 1024


def _params(semantics=None):
    return pltpu.CompilerParams(dimension_semantics=semantics, vmem_limit_bytes=VMEM_LIMIT_BYTES)


def _dot(a, b):
    return jnp.dot(a, b, preferred_element_type=F32)


def _dot_nt(a, b):
    return lax.dot_general(a, b, (((1,), (1,)), ((), ())), preferred_element_type=F32)


def _dot_tn(a, b):
    return lax.dot_general(a, b, (((0,), (0,)), ((), ())), preferred_element_type=F32)


def _mx(v):
    return v.astype(MXU_DTYPE)


def _lane_lo(rows):
    return lax.broadcasted_iota(jnp.int32, (rows, LANES), 1) < HEAD_DIM


def _half_ones(width=LANES):
    group_bits = HEAD_DIM.bit_length() - 1
    r = jnp.right_shift(lax.broadcasted_iota(jnp.int32, (width, width), 0), group_bits)
    c = jnp.right_shift(lax.broadcasted_iota(jnp.int32, (width, width), 1), group_bits)
    return jnp.where(r == c, 1.0, 0.0).astype(jnp.bfloat16)


WIDE = 2 * LANES


def _half_sum(v, ones):
    p1 = v.astype(jnp.bfloat16)
    p2 = (v - p1.astype(F32)).astype(jnp.bfloat16)
    return _dot(p1, ones) + _dot(p2, ones)


def _half_rms(v, ones):
    r = lax.rsqrt(_half_sum(v * v, ones) * (1.0 / HEAD_DIM) + EPS)
    return v * r, r


def _half_rms_bwd(dy, vhat, r, ones):
    return r * (dy - vhat * (_half_sum(vhat * dy, ones) * (1.0 / HEAD_DIM)))


def _group_rows(v):
    rows, n = v.shape
    return v.reshape(rows // SUBLANES, SUBLANES, n).sum(axis=0)


def _sigmoid(v):
    return 1.0 / (1.0 + jnp.exp(-v))


ROW_CHUNK = 32
VARIANT_HEADS = ((0, 2, 5, 7), (1, 3, 4, 6))
HEAD_SLOT = {h: (v, s) for v, heads in enumerate(VARIANT_HEADS) for s, h in enumerate(heads)}
STACK = Q_PER_KV * BLK


def _fill_attn_bias(bias_s):
    qi = lax.broadcasted_iota(jnp.int32, (BLK, 3 * BLK), 0)
    ci = lax.broadcasted_iota(jnp.int32, (BLK, 3 * BLK), 1)
    dist = jnp.abs(ci - BLK - qi)
    distf = dist.astype(F32)
    window = dist <= BLK
    for kind, seen in enumerate((window & (ci >= BLK), window, window & (ci < 2 * BLK))):
        for h in range(N_HEADS):
            bias_s[kind, h] = jnp.where(seen, -(ALIBI_SLOPES[h] * distf), NEG_INF)


def _block_kind(block, seq):
    assert seq >= 2 * BLK
    return jnp.where(block == 0, 0, jnp.where(block == seq // BLK - 1, 2, 1))


def _stage_queries(qn, lo_t, j, nb, qs):
    for a in range(2):
        v, slot = HEAD_SLOT[2 * j + a]
        qm = _mx(jnp.where(lo_t, qn, 0.0) if a == 0 else jnp.where(lo_t, 0.0, qn))
        for n in range(nb):
            qs[n, v, slot * BLK : (slot + 1) * BLK, :] = qm[n * BLK : (n + 1) * BLK]


def _unstack_pair(stacked, j, lo):
    (v0, s0), (v1, s1) = HEAD_SLOT[2 * j], HEAD_SLOT[2 * j + 1]
    return jnp.where(lo, stacked[v0][s0 * BLK : (s0 + 1) * BLK], stacked[v1][s1 * BLK : (s1 + 1) * BLK])


def _stage_keys(kvp_ref, qkv_ref, kvn_ref, kg, ones, tile, ks, kr, vs, vr, khat_s=None, rk_s=None):
    pieces = (
        (0, BLK, kvp_ref[:, 0:D_KV], kvp_ref[:, D_KV : 2 * D_KV]),
        (BLK, tile, qkv_ref[:, D_ATTN : D_ATTN + D_KV], qkv_ref[:, D_ATTN + D_KV : D_QKV]),
        (BLK + tile, BLK, kvn_ref[:, 0:D_KV], kvn_ref[:, D_KV : 2 * D_KV]),
    )
    for r0, n, k, v in pieces:
        khat, rk = _half_rms(k, ones)
        kn = khat * kg
        ks[r0 : r0 + n, :] = _mx(kn)
        kr[r0 : r0 + n, :] = _mx(pltpu.roll(kn, HEAD_DIM, 1))
        vs[r0 : r0 + n, :] = _mx(v)
        vr[r0 : r0 + n, :] = _mx(pltpu.roll(v, HEAD_DIM, 1))
        if khat_s is not None:
            khat_s[r0 : r0 + n, :] = khat
            rk_s[r0 : r0 + n, :] = rk


def _halo_specs(tile, seq):
    nb = tile // BLK
    last = seq // BLK - 1
    kv_col = D_ATTN // (2 * D_KV)
    prev = pl.BlockSpec((BLK, 2 * D_KV), lambda i: (jnp.maximum(i * nb - 1, 0), kv_col))
    nxt = pl.BlockSpec((BLK, 2 * D_KV), lambda i: (jnp.minimum((i + 1) * nb, last), kv_col))
    return prev, nxt


def _row_spec(tile, width):
    return pl.BlockSpec((tile, width), lambda i: (i, 0))


def _full_spec(shape):
    nd = len(shape)
    return pl.BlockSpec(shape, lambda i: (0,) * nd)


SMEM_SPEC = pl.BlockSpec(memory_space=pltpu.SMEM)
VMEM_SPEC = pl.BlockSpec(memory_space=pltpu.VMEM)
HBM_SPEC = pl.BlockSpec(memory_space=pltpu.HBM)


def _rider_steps(nt):
    return 0, (3 * nt) // 4, nt - 1


def _gather_rider(sources, gathered, sems, step, nt):
    start, forward, finish = _all_gather_stages(
        [_row_block(g, s.shape[0]) for g, s in zip(gathered, sources)], sems[0], sems[1], sources=sources, local_sems=sems[2]
    )
    at_start, at_forward, at_finish = _rider_steps(nt)
    pl.when(step == at_start)(start)

    def after_compute():
        pl.when(step == at_forward)(forward)
        pl.when(step == at_finish)(finish)

    return after_compute


def _gathered_shapes(gather):
    return [jax.ShapeDtypeStruct((N_DEV * g.shape[0], g.shape[1]), g.dtype) for g in gather]


def _ln_proj_fwd(x, gain, scale1, shift, w_in_t, name):
    seq, d = x.shape
    tile = min(TOKEN_TILE, seq)

    def body(x_ref, g_ref, s1_ref, sh_ref, wt_ref, pa_ref, pb_ref):
        xv = x_ref[...]
        r = lax.rsqrt(jnp.mean(xv * xv, axis=-1, keepdims=True) + EPS)
        h = _mx((xv * r) * g_ref[...] * s1_ref[...] + sh_ref[...])
        pa_ref[...] = _dot_nt(h, wt_ref[0:D_QKV, :])
        pb_ref[...] = _dot_nt(h, wt_ref[D_QKV:D_IN, :])

    vec = _full_spec((1, d))
    return pl.pallas_call(
        body,
        name=name,
        grid=(seq // tile,),
        in_specs=[_row_spec(tile, d), vec, vec, vec, _full_spec((D_IN, d))],
        out_specs=[_row_spec(tile, D_QKV), _row_spec(tile, D_REST)],
        out_shape=[jax.ShapeDtypeStruct((seq, D_QKV), F32), jax.ShapeDtypeStruct((seq, D_REST), F32)],
        compiler_params=_params(("parallel",)),
    )(x, gain, scale1, shift, w_in_t)


def _attn_fwd(pa, q_gain2, k_gain2, sink, name, gather=()):
    seq = pa.shape[0]
    tile = min(TOKEN_TILE, seq)
    nb = tile // BLK
    nt = seq // tile
    ext = tile + 2 * BLK
    n_ride = len(gather)
    riding = n_ride > 0

    def body(sink_ref, qkv_ref, kvp_ref, kvn_ref, qg_ref, kg_ref, *rest):
        i = pl.program_id(0)
        sources, (o_ref, p_ref, psink_ref), gathered = rest[:n_ride], rest[n_ride : n_ride + 3], rest[n_ride + 3 : 2 * n_ride + 3]
        qs, ks, kr, vs, vr, bias_s, s_scr, *sems = rest[2 * n_ride + 3 :]
        if riding:
            after_compute = _gather_rider(sources, gathered, sems, i, nt)

        @pl.when(i == 0)
        def _():
            _fill_attn_bias(bias_s)

        ones = _half_ones()
        lo = _lane_lo(BLK)
        lo_t = _lane_lo(tile)
        head_lane = lax.broadcasted_iota(jnp.int32, (ROW_CHUNK, LANES), 1)
        _stage_keys(kvp_ref, qkv_ref, kvn_ref, kg_ref[...], ones, tile, ks, kr, vs, vr)
        for j in range(N_PAIRS):
            qhat, _ = _half_rms(qkv_ref[:, j * LANES : (j + 1) * LANES], ones)
            _stage_queries(qhat * (qg_ref[...] * Q_SCALE), lo_t, j, nb, qs)

        def block(n, carry):
            r0 = pl.multiple_of(n * BLK, BLK)
            krows = pl.ds(r0, 3 * BLK)
            kind = _block_kind(i * nb + n, seq)
            for v in range(2):
                s_scr[v] = _dot_nt(qs[n, v], (kr if v else ks)[krows, :])
            for rc in range(0, BLK, ROW_CHUNK):
                p_sink = jnp.zeros((ROW_CHUNK, LANES), F32)
                for h in range(N_HEADS):
                    v, slot = HEAD_SLOT[h]
                    sink_h = sink_ref[h]
                    rows = slice(slot * BLK + rc, slot * BLK + rc + ROW_CHUNK)
                    s = s_scr[v, rows, :] + bias_s[kind, h, rc : rc + ROW_CHUNK, :]
                    m = jnp.maximum(jnp.max(s, axis=-1, keepdims=True), sink_h)
                    p = jnp.exp(s - m)
                    e_sink = jnp.exp(sink_h - m)
                    inv = 1.0 / (jnp.sum(p, axis=-1, keepdims=True) + e_sink)
                    p_ref[n, v, rows, :] = _mx(p * inv)
                    p_sink = jnp.where(head_lane == h, e_sink * inv, p_sink)
                psink_ref[pl.ds(pl.multiple_of(r0 + rc, ROW_CHUNK), ROW_CHUNK), :] = p_sink
            outs = [_dot(p_ref[n, v], (vr if v else vs)[krows, :]) for v in range(2)]
            for j in range(N_PAIRS):
                o_ref[pl.ds(r0, BLK), j * LANES : (j + 1) * LANES] = _unstack_pair(outs, j, lo)
            return carry

        lax.fori_loop(0, nb, block, 0)
        if riding:
            after_compute()

    prev, nxt = _halo_specs(tile, seq)
    vec = _full_spec((1, LANES))
    in_specs = [SMEM_SPEC, _row_spec(tile, D_QKV), prev, nxt, vec, vec]
    out_specs = [
        _row_spec(tile, D_ATTN),
        pl.BlockSpec((nb, 2, STACK, 3 * BLK), lambda i: (i, 0, 0, 0)),
        _row_spec(tile, LANES),
    ]
    out_shape = [
        jax.ShapeDtypeStruct((seq, D_ATTN), F32),
        jax.ShapeDtypeStruct((seq // BLK, 2, STACK, 3 * BLK), MXU_DTYPE),
        jax.ShapeDtypeStruct((seq, LANES), F32),
    ]
    scratch = [
        pltpu.VMEM((nb, 2, STACK, LANES), MXU_DTYPE),
        pltpu.VMEM((ext, LANES), MXU_DTYPE),
        pltpu.VMEM((ext, LANES), MXU_DTYPE),
        pltpu.VMEM((ext, LANES), MXU_DTYPE),
        pltpu.VMEM((ext, LANES), MXU_DTYPE),
        pltpu.VMEM((3, N_HEADS, BLK, 3 * BLK), F32),
        pltpu.VMEM((2, STACK, 3 * BLK), F32),
    ]
    return pl.pallas_call(
        body,
        name=name,
        grid=(nt,),
        in_specs=in_specs + [HBM_SPEC] * n_ride,
        out_specs=out_specs + [HBM_SPEC] * n_ride,
        out_shape=out_shape + _gathered_shapes(gather),
        scratch_shapes=scratch + _rider_sems(n_ride),
        compiler_params=_params(("arbitrary",)),
    )(sink, pa, pa, pa, q_gain2, k_gain2, *gather)


def _mix_out_fwd(pb, o, x, gate, w_out, w_s, b_st, name, target=None):
    seq, d = x.shape
    tile = min(TOKEN_TILE, seq)
    nb = tile // BLK
    with_loss = target is not None

    def body(pb_ref, o_ref, x_ref, gate_ref, wo_ref, ws_ref, bs_ref, *rest):
        if with_loss:
            t_ref, xo_ref, acc_ref, y_s, vn_s = rest

            @pl.when(pl.program_id(0) == 0)
            def _():
                acc_ref[...] = jnp.zeros_like(acc_ref)
        else:
            xo_ref, y_s, vn_s = rest
        ones = _half_ones(WIDE)
        lo = _lane_lo(BLK)
        ga = pb_ref[:, 0:D_ATTN]
        y_s[:, 0:D_ATTN] = _mx(o_ref[...] * (ga * _sigmoid(ga)))
        for j in range(D_GM // WIDE):
            vhat, _ = _half_rms(pb_ref[:, 2 * D_GM + j * WIDE : 2 * D_GM + (j + 1) * WIDE], ones)
            vn_s[:, j * WIDE : (j + 1) * WIDE] = _mx(vhat)

        def chunk(n, carry):
            rows = pl.ds(pl.multiple_of(n * BLK, BLK), BLK)
            for j in range(N_PAIRS):
                cols = slice(j * LANES, (j + 1) * LANES)
                vn = vn_s[rows, cols]
                sv = jnp.where(lo, _dot(ws_ref[2 * j], vn), _dot(ws_ref[2 * j + 1], vn)) + bs_ref[:, cols]
                u = pb_ref[rows, D_ATTN + j * LANES : D_ATTN + (j + 1) * LANES]
                gg = pb_ref[rows, D_ATTN + 2 * D_GM + j * LANES : D_ATTN + 2 * D_GM + (j + 1) * LANES]
                y_s[rows, D_ATTN + j * LANES : D_ATTN + (j + 1) * LANES] = _mx((u * sv) * (gg * _sigmoid(gg)))
            return carry

        lax.fori_loop(0, nb, chunk, 0)
        y = x_ref[...] + gate_ref[...] * _dot(y_s[...], wo_ref[...])
        if with_loss:
            e = y - t_ref[...]
            xo_ref[...] = e * (1.0 / d)
            acc_ref[...] += jnp.sum(jnp.sum(e * e, axis=-1, keepdims=True), axis=0, keepdims=True)
        else:
            xo_ref[...] = y

    row = _row_spec(tile, d)
    acc_shape = (SUBLANES, LANES)
    return pl.pallas_call(
        body,
        name=name,
        grid=(seq // tile,),
        in_specs=[
            _row_spec(tile, D_REST),
            _row_spec(tile, D_ATTN),
            row,
            _full_spec((1, d)),
            _full_spec((D_MIX, d)),
            _full_spec((N_GROUPS, BLK, BLK)),
            _full_spec((BLK, D_GM)),
        ]
        + ([row] if with_loss else []),
        out_specs=[row, _full_spec(acc_shape)] if with_loss else row,
        out_shape=[jax.ShapeDtypeStruct((seq, d), F32), jax.ShapeDtypeStruct(acc_shape, F32)]
        if with_loss
        else jax.ShapeDtypeStruct((seq, d), F32),
        scratch_shapes=[pltpu.VMEM((tile, D_MIX), MXU_DTYPE), pltpu.VMEM((tile, D_GM), MXU_DTYPE)],
        compiler_params=_params(("arbitrary",) if with_loss else ("parallel",)),
    )(pb, o, x, gate, w_out, w_s, b_st, *([target] if with_loss else []))


def _mix_out_bwd(dxn, pb, o, gate, w_out, w_s, w_s_t, b_st, name):
    seq, d = dxn.shape
    tile = min(TOKEN_TILE, seq)
    nb = tile // BLK
    nt = seq // tile

    def body(dxn_ref, pb_ref, o_ref, gate_ref, wo_ref, ws_ref, wst_ref, bs_ref,
             dpb_ref, do_ref, dwo_ref, dgate_ref, dws_ref, dbs_ref, g_ref, y_s, dy_s, vn_s, rv_s, vnb_s, sv_s, dsv_s, dvn_s):
        @pl.when(pl.program_id(0) == 0)
        def _():
            g_ref[...] = jnp.zeros_like(g_ref)
            dws_ref[...] = jnp.zeros_like(dws_ref)
            dbs_ref[...] = jnp.zeros_like(dbs_ref)

        ones = _half_ones(WIDE)
        lo = _lane_lo(BLK)
        c_u = slice(D_ATTN, D_ATTN + D_GM)
        c_vg = slice(D_ATTN + D_GM, D_ATTN + 2 * D_GM)
        c_gg = slice(D_ATTN + 2 * D_GM, D_REST)
        dxv = dxn_ref[...]
        dy_s[...] = _dot_nt(_mx(dxv * gate_ref[...]), wo_ref[...])
        ga = pb_ref[:, 0:D_ATTN]
        sig = _sigmoid(ga)
        sil = ga * sig
        ov = o_ref[...]
        y_s[:, 0:D_ATTN] = _mx(ov * sil)
        da = dy_s[:, 0:D_ATTN]
        do_ref[...] = da * sil
        dpb_ref[:, 0:D_ATTN] = (da * ov * (sig * (1.0 + ga * (1.0 - sig)))).astype(dpb_ref.dtype)
        for j in range(D_GM // WIDE):
            cols = slice(j * WIDE, (j + 1) * WIDE)
            vhat, rv = _half_rms(pb_ref[:, 2 * D_GM + j * WIDE : 2 * D_GM + (j + 1) * WIDE], ones)
            vn_s[:, cols] = vhat
            rv_s[:, cols] = rv
            vnb_s[:, cols] = _mx(vhat)

        def spatial_fwd(n, carry):
            rows = pl.ds(pl.multiple_of(n * BLK, BLK), BLK)
            for j in range(N_PAIRS):
                cols = slice(j * LANES, (j + 1) * LANES)
                vn = vnb_s[rows, cols]
                sv_s[rows, cols] = jnp.where(lo, _dot(ws_ref[2 * j], vn), _dot(ws_ref[2 * j + 1], vn)) + bs_ref[:, cols]
            return carry

        lax.fori_loop(0, nb, spatial_fwd, 0)

        def gating(n, carry):
            rows = pl.ds(pl.multiple_of(n * BLK, BLK), BLK)
            sv = sv_s[rows, :]
            u = pb_ref[rows, c_u]
            gg = pb_ref[rows, c_gg]
            sg = _sigmoid(gg)
            silg = gg * sg
            m0 = u * sv
            y_s[rows, D_ATTN:D_MIX] = _mx(m0 * silg)
            dm = dy_s[rows, D_ATTN:D_MIX]
            dm0 = dm * silg
            dpb_ref[rows, c_gg] = (dm * m0 * (sg * (1.0 + gg * (1.0 - sg)))).astype(dpb_ref.dtype)
            dpb_ref[rows, c_u] = (dm0 * sv).astype(dpb_ref.dtype)
            dsv = dm0 * u
            dsv_s[rows, :] = _mx(dsv)
            dbs_ref[...] += dsv
            return carry

        lax.fori_loop(0, nb, gating, 0)

        def spatial_bwd(n, carry):
            rows = pl.ds(pl.multiple_of(n * BLK, BLK), BLK)
            for j in range(N_PAIRS):
                cols = slice(j * LANES, (j + 1) * LANES)
                dsv = dsv_s[rows, cols]
                dvn_s[rows, cols] = jnp.where(lo, _dot(wst_ref[2 * j], dsv), _dot(wst_ref[2 * j + 1], dsv))
            return carry

        lax.fori_loop(0, nb, spatial_bwd, 0)
        zero = jnp.zeros((BLK, LANES), MXU_DTYPE)
        for j in range(N_PAIRS):
            cols = slice(j * LANES, (j + 1) * LANES)
            chunks = [dsv_s[n * BLK : (n + 1) * BLK, cols] for n in range(nb)]
            vn_all = jnp.concatenate([vnb_s[n * BLK : (n + 1) * BLK, cols] for n in range(nb)], axis=1)
            dws_ref[2 * j] += _dot_nt(jnp.concatenate([jnp.where(lo, c, zero) for c in chunks], axis=1), vn_all)
            dws_ref[2 * j + 1] += _dot_nt(jnp.concatenate([jnp.where(lo, zero, c) for c in chunks], axis=1), vn_all)
        for j in range(D_GM // WIDE):
            cols = slice(j * WIDE, (j + 1) * WIDE)
            dpb_ref[:, D_ATTN + D_GM + j * WIDE : D_ATTN + D_GM + (j + 1) * WIDE] = _half_rms_bwd(
                dvn_s[:, cols], vn_s[:, cols], rv_s[:, cols], ones
            ).astype(dpb_ref.dtype)
        g_ref[...] += _dot_tn(y_s[...], _mx(dxv))

        @pl.when(pl.program_id(0) == nt - 1)
        def _():
            gv = g_ref[...]
            dwo_ref[...] = (gv * gate_ref[...]).astype(dwo_ref.dtype)
            dgate_ref[...] = _group_rows(gv * wo_ref[...].astype(F32))

    return pl.pallas_call(
        body,
        name=name,
        grid=(seq // tile,),
        in_specs=[
            _row_spec(tile, d),
            _row_spec(tile, D_REST),
            _row_spec(tile, D_ATTN),
            _full_spec((1, d)),
            _full_spec((D_MIX, d)),
            _full_spec((N_GROUPS, BLK, BLK)),
            _full_spec((N_GROUPS, BLK, BLK)),
            _full_spec((BLK, D_GM)),
        ],
        out_specs=[
            _row_spec(tile, D_REST),
            _row_spec(tile, D_ATTN),
            _full_spec((D_MIX, d)),
            _full_spec((SUBLANES, d)),
            _full_spec((N_GROUPS, BLK, BLK)),
            _full_spec((BLK, D_GM)),
        ],
        out_shape=[
            jax.ShapeDtypeStruct((seq, D_REST), MXU_DTYPE),
            jax.ShapeDtypeStruct((seq, D_ATTN), F32),
            jax.ShapeDtypeStruct((D_MIX, d), jnp.bfloat16),
            jax.ShapeDtypeStruct((SUBLANES, d), F32),
            jax.ShapeDtypeStruct((N_GROUPS, BLK, BLK), F32),
            jax.ShapeDtypeStruct((BLK, D_GM), F32),
        ],
        scratch_shapes=[
            pltpu.VMEM((D_MIX, d), F32),
            pltpu.VMEM((tile, D_MIX), MXU_DTYPE),
            pltpu.VMEM((tile, D_MIX), F32),
            pltpu.VMEM((tile, D_GM), F32),
            pltpu.VMEM((tile, D_GM), F32),
            pltpu.VMEM((tile, D_GM), MXU_DTYPE),
            pltpu.VMEM((tile, D_GM), F32),
            pltpu.VMEM((tile, D_GM), MXU_DTYPE),
            pltpu.VMEM((tile, D_GM), F32),
        ],
        compiler_params=_params(("arbitrary",)),
    )(dxn, pb, o, gate, w_out, w_s, w_s_t, b_st)


def _attn_bwd(pa, o, do, probs, p_sink, q_gain2, k_gain2, name, scatter=()):
    seq = pa.shape[0]
    tile = min(TOKEN_TILE, seq)
    nb = tile // BLK
    nt = seq // tile
    ext = tile + 2 * BLK
    n_ride = len(scatter)
    riding = n_ride > 0

    def body(qkv_ref, kvp_ref, kvn_ref, o_ref, do_ref, p_ref, psink_ref, qg_ref, kg_ref, *rest):
        i = pl.program_id(0)
        blocks, rest = rest[:n_ride], rest[n_ride:]
        dq_ref, dkv_ref, hp_ref, hn_ref, dqg_ref, dkg_ref, dsk_ref = rest[:7]
        landing, rest = rest[7 : 7 + n_ride], rest[7 + n_ride :]
        (qs, dos, qhat_s, rq_s, ks, kr, vs, vr, khat_s, rk_s, dqn_s, dka, dva, dp_scr, ds_scr) = rest[:15]
        if riding:
            start, finish = _scatter_stages(blocks, landing, *rest[15:])
            at_start, _, at_finish = _rider_steps(nt)
            pl.when(i == at_start)(start)

        @pl.when(i == 0)
        def _():
            dqg_ref[...] = jnp.zeros_like(dqg_ref)
            dkg_ref[...] = jnp.zeros_like(dkg_ref)
            dsk_ref[...] = jnp.zeros_like(dsk_ref)

        ones = _half_ones()
        lo = _lane_lo(BLK)
        lo_t = _lane_lo(tile)
        lo_c = _lane_lo(ROW_CHUNK)
        qg = qg_ref[...] * Q_SCALE
        kg = kg_ref[...]
        _stage_keys(kvp_ref, qkv_ref, kvn_ref, kg, ones, tile, ks, kr, vs, vr, khat_s, rk_s)
        head_lane = lax.broadcasted_iota(jnp.int32, (tile, LANES), 1)
        d_rows = jnp.zeros((tile, LANES), F32)
        for j in range(N_PAIRS):
            cols = slice(j * LANES, (j + 1) * LANES)
            qhat, rq = _half_rms(qkv_ref[:, cols], ones)
            qhat_s[:, cols] = qhat
            rq_s[:, cols] = rq
            _stage_queries(qhat * qg, lo_t, j, nb, qs)
            dov = do_ref[:, cols]
            _stage_queries(dov, lo_t, j, nb, dos)
            d_pair = _half_sum(dov * o_ref[:, cols], ones)
            d_rows = jnp.where(head_lane == 2 * j, d_pair, d_rows)
            d_rows = jnp.where(head_lane == 2 * j + 1, pltpu.roll(d_pair, HEAD_DIM, 1), d_rows)
        dsk_ref[...] -= _group_rows(psink_ref[...] * d_rows)
        dka[...] = jnp.zeros_like(dka)
        dva[...] = jnp.zeros_like(dva)

        def block(n, carry):
            r0 = pl.multiple_of(n * BLK, BLK)
            krows = pl.ds(r0, 3 * BLK)
            for v in range(2):
                dp_scr[v] = _dot_nt(dos[n, v], (vr if v else vs)[krows, :])
            for h in range(N_HEADS):
                v, slot = HEAD_SLOT[h]
                j, a = divmod(h, 2)
                cols = slice(j * LANES, (j + 1) * LANES)
                for rc in range(0, BLK, ROW_CHUNK):
                    rows = slice(slot * BLK + rc, slot * BLK + rc + ROW_CHUNK)
                    trows = pl.ds(pl.multiple_of(r0 + rc, ROW_CHUNK), ROW_CHUNK)
                    prod = do_ref[trows, cols] * o_ref[trows, cols]
                    prod = jnp.where(lo_c, prod, 0.0) if a == 0 else jnp.where(lo_c, 0.0, prod)
                    dcol = jnp.sum(prod, axis=-1, keepdims=True)
                    ds_scr[v, rows, :] = _mx(p_ref[n, v, rows, :].astype(F32) * (dp_scr[v, rows, :] - dcol))
            dqv = []
            for v in range(2):
                dqv.append(_dot(ds_scr[v], (kr if v else ks)[krows, :]))
                dka[v, krows, :] += _dot_tn(ds_scr[v], qs[n, v])
                dva[v, krows, :] += _dot_tn(p_ref[n, v], dos[n, v])
            for j in range(N_PAIRS):
                dqn_s[pl.ds(r0, BLK), j * LANES : (j + 1) * LANES] = _unstack_pair(dqv, j, lo)
            return carry

        lax.fori_loop(0, nb, block, 0)
        for j in range(N_PAIRS):
            cols = slice(j * LANES, (j + 1) * LANES)
            dqn = dqn_s[:, cols]
            qhat = qhat_s[:, cols]
            dqg_ref[:, cols] += _group_rows(dqn * qhat) * Q_SCALE
            dq_ref[:, cols] = _half_rms_bwd(dqn * qg, qhat, rq_s[:, cols], ones).astype(dq_ref.dtype)
        dkn = dka[0] + pltpu.roll(dka[1], HEAD_DIM, 1)
        khat = khat_s[...]
        dkg_ref[...] += _group_rows(dkn * khat)
        dk = _half_rms_bwd(dkn * kg, khat, rk_s[...], ones)
        dv = dva[0] + pltpu.roll(dva[1], HEAD_DIM, 1)
        hp_ref[:, 0:D_KV] = dk[0:BLK]
        hp_ref[:, D_KV : 2 * D_KV] = dv[0:BLK]
        dkv_ref[:, 0:D_KV] = dk[BLK : BLK + tile]
        dkv_ref[:, D_KV : 2 * D_KV] = dv[BLK : BLK + tile]
        hn_ref[:, 0:D_KV] = dk[BLK + tile : ext]
        hn_ref[:, D_KV : 2 * D_KV] = dv[BLK + tile : ext]
        if riding:
            pl.when(i == at_finish)(finish)

    prev, nxt = _halo_specs(tile, seq)
    vec = _full_spec((1, LANES))
    halo = pl.BlockSpec((None, BLK, 2 * D_KV), lambda i: (i, 0, 0))
    return pl.pallas_call(
        body,
        name=name,
        grid=(nt,),
        in_specs=[
            _row_spec(tile, D_QKV),
            prev,
            nxt,
            _row_spec(tile, D_ATTN),
            _row_spec(tile, D_ATTN),
            pl.BlockSpec((nb, 2, STACK, 3 * BLK), lambda i: (i, 0, 0, 0)),
            _row_spec(tile, LANES),
            vec,
            vec,
        ]
        + [HBM_SPEC] * n_ride,
        out_specs=[
            _row_spec(tile, D_ATTN),
            _row_spec(tile, 2 * D_KV),
            halo,
            halo,
            _full_spec((SUBLANES, D_ATTN)),
            _full_spec((SUBLANES, LANES)),
            _full_spec((SUBLANES, LANES)),
        ]
        + [HBM_SPEC] * n_ride,
        out_shape=[
            jax.ShapeDtypeStruct((seq, D_ATTN), MXU_DTYPE),
            jax.ShapeDtypeStruct((seq, 2 * D_KV), F32),
            jax.ShapeDtypeStruct((nt, BLK, 2 * D_KV), F32),
            jax.ShapeDtypeStruct((nt, BLK, 2 * D_KV), F32),
            jax.ShapeDtypeStruct((SUBLANES, D_ATTN), F32),
            jax.ShapeDtypeStruct((SUBLANES, LANES), F32),
            jax.ShapeDtypeStruct((SUBLANES, LANES), F32),
        ]
        + _landing_shapes(scatter),
        scratch_shapes=[
            pltpu.VMEM((nb, 2, STACK, LANES), MXU_DTYPE),
            pltpu.VMEM((nb, 2, STACK, LANES), MXU_DTYPE),
            pltpu.VMEM((tile, D_ATTN), F32),
            pltpu.VMEM((tile, D_ATTN), F32),
            pltpu.VMEM((ext, LANES), MXU_DTYPE),
            pltpu.VMEM((ext, LANES), MXU_DTYPE),
            pltpu.VMEM((ext, LANES), MXU_DTYPE),
            pltpu.VMEM((ext, LANES), MXU_DTYPE),
            pltpu.VMEM((ext, LANES), F32),
            pltpu.VMEM((ext, LANES), F32),
            pltpu.VMEM((tile, D_ATTN), F32),
            pltpu.VMEM((2, ext, LANES), F32),
            pltpu.VMEM((2, ext, LANES), F32),
            pltpu.VMEM((2, STACK, 3 * BLK), F32),
            pltpu.VMEM((2, STACK, 3 * BLK), MXU_DTYPE),
        ]
        + _rider_sems(n_ride),
        compiler_params=_params(("arbitrary",)),
    )(pa, pa, pa, o, do, probs, p_sink, q_gain2, k_gain2, *scatter)


def _halo_in_specs(tile, nt):
    from_prev = pl.BlockSpec((None, BLK, 2 * D_KV), lambda i: (jnp.maximum(i - 1, 0), 0, 0))
    from_next = pl.BlockSpec((None, BLK, 2 * D_KV), lambda i: (jnp.minimum(i + 1, nt - 1), 0, 0))
    return from_prev, from_next


def _landing_shapes(scatter):
    return [jax.ShapeDtypeStruct((N_DEV,) + b.shape[2:], b.dtype) for b in scatter]


def _rider_sems(n_ride):
    if not n_ride:
        return []
    return [pltpu.SemaphoreType.DMA((7 * n_ride,)), pltpu.SemaphoreType.DMA((7 * n_ride,)), pltpu.SemaphoreType.DMA((n_ride,))]


def _proj_bwd_dx(x, dxn, dq, dkvb, dpb, w_in_t, gain, scale1, name):
    seq, d = x.shape
    tile = min(TOKEN_TILE, seq)

    def row(width):
        return _row_spec(tile, width)

    def body(x_ref, dxn_ref, dq_ref, dkvb_ref, dpb_ref, wt_ref, g_ref, s1_ref, dx_ref, c0_ref, c1_ref):
        @pl.when(pl.program_id(0) == 0)
        def _():
            c0_ref[...] = jnp.zeros_like(c0_ref)
            c1_ref[...] = jnp.zeros_like(c1_ref)

        dh = (
            _dot(dq_ref[...], wt_ref[0:D_ATTN, :])
            + _dot(dkvb_ref[...], wt_ref[D_ATTN:D_QKV, :])
            + _dot(dpb_ref[...], wt_ref[D_QKV:D_IN, :])
        )
        xv = x_ref[...]
        r = lax.rsqrt(jnp.mean(xv * xv, axis=-1, keepdims=True) + EPS)
        xn = xv * r
        c0_ref[...] += _group_rows(dh)
        c1_ref[...] += _group_rows(dh * xn)
        dxn_ = dh * (g_ref[...] * s1_ref[...])
        dx_ref[...] = dxn_ref[...] + r * (dxn_ - xn * jnp.mean(xn * dxn_, axis=-1, keepdims=True))

    vec = _full_spec((1, d))
    return pl.pallas_call(
        body,
        name=name,
        grid=(seq // tile,),
        in_specs=[row(d), row(d), row(D_ATTN), row(2 * D_KV), row(D_REST), _full_spec((D_IN, d)), vec, vec],
        out_specs=[row(d), _full_spec((SUBLANES, d)), _full_spec((SUBLANES, d))],
        out_shape=[
            jax.ShapeDtypeStruct((seq, d), F32),
            jax.ShapeDtypeStruct((SUBLANES, d), F32),
            jax.ShapeDtypeStruct((SUBLANES, d), F32),
        ],
        compiler_params=_params(("arbitrary",)),
    )(x, dxn, dq, dkvb, dpb, w_in_t, gain, scale1)


def _proj_bwd_dw(x, gain, scale1, shift, dq, dkv, halo_prev, halo_next, dpb, name, gather=()):
    seq, d = x.shape
    tile = min(TOKEN_TILE, seq)
    nt = seq // tile
    assert tile >= 2 * BLK
    n_ride = len(gather)

    def body(x_ref, g_ref, s1_ref, sh_ref, dq_ref, dkv_ref, hn_ref, hp_ref, dpb_ref, *rest):
        i = pl.program_id(0)
        sources, rest = rest[:n_ride], rest[n_ride:]
        dw_ref, dkvb_ref = rest[:2]
        gathered, (acc, *sems) = rest[2 : 2 + n_ride], rest[2 + n_ride :]
        after_compute = _gather_rider(sources, gathered, sems, i, nt) if n_ride else None

        @pl.when(i == 0)
        def _():
            acc[...] = jnp.zeros_like(acc)

        top = dkv_ref[0:BLK, :] + jnp.where(i > 0, hn_ref[...], 0.0)
        bot = dkv_ref[tile - BLK : tile, :] + jnp.where(i < nt - 1, hp_ref[...], 0.0)
        dkvb_ref[0:BLK, :] = top.astype(dkvb_ref.dtype)
        dkvb_ref[tile - BLK : tile, :] = bot.astype(dkvb_ref.dtype)
        if tile > 2 * BLK:
            dkvb_ref[BLK : tile - BLK, :] = dkv_ref[BLK : tile - BLK, :].astype(dkvb_ref.dtype)
        xv = x_ref[...]
        r = lax.rsqrt(jnp.mean(xv * xv, axis=-1, keepdims=True) + EPS)
        h = _mx((xv * r) * g_ref[...] * s1_ref[...] + sh_ref[...])
        acc[0:D_ATTN, :] += _dot_tn(dq_ref[...], h)
        acc[D_ATTN:D_QKV, :] += _dot_tn(dkvb_ref[...], h)
        acc[D_QKV:D_IN, :] += _dot_tn(dpb_ref[...], h)

        @pl.when(i == nt - 1)
        def _():
            dw_ref[...] = acc[...].astype(dw_ref.dtype)

        if n_ride:
            after_compute()

    from_prev, from_next = _halo_in_specs(tile, nt)
    vec = _full_spec((1, d))
    return pl.pallas_call(
        body,
        name=name,
        grid=(nt,),
        in_specs=[
            _row_spec(tile, d),
            vec,
            vec,
            vec,
            _row_spec(tile, D_ATTN),
            _row_spec(tile, 2 * D_KV),
            from_prev,
            from_next,
            _row_spec(tile, D_REST),
        ]
        + [HBM_SPEC] * n_ride,
        out_specs=[_full_spec((D_IN, d)), _row_spec(tile, 2 * D_KV)] + [HBM_SPEC] * n_ride,
        out_shape=[jax.ShapeDtypeStruct((D_IN, d), jnp.bfloat16), jax.ShapeDtypeStruct((seq, 2 * D_KV), MXU_DTYPE)]
        + _gathered_shapes(gather),
        scratch_shapes=[pltpu.VMEM((D_IN, d), F32)] + _rider_sems(n_ride),
        compiler_params=_params(("arbitrary",)),
    )(x, gain, scale1, shift, dq, dkv, halo_next, halo_prev, dpb, *gather)


def _adamw_math(w, g, m, v):
    m = ADAM_B1 * m + (1.0 - ADAM_B1) * g
    v = ADAM_B2 * v + (1.0 - ADAM_B2) * (g * g)
    m_hat = m / (1.0 - ADAM_B1**ADAM_STEP)
    v_hat = v / (1.0 - ADAM_B2**ADAM_STEP)
    delta = -ADAM_LR * (m_hat / (jnp.sqrt(v_hat) + ADAM_EPS) + ADAM_WD * w)
    return delta, m, v


def _small_update(gathered, gathered_ws, w, m, v, ws, m_ws, v_ws):
    def body(ga_ref, gws_ref, w_ref, m_ref, v_ref, ws_ref, mws_ref, vws_ref, *outs):
        for src, refs, out in ((ga_ref, (w_ref, m_ref, v_ref), outs[0:4]), (gws_ref, (ws_ref, mws_ref, vws_ref), outs[4:8])):
            g = src[0].astype(F32)
            for j in range(1, N_DEV):
                g = g + src[j].astype(F32)
            out[0][...] = g
            out[1][...], out[2][...], out[3][...] = _adamw_math(refs[0][...], g, refs[1][...], refs[2][...])

    shapes = [jax.ShapeDtypeStruct(w.shape, F32)] * 4 + [jax.ShapeDtypeStruct(ws.shape, F32)] * 4
    return pl.pallas_call(
        body,
        name="small_update",
        in_specs=[VMEM_SPEC] * 8,
        out_specs=[VMEM_SPEC] * 8,
        out_shape=shapes,
        compiler_params=_params(),
    )(gathered, gathered_ws, w, m, v, ws, m_ws, v_ws)


def _ada_update(c_all, d_ada_cols, w, m, v):
    n_layers = w.shape[0]

    def body(c_ref, da_ref, w_ref, m_ref, v_ref, g_ref, d_ref, mo_ref, vo_ref):
        cv = c_ref[...]
        cond = cv * _sigmoid(cv)
        for l in range(n_layers):
            g = lax.dot_general(
                cond, da_ref[l], (((0,), (0,)), ((), ())), preferred_element_type=F32, precision=lax.Precision.HIGHEST
            )
            g_ref[l] = g
            d_ref[l], mo_ref[l], vo_ref[l] = _adamw_math(w_ref[l], g, m_ref[l], v_ref[l])

    return pl.pallas_call(
        body,
        name="ada_update",
        in_specs=[VMEM_SPEC] * 5,
        out_specs=[VMEM_SPEC] * 4,
        out_shape=[jax.ShapeDtypeStruct(w.shape, F32)] * 4,
        compiler_params=_params(),
    )(c_all, d_ada_cols, w, m, v)


def _position():
    return lax.axis_index("x"), lax.axis_index("y"), lax.axis_index("c")


def _flip(pos, k):
    x, y, c = pos
    return (1 - x if k & 4 else x, 1 - y if k & 2 else y, 1 - c if k & 1 else c)


def _index(pos):
    x, y, c = pos
    return 4 * x + 2 * y + c


def _remote(src, dst, send_sem, recv_sem, to):
    return pltpu.make_async_remote_copy(
        src_ref=src, dst_ref=dst, send_sem=send_sem, recv_sem=recv_sem, device_id=to, device_id_type=MESH_ID
    )


def _all_gather_stages(slots, send_sems, recv_sems, sources=None, local_sems=None):
    me = _position()
    sibling = _flip(me, 1)
    others = (4, 2, 6)
    arrays = range(len(slots))

    def copy(t, k, block, to, own=False):
        slot = slots[t](_index(block))
        src = sources[t] if own and sources is not None else slot
        return _remote(src, slot, send_sems.at[7 * t + k], recv_sems.at[7 * t + k], to)

    def first(t):
        return [copy(t, 0, me, sibling, own=True)] + [copy(t, 1 + j, me, _flip(me, f), own=True) for j, f in enumerate(others)]

    def passed(t, j):
        return copy(t, 4 + j, _flip(me, others[j]), sibling)

    def local(t):
        return pltpu.make_async_copy(sources[t], slots[t](_index(me)), local_sems.at[t])

    def start():
        for t in arrays:
            if sources is not None:
                local(t).start()
            for cp in first(t):
                cp.start()

    def forward():
        for j, f in enumerate(others):
            for t in arrays:
                copy(t, 1 + j, _flip(me, f), me).wait_recv()
                passed(t, j).start()

    def finish():
        for t in arrays:
            copy(t, 0, sibling, me).wait_recv()
            for j, f in enumerate(others):
                copy(t, 4 + j, _flip(sibling, f), me).wait_recv()
        for t in arrays:
            for cp in first(t) + [passed(t, j) for j in range(len(others))]:
                cp.wait_send()
            if sources is not None:
                local(t).wait()

    return start, forward, finish


def _two_level_all_gather(slots, send_sems, recv_sems, between=None):
    start, forward, finish = _all_gather_stages(slots, send_sems, recv_sems)
    start()
    if between is not None:
        between()
    forward()
    finish()


def _row_block(ref, rows):
    return lambda j: ref.at[pl.ds(pl.multiple_of(j * rows, 16), rows), :]


def _scatter_stages(blocks, landing, send_sems, recv_sems, local_sems):
    me = _position()
    my = _index(me)
    arrays = range(len(blocks))

    def copy(t, k):
        px, py, pc = to = _flip(me, k)
        return _remote(blocks[t].at[2 * px + py, pc], landing[t].at[my], send_sems.at[7 * t + k - 1], recv_sems.at[7 * t + k - 1], to)

    def arrival(t, k):
        slot = landing[t].at[_index(_flip(me, k))]
        return _remote(slot, slot, send_sems.at[7 * t + k - 1], recv_sems.at[7 * t + k - 1], _flip(me, k))

    def local(t):
        x, y, c = me
        return pltpu.make_async_copy(blocks[t].at[2 * x + y, c], landing[t].at[my], local_sems.at[t])

    def start():
        for t in arrays:
            local(t).start()
            for k in range(1, N_DEV):
                copy(t, k).start()

    def finish():
        for t in arrays:
            for k in range(1, N_DEV):
                arrival(t, k).wait_recv()
        for t in arrays:
            for k in range(1, N_DEV):
                copy(t, k).wait_send()
            local(t).wait()

    return start, finish


def _ada_exchange(c_ref, w_ref, call_ref, parts_ref, sbuf, sem_s1, sem_r1, sem_s2, sem_r2):
    d = c_ref.shape[-1]
    n_layers = w_ref.shape[0]
    me = _position()
    my = _index(me)
    call_ref[my] = jnp.broadcast_to(c_ref[...], (SUBLANES, d))
    mine = call_ref.at[my]
    first = [_remote(mine, mine, sem_s1.at[k - 1], sem_r1.at[k - 1], _flip(me, k)) for k in range(1, N_DEV)]
    for cp in first:
        cp.start()
    for k in range(1, N_DEV):
        theirs = call_ref.at[_index(_flip(me, k))]
        _remote(theirs, theirs, sem_s1.at[k - 1], sem_r1.at[k - 1], _flip(me, k)).wait_recv()
    cv = call_ref[...].reshape(N_DEV * SUBLANES, d)
    cond = cv * _sigmoid(cv)
    for l in range(n_layers):
        rows = jnp.dot(cond, w_ref[l], preferred_element_type=F32, precision=lax.Precision.HIGHEST)
        for b in range(N_DEV):
            sbuf[b, l] = rows[b * SUBLANES : (b + 1) * SUBLANES]
    parts_ref[my] = sbuf[my]
    second = []
    for k in range(1, N_DEV):
        to = _flip(me, k)
        second.append(_remote(sbuf.at[_index(to)], parts_ref.at[my], sem_s2.at[k - 1], sem_r2.at[k - 1], to))
    for cp in second:
        cp.start()
    for k in range(1, N_DEV):
        theirs = parts_ref.at[_index(_flip(me, k))]
        _remote(theirs, theirs, sem_s2.at[k - 1], sem_r2.at[k - 1], _flip(me, k)).wait_recv()
    for cp in first + second:
        cp.wait_send()


def _gather_weights(w_in_t, w_out, c_row, w_ada):
    n_layers, rows_in, d = w_in_t.shape
    width = w_ada.shape[2]

    def body(wi_ref, wo_ref, c_ref, wa_ref, gi_ref, si_ref, so_ref, call_ref, parts_ref, sbuf, send_sems, recv_sems, *ada_sems):
        my = _index(_position())
        si_ref[...] = wi_ref[...].astype(si_ref.dtype)
        so_ref[...] = wo_ref[...].astype(so_ref.dtype)
        gi_ref[pl.ds(pl.multiple_of(my * rows_in, 16), rows_in), :] = si_ref[0]
        _two_level_all_gather(
            (_row_block(gi_ref, rows_in),),
            send_sems,
            recv_sems,
            between=functools.partial(_ada_exchange, c_ref, wa_ref, call_ref, parts_ref, sbuf, *ada_sems),
        )

    return pl.pallas_call(
        body,
        name="gather_weights",
        in_specs=[VMEM_SPEC] * 4,
        out_specs=[VMEM_SPEC] * 5,
        out_shape=[
            jax.ShapeDtypeStruct((N_DEV * rows_in, d), MXU_DTYPE),
            jax.ShapeDtypeStruct(w_in_t.shape, MXU_DTYPE),
            jax.ShapeDtypeStruct(w_out.shape, MXU_DTYPE),
            jax.ShapeDtypeStruct((N_DEV, SUBLANES, d), F32),
            jax.ShapeDtypeStruct((N_DEV, n_layers, SUBLANES, width), F32),
        ],
        scratch_shapes=[
            pltpu.VMEM((N_DEV, n_layers, SUBLANES, width), F32),
            pltpu.SemaphoreType.DMA((7,)),
            pltpu.SemaphoreType.DMA((7,)),
        ]
        + [pltpu.SemaphoreType.DMA((N_DEV - 1,))] * 4,
        compiler_params=_params(),
    )(w_in_t, w_out, c_row, w_ada)


def _gather_small(packed, adam=()):
    n_adam = len(adam)

    def body(p_ref, *rest):
        quads = [rest[4 * t : 4 * t + 4] for t in range(n_adam)]
        rest = rest[4 * n_adam :]
        g_ref = rest[0]
        results = [rest[1 + 3 * t : 4 + 3 * t] for t in range(n_adam)]
        send_sems, recv_sems = rest[1 + 3 * n_adam :]
        g_ref[_index(_position())] = p_ref[...]

        def updates():
            for (w_ref, gr_ref, m_ref, v_ref), (d_ref, mo_ref, vo_ref) in zip(quads, results):
                d_ref[...], mo_ref[...], vo_ref[...] = _adamw_math(w_ref[...], gr_ref[...], m_ref[...], v_ref[...])

        _two_level_all_gather((lambda j: g_ref.at[j],), send_sems, recv_sems, between=updates)

    return pl.pallas_call(
        body,
        name="gather_small",
        in_specs=[VMEM_SPEC] * (1 + 4 * n_adam),
        out_specs=[VMEM_SPEC] * (1 + 3 * n_adam),
        out_shape=[jax.ShapeDtypeStruct((N_DEV,) + packed.shape, F32)]
        + [jax.ShapeDtypeStruct(q[0].shape, F32) for q in adam for _ in range(3)],
        scratch_shapes=[pltpu.SemaphoreType.DMA((7,)), pltpu.SemaphoreType.DMA((7,))],
        compiler_params=_params(),
    )(packed, *[a for q in adam for a in q])


def _scatter_finish(landed, name, own=()):
    n = len(landed)

    def body(*refs):
        if own:
            x, y, c = _position()
            my = _index((x, y, c))
        for t, (src, out) in enumerate(zip(refs[:n], refs[n + len(own) :])):
            g = None
            for j in range(N_DEV):
                part = src[j].astype(F32)
                if own:
                    part = jnp.where(j == my, refs[n + t][2 * x + y, c].astype(F32), part)
                g = part if g is None else g + part
            out[...] = g

    return pl.pallas_call(
        body,
        name=name,
        in_specs=[VMEM_SPEC] * (n + len(own)),
        out_specs=[VMEM_SPEC] * n,
        out_shape=[jax.ShapeDtypeStruct(a.shape[1:], F32) for a in landed],
        compiler_params=_params(),
    )(*landed, *own)


SEM_SPEC = pl.BlockSpec(memory_space=pltpu.SEMAPHORE)
SPLIT_COPY = pltpu.SideEffectType.DATAFLOW_SIDE_EFFECTING


def _scatter_start(blocks, name):
    land_shape = (N_DEV,) + blocks.shape[2:]

    def body(blocks_ref, land_ref, send_sems, recv_sems, blocks_thru, land_thru, token):
        me = _position()
        my = _index(me)
        for k in range(1, N_DEV):
            px, py, pc = to = _flip(me, k)
            _remote(blocks_ref.at[2 * px + py, pc], land_ref.at[my], send_sems.at[k - 1], recv_sems.at[k - 1], to).start()
        token[...] = jnp.zeros_like(token)

    return pl.pallas_call(
        body,
        name=name,
        in_specs=(HBM_SPEC, HBM_SPEC),
        out_specs=(SEM_SPEC, SEM_SPEC, HBM_SPEC, HBM_SPEC, VMEM_SPEC),
        out_shape=(
            pltpu.SemaphoreType.DMA((N_DEV - 1,)),
            pltpu.SemaphoreType.DMA((N_DEV - 1,)),
            pltpu.HBM(blocks.shape, blocks.dtype),
            pltpu.HBM(land_shape, blocks.dtype),
            jax.ShapeDtypeStruct((SUBLANES, LANES), F32),
        ),
        input_output_aliases={0: 2, 1: 3},
        compiler_params=pltpu.CompilerParams(has_side_effects=SPLIT_COPY),
    )(pltpu.with_memory_space_constraint(blocks, pltpu.HBM), pltpu.with_memory_space_constraint(lax.empty(land_shape, blocks.dtype), pltpu.HBM))


def _scatter_wait(send_sems, recv_sems, blocks_thru, land_thru, after, name):
    def body(blocks_ref, land_ref, send_sems, recv_sems, after_ref, blocks_dead, got_ref):
        me = _position()
        my = _index(me)
        for k in range(1, N_DEV):
            px, py, pc = to = _flip(me, k)
            _remote(blocks_ref.at[2 * px + py, pc], land_ref.at[my], send_sems.at[k - 1], recv_sems.at[k - 1], to).wait_send()
        for k in range(1, N_DEV):
            slot = land_ref.at[_index(_flip(me, k))]
            _remote(slot, slot, send_sems.at[k - 1], recv_sems.at[k - 1], _flip(me, k)).wait_recv()

    return pl.pallas_call(
        body,
        name=name,
        in_specs=(HBM_SPEC, HBM_SPEC, SEM_SPEC, SEM_SPEC, pl.BlockSpec(memory_space=pl.ANY)),
        out_specs=(HBM_SPEC, HBM_SPEC),
        out_shape=(pltpu.HBM(blocks_thru.shape, blocks_thru.dtype), pltpu.HBM(land_thru.shape, land_thru.dtype)),
        input_output_aliases={0: 0, 1: 1},
        compiler_params=pltpu.CompilerParams(has_side_effects=SPLIT_COPY),
    )(blocks_thru, land_thru, send_sems, recv_sems, after)


def _pack_rows(parts):
    rows, offsets, at = [], [], 0
    for p in parts:
        flat = p.reshape(-1)
        n = -(-flat.shape[0] // (SUBLANES * LANES)) * SUBLANES
        rows.append(jnp.pad(flat, (0, n * LANES - flat.shape[0])).reshape(n, LANES))
        offsets.append(at)
        at += n
    return jnp.concatenate(rows, axis=0), offsets


def _unpack_rows(packed, offsets, shapes):
    out = []
    for off, shape in zip(offsets, shapes):
        size = 1
        for s in shape:
            size *= s
        n = -(-size // (SUBLANES * LANES)) * SUBLANES
        out.append(packed[off : off + n].reshape(-1)[:size].reshape(shape))
    return out


def kernel(x, c, w_ada, b_ada, norm_gain, w_in, q_gain, k_gain, sink, w_s, b_s, w_out, loss_target, m_w_ada, m_b_ada, m_norm_gain, m_w_in, m_q_gain, m_k_gain, m_sink, m_w_s, m_b_s, m_w_out, v_w_ada, v_b_ada, v_norm_gain, v_w_in, v_q_gain, v_k_gain, v_sink, v_w_s, v_b_s, v_w_out):
    seq, d = x.shape[1], x.shape[2]
    n_layers = w_in.shape[0]
    w_cols = w_in.shape[2]
    ada_cols = w_ada.shape[2]
    my = _index(_position())
    xs = x.reshape(seq, d)
    target = loss_target.reshape(seq, d)

    w_in_t0, shard_in, shard_out, c_all, ada_parts = _gather_weights(w_in.transpose(0, 2, 1), w_out, c, w_ada)
    w_in_ts, w_outs = [w_in_t0], []
    ada = ada_parts[:, :, 0, :].transpose(1, 0, 2).reshape(n_layers, 3 * d) + b_ada
    shift, scale1, gate = ada[:, None, 0:d], 1.0 + ada[:, None, d : 2 * d], ada[:, None, 2 * d : 3 * d]
    gain = norm_gain[:, None, :]

    w_s_m = w_s.astype(MXU_DTYPE)
    w_s_t = w_s_m.transpose(0, 1, 3, 2)
    b_st = jnp.repeat(b_s.transpose(0, 2, 1), HEAD_DIM, axis=2)
    q_gain2 = jnp.tile(q_gain, (1, 2))[:, None, :]
    k_gain2 = jnp.tile(k_gain, (1, 2))[:, None, :]

    xl, saved = xs, []
    for l in range(n_layers):
        last = l == n_layers - 1
        pa, pb = _ln_proj_fwd(xl, gain[l], scale1[l], shift[l], w_in_ts[l], f"ln_proj_fwd_{l}")
        wanted = ([shard_out[0]] if l == 0 else []) + ([] if last else [shard_out[l + 1], shard_in[l + 1]])
        o, probs, p_sink, *arrived = _attn_fwd(pa, q_gain2[l], k_gain2[l], sink[l], f"attn_fwd_{l}", gather=tuple(wanted))
        if not last:
            w_in_ts.append(arrived.pop())
        w_outs += arrived
        saved.append((xl, pa, pb, o, probs, p_sink))
        out = _mix_out_fwd(pb, o, xl, gate[l], w_outs[l], w_s_m[l], b_st[l], f"mix_out_fwd_{l}", target if last else None)
        if last:
            dx, sq_err = out
        else:
            xl = out

    g_w_in, g_w_out, small, d_ada_rows = [None] * n_layers, [None] * n_layers, [None] * n_layers, [None] * n_layers
    waiting = []
    d_ws_all = [None] * n_layers
    for l in reversed(range(n_layers)):
        x_l, pa, pb, o, probs, p_sink = saved[l]
        dpb, do, dw_out, d_gate8, d_ws, d_bs = _mix_out_bwd(
            dx, pb, o, gate[l], w_outs[l], w_s_m[l], w_s_t[l], b_st[l], f"mix_out_bwd_{l}"
        )
        waiting.append((g_w_out, l, dw_out.reshape(4, 2, D_MIX // N_DEV, d)))
        riding, waiting = ([], waiting) if 0 < l == n_layers - 1 else (waiting, [])
        attn = _attn_bwd(
            pa, o, do, probs, p_sink, q_gain2[l], k_gain2[l], f"attn_bwd_{l}", scatter=tuple(b for _, _, b in riding)
        )
        dq, dkv, halo_prev, halo_next, d_qg, d_kg, d_sk = attn[:7]
        if riding:
            for (dest, layer, _), total in zip(riding, _scatter_finish(attn[7:], f"scatter_finish_{l}")):
                dest[layer] = total.transpose(1, 0) if dest is g_w_in else total
        d_ws_all[l] = d_ws
        dw_args = (x_l, gain[l], scale1[l], shift[l], dq, dkv, halo_prev, halo_next, dpb, f"proj_bwd_dw_{l}")
        if l > 0:
            dw_in_t, dkvb = _proj_bwd_dw(*dw_args)
        else:
            d_ws_wire = jnp.stack(d_ws_all).reshape(-1, LANES).astype(jnp.bfloat16)
            dw_in_t, dkvb, gathered_ws = _proj_bwd_dw(*dw_args, gather=(d_ws_wire,))
        blocks_in = dw_in_t.reshape(4, 2, w_cols, d)
        if l > 0:
            waiting.append((g_w_in, l, blocks_in))
            dx, c0, c1 = _proj_bwd_dx(x_l, dx, dq, dkvb, dpb, w_in_ts[l], gain[l], scale1[l], f"proj_bwd_dx_{l}")
        else:
            *in_flight, token = _scatter_start(blocks_in, "scatter_start_in_0")
            dx, c0, c1 = _proj_bwd_dx(
                x_l, dx, dq, dkvb, dpb, w_in_ts[l], gain[l] + token[0, 0], scale1[l], f"proj_bwd_dx_{l}"
            )
            sent, landed = _scatter_wait(*in_flight, dx, "scatter_wait_in_0")
            g_w_in[l] = _scatter_finish((landed,), "scatter_finish_in_0", own=(sent,))[0].transpose(1, 0)
        c0s, c1s = c0.sum(axis=0), c1.sum(axis=0)
        d_ada_rows[l] = jnp.concatenate([c0s, norm_gain[l] * c1s, d_gate8.sum(axis=0)])
        small[l] = (
            scale1[l, 0] * c1s,
            d_qg.sum(axis=0).reshape(N_HEADS, HEAD_DIM).sum(axis=0),
            d_kg.sum(axis=0).reshape(2, HEAD_DIM).sum(axis=0),
            d_sk.sum(axis=0)[0:N_HEADS],
            d_bs.reshape(BLK, N_GROUPS, HEAD_DIM).sum(axis=2).transpose(1, 0),
        )

    names = ("norm_gain", "q_gain", "k_gain", "sink", "b_s")
    stacked = [jnp.stack([small[l][t] for l in range(n_layers)]) for t in range(len(names))]
    d_ada = jnp.stack(d_ada_rows)
    packed, offsets = _pack_rows(stacked + [d_ada, sq_err[0, 0:1]])
    g_w_in, g_w_out = jnp.stack(g_w_in), jnp.stack(g_w_out)
    gathered, *upd = _gather_small(packed, adam=((w_in, g_w_in, m_w_in, v_w_in), (w_out, g_w_out, m_w_out, v_w_out)))
    gathered_ws = gathered_ws.reshape(N_DEV, -1, LANES)
    upd_in, upd_out = upd[0:3], upd[3:6]
    no_weight = jnp.zeros((1,), F32)
    weights = (norm_gain, q_gain, k_gain, sink, b_s, b_ada, no_weight)
    moments_m = (m_norm_gain, m_q_gain, m_k_gain, m_sink, m_b_s, m_b_ada, no_weight)
    moments_v = (v_norm_gain, v_q_gain, v_k_gain, v_sink, v_b_s, v_b_ada, no_weight)
    w_pack, _ = _pack_rows(weights)
    m_pack, _ = _pack_rows(moments_m)
    v_pack, _ = _pack_rows(moments_v)
    shapes = [w.shape for w in weights]
    flat_ws = lambda a: a.reshape(-1, LANES)
    updated = _small_update(gathered, gathered_ws, w_pack, m_pack, v_pack, flat_ws(w_s), flat_ws(m_w_s), flat_ws(v_w_s))
    g_small, d_small, m_small, v_small = (_unpack_rows(p, offsets, shapes) for p in updated[0:4])
    ws_small = [p.reshape(w_s.shape) for p in updated[4:8]]
    loss = g_small[-1][0] * (0.5 / d)

    ada_off = offsets[-2]
    ada_n = -(-n_layers * 3 * d // (SUBLANES * LANES)) * SUBLANES
    d_ada_all = gathered[:, ada_off : ada_off + ada_n].reshape(N_DEV, -1)[:, : n_layers * 3 * d].reshape(N_DEV, n_layers, 3 * d)
    d_ada_cols = lax.dynamic_slice_in_dim(d_ada_all, my * ada_cols, ada_cols, axis=2)
    g_w_ada, *upd_ada = _ada_update(c_all[:, 0, :], d_ada_cols.transpose(1, 0, 2), w_ada, m_w_ada, v_w_ada)

    def ordered(ada_, in_, out_, small_, ws):
        ng, qg, kg, sk, bs, ba, _ = small_
        return (ada_, ba, ng, in_, qg, kg, sk, ws, bs, out_)

    grads = ordered(g_w_ada, g_w_in, g_w_out, g_small, ws_small[0])
    deltas = ordered(upd_ada[0], upd_in[0], upd_out[0], d_small, ws_small[1])
    new_m = ordered(upd_ada[1], upd_in[1], upd_out[1], m_small, ws_small[2])
    new_v = ordered(upd_ada[2], upd_in[2], upd_out[2], v_small, ws_small[3])
    return (loss, dx.reshape(x.shape), *grads, *deltas, *new_m, *new_v)
```

```python
import functools

import jax
import jax.numpy as jnp
from jax import lax
from jax.experimental import pallas as pl
from jax.experimental.pallas import tpu as pltpu

F32 = jnp.float32
MXU_DTYPE = jnp.bfloat16
MESH_ID = pl.DeviceIdType.MESH

N_DEV = 8
HEAD_DIM = 64
N_HEADS = 8
Q_PER_KV = 4
D_ATTN = 512
D_KV = 128
D_GM = 512
N_GROUPS = 8
D_MIX = D_ATTN + D_GM
BLK = 128
LANES = 128
SUBLANES = 8
N_PAIRS = D_ATTN // LANES
D_QKV = D_ATTN + 2 * D_KV
D_REST = D_ATTN + 3 * D_GM
D_IN = D_QKV + D_REST
EPS = 1e-6
NEG_INF = -1e30
ALIBI_SLOPES = tuple(2.0 ** (-8.0 * (h + 1) / N_HEADS) for h in range(N_HEADS))
Q_SCALE = 1.0 / 8.0

ADAM_LR = 0.001
ADAM_B1 = 0.9
ADAM_B2 = 0.999
ADAM_EPS = 1e-08
ADAM_WD = 0.01
ADAM_STEP = 10

TOKEN_TILE = 512
VMEM_LIMIT_BYTES = 56 * 1024 * 1024


def _params(semantics=None):
    return pltpu.CompilerParams(dimension_semantics=semantics, vmem_limit_bytes=VMEM_LIMIT_BYTES)


def _dot(a, b):
    return jnp.dot(a, b, preferred_element_type=F32)


def _dot_nt(a, b):
    return lax.dot_general(a, b, (((1,), (1,)), ((), ())), preferred_element_type=F32)


def _dot_tn(a, b):
    return lax.dot_general(a, b, (((0,), (0,)), ((), ())), preferred_element_type=F32)


def _mx(v):
    return v.astype(MXU_DTYPE)


def _lane_lo(rows):
    return lax.broadcasted_iota(jnp.int32, (rows, LANES), 1) < HEAD_DIM


def _half_ones(width=LANES):
    group_bits = HEAD_DIM.bit_length() - 1
    r = jnp.right_shift(lax.broadcasted_iota(jnp.int32, (width, width), 0), group_bits)
    c = jnp.right_shift(lax.broadcasted_iota(jnp.int32, (width, width), 1), group_bits)
    return jnp.where(r == c, 1.0, 0.0).astype(jnp.bfloat16)


WIDE = 2 * LANES


def _half_sum(v, ones):
    p1 = v.astype(jnp.bfloat16)
    p2 = (v - p1.astype(F32)).astype(jnp.bfloat16)
    return _dot(p1, ones) + _dot(p2, ones)


def _half_rms(v, ones):
    r = lax.rsqrt(_half_sum(v * v, ones) * (1.0 / HEAD_DIM) + EPS)
    return v * r, r


def _half_rms_bwd(dy, vhat, r, ones):
    return r * (dy - vhat * (_half_sum(vhat * dy, ones) * (1.0 / HEAD_DIM)))


def _group_rows(v):
    rows, n = v.shape
    return v.reshape(rows // SUBLANES, SUBLANES, n).sum(axis=0)


def _sigmoid(v):
    return 1.0 / (1.0 + jnp.exp(-v))


ROW_CHUNK = 32
VARIANT_HEADS = ((0, 2, 5, 7), (1, 3, 4, 6))
HEAD_SLOT = {h: (v, s) for v, heads in enumerate(VARIANT_HEADS) for s, h in enumerate(heads)}
STACK = Q_PER_KV * BLK


def _fill_attn_bias(bias_s):
    qi = lax.broadcasted_iota(jnp.int32, (BLK, 3 * BLK), 0)
    ci = lax.broadcasted_iota(jnp.int32, (BLK, 3 * BLK), 1)
    dist = jnp.abs(ci - BLK - qi)
    distf = dist.astype(F32)
    window = dist <= BLK
    for kind, seen in enumerate((window & (ci >= BLK), window, window & (ci < 2 * BLK))):
        for h in range(N_HEADS):
            bias_s[kind, h] = jnp.where(seen, -(ALIBI_SLOPES[h] * distf), NEG_INF)


def _block_kind(block, seq):
    assert seq >= 2 * BLK
    return jnp.where(block == 0, 0, jnp.where(block == seq // BLK - 1, 2, 1))


def _stage_queries(qn, lo_t, j, nb, qs):
    for a in range(2):
        v, slot = HEAD_SLOT[2 * j + a]
        qm = _mx(jnp.where(lo_t, qn, 0.0) if a == 0 else jnp.where(lo_t, 0.0, qn))
        for n in range(nb):
            qs[n, v, slot * BLK : (slot + 1) * BLK, :] = qm[n * BLK : (n + 1) * BLK]


def _unstack_pair(stacked, j, lo):
    (v0, s0), (v1, s1) = HEAD_SLOT[2 * j], HEAD_SLOT[2 * j + 1]
    return jnp.where(lo, stacked[v0][s0 * BLK : (s0 + 1) * BLK], stacked[v1][s1 * BLK : (s1 + 1) * BLK])


def _stage_keys(kvp_ref, qkv_ref, kvn_ref, kg, ones, tile, ks, kr, vs, vr, khat_s=None, rk_s=None):
    pieces = (
        (0, BLK, kvp_ref[:, 0:D_KV], kvp_ref[:, D_KV : 2 * D_KV]),
        (BLK, tile, qkv_ref[:, D_ATTN : D_ATTN + D_KV], qkv_ref[:, D_ATTN + D_KV : D_QKV]),
        (BLK + tile, BLK, kvn_ref[:, 0:D_KV], kvn_ref[:, D_KV : 2 * D_KV]),
    )
    for r0, n, k, v in pieces:
        khat, rk = _half_rms(k, ones)
        kn = khat * kg
        ks[r0 : r0 + n, :] = _mx(kn)
        kr[r0 : r0 + n, :] = _mx(pltpu.roll(kn, HEAD_DIM, 1))
        vs[r0 : r0 + n, :] = _mx(v)
        vr[r0 : r0 + n, :] = _mx(pltpu.roll(v, HEAD_DIM, 1))
        if khat_s is not None:
            khat_s[r0 : r0 + n, :] = khat
            rk_s[r0 : r0 + n, :] = rk


def _halo_specs(tile, seq):
    nb = tile // BLK
    last = seq // BLK - 1
    kv_col = D_ATTN // (2 * D_KV)
    prev = pl.BlockSpec((BLK, 2 * D_KV), lambda i: (jnp.maximum(i * nb - 1, 0), kv_col))
    nxt = pl.BlockSpec((BLK, 2 * D_KV), lambda i: (jnp.minimum((i + 1) * nb, last), kv_col))
    return prev, nxt


def _row_spec(tile, width):
    return pl.BlockSpec((tile, width), lambda i: (i, 0))


def _full_spec(shape):
    nd = len(shape)
    return pl.BlockSpec(shape, lambda i: (0,) * nd)


SMEM_SPEC = pl.BlockSpec(memory_space=pltpu.SMEM)
VMEM_SPEC = pl.BlockSpec(memory_space=pltpu.VMEM)
HBM_SPEC = pl.BlockSpec(memory_space=pltpu.HBM)


def _rider_steps(nt):
    return 0, (3 * nt) // 4, nt - 1


def _gather_rider(sources, gathered, sems, step, nt):
    start, forward, finish = _all_gather_stages(
        [_row_block(g, s.shape[0]) for g, s in zip(gathered, sources)], sems[0], sems[1], sources=sources, local_sems=sems[2]
    )
    at_start, at_forward, at_finish = _rider_steps(nt)
    pl.when(step == at_start)(start)

    def after_compute():
        pl.when(step == at_forward)(forward)
        pl.when(step == at_finish)(finish)

    return after_compute


def _gathered_shapes(gather):
    return [jax.ShapeDtypeStruct((N_DEV * g.shape[0], g.shape[1]), g.dtype) for g in gather]


def _ln_proj_fwd(x, gain, scale1, shift, w_in_t, name):
    seq, d = x.shape
    tile = min(TOKEN_TILE, seq)

    def body(x_ref, g_ref, s1_ref, sh_ref, wt_ref, pa_ref, pb_ref):
        xv = x_ref[...]
        r = lax.rsqrt(jnp.mean(xv * xv, axis=-1, keepdims=True) + EPS)
        h = _mx((xv * r) * g_ref[...] * s1_ref[...] + sh_ref[...])
        pa_ref[...] = _dot_nt(h, wt_ref[0:D_QKV, :])
        pb_ref[...] = _dot_nt(h, wt_ref[D_QKV:D_IN, :])

    vec = _full_spec((1, d))
    return pl.pallas_call(
        body,
        name=name,
        grid=(seq // tile,),
        in_specs=[_row_spec(tile, d), vec, vec, vec, _full_spec((D_IN, d))],
        out_specs=[_row_spec(tile, D_QKV), _row_spec(tile, D_REST)],
        out_shape=[jax.ShapeDtypeStruct((seq, D_QKV), F32), jax.ShapeDtypeStruct((seq, D_REST), F32)],
        compiler_params=_params(("parallel",)),
    )(x, gain, scale1, shift, w_in_t)


def _attn_fwd(pa, q_gain2, k_gain2, sink, name, gather=()):
    seq = pa.shape[0]
    tile = min(TOKEN_TILE, seq)
    nb = tile // BLK
    nt = seq // tile
    ext = tile + 2 * BLK
    n_ride = len(gather)
    riding = n_ride > 0

    def body(sink_ref, qkv_ref, kvp_ref, kvn_ref, qg_ref, kg_ref, *rest):
        i = pl.program_id(0)
        sources, (o_ref, p_ref, psink_ref), gathered = rest[:n_ride], rest[n_ride : n_ride + 3], rest[n_ride + 3 : 2 * n_ride + 3]
        qs, ks, kr, vs, vr, bias_s, s_scr, *sems = rest[2 * n_ride + 3 :]
        if riding:
            after_compute = _gather_rider(sources, gathered, sems, i, nt)

        @pl.when(i == 0)
        def _():
            _fill_attn_bias(bias_s)

        ones = _half_ones()
        lo = _lane_lo(BLK)
        lo_t = _lane_lo(tile)
        head_lane = lax.broadcasted_iota(jnp.int32, (ROW_CHUNK, LANES), 1)
        _stage_keys(kvp_ref, qkv_ref, kvn_ref, kg_ref[...], ones, tile, ks, kr, vs, vr)
        for j in range(N_PAIRS):
            qhat, _ = _half_rms(qkv_ref[:, j * LANES : (j + 1) * LANES], ones)
            _stage_queries(qhat * (qg_ref[...] * Q_SCALE), lo_t, j, nb, qs)

        def block(n, carry):
            r0 = pl.multiple_of(n * BLK, BLK)
            krows = pl.ds(r0, 3 * BLK)
            kind = _block_kind(i * nb + n, seq)
            for v in range(2):
                s_scr[v] = _dot_nt(qs[n, v], (kr if v else ks)[krows, :])
            for rc in range(0, BLK, ROW_CHUNK):
                p_sink = jnp.zeros((ROW_CHUNK, LANES), F32)
                for h in range(N_HEADS):
                    v, slot = HEAD_SLOT[h]
                    sink_h = sink_ref[h]
                    rows = slice(slot * BLK + rc, slot * BLK + rc + ROW_CHUNK)
                    s = s_scr[v, rows, :] + bias_s[kind, h, rc : rc + ROW_CHUNK, :]
                    m = jnp.maximum(jnp.max(s, axis=-1, keepdims=True), sink_h)
                    p = jnp.exp(s - m)
                    e_sink = jnp.exp(sink_h - m)
                    inv = 1.0 / (jnp.sum(p, axis=-1, keepdims=True) + e_sink)
                    p_ref[n, v, rows, :] = _mx(p * inv)
                    p_sink = jnp.where(head_lane == h, e_sink * inv, p_sink)
                psink_ref[pl.ds(pl.multiple_of(r0 + rc, ROW_CHUNK), ROW_CHUNK), :] = p_sink
            outs = [_dot(p_ref[n, v], (vr if v else vs)[krows, :]) for v in range(2)]
            for j in range(N_PAIRS):
                o_ref[pl.ds(r0, BLK), j * LANES : (j + 1) * LANES] = _unstack_pair(outs, j, lo)
            return carry

        lax.fori_loop(0, nb, block, 0)
        if riding:
            after_compute()

    prev, nxt = _halo_specs(tile, seq)
    vec = _full_spec((1, LANES))
    in_specs = [SMEM_SPEC, _row_spec(tile, D_QKV), prev, nxt, vec, vec]
    out_specs = [
        _row_spec(tile, D_ATTN),
        pl.BlockSpec((nb, 2, STACK, 3 * BLK), lambda i: (i, 0, 0, 0)),
        _row_spec(tile, LANES),
    ]
    out_shape = [
        jax.ShapeDtypeStruct((seq, D_ATTN), F32),
        jax.ShapeDtypeStruct((seq // BLK, 2, STACK, 3 * BLK), MXU_DTYPE),
        jax.ShapeDtypeStruct((seq, LANES), F32),
    ]
    scratch = [
        pltpu.VMEM((nb, 2, STACK, LANES), MXU_DTYPE),
        pltpu.VMEM((ext, LANES), MXU_DTYPE),
        pltpu.VMEM((ext, LANES), MXU_DTYPE),
        pltpu.VMEM((ext, LANES), MXU_DTYPE),
        pltpu.VMEM((ext, LANES), MXU_DTYPE),
        pltpu.VMEM((3, N_HEADS, BLK, 3 * BLK), F32),
        pltpu.VMEM((2, STACK, 3 * BLK), F32),
    ]
    return pl.pallas_call(
        body,
        name=name,
        grid=(nt,),
        in_specs=in_specs + [HBM_SPEC] * n_ride,
        out_specs=out_specs + [HBM_SPEC] * n_ride,
        out_shape=out_shape + _gathered_shapes(gather),
        scratch_shapes=scratch + _rider_sems(n_ride),
        compiler_params=_params(("arbitrary",)),
    )(sink, pa, pa, pa, q_gain2, k_gain2, *gather)


def _mix_out_fwd(pb, o, x, gate, w_out, w_s, b_st, name, target=None):
    seq, d = x.shape
    tile = min(TOKEN_TILE, seq)
    nb = tile // BLK
    with_loss = target is not None

    def body(pb_ref, o_ref, x_ref, gate_ref, wo_ref, ws_ref, bs_ref, *rest):
        if with_loss:
            t_ref, xo_ref, acc_ref, sv_ref, y_s, vn_s = rest

            @pl.when(pl.program_id(0) == 0)
            def _():
                acc_ref[...] = jnp.zeros_like(acc_ref)
        else:
            xo_ref, sv_ref, y_s, vn_s = rest
        ones = _half_ones(WIDE)
        lo = _lane_lo(BLK)
        ga = pb_ref[:, 0:D_ATTN]
        y_s[:, 0:D_ATTN] = _mx(o_ref[...] * (ga * _sigmoid(ga)))
        for j in range(D_GM // WIDE):
            vhat, _ = _half_rms(pb_ref[:, 2 * D_GM + j * WIDE : 2 * D_GM + (j + 1) * WIDE], ones)
            vn_s[:, j * WIDE : (j + 1) * WIDE] = _mx(vhat)

        def chunk(n, carry):
            rows = pl.ds(pl.multiple_of(n * BLK, BLK), BLK)
            for j in range(N_PAIRS):
                cols = slice(j * LANES, (j + 1) * LANES)
                vn = vn_s[rows, cols]
                sv = jnp.where(lo, _dot(ws_ref[2 * j], vn), _dot(ws_ref[2 * j + 1], vn)) + bs_ref[:, cols]
                sv_ref[rows, cols] = sv
                u = pb_ref[rows, D_ATTN + j * LANES : D_ATTN + (j + 1) * LANES]
                gg = pb_ref[rows, D_ATTN + 2 * D_GM + j * LANES : D_ATTN + 2 * D_GM + (j + 1) * LANES]
                y_s[rows, D_ATTN + j * LANES : D_ATTN + (j + 1) * LANES] = _mx((u * sv) * (gg * _sigmoid(gg)))
            return carry

        lax.fori_loop(0, nb, chunk, 0)
        y = x_ref[...] + gate_ref[...] * _dot(y_s[...], wo_ref[...])
        if with_loss:
            e = y - t_ref[...]
            xo_ref[...] = e * (1.0 / d)
            acc_ref[...] += jnp.sum(jnp.sum(e * e, axis=-1, keepdims=True), axis=0, keepdims=True)
        else:
            xo_ref[...] = y

    row = _row_spec(tile, d)
    acc_shape = (SUBLANES, LANES)
    return pl.pallas_call(
        body,
        name=name,
        grid=(seq // tile,),
        in_specs=[
            _row_spec(tile, D_REST),
            _row_spec(tile, D_ATTN),
            row,
            _full_spec((1, d)),
            _full_spec((D_MIX, d)),
            _full_spec((N_GROUPS, BLK, BLK)),
            _full_spec((BLK, D_GM)),
        ]
        + ([row] if with_loss else []),
        out_specs=[row] + ([_full_spec(acc_shape)] if with_loss else []) + [_row_spec(tile, D_GM)],
        out_shape=[jax.ShapeDtypeStruct((seq, d), F32)]
        + ([jax.ShapeDtypeStruct(acc_shape, F32)] if with_loss else [])
        + [jax.ShapeDtypeStruct((seq, D_GM), F32)],
        scratch_shapes=[pltpu.VMEM((tile, D_MIX), MXU_DTYPE), pltpu.VMEM((tile, D_GM), MXU_DTYPE)],
        compiler_params=_params(("arbitrary",) if with_loss else ("parallel",)),
    )(pb, o, x, gate, w_out, w_s, b_st, *([target] if with_loss else []))


def _mix_out_bwd(dxn, pb, o, sv, gate, w_out, w_s_t, name):
    seq, d = dxn.shape
    tile = min(TOKEN_TILE, seq)
    nb = tile // BLK
    nt = seq // tile

    def body(dxn_ref, pb_ref, o_ref, sv_ref, gate_ref, wo_ref, wst_ref,
             dpb_ref, do_ref, dwo_ref, dgate_ref, dws_ref, dbs_ref, g_ref, y_s, dy_s, vn_s, rv_s, vnb_s, dsv_s, dvn_s):
        @pl.when(pl.program_id(0) == 0)
        def _():
            g_ref[...] = jnp.zeros_like(g_ref)
            dws_ref[...] = jnp.zeros_like(dws_ref)
            dbs_ref[...] = jnp.zeros_like(dbs_ref)

        ones = _half_ones(WIDE)
        lo = _lane_lo(BLK)
        c_u = slice(D_ATTN, D_ATTN + D_GM)
        c_vg = slice(D_ATTN + D_GM, D_ATTN + 2 * D_GM)
        c_gg = slice(D_ATTN + 2 * D_GM, D_REST)
        dxv = dxn_ref[...]
        dy_s[...] = _dot_nt(_mx(dxv * gate_ref[...]), wo_ref[...])
        ga = pb_ref[:, 0:D_ATTN]
        sig = _sigmoid(ga)
        sil = ga * sig
        ov = o_ref[...]
        y_s[:, 0:D_ATTN] = _mx(ov * sil)
        da = dy_s[:, 0:D_ATTN]
        do_ref[...] = da * sil
        dpb_ref[:, 0:D_ATTN] = (da * ov * (sig * (1.0 + ga * (1.0 - sig)))).astype(dpb_ref.dtype)
        for j in range(D_GM // WIDE):
            cols = slice(j * WIDE, (j + 1) * WIDE)
            vhat, rv = _half_rms(pb_ref[:, 2 * D_GM + j * WIDE : 2 * D_GM + (j + 1) * WIDE], ones)
            vn_s[:, cols] = vhat
            rv_s[:, cols] = rv
            vnb_s[:, cols] = _mx(vhat)

        def gating(n, carry):
            rows = pl.ds(pl.multiple_of(n * BLK, BLK), BLK)
            sv = sv_ref[rows, :]
            u = pb_ref[rows, c_u]
            gg = pb_ref[rows, c_gg]
            sg = _sigmoid(gg)
            silg = gg * sg
            m0 = u * sv
            y_s[rows, D_ATTN:D_MIX] = _mx(m0 * silg)
            dm = dy_s[rows, D_ATTN:D_MIX]
            dm0 = dm * silg
            dpb_ref[rows, c_gg] = (dm * m0 * (sg * (1.0 + gg * (1.0 - sg)))).astype(dpb_ref.dtype)
            dpb_ref[rows, c_u] = (dm0 * sv).astype(dpb_ref.dtype)
            dsv = dm0 * u
            dsv_s[rows, :] = _mx(dsv)
            dbs_ref[...] += dsv
            return carry

        lax.fori_loop(0, nb, gating, 0)

        def spatial_bwd(n, carry):
            rows = pl.ds(pl.multiple_of(n * BLK, BLK), BLK)
            for j in range(N_PAIRS):
                cols = slice(j * LANES, (j + 1) * LANES)
                dsv = dsv_s[rows, cols]
                dvn_s[rows, cols] = jnp.where(lo, _dot(wst_ref[2 * j], dsv), _dot(wst_ref[2 * j + 1], dsv))
            return carry

        lax.fori_loop(0, nb, spatial_bwd, 0)
        zero = jnp.zeros((BLK, LANES), MXU_DTYPE)
        for j in range(N_PAIRS):
            cols = slice(j * LANES, (j + 1) * LANES)
            chunks = [dsv_s[n * BLK : (n + 1) * BLK, cols] for n in range(nb)]
            vn_all = jnp.concatenate([vnb_s[n * BLK : (n + 1) * BLK, cols] for n in range(nb)], axis=1)
            dws_ref[2 * j] += _dot_nt(jnp.concatenate([jnp.where(lo, c, zero) for c in chunks], axis=1), vn_all)
            dws_ref[2 * j + 1] += _dot_nt(jnp.concatenate([jnp.where(lo, zero, c) for c in chunks], axis=1), vn_all)
        for j in range(D_GM // WIDE):
            cols = slice(j * WIDE, (j + 1) * WIDE)
            dpb_ref[:, D_ATTN + D_GM + j * WIDE : D_ATTN + D_GM + (j + 1) * WIDE] = _half_rms_bwd(
                dvn_s[:, cols], vn_s[:, cols], rv_s[:, cols], ones
            ).astype(dpb_ref.dtype)
        g_ref[...] += _dot_tn(y_s[...], _mx(dxv))

        @pl.when(pl.program_id(0) == nt - 1)
        def _():
            gv = g_ref[...]
            dwo_ref[...] = (gv * gate_ref[...]).astype(dwo_ref.dtype)
            dgate_ref[...] = _group_rows(gv * wo_ref[...].astype(F32))

    return pl.pallas_call(
        body,
        name=name,
        grid=(seq // tile,),
        in_specs=[
            _row_spec(tile, d),
            _row_spec(tile, D_REST),
            _row_spec(tile, D_ATTN),
            _row_spec(tile, D_GM),
            _full_spec((1, d)),
            _full_spec((D_MIX, d)),
            _full_spec((N_GROUPS, BLK, BLK)),
        ],
        out_specs=[
            _row_spec(tile, D_REST),
            _row_spec(tile, D_ATTN),
            _full_spec((D_MIX, d)),
            _full_spec((SUBLANES, d)),
            _full_spec((N_GROUPS, BLK, BLK)),
            _full_spec((BLK, D_GM)),
        ],
        out_shape=[
            jax.ShapeDtypeStruct((seq, D_REST), MXU_DTYPE),
            jax.ShapeDtypeStruct((seq, D_ATTN), F32),
            jax.ShapeDtypeStruct((D_MIX, d), jnp.bfloat16),
            jax.ShapeDtypeStruct((SUBLANES, d), F32),
            jax.ShapeDtypeStruct((N_GROUPS, BLK, BLK), F32),
            jax.ShapeDtypeStruct((BLK, D_GM), F32),
        ],
        scratch_shapes=[
            pltpu.VMEM((D_MIX, d), F32),
            pltpu.VMEM((tile, D_MIX), MXU_DTYPE),
            pltpu.VMEM((tile, D_MIX), F32),
            pltpu.VMEM((tile, D_GM), F32),
            pltpu.VMEM((tile, D_GM), F32),
            pltpu.VMEM((tile, D_GM), MXU_DTYPE),
            pltpu.VMEM((tile, D_GM), MXU_DTYPE),
            pltpu.VMEM((tile, D_GM), F32),
        ],
        compiler_params=_params(("arbitrary",)),
    )(dxn, pb, o, sv, gate, w_out, w_s_t)


def _attn_bwd(pa, o, do, probs, p_sink, q_gain2, k_gain2, name, scatter=()):
    seq = pa.shape[0]
    tile = min(TOKEN_TILE, seq)
    nb = tile // BLK
    nt = seq // tile
    ext = tile + 2 * BLK
    n_ride = len(scatter)
    riding = n_ride > 0

    def body(qkv_ref, kvp_ref, kvn_ref, o_ref, do_ref, p_ref, psink_ref, qg_ref, kg_ref, *rest):
        i = pl.program_id(0)
        blocks, rest = rest[:n_ride], rest[n_ride:]
        dq_ref, dkv_ref, hp_ref, hn_ref, dqg_ref, dkg_ref, dsk_ref = rest[:7]
        landing, rest = rest[7 : 7 + n_ride], rest[7 + n_ride :]
        (qs, dos, qhat_s, rq_s, ks, kr, vs, vr, khat_s, rk_s, dqn_s, dka, dva, dp_scr, ds_scr) = rest[:15]
        if riding:
            start, finish = _scatter_stages(blocks, landing, *rest[15:])
            at_start, _, at_finish = _rider_steps(nt)
            pl.when(i == at_start)(start)

        @pl.when(i == 0)
        def _():
            dqg_ref[...] = jnp.zeros_like(dqg_ref)
            dkg_ref[...] = jnp.zeros_like(dkg_ref)
            dsk_ref[...] = jnp.zeros_like(dsk_ref)

        ones = _half_ones()
        lo = _lane_lo(BLK)
        lo_t = _lane_lo(tile)
        lo_c = _lane_lo(ROW_CHUNK)
        qg = qg_ref[...] * Q_SCALE
        kg = kg_ref[...]
        _stage_keys(kvp_ref, qkv_ref, kvn_ref, kg, ones, tile, ks, kr, vs, vr, khat_s, rk_s)
        head_lane = lax.broadcasted_iota(jnp.int32, (tile, LANES), 1)
        d_rows = jnp.zeros((tile, LANES), F32)
        for j in range(N_PAIRS):
            cols = slice(j * LANES, (j + 1) * LANES)
            qhat, rq = _half_rms(qkv_ref[:, cols], ones)
            qhat_s[:, cols] = qhat
            rq_s[:, cols] = rq
            _stage_queries(qhat * qg, lo_t, j, nb, qs)
            dov = do_ref[:, cols]
            _stage_queries(dov, lo_t, j, nb, dos)
            d_pair = _half_sum(dov * o_ref[:, cols], ones)
            d_rows = jnp.where(head_lane == 2 * j, d_pair, d_rows)
            d_rows = jnp.where(head_lane == 2 * j + 1, pltpu.roll(d_pair, HEAD_DIM, 1), d_rows)
        dsk_ref[...] -= _group_rows(psink_ref[...] * d_rows)
        dka[...] = jnp.zeros_like(dka)
        dva[...] = jnp.zeros_like(dva)

        def block(n, carry):
            r0 = pl.multiple_of(n * BLK, BLK)
            krows = pl.ds(r0, 3 * BLK)
            for v in range(2):
                dp_scr[v] = _dot_nt(dos[n, v], (vr if v else vs)[krows, :])
            for h in range(N_HEADS):
                v, slot = HEAD_SLOT[h]
                j, a = divmod(h, 2)
                cols = slice(j * LANES, (j + 1) * LANES)
                for rc in range(0, BLK, ROW_CHUNK):
                    rows = slice(slot * BLK + rc, slot * BLK + rc + ROW_CHUNK)
                    trows = pl.ds(pl.multiple_of(r0 + rc, ROW_CHUNK), ROW_CHUNK)
                    prod = do_ref[trows, cols] * o_ref[trows, cols]
                    prod = jnp.where(lo_c, prod, 0.0) if a == 0 else jnp.where(lo_c, 0.0, prod)
                    dcol = jnp.sum(prod, axis=-1, keepdims=True)
                    ds_scr[v, rows, :] = _mx(p_ref[n, v, rows, :].astype(F32) * (dp_scr[v, rows, :] - dcol))
            dqv = []
            for v in range(2):
                dqv.append(_dot(ds_scr[v], (kr if v else ks)[krows, :]))
                dka[v, krows, :] += _dot_tn(ds_scr[v], qs[n, v])
                dva[v, krows, :] += _dot_tn(p_ref[n, v], dos[n, v])
            for j in range(N_PAIRS):
                dqn_s[pl.ds(r0, BLK), j * LANES : (j + 1) * LANES] = _unstack_pair(dqv, j, lo)
            return carry

        lax.fori_loop(0, nb, block, 0)
        for j in range(N_PAIRS):
            cols = slice(j * LANES, (j + 1) * LANES)
            dqn = dqn_s[:, cols]
            qhat = qhat_s[:, cols]
            dqg_ref[:, cols] += _group_rows(dqn * qhat) * Q_SCALE
            dq_ref[:, cols] = _half_rms_bwd(dqn * qg, qhat, rq_s[:, cols], ones).astype(dq_ref.dtype)
        dkn = dka[0] + pltpu.roll(dka[1], HEAD_DIM, 1)
        khat = khat_s[...]
        dkg_ref[...] += _group_rows(dkn * khat)
        dk = _half_rms_bwd(dkn * kg, khat, rk_s[...], ones)
        dv = dva[0] + pltpu.roll(dva[1], HEAD_DIM, 1)
        hp_ref[:, 0:D_KV] = dk[0:BLK]
        hp_ref[:, D_KV : 2 * D_KV] = dv[0:BLK]
        dkv_ref[:, 0:D_KV] = dk[BLK : BLK + tile]
        dkv_ref[:, D_KV : 2 * D_KV] = dv[BLK : BLK + tile]
        hn_ref[:, 0:D_KV] = dk[BLK + tile : ext]
        hn_ref[:, D_KV : 2 * D_KV] = dv[BLK + tile : ext]
        if riding:
            pl.when(i == at_finish)(finish)

    prev, nxt = _halo_specs(tile, seq)
    vec = _full_spec((1, LANES))
    halo = pl.BlockSpec((None, BLK, 2 * D_KV), lambda i: (i, 0, 0))
    return pl.pallas_call(
        body,
        name=name,
        grid=(nt,),
        in_specs=[
            _row_spec(tile, D_QKV),
            prev,
            nxt,
            _row_spec(tile, D_ATTN),
            _row_spec(tile, D_ATTN),
            pl.BlockSpec((nb, 2, STACK, 3 * BLK), lambda i: (i, 0, 0, 0)),
            _row_spec(tile, LANES),
            vec,
            vec,
        ]
        + [HBM_SPEC] * n_ride,
        out_specs=[
            _row_spec(tile, D_ATTN),
            _row_spec(tile, 2 * D_KV),
            halo,
            halo,
            _full_spec((SUBLANES, D_ATTN)),
            _full_spec((SUBLANES, LANES)),
            _full_spec((SUBLANES, LANES)),
        ]
        + [HBM_SPEC] * n_ride,
        out_shape=[
            jax.ShapeDtypeStruct((seq, D_ATTN), MXU_DTYPE),
            jax.ShapeDtypeStruct((seq, 2 * D_KV), F32),
            jax.ShapeDtypeStruct((nt, BLK, 2 * D_KV), F32),
            jax.ShapeDtypeStruct((nt, BLK, 2 * D_KV), F32),
            jax.ShapeDtypeStruct((SUBLANES, D_ATTN), F32),
            jax.ShapeDtypeStruct((SUBLANES, LANES), F32),
            jax.ShapeDtypeStruct((SUBLANES, LANES), F32),
        ]
        + _landing_shapes(scatter),
        scratch_shapes=[
            pltpu.VMEM((nb, 2, STACK, LANES), MXU_DTYPE),
            pltpu.VMEM((nb, 2, STACK, LANES), MXU_DTYPE),
            pltpu.VMEM((tile, D_ATTN), F32),
            pltpu.VMEM((tile, D_ATTN), F32),
            pltpu.VMEM((ext, LANES), MXU_DTYPE),
            pltpu.VMEM((ext, LANES), MXU_DTYPE),
            pltpu.VMEM((ext, LANES), MXU_DTYPE),
            pltpu.VMEM((ext, LANES), MXU_DTYPE),
            pltpu.VMEM((ext, LANES), F32),
            pltpu.VMEM((ext, LANES), F32),
            pltpu.VMEM((tile, D_ATTN), F32),
            pltpu.VMEM((2, ext, LANES), F32),
            pltpu.VMEM((2, ext, LANES), F32),
            pltpu.VMEM((2, STACK, 3 * BLK), F32),
            pltpu.VMEM((2, STACK, 3 * BLK), MXU_DTYPE),
        ]
        + _rider_sems(n_ride),
        compiler_params=_params(("arbitrary",)),
    )(pa, pa, pa, o, do, probs, p_sink, q_gain2, k_gain2, *scatter)


def _halo_in_specs(tile, nt):
    from_prev = pl.BlockSpec((None, BLK, 2 * D_KV), lambda i: (jnp.maximum(i - 1, 0), 0, 0))
    from_next = pl.BlockSpec((None, BLK, 2 * D_KV), lambda i: (jnp.minimum(i + 1, nt - 1), 0, 0))
    return from_prev, from_next


def _landing_shapes(scatter):
    return [jax.ShapeDtypeStruct((N_DEV,) + b.shape[2:], b.dtype) for b in scatter]


def _rider_sems(n_ride):
    if not n_ride:
        return []
    return [pltpu.SemaphoreType.DMA((7 * n_ride,)), pltpu.SemaphoreType.DMA((7 * n_ride,)), pltpu.SemaphoreType.DMA((n_ride,))]


def _proj_bwd_dx(x, dxn, dq, dkvb, dpb, w_in_t, gain, scale1, name):
    seq, d = x.shape
    tile = min(TOKEN_TILE, seq)

    def row(width):
        return _row_spec(tile, width)

    def body(x_ref, dxn_ref, dq_ref, dkvb_ref, dpb_ref, wt_ref, g_ref, s1_ref, dx_ref, c0_ref, c1_ref):
        @pl.when(pl.program_id(0) == 0)
        def _():
            c0_ref[...] = jnp.zeros_like(c0_ref)
            c1_ref[...] = jnp.zeros_like(c1_ref)

        dh = (
            _dot(dq_ref[...], wt_ref[0:D_ATTN, :])
            + _dot(dkvb_ref[...], wt_ref[D_ATTN:D_QKV, :])
            + _dot(dpb_ref[...], wt_ref[D_QKV:D_IN, :])
        )
        xv = x_ref[...]
        r = lax.rsqrt(jnp.mean(xv * xv, axis=-1, keepdims=True) + EPS)
        xn = xv * r
        c0_ref[...] += _group_rows(dh)
        c1_ref[...] += _group_rows(dh * xn)
        dxn_ = dh * (g_ref[...] * s1_ref[...])
        dx_ref[...] = dxn_ref[...] + r * (dxn_ - xn * jnp.mean(xn * dxn_, axis=-1, keepdims=True))

    vec = _full_spec((1, d))
    return pl.pallas_call(
        body,
        name=name,
        grid=(seq // tile,),
        in_specs=[row(d), row(d), row(D_ATTN), row(2 * D_KV), row(D_REST), _full_spec((D_IN, d)), vec, vec],
        out_specs=[row(d), _full_spec((SUBLANES, d)), _full_spec((SUBLANES, d))],
        out_shape=[
            jax.ShapeDtypeStruct((seq, d), F32),
            jax.ShapeDtypeStruct((SUBLANES, d), F32),
            jax.ShapeDtypeStruct((SUBLANES, d), F32),
        ],
        compiler_params=_params(("arbitrary",)),
    )(x, dxn, dq, dkvb, dpb, w_in_t, gain, scale1)


def _proj_bwd_dw(x, gain, scale1, shift, dq, dkv, halo_prev, halo_next, dpb, name, gather=()):
    seq, d = x.shape
    tile = min(TOKEN_TILE, seq)
    nt = seq // tile
    assert tile >= 2 * BLK
    n_ride = len(gather)

    def body(x_ref, g_ref, s1_ref, sh_ref, dq_ref, dkv_ref, hn_ref, hp_ref, dpb_ref, *rest):
        i = pl.program_id(0)
        sources, rest = rest[:n_ride], rest[n_ride:]
        dw_ref, dkvb_ref = rest[:2]
        gathered, (acc, *sems) = rest[2 : 2 + n_ride], rest[2 + n_ride :]
        after_compute = _gather_rider(sources, gathered, sems, i, nt) if n_ride else None

        @pl.when(i == 0)
        def _():
            acc[...] = jnp.zeros_like(acc)

        top = dkv_ref[0:BLK, :] + jnp.where(i > 0, hn_ref[...], 0.0)
        bot = dkv_ref[tile - BLK : tile, :] + jnp.where(i < nt - 1, hp_ref[...], 0.0)
        dkvb_ref[0:BLK, :] = top.astype(dkvb_ref.dtype)
        dkvb_ref[tile - BLK : tile, :] = bot.astype(dkvb_ref.dtype)
        if tile > 2 * BLK:
            dkvb_ref[BLK : tile - BLK, :] = dkv_ref[BLK : tile - BLK, :].astype(dkvb_ref.dtype)
        xv = x_ref[...]
        r = lax.rsqrt(jnp.mean(xv * xv, axis=-1, keepdims=True) + EPS)
        h = _mx((xv * r) * g_ref[...] * s1_ref[...] + sh_ref[...])
        acc[0:D_ATTN, :] += _dot_tn(dq_ref[...], h)
        acc[D_ATTN:D_QKV, :] += _dot_tn(dkvb_ref[...], h)
        acc[D_QKV:D_IN, :] += _dot_tn(dpb_ref[...], h)

        @pl.when(i == nt - 1)
        def _():
            dw_ref[...] = acc[...].astype(dw_ref.dtype)

        if n_ride:
            after_compute()

    from_prev, from_next = _halo_in_specs(tile, nt)
    vec = _full_spec((1, d))
    return pl.pallas_call(
        body,
        name=name,
        grid=(nt,),
        in_specs=[
            _row_spec(tile, d),
            vec,
            vec,
            vec,
            _row_spec(tile, D_ATTN),
            _row_spec(tile, 2 * D_KV),
            from_prev,
            from_next,
            _row_spec(tile, D_REST),
        ]
        + [HBM_SPEC] * n_ride,
        out_specs=[_full_spec((D_IN, d)), _row_spec(tile, 2 * D_KV)] + [HBM_SPEC] * n_ride,
        out_shape=[jax.ShapeDtypeStruct((D_IN, d), jnp.bfloat16), jax.ShapeDtypeStruct((seq, 2 * D_KV), MXU_DTYPE)]
        + _gathered_shapes(gather),
        scratch_shapes=[pltpu.VMEM((D_IN, d), F32)] + _rider_sems(n_ride),
        compiler_params=_params(("arbitrary",)),
    )(x, gain, scale1, shift, dq, dkv, halo_next, halo_prev, dpb, *gather)


def _adamw_math(w, g, m, v):
    m = ADAM_B1 * m + (1.0 - ADAM_B1) * g
    v = ADAM_B2 * v + (1.0 - ADAM_B2) * (g * g)
    m_hat = m / (1.0 - ADAM_B1**ADAM_STEP)
    v_hat = v / (1.0 - ADAM_B2**ADAM_STEP)
    delta = -ADAM_LR * (m_hat / (jnp.sqrt(v_hat) + ADAM_EPS) + ADAM_WD * w)
    return delta, m, v


def _small_update(gathered, gathered_ws, w, m, v, ws, m_ws, v_ws):
    def body(ga_ref, gws_ref, w_ref, m_ref, v_ref, ws_ref, mws_ref, vws_ref, *outs):
        for src, refs, out in ((ga_ref, (w_ref, m_ref, v_ref), outs[0:4]), (gws_ref, (ws_ref, mws_ref, vws_ref), outs[4:8])):
            g = src[0].astype(F32)
            for j in range(1, N_DEV):
                g = g + src[j].astype(F32)
            out[0][...] = g
            out[1][...], out[2][...], out[3][...] = _adamw_math(refs[0][...], g, refs[1][...], refs[2][...])

    shapes = [jax.ShapeDtypeStruct(w.shape, F32)] * 4 + [jax.ShapeDtypeStruct(ws.shape, F32)] * 4
    return pl.pallas_call(
        body,
        name="small_update",
        in_specs=[VMEM_SPEC] * 8,
        out_specs=[VMEM_SPEC] * 8,
        out_shape=shapes,
        compiler_params=_params(),
    )(gathered, gathered_ws, w, m, v, ws, m_ws, v_ws)


def _ada_update(c_all, d_ada_cols, w, m, v):
    n_layers = w.shape[0]

    def body(c_ref, da_ref, w_ref, m_ref, v_ref, g_ref, d_ref, mo_ref, vo_ref):
        cv = c_ref[...]
        cond = cv * _sigmoid(cv)
        for l in range(n_layers):
            g = lax.dot_general(
                cond, da_ref[l], (((0,), (0,)), ((), ())), preferred_element_type=F32, precision=lax.Precision.HIGHEST
            )
            g_ref[l] = g
            d_ref[l], mo_ref[l], vo_ref[l] = _adamw_math(w_ref[l], g, m_ref[l], v_ref[l])

    return pl.pallas_call(
        body,
        name="ada_update",
        in_specs=[VMEM_SPEC] * 5,
        out_specs=[VMEM_SPEC] * 4,
        out_shape=[jax.ShapeDtypeStruct(w.shape, F32)] * 4,
        compiler_params=_params(),
    )(c_all, d_ada_cols, w, m, v)


def _position():
    return lax.axis_index("x"), lax.axis_index("y"), lax.axis_index("c")


def _flip(pos, k):
    x, y, c = pos
    return (1 - x if k & 4 else x, 1 - y if k & 2 else y, 1 - c if k & 1 else c)


def _index(pos):
    x, y, c = pos
    return 4 * x + 2 * y + c


def _remote(src, dst, send_sem, recv_sem, to):
    return pltpu.make_async_remote_copy(
        src_ref=src, dst_ref=dst, send_sem=send_sem, recv_sem=recv_sem, device_id=to, device_id_type=MESH_ID
    )


def _all_gather_stages(slots, send_sems, recv_sems, sources=None, local_sems=None):
    me = _position()
    sibling = _flip(me, 1)
    others = (4, 2, 6)
    arrays = range(len(slots))

    def copy(t, k, block, to, own=False):
        slot = slots[t](_index(block))
        src = sources[t] if own and sources is not None else slot
        return _remote(src, slot, send_sems.at[7 * t + k], recv_sems.at[7 * t + k], to)

    def first(t):
        return [copy(t, 0, me, sibling, own=True)] + [copy(t, 1 + j, me, _flip(me, f), own=True) for j, f in enumerate(others)]

    def passed(t, j):
        return copy(t, 4 + j, _flip(me, others[j]), sibling)

    def local(t):
        return pltpu.make_async_copy(sources[t], slots[t](_index(me)), local_sems.at[t])

    def start():
        for t in arrays:
            if sources is not None:
                local(t).start()
            for cp in first(t):
                cp.start()

    def forward():
        for j, f in enumerate(others):
            for t in arrays:
                copy(t, 1 + j, _flip(me, f), me).wait_recv()
                passed(t, j).start()

    def finish():
        for t in arrays:
            copy(t, 0, sibling, me).wait_recv()
            for j, f in enumerate(others):
                copy(t, 4 + j, _flip(sibling, f), me).wait_recv()
        for t in arrays:
            for cp in first(t) + [passed(t, j) for j in range(len(others))]:
                cp.wait_send()
            if sources is not None:
                local(t).wait()

    return start, forward, finish


def _two_level_all_gather(slots, send_sems, recv_sems, between=None):
    start, forward, finish = _all_gather_stages(slots, send_sems, recv_sems)
    start()
    if between is not None:
        between()
    forward()
    finish()


def _row_block(ref, rows):
    return lambda j: ref.at[pl.ds(pl.multiple_of(j * rows, 16), rows), :]


def _scatter_stages(blocks, landing, send_sems, recv_sems, local_sems):
    me = _position()
    my = _index(me)
    arrays = range(len(blocks))

    def copy(t, k):
        px, py, pc = to = _flip(me, k)
        return _remote(blocks[t].at[2 * px + py, pc], landing[t].at[my], send_sems.at[7 * t + k - 1], recv_sems.at[7 * t + k - 1], to)

    def arrival(t, k):
        slot = landing[t].at[_index(_flip(me, k))]
        return _remote(slot, slot, send_sems.at[7 * t + k - 1], recv_sems.at[7 * t + k - 1], _flip(me, k))

    def local(t):
        x, y, c = me
        return pltpu.make_async_copy(blocks[t].at[2 * x + y, c], landing[t].at[my], local_sems.at[t])

    def start():
        for t in arrays:
            local(t).start()
            for k in range(1, N_DEV):
                copy(t, k).start()

    def finish():
        for t in arrays:
            for k in range(1, N_DEV):
                arrival(t, k).wait_recv()
        for t in arrays:
            for k in range(1, N_DEV):
                copy(t, k).wait_send()
            local(t).wait()

    return start, finish


def _ada_exchange(c_ref, w_ref, call_ref, parts_ref, sbuf, sem_s1, sem_r1, sem_s2, sem_r2):
    d = c_ref.shape[-1]
    n_layers = w_ref.shape[0]
    me = _position()
    my = _index(me)
    call_ref[my] = jnp.broadcast_to(c_ref[...], (SUBLANES, d))
    mine = call_ref.at[my]
    first = [_remote(mine, mine, sem_s1.at[k - 1], sem_r1.at[k - 1], _flip(me, k)) for k in range(1, N_DEV)]
    for cp in first:
        cp.start()
    for k in range(1, N_DEV):
        theirs = call_ref.at[_index(_flip(me, k))]
        _remote(theirs, theirs, sem_s1.at[k - 1], sem_r1.at[k - 1], _flip(me, k)).wait_recv()
    cv = call_ref[...].reshape(N_DEV * SUBLANES, d)
    cond = cv * _sigmoid(cv)
    for l in range(n_layers):
        rows = jnp.dot(cond, w_ref[l], preferred_element_type=F32, precision=lax.Precision.HIGHEST)
        for b in range(N_DEV):
            sbuf[b, l] = rows[b * SUBLANES : (b + 1) * SUBLANES]
    parts_ref[my] = sbuf[my]
    second = []
    for k in range(1, N_DEV):
        to = _flip(me, k)
        second.append(_remote(sbuf.at[_index(to)], parts_ref.at[my], sem_s2.at[k - 1], sem_r2.at[k - 1], to))
    for cp in second:
        cp.start()
    for k in range(1, N_DEV):
        theirs = parts_ref.at[_index(_flip(me, k))]
        _remote(theirs, theirs, sem_s2.at[k - 1], sem_r2.at[k - 1], _flip(me, k)).wait_recv()
    for cp in first + second:
        cp.wait_send()


def _gather_weights(w_in_t, w_out, c_row, w_ada):
    n_layers, rows_in, d = w_in_t.shape
    width = w_ada.shape[2]

    def body(wi_ref, wo_ref, c_ref, wa_ref, gi_ref, si_ref, so_ref, call_ref, parts_ref, sbuf, send_sems, recv_sems, *ada_sems):
        my = _index(_position())
        si_ref[...] = wi_ref[...].astype(si_ref.dtype)
        so_ref[...] = wo_ref[...].astype(so_ref.dtype)
        gi_ref[pl.ds(pl.multiple_of(my * rows_in, 16), rows_in), :] = si_ref[0]
        _two_level_all_gather(
            (_row_block(gi_ref, rows_in),),
            send_sems,
            recv_sems,
            between=functools.partial(_ada_exchange, c_ref, wa_ref, call_ref, parts_ref, sbuf, *ada_sems),
        )

    return pl.pallas_call(
        body,
        name="gather_weights",
        in_specs=[VMEM_SPEC] * 4,
        out_specs=[VMEM_SPEC] * 5,
        out_shape=[
            jax.ShapeDtypeStruct((N_DEV * rows_in, d), MXU_DTYPE),
            jax.ShapeDtypeStruct(w_in_t.shape, MXU_DTYPE),
            jax.ShapeDtypeStruct(w_out.shape, MXU_DTYPE),
            jax.ShapeDtypeStruct((N_DEV, SUBLANES, d), F32),
            jax.ShapeDtypeStruct((N_DEV, n_layers, SUBLANES, width), F32),
        ],
        scratch_shapes=[
            pltpu.VMEM((N_DEV, n_layers, SUBLANES, width), F32),
            pltpu.SemaphoreType.DMA((7,)),
            pltpu.SemaphoreType.DMA((7,)),
        ]
        + [pltpu.SemaphoreType.DMA((N_DEV - 1,))] * 4,
        compiler_params=_params(),
    )(w_in_t, w_out, c_row, w_ada)


def _gather_small(packed, adam=()):
    n_adam = len(adam)

    def body(p_ref, *rest):
        quads = [rest[4 * t : 4 * t + 4] for t in range(n_adam)]
        rest = rest[4 * n_adam :]
        g_ref = rest[0]
        results = [rest[1 + 3 * t : 4 + 3 * t] for t in range(n_adam)]
        send_sems, recv_sems = rest[1 + 3 * n_adam :]
        g_ref[_index(_position())] = p_ref[...]

        def updates():
            for (w_ref, gr_ref, m_ref, v_ref), (d_ref, mo_ref, vo_ref) in zip(quads, results):
                d_ref[...], mo_ref[...], vo_ref[...] = _adamw_math(w_ref[...], gr_ref[...], m_ref[...], v_ref[...])

        _two_level_all_gather((lambda j: g_ref.at[j],), send_sems, recv_sems, between=updates)

    return pl.pallas_call(
        body,
        name="gather_small",
        in_specs=[VMEM_SPEC] * (1 + 4 * n_adam),
        out_specs=[VMEM_SPEC] * (1 + 3 * n_adam),
        out_shape=[jax.ShapeDtypeStruct((N_DEV,) + packed.shape, F32)]
        + [jax.ShapeDtypeStruct(q[0].shape, F32) for q in adam for _ in range(3)],
        scratch_shapes=[pltpu.SemaphoreType.DMA((7,)), pltpu.SemaphoreType.DMA((7,))],
        compiler_params=_params(),
    )(packed, *[a for q in adam for a in q])


def _scatter_finish(landed, name, own=()):
    n = len(landed)

    def body(*refs):
        if own:
            x, y, c = _position()
            my = _index((x, y, c))
        for t, (src, out) in enumerate(zip(refs[:n], refs[n + len(own) :])):
            g = None
            for j in range(N_DEV):
                part = src[j].astype(F32)
                if own:
                    part = jnp.where(j == my, refs[n + t][2 * x + y, c].astype(F32), part)
                g = part if g is None else g + part
            out[...] = g

    return pl.pallas_call(
        body,
        name=name,
        in_specs=[VMEM_SPEC] * (n + len(own)),
        out_specs=[VMEM_SPEC] * n,
        out_shape=[jax.ShapeDtypeStruct(a.shape[1:], F32) for a in landed],
        compiler_params=_params(),
    )(*landed, *own)


SEM_SPEC = pl.BlockSpec(memory_space=pltpu.SEMAPHORE)
SPLIT_COPY = pltpu.SideEffectType.DATAFLOW_SIDE_EFFECTING


def _scatter_start(blocks, name):
    land_shape = (N_DEV,) + blocks.shape[2:]

    def body(blocks_ref, land_ref, send_sems, recv_sems, blocks_thru, land_thru, token):
        me = _position()
        my = _index(me)
        for k in range(1, N_DEV):
            px, py, pc = to = _flip(me, k)
            _remote(blocks_ref.at[2 * px + py, pc], land_ref.at[my], send_sems.at[k - 1], recv_sems.at[k - 1], to).start()
        token[...] = jnp.zeros_like(token)

    return pl.pallas_call(
        body,
        name=name,
        in_specs=(HBM_SPEC, HBM_SPEC),
        out_specs=(SEM_SPEC, SEM_SPEC, HBM_SPEC, HBM_SPEC, VMEM_SPEC),
        out_shape=(
            pltpu.SemaphoreType.DMA((N_DEV - 1,)),
            pltpu.SemaphoreType.DMA((N_DEV - 1,)),
            pltpu.HBM(blocks.shape, blocks.dtype),
            pltpu.HBM(land_shape, blocks.dtype),
            jax.ShapeDtypeStruct((SUBLANES, LANES), F32),
        ),
        input_output_aliases={0: 2, 1: 3},
        compiler_params=pltpu.CompilerParams(has_side_effects=SPLIT_COPY),
    )(pltpu.with_memory_space_constraint(blocks, pltpu.HBM), pltpu.with_memory_space_constraint(lax.empty(land_shape, blocks.dtype), pltpu.HBM))


def _scatter_wait(send_sems, recv_sems, blocks_thru, land_thru, after, name):
    def body(blocks_ref, land_ref, send_sems, recv_sems, after_ref, blocks_dead, got_ref):
        me = _position()
        my = _index(me)
        for k in range(1, N_DEV):
            px, py, pc = to = _flip(me, k)
            _remote(blocks_ref.at[2 * px + py, pc], land_ref.at[my], send_sems.at[k - 1], recv_sems.at[k - 1], to).wait_send()
        for k in range(1, N_DEV):
            slot = land_ref.at[_index(_flip(me, k))]
            _remote(slot, slot, send_sems.at[k - 1], recv_sems.at[k - 1], _flip(me, k)).wait_recv()

    return pl.pallas_call(
        body,
        name=name,
        in_specs=(HBM_SPEC, HBM_SPEC, SEM_SPEC, SEM_SPEC, pl.BlockSpec(memory_space=pl.ANY)),
        out_specs=(HBM_SPEC, HBM_SPEC),
        out_shape=(pltpu.HBM(blocks_thru.shape, blocks_thru.dtype), pltpu.HBM(land_thru.shape, land_thru.dtype)),
        input_output_aliases={0: 0, 1: 1},
        compiler_params=pltpu.CompilerParams(has_side_effects=SPLIT_COPY),
    )(blocks_thru, land_thru, send_sems, recv_sems, after)


def _pack_rows(parts):
    rows, offsets, at = [], [], 0
    for p in parts:
        flat = p.reshape(-1)
        n = -(-flat.shape[0] // (SUBLANES * LANES)) * SUBLANES
        rows.append(jnp.pad(flat, (0, n * LANES - flat.shape[0])).reshape(n, LANES))
        offsets.append(at)
        at += n
    return jnp.concatenate(rows, axis=0), offsets


def _unpack_rows(packed, offsets, shapes):
    out = []
    for off, shape in zip(offsets, shapes):
        size = 1
        for s in shape:
            size *= s
        n = -(-size // (SUBLANES * LANES)) * SUBLANES
        out.append(packed[off : off + n].reshape(-1)[:size].reshape(shape))
    return out


def kernel(x, c, w_ada, b_ada, norm_gain, w_in, q_gain, k_gain, sink, w_s, b_s, w_out, loss_target, m_w_ada, m_b_ada, m_norm_gain, m_w_in, m_q_gain, m_k_gain, m_sink, m_w_s, m_b_s, m_w_out, v_w_ada, v_b_ada, v_norm_gain, v_w_in, v_q_gain, v_k_gain, v_sink, v_w_s, v_b_s, v_w_out):
    seq, d = x.shape[1], x.shape[2]
    n_layers = w_in.shape[0]
    w_cols = w_in.shape[2]
    ada_cols = w_ada.shape[2]
    my = _index(_position())
    xs = x.reshape(seq, d)
    target = loss_target.reshape(seq, d)

    w_in_t0, shard_in, shard_out, c_all, ada_parts = _gather_weights(w_in.transpose(0, 2, 1), w_out, c, w_ada)
    w_in_ts, w_outs = [w_in_t0], []
    ada = ada_parts[:, :, 0, :].transpose(1, 0, 2).reshape(n_layers, 3 * d) + b_ada
    shift, scale1, gate = ada[:, None, 0:d], 1.0 + ada[:, None, d : 2 * d], ada[:, None, 2 * d : 3 * d]
    gain = norm_gain[:, None, :]

    w_s_m = w_s.astype(MXU_DTYPE)
    w_s_t = w_s_m.transpose(0, 1, 3, 2)
    b_st = jnp.repeat(b_s.transpose(0, 2, 1), HEAD_DIM, axis=2)
    q_gain2 = jnp.tile(q_gain, (1, 2))[:, None, :]
    k_gain2 = jnp.tile(k_gain, (1, 2))[:, None, :]

    xl, saved = xs, []
    for l in range(n_layers):
        last = l == n_layers - 1
        pa, pb = _ln_proj_fwd(xl, gain[l], scale1[l], shift[l], w_in_ts[l], f"ln_proj_fwd_{l}")
        wanted = ([shard_out[0]] if l == 0 else []) + ([] if last else [shard_out[l + 1], shard_in[l + 1]])
        o, probs, p_sink, *arrived = _attn_fwd(pa, q_gain2[l], k_gain2[l], sink[l], f"attn_fwd_{l}", gather=tuple(wanted))
        if not last:
            w_in_ts.append(arrived.pop())
        w_outs += arrived
        *out, sv = _mix_out_fwd(pb, o, xl, gate[l], w_outs[l], w_s_m[l], b_st[l], f"mix_out_fwd_{l}", target if last else None)
        saved.append((xl, pa, pb, o, probs, p_sink, sv))
        if last:
            dx, sq_err = out
        else:
            (xl,) = out

    g_w_in, g_w_out, small, d_ada_rows = [None] * n_layers, [None] * n_layers, [None] * n_layers, [None] * n_layers
    waiting = []
    d_ws_all = [None] * n_layers
    for l in reversed(range(n_layers)):
        x_l, pa, pb, o, probs, p_sink, sv = saved[l]
        dpb, do, dw_out, d_gate8, d_ws, d_bs = _mix_out_bwd(dx, pb, o, sv, gate[l], w_outs[l], w_s_t[l], f"mix_out_bwd_{l}")
        waiting.append((g_w_out, l, dw_out.reshape(4, 2, D_MIX // N_DEV, d)))
        riding, waiting = ([], waiting) if 0 < l == n_layers - 1 else (waiting, [])
        attn = _attn_bwd(
            pa, o, do, probs, p_sink, q_gain2[l], k_gain2[l], f"attn_bwd_{l}", scatter=tuple(b for _, _, b in riding)
        )
        dq, dkv, halo_prev, halo_next, d_qg, d_kg, d_sk = attn[:7]
        if riding:
            for (dest, layer, _), total in zip(riding, _scatter_finish(attn[7:], f"scatter_finish_{l}")):
                dest[layer] = total.transpose(1, 0) if dest is g_w_in else total
        d_ws_all[l] = d_ws
        dw_args = (x_l, gain[l], scale1[l], shift[l], dq, dkv, halo_prev, halo_next, dpb, f"proj_bwd_dw_{l}")
        if l > 0:
            dw_in_t, dkvb = _proj_bwd_dw(*dw_args)
        else:
            d_ws_wire = jnp.stack(d_ws_all).reshape(-1, LANES).astype(jnp.bfloat16)
            dw_in_t, dkvb, gathered_ws = _proj_bwd_dw(*dw_args, gather=(d_ws_wire,))
        blocks_in = dw_in_t.reshape(4, 2, w_cols, d)
        if l > 0:
            waiting.append((g_w_in, l, blocks_in))
            dx, c0, c1 = _proj_bwd_dx(x_l, dx, dq, dkvb, dpb, w_in_ts[l], gain[l], scale1[l], f"proj_bwd_dx_{l}")
        else:
            *in_flight, token = _scatter_start(blocks_in, "scatter_start_in_0")
            dx, c0, c1 = _proj_bwd_dx(
                x_l, dx, dq, dkvb, dpb, w_in_ts[l], gain[l] + token[0, 0], scale1[l], f"proj_bwd_dx_{l}"
            )
            sent, landed = _scatter_wait(*in_flight, dx, "scatter_wait_in_0")
            g_w_in[l] = _scatter_finish((landed,), "scatter_finish_in_0", own=(sent,))[0].transpose(1, 0)
        c0s, c1s = c0.sum(axis=0), c1.sum(axis=0)
        d_ada_rows[l] = jnp.concatenate([c0s, norm_gain[l] * c1s, d_gate8.sum(axis=0)])
        small[l] = (
            scale1[l, 0] * c1s,
            d_qg.sum(axis=0).reshape(N_HEADS, HEAD_DIM).sum(axis=0),
            d_kg.sum(axis=0).reshape(2, HEAD_DIM).sum(axis=0),
            d_sk.sum(axis=0)[0:N_HEADS],
            d_bs.reshape(BLK, N_GROUPS, HEAD_DIM).sum(axis=2).transpose(1, 0),
        )

    names = ("norm_gain", "q_gain", "k_gain", "sink", "b_s")
    stacked = [jnp.stack([small[l][t] for l in range(n_layers)]) for t in range(len(names))]
    d_ada = jnp.stack(d_ada_rows)
    packed, offsets = _pack_rows(stacked + [d_ada, sq_err[0, 0:1]])
    g_w_in, g_w_out = jnp.stack(g_w_in), jnp.stack(g_w_out)
    gathered, *upd = _gather_small(packed, adam=((w_in, g_w_in, m_w_in, v_w_in), (w_out, g_w_out, m_w_out, v_w_out)))
    gathered_ws = gathered_ws.reshape(N_DEV, -1, LANES)
    upd_in, upd_out = upd[0:3], upd[3:6]
    no_weight = jnp.zeros((1,), F32)
    weights = (norm_gain, q_gain, k_gain, sink, b_s, b_ada, no_weight)
    moments_m = (m_norm_gain, m_q_gain, m_k_gain, m_sink, m_b_s, m_b_ada, no_weight)
    moments_v = (v_norm_gain, v_q_gain, v_k_gain, v_sink, v_b_s, v_b_ada, no_weight)
    w_pack, _ = _pack_rows(weights)
    m_pack, _ = _pack_rows(moments_m)
    v_pack, _ = _pack_rows(moments_v)
    shapes = [w.shape for w in weights]
    flat_ws = lambda a: a.reshape(-1, LANES)
    updated = _small_update(gathered, gathered_ws, w_pack, m_pack, v_pack, flat_ws(w_s), flat_ws(m_w_s), flat_ws(v_w_s))
    g_small, d_small, m_small, v_small = (_unpack_rows(p, offsets, shapes) for p in updated[0:4])
    ws_small = [p.reshape(w_s.shape) for p in updated[4:8]]
    loss = g_small[-1][0] * (0.5 / d)

    ada_off = offsets[-2]
    ada_n = -(-n_layers * 3 * d // (SUBLANES * LANES)) * SUBLANES
    d_ada_all = gathered[:, ada_off : ada_off + ada_n].reshape(N_DEV, -1)[:, : n_layers * 3 * d].reshape(N_DEV, n_layers, 3 * d)
    d_ada_cols = lax.dynamic_slice_in_dim(d_ada_all, my * ada_cols, ada_cols, axis=2)
    g_w_ada, *upd_ada = _ada_update(c_all[:, 0, :], d_ada_cols.transpose(1, 0, 2), w_ada, m_w_ada, v_w_ada)

    def ordered(ada_, in_, out_, small_, ws):
        ng, qg, kg, sk, bs, ba, _ = small_
        return (ada_, ba, ng, in_, qg, kg, sk, ws, bs, out_)

    grads = ordered(g_w_ada, g_w_in, g_w_out, g_small, ws_small[0])
    deltas = ordered(upd_ada[0], upd_in[0], upd_out[0], d_small, ws_small[1])
    new_m = ordered(upd_ada[1], upd_in[1], upd_out[1], m_small, ws_small[2])
    new_v = ordered(upd_ada[2], upd_in[2], upd_out[2], v_small, ws_small[3])
    return (loss, dx.reshape(x.shape), *grads, *deltas, *new_m, *new_v)
```

```python
import functools

import jax
import jax.numpy as jnp
from jax import lax
from jax.experimental import pallas as pl
from jax.experimental.pallas import tpu as pltpu

F32 = jnp.float32
MXU_DTYPE = jnp.bfloat16
MESH_ID = pl.DeviceIdType.MESH

N_DEV = 8
HEAD_DIM = 64
N_HEADS = 8
Q_PER_KV = 4
D_ATTN = 512
D_KV = 128
D_GM = 512
N_GROUPS = 8
D_MIX = D_ATTN + D_GM
BLK = 128
LANES = 128
SUBLANES = 8
N_PAIRS = D_ATTN // LANES
D_QKV = D_ATTN + 2 * D_KV
D_REST = D_ATTN + 3 * D_GM
D_IN = D_QKV + D_REST
EPS = 1e-6
NEG_INF = -1e30
ALIBI_SLOPES = tuple(2.0 ** (-8.0 * (h + 1) / N_HEADS) for h in range(N_HEADS))
Q_SCALE = 1.0 / 8.0

ADAM_LR = 0.001
ADAM_B1 = 0.9
ADAM_B2 = 0.999
ADAM_EPS = 1e-08
ADAM_WD = 0.01
ADAM_STEP = 10

TOKEN_TILE = 512
VMEM_LIMIT_BYTES = 56 * 1024 * 1024


def _params(semantics=None):
    return pltpu.CompilerParams(dimension_semantics=semantics, vmem_limit_bytes=VMEM_LIMIT_BYTES)


def _dot(a, b):
    return jnp.dot(a, b, preferred_element_type=F32)


def _dot_nt(a, b):
    return lax.dot_general(a, b, (((1,), (1,)), ((), ())), preferred_element_type=F32)


def _dot_tn(a, b):
    return lax.dot_general(a, b, (((0,), (0,)), ((), ())), preferred_element_type=F32)


def _mx(v):
    return v.astype(MXU_DTYPE)


def _lane_lo(rows):
    return lax.broadcasted_iota(jnp.int32, (rows, LANES), 1) < HEAD_DIM


def _half_ones(width=LANES):
    group_bits = HEAD_DIM.bit_length() - 1
    r = jnp.right_shift(lax.broadcasted_iota(jnp.int32, (width, width), 0), group_bits)
    c = jnp.right_shift(lax.broadcasted_iota(jnp.int32, (width, width), 1), group_bits)
    return jnp.where(r == c, 1.0, 0.0).astype(jnp.bfloat16)


WIDE = 2 * LANES


def _half_sum(v, ones):
    p1 = v.astype(jnp.bfloat16)
    p2 = (v - p1.astype(F32)).astype(jnp.bfloat16)
    return _dot(p1, ones) + _dot(p2, ones)


def _half_rms(v, ones):
    r = lax.rsqrt(_half_sum(v * v, ones) * (1.0 / HEAD_DIM) + EPS)
    return v * r, r


def _half_rms_bwd(dy, vhat, r, ones):
    return r * (dy - vhat * (_half_sum(vhat * dy, ones) * (1.0 / HEAD_DIM)))


def _group_rows(v):
    rows, n = v.shape
    return v.reshape(rows // SUBLANES, SUBLANES, n).sum(axis=0)


def _sigmoid(v):
    return 1.0 / (1.0 + jnp.exp(-v))


ROW_CHUNK = 32
VARIANT_HEADS = ((0, 2, 5, 7), (1, 3, 4, 6))
HEAD_SLOT = {h: (v, s) for v, heads in enumerate(VARIANT_HEADS) for s, h in enumerate(heads)}
STACK = Q_PER_KV * BLK


def _fill_attn_bias(bias_s):
    qi = lax.broadcasted_iota(jnp.int32, (BLK, 3 * BLK), 0)
    ci = lax.broadcasted_iota(jnp.int32, (BLK, 3 * BLK), 1)
    dist = jnp.abs(ci - BLK - qi)
    distf = dist.astype(F32)
    window = dist <= BLK
    for kind, seen in enumerate((window & (ci >= BLK), window, window & (ci < 2 * BLK))):
        for h in range(N_HEADS):
            bias_s[kind, h] = jnp.where(seen, -(ALIBI_SLOPES[h] * distf), NEG_INF)


def _block_kind(block, seq):
    assert seq >= 2 * BLK
    return jnp.where(block == 0, 0, jnp.where(block == seq // BLK - 1, 2, 1))


def _stage_queries(qn, lo_t, j, nb, qs):
    for a in range(2):
        v, slot = HEAD_SLOT[2 * j + a]
        qm = _mx(jnp.where(lo_t, qn, 0.0) if a == 0 else jnp.where(lo_t, 0.0, qn))
        for n in range(nb):
            qs[n, v, slot * BLK : (slot + 1) * BLK, :] = qm[n * BLK : (n + 1) * BLK]


def _unstack_pair(stacked, j, lo):
    (v0, s0), (v1, s1) = HEAD_SLOT[2 * j], HEAD_SLOT[2 * j + 1]
    return jnp.where(lo, stacked[v0][s0 * BLK : (s0 + 1) * BLK], stacked[v1][s1 * BLK : (s1 + 1) * BLK])


def _stage_keys(kvp_ref, qkv_ref, kvn_ref, kg, ones, tile, ks, kr, vs, vr, khat_s=None, rk_s=None):
    pieces = (
        (0, BLK, kvp_ref[:, 0:D_KV], kvp_ref[:, D_KV : 2 * D_KV]),
        (BLK, tile, qkv_ref[:, D_ATTN : D_ATTN + D_KV], qkv_ref[:, D_ATTN + D_KV : D_QKV]),
        (BLK + tile, BLK, kvn_ref[:, 0:D_KV], kvn_ref[:, D_KV : 2 * D_KV]),
    )
    for r0, n, k, v in pieces:
        khat, rk = _half_rms(k, ones)
        kn = khat * kg
        ks[r0 : r0 + n, :] = _mx(kn)
        kr[r0 : r0 + n, :] = _mx(pltpu.roll(kn, HEAD_DIM, 1))
        vs[r0 : r0 + n, :] = _mx(v)
        vr[r0 : r0 + n, :] = _mx(pltpu.roll(v, HEAD_DIM, 1))
        if khat_s is not None:
            khat_s[r0 : r0 + n, :] = khat
            rk_s[r0 : r0 + n, :] = rk


def _halo_specs(tile, seq):
    nb = tile // BLK
    last = seq // BLK - 1
    kv_col = D_ATTN // (2 * D_KV)
    prev = pl.BlockSpec((BLK, 2 * D_KV), lambda i: (jnp.maximum(i * nb - 1, 0), kv_col))
    nxt = pl.BlockSpec((BLK, 2 * D_KV), lambda i: (jnp.minimum((i + 1) * nb, last), kv_col))
    return prev, nxt


def _row_spec(tile, width):
    return pl.BlockSpec((tile, width), lambda i: (i, 0))


def _full_spec(shape):
    nd = len(shape)
    return pl.BlockSpec(shape, lambda i: (0,) * nd)


SMEM_SPEC = pl.BlockSpec(memory_space=pltpu.SMEM)
VMEM_SPEC = pl.BlockSpec(memory_space=pltpu.VMEM)
HBM_SPEC = pl.BlockSpec(memory_space=pltpu.HBM)


def _rider_steps(nt):
    return 0, (3 * nt) // 4, nt - 1


def _gather_rider(sources, gathered, sems, step, nt):
    start, forward, finish = _all_gather_stages(
        [_row_block(g, s.shape[0]) for g, s in zip(gathered, sources)], sems[0], sems[1], sources=sources, local_sems=sems[2]
    )
    at_start, at_forward, at_finish = _rider_steps(nt)
    pl.when(step == at_start)(start)

    def after_compute():
        pl.when(step == at_forward)(forward)
        pl.when(step == at_finish)(finish)

    return after_compute


def _gathered_shapes(gather):
    return [jax.ShapeDtypeStruct((N_DEV * g.shape[0], g.shape[1]), g.dtype) for g in gather]


def _ln_proj_fwd(x, gain, scale1, shift, w_in_t, name):
    seq, d = x.shape
    tile = min(TOKEN_TILE, seq)

    def body(x_ref, g_ref, s1_ref, sh_ref, wt_ref, pa_ref, pb_ref):
        xv = x_ref[...]
        r = lax.rsqrt(jnp.mean(xv * xv, axis=-1, keepdims=True) + EPS)
        h = _mx((xv * r) * g_ref[...] * s1_ref[...] + sh_ref[...])
        pa_ref[...] = _dot_nt(h, wt_ref[0:D_QKV, :])
        pb_ref[...] = _dot_nt(h, wt_ref[D_QKV:D_IN, :])

    vec = _full_spec((1, d))
    return pl.pallas_call(
        body,
        name=name,
        grid=(seq // tile,),
        in_specs=[_row_spec(tile, d), vec, vec, vec, _full_spec((D_IN, d))],
        out_specs=[_row_spec(tile, D_QKV), _row_spec(tile, D_REST)],
        out_shape=[jax.ShapeDtypeStruct((seq, D_QKV), F32), jax.ShapeDtypeStruct((seq, D_REST), F32)],
        compiler_params=_params(("parallel",)),
    )(x, gain, scale1, shift, w_in_t)


def _attn_fwd(pa, q_gain2, k_gain2, sink, name, gather=()):
    seq = pa.shape[0]
    tile = min(TOKEN_TILE, seq)
    nb = tile // BLK
    nt = seq // tile
    ext = tile + 2 * BLK
    n_ride = len(gather)
    riding = n_ride > 0

    def body(sink_ref, qkv_ref, kvp_ref, kvn_ref, qg_ref, kg_ref, *rest):
        i = pl.program_id(0)
        sources, (o_ref, p_ref, psink_ref), gathered = rest[:n_ride], rest[n_ride : n_ride + 3], rest[n_ride + 3 : 2 * n_ride + 3]
        qs, ks, kr, vs, vr, bias_s, s_scr, *sems = rest[2 * n_ride + 3 :]
        if riding:
            after_compute = _gather_rider(sources, gathered, sems, i, nt)

        @pl.when(i == 0)
        def _():
            _fill_attn_bias(bias_s)

        ones = _half_ones()
        lo = _lane_lo(BLK)
        lo_t = _lane_lo(tile)
        head_lane = lax.broadcasted_iota(jnp.int32, (ROW_CHUNK, LANES), 1)
        _stage_keys(kvp_ref, qkv_ref, kvn_ref, kg_ref[...], ones, tile, ks, kr, vs, vr)
        for j in range(N_PAIRS):
            qhat, _ = _half_rms(qkv_ref[:, j * LANES : (j + 1) * LANES], ones)
            _stage_queries(qhat * (qg_ref[...] * Q_SCALE), lo_t, j, nb, qs)

        def block(n, carry):
            r0 = pl.multiple_of(n * BLK, BLK)
            krows = pl.ds(r0, 3 * BLK)
            kind = _block_kind(i * nb + n, seq)
            for v in range(2):
                s_scr[v] = _dot_nt(qs[n, v], (kr if v else ks)[krows, :])
            for rc in range(0, BLK, ROW_CHUNK):
                p_sink = jnp.zeros((ROW_CHUNK, LANES), F32)
                for h in range(N_HEADS):
                    v, slot = HEAD_SLOT[h]
                    sink_h = sink_ref[h]
                    rows = slice(slot * BLK + rc, slot * BLK + rc + ROW_CHUNK)
                    s = s_scr[v, rows, :] + bias_s[kind, h, rc : rc + ROW_CHUNK, :]
                    m = jnp.maximum(jnp.max(s, axis=-1, keepdims=True), sink_h)
                    p = jnp.exp(s - m)
                    e_sink = jnp.exp(sink_h - m)
                    inv = 1.0 / (jnp.sum(p, axis=-1, keepdims=True) + e_sink)
                    p_ref[n, v, rows, :] = _mx(p * inv)
                    p_sink = jnp.where(head_lane == h, e_sink * inv, p_sink)
                psink_ref[pl.ds(pl.multiple_of(r0 + rc, ROW_CHUNK), ROW_CHUNK), :] = p_sink
            outs = [_dot(p_ref[n, v], (vr if v else vs)[krows, :]) for v in range(2)]
            for j in range(N_PAIRS):
                o_ref[pl.ds(r0, BLK), j * LANES : (j + 1) * LANES] = _unstack_pair(outs, j, lo)
            return carry

        lax.fori_loop(0, nb, block, 0)
        if riding:
            after_compute()

    prev, nxt = _halo_specs(tile, seq)
    vec = _full_spec((1, LANES))
    in_specs = [SMEM_SPEC, _row_spec(tile, D_QKV), prev, nxt, vec, vec]
    out_specs = [
        _row_spec(tile, D_ATTN),
        pl.BlockSpec((nb, 2, STACK, 3 * BLK), lambda i: (i, 0, 0, 0)),
        _row_spec(tile, LANES),
    ]
    out_shape = [
        jax.ShapeDtypeStruct((seq, D_ATTN), F32),
        jax.ShapeDtypeStruct((seq // BLK, 2, STACK, 3 * BLK), MXU_DTYPE),
        jax.ShapeDtypeStruct((seq, LANES), F32),
    ]
    scratch = [
        pltpu.VMEM((nb, 2, STACK, LANES), MXU_DTYPE),
        pltpu.VMEM((ext, LANES), MXU_DTYPE),
        pltpu.VMEM((ext, LANES), MXU_DTYPE),
        pltpu.VMEM((ext, LANES), MXU_DTYPE),
        pltpu.VMEM((ext, LANES), MXU_DTYPE),
        pltpu.VMEM((3, N_HEADS, BLK, 3 * BLK), F32),
        pltpu.VMEM((2, STACK, 3 * BLK), F32),
    ]
    return pl.pallas_call(
        body,
        name=name,
        grid=(nt,),
        in_specs=in_specs + [HBM_SPEC] * n_ride,
        out_specs=out_specs + [HBM_SPEC] * n_ride,
        out_shape=out_shape + _gathered_shapes(gather),
        scratch_shapes=scratch + _rider_sems(n_ride),
        compiler_params=_params(("arbitrary",)),
    )(sink, pa, pa, pa, q_gain2, k_gain2, *gather)


def _mix_out_fwd(pb, o, x, gate, w_out, w_s, b_st, name, target=None):
    seq, d = x.shape
    tile = min(TOKEN_TILE, seq)
    nb = tile // BLK
    with_loss = target is not None

    def body(pb_ref, o_ref, x_ref, gate_ref, wo_ref, ws_ref, bs_ref, *rest):
        if with_loss:
            t_ref, xo_ref, acc_ref, sv_ref, y_s, vn_s = rest

            @pl.when(pl.program_id(0) == 0)
            def _():
                acc_ref[...] = jnp.zeros_like(acc_ref)
        else:
            xo_ref, sv_ref, y_s, vn_s = rest
        ones = _half_ones(WIDE)
        lo = _lane_lo(BLK)
        ga = pb_ref[:, 0:D_ATTN]
        y_s[:, 0:D_ATTN] = _mx(o_ref[...] * (ga * _sigmoid(ga)))
        for j in range(D_GM // WIDE):
            vhat, _ = _half_rms(pb_ref[:, 2 * D_GM + j * WIDE : 2 * D_GM + (j + 1) * WIDE], ones)
            vn_s[:, j * WIDE : (j + 1) * WIDE] = _mx(vhat)

        def chunk(n, carry):
            rows = pl.ds(pl.multiple_of(n * BLK, BLK), BLK)
            for j in range(N_PAIRS):
                cols = slice(j * LANES, (j + 1) * LANES)
                vn = vn_s[rows, cols]
                sv = jnp.where(lo, _dot(ws_ref[2 * j], vn), _dot(ws_ref[2 * j + 1], vn)) + bs_ref[:, cols]
                sv_ref[rows, cols] = sv
                u = pb_ref[rows, D_ATTN + j * LANES : D_ATTN + (j + 1) * LANES]
                gg = pb_ref[rows, D_ATTN + 2 * D_GM + j * LANES : D_ATTN + 2 * D_GM + (j + 1) * LANES]
                y_s[rows, D_ATTN + j * LANES : D_ATTN + (j + 1) * LANES] = _mx((u * sv) * (gg * _sigmoid(gg)))
            return carry

        lax.fori_loop(0, nb, chunk, 0)
        y = x_ref[...] + gate_ref[...] * _dot(y_s[...], wo_ref[...])
        if with_loss:
            e = y - t_ref[...]
            xo_ref[...] = e * (1.0 / d)
            acc_ref[...] += jnp.sum(jnp.sum(e * e, axis=-1, keepdims=True), axis=0, keepdims=True)
        else:
            xo_ref[...] = y

    row = _row_spec(tile, d)
    acc_shape = (SUBLANES, LANES)
    return pl.pallas_call(
        body,
        name=name,
        grid=(seq // tile,),
        in_specs=[
            _row_spec(tile, D_REST),
            _row_spec(tile, D_ATTN),
            row,
            _full_spec((1, d)),
            _full_spec((D_MIX, d)),
            _full_spec((N_GROUPS, BLK, BLK)),
            _full_spec((BLK, D_GM)),
        ]
        + ([row] if with_loss else []),
        out_specs=[row] + ([_full_spec(acc_shape)] if with_loss else []) + [_row_spec(tile, D_GM)],
        out_shape=[jax.ShapeDtypeStruct((seq, d), F32)]
        + ([jax.ShapeDtypeStruct(acc_shape, F32)] if with_loss else [])
        + [jax.ShapeDtypeStruct((seq, D_GM), F32)],
        scratch_shapes=[pltpu.VMEM((tile, D_MIX), MXU_DTYPE), pltpu.VMEM((tile, D_GM), MXU_DTYPE)],
        compiler_params=_params(("arbitrary",) if with_loss else ("parallel",)),
    )(pb, o, x, gate, w_out, w_s, b_st, *([target] if with_loss else []))


def _mix_out_bwd(dxn, pb, o, sv, gate, w_out, w_s_t, name):
    seq, d = dxn.shape
    tile = min(TOKEN_TILE, seq)
    nb = tile // BLK
    nt = seq // tile

    def body(dxn_ref, pb_ref, o_ref, sv_ref, gate_ref, wo_ref, wst_ref,
             dpb_ref, do_ref, dwo_ref, dgate_ref, dws_ref, dbs_ref, g_ref, y_s, dy_s, vn_s, rv_s, vnb_s, dsv_s, dvn_s):
        @pl.when(pl.program_id(0) == 0)
        def _():
            g_ref[...] = jnp.zeros_like(g_ref)
            dws_ref[...] = jnp.zeros_like(dws_ref)
            dbs_ref[...] = jnp.zeros_like(dbs_ref)

        ones = _half_ones(WIDE)
        lo = _lane_lo(BLK)
        c_u = slice(D_ATTN, D_ATTN + D_GM)
        c_vg = slice(D_ATTN + D_GM, D_ATTN + 2 * D_GM)
        c_gg = slice(D_ATTN + 2 * D_GM, D_REST)
        dxv = dxn_ref[...]
        dy_s[...] = _dot_nt(_mx(dxv * gate_ref[...]), wo_ref[...])
        ga = pb_ref[:, 0:D_ATTN]
        sig = _sigmoid(ga)
        sil = ga * sig
        ov = o_ref[...]
        y_s[:, 0:D_ATTN] = _mx(ov * sil)
        da = dy_s[:, 0:D_ATTN]
        do_ref[...] = da * sil
        dpb_ref[:, 0:D_ATTN] = (da * ov * (sig * (1.0 + ga * (1.0 - sig)))).astype(dpb_ref.dtype)
        for j in range(D_GM // WIDE):
            cols = slice(j * WIDE, (j + 1) * WIDE)
            vhat, rv = _half_rms(pb_ref[:, 2 * D_GM + j * WIDE : 2 * D_GM + (j + 1) * WIDE], ones)
            vn_s[:, cols] = vhat
            rv_s[:, cols] = rv
            vnb_s[:, cols] = _mx(vhat)

        def gating(n, carry):
            rows = pl.ds(pl.multiple_of(n * BLK, BLK), BLK)
            sv = sv_ref[rows, :]
            u = pb_ref[rows, c_u]
            gg = pb_ref[rows, c_gg]
            sg = _sigmoid(gg)
            silg = gg * sg
            m0 = u * sv
            y_s[rows, D_ATTN:D_MIX] = _mx(m0 * silg)
            dm = dy_s[rows, D_ATTN:D_MIX]
            dm0 = dm * silg
            dpb_ref[rows, c_gg] = (dm * m0 * (sg * (1.0 + gg * (1.0 - sg)))).astype(dpb_ref.dtype)
            dpb_ref[rows, c_u] = (dm0 * sv).astype(dpb_ref.dtype)
            dsv = dm0 * u
            dsv_s[rows, :] = _mx(dsv)
            dbs_ref[...] += dsv
            return carry

        lax.fori_loop(0, nb, gating, 0)

        def spatial_bwd(n, carry):
            rows = pl.ds(pl.multiple_of(n * BLK, BLK), BLK)
            for j in range(N_PAIRS):
                cols = slice(j * LANES, (j + 1) * LANES)
                dsv = dsv_s[rows, cols]
                dvn_s[rows, cols] = jnp.where(lo, _dot(wst_ref[2 * j], dsv), _dot(wst_ref[2 * j + 1], dsv))
            return carry

        lax.fori_loop(0, nb, spatial_bwd, 0)
        zero = jnp.zeros((BLK, LANES), MXU_DTYPE)
        for j in range(N_PAIRS):
            cols = slice(j * LANES, (j + 1) * LANES)
            chunks = [dsv_s[n * BLK : (n + 1) * BLK, cols] for n in range(nb)]
            vn_all = jnp.concatenate([vnb_s[n * BLK : (n + 1) * BLK, cols] for n in range(nb)], axis=1)
            dws_ref[2 * j] += _dot_nt(jnp.concatenate([jnp.where(lo, c, zero) for c in chunks], axis=1), vn_all)
            dws_ref[2 * j + 1] += _dot_nt(jnp.concatenate([jnp.where(lo, zero, c) for c in chunks], axis=1), vn_all)
        for j in range(D_GM // WIDE):
            cols = slice(j * WIDE, (j + 1) * WIDE)
            dpb_ref[:, D_ATTN + D_GM + j * WIDE : D_ATTN + D_GM + (j + 1) * WIDE] = _half_rms_bwd(
                dvn_s[:, cols], vn_s[:, cols], rv_s[:, cols], ones
            ).astype(dpb_ref.dtype)
        g_ref[...] += _dot_tn(y_s[...], _mx(dxv))

        @pl.when(pl.program_id(0) == nt - 1)
        def _():
            gv = g_ref[...]
            dwo_ref[...] = (gv * gate_ref[...]).astype(dwo_ref.dtype)
            dgate_ref[...] = _group_rows(gv * wo_ref[...].astype(F32))

    return pl.pallas_call(
        body,
        name=name,
        grid=(seq // tile,),
        in_specs=[
            _row_spec(tile, d),
            _row_spec(tile, D_REST),
            _row_spec(tile, D_ATTN),
            _row_spec(tile, D_GM),
            _full_spec((1, d)),
            _full_spec((D_MIX, d)),
            _full_spec((N_GROUPS, BLK, BLK)),
        ],
        out_specs=[
            _row_spec(tile, D_REST),
            _row_spec(tile, D_ATTN),
            _full_spec((D_MIX, d)),
            _full_spec((SUBLANES, d)),
            _full_spec((N_GROUPS, BLK, BLK)),
            _full_spec((BLK, D_GM)),
        ],
        out_shape=[
            jax.ShapeDtypeStruct((seq, D_REST), MXU_DTYPE),
            jax.ShapeDtypeStruct((seq, D_ATTN), F32),
            jax.ShapeDtypeStruct((D_MIX, d), jnp.bfloat16),
            jax.ShapeDtypeStruct((SUBLANES, d), F32),
            jax.ShapeDtypeStruct((N_GROUPS, BLK, BLK), F32),
            jax.ShapeDtypeStruct((BLK, D_GM), F32),
        ],
        scratch_shapes=[
            pltpu.VMEM((D_MIX, d), F32),
            pltpu.VMEM((tile, D_MIX), MXU_DTYPE),
            pltpu.VMEM((tile, D_MIX), F32),
            pltpu.VMEM((tile, D_GM), F32),
            pltpu.VMEM((tile, D_GM), F32),
            pltpu.VMEM((tile, D_GM), MXU_DTYPE),
            pltpu.VMEM((tile, D_GM), MXU_DTYPE),
            pltpu.VMEM((tile, D_GM), F32),
        ],
        compiler_params=_params(("arbitrary",)),
    )(dxn, pb, o, sv, gate, w_out, w_s_t)


def _attn_bwd(pa, o, do, probs, p_sink, q_gain2, k_gain2, name, scatter=()):
    seq = pa.shape[0]
    tile = min(TOKEN_TILE, seq)
    nb = tile // BLK
    nt = seq // tile
    ext = tile + 2 * BLK
    n_ride = len(scatter)
    riding = n_ride > 0

    def body(qkv_ref, kvp_ref, kvn_ref, o_ref, do_ref, p_ref, psink_ref, qg_ref, kg_ref, *rest):
        i = pl.program_id(0)
        blocks, rest = rest[:n_ride], rest[n_ride:]
        dq_ref, dkv_ref, hp_ref, hn_ref, dqg_ref, dkg_ref, dsk_ref = rest[:7]
        landing, rest = rest[7 : 7 + n_ride], rest[7 + n_ride :]
        (qs, dos, qhat_s, rq_s, ks, kr, vs, vr, khat_s, rk_s, dqn_s, dka, dva, dp_scr, ds_scr) = rest[:15]
        if riding:
            start, finish = _scatter_stages(blocks, landing, *rest[15:])
            at_start, _, at_finish = _rider_steps(nt)
            pl.when(i == at_start)(start)

        @pl.when(i == 0)
        def _():
            dqg_ref[...] = jnp.zeros_like(dqg_ref)
            dkg_ref[...] = jnp.zeros_like(dkg_ref)
            dsk_ref[...] = jnp.zeros_like(dsk_ref)

        ones = _half_ones()
        lo = _lane_lo(BLK)
        lo_t = _lane_lo(tile)
        lo_c = _lane_lo(ROW_CHUNK)
        qg = qg_ref[...] * Q_SCALE
        kg = kg_ref[...]
        _stage_keys(kvp_ref, qkv_ref, kvn_ref, kg, ones, tile, ks, kr, vs, vr, khat_s, rk_s)
        head_lane = lax.broadcasted_iota(jnp.int32, (tile, LANES), 1)
        d_rows = jnp.zeros((tile, LANES), F32)
        for j in range(N_PAIRS):
            cols = slice(j * LANES, (j + 1) * LANES)
            qhat, rq = _half_rms(qkv_ref[:, cols], ones)
            qhat_s[:, cols] = qhat
            rq_s[:, cols] = rq
            _stage_queries(qhat * qg, lo_t, j, nb, qs)
            dov = do_ref[:, cols]
            _stage_queries(dov, lo_t, j, nb, dos)
            d_pair = _half_sum(dov * o_ref[:, cols], ones)
            d_rows = jnp.where(head_lane == 2 * j, d_pair, d_rows)
            d_rows = jnp.where(head_lane == 2 * j + 1, pltpu.roll(d_pair, HEAD_DIM, 1), d_rows)
        dsk_ref[...] -= _group_rows(psink_ref[...] * d_rows)
        dka[...] = jnp.zeros_like(dka)
        dva[...] = jnp.zeros_like(dva)

        def block(n, carry):
            r0 = pl.multiple_of(n * BLK, BLK)
            krows = pl.ds(r0, 3 * BLK)
            for v in range(2):
                dp_scr[v] = _dot_nt(dos[n, v], (vr if v else vs)[krows, :])
            for h in range(N_HEADS):
                v, slot = HEAD_SLOT[h]
                j, a = divmod(h, 2)
                cols = slice(j * LANES, (j + 1) * LANES)
                for rc in range(0, BLK, ROW_CHUNK):
                    rows = slice(slot * BLK + rc, slot * BLK + rc + ROW_CHUNK)
                    trows = pl.ds(pl.multiple_of(r0 + rc, ROW_CHUNK), ROW_CHUNK)
                    prod = do_ref[trows, cols] * o_ref[trows, cols]
                    prod = jnp.where(lo_c, prod, 0.0) if a == 0 else jnp.where(lo_c, 0.0, prod)
                    dcol = jnp.sum(prod, axis=-1, keepdims=True)
                    ds_scr[v, rows, :] = _mx(p_ref[n, v, rows, :].astype(F32) * (dp_scr[v, rows, :] - dcol))
            dqv = []
            for v in range(2):
                dqv.append(_dot(ds_scr[v], (kr if v else ks)[krows, :]))
                dka[v, krows, :] += _dot_tn(ds_scr[v], qs[n, v])
                dva[v, krows, :] += _dot_tn(p_ref[n, v], dos[n, v])
            for j in range(N_PAIRS):
                dqn_s[pl.ds(r0, BLK), j * LANES : (j + 1) * LANES] = _unstack_pair(dqv, j, lo)
            return carry

        lax.fori_loop(0, nb, block, 0)
        for j in range(N_PAIRS):
            cols = slice(j * LANES, (j + 1) * LANES)
            dqn = dqn_s[:, cols]
            qhat = qhat_s[:, cols]
            dqg_ref[:, cols] += _group_rows(dqn * qhat) * Q_SCALE
            dq_ref[:, cols] = _half_rms_bwd(dqn * qg, qhat, rq_s[:, cols], ones).astype(dq_ref.dtype)
        dkn = dka[0] + pltpu.roll(dka[1], HEAD_DIM, 1)
        khat = khat_s[...]
        dkg_ref[...] += _group_rows(dkn * khat)
        dk = _half_rms_bwd(dkn * kg, khat, rk_s[...], ones)
        dv = dva[0] + pltpu.roll(dva[1], HEAD_DIM, 1)
        hp_ref[:, 0:D_KV] = dk[0:BLK]
        hp_ref[:, D_KV : 2 * D_KV] = dv[0:BLK]
        dkv_ref[:, 0:D_KV] = dk[BLK : BLK + tile]
        dkv_ref[:, D_KV : 2 * D_KV] = dv[BLK : BLK + tile]
        hn_ref[:, 0:D_KV] = dk[BLK + tile : ext]
        hn_ref[:, D_KV : 2 * D_KV] = dv[BLK + tile : ext]
        if riding:
            pl.when(i == at_finish)(finish)

    prev, nxt = _halo_specs(tile, seq)
    vec = _full_spec((1, LANES))
    halo = pl.BlockSpec((None, BLK, 2 * D_KV), lambda i: (i, 0, 0))
    return pl.pallas_call(
        body,
        name=name,
        grid=(nt,),
        in_specs=[
            _row_spec(tile, D_QKV),
            prev,
            nxt,
            _row_spec(tile, D_ATTN),
            _row_spec(tile, D_ATTN),
            pl.BlockSpec((nb, 2, STACK, 3 * BLK), lambda i: (i, 0, 0, 0)),
            _row_spec(tile, LANES),
            vec,
            vec,
        ]
        + [HBM_SPEC] * n_ride,
        out_specs=[
            _row_spec(tile, D_ATTN),
            _row_spec(tile, 2 * D_KV),
            halo,
            halo,
            _full_spec((SUBLANES, D_ATTN)),
            _full_spec((SUBLANES, LANES)),
            _full_spec((SUBLANES, LANES)),
        ]
        + [HBM_SPEC] * n_ride,
        out_shape=[
            jax.ShapeDtypeStruct((seq, D_ATTN), MXU_DTYPE),
            jax.ShapeDtypeStruct((seq, 2 * D_KV), F32),
            jax.ShapeDtypeStruct((nt, BLK, 2 * D_KV), F32),
            jax.ShapeDtypeStruct((nt, BLK, 2 * D_KV), F32),
            jax.ShapeDtypeStruct((SUBLANES, D_ATTN), F32),
            jax.ShapeDtypeStruct((SUBLANES, LANES), F32),
            jax.ShapeDtypeStruct((SUBLANES, LANES), F32),
        ]
        + _landing_shapes(scatter),
        scratch_shapes=[
            pltpu.VMEM((nb, 2, STACK, LANES), MXU_DTYPE),
            pltpu.VMEM((nb, 2, STACK, LANES), MXU_DTYPE),
            pltpu.VMEM((tile, D_ATTN), F32),
            pltpu.VMEM((tile, D_ATTN), F32),
            pltpu.VMEM((ext, LANES), MXU_DTYPE),
            pltpu.VMEM((ext, LANES), MXU_DTYPE),
            pltpu.VMEM((ext, LANES), MXU_DTYPE),
            pltpu.VMEM((ext, LANES), MXU_DTYPE),
            pltpu.VMEM((ext, LANES), F32),
            pltpu.VMEM((ext, LANES), F32),
            pltpu.VMEM((tile, D_ATTN), F32),
            pltpu.VMEM((2, ext, LANES), F32),
            pltpu.VMEM((2, ext, LANES), F32),
            pltpu.VMEM((2, STACK, 3 * BLK), F32),
            pltpu.VMEM((2, STACK, 3 * BLK), MXU_DTYPE),
        ]
        + _rider_sems(n_ride),
        compiler_params=_params(("arbitrary",)),
    )(pa, pa, pa, o, do, probs, p_sink, q_gain2, k_gain2, *scatter)


def _halo_in_specs(tile, nt):
    from_prev = pl.BlockSpec((None, BLK, 2 * D_KV), lambda i: (jnp.maximum(i - 1, 0), 0, 0))
    from_next = pl.BlockSpec((None, BLK, 2 * D_KV), lambda i: (jnp.minimum(i + 1, nt - 1), 0, 0))
    return from_prev, from_next


def _landing_shapes(scatter):
    return [jax.ShapeDtypeStruct((N_DEV,) + b.shape[2:], b.dtype) for b in scatter]


def _rider_sems(n_ride):
    if not n_ride:
        return []
    return [pltpu.SemaphoreType.DMA((7 * n_ride,)), pltpu.SemaphoreType.DMA((7 * n_ride,)), pltpu.SemaphoreType.DMA((n_ride,))]


def _proj_bwd_dx(x, dxn, dq, dkvb, dpb, w_in_t, gain, scale1, name):
    seq, d = x.shape
    tile = min(TOKEN_TILE, seq)

    def row(width):
        return _row_spec(tile, width)

    def body(x_ref, dxn_ref, dq_ref, dkvb_ref, dpb_ref, wt_ref, g_ref, s1_ref, dx_ref, c0_ref, c1_ref):
        @pl.when(pl.program_id(0) == 0)
        def _():
            c0_ref[...] = jnp.zeros_like(c0_ref)
            c1_ref[...] = jnp.zeros_like(c1_ref)

        dh = (
            _dot(dq_ref[...], wt_ref[0:D_ATTN, :])
            + _dot(dkvb_ref[...], wt_ref[D_ATTN:D_QKV, :])
            + _dot(dpb_ref[...], wt_ref[D_QKV:D_IN, :])
        )
        xv = x_ref[...]
        r = lax.rsqrt(jnp.mean(xv * xv, axis=-1, keepdims=True) + EPS)
        xn = xv * r
        c0_ref[...] += _group_rows(dh)
        c1_ref[...] += _group_rows(dh * xn)
        dxn_ = dh * (g_ref[...] * s1_ref[...])
        dx_ref[...] = dxn_ref[...] + r * (dxn_ - xn * jnp.mean(xn * dxn_, axis=-1, keepdims=True))

    vec = _full_spec((1, d))
    return pl.pallas_call(
        body,
        name=name,
        grid=(seq // tile,),
        in_specs=[row(d), row(d), row(D_ATTN), row(2 * D_KV), row(D_REST), _full_spec((D_IN, d)), vec, vec],
        out_specs=[row(d), _full_spec((SUBLANES, d)), _full_spec((SUBLANES, d))],
        out_shape=[
            jax.ShapeDtypeStruct((seq, d), F32),
            jax.ShapeDtypeStruct((SUBLANES, d), F32),
            jax.ShapeDtypeStruct((SUBLANES, d), F32),
        ],
        compiler_params=_params(("arbitrary",)),
    )(x, dxn, dq, dkvb, dpb, w_in_t, gain, scale1)


def _proj_bwd_dw(x, gain, scale1, shift, dq, dkv, halo_prev, halo_next, dpb, name, gather=(), scatter=()):
    seq, d = x.shape
    tile = min(TOKEN_TILE, seq)
    nt = seq // tile
    assert tile >= 2 * BLK
    n_ride = len(gather)
    n_scatter = len(scatter)

    def body(x_ref, g_ref, s1_ref, sh_ref, dq_ref, dkv_ref, hn_ref, hp_ref, dpb_ref, *rest):
        i = pl.program_id(0)
        sources, blocks, rest = rest[:n_ride], rest[n_ride : n_ride + n_scatter], rest[n_ride + n_scatter :]
        dw_ref, dkvb_ref = rest[:2]
        gathered, landing = rest[2 : 2 + n_ride], rest[2 + n_ride : 2 + n_ride + n_scatter]
        acc, *sems = rest[2 + n_ride + n_scatter :]
        after_compute = _gather_rider(sources, gathered, sems[:3], i, nt) if n_ride else None
        if n_scatter:
            start, finish = _scatter_stages(blocks, landing, *sems[3 if n_ride else 0 :])
            at_start, _, at_finish = _rider_steps(nt)
            pl.when(i == at_start)(start)

        @pl.when(i == 0)
        def _():
            acc[...] = jnp.zeros_like(acc)

        top = dkv_ref[0:BLK, :] + jnp.where(i > 0, hn_ref[...], 0.0)
        bot = dkv_ref[tile - BLK : tile, :] + jnp.where(i < nt - 1, hp_ref[...], 0.0)
        dkvb_ref[0:BLK, :] = top.astype(dkvb_ref.dtype)
        dkvb_ref[tile - BLK : tile, :] = bot.astype(dkvb_ref.dtype)
        if tile > 2 * BLK:
            dkvb_ref[BLK : tile - BLK, :] = dkv_ref[BLK : tile - BLK, :].astype(dkvb_ref.dtype)
        xv = x_ref[...]
        r = lax.rsqrt(jnp.mean(xv * xv, axis=-1, keepdims=True) + EPS)
        h = _mx((xv * r) * g_ref[...] * s1_ref[...] + sh_ref[...])
        acc[0:D_ATTN, :] += _dot_tn(dq_ref[...], h)
        acc[D_ATTN:D_QKV, :] += _dot_tn(dkvb_ref[...], h)
        acc[D_QKV:D_IN, :] += _dot_tn(dpb_ref[...], h)

        @pl.when(i == nt - 1)
        def _():
            dw_ref[...] = acc[...].astype(dw_ref.dtype)

        if n_ride:
            after_compute()
        if n_scatter:
            pl.when(i == at_finish)(finish)

    from_prev, from_next = _halo_in_specs(tile, nt)
    vec = _full_spec((1, d))
    return pl.pallas_call(
        body,
        name=name,
        grid=(nt,),
        in_specs=[
            _row_spec(tile, d),
            vec,
            vec,
            vec,
            _row_spec(tile, D_ATTN),
            _row_spec(tile, 2 * D_KV),
            from_prev,
            from_next,
            _row_spec(tile, D_REST),
        ]
        + [HBM_SPEC] * (n_ride + n_scatter),
        out_specs=[_full_spec((D_IN, d)), _row_spec(tile, 2 * D_KV)] + [HBM_SPEC] * (n_ride + n_scatter),
        out_shape=[jax.ShapeDtypeStruct((D_IN, d), jnp.bfloat16), jax.ShapeDtypeStruct((seq, 2 * D_KV), MXU_DTYPE)]
        + _gathered_shapes(gather)
        + _landing_shapes(scatter),
        scratch_shapes=[pltpu.VMEM((D_IN, d), F32)] + _rider_sems(n_ride) + _rider_sems(n_scatter),
        compiler_params=_params(("arbitrary",)),
    )(x, gain, scale1, shift, dq, dkv, halo_next, halo_prev, dpb, *gather, *scatter)


def _adamw_math(w, g, m, v):
    m = ADAM_B1 * m + (1.0 - ADAM_B1) * g
    v = ADAM_B2 * v + (1.0 - ADAM_B2) * (g * g)
    m_hat = m / (1.0 - ADAM_B1**ADAM_STEP)
    v_hat = v / (1.0 - ADAM_B2**ADAM_STEP)
    delta = -ADAM_LR * (m_hat / (jnp.sqrt(v_hat) + ADAM_EPS) + ADAM_WD * w)
    return delta, m, v


def _small_update(gathered, gathered_ws, w, m, v, ws, m_ws, v_ws):
    def body(ga_ref, gws_ref, w_ref, m_ref, v_ref, ws_ref, mws_ref, vws_ref, *outs):
        for src, refs, out in ((ga_ref, (w_ref, m_ref, v_ref), outs[0:4]), (gws_ref, (ws_ref, mws_ref, vws_ref), outs[4:8])):
            g = src[0].astype(F32)
            for j in range(1, N_DEV):
                g = g + src[j].astype(F32)
            out[0][...] = g
            out[1][...], out[2][...], out[3][...] = _adamw_math(refs[0][...], g, refs[1][...], refs[2][...])

    shapes = [jax.ShapeDtypeStruct(w.shape, F32)] * 4 + [jax.ShapeDtypeStruct(ws.shape, F32)] * 4
    return pl.pallas_call(
        body,
        name="small_update",
        in_specs=[VMEM_SPEC] * 8,
        out_specs=[VMEM_SPEC] * 8,
        out_shape=shapes,
        compiler_params=_params(),
    )(gathered, gathered_ws, w, m, v, ws, m_ws, v_ws)


def _ada_update(c_all, d_ada_cols, w, m, v):
    n_layers = w.shape[0]

    def body(c_ref, da_ref, w_ref, m_ref, v_ref, g_ref, d_ref, mo_ref, vo_ref):
        cv = c_ref[...]
        cond = cv * _sigmoid(cv)
        for l in range(n_layers):
            g = lax.dot_general(
                cond, da_ref[l], (((0,), (0,)), ((), ())), preferred_element_type=F32, precision=lax.Precision.HIGHEST
            )
            g_ref[l] = g
            d_ref[l], mo_ref[l], vo_ref[l] = _adamw_math(w_ref[l], g, m_ref[l], v_ref[l])

    return pl.pallas_call(
        body,
        name="ada_update",
        in_specs=[VMEM_SPEC] * 5,
        out_specs=[VMEM_SPEC] * 4,
        out_shape=[jax.ShapeDtypeStruct(w.shape, F32)] * 4,
        compiler_params=_params(),
    )(c_all, d_ada_cols, w, m, v)


def _position():
    return lax.axis_index("x"), lax.axis_index("y"), lax.axis_index("c")


def _flip(pos, k):
    x, y, c = pos
    return (1 - x if k & 4 else x, 1 - y if k & 2 else y, 1 - c if k & 1 else c)


def _index(pos):
    x, y, c = pos
    return 4 * x + 2 * y + c


def _remote(src, dst, send_sem, recv_sem, to):
    return pltpu.make_async_remote_copy(
        src_ref=src, dst_ref=dst, send_sem=send_sem, recv_sem=recv_sem, device_id=to, device_id_type=MESH_ID
    )


def _all_gather_stages(slots, send_sems, recv_sems, sources=None, local_sems=None):
    me = _position()
    sibling = _flip(me, 1)
    others = (4, 2, 6)
    arrays = range(len(slots))

    def copy(t, k, block, to, own=False):
        slot = slots[t](_index(block))
        src = sources[t] if own and sources is not None else slot
        return _remote(src, slot, send_sems.at[7 * t + k], recv_sems.at[7 * t + k], to)

    def first(t):
        return [copy(t, 0, me, sibling, own=True)] + [copy(t, 1 + j, me, _flip(me, f), own=True) for j, f in enumerate(others)]

    def passed(t, j):
        return copy(t, 4 + j, _flip(me, others[j]), sibling)

    def local(t):
        return pltpu.make_async_copy(sources[t], slots[t](_index(me)), local_sems.at[t])

    def start():
        for t in arrays:
            if sources is not None:
                local(t).start()
            for cp in first(t):
                cp.start()

    def forward():
        for j, f in enumerate(others):
            for t in arrays:
                copy(t, 1 + j, _flip(me, f), me).wait_recv()
                passed(t, j).start()

    def finish():
        for t in arrays:
            copy(t, 0, sibling, me).wait_recv()
            for j, f in enumerate(others):
                copy(t, 4 + j, _flip(sibling, f), me).wait_recv()
        for t in arrays:
            for cp in first(t) + [passed(t, j) for j in range(len(others))]:
                cp.wait_send()
            if sources is not None:
                local(t).wait()

    return start, forward, finish


def _two_level_all_gather(slots, send_sems, recv_sems, between=None):
    start, forward, finish = _all_gather_stages(slots, send_sems, recv_sems)
    start()
    if between is not None:
        between()
    forward()
    finish()


def _row_block(ref, rows):
    return lambda j: ref.at[pl.ds(pl.multiple_of(j * rows, 16), rows), :]


def _scatter_stages(blocks, landing, send_sems, recv_sems, local_sems):
    me = _position()
    my = _index(me)
    arrays = range(len(blocks))

    def copy(t, k):
        px, py, pc = to = _flip(me, k)
        return _remote(blocks[t].at[2 * px + py, pc], landing[t].at[my], send_sems.at[7 * t + k - 1], recv_sems.at[7 * t + k - 1], to)

    def arrival(t, k):
        slot = landing[t].at[_index(_flip(me, k))]
        return _remote(slot, slot, send_sems.at[7 * t + k - 1], recv_sems.at[7 * t + k - 1], _flip(me, k))

    def local(t):
        x, y, c = me
        return pltpu.make_async_copy(blocks[t].at[2 * x + y, c], landing[t].at[my], local_sems.at[t])

    def start():
        for t in arrays:
            local(t).start()
            for k in range(1, N_DEV):
                copy(t, k).start()

    def finish():
        for t in arrays:
            for k in range(1, N_DEV):
                arrival(t, k).wait_recv()
        for t in arrays:
            for k in range(1, N_DEV):
                copy(t, k).wait_send()
            local(t).wait()

    return start, finish


def _ada_exchange(c_ref, w_ref, call_ref, parts_ref, sbuf, sem_s1, sem_r1, sem_s2, sem_r2):
    d = c_ref.shape[-1]
    n_layers = w_ref.shape[0]
    me = _position()
    my = _index(me)
    call_ref[my] = jnp.broadcast_to(c_ref[...], (SUBLANES, d))
    mine = call_ref.at[my]
    first = [_remote(mine, mine, sem_s1.at[k - 1], sem_r1.at[k - 1], _flip(me, k)) for k in range(1, N_DEV)]
    for cp in first:
        cp.start()
    for k in range(1, N_DEV):
        theirs = call_ref.at[_index(_flip(me, k))]
        _remote(theirs, theirs, sem_s1.at[k - 1], sem_r1.at[k - 1], _flip(me, k)).wait_recv()
    cv = call_ref[...].reshape(N_DEV * SUBLANES, d)
    cond = cv * _sigmoid(cv)
    for l in range(n_layers):
        rows = jnp.dot(cond, w_ref[l], preferred_element_type=F32, precision=lax.Precision.HIGHEST)
        for b in range(N_DEV):
            sbuf[b, l] = rows[b * SUBLANES : (b + 1) * SUBLANES]
    parts_ref[my] = sbuf[my]
    second = []
    for k in range(1, N_DEV):
        to = _flip(me, k)
        second.append(_remote(sbuf.at[_index(to)], parts_ref.at[my], sem_s2.at[k - 1], sem_r2.at[k - 1], to))
    for cp in second:
        cp.start()
    for k in range(1, N_DEV):
        theirs = parts_ref.at[_index(_flip(me, k))]
        _remote(theirs, theirs, sem_s2.at[k - 1], sem_r2.at[k - 1], _flip(me, k)).wait_recv()
    for cp in first + second:
        cp.wait_send()


def _gather_weights(w_in_t, w_out, c_row, w_ada):
    n_layers, rows_in, d = w_in_t.shape
    width = w_ada.shape[2]

    def body(wi_ref, wo_ref, c_ref, wa_ref, gi_ref, si_ref, so_ref, call_ref, parts_ref, sbuf, send_sems, recv_sems, *ada_sems):
        my = _index(_position())
        si_ref[...] = wi_ref[...].astype(si_ref.dtype)
        so_ref[...] = wo_ref[...].astype(so_ref.dtype)
        gi_ref[pl.ds(pl.multiple_of(my * rows_in, 16), rows_in), :] = si_ref[0]
        _two_level_all_gather(
            (_row_block(gi_ref, rows_in),),
            send_sems,
            recv_sems,
            between=functools.partial(_ada_exchange, c_ref, wa_ref, call_ref, parts_ref, sbuf, *ada_sems),
        )

    return pl.pallas_call(
        body,
        name="gather_weights",
        in_specs=[VMEM_SPEC] * 4,
        out_specs=[VMEM_SPEC] * 5,
        out_shape=[
            jax.ShapeDtypeStruct((N_DEV * rows_in, d), MXU_DTYPE),
            jax.ShapeDtypeStruct(w_in_t.shape, MXU_DTYPE),
            jax.ShapeDtypeStruct(w_out.shape, MXU_DTYPE),
            jax.ShapeDtypeStruct((N_DEV, SUBLANES, d), F32),
            jax.ShapeDtypeStruct((N_DEV, n_layers, SUBLANES, width), F32),
        ],
        scratch_shapes=[
            pltpu.VMEM((N_DEV, n_layers, SUBLANES, width), F32),
            pltpu.SemaphoreType.DMA((7,)),
            pltpu.SemaphoreType.DMA((7,)),
        ]
        + [pltpu.SemaphoreType.DMA((N_DEV - 1,))] * 4,
        compiler_params=_params(),
    )(w_in_t, w_out, c_row, w_ada)


def _gather_small(packed, adam=()):
    n_adam = len(adam)

    def body(p_ref, *rest):
        quads = [rest[4 * t : 4 * t + 4] for t in range(n_adam)]
        rest = rest[4 * n_adam :]
        g_ref = rest[0]
        results = [rest[1 + 3 * t : 4 + 3 * t] for t in range(n_adam)]
        send_sems, recv_sems = rest[1 + 3 * n_adam :]
        g_ref[_index(_position())] = p_ref[...]

        def updates():
            for (w_ref, gr_ref, m_ref, v_ref), (d_ref, mo_ref, vo_ref) in zip(quads, results):
                d_ref[...], mo_ref[...], vo_ref[...] = _adamw_math(w_ref[...], gr_ref[...], m_ref[...], v_ref[...])

        _two_level_all_gather((lambda j: g_ref.at[j],), send_sems, recv_sems, between=updates)

    return pl.pallas_call(
        body,
        name="gather_small",
        in_specs=[VMEM_SPEC] * (1 + 4 * n_adam),
        out_specs=[VMEM_SPEC] * (1 + 3 * n_adam),
        out_shape=[jax.ShapeDtypeStruct((N_DEV,) + packed.shape, F32)]
        + [jax.ShapeDtypeStruct(q[0].shape, F32) for q in adam for _ in range(3)],
        scratch_shapes=[pltpu.SemaphoreType.DMA((7,)), pltpu.SemaphoreType.DMA((7,))],
        compiler_params=_params(),
    )(packed, *[a for q in adam for a in q])


def _scatter_finish(landed, name, own=()):
    n = len(landed)

    def body(*refs):
        if own:
            x, y, c = _position()
            my = _index((x, y, c))
        for t, (src, out) in enumerate(zip(refs[:n], refs[n + len(own) :])):
            g = None
            for j in range(N_DEV):
                part = src[j].astype(F32)
                if own:
                    part = jnp.where(j == my, refs[n + t][2 * x + y, c].astype(F32), part)
                g = part if g is None else g + part
            out[...] = g

    return pl.pallas_call(
        body,
        name=name,
        in_specs=[VMEM_SPEC] * (n + len(own)),
        out_specs=[VMEM_SPEC] * n,
        out_shape=[jax.ShapeDtypeStruct(a.shape[1:], F32) for a in landed],
        compiler_params=_params(),
    )(*landed, *own)


SEM_SPEC = pl.BlockSpec(memory_space=pltpu.SEMAPHORE)
SPLIT_COPY = pltpu.SideEffectType.DATAFLOW_SIDE_EFFECTING


def _scatter_start(blocks, name):
    land_shape = (N_DEV,) + blocks.shape[2:]

    def body(blocks_ref, land_ref, send_sems, recv_sems, blocks_thru, land_thru, token):
        me = _position()
        my = _index(me)
        for k in range(1, N_DEV):
            px, py, pc = to = _flip(me, k)
            _remote(blocks_ref.at[2 * px + py, pc], land_ref.at[my], send_sems.at[k - 1], recv_sems.at[k - 1], to).start()
        token[...] = jnp.zeros_like(token)

    return pl.pallas_call(
        body,
        name=name,
        in_specs=(HBM_SPEC, HBM_SPEC),
        out_specs=(SEM_SPEC, SEM_SPEC, HBM_SPEC, HBM_SPEC, VMEM_SPEC),
        out_shape=(
            pltpu.SemaphoreType.DMA((N_DEV - 1,)),
            pltpu.SemaphoreType.DMA((N_DEV - 1,)),
            pltpu.HBM(blocks.shape, blocks.dtype),
            pltpu.HBM(land_shape, blocks.dtype),
            jax.ShapeDtypeStruct((SUBLANES, LANES), F32),
        ),
        input_output_aliases={0: 2, 1: 3},
        compiler_params=pltpu.CompilerParams(has_side_effects=SPLIT_COPY),
    )(pltpu.with_memory_space_constraint(blocks, pltpu.HBM), pltpu.with_memory_space_constraint(lax.empty(land_shape, blocks.dtype), pltpu.HBM))


def _scatter_wait(send_sems, recv_sems, blocks_thru, land_thru, after, name):
    def body(blocks_ref, land_ref, send_sems, recv_sems, after_ref, blocks_dead, got_ref):
        me = _position()
        my = _index(me)
        for k in range(1, N_DEV):
            px, py, pc = to = _flip(me, k)
            _remote(blocks_ref.at[2 * px + py, pc], land_ref.at[my], send_sems.at[k - 1], recv_sems.at[k - 1], to).wait_send()
        for k in range(1, N_DEV):
            slot = land_ref.at[_index(_flip(me, k))]
            _remote(slot, slot, send_sems.at[k - 1], recv_sems.at[k - 1], _flip(me, k)).wait_recv()

    return pl.pallas_call(
        body,
        name=name,
        in_specs=(HBM_SPEC, HBM_SPEC, SEM_SPEC, SEM_SPEC, pl.BlockSpec(memory_space=pl.ANY)),
        out_specs=(HBM_SPEC, HBM_SPEC),
        out_shape=(pltpu.HBM(blocks_thru.shape, blocks_thru.dtype), pltpu.HBM(land_thru.shape, land_thru.dtype)),
        input_output_aliases={0: 0, 1: 1},
        compiler_params=pltpu.CompilerParams(has_side_effects=SPLIT_COPY),
    )(blocks_thru, land_thru, send_sems, recv_sems, after)


def _pack_rows(parts):
    rows, offsets, at = [], [], 0
    for p in parts:
        flat = p.reshape(-1)
        n = -(-flat.shape[0] // (SUBLANES * LANES)) * SUBLANES
        rows.append(jnp.pad(flat, (0, n * LANES - flat.shape[0])).reshape(n, LANES))
        offsets.append(at)
        at += n
    return jnp.concatenate(rows, axis=0), offsets


def _unpack_rows(packed, offsets, shapes):
    out = []
    for off, shape in zip(offsets, shapes):
        size = 1
        for s in shape:
            size *= s
        n = -(-size // (SUBLANES * LANES)) * SUBLANES
        out.append(packed[off : off + n].reshape(-1)[:size].reshape(shape))
    return out


def kernel(x, c, w_ada, b_ada, norm_gain, w_in, q_gain, k_gain, sink, w_s, b_s, w_out, loss_target, m_w_ada, m_b_ada, m_norm_gain, m_w_in, m_q_gain, m_k_gain, m_sink, m_w_s, m_b_s, m_w_out, v_w_ada, v_b_ada, v_norm_gain, v_w_in, v_q_gain, v_k_gain, v_sink, v_w_s, v_b_s, v_w_out):
    seq, d = x.shape[1], x.shape[2]
    n_layers = w_in.shape[0]
    w_cols = w_in.shape[2]
    ada_cols = w_ada.shape[2]
    my = _index(_position())
    xs = x.reshape(seq, d)
    target = loss_target.reshape(seq, d)

    w_in_t0, shard_in, shard_out, c_all, ada_parts = _gather_weights(w_in.transpose(0, 2, 1), w_out, c, w_ada)
    w_in_ts, w_outs = [w_in_t0], []
    ada = ada_parts[:, :, 0, :].transpose(1, 0, 2).reshape(n_layers, 3 * d) + b_ada
    shift, scale1, gate = ada[:, None, 0:d], 1.0 + ada[:, None, d : 2 * d], ada[:, None, 2 * d : 3 * d]
    gain = norm_gain[:, None, :]

    w_s_m = w_s.astype(MXU_DTYPE)
    w_s_t = w_s_m.transpose(0, 1, 3, 2)
    b_st = jnp.repeat(b_s.transpose(0, 2, 1), HEAD_DIM, axis=2)
    q_gain2 = jnp.tile(q_gain, (1, 2))[:, None, :]
    k_gain2 = jnp.tile(k_gain, (1, 2))[:, None, :]

    xl, saved = xs, []
    for l in range(n_layers):
        last = l == n_layers - 1
        pa, pb = _ln_proj_fwd(xl, gain[l], scale1[l], shift[l], w_in_ts[l], f"ln_proj_fwd_{l}")
        wanted = ([shard_out[0]] if l == 0 else []) + ([] if last else [shard_out[l + 1], shard_in[l + 1]])
        o, probs, p_sink, *arrived = _attn_fwd(pa, q_gain2[l], k_gain2[l], sink[l], f"attn_fwd_{l}", gather=tuple(wanted))
        if not last:
            w_in_ts.append(arrived.pop())
        w_outs += arrived
        *out, sv = _mix_out_fwd(pb, o, xl, gate[l], w_outs[l], w_s_m[l], b_st[l], f"mix_out_fwd_{l}", target if last else None)
        saved.append((xl, pa, pb, o, probs, p_sink, sv))
        if last:
            dx, sq_err = out
        else:
            (xl,) = out

    g_w_in, g_w_out, small, d_ada_rows = [None] * n_layers, [None] * n_layers, [None] * n_layers, [None] * n_layers
    waiting = []
    d_ws_all = [None] * n_layers
    for l in reversed(range(n_layers)):
        x_l, pa, pb, o, probs, p_sink, sv = saved[l]
        dpb, do, dw_out, d_gate8, d_ws, d_bs = _mix_out_bwd(dx, pb, o, sv, gate[l], w_outs[l], w_s_t[l], f"mix_out_bwd_{l}")
        own_out = (g_w_out, l, dw_out.reshape(4, 2, D_MIX // N_DEV, d))
        late = []
        if 0 < l == n_layers - 1:
            riding, waiting = [], waiting + [own_out]
        elif l == 0 and waiting:
            riding, waiting, late = waiting, [], [own_out]
        else:
            riding, waiting = waiting + [own_out], []
        attn = _attn_bwd(
            pa, o, do, probs, p_sink, q_gain2[l], k_gain2[l], f"attn_bwd_{l}", scatter=tuple(b for _, _, b in riding)
        )
        dq, dkv, halo_prev, halo_next, d_qg, d_kg, d_sk = attn[:7]
        if riding:
            for (dest, layer, _), total in zip(riding, _scatter_finish(attn[7:], f"scatter_finish_{l}")):
                dest[layer] = total.transpose(1, 0) if dest is g_w_in else total
        d_ws_all[l] = d_ws
        dw_args = (x_l, gain[l], scale1[l], shift[l], dq, dkv, halo_prev, halo_next, dpb, f"proj_bwd_dw_{l}")
        if l > 0:
            dw_in_t, dkvb = _proj_bwd_dw(*dw_args)
        else:
            d_ws_wire = jnp.stack(d_ws_all).reshape(-1, LANES).astype(jnp.bfloat16)
            dw_in_t, dkvb, gathered_ws, *landed = _proj_bwd_dw(
                *dw_args, gather=(d_ws_wire,), scatter=tuple(b for _, _, b in late)
            )
            if late:
                for (dest, layer, _), total in zip(late, _scatter_finish(landed, "scatter_finish_late")):
                    dest[layer] = total
        blocks_in = dw_in_t.reshape(4, 2, w_cols, d)
        if l > 0:
            waiting.append((g_w_in, l, blocks_in))
            dx, c0, c1 = _proj_bwd_dx(x_l, dx, dq, dkvb, dpb, w_in_ts[l], gain[l], scale1[l], f"proj_bwd_dx_{l}")
        else:
            *in_flight, token = _scatter_start(blocks_in, "scatter_start_in_0")
            dx, c0, c1 = _proj_bwd_dx(
                x_l, dx, dq, dkvb, dpb, w_in_ts[l], gain[l] + token[0, 0], scale1[l], f"proj_bwd_dx_{l}"
            )
            sent, landed = _scatter_wait(*in_flight, dx, "scatter_wait_in_0")
            g_w_in[l] = _scatter_finish((landed,), "scatter_finish_in_0", own=(sent,))[0].transpose(1, 0)
        c0s, c1s = c0.sum(axis=0), c1.sum(axis=0)
        d_ada_rows[l] = jnp.concatenate([c0s, norm_gain[l] * c1s, d_gate8.sum(axis=0)])
        small[l] = (
            scale1[l, 0] * c1s,
            d_qg.sum(axis=0).reshape(N_HEADS, HEAD_DIM).sum(axis=0),
            d_kg.sum(axis=0).reshape(2, HEAD_DIM).sum(axis=0),
            d_sk.sum(axis=0)[0:N_HEADS],
            d_bs.reshape(BLK, N_GROUPS, HEAD_DIM).sum(axis=2).transpose(1, 0),
        )

    names = ("norm_gain", "q_gain", "k_gain", "sink", "b_s")
    stacked = [jnp.stack([small[l][t] for l in range(n_layers)]) for t in range(len(names))]
    d_ada = jnp.stack(d_ada_rows)
    packed, offsets = _pack_rows(stacked + [d_ada, sq_err[0, 0:1]])
    g_w_in, g_w_out = jnp.stack(g_w_in), jnp.stack(g_w_out)
    gathered, *upd = _gather_small(packed, adam=((w_in, g_w_in, m_w_in, v_w_in), (w_out, g_w_out, m_w_out, v_w_out)))
    gathered_ws = gathered_ws.reshape(N_DEV, -1, LANES)
    upd_in, upd_out = upd[0:3], upd[3:6]
    no_weight = jnp.zeros((1,), F32)
    weights = (norm_gain, q_gain, k_gain, sink, b_s, b_ada, no_weight)
    moments_m = (m_norm_gain, m_q_gain, m_k_gain, m_sink, m_b_s, m_b_ada, no_weight)
    moments_v = (v_norm_gain, v_q_gain, v_k_gain, v_sink, v_b_s, v_b_ada, no_weight)
    w_pack, _ = _pack_rows(weights)
    m_pack, _ = _pack_rows(moments_m)
    v_pack, _ = _pack_rows(moments_v)
    shapes = [w.shape for w in weights]
    flat_ws = lambda a: a.reshape(-1, LANES)
    updated = _small_update(gathered, gathered_ws, w_pack, m_pack, v_pack, flat_ws(w_s), flat_ws(m_w_s), flat_ws(v_w_s))
    g_small, d_small, m_small, v_small = (_unpack_rows(p, offsets, shapes) for p in updated[0:4])
    ws_small = [p.reshape(w_s.shape) for p in updated[4:8]]
    loss = g_small[-1][0] * (0.5 / d)

    ada_off = offsets[-2]
    ada_n = -(-n_layers * 3 * d // (SUBLANES * LANES)) * SUBLANES
    d_ada_all = gathered[:, ada_off : ada_off + ada_n].reshape(N_DEV, -1)[:, : n_layers * 3 * d].reshape(N_DEV, n_layers, 3 * d)
    d_ada_cols = lax.dynamic_slice_in_dim(d_ada_all, my * ada_cols, ada_cols, axis=2)
    g_w_ada, *upd_ada = _ada_update(c_all[:, 0, :], d_ada_cols.transpose(1, 0, 2), w_ada, m_w_ada, v_w_ada)

    def ordered(ada_, in_, out_, small_, ws):
        ng, qg, kg, sk, bs, ba, _ = small_
        return (ada_, ba, ng, in_, qg, kg, sk, ws, bs, out_)

    grads = ordered(g_w_ada, g_w_in, g_w_out, g_small, ws_small[0])
    deltas = ordered(upd_ada[0], upd_in[0], upd_out[0], d_small, ws_small[1])
    new_m = ordered(upd_ada[1], upd_in[1], upd_out[1], m_small, ws_small[2])
    new_v = ordered(upd_ada[2], upd_in[2], upd_out[2], v_small, ws_small[3])
    return (loss, dx.reshape(x.shape), *grads, *deltas, *new_m, *new_v)
```

```python
import functools

import jax
import jax.numpy as jnp
from jax import lax
from jax.experimental import pallas as pl
from jax.experimental.pallas import tpu as pltpu

F32 = jnp.float32
MXU_DTYPE = jnp.bfloat16
MESH_ID = pl.DeviceIdType.MESH

N_DEV = 8
HEAD_DIM = 64
N_HEADS = 8
Q_PER_KV = 4
D_ATTN = 512
D_KV = 128
D_GM = 512
N_GROUPS = 8
D_MIX = D_ATTN + D_GM
BLK = 128
LANES = 128
SUBLANES = 8
N_PAIRS = D_ATTN // LANES
D_QKV = D_ATTN + 2 * D_KV
D_REST = D_ATTN + 3 * D_GM
D_IN = D_QKV + D_REST
EPS = 1e-6
NEG_INF = -1e30
ALIBI_SLOPES = tuple(2.0 ** (-8.0 * (h + 1) / N_HEADS) for h in range(N_HEADS))
Q_SCALE = 1.0 / 8.0

ADAM_LR = 0.001
ADAM_B1 = 0.9
ADAM_B2 = 0.999
ADAM_EPS = 1e-08
ADAM_WD = 0.01
ADAM_STEP = 10

TOKEN_TILE = 512
VMEM_LIMIT_BYTES = 56 * 1024 * 1024


def _params(semantics=None):
    return pltpu.CompilerParams(dimension_semantics=semantics, vmem_limit_bytes=VMEM_LIMIT_BYTES)


def _dot(a, b):
    return jnp.dot(a, b, preferred_element_type=F32)


def _dot_nt(a, b):
    return lax.dot_general(a, b, (((1,), (1,)), ((), ())), preferred_element_type=F32)


def _dot_tn(a, b):
    return lax.dot_general(a, b, (((0,), (0,)), ((), ())), preferred_element_type=F32)


def _mx(v):
    return v.astype(MXU_DTYPE)


def _lane_lo(rows):
    return lax.broadcasted_iota(jnp.int32, (rows, LANES), 1) < HEAD_DIM


def _half_ones(width=LANES):
    group_bits = HEAD_DIM.bit_length() - 1
    r = jnp.right_shift(lax.broadcasted_iota(jnp.int32, (width, width), 0), group_bits)
    c = jnp.right_shift(lax.broadcasted_iota(jnp.int32, (width, width), 1), group_bits)
    return jnp.where(r == c, 1.0, 0.0).astype(jnp.bfloat16)


WIDE = 2 * LANES


def _half_sum(v, ones):
    p1 = v.astype(jnp.bfloat16)
    p2 = (v - p1.astype(F32)).astype(jnp.bfloat16)
    return _dot(p1, ones) + _dot(p2, ones)


def _half_rms(v, ones):
    r = lax.rsqrt(_half_sum(v * v, ones) * (1.0 / HEAD_DIM) + EPS)
    return v * r, r


def _half_rms_bwd(dy, vhat, r, ones):
    return r * (dy - vhat * (_half_sum(vhat * dy, ones) * (1.0 / HEAD_DIM)))


def _group_rows(v):
    rows, n = v.shape
    return v.reshape(rows // SUBLANES, SUBLANES, n).sum(axis=0)


def _sigmoid(v):
    return 1.0 / (1.0 + jnp.exp(-v))


ROW_CHUNK = 32
VARIANT_HEADS = ((0, 2, 5, 7), (1, 3, 4, 6))
HEAD_SLOT = {h: (v, s) for v, heads in enumerate(VARIANT_HEADS) for s, h in enumerate(heads)}
STACK = Q_PER_KV * BLK


def _fill_attn_bias(bias_s):
    qi = lax.broadcasted_iota(jnp.int32, (BLK, 3 * BLK), 0)
    ci = lax.broadcasted_iota(jnp.int32, (BLK, 3 * BLK), 1)
    dist = jnp.abs(ci - BLK - qi)
    distf = dist.astype(F32)
    window = dist <= BLK
    for kind, seen in enumerate((window & (ci >= BLK), window, window & (ci < 2 * BLK))):
        for h in range(N_HEADS):
            bias_s[kind, h] = jnp.where(seen, -(ALIBI_SLOPES[h] * distf), NEG_INF)


def _block_kind(block, seq):
    assert seq >= 2 * BLK
    return jnp.where(block == 0, 0, jnp.where(block == seq // BLK - 1, 2, 1))


def _stage_queries(qn, lo_t, j, nb, qs):
    for a in range(2):
        v, slot = HEAD_SLOT[2 * j + a]
        qm = _mx(jnp.where(lo_t, qn, 0.0) if a == 0 else jnp.where(lo_t, 0.0, qn))
        for n in range(nb):
            qs[n, v, slot * BLK : (slot + 1) * BLK, :] = qm[n * BLK : (n + 1) * BLK]


def _unstack_pair(stacked, j, lo):
    (v0, s0), (v1, s1) = HEAD_SLOT[2 * j], HEAD_SLOT[2 * j + 1]
    return jnp.where(lo, stacked[v0][s0 * BLK : (s0 + 1) * BLK], stacked[v1][s1 * BLK : (s1 + 1) * BLK])


def _stage_keys(kvp_ref, qkv_ref, kvn_ref, kg, ones, tile, ks, kr, vs, vr, khat_s=None, rk_s=None):
    pieces = (
        (0, BLK, kvp_ref[:, 0:D_KV], kvp_ref[:, D_KV : 2 * D_KV]),
        (BLK, tile, qkv_ref[:, D_ATTN : D_ATTN + D_KV], qkv_ref[:, D_ATTN + D_KV : D_QKV]),
        (BLK + tile, BLK, kvn_ref[:, 0:D_KV], kvn_ref[:, D_KV : 2 * D_KV]),
    )
    for r0, n, k, v in pieces:
        khat, rk = _half_rms(k, ones)
        kn = khat * kg
        ks[r0 : r0 + n, :] = _mx(kn)
        kr[r0 : r0 + n, :] = _mx(pltpu.roll(kn, HEAD_DIM, 1))
        vs[r0 : r0 + n, :] = _mx(v)
        vr[r0 : r0 + n, :] = _mx(pltpu.roll(v, HEAD_DIM, 1))
        if khat_s is not None:
            khat_s[r0 : r0 + n, :] = khat
            rk_s[r0 : r0 + n, :] = rk


def _halo_specs(tile, seq):
    nb = tile // BLK
    last = seq // BLK - 1
    kv_col = D_ATTN // (2 * D_KV)
    prev = pl.BlockSpec((BLK, 2 * D_KV), lambda i: (jnp.maximum(i * nb - 1, 0), kv_col))
    nxt = pl.BlockSpec((BLK, 2 * D_KV), lambda i: (jnp.minimum((i + 1) * nb, last), kv_col))
    return prev, nxt


def _row_spec(tile, width):
    return pl.BlockSpec((tile, width), lambda i: (i, 0))


def _full_spec(shape):
    nd = len(shape)
    return pl.BlockSpec(shape, lambda i: (0,) * nd)


SMEM_SPEC = pl.BlockSpec(memory_space=pltpu.SMEM)
VMEM_SPEC = pl.BlockSpec(memory_space=pltpu.VMEM)
HBM_SPEC = pl.BlockSpec(memory_space=pltpu.HBM)


def _rider_steps(nt):
    return 0, (3 * nt) // 4, nt - 1


def _gather_rider(sources, gathered, sems, step, nt):
    start, forward, finish = _all_gather_stages(
        [_row_block(g, s.shape[0]) for g, s in zip(gathered, sources)], sems[0], sems[1], sources=sources, local_sems=sems[2]
    )
    at_start, at_forward, at_finish = _rider_steps(nt)
    pl.when(step == at_start)(start)

    def after_compute():
        pl.when(step == at_forward)(forward)
        pl.when(step == at_finish)(finish)

    return after_compute


def _gathered_shapes(gather):
    return [jax.ShapeDtypeStruct((N_DEV * g.shape[0], g.shape[1]), g.dtype) for g in gather]


def _ln_proj_fwd(x, gain, scale1, shift, w_in_t, name):
    seq, d = x.shape
    tile = min(TOKEN_TILE, seq)

    def body(x_ref, g_ref, s1_ref, sh_ref, wt_ref, pa_ref, pb_ref):
        xv = x_ref[...]
        r = lax.rsqrt(jnp.mean(xv * xv, axis=-1, keepdims=True) + EPS)
        h = _mx((xv * r) * g_ref[...] * s1_ref[...] + sh_ref[...])
        pa_ref[...] = _dot_nt(h, wt_ref[0:D_QKV, :])
        pb_ref[...] = _dot_nt(h, wt_ref[D_QKV:D_IN, :])

    vec = _full_spec((1, d))
    return pl.pallas_call(
        body,
        name=name,
        grid=(seq // tile,),
        in_specs=[_row_spec(tile, d), vec, vec, vec, _full_spec((D_IN, d))],
        out_specs=[_row_spec(tile, D_QKV), _row_spec(tile, D_REST)],
        out_shape=[jax.ShapeDtypeStruct((seq, D_QKV), F32), jax.ShapeDtypeStruct((seq, D_REST), F32)],
        compiler_params=_params(("parallel",)),
    )(x, gain, scale1, shift, w_in_t)


def _attn_fwd(pa, q_gain2, k_gain2, sink, name, gather=()):
    seq = pa.shape[0]
    tile = min(TOKEN_TILE, seq)
    nb = tile // BLK
    nt = seq // tile
    ext = tile + 2 * BLK
    n_ride = len(gather)
    riding = n_ride > 0

    def body(sink_ref, qkv_ref, kvp_ref, kvn_ref, qg_ref, kg_ref, *rest):
        i = pl.program_id(0)
        sources, (o_ref, p_ref, psink_ref), gathered = rest[:n_ride], rest[n_ride : n_ride + 3], rest[n_ride + 3 : 2 * n_ride + 3]
        qs, ks, kr, vs, vr, bias_s, s_scr, *sems = rest[2 * n_ride + 3 :]
        if riding:
            after_compute = _gather_rider(sources, gathered, sems, i, nt)

        @pl.when(i == 0)
        def _():
            _fill_attn_bias(bias_s)

        ones = _half_ones()
        lo = _lane_lo(BLK)
        lo_t = _lane_lo(tile)
        head_lane = lax.broadcasted_iota(jnp.int32, (ROW_CHUNK, LANES), 1)
        _stage_keys(kvp_ref, qkv_ref, kvn_ref, kg_ref[...], ones, tile, ks, kr, vs, vr)
        for j in range(N_PAIRS):
            qhat, _ = _half_rms(qkv_ref[:, j * LANES : (j + 1) * LANES], ones)
            _stage_queries(qhat * (qg_ref[...] * Q_SCALE), lo_t, j, nb, qs)

        def block(n, carry):
            r0 = pl.multiple_of(n * BLK, BLK)
            krows = pl.ds(r0, 3 * BLK)
            kind = _block_kind(i * nb + n, seq)
            for v in range(2):
                s_scr[v] = _dot_nt(qs[n, v], (kr if v else ks)[krows, :])
            for rc in range(0, BLK, ROW_CHUNK):
                p_sink = jnp.zeros((ROW_CHUNK, LANES), F32)
                for h in range(N_HEADS):
                    v, slot = HEAD_SLOT[h]
                    sink_h = sink_ref[h]
                    rows = slice(slot * BLK + rc, slot * BLK + rc + ROW_CHUNK)
                    s = s_scr[v, rows, :] + bias_s[kind, h, rc : rc + ROW_CHUNK, :]
                    m = jnp.maximum(jnp.max(s, axis=-1, keepdims=True), sink_h)
                    p = jnp.exp(s - m)
                    e_sink = jnp.exp(sink_h - m)
                    inv = 1.0 / (jnp.sum(p, axis=-1, keepdims=True) + e_sink)
                    p_ref[n, v, rows, :] = _mx(p * inv)
                    p_sink = jnp.where(head_lane == h, e_sink * inv, p_sink)
                psink_ref[pl.ds(pl.multiple_of(r0 + rc, ROW_CHUNK), ROW_CHUNK), :] = p_sink
            outs = [_dot(p_ref[n, v], (vr if v else vs)[krows, :]) for v in range(2)]
            for j in range(N_PAIRS):
                o_ref[pl.ds(r0, BLK), j * LANES : (j + 1) * LANES] = _unstack_pair(outs, j, lo)
            return carry

        lax.fori_loop(0, nb, block, 0)
        if riding:
            after_compute()

    prev, nxt = _halo_specs(tile, seq)
    vec = _full_spec((1, LANES))
    in_specs = [SMEM_SPEC, _row_spec(tile, D_QKV), prev, nxt, vec, vec]
    out_specs = [
        _row_spec(tile, D_ATTN),
        pl.BlockSpec((nb, 2, STACK, 3 * BLK), lambda i: (i, 0, 0, 0)),
        _row_spec(tile, LANES),
    ]
    out_shape = [
        jax.ShapeDtypeStruct((seq, D_ATTN), F32),
        jax.ShapeDtypeStruct((seq // BLK, 2, STACK, 3 * BLK), MXU_DTYPE),
        jax.ShapeDtypeStruct((seq, LANES), F32),
    ]
    scratch = [
        pltpu.VMEM((nb, 2, STACK, LANES), MXU_DTYPE),
        pltpu.VMEM((ext, LANES), MXU_DTYPE),
        pltpu.VMEM((ext, LANES), MXU_DTYPE),
        pltpu.VMEM((ext, LANES), MXU_DTYPE),
        pltpu.VMEM((ext, LANES), MXU_DTYPE),
        pltpu.VMEM((3, N_HEADS, BLK, 3 * BLK), F32),
        pltpu.VMEM((2, STACK, 3 * BLK), F32),
    ]
    return pl.pallas_call(
        body,
        name=name,
        grid=(nt,),
        in_specs=in_specs + [HBM_SPEC] * n_ride,
        out_specs=out_specs + [HBM_SPEC] * n_ride,
        out_shape=out_shape + _gathered_shapes(gather),
        scratch_shapes=scratch + _rider_sems(n_ride),
        compiler_params=_params(("arbitrary",)),
    )(sink, pa, pa, pa, q_gain2, k_gain2, *gather)


def _mix_out_fwd(pb, o, x, gate, w_out, w_s, b_st, name, target=None):
    seq, d = x.shape
    tile = min(TOKEN_TILE, seq)
    nb = tile // BLK
    with_loss = target is not None

    def body(pb_ref, o_ref, x_ref, gate_ref, wo_ref, ws_ref, bs_ref, *rest):
        if with_loss:
            t_ref, xo_ref, acc_ref, sv_ref, y_s, vn_s = rest

            @pl.when(pl.program_id(0) == 0)
            def _():
                acc_ref[...] = jnp.zeros_like(acc_ref)
        else:
            xo_ref, sv_ref, y_s, vn_s = rest
        ones = _half_ones(WIDE)
        lo = _lane_lo(BLK)
        ga = pb_ref[:, 0:D_ATTN]
        y_s[:, 0:D_ATTN] = _mx(o_ref[...] * (ga * _sigmoid(ga)))
        for j in range(D_GM // WIDE):
            vhat, _ = _half_rms(pb_ref[:, 2 * D_GM + j * WIDE : 2 * D_GM + (j + 1) * WIDE], ones)
            vn_s[:, j * WIDE : (j + 1) * WIDE] = _mx(vhat)

        def chunk(n, carry):
            rows = pl.ds(pl.multiple_of(n * BLK, BLK), BLK)
            for j in range(N_PAIRS):
                cols = slice(j * LANES, (j + 1) * LANES)
                vn = vn_s[rows, cols]
                sv = jnp.where(lo, _dot(ws_ref[2 * j], vn), _dot(ws_ref[2 * j + 1], vn)) + bs_ref[:, cols]
                sv_ref[rows, cols] = sv
                u = pb_ref[rows, D_ATTN + j * LANES : D_ATTN + (j + 1) * LANES]
                gg = pb_ref[rows, D_ATTN + 2 * D_GM + j * LANES : D_ATTN + 2 * D_GM + (j + 1) * LANES]
                y_s[rows, D_ATTN + j * LANES : D_ATTN + (j + 1) * LANES] = _mx((u * sv) * (gg * _sigmoid(gg)))
            return carry

        lax.fori_loop(0, nb, chunk, 0)
        y = x_ref[...] + gate_ref[...] * _dot(y_s[...], wo_ref[...])
        if with_loss:
            e = y - t_ref[...]
            xo_ref[...] = e * (1.0 / d)
            acc_ref[...] += jnp.sum(jnp.sum(e * e, axis=-1, keepdims=True), axis=0, keepdims=True)
        else:
            xo_ref[...] = y

    row = _row_spec(tile, d)
    acc_shape = (SUBLANES, LANES)
    return pl.pallas_call(
        body,
        name=name,
        grid=(seq // tile,),
        in_specs=[
            _row_spec(tile, D_REST),
            _row_spec(tile, D_ATTN),
            row,
            _full_spec((1, d)),
            _full_spec((D_MIX, d)),
            _full_spec((N_GROUPS, BLK, BLK)),
            _full_spec((BLK, D_GM)),
        ]
        + ([row] if with_loss else []),
        out_specs=[row] + ([_full_spec(acc_shape)] if with_loss else []) + [_row_spec(tile, D_GM)],
        out_shape=[jax.ShapeDtypeStruct((seq, d), F32)]
        + ([jax.ShapeDtypeStruct(acc_shape, F32)] if with_loss else [])
        + [jax.ShapeDtypeStruct((seq, D_GM), F32)],
        scratch_shapes=[pltpu.VMEM((tile, D_MIX), MXU_DTYPE), pltpu.VMEM((tile, D_GM), MXU_DTYPE)],
        compiler_params=_params(("arbitrary",) if with_loss else ("parallel",)),
    )(pb, o, x, gate, w_out, w_s, b_st, *([target] if with_loss else []))


def _mix_out_bwd(dxn, pb, o, sv, gate, w_out, w_s_t, name):
    seq, d = dxn.shape
    tile = min(TOKEN_TILE, seq)
    nb = tile // BLK
    nt = seq // tile

    def body(dxn_ref, pb_ref, o_ref, sv_ref, gate_ref, wo_ref, wst_ref,
             dpb_ref, do_ref, dwo_ref, dgate_ref, dws_ref, dbs_ref, g_ref, y_s, dy_s, vn_s, rv_s, vnb_s, dsv_s, dvn_s):
        @pl.when(pl.program_id(0) == 0)
        def _():
            g_ref[...] = jnp.zeros_like(g_ref)
            dws_ref[...] = jnp.zeros_like(dws_ref)
            dbs_ref[...] = jnp.zeros_like(dbs_ref)

        ones = _half_ones(WIDE)
        lo = _lane_lo(BLK)
        c_u = slice(D_ATTN, D_ATTN + D_GM)
        c_vg = slice(D_ATTN + D_GM, D_ATTN + 2 * D_GM)
        c_gg = slice(D_ATTN + 2 * D_GM, D_REST)
        dxv = dxn_ref[...]
        dy_s[...] = _dot_nt(_mx(dxv * gate_ref[...]), wo_ref[...])
        ga = pb_ref[:, 0:D_ATTN]
        sig = _sigmoid(ga)
        sil = ga * sig
        ov = o_ref[...]
        y_s[:, 0:D_ATTN] = _mx(ov * sil)
        da = dy_s[:, 0:D_ATTN]
        do_ref[...] = da * sil
        dpb_ref[:, 0:D_ATTN] = (da * ov * (sig * (1.0 + ga * (1.0 - sig)))).astype(dpb_ref.dtype)
        for j in range(D_GM // WIDE):
            cols = slice(j * WIDE, (j + 1) * WIDE)
            vhat, rv = _half_rms(pb_ref[:, 2 * D_GM + j * WIDE : 2 * D_GM + (j + 1) * WIDE], ones)
            vn_s[:, cols] = vhat
            rv_s[:, cols] = rv
            vnb_s[:, cols] = _mx(vhat)

        def gating(n, carry):
            rows = pl.ds(pl.multiple_of(n * BLK, BLK), BLK)
            sv = sv_ref[rows, :]
            u = pb_ref[rows, c_u]
            gg = pb_ref[rows, c_gg]
            sg = _sigmoid(gg)
            silg = gg * sg
            m0 = u * sv
            y_s[rows, D_ATTN:D_MIX] = _mx(m0 * silg)
            dm = dy_s[rows, D_ATTN:D_MIX]
            dm0 = dm * silg
            dpb_ref[rows, c_gg] = (dm * m0 * (sg * (1.0 + gg * (1.0 - sg)))).astype(dpb_ref.dtype)
            dpb_ref[rows, c_u] = (dm0 * sv).astype(dpb_ref.dtype)
            dsv = dm0 * u
            dsv_s[rows, :] = _mx(dsv)
            dbs_ref[...] += dsv
            return carry

        lax.fori_loop(0, nb, gating, 0)

        def spatial_bwd(n, carry):
            rows = pl.ds(pl.multiple_of(n * BLK, BLK), BLK)
            for j in range(N_PAIRS):
                cols = slice(j * LANES, (j + 1) * LANES)
                dsv = dsv_s[rows, cols]
                dvn_s[rows, cols] = jnp.where(lo, _dot(wst_ref[2 * j], dsv), _dot(wst_ref[2 * j + 1], dsv))
            return carry

        lax.fori_loop(0, nb, spatial_bwd, 0)
        zero = jnp.zeros((BLK, LANES), MXU_DTYPE)
        for j in range(N_PAIRS):
            cols = slice(j * LANES, (j + 1) * LANES)
            chunks = [dsv_s[n * BLK : (n + 1) * BLK, cols] for n in range(nb)]
            vn_all = jnp.concatenate([vnb_s[n * BLK : (n + 1) * BLK, cols] for n in range(nb)], axis=1)
            dws_ref[2 * j] += _dot_nt(jnp.concatenate([jnp.where(lo, c, zero) for c in chunks], axis=1), vn_all)
            dws_ref[2 * j + 1] += _dot_nt(jnp.concatenate([jnp.where(lo, zero, c) for c in chunks], axis=1), vn_all)
        for j in range(D_GM // WIDE):
            cols = slice(j * WIDE, (j + 1) * WIDE)
            dpb_ref[:, D_ATTN + D_GM + j * WIDE : D_ATTN + D_GM + (j + 1) * WIDE] = _half_rms_bwd(
                dvn_s[:, cols], vn_s[:, cols], rv_s[:, cols], ones
            ).astype(dpb_ref.dtype)
        g_ref[...] += _dot_tn(y_s[...], _mx(dxv))

        @pl.when(pl.program_id(0) == nt - 1)
        def _():
            gv = g_ref[...]
            dwo_ref[...] = (gv * gate_ref[...]).astype(dwo_ref.dtype)
            dgate_ref[...] = _group_rows(gv * wo_ref[...].astype(F32))

    return pl.pallas_call(
        body,
        name=name,
        grid=(seq // tile,),
        in_specs=[
            _row_spec(tile, d),
            _row_spec(tile, D_REST),
            _row_spec(tile, D_ATTN),
            _row_spec(tile, D_GM),
            _full_spec((1, d)),
            _full_spec((D_MIX, d)),
            _full_spec((N_GROUPS, BLK, BLK)),
        ],
        out_specs=[
            _row_spec(tile, D_REST),
            _row_spec(tile, D_ATTN),
            _full_spec((D_MIX, d)),
            _full_spec((SUBLANES, d)),
            _full_spec((N_GROUPS, BLK, BLK)),
            _full_spec((BLK, D_GM)),
        ],
        out_shape=[
            jax.ShapeDtypeStruct((seq, D_REST), MXU_DTYPE),
            jax.ShapeDtypeStruct((seq, D_ATTN), F32),
            jax.ShapeDtypeStruct((D_MIX, d), jnp.bfloat16),
            jax.ShapeDtypeStruct((SUBLANES, d), F32),
            jax.ShapeDtypeStruct((N_GROUPS, BLK, BLK), F32),
            jax.ShapeDtypeStruct((BLK, D_GM), F32),
        ],
        scratch_shapes=[
            pltpu.VMEM((D_MIX, d), F32),
            pltpu.VMEM((tile, D_MIX), MXU_DTYPE),
            pltpu.VMEM((tile, D_MIX), F32),
            pltpu.VMEM((tile, D_GM), F32),
            pltpu.VMEM((tile, D_GM), F32),
            pltpu.VMEM((tile, D_GM), MXU_DTYPE),
            pltpu.VMEM((tile, D_GM), MXU_DTYPE),
            pltpu.VMEM((tile, D_GM), F32),
        ],
        compiler_params=_params(("arbitrary",)),
    )(dxn, pb, o, sv, gate, w_out, w_s_t)


def _attn_bwd(pa, o, do, probs, p_sink, q_gain2, k_gain2, name, scatter=()):
    seq = pa.shape[0]
    tile = min(TOKEN_TILE, seq)
    nb = tile // BLK
    nt = seq // tile
    ext = tile + 2 * BLK
    n_ride = len(scatter)
    riding = n_ride > 0

    def body(qkv_ref, kvp_ref, kvn_ref, o_ref, do_ref, p_ref, psink_ref, qg_ref, kg_ref, *rest):
        i = pl.program_id(0)
        blocks, rest = rest[:n_ride], rest[n_ride:]
        dq_ref, dkv_ref, hp_ref, hn_ref, dqg_ref, dkg_ref, dsk_ref = rest[:7]
        landing, rest = rest[7 : 7 + n_ride], rest[7 + n_ride :]
        (qs, dos, qhat_s, rq_s, ks, kr, vs, vr, khat_s, rk_s, dqn_s, dka, dva, dp_scr, ds_scr) = rest[:15]
        if riding:
            start, finish = _scatter_stages(blocks, landing, *rest[15:])
            at_start, _, at_finish = _rider_steps(nt)
            pl.when(i == at_start)(start)

        @pl.when(i == 0)
        def _():
            dqg_ref[...] = jnp.zeros_like(dqg_ref)
            dkg_ref[...] = jnp.zeros_like(dkg_ref)
            dsk_ref[...] = jnp.zeros_like(dsk_ref)

        ones = _half_ones()
        lo = _lane_lo(BLK)
        lo_t = _lane_lo(tile)
        lo_c = _lane_lo(ROW_CHUNK)
        qg = qg_ref[...] * Q_SCALE
        kg = kg_ref[...]
        _stage_keys(kvp_ref, qkv_ref, kvn_ref, kg, ones, tile, ks, kr, vs, vr, khat_s, rk_s)
        head_lane = lax.broadcasted_iota(jnp.int32, (tile, LANES), 1)
        d_rows = jnp.zeros((tile, LANES), F32)
        for j in range(N_PAIRS):
            cols = slice(j * LANES, (j + 1) * LANES)
            qhat, rq = _half_rms(qkv_ref[:, cols], ones)
            qhat_s[:, cols] = qhat
            rq_s[:, cols] = rq
            _stage_queries(qhat * qg, lo_t, j, nb, qs)
            dov = do_ref[:, cols]
            _stage_queries(dov, lo_t, j, nb, dos)
            d_pair = _half_sum(dov * o_ref[:, cols], ones)
            d_rows = jnp.where(head_lane == 2 * j, d_pair, d_rows)
            d_rows = jnp.where(head_lane == 2 * j + 1, pltpu.roll(d_pair, HEAD_DIM, 1), d_rows)
        dsk_ref[...] -= _group_rows(psink_ref[...] * d_rows)
        dka[...] = jnp.zeros_like(dka)
        dva[...] = jnp.zeros_like(dva)

        def block(n, carry):
            r0 = pl.multiple_of(n * BLK, BLK)
            krows = pl.ds(r0, 3 * BLK)
            for v in range(2):
                dp_scr[v] = _dot_nt(dos[n, v], (vr if v else vs)[krows, :])
            for h in range(N_HEADS):
                v, slot = HEAD_SLOT[h]
                j, a = divmod(h, 2)
                cols = slice(j * LANES, (j + 1) * LANES)
                for rc in range(0, BLK, ROW_CHUNK):
                    rows = slice(slot * BLK + rc, slot * BLK + rc + ROW_CHUNK)
                    trows = pl.ds(pl.multiple_of(r0 + rc, ROW_CHUNK), ROW_CHUNK)
                    prod = do_ref[trows, cols] * o_ref[trows, cols]
                    prod = jnp.where(lo_c, prod, 0.0) if a == 0 else jnp.where(lo_c, 0.0, prod)
                    dcol = jnp.sum(prod, axis=-1, keepdims=True)
                    ds_scr[v, rows, :] = _mx(p_ref[n, v, rows, :].astype(F32) * (dp_scr[v, rows, :] - dcol))
            dqv = []
            for v in range(2):
                dqv.append(_dot(ds_scr[v], (kr if v else ks)[krows, :]))
                dka[v, krows, :] += _dot_tn(ds_scr[v], qs[n, v])
                dva[v, krows, :] += _dot_tn(p_ref[n, v], dos[n, v])
            for j in range(N_PAIRS):
                dqn_s[pl.ds(r0, BLK), j * LANES : (j + 1) * LANES] = _unstack_pair(dqv, j, lo)
            return carry

        lax.fori_loop(0, nb, block, 0)
        for j in range(N_PAIRS):
            cols = slice(j * LANES, (j + 1) * LANES)
            dqn = dqn_s[:, cols]
            qhat = qhat_s[:, cols]
            dqg_ref[:, cols] += _group_rows(dqn * qhat) * Q_SCALE
            dq_ref[:, cols] = _half_rms_bwd(dqn * qg, qhat, rq_s[:, cols], ones).astype(dq_ref.dtype)
        dkn = dka[0] + pltpu.roll(dka[1], HEAD_DIM, 1)
        khat = khat_s[...]
        dkg_ref[...] += _group_rows(dkn * khat)
        dk = _half_rms_bwd(dkn * kg, khat, rk_s[...], ones)
        dv = dva[0] + pltpu.roll(dva[1], HEAD_DIM, 1)
        hp_ref[:, 0:D_KV] = dk[0:BLK]
        hp_ref[:, D_KV : 2 * D_KV] = dv[0:BLK]
        dkv_ref[:, 0:D_KV] = dk[BLK : BLK + tile]
        dkv_ref[:, D_KV : 2 * D_KV] = dv[BLK : BLK + tile]
        hn_ref[:, 0:D_KV] = dk[BLK + tile : ext]
        hn_ref[:, D_KV : 2 * D_KV] = dv[BLK + tile : ext]
        if riding:
            pl.when(i == at_finish)(finish)

    prev, nxt = _halo_specs(tile, seq)
    vec = _full_spec((1, LANES))
    halo = pl.BlockSpec((None, BLK, 2 * D_KV), lambda i: (i, 0, 0))
    return pl.pallas_call(
        body,
        name=name,
        grid=(nt,),
        in_specs=[
            _row_spec(tile, D_QKV),
            prev,
            nxt,
            _row_spec(tile, D_ATTN),
            _row_spec(tile, D_ATTN),
            pl.BlockSpec((nb, 2, STACK, 3 * BLK), lambda i: (i, 0, 0, 0)),
            _row_spec(tile, LANES),
            vec,
            vec,
        ]
        + [HBM_SPEC] * n_ride,
        out_specs=[
            _row_spec(tile, D_ATTN),
            _row_spec(tile, 2 * D_KV),
            halo,
            halo,
            _full_spec((SUBLANES, D_ATTN)),
            _full_spec((SUBLANES, LANES)),
            _full_spec((SUBLANES, LANES)),
        ]
        + [HBM_SPEC] * n_ride,
        out_shape=[
            jax.ShapeDtypeStruct((seq, D_ATTN), MXU_DTYPE),
            jax.ShapeDtypeStruct((seq, 2 * D_KV), F32),
            jax.ShapeDtypeStruct((nt, BLK, 2 * D_KV), F32),
            jax.ShapeDtypeStruct((nt, BLK, 2 * D_KV), F32),
            jax.ShapeDtypeStruct((SUBLANES, D_ATTN), F32),
            jax.ShapeDtypeStruct((SUBLANES, LANES), F32),
            jax.ShapeDtypeStruct((SUBLANES, LANES), F32),
        ]
        + _landing_shapes(scatter),
        scratch_shapes=[
            pltpu.VMEM((nb, 2, STACK, LANES), MXU_DTYPE),
            pltpu.VMEM((nb, 2, STACK, LANES), MXU_DTYPE),
            pltpu.VMEM((tile, D_ATTN), F32),
            pltpu.VMEM((tile, D_ATTN), F32),
            pltpu.VMEM((ext, LANES), MXU_DTYPE),
            pltpu.VMEM((ext, LANES), MXU_DTYPE),
            pltpu.VMEM((ext, LANES), MXU_DTYPE),
            pltpu.VMEM((ext, LANES), MXU_DTYPE),
            pltpu.VMEM((ext, LANES), F32),
            pltpu.VMEM((ext, LANES), F32),
            pltpu.VMEM((tile, D_ATTN), F32),
            pltpu.VMEM((2, ext, LANES), F32),
            pltpu.VMEM((2, ext, LANES), F32),
            pltpu.VMEM((2, STACK, 3 * BLK), F32),
            pltpu.VMEM((2, STACK, 3 * BLK), MXU_DTYPE),
        ]
        + _rider_sems(n_ride),
        compiler_params=_params(("arbitrary",)),
    )(pa, pa, pa, o, do, probs, p_sink, q_gain2, k_gain2, *scatter)


def _halo_in_specs(tile, nt):
    from_prev = pl.BlockSpec((None, BLK, 2 * D_KV), lambda i: (jnp.maximum(i - 1, 0), 0, 0))
    from_next = pl.BlockSpec((None, BLK, 2 * D_KV), lambda i: (jnp.minimum(i + 1, nt - 1), 0, 0))
    return from_prev, from_next


def _landing_shapes(scatter):
    return [jax.ShapeDtypeStruct((N_DEV,) + b.shape[2:], b.dtype) for b in scatter]


def _rider_sems(n_ride):
    if not n_ride:
        return []
    return [pltpu.SemaphoreType.DMA((7 * n_ride,)), pltpu.SemaphoreType.DMA((7 * n_ride,)), pltpu.SemaphoreType.DMA((n_ride,))]


def _proj_bwd_dx(x, dxn, dq, dkvb, dpb, w_in_t, gain, scale1, name):
    seq, d = x.shape
    tile = min(TOKEN_TILE, seq)

    def row(width):
        return _row_spec(tile, width)

    def body(x_ref, dxn_ref, dq_ref, dkvb_ref, dpb_ref, wt_ref, g_ref, s1_ref, dx_ref, c0_ref, c1_ref):
        @pl.when(pl.program_id(0) == 0)
        def _():
            c0_ref[...] = jnp.zeros_like(c0_ref)
            c1_ref[...] = jnp.zeros_like(c1_ref)

        dh = (
            _dot(dq_ref[...], wt_ref[0:D_ATTN, :])
            + _dot(dkvb_ref[...], wt_ref[D_ATTN:D_QKV, :])
            + _dot(dpb_ref[...], wt_ref[D_QKV:D_IN, :])
        )
        xv = x_ref[...]
        r = lax.rsqrt(jnp.mean(xv * xv, axis=-1, keepdims=True) + EPS)
        xn = xv * r
        c0_ref[...] += _group_rows(dh)
        c1_ref[...] += _group_rows(dh * xn)
        dxn_ = dh * (g_ref[...] * s1_ref[...])
        dx_ref[...] = dxn_ref[...] + r * (dxn_ - xn * jnp.mean(xn * dxn_, axis=-1, keepdims=True))

    vec = _full_spec((1, d))
    return pl.pallas_call(
        body,
        name=name,
        grid=(seq // tile,),
        in_specs=[row(d), row(d), row(D_ATTN), row(2 * D_KV), row(D_REST), _full_spec((D_IN, d)), vec, vec],
        out_specs=[row(d), _full_spec((SUBLANES, d)), _full_spec((SUBLANES, d))],
        out_shape=[
            jax.ShapeDtypeStruct((seq, d), F32),
            jax.ShapeDtypeStruct((SUBLANES, d), F32),
            jax.ShapeDtypeStruct((SUBLANES, d), F32),
        ],
        compiler_params=_params(("arbitrary",)),
    )(x, dxn, dq, dkvb, dpb, w_in_t, gain, scale1)


def _proj_bwd_dw(x, gain, scale1, shift, dq, dkv, halo_prev, halo_next, dpb, name, gather=()):
    seq, d = x.shape
    tile = min(TOKEN_TILE, seq)
    nt = seq // tile
    assert tile >= 2 * BLK
    n_ride = len(gather)

    def body(x_ref, g_ref, s1_ref, sh_ref, dq_ref, dkv_ref, hn_ref, hp_ref, dpb_ref, *rest):
        i = pl.program_id(0)
        sources, rest = rest[:n_ride], rest[n_ride:]
        dw_ref, dkvb_ref = rest[:2]
        gathered, (acc, *sems) = rest[2 : 2 + n_ride], rest[2 + n_ride :]
        after_compute = _gather_rider(sources, gathered, sems, i, nt) if n_ride else None

        @pl.when(i == 0)
        def _():
            acc[...] = jnp.zeros_like(acc)

        top = dkv_ref[0:BLK, :] + jnp.where(i > 0, hn_ref[...], 0.0)
        bot = dkv_ref[tile - BLK : tile, :] + jnp.where(i < nt - 1, hp_ref[...], 0.0)
        dkvb_ref[0:BLK, :] = top.astype(dkvb_ref.dtype)
        dkvb_ref[tile - BLK : tile, :] = bot.astype(dkvb_ref.dtype)
        if tile > 2 * BLK:
            dkvb_ref[BLK : tile - BLK, :] = dkv_ref[BLK : tile - BLK, :].astype(dkvb_ref.dtype)
        xv = x_ref[...]
        r = lax.rsqrt(jnp.mean(xv * xv, axis=-1, keepdims=True) + EPS)
        h = _mx((xv * r) * g_ref[...] * s1_ref[...] + sh_ref[...])
        acc[0:D_ATTN, :] += _dot_tn(dq_ref[...], h)
        acc[D_ATTN:D_QKV, :] += _dot_tn(dkvb_ref[...], h)
        acc[D_QKV:D_IN, :] += _dot_tn(dpb_ref[...], h)

        @pl.when(i == nt - 1)
        def _():
            dw_ref[...] = acc[...].astype(dw_ref.dtype)

        if n_ride:
            after_compute()

    from_prev, from_next = _halo_in_specs(tile, nt)
    vec = _full_spec((1, d))
    return pl.pallas_call(
        body,
        name=name,
        grid=(nt,),
        in_specs=[
            _row_spec(tile, d),
            vec,
            vec,
            vec,
            _row_spec(tile, D_ATTN),
            _row_spec(tile, 2 * D_KV),
            from_prev,
            from_next,
            _row_spec(tile, D_REST),
        ]
        + [HBM_SPEC] * n_ride,
        out_specs=[_full_spec((D_IN, d)), _row_spec(tile, 2 * D_KV)] + [HBM_SPEC] * n_ride,
        out_shape=[jax.ShapeDtypeStruct((D_IN, d), jnp.bfloat16), jax.ShapeDtypeStruct((seq, 2 * D_KV), MXU_DTYPE)]
        + _gathered_shapes(gather),
        scratch_shapes=[pltpu.VMEM((D_IN, d), F32)] + _rider_sems(n_ride),
        compiler_params=_params(("arbitrary",)),
    )(x, gain, scale1, shift, dq, dkv, halo_next, halo_prev, dpb, *gather)


def _adamw_math(w, g, m, v):
    m = ADAM_B1 * m + (1.0 - ADAM_B1) * g
    v = ADAM_B2 * v + (1.0 - ADAM_B2) * (g * g)
    m_hat = m / (1.0 - ADAM_B1**ADAM_STEP)
    v_hat = v / (1.0 - ADAM_B2**ADAM_STEP)
    delta = -ADAM_LR * (m_hat / (jnp.sqrt(v_hat) + ADAM_EPS) + ADAM_WD * w)
    return delta, m, v


def _small_update(gathered, gathered_ws, w, m, v, ws, m_ws, v_ws):
    def body(ga_ref, gws_ref, w_ref, m_ref, v_ref, ws_ref, mws_ref, vws_ref, *outs):
        for src, refs, out in ((ga_ref, (w_ref, m_ref, v_ref), outs[0:4]), (gws_ref, (ws_ref, mws_ref, vws_ref), outs[4:8])):
            g = src[0].astype(F32)
            for j in range(1, N_DEV):
                g = g + src[j].astype(F32)
            out[0][...] = g
            out[1][...], out[2][...], out[3][...] = _adamw_math(refs[0][...], g, refs[1][...], refs[2][...])

    shapes = [jax.ShapeDtypeStruct(w.shape, F32)] * 4 + [jax.ShapeDtypeStruct(ws.shape, F32)] * 4
    return pl.pallas_call(
        body,
        name="small_update",
        in_specs=[VMEM_SPEC] * 8,
        out_specs=[VMEM_SPEC] * 8,
        out_shape=shapes,
        compiler_params=_params(),
    )(gathered, gathered_ws, w, m, v, ws, m_ws, v_ws)


def _ada_update(c_all, d_ada_cols, w, m, v):
    n_layers = w.shape[0]

    def body(c_ref, da_ref, w_ref, m_ref, v_ref, g_ref, d_ref, mo_ref, vo_ref):
        cv = c_ref[...]
        cond = cv * _sigmoid(cv)
        for l in range(n_layers):
            g = lax.dot_general(
                cond, da_ref[l], (((0,), (0,)), ((), ())), preferred_element_type=F32, precision=lax.Precision.HIGHEST
            )
            g_ref[l] = g
            d_ref[l], mo_ref[l], vo_ref[l] = _adamw_math(w_ref[l], g, m_ref[l], v_ref[l])

    return pl.pallas_call(
        body,
        name="ada_update",
        in_specs=[VMEM_SPEC] * 5,
        out_specs=[VMEM_SPEC] * 4,
        out_shape=[jax.ShapeDtypeStruct(w.shape, F32)] * 4,
        compiler_params=_params(),
    )(c_all, d_ada_cols, w, m, v)


def _position():
    return lax.axis_index("x"), lax.axis_index("y"), lax.axis_index("c")


def _flip(pos, k):
    x, y, c = pos
    return (1 - x if k & 4 else x, 1 - y if k & 2 else y, 1 - c if k & 1 else c)


def _index(pos):
    x, y, c = pos
    return 4 * x + 2 * y + c


def _remote(src, dst, send_sem, recv_sem, to):
    return pltpu.make_async_remote_copy(
        src_ref=src, dst_ref=dst, send_sem=send_sem, recv_sem=recv_sem, device_id=to, device_id_type=MESH_ID
    )


def _all_gather_stages(slots, send_sems, recv_sems, sources=None, local_sems=None):
    me = _position()
    sibling = _flip(me, 1)
    others = (4, 2, 6)
    arrays = range(len(slots))

    def copy(t, k, block, to, own=False):
        slot = slots[t](_index(block))
        src = sources[t] if own and sources is not None else slot
        return _remote(src, slot, send_sems.at[7 * t + k], recv_sems.at[7 * t + k], to)

    def first(t):
        return [copy(t, 0, me, sibling, own=True)] + [copy(t, 1 + j, me, _flip(me, f), own=True) for j, f in enumerate(others)]

    def passed(t, j):
        return copy(t, 4 + j, _flip(me, others[j]), sibling)

    def local(t):
        return pltpu.make_async_copy(sources[t], slots[t](_index(me)), local_sems.at[t])

    def start():
        for t in arrays:
            if sources is not None:
                local(t).start()
            for cp in first(t):
                cp.start()

    def forward():
        for j, f in enumerate(others):
            for t in arrays:
                copy(t, 1 + j, _flip(me, f), me).wait_recv()
                passed(t, j).start()

    def finish():
        for t in arrays:
            copy(t, 0, sibling, me).wait_recv()
            for j, f in enumerate(others):
                copy(t, 4 + j, _flip(sibling, f), me).wait_recv()
        for t in arrays:
            for cp in first(t) + [passed(t, j) for j in range(len(others))]:
                cp.wait_send()
            if sources is not None:
                local(t).wait()

    return start, forward, finish


def _two_level_all_gather(slots, send_sems, recv_sems, between=None):
    start, forward, finish = _all_gather_stages(slots, send_sems, recv_sems)
    start()
    if between is not None:
        between()
    forward()
    finish()


def _row_block(ref, rows):
    return lambda j: ref.at[pl.ds(pl.multiple_of(j * rows, 16), rows), :]


def _scatter_stages(blocks, landing, send_sems, recv_sems, local_sems):
    me = _position()
    my = _index(me)
    arrays = range(len(blocks))

    def copy(t, k):
        px, py, pc = to = _flip(me, k)
        return _remote(blocks[t].at[2 * px + py, pc], landing[t].at[my], send_sems.at[7 * t + k - 1], recv_sems.at[7 * t + k - 1], to)

    def arrival(t, k):
        slot = landing[t].at[_index(_flip(me, k))]
        return _remote(slot, slot, send_sems.at[7 * t + k - 1], recv_sems.at[7 * t + k - 1], _flip(me, k))

    def local(t):
        x, y, c = me
        return pltpu.make_async_copy(blocks[t].at[2 * x + y, c], landing[t].at[my], local_sems.at[t])

    def start():
        for t in arrays:
            local(t).start()
            for k in range(1, N_DEV):
                copy(t, k).start()

    def finish():
        for t in arrays:
            for k in range(1, N_DEV):
                arrival(t, k).wait_recv()
        for t in arrays:
            for k in range(1, N_DEV):
                copy(t, k).wait_send()
            local(t).wait()

    return start, finish


def _ada_exchange(c_ref, w_ref, call_ref, parts_ref, sbuf, sem_s1, sem_r1, sem_s2, sem_r2):
    d = c_ref.shape[-1]
    n_layers = w_ref.shape[0]
    me = _position()
    my = _index(me)
    call_ref[my] = jnp.broadcast_to(c_ref[...], (SUBLANES, d))
    mine = call_ref.at[my]
    first = [_remote(mine, mine, sem_s1.at[k - 1], sem_r1.at[k - 1], _flip(me, k)) for k in range(1, N_DEV)]
    for cp in first:
        cp.start()
    for k in range(1, N_DEV):
        theirs = call_ref.at[_index(_flip(me, k))]
        _remote(theirs, theirs, sem_s1.at[k - 1], sem_r1.at[k - 1], _flip(me, k)).wait_recv()
    cv = call_ref[...].reshape(N_DEV * SUBLANES, d)
    cond = cv * _sigmoid(cv)
    for l in range(n_layers):
        rows = jnp.dot(cond, w_ref[l], preferred_element_type=F32, precision=lax.Precision.HIGHEST)
        for b in range(N_DEV):
            sbuf[b, l] = rows[b * SUBLANES : (b + 1) * SUBLANES]
    parts_ref[my] = sbuf[my]
    second = []
    for k in range(1, N_DEV):
        to = _flip(me, k)
        second.append(_remote(sbuf.at[_index(to)], parts_ref.at[my], sem_s2.at[k - 1], sem_r2.at[k - 1], to))
    for cp in second:
        cp.start()
    for k in range(1, N_DEV):
        theirs = parts_ref.at[_index(_flip(me, k))]
        _remote(theirs, theirs, sem_s2.at[k - 1], sem_r2.at[k - 1], _flip(me, k)).wait_recv()
    for cp in first + second:
        cp.wait_send()


def _gather_weights(w_in_t, w_out, c_row, w_ada):
    n_layers, rows_in, d = w_in_t.shape
    width = w_ada.shape[2]

    def body(wi_ref, wo_ref, c_ref, wa_ref, gi_ref, si_ref, so_ref, call_ref, parts_ref, sbuf, send_sems, recv_sems, *ada_sems):
        my = _index(_position())
        si_ref[...] = wi_ref[...].astype(si_ref.dtype)
        so_ref[...] = wo_ref[...].astype(so_ref.dtype)
        gi_ref[pl.ds(pl.multiple_of(my * rows_in, 16), rows_in), :] = si_ref[0]
        _two_level_all_gather(
            (_row_block(gi_ref, rows_in),),
            send_sems,
            recv_sems,
            between=functools.partial(_ada_exchange, c_ref, wa_ref, call_ref, parts_ref, sbuf, *ada_sems),
        )

    return pl.pallas_call(
        body,
        name="gather_weights",
        in_specs=[VMEM_SPEC] * 4,
        out_specs=[VMEM_SPEC] * 5,
        out_shape=[
            jax.ShapeDtypeStruct((N_DEV * rows_in, d), MXU_DTYPE),
            jax.ShapeDtypeStruct(w_in_t.shape, MXU_DTYPE),
            jax.ShapeDtypeStruct(w_out.shape, MXU_DTYPE),
            jax.ShapeDtypeStruct((N_DEV, SUBLANES, d), F32),
            jax.ShapeDtypeStruct((N_DEV, n_layers, SUBLANES, width), F32),
        ],
        scratch_shapes=[
            pltpu.VMEM((N_DEV, n_layers, SUBLANES, width), F32),
            pltpu.SemaphoreType.DMA((7,)),
            pltpu.SemaphoreType.DMA((7,)),
        ]
        + [pltpu.SemaphoreType.DMA((N_DEV - 1,))] * 4,
        compiler_params=_params(),
    )(w_in_t, w_out, c_row, w_ada)


def _gather_small(packed, adam=()):
    n_adam = len(adam)

    def body(p_ref, *rest):
        quads = [rest[4 * t : 4 * t + 4] for t in range(n_adam)]
        rest = rest[4 * n_adam :]
        g_ref = rest[0]
        results = [rest[1 + 3 * t : 4 + 3 * t] for t in range(n_adam)]
        send_sems, recv_sems = rest[1 + 3 * n_adam :]
        g_ref[_index(_position())] = p_ref[...]

        def updates():
            for (w_ref, gr_ref, m_ref, v_ref), (d_ref, mo_ref, vo_ref) in zip(quads, results):
                d_ref[...], mo_ref[...], vo_ref[...] = _adamw_math(w_ref[...], gr_ref[...], m_ref[...], v_ref[...])

        _two_level_all_gather((lambda j: g_ref.at[j],), send_sems, recv_sems, between=updates)

    return pl.pallas_call(
        body,
        name="gather_small",
        in_specs=[VMEM_SPEC] * (1 + 4 * n_adam),
        out_specs=[VMEM_SPEC] * (1 + 3 * n_adam),
        out_shape=[jax.ShapeDtypeStruct((N_DEV,) + packed.shape, F32)]
        + [jax.ShapeDtypeStruct(q[0].shape, F32) for q in adam for _ in range(3)],
        scratch_shapes=[pltpu.SemaphoreType.DMA((7,)), pltpu.SemaphoreType.DMA((7,))],
        compiler_params=_params(),
    )(packed, *[a for q in adam for a in q])


def _scatter_finish(landed, name, own=()):
    n = len(landed)

    def body(*refs):
        if own:
            x, y, c = _position()
            my = _index((x, y, c))
        for t, (src, out) in enumerate(zip(refs[:n], refs[n + len(own) :])):
            g = None
            for j in range(N_DEV):
                part = src[j].astype(F32)
                if own:
                    part = jnp.where(j == my, refs[n + t][2 * x + y, c].astype(F32), part)
                g = part if g is None else g + part
            out[...] = g

    return pl.pallas_call(
        body,
        name=name,
        in_specs=[VMEM_SPEC] * (n + len(own)),
        out_specs=[VMEM_SPEC] * n,
        out_shape=[jax.ShapeDtypeStruct(a.shape[1:], F32) for a in landed],
        compiler_params=_params(),
    )(*landed, *own)


SEM_SPEC = pl.BlockSpec(memory_space=pltpu.SEMAPHORE)
SPLIT_COPY = pltpu.SideEffectType.DATAFLOW_SIDE_EFFECTING


def _scatter_start(blocks, name):
    land_shape = (N_DEV,) + blocks.shape[2:]

    def body(blocks_ref, land_ref, send_sems, recv_sems, blocks_thru, land_thru, token):
        me = _position()
        my = _index(me)
        for k in range(1, N_DEV):
            px, py, pc = to = _flip(me, k)
            _remote(blocks_ref.at[2 * px + py, pc], land_ref.at[my], send_sems.at[k - 1], recv_sems.at[k - 1], to).start()
        token[...] = jnp.zeros_like(token)

    return pl.pallas_call(
        body,
        name=name,
        in_specs=(HBM_SPEC, HBM_SPEC),
        out_specs=(SEM_SPEC, SEM_SPEC, HBM_SPEC, HBM_SPEC, VMEM_SPEC),
        out_shape=(
            pltpu.SemaphoreType.DMA((N_DEV - 1,)),
            pltpu.SemaphoreType.DMA((N_DEV - 1,)),
            pltpu.HBM(blocks.shape, blocks.dtype),
            pltpu.HBM(land_shape, blocks.dtype),
            jax.ShapeDtypeStruct((SUBLANES, LANES), F32),
        ),
        input_output_aliases={0: 2, 1: 3},
        compiler_params=pltpu.CompilerParams(has_side_effects=SPLIT_COPY),
    )(pltpu.with_memory_space_constraint(blocks, pltpu.HBM), pltpu.with_memory_space_constraint(lax.empty(land_shape, blocks.dtype), pltpu.HBM))


def _scatter_wait(send_sems, recv_sems, blocks_thru, land_thru, after, name):
    def body(blocks_ref, land_ref, send_sems, recv_sems, after_ref, blocks_dead, got_ref):
        me = _position()
        my = _index(me)
        for k in range(1, N_DEV):
            px, py, pc = to = _flip(me, k)
            _remote(blocks_ref.at[2 * px + py, pc], land_ref.at[my], send_sems.at[k - 1], recv_sems.at[k - 1], to).wait_send()
        for k in range(1, N_DEV):
            slot = land_ref.at[_index(_flip(me, k))]
            _remote(slot, slot, send_sems.at[k - 1], recv_sems.at[k - 1], _flip(me, k)).wait_recv()

    return pl.pallas_call(
        body,
        name=name,
        in_specs=(HBM_SPEC, HBM_SPEC, SEM_SPEC, SEM_SPEC, pl.BlockSpec(memory_space=pl.ANY)),
        out_specs=(HBM_SPEC, HBM_SPEC),
        out_shape=(pltpu.HBM(blocks_thru.shape, blocks_thru.dtype), pltpu.HBM(land_thru.shape, land_thru.dtype)),
        input_output_aliases={0: 0, 1: 1},
        compiler_params=pltpu.CompilerParams(has_side_effects=SPLIT_COPY),
    )(blocks_thru, land_thru, send_sems, recv_sems, after)


def _pack_rows(parts):
    rows, offsets, at = [], [], 0
    for p in parts:
        flat = p.reshape(-1)
        n = -(-flat.shape[0] // (SUBLANES * LANES)) * SUBLANES
        rows.append(jnp.pad(flat, (0, n * LANES - flat.shape[0])).reshape(n, LANES))
        offsets.append(at)
        at += n
    return jnp.concatenate(rows, axis=0), offsets


def _unpack_rows(packed, offsets, shapes):
    out = []
    for off, shape in zip(offsets, shapes):
        size = 1
        for s in shape:
            size *= s
        n = -(-size // (SUBLANES * LANES)) * SUBLANES
        out.append(packed[off : off + n].reshape(-1)[:size].reshape(shape))
    return out


def kernel(x, c, w_ada, b_ada, norm_gain, w_in, q_gain, k_gain, sink, w_s, b_s, w_out, loss_target, m_w_ada, m_b_ada, m_norm_gain, m_w_in, m_q_gain, m_k_gain, m_sink, m_w_s, m_b_s, m_w_out, v_w_ada, v_b_ada, v_norm_gain, v_w_in, v_q_gain, v_k_gain, v_sink, v_w_s, v_b_s, v_w_out):
    seq, d = x.shape[1], x.shape[2]
    n_layers = w_in.shape[0]
    w_cols = w_in.shape[2]
    ada_cols = w_ada.shape[2]
    my = _index(_position())
    xs = x.reshape(seq, d)
    target = loss_target.reshape(seq, d)

    rows_first = lambda a: a.transpose(0, 2, 1)
    w_in_t0, shard_in, shard_out, c_all, ada_parts = _gather_weights(rows_first(w_in), w_out, c, w_ada)
    w_in_ts, w_outs = [w_in_t0], []
    ada = ada_parts[:, :, 0, :].transpose(1, 0, 2).reshape(n_layers, 3 * d) + b_ada
    shift, scale1, gate = ada[:, None, 0:d], 1.0 + ada[:, None, d : 2 * d], ada[:, None, 2 * d : 3 * d]
    gain = norm_gain[:, None, :]

    w_s_m = w_s.astype(MXU_DTYPE)
    w_s_t = w_s_m.transpose(0, 1, 3, 2)
    b_st = jnp.repeat(b_s.transpose(0, 2, 1), HEAD_DIM, axis=2)
    q_gain2 = jnp.tile(q_gain, (1, 2))[:, None, :]
    k_gain2 = jnp.tile(k_gain, (1, 2))[:, None, :]

    xl, saved = xs, []
    for l in range(n_layers):
        last = l == n_layers - 1
        pa, pb = _ln_proj_fwd(xl, gain[l], scale1[l], shift[l], w_in_ts[l], f"ln_proj_fwd_{l}")
        wanted = ([shard_out[0]] if l == 0 else []) + ([] if last else [shard_out[l + 1], shard_in[l + 1]])
        o, probs, p_sink, *arrived = _attn_fwd(pa, q_gain2[l], k_gain2[l], sink[l], f"attn_fwd_{l}", gather=tuple(wanted))
        if not last:
            w_in_ts.append(arrived.pop())
        w_outs += arrived
        *out, sv = _mix_out_fwd(pb, o, xl, gate[l], w_outs[l], w_s_m[l], b_st[l], f"mix_out_fwd_{l}", target if last else None)
        saved.append((xl, pa, pb, o, probs, p_sink, sv))
        if last:
            dx, sq_err = out
        else:
            (xl,) = out

    g_w_in, g_w_out, small, d_ada_rows = [None] * n_layers, [None] * n_layers, [None] * n_layers, [None] * n_layers
    waiting = []
    d_ws_all = [None] * n_layers
    for l in reversed(range(n_layers)):
        x_l, pa, pb, o, probs, p_sink, sv = saved[l]
        dpb, do, dw_out, d_gate8, d_ws, d_bs = _mix_out_bwd(dx, pb, o, sv, gate[l], w_outs[l], w_s_t[l], f"mix_out_bwd_{l}")
        waiting.append((g_w_out, l, dw_out.reshape(4, 2, D_MIX // N_DEV, d)))
        riding, waiting = ([], waiting) if 0 < l == n_layers - 1 else (waiting, [])
        attn = _attn_bwd(
            pa, o, do, probs, p_sink, q_gain2[l], k_gain2[l], f"attn_bwd_{l}", scatter=tuple(b for _, _, b in riding)
        )
        dq, dkv, halo_prev, halo_next, d_qg, d_kg, d_sk = attn[:7]
        if riding:
            for (dest, layer, _), total in zip(riding, _scatter_finish(attn[7:], f"scatter_finish_{l}")):
                dest[layer] = total
        d_ws_all[l] = d_ws
        dw_args = (x_l, gain[l], scale1[l], shift[l], dq, dkv, halo_prev, halo_next, dpb, f"proj_bwd_dw_{l}")
        if l > 0:
            dw_in_t, dkvb = _proj_bwd_dw(*dw_args)
        else:
            d_ws_wire = jnp.stack(d_ws_all).reshape(-1, LANES).astype(jnp.bfloat16)
            dw_in_t, dkvb, gathered_ws = _proj_bwd_dw(*dw_args, gather=(d_ws_wire,))
        blocks_in = dw_in_t.reshape(4, 2, w_cols, d)
        if l > 0:
            waiting.append((g_w_in, l, blocks_in))
            dx, c0, c1 = _proj_bwd_dx(x_l, dx, dq, dkvb, dpb, w_in_ts[l], gain[l], scale1[l], f"proj_bwd_dx_{l}")
        else:
            *in_flight, token = _scatter_start(blocks_in, "scatter_start_in_0")
            dx, c0, c1 = _proj_bwd_dx(
                x_l, dx, dq, dkvb, dpb, w_in_ts[l], gain[l] + token[0, 0], scale1[l], f"proj_bwd_dx_{l}"
            )
            sent, landed = _scatter_wait(*in_flight, dx, "scatter_wait_in_0")
            g_w_in[l] = _scatter_finish((landed,), "scatter_finish_in_0", own=(sent,))[0]
        c0s, c1s = c0.sum(axis=0), c1.sum(axis=0)
        d_ada_rows[l] = jnp.concatenate([c0s, norm_gain[l] * c1s, d_gate8.sum(axis=0)])
        small[l] = (
            scale1[l, 0] * c1s,
            d_qg.sum(axis=0).reshape(N_HEADS, HEAD_DIM).sum(axis=0),
            d_kg.sum(axis=0).reshape(2, HEAD_DIM).sum(axis=0),
            d_sk.sum(axis=0)[0:N_HEADS],
            d_bs.reshape(BLK, N_GROUPS, HEAD_DIM).sum(axis=2).transpose(1, 0),
        )

    names = ("norm_gain", "q_gain", "k_gain", "sink", "b_s")
    stacked = [jnp.stack([small[l][t] for l in range(n_layers)]) for t in range(len(names))]
    d_ada = jnp.stack(d_ada_rows)
    packed, offsets = _pack_rows(stacked + [d_ada, sq_err[0, 0:1]])
    g_w_in_t, g_w_out = jnp.stack(g_w_in), jnp.stack(g_w_out)
    adam_in = (rows_first(w_in), g_w_in_t, rows_first(m_w_in), rows_first(v_w_in))
    gathered, *upd = _gather_small(packed, adam=(adam_in, (w_out, g_w_out, m_w_out, v_w_out)))
    gathered_ws = gathered_ws.reshape(N_DEV, -1, LANES)
    g_w_in = rows_first(g_w_in_t)
    upd_in, upd_out = [rows_first(u) for u in upd[0:3]], upd[3:6]
    no_weight = jnp.zeros((1,), F32)
    weights = (norm_gain, q_gain, k_gain, sink, b_s, b_ada, no_weight)
    moments_m = (m_norm_gain, m_q_gain, m_k_gain, m_sink, m_b_s, m_b_ada, no_weight)
    moments_v = (v_norm_gain, v_q_gain, v_k_gain, v_sink, v_b_s, v_b_ada, no_weight)
    w_pack, _ = _pack_rows(weights)
    m_pack, _ = _pack_rows(moments_m)
    v_pack, _ = _pack_rows(moments_v)
    shapes = [w.shape for w in weights]
    flat_ws = lambda a: a.reshape(-1, LANES)
    updated = _small_update(gathered, gathered_ws, w_pack, m_pack, v_pack, flat_ws(w_s), flat_ws(m_w_s), flat_ws(v_w_s))
    g_small, d_small, m_small, v_small = (_unpack_rows(p, offsets, shapes) for p in updated[0:4])
    ws_small = [p.reshape(w_s.shape) for p in updated[4:8]]
    loss = g_small[-1][0] * (0.5 / d)

    ada_off = offsets[-2]
    ada_n = -(-n_layers * 3 * d // (SUBLANES * LANES)) * SUBLANES
    d_ada_all = gathered[:, ada_off : ada_off + ada_n].reshape(N_DEV, -1)[:, : n_layers * 3 * d].reshape(N_DEV, n_layers, 3 * d)
    d_ada_cols = lax.dynamic_slice_in_dim(d_ada_all, my * ada_cols, ada_cols, axis=2)
    g_w_ada, *upd_ada = _ada_update(c_all[:, 0, :], d_ada_cols.transpose(1, 0, 2), w_ada, m_w_ada, v_w_ada)

    def ordered(ada_, in_, out_, small_, ws):
        ng, qg, kg, sk, bs, ba, _ = small_
        return (ada_, ba, ng, in_, qg, kg, sk, ws, bs, out_)

    grads = ordered(g_w_ada, g_w_in, g_w_out, g_small, ws_small[0])
    deltas = ordered(upd_ada[0], upd_in[0], upd_out[0], d_small, ws_small[1])
    new_m = ordered(upd_ada[1], upd_in[1], upd_out[1], m_small, ws_small[2])
    new_v = ordered(upd_ada[2], upd_in[2], upd_out[2], v_small, ws_small[3])
    return (loss, dx.reshape(x.shape), *grads, *deltas, *new_m, *new_v)
```

```python
import functools

import jax
import jax.numpy as jnp
from jax import lax
from jax.experimental import pallas as pl
from jax.experimental.pallas import tpu as pltpu

F32 = jnp.float32
MXU_DTYPE = jnp.bfloat16
MESH_ID = pl.DeviceIdType.MESH

N_DEV = 8
HEAD_DIM = 64
N_HEADS = 8
Q_PER_KV = 4
D_ATTN = 512
D_KV = 128
D_GM = 512
N_GROUPS = 8
D_MIX = D_ATTN + D_GM
BLK = 128
LANES = 128
SUBLANES = 8
N_PAIRS = D_ATTN // LANES
D_QKV = D_ATTN + 2 * D_KV
D_REST = D_ATTN + 3 * D_GM
D_IN = D_QKV + D_REST
EPS = 1e-6
NEG_INF = -1e30
ALIBI_SLOPES = tuple(2.0 ** (-8.0 * (h + 1) / N_HEADS) for h in range(N_HEADS))
Q_SCALE = 1.0 / 8.0

ADAM_LR = 0.001
ADAM_B1 = 0.9
ADAM_B2 = 0.999
ADAM_EPS = 1e-08
ADAM_WD = 0.01
ADAM_STEP = 10

TOKEN_TILE = 512
VMEM_LIMIT_BYTES = 56 * 1024 * 1024


def _params(semantics=None):
    return pltpu.CompilerParams(dimension_semantics=semantics, vmem_limit_bytes=VMEM_LIMIT_BYTES)


def _dot(a, b):
    return jnp.dot(a, b, preferred_element_type=F32)


def _dot_nt(a, b):
    return lax.dot_general(a, b, (((1,), (1,)), ((), ())), preferred_element_type=F32)


def _dot_tn(a, b):
    return lax.dot_general(a, b, (((0,), (0,)), ((), ())), preferred_element_type=F32)


def _mx(v):
    return v.astype(MXU_DTYPE)


def _lane_lo(rows):
    return lax.broadcasted_iota(jnp.int32, (rows, LANES), 1) < HEAD_DIM


def _half_ones(width=LANES):
    group_bits = HEAD_DIM.bit_length() - 1
    r = jnp.right_shift(lax.broadcasted_iota(jnp.int32, (width, width), 0), group_bits)
    c = jnp.right_shift(lax.broadcasted_iota(jnp.int32, (width, width), 1), group_bits)
    return jnp.where(r == c, 1.0, 0.0).astype(jnp.bfloat16)


WIDE = 2 * LANES


def _half_sum(v, ones):
    p1 = v.astype(jnp.bfloat16)
    p2 = (v - p1.astype(F32)).astype(jnp.bfloat16)
    return _dot(p1, ones) + _dot(p2, ones)


def _half_rms(v, ones):
    r = lax.rsqrt(_half_sum(v * v, ones) * (1.0 / HEAD_DIM) + EPS)
    return v * r, r


def _half_rms_bwd(dy, vhat, r, ones):
    return r * (dy - vhat * (_half_sum(vhat * dy, ones) * (1.0 / HEAD_DIM)))


def _group_rows(v):
    rows, n = v.shape
    return v.reshape(rows // SUBLANES, SUBLANES, n).sum(axis=0)


def _sigmoid(v):
    return 1.0 / (1.0 + jnp.exp(-v))


ROW_CHUNK = 32
VARIANT_HEADS = ((0, 2, 5, 7), (1, 3, 4, 6))
HEAD_SLOT = {h: (v, s) for v, heads in enumerate(VARIANT_HEADS) for s, h in enumerate(heads)}
STACK = Q_PER_KV * BLK


def _fill_attn_bias(bias_s):
    qi = lax.broadcasted_iota(jnp.int32, (BLK, 3 * BLK), 0)
    ci = lax.broadcasted_iota(jnp.int32, (BLK, 3 * BLK), 1)
    dist = jnp.abs(ci - BLK - qi)
    distf = dist.astype(F32)
    window = dist <= BLK
    for kind, seen in enumerate((window & (ci >= BLK), window, window & (ci < 2 * BLK))):
        for h in range(N_HEADS):
            bias_s[kind, h] = jnp.where(seen, -(ALIBI_SLOPES[h] * distf), NEG_INF)


def _block_kind(block, seq):
    assert seq >= 2 * BLK
    return jnp.where(block == 0, 0, jnp.where(block == seq // BLK - 1, 2, 1))


def _stage_queries(qn, lo_t, j, nb, qs):
    for a in range(2):
        v, slot = HEAD_SLOT[2 * j + a]
        qm = _mx(jnp.where(lo_t, qn, 0.0) if a == 0 else jnp.where(lo_t, 0.0, qn))
        for n in range(nb):
            qs[n, v, slot * BLK : (slot + 1) * BLK, :] = qm[n * BLK : (n + 1) * BLK]


def _unstack_pair(stacked, j, lo):
    (v0, s0), (v1, s1) = HEAD_SLOT[2 * j], HEAD_SLOT[2 * j + 1]
    return jnp.where(lo, stacked[v0][s0 * BLK : (s0 + 1) * BLK], stacked[v1][s1 * BLK : (s1 + 1) * BLK])


def _stage_keys(kvp_ref, qkv_ref, kvn_ref, kg, ones, tile, ks, kr, vs, vr, khat_s=None, rk_s=None):
    pieces = (
        (0, BLK, kvp_ref[:, 0:D_KV], kvp_ref[:, D_KV : 2 * D_KV]),
        (BLK, tile, qkv_ref[:, D_ATTN : D_ATTN + D_KV], qkv_ref[:, D_ATTN + D_KV : D_QKV]),
        (BLK + tile, BLK, kvn_ref[:, 0:D_KV], kvn_ref[:, D_KV : 2 * D_KV]),
    )
    for r0, n, k, v in pieces:
        khat, rk = _half_rms(k, ones)
        kn = khat * kg
        ks[r0 : r0 + n, :] = _mx(kn)
        kr[r0 : r0 + n, :] = _mx(pltpu.roll(kn, HEAD_DIM, 1))
        vs[r0 : r0 + n, :] = _mx(v)
        vr[r0 : r0 + n, :] = _mx(pltpu.roll(v, HEAD_DIM, 1))
        if khat_s is not None:
            khat_s[r0 : r0 + n, :] = khat
            rk_s[r0 : r0 + n, :] = rk


def _halo_specs(tile, seq):
    nb = tile // BLK
    last = seq // BLK - 1
    kv_col = D_ATTN // (2 * D_KV)
    prev = pl.BlockSpec((BLK, 2 * D_KV), lambda i: (jnp.maximum(i * nb - 1, 0), kv_col))
    nxt = pl.BlockSpec((BLK, 2 * D_KV), lambda i: (jnp.minimum((i + 1) * nb, last), kv_col))
    return prev, nxt


def _row_spec(tile, width):
    return pl.BlockSpec((tile, width), lambda i: (i, 0))


def _full_spec(shape):
    nd = len(shape)
    return pl.BlockSpec(shape, lambda i: (0,) * nd)


SMEM_SPEC = pl.BlockSpec(memory_space=pltpu.SMEM)
VMEM_SPEC = pl.BlockSpec(memory_space=pltpu.VMEM)
HBM_SPEC = pl.BlockSpec(memory_space=pltpu.HBM)


def _rider_steps(nt):
    return 0, (3 * nt) // 4, nt - 1


def _gather_rider(sources, gathered, sems, step, nt):
    start, forward, finish = _all_gather_stages(
        [_row_block(g, s.shape[0]) for g, s in zip(gathered, sources)], sems[0], sems[1], sources=sources, local_sems=sems[2]
    )
    at_start, at_forward, at_finish = _rider_steps(nt)
    pl.when(step == at_start)(start)

    def after_compute():
        pl.when(step == at_forward)(forward)
        pl.when(step == at_finish)(finish)

    return after_compute


def _gathered_shapes(gather):
    return [jax.ShapeDtypeStruct((N_DEV * g.shape[0], g.shape[1]), g.dtype) for g in gather]


def _ln_proj_fwd(x, gain, scale1, shift, w_in_t, name):
    seq, d = x.shape
    tile = min(TOKEN_TILE, seq)

    def body(x_ref, g_ref, s1_ref, sh_ref, wt_ref, pa_ref, pb_ref):
        xv = x_ref[...]
        r = lax.rsqrt(jnp.mean(xv * xv, axis=-1, keepdims=True) + EPS)
        h = _mx((xv * r) * g_ref[...] * s1_ref[...] + sh_ref[...])
        pa_ref[...] = _dot_nt(h, wt_ref[0:D_QKV, :])
        pb_ref[...] = _dot_nt(h, wt_ref[D_QKV:D_IN, :])

    vec = _full_spec((1, d))
    return pl.pallas_call(
        body,
        name=name,
        grid=(seq // tile,),
        in_specs=[_row_spec(tile, d), vec, vec, vec, _full_spec((D_IN, d))],
        out_specs=[_row_spec(tile, D_QKV), _row_spec(tile, D_REST)],
        out_shape=[jax.ShapeDtypeStruct((seq, D_QKV), F32), jax.ShapeDtypeStruct((seq, D_REST), F32)],
        compiler_params=_params(("parallel",)),
    )(x, gain, scale1, shift, w_in_t)


def _attn_fwd(pa, q_gain2, k_gain2, sink, name, gather=()):
    seq = pa.shape[0]
    tile = min(TOKEN_TILE, seq)
    nb = tile // BLK
    nt = seq // tile
    ext = tile + 2 * BLK
    n_ride = len(gather)
    riding = n_ride > 0

    def body(sink_ref, qkv_ref, kvp_ref, kvn_ref, qg_ref, kg_ref, *rest):
        i = pl.program_id(0)
        sources, (o_ref, p_ref, psink_ref), gathered = rest[:n_ride], rest[n_ride : n_ride + 3], rest[n_ride + 3 : 2 * n_ride + 3]
        qs, ks, kr, vs, vr, bias_s, s_scr, *sems = rest[2 * n_ride + 3 :]
        if riding:
            after_compute = _gather_rider(sources, gathered, sems, i, nt)

        @pl.when(i == 0)
        def _():
            _fill_attn_bias(bias_s)

        ones = _half_ones()
        lo = _lane_lo(BLK)
        lo_t = _lane_lo(tile)
        head_lane = lax.broadcasted_iota(jnp.int32, (ROW_CHUNK, LANES), 1)
        _stage_keys(kvp_ref, qkv_ref, kvn_ref, kg_ref[...], ones, tile, ks, kr, vs, vr)
        for j in range(N_PAIRS):
            qhat, _ = _half_rms(qkv_ref[:, j * LANES : (j + 1) * LANES], ones)
            _stage_queries(qhat * (qg_ref[...] * Q_SCALE), lo_t, j, nb, qs)

        def block(n, carry):
            r0 = pl.multiple_of(n * BLK, BLK)
            krows = pl.ds(r0, 3 * BLK)
            kind = _block_kind(i * nb + n, seq)
            for v in range(2):
                s_scr[v] = _dot_nt(qs[n, v], (kr if v else ks)[krows, :])
            for rc in range(0, BLK, ROW_CHUNK):
                p_sink = jnp.zeros((ROW_CHUNK, LANES), F32)
                for h in range(N_HEADS):
                    v, slot = HEAD_SLOT[h]
                    sink_h = sink_ref[h]
                    rows = slice(slot * BLK + rc, slot * BLK + rc + ROW_CHUNK)
                    s = s_scr[v, rows, :] + bias_s[kind, h, rc : rc + ROW_CHUNK, :]
                    m = jnp.maximum(jnp.max(s, axis=-1, keepdims=True), sink_h)
                    p = jnp.exp(s - m)
                    e_sink = jnp.exp(sink_h - m)
                    inv = 1.0 / (jnp.sum(p, axis=-1, keepdims=True) + e_sink)
                    p_ref[n, v, rows, :] = _mx(p * inv)
                    p_sink = jnp.where(head_lane == h, e_sink * inv, p_sink)
                psink_ref[pl.ds(pl.multiple_of(r0 + rc, ROW_CHUNK), ROW_CHUNK), :] = p_sink
            outs = [_dot(p_ref[n, v], (vr if v else vs)[krows, :]) for v in range(2)]
            for j in range(N_PAIRS):
                o_ref[pl.ds(r0, BLK), j * LANES : (j + 1) * LANES] = _unstack_pair(outs, j, lo)
            return carry

        lax.fori_loop(0, nb, block, 0)
        if riding:
            after_compute()

    prev, nxt = _halo_specs(tile, seq)
    vec = _full_spec((1, LANES))
    in_specs = [SMEM_SPEC, _row_spec(tile, D_QKV), prev, nxt, vec, vec]
    out_specs = [
        _row_spec(tile, D_ATTN),
        pl.BlockSpec((nb, 2, STACK, 3 * BLK), lambda i: (i, 0, 0, 0)),
        _row_spec(tile, LANES),
    ]
    out_shape = [
        jax.ShapeDtypeStruct((seq, D_ATTN), F32),
        jax.ShapeDtypeStruct((seq // BLK, 2, STACK, 3 * BLK), MXU_DTYPE),
        jax.ShapeDtypeStruct((seq, LANES), F32),
    ]
    scratch = [
        pltpu.VMEM((nb, 2, STACK, LANES), MXU_DTYPE),
        pltpu.VMEM((ext, LANES), MXU_DTYPE),
        pltpu.VMEM((ext, LANES), MXU_DTYPE),
        pltpu.VMEM((ext, LANES), MXU_DTYPE),
        pltpu.VMEM((ext, LANES), MXU_DTYPE),
        pltpu.VMEM((3, N_HEADS, BLK, 3 * BLK), F32),
        pltpu.VMEM((2, STACK, 3 * BLK), F32),
    ]
    return pl.pallas_call(
        body,
        name=name,
        grid=(nt,),
        in_specs=in_specs + [HBM_SPEC] * n_ride,
        out_specs=out_specs + [HBM_SPEC] * n_ride,
        out_shape=out_shape + _gathered_shapes(gather),
        scratch_shapes=scratch + _rider_sems(n_ride),
        compiler_params=_params(("arbitrary",)),
    )(sink, pa, pa, pa, q_gain2, k_gain2, *gather)


def _mix_out_fwd(pb, o, x, gate, w_out, w_s, b_st, name, target=None):
    seq, d = x.shape
    tile = min(TOKEN_TILE, seq)
    nb = tile // BLK
    with_loss = target is not None

    def body(pb_ref, o_ref, x_ref, gate_ref, wo_ref, ws_ref, bs_ref, *rest):
        if with_loss:
            t_ref, xo_ref, acc_ref, sv_ref, y_s, vn_s = rest

            @pl.when(pl.program_id(0) == 0)
            def _():
                acc_ref[...] = jnp.zeros_like(acc_ref)
        else:
            xo_ref, sv_ref, y_s, vn_s = rest
        ones = _half_ones(WIDE)
        lo = _lane_lo(BLK)
        ga = pb_ref[:, 0:D_ATTN]
        y_s[:, 0:D_ATTN] = _mx(o_ref[...] * (ga * _sigmoid(ga)))
        for j in range(D_GM // WIDE):
            vhat, _ = _half_rms(pb_ref[:, 2 * D_GM + j * WIDE : 2 * D_GM + (j + 1) * WIDE], ones)
            vn_s[:, j * WIDE : (j + 1) * WIDE] = _mx(vhat)

        def chunk(n, carry):
            rows = pl.ds(pl.multiple_of(n * BLK, BLK), BLK)
            for j in range(N_PAIRS):
                cols = slice(j * LANES, (j + 1) * LANES)
                vn = vn_s[rows, cols]
                sv = jnp.where(lo, _dot(ws_ref[2 * j], vn), _dot(ws_ref[2 * j + 1], vn)) + bs_ref[:, cols]
                sv_ref[rows, cols] = sv
                u = pb_ref[rows, D_ATTN + j * LANES : D_ATTN + (j + 1) * LANES]
                gg = pb_ref[rows, D_ATTN + 2 * D_GM + j * LANES : D_ATTN + 2 * D_GM + (j + 1) * LANES]
                y_s[rows, D_ATTN + j * LANES : D_ATTN + (j + 1) * LANES] = _mx((u * sv) * (gg * _sigmoid(gg)))
            return carry

        lax.fori_loop(0, nb, chunk, 0)
        y = x_ref[...] + gate_ref[...] * _dot(y_s[...], wo_ref[...])
        if with_loss:
            e = y - t_ref[...]
            xo_ref[...] = e * (1.0 / d)
            acc_ref[...] += jnp.sum(jnp.sum(e * e, axis=-1, keepdims=True), axis=0, keepdims=True)
        else:
            xo_ref[...] = y

    row = _row_spec(tile, d)
    acc_shape = (SUBLANES, LANES)
    return pl.pallas_call(
        body,
        name=name,
        grid=(seq // tile,),
        in_specs=[
            _row_spec(tile, D_REST),
            _row_spec(tile, D_ATTN),
            row,
            _full_spec((1, d)),
            _full_spec((D_MIX, d)),
            _full_spec((N_GROUPS, BLK, BLK)),
            _full_spec((BLK, D_GM)),
        ]
        + ([row] if with_loss else []),
        out_specs=[row] + ([_full_spec(acc_shape)] if with_loss else []) + [_row_spec(tile, D_GM)],
        out_shape=[jax.ShapeDtypeStruct((seq, d), F32)]
        + ([jax.ShapeDtypeStruct(acc_shape, F32)] if with_loss else [])
        + [jax.ShapeDtypeStruct((seq, D_GM), F32)],
        scratch_shapes=[pltpu.VMEM((tile, D_MIX), MXU_DTYPE), pltpu.VMEM((tile, D_GM), MXU_DTYPE)],
        compiler_params=_params(("arbitrary",) if with_loss else ("parallel",)),
    )(pb, o, x, gate, w_out, w_s, b_st, *([target] if with_loss else []))


def _mix_out_bwd(dxn, pb, o, sv, gate, w_out, w_s_t, name):
    seq, d = dxn.shape
    tile = min(TOKEN_TILE, seq)
    nb = tile // BLK
    nt = seq // tile

    def body(dxn_ref, pb_ref, o_ref, sv_ref, gate_ref, wo_ref, wst_ref,
             dpb_ref, do_ref, dwo_ref, dgate_ref, dws_ref, dbs_ref, g_ref, y_s, dy_s, vn_s, rv_s, vnb_s, dsv_s, dvn_s):
        @pl.when(pl.program_id(0) == 0)
        def _():
            g_ref[...] = jnp.zeros_like(g_ref)
            dws_ref[...] = jnp.zeros_like(dws_ref)
            dbs_ref[...] = jnp.zeros_like(dbs_ref)

        ones = _half_ones(WIDE)
        lo = _lane_lo(BLK)
        c_u = slice(D_ATTN, D_ATTN + D_GM)
        c_vg = slice(D_ATTN + D_GM, D_ATTN + 2 * D_GM)
        c_gg = slice(D_ATTN + 2 * D_GM, D_REST)
        dxv = dxn_ref[...]
        dy_s[...] = _dot_nt(_mx(dxv * gate_ref[...]), wo_ref[...])
        ga = pb_ref[:, 0:D_ATTN]
        sig = _sigmoid(ga)
        sil = ga * sig
        ov = o_ref[...]
        y_s[:, 0:D_ATTN] = _mx(ov * sil)
        da = dy_s[:, 0:D_ATTN]
        do_ref[...] = da * sil
        dpb_ref[:, 0:D_ATTN] = (da * ov * (sig * (1.0 + ga * (1.0 - sig)))).astype(dpb_ref.dtype)
        for j in range(D_GM // WIDE):
            cols = slice(j * WIDE, (j + 1) * WIDE)
            vhat, rv = _half_rms(pb_ref[:, 2 * D_GM + j * WIDE : 2 * D_GM + (j + 1) * WIDE], ones)
            vn_s[:, cols] = vhat
            rv_s[:, cols] = rv
            vnb_s[:, cols] = _mx(vhat)

        def gating(n, carry):
            rows = pl.ds(pl.multiple_of(n * BLK, BLK), BLK)
            sv = sv_ref[rows, :]
            u = pb_ref[rows, c_u]
            gg = pb_ref[rows, c_gg]
            sg = _sigmoid(gg)
            silg = gg * sg
            m0 = u * sv
            y_s[rows, D_ATTN:D_MIX] = _mx(m0 * silg)
            dm = dy_s[rows, D_ATTN:D_MIX]
            dm0 = dm * silg
            dpb_ref[rows, c_gg] = (dm * m0 * (sg * (1.0 + gg * (1.0 - sg)))).astype(dpb_ref.dtype)
            dpb_ref[rows, c_u] = (dm0 * sv).astype(dpb_ref.dtype)
            dsv = dm0 * u
            dsv_s[rows, :] = _mx(dsv)
            dbs_ref[...] += dsv
            return carry

        lax.fori_loop(0, nb, gating, 0)

        def spatial_bwd(n, carry):
            rows = pl.ds(pl.multiple_of(n * BLK, BLK), BLK)
            for j in range(N_PAIRS):
                cols = slice(j * LANES, (j + 1) * LANES)
                dsv = dsv_s[rows, cols]
                dvn_s[rows, cols] = jnp.where(lo, _dot(wst_ref[2 * j], dsv), _dot(wst_ref[2 * j + 1], dsv))
            return carry

        lax.fori_loop(0, nb, spatial_bwd, 0)
        zero = jnp.zeros((BLK, LANES), MXU_DTYPE)
        for j in range(N_PAIRS):
            cols = slice(j * LANES, (j + 1) * LANES)
            chunks = [dsv_s[n * BLK : (n + 1) * BLK, cols] for n in range(nb)]
            vn_all = jnp.concatenate([vnb_s[n * BLK : (n + 1) * BLK, cols] for n in range(nb)], axis=1)
            dws_ref[2 * j] += _dot_nt(jnp.concatenate([jnp.where(lo, c, zero) for c in chunks], axis=1), vn_all)
            dws_ref[2 * j + 1] += _dot_nt(jnp.concatenate([jnp.where(lo, zero, c) for c in chunks], axis=1), vn_all)
        for j in range(D_GM // WIDE):
            cols = slice(j * WIDE, (j + 1) * WIDE)
            dpb_ref[:, D_ATTN + D_GM + j * WIDE : D_ATTN + D_GM + (j + 1) * WIDE] = _half_rms_bwd(
                dvn_s[:, cols], vn_s[:, cols], rv_s[:, cols], ones
            ).astype(dpb_ref.dtype)
        g_ref[...] += _dot_tn(y_s[...], _mx(dxv))

        @pl.when(pl.program_id(0) == nt - 1)
        def _():
            gv = g_ref[...]
            dwo_ref[...] = (gv * gate_ref[...]).astype(dwo_ref.dtype)
            dgate_ref[...] = _group_rows(gv * wo_ref[...].astype(F32))

    return pl.pallas_call(
        body,
        name=name,
        grid=(seq // tile,),
        in_specs=[
            _row_spec(tile, d),
            _row_spec(tile, D_REST),
            _row_spec(tile, D_ATTN),
            _row_spec(tile, D_GM),
            _full_spec((1, d)),
            _full_spec((D_MIX, d)),
            _full_spec((N_GROUPS, BLK, BLK)),
        ],
        out_specs=[
            _row_spec(tile, D_REST),
            _row_spec(tile, D_ATTN),
            _full_spec((D_MIX, d)),
            _full_spec((SUBLANES, d)),
            _full_spec((N_GROUPS, BLK, BLK)),
            _full_spec((BLK, D_GM)),
        ],
        out_shape=[
            jax.ShapeDtypeStruct((seq, D_REST), MXU_DTYPE),
            jax.ShapeDtypeStruct((seq, D_ATTN), F32),
            jax.ShapeDtypeStruct((D_MIX, d), jnp.bfloat16),
            jax.ShapeDtypeStruct((SUBLANES, d), F32),
            jax.ShapeDtypeStruct((N_GROUPS, BLK, BLK), F32),
            jax.ShapeDtypeStruct((BLK, D_GM), F32),
        ],
        scratch_shapes=[
            pltpu.VMEM((D_MIX, d), F32),
            pltpu.VMEM((tile, D_MIX), MXU_DTYPE),
            pltpu.VMEM((tile, D_MIX), F32),
            pltpu.VMEM((tile, D_GM), F32),
            pltpu.VMEM((tile, D_GM), F32),
            pltpu.VMEM((tile, D_GM), MXU_DTYPE),
            pltpu.VMEM((tile, D_GM), MXU_DTYPE),
            pltpu.VMEM((tile, D_GM), F32),
        ],
        compiler_params=_params(("arbitrary",)),
    )(dxn, pb, o, sv, gate, w_out, w_s_t)


def _attn_bwd(pa, o, do, probs, p_sink, q_gain2, k_gain2, name, scatter=()):
    seq = pa.shape[0]
    tile = min(TOKEN_TILE, seq)
    nb = tile // BLK
    nt = seq // tile
    ext = tile + 2 * BLK
    n_ride = len(scatter)
    riding = n_ride > 0

    def body(qkv_ref, kvp_ref, kvn_ref, o_ref, do_ref, p_ref, psink_ref, qg_ref, kg_ref, *rest):
        i = pl.program_id(0)
        blocks, rest = rest[:n_ride], rest[n_ride:]
        dq_ref, dkv_ref, hp_ref, hn_ref, dqg_ref, dkg_ref, dsk_ref = rest[:7]
        landing, rest = rest[7 : 7 + n_ride], rest[7 + n_ride :]
        (qs, dos, qhat_s, rq_s, ks, kr, vs, vr, khat_s, rk_s, dqn_s, dka, dva, dp_scr, ds_scr) = rest[:15]
        if riding:
            start, finish = _scatter_stages(blocks, landing, *rest[15:])
            at_start, _, at_finish = _rider_steps(nt)
            pl.when(i == at_start)(start)

        @pl.when(i == 0)
        def _():
            dqg_ref[...] = jnp.zeros_like(dqg_ref)
            dkg_ref[...] = jnp.zeros_like(dkg_ref)
            dsk_ref[...] = jnp.zeros_like(dsk_ref)

        ones = _half_ones()
        lo = _lane_lo(BLK)
        lo_t = _lane_lo(tile)
        lo_c = _lane_lo(ROW_CHUNK)
        qg = qg_ref[...] * Q_SCALE
        kg = kg_ref[...]
        _stage_keys(kvp_ref, qkv_ref, kvn_ref, kg, ones, tile, ks, kr, vs, vr, khat_s, rk_s)
        head_lane = lax.broadcasted_iota(jnp.int32, (tile, LANES), 1)
        d_rows = jnp.zeros((tile, LANES), F32)
        for j in range(N_PAIRS):
            cols = slice(j * LANES, (j + 1) * LANES)
            qhat, rq = _half_rms(qkv_ref[:, cols], ones)
            qhat_s[:, cols] = qhat
            rq_s[:, cols] = rq
            _stage_queries(qhat * qg, lo_t, j, nb, qs)
            dov = do_ref[:, cols]
            _stage_queries(dov, lo_t, j, nb, dos)
            d_pair = _half_sum(dov * o_ref[:, cols], ones)
            d_rows = jnp.where(head_lane == 2 * j, d_pair, d_rows)
            d_rows = jnp.where(head_lane == 2 * j + 1, pltpu.roll(d_pair, HEAD_DIM, 1), d_rows)
        dsk_ref[...] -= _group_rows(psink_ref[...] * d_rows)
        dka[...] = jnp.zeros_like(dka)
        dva[...] = jnp.zeros_like(dva)

        def block(n, carry):
            r0 = pl.multiple_of(n * BLK, BLK)
            krows = pl.ds(r0, 3 * BLK)
            for v in range(2):
                dp_scr[v] = _dot_nt(dos[n, v], (vr if v else vs)[krows, :])
            for h in range(N_HEADS):
                v, slot = HEAD_SLOT[h]
                j, a = divmod(h, 2)
                cols = slice(j * LANES, (j + 1) * LANES)
                for rc in range(0, BLK, ROW_CHUNK):
                    rows = slice(slot * BLK + rc, slot * BLK + rc + ROW_CHUNK)
                    trows = pl.ds(pl.multiple_of(r0 + rc, ROW_CHUNK), ROW_CHUNK)
                    prod = do_ref[trows, cols] * o_ref[trows, cols]
                    prod = jnp.where(lo_c, prod, 0.0) if a == 0 else jnp.where(lo_c, 0.0, prod)
                    dcol = jnp.sum(prod, axis=-1, keepdims=True)
                    ds_scr[v, rows, :] = _mx(p_ref[n, v, rows, :].astype(F32) * (dp_scr[v, rows, :] - dcol))
            dqv = []
            for v in range(2):
                dqv.append(_dot(ds_scr[v], (kr if v else ks)[krows, :]))
                dka[v, krows, :] += _dot_tn(ds_scr[v], qs[n, v])
                dva[v, krows, :] += _dot_tn(p_ref[n, v], dos[n, v])
            for j in range(N_PAIRS):
                dqn_s[pl.ds(r0, BLK), j * LANES : (j + 1) * LANES] = _unstack_pair(dqv, j, lo)
            return carry

        lax.fori_loop(0, nb, block, 0)
        for j in range(N_PAIRS):
            cols = slice(j * LANES, (j + 1) * LANES)
            dqn = dqn_s[:, cols]
            qhat = qhat_s[:, cols]
            dqg_ref[:, cols] += _group_rows(dqn * qhat) * Q_SCALE
            dq_ref[:, cols] = _half_rms_bwd(dqn * qg, qhat, rq_s[:, cols], ones).astype(dq_ref.dtype)
        dkn = dka[0] + pltpu.roll(dka[1], HEAD_DIM, 1)
        khat = khat_s[...]
        dkg_ref[...] += _group_rows(dkn * khat)
        dk = _half_rms_bwd(dkn * kg, khat, rk_s[...], ones)
        dv = dva[0] + pltpu.roll(dva[1], HEAD_DIM, 1)
        hp_ref[:, 0:D_KV] = dk[0:BLK]
        hp_ref[:, D_KV : 2 * D_KV] = dv[0:BLK]
        dkv_ref[:, 0:D_KV] = dk[BLK : BLK + tile]
        dkv_ref[:, D_KV : 2 * D_KV] = dv[BLK : BLK + tile]
        hn_ref[:, 0:D_KV] = dk[BLK + tile : ext]
        hn_ref[:, D_KV : 2 * D_KV] = dv[BLK + tile : ext]
        if riding:
            pl.when(i == at_finish)(finish)

    prev, nxt = _halo_specs(tile, seq)
    vec = _full_spec((1, LANES))
    halo = pl.BlockSpec((None, BLK, 2 * D_KV), lambda i: (i, 0, 0))
    return pl.pallas_call(
        body,
        name=name,
        grid=(nt,),
        in_specs=[
            _row_spec(tile, D_QKV),
            prev,
            nxt,
            _row_spec(tile, D_ATTN),
            _row_spec(tile, D_ATTN),
            pl.BlockSpec((nb, 2, STACK, 3 * BLK), lambda i: (i, 0, 0, 0)),
            _row_spec(tile, LANES),
            vec,
            vec,
        ]
        + [HBM_SPEC] * n_ride,
        out_specs=[
            _row_spec(tile, D_ATTN),
            _row_spec(tile, 2 * D_KV),
            halo,
            halo,
            _full_spec((SUBLANES, D_ATTN)),
            _full_spec((SUBLANES, LANES)),
            _full_spec((SUBLANES, LANES)),
        ]
        + [HBM_SPEC] * n_ride,
        out_shape=[
            jax.ShapeDtypeStruct((seq, D_ATTN), MXU_DTYPE),
            jax.ShapeDtypeStruct((seq, 2 * D_KV), F32),
            jax.ShapeDtypeStruct((nt, BLK, 2 * D_KV), F32),
            jax.ShapeDtypeStruct((nt, BLK, 2 * D_KV), F32),
            jax.ShapeDtypeStruct((SUBLANES, D_ATTN), F32),
            jax.ShapeDtypeStruct((SUBLANES, LANES), F32),
            jax.ShapeDtypeStruct((SUBLANES, LANES), F32),
        ]
        + _landing_shapes(scatter),
        scratch_shapes=[
            pltpu.VMEM((nb, 2, STACK, LANES), MXU_DTYPE),
            pltpu.VMEM((nb, 2, STACK, LANES), MXU_DTYPE),
            pltpu.VMEM((tile, D_ATTN), F32),
            pltpu.VMEM((tile, D_ATTN), F32),
            pltpu.VMEM((ext, LANES), MXU_DTYPE),
            pltpu.VMEM((ext, LANES), MXU_DTYPE),
            pltpu.VMEM((ext, LANES), MXU_DTYPE),
            pltpu.VMEM((ext, LANES), MXU_DTYPE),
            pltpu.VMEM((ext, LANES), F32),
            pltpu.VMEM((ext, LANES), F32),
            pltpu.VMEM((tile, D_ATTN), F32),
            pltpu.VMEM((2, ext, LANES), F32),
            pltpu.VMEM((2, ext, LANES), F32),
            pltpu.VMEM((2, STACK, 3 * BLK), F32),
            pltpu.VMEM((2, STACK, 3 * BLK), MXU_DTYPE),
        ]
        + _rider_sems(n_ride),
        compiler_params=_params(("arbitrary",)),
    )(pa, pa, pa, o, do, probs, p_sink, q_gain2, k_gain2, *scatter)


def _halo_in_specs(tile, nt):
    from_prev = pl.BlockSpec((None, BLK, 2 * D_KV), lambda i: (jnp.maximum(i - 1, 0), 0, 0))
    from_next = pl.BlockSpec((None, BLK, 2 * D_KV), lambda i: (jnp.minimum(i + 1, nt - 1), 0, 0))
    return from_prev, from_next


def _landing_shapes(scatter):
    return [jax.ShapeDtypeStruct((N_DEV,) + b.shape[2:], b.dtype) for b in scatter]


def _rider_sems(n_ride):
    if not n_ride:
        return []
    return [pltpu.SemaphoreType.DMA((7 * n_ride,)), pltpu.SemaphoreType.DMA((7 * n_ride,)), pltpu.SemaphoreType.DMA((n_ride,))]


def _proj_bwd_dx(x, dxn, dq, dkvb, dpb, w_in_t, gain, scale1, name):
    seq, d = x.shape
    tile = min(TOKEN_TILE, seq)

    def row(width):
        return _row_spec(tile, width)

    def body(x_ref, dxn_ref, dq_ref, dkvb_ref, dpb_ref, wt_ref, g_ref, s1_ref, dx_ref, c0_ref, c1_ref):
        @pl.when(pl.program_id(0) == 0)
        def _():
            c0_ref[...] = jnp.zeros_like(c0_ref)
            c1_ref[...] = jnp.zeros_like(c1_ref)

        dh = (
            _dot(dq_ref[...], wt_ref[0:D_ATTN, :])
            + _dot(dkvb_ref[...], wt_ref[D_ATTN:D_QKV, :])
            + _dot(dpb_ref[...], wt_ref[D_QKV:D_IN, :])
        )
        xv = x_ref[...]
        r = lax.rsqrt(jnp.mean(xv * xv, axis=-1, keepdims=True) + EPS)
        xn = xv * r
        c0_ref[...] += _group_rows(dh)
        c1_ref[...] += _group_rows(dh * xn)
        dxn_ = dh * (g_ref[...] * s1_ref[...])
        dx_ref[...] = dxn_ref[...] + r * (dxn_ - xn * jnp.mean(xn * dxn_, axis=-1, keepdims=True))

    vec = _full_spec((1, d))
    return pl.pallas_call(
        body,
        name=name,
        grid=(seq // tile,),
        in_specs=[row(d), row(d), row(D_ATTN), row(2 * D_KV), row(D_REST), _full_spec((D_IN, d)), vec, vec],
        out_specs=[row(d), _full_spec((SUBLANES, d)), _full_spec((SUBLANES, d))],
        out_shape=[
            jax.ShapeDtypeStruct((seq, d), F32),
            jax.ShapeDtypeStruct((SUBLANES, d), F32),
            jax.ShapeDtypeStruct((SUBLANES, d), F32),
        ],
        compiler_params=_params(("arbitrary",)),
    )(x, dxn, dq, dkvb, dpb, w_in_t, gain, scale1)


def _proj_bwd_dw(x, gain, scale1, shift, dq, dkv, halo_prev, halo_next, dpb, name, gather=()):
    seq, d = x.shape
    tile = min(TOKEN_TILE, seq)
    nt = seq // tile
    assert tile >= 2 * BLK
    n_ride = len(gather)

    def body(x_ref, g_ref, s1_ref, sh_ref, dq_ref, dkv_ref, hn_ref, hp_ref, dpb_ref, *rest):
        i = pl.program_id(0)
        sources, rest = rest[:n_ride], rest[n_ride:]
        dw_ref, dkvb_ref = rest[:2]
        gathered, (acc, *sems) = rest[2 : 2 + n_ride], rest[2 + n_ride :]
        after_compute = _gather_rider(sources, gathered, sems, i, nt) if n_ride else None

        @pl.when(i == 0)
        def _():
            acc[...] = jnp.zeros_like(acc)

        top = dkv_ref[0:BLK, :] + jnp.where(i > 0, hn_ref[...], 0.0)
        bot = dkv_ref[tile - BLK : tile, :] + jnp.where(i < nt - 1, hp_ref[...], 0.0)
        dkvb_ref[0:BLK, :] = top.astype(dkvb_ref.dtype)
        dkvb_ref[tile - BLK : tile, :] = bot.astype(dkvb_ref.dtype)
        if tile > 2 * BLK:
            dkvb_ref[BLK : tile - BLK, :] = dkv_ref[BLK : tile - BLK, :].astype(dkvb_ref.dtype)
        xv = x_ref[...]
        r = lax.rsqrt(jnp.mean(xv * xv, axis=-1, keepdims=True) + EPS)
        h = _mx((xv * r) * g_ref[...] * s1_ref[...] + sh_ref[...])
        acc[0:D_ATTN, :] += _dot_tn(dq_ref[...], h)
        acc[D_ATTN:D_QKV, :] += _dot_tn(dkvb_ref[...], h)
        acc[D_QKV:D_IN, :] += _dot_tn(dpb_ref[...], h)

        @pl.when(i == nt - 1)
        def _():
            dw_ref[...] = acc[...].astype(dw_ref.dtype)

        if n_ride:
            after_compute()

    from_prev, from_next = _halo_in_specs(tile, nt)
    vec = _full_spec((1, d))
    return pl.pallas_call(
        body,
        name=name,
        grid=(nt,),
        in_specs=[
            _row_spec(tile, d),
            vec,
            vec,
            vec,
            _row_spec(tile, D_ATTN),
            _row_spec(tile, 2 * D_KV),
            from_prev,
            from_next,
            _row_spec(tile, D_REST),
        ]
        + [HBM_SPEC] * n_ride,
        out_specs=[_full_spec((D_IN, d)), _row_spec(tile, 2 * D_KV)] + [HBM_SPEC] * n_ride,
        out_shape=[jax.ShapeDtypeStruct((D_IN, d), jnp.bfloat16), jax.ShapeDtypeStruct((seq, 2 * D_KV), MXU_DTYPE)]
        + _gathered_shapes(gather),
        scratch_shapes=[pltpu.VMEM((D_IN, d), F32)] + _rider_sems(n_ride),
        compiler_params=_params(("arbitrary",)),
    )(x, gain, scale1, shift, dq, dkv, halo_next, halo_prev, dpb, *gather)


def _adamw_math(w, g, m, v):
    m = ADAM_B1 * m + (1.0 - ADAM_B1) * g
    v = ADAM_B2 * v + (1.0 - ADAM_B2) * (g * g)
    m_hat = m / (1.0 - ADAM_B1**ADAM_STEP)
    v_hat = v / (1.0 - ADAM_B2**ADAM_STEP)
    delta = -ADAM_LR * (m_hat / (jnp.sqrt(v_hat) + ADAM_EPS) + ADAM_WD * w)
    return delta, m, v


def _small_update(gathered, gathered_ws, w, m, v, ws, m_ws, v_ws):
    def body(ga_ref, gws_ref, w_ref, m_ref, v_ref, ws_ref, mws_ref, vws_ref, *outs):
        for src, refs, out in ((ga_ref, (w_ref, m_ref, v_ref), outs[0:4]), (gws_ref, (ws_ref, mws_ref, vws_ref), outs[4:8])):
            g = src[0].astype(F32)
            for j in range(1, N_DEV):
                g = g + src[j].astype(F32)
            out[0][...] = g
            out[1][...], out[2][...], out[3][...] = _adamw_math(refs[0][...], g, refs[1][...], refs[2][...])

    shapes = [jax.ShapeDtypeStruct(w.shape, F32)] * 4 + [jax.ShapeDtypeStruct(ws.shape, F32)] * 4
    return pl.pallas_call(
        body,
        name="small_update",
        in_specs=[VMEM_SPEC] * 8,
        out_specs=[VMEM_SPEC] * 8,
        out_shape=shapes,
        compiler_params=_params(),
    )(gathered, gathered_ws, w, m, v, ws, m_ws, v_ws)


def _ada_update(c_all, d_ada_cols, w, m, v):
    n_layers = w.shape[0]

    def body(c_ref, da_ref, w_ref, m_ref, v_ref, g_ref, d_ref, mo_ref, vo_ref):
        cv = c_ref[...]
        cond = cv * _sigmoid(cv)
        for l in range(n_layers):
            g = lax.dot_general(
                cond, da_ref[l], (((0,), (0,)), ((), ())), preferred_element_type=F32, precision=lax.Precision.HIGHEST
            )
            g_ref[l] = g
            d_ref[l], mo_ref[l], vo_ref[l] = _adamw_math(w_ref[l], g, m_ref[l], v_ref[l])

    return pl.pallas_call(
        body,
        name="ada_update",
        in_specs=[VMEM_SPEC] * 5,
        out_specs=[VMEM_SPEC] * 4,
        out_shape=[jax.ShapeDtypeStruct(w.shape, F32)] * 4,
        compiler_params=_params(),
    )(c_all, d_ada_cols, w, m, v)


def _position():
    return lax.axis_index("x"), lax.axis_index("y"), lax.axis_index("c")


def _flip(pos, k):
    x, y, c = pos
    return (1 - x if k & 4 else x, 1 - y if k & 2 else y, 1 - c if k & 1 else c)


def _index(pos):
    x, y, c = pos
    return 4 * x + 2 * y + c


def _remote(src, dst, send_sem, recv_sem, to):
    return pltpu.make_async_remote_copy(
        src_ref=src, dst_ref=dst, send_sem=send_sem, recv_sem=recv_sem, device_id=to, device_id_type=MESH_ID
    )


def _all_gather_stages(slots, send_sems, recv_sems, sources=None, local_sems=None):
    me = _position()
    sibling = _flip(me, 1)
    others = (4, 2, 6)
    arrays = range(len(slots))

    def copy(t, k, block, to, own=False):
        slot = slots[t](_index(block))
        src = sources[t] if own and sources is not None else slot
        return _remote(src, slot, send_sems.at[7 * t + k], recv_sems.at[7 * t + k], to)

    def first(t):
        return [copy(t, 0, me, sibling, own=True)] + [copy(t, 1 + j, me, _flip(me, f), own=True) for j, f in enumerate(others)]

    def passed(t, j):
        return copy(t, 4 + j, _flip(me, others[j]), sibling)

    def local(t):
        return pltpu.make_async_copy(sources[t], slots[t](_index(me)), local_sems.at[t])

    def start():
        for t in arrays:
            if sources is not None:
                local(t).start()
            for cp in first(t):
                cp.start()

    def forward():
        for j, f in enumerate(others):
            for t in arrays:
                copy(t, 1 + j, _flip(me, f), me).wait_recv()
                passed(t, j).start()

    def finish():
        for t in arrays:
            copy(t, 0, sibling, me).wait_recv()
            for j, f in enumerate(others):
                copy(t, 4 + j, _flip(sibling, f), me).wait_recv()
        for t in arrays:
            for cp in first(t) + [passed(t, j) for j in range(len(others))]:
                cp.wait_send()
            if sources is not None:
                local(t).wait()

    return start, forward, finish


def _two_level_all_gather(slots, send_sems, recv_sems, between=None):
    start, forward, finish = _all_gather_stages(slots, send_sems, recv_sems)
    start()
    if between is not None:
        between()
    forward()
    finish()


def _row_block(ref, rows):
    return lambda j: ref.at[pl.ds(pl.multiple_of(j * rows, 16), rows), :]


def _scatter_stages(blocks, landing, send_sems, recv_sems, local_sems):
    me = _position()
    my = _index(me)
    arrays = range(len(blocks))

    def copy(t, k):
        px, py, pc = to = _flip(me, k)
        return _remote(blocks[t].at[2 * px + py, pc], landing[t].at[my], send_sems.at[7 * t + k - 1], recv_sems.at[7 * t + k - 1], to)

    def arrival(t, k):
        slot = landing[t].at[_index(_flip(me, k))]
        return _remote(slot, slot, send_sems.at[7 * t + k - 1], recv_sems.at[7 * t + k - 1], _flip(me, k))

    def local(t):
        x, y, c = me
        return pltpu.make_async_copy(blocks[t].at[2 * x + y, c], landing[t].at[my], local_sems.at[t])

    def start():
        for t in arrays:
            local(t).start()
            for k in range(1, N_DEV):
                copy(t, k).start()

    def finish():
        for t in arrays:
            for k in range(1, N_DEV):
                arrival(t, k).wait_recv()
        for t in arrays:
            for k in range(1, N_DEV):
                copy(t, k).wait_send()
            local(t).wait()

    return start, finish


def _ada_exchange(c_ref, w_ref, call_ref, parts_ref, sbuf, sem_s1, sem_r1, sem_s2, sem_r2):
    d = c_ref.shape[-1]
    n_layers = w_ref.shape[0]
    me = _position()
    my = _index(me)
    call_ref[my] = jnp.broadcast_to(c_ref[...], (SUBLANES, d))
    mine = call_ref.at[my]
    first = [_remote(mine, mine, sem_s1.at[k - 1], sem_r1.at[k - 1], _flip(me, k)) for k in range(1, N_DEV)]
    for cp in first:
        cp.start()
    for k in range(1, N_DEV):
        theirs = call_ref.at[_index(_flip(me, k))]
        _remote(theirs, theirs, sem_s1.at[k - 1], sem_r1.at[k - 1], _flip(me, k)).wait_recv()
    cv = call_ref[...].reshape(N_DEV * SUBLANES, d)
    cond = cv * _sigmoid(cv)
    for l in range(n_layers):
        rows = jnp.dot(cond, w_ref[l], preferred_element_type=F32, precision=lax.Precision.HIGHEST)
        for b in range(N_DEV):
            sbuf[b, l] = rows[b * SUBLANES : (b + 1) * SUBLANES]
    parts_ref[my] = sbuf[my]
    second = []
    for k in range(1, N_DEV):
        to = _flip(me, k)
        second.append(_remote(sbuf.at[_index(to)], parts_ref.at[my], sem_s2.at[k - 1], sem_r2.at[k - 1], to))
    for cp in second:
        cp.start()
    for k in range(1, N_DEV):
        theirs = parts_ref.at[_index(_flip(me, k))]
        _remote(theirs, theirs, sem_s2.at[k - 1], sem_r2.at[k - 1], _flip(me, k)).wait_recv()
    for cp in first + second:
        cp.wait_send()


def _gather_weights(w_in_t, w_out, c_row, w_ada):
    n_layers, rows_in, d = w_in_t.shape
    width = w_ada.shape[2]

    def body(wi_ref, wo_ref, c_ref, wa_ref, gi_ref, si_ref, so_ref, call_ref, parts_ref, sbuf, send_sems, recv_sems, *ada_sems):
        my = _index(_position())
        si_ref[...] = wi_ref[...].astype(si_ref.dtype)
        so_ref[...] = wo_ref[...].astype(so_ref.dtype)
        gi_ref[pl.ds(pl.multiple_of(my * rows_in, 16), rows_in), :] = si_ref[0]
        _two_level_all_gather(
            (_row_block(gi_ref, rows_in),),
            send_sems,
            recv_sems,
            between=functools.partial(_ada_exchange, c_ref, wa_ref, call_ref, parts_ref, sbuf, *ada_sems),
        )

    return pl.pallas_call(
        body,
        name="gather_weights",
        in_specs=[VMEM_SPEC] * 4,
        out_specs=[VMEM_SPEC] * 5,
        out_shape=[
            jax.ShapeDtypeStruct((N_DEV * rows_in, d), MXU_DTYPE),
            jax.ShapeDtypeStruct(w_in_t.shape, MXU_DTYPE),
            jax.ShapeDtypeStruct(w_out.shape, MXU_DTYPE),
            jax.ShapeDtypeStruct((N_DEV, SUBLANES, d), F32),
            jax.ShapeDtypeStruct((N_DEV, n_layers, SUBLANES, width), F32),
        ],
        scratch_shapes=[
            pltpu.VMEM((N_DEV, n_layers, SUBLANES, width), F32),
            pltpu.SemaphoreType.DMA((7,)),
            pltpu.SemaphoreType.DMA((7,)),
        ]
        + [pltpu.SemaphoreType.DMA((N_DEV - 1,))] * 4,
        compiler_params=_params(),
    )(w_in_t, w_out, c_row, w_ada)


def _gather_small(packed, adam=()):
    n_adam = len(adam)
    n_in, n_out = 4 * n_adam, 3 * n_adam

    def body(p_ref, *rest):
        operands, rest = rest[:n_in], rest[n_in:]
        g_ref, results, rest = rest[0], rest[1 : 1 + n_out], rest[1 + n_out :]
        send_sems, recv_sems, load_sems, store_sems = rest[:4]
        loaded, stored = rest[4 : 4 + n_in], rest[4 + n_in :]
        loads = [pltpu.make_async_copy(operands[k], loaded[k], load_sems.at[k]) for k in range(n_in)]
        stores = [pltpu.make_async_copy(stored[k], results[k], store_sems.at[k]) for k in range(n_out)]
        for load in loads:
            load.start()
        g_ref[_index(_position())] = p_ref[...]

        def updates():
            for t in range(n_adam):
                for load in loads[4 * t : 4 * t + 4]:
                    load.wait()
                w_ref, gr_ref, m_ref, v_ref = loaded[4 * t : 4 * t + 4]
                new = _adamw_math(w_ref[...], gr_ref[...], m_ref[...], v_ref[...])
                for k, value in zip(range(3 * t, 3 * t + 3), new):
                    stored[k][...] = value
                    stores[k].start()

        _two_level_all_gather((lambda j: g_ref.at[j],), send_sems, recv_sems, between=updates)
        for store in stores:
            store.wait()

    moved_in = [a for q in adam for a in q]
    moved_out = [jax.ShapeDtypeStruct(q[0].shape, F32) for q in adam for _ in range(3)]
    return pl.pallas_call(
        body,
        name="gather_small",
        in_specs=[VMEM_SPEC] + [HBM_SPEC] * n_in,
        out_specs=[VMEM_SPEC] + [HBM_SPEC] * n_out,
        out_shape=[jax.ShapeDtypeStruct((N_DEV,) + packed.shape, F32)] + moved_out,
        scratch_shapes=[pltpu.SemaphoreType.DMA((7,)), pltpu.SemaphoreType.DMA((7,))]
        + [pltpu.SemaphoreType.DMA((max(n_in, 1),)), pltpu.SemaphoreType.DMA((max(n_out, 1),))]
        + [pltpu.VMEM(a.shape, F32) for a in moved_in]
        + [pltpu.VMEM(s.shape, F32) for s in moved_out],
        compiler_params=_params(),
    )(packed, *moved_in)


def _scatter_finish(landed, name, own=()):
    n = len(landed)

    def body(*refs):
        if own:
            x, y, c = _position()
            my = _index((x, y, c))
        for t, (src, out) in enumerate(zip(refs[:n], refs[n + len(own) :])):
            g = None
            for j in range(N_DEV):
                part = src[j].astype(F32)
                if own:
                    part = jnp.where(j == my, refs[n + t][2 * x + y, c].astype(F32), part)
                g = part if g is None else g + part
            out[...] = g

    return pl.pallas_call(
        body,
        name=name,
        in_specs=[VMEM_SPEC] * (n + len(own)),
        out_specs=[VMEM_SPEC] * n,
        out_shape=[jax.ShapeDtypeStruct(a.shape[1:], F32) for a in landed],
        compiler_params=_params(),
    )(*landed, *own)


SEM_SPEC = pl.BlockSpec(memory_space=pltpu.SEMAPHORE)
SPLIT_COPY = pltpu.SideEffectType.DATAFLOW_SIDE_EFFECTING


def _scatter_start(blocks, name):
    land_shape = (N_DEV,) + blocks.shape[2:]

    def body(blocks_ref, land_ref, send_sems, recv_sems, blocks_thru, land_thru, token):
        me = _position()
        my = _index(me)
        for k in range(1, N_DEV):
            px, py, pc = to = _flip(me, k)
            _remote(blocks_ref.at[2 * px + py, pc], land_ref.at[my], send_sems.at[k - 1], recv_sems.at[k - 1], to).start()
        token[...] = jnp.zeros_like(token)

    return pl.pallas_call(
        body,
        name=name,
        in_specs=(HBM_SPEC, HBM_SPEC),
        out_specs=(SEM_SPEC, SEM_SPEC, HBM_SPEC, HBM_SPEC, VMEM_SPEC),
        out_shape=(
            pltpu.SemaphoreType.DMA((N_DEV - 1,)),
            pltpu.SemaphoreType.DMA((N_DEV - 1,)),
            pltpu.HBM(blocks.shape, blocks.dtype),
            pltpu.HBM(land_shape, blocks.dtype),
            jax.ShapeDtypeStruct((SUBLANES, LANES), F32),
        ),
        input_output_aliases={0: 2, 1: 3},
        compiler_params=pltpu.CompilerParams(has_side_effects=SPLIT_COPY),
    )(pltpu.with_memory_space_constraint(blocks, pltpu.HBM), pltpu.with_memory_space_constraint(lax.empty(land_shape, blocks.dtype), pltpu.HBM))


def _scatter_wait(send_sems, recv_sems, blocks_thru, land_thru, after, name):
    def body(blocks_ref, land_ref, send_sems, recv_sems, after_ref, blocks_dead, got_ref):
        me = _position()
        my = _index(me)
        for k in range(1, N_DEV):
            px, py, pc = to = _flip(me, k)
            _remote(blocks_ref.at[2 * px + py, pc], land_ref.at[my], send_sems.at[k - 1], recv_sems.at[k - 1], to).wait_send()
        for k in range(1, N_DEV):
            slot = land_ref.at[_index(_flip(me, k))]
            _remote(slot, slot, send_sems.at[k - 1], recv_sems.at[k - 1], _flip(me, k)).wait_recv()

    return pl.pallas_call(
        body,
        name=name,
        in_specs=(HBM_SPEC, HBM_SPEC, SEM_SPEC, SEM_SPEC, pl.BlockSpec(memory_space=pl.ANY)),
        out_specs=(HBM_SPEC, HBM_SPEC),
        out_shape=(pltpu.HBM(blocks_thru.shape, blocks_thru.dtype), pltpu.HBM(land_thru.shape, land_thru.dtype)),
        input_output_aliases={0: 0, 1: 1},
        compiler_params=pltpu.CompilerParams(has_side_effects=SPLIT_COPY),
    )(blocks_thru, land_thru, send_sems, recv_sems, after)


def _pack_rows(parts):
    rows, offsets, at = [], [], 0
    for p in parts:
        flat = p.reshape(-1)
        n = -(-flat.shape[0] // (SUBLANES * LANES)) * SUBLANES
        rows.append(jnp.pad(flat, (0, n * LANES - flat.shape[0])).reshape(n, LANES))
        offsets.append(at)
        at += n
    return jnp.concatenate(rows, axis=0), offsets


def _unpack_rows(packed, offsets, shapes):
    out = []
    for off, shape in zip(offsets, shapes):
        size = 1
        for s in shape:
            size *= s
        n = -(-size // (SUBLANES * LANES)) * SUBLANES
        out.append(packed[off : off + n].reshape(-1)[:size].reshape(shape))
    return out


def kernel(x, c, w_ada, b_ada, norm_gain, w_in, q_gain, k_gain, sink, w_s, b_s, w_out, loss_target, m_w_ada, m_b_ada, m_norm_gain, m_w_in, m_q_gain, m_k_gain, m_sink, m_w_s, m_b_s, m_w_out, v_w_ada, v_b_ada, v_norm_gain, v_w_in, v_q_gain, v_k_gain, v_sink, v_w_s, v_b_s, v_w_out):
    seq, d = x.shape[1], x.shape[2]
    n_layers = w_in.shape[0]
    w_cols = w_in.shape[2]
    ada_cols = w_ada.shape[2]
    my = _index(_position())
    xs = x.reshape(seq, d)
    target = loss_target.reshape(seq, d)

    rows_first = lambda a: a.transpose(0, 2, 1)
    w_in_t0, shard_in, shard_out, c_all, ada_parts = _gather_weights(rows_first(w_in), w_out, c, w_ada)
    w_in_ts, w_outs = [w_in_t0], []
    ada = ada_parts[:, :, 0, :].transpose(1, 0, 2).reshape(n_layers, 3 * d) + b_ada
    shift, scale1, gate = ada[:, None, 0:d], 1.0 + ada[:, None, d : 2 * d], ada[:, None, 2 * d : 3 * d]
    gain = norm_gain[:, None, :]

    w_s_m = w_s.astype(MXU_DTYPE)
    w_s_t = w_s_m.transpose(0, 1, 3, 2)
    b_st = jnp.repeat(b_s.transpose(0, 2, 1), HEAD_DIM, axis=2)
    q_gain2 = jnp.tile(q_gain, (1, 2))[:, None, :]
    k_gain2 = jnp.tile(k_gain, (1, 2))[:, None, :]

    xl, saved = xs, []
    for l in range(n_layers):
        last = l == n_layers - 1
        pa, pb = _ln_proj_fwd(xl, gain[l], scale1[l], shift[l], w_in_ts[l], f"ln_proj_fwd_{l}")
        wanted = ([shard_out[0]] if l == 0 else []) + ([] if last else [shard_out[l + 1], shard_in[l + 1]])
        o, probs, p_sink, *arrived = _attn_fwd(pa, q_gain2[l], k_gain2[l], sink[l], f"attn_fwd_{l}", gather=tuple(wanted))
        if not last:
            w_in_ts.append(arrived.pop())
        w_outs += arrived
        *out, sv = _mix_out_fwd(pb, o, xl, gate[l], w_outs[l], w_s_m[l], b_st[l], f"mix_out_fwd_{l}", target if last else None)
        saved.append((xl, pa, pb, o, probs, p_sink, sv))
        if last:
            dx, sq_err = out
        else:
            (xl,) = out

    g_w_in, g_w_out, small, d_ada_rows = [None] * n_layers, [None] * n_layers, [None] * n_layers, [None] * n_layers
    waiting = []
    d_ws_all = [None] * n_layers
    for l in reversed(range(n_layers)):
        x_l, pa, pb, o, probs, p_sink, sv = saved[l]
        dpb, do, dw_out, d_gate8, d_ws, d_bs = _mix_out_bwd(dx, pb, o, sv, gate[l], w_outs[l], w_s_t[l], f"mix_out_bwd_{l}")
        waiting.append((g_w_out, l, dw_out.reshape(4, 2, D_MIX // N_DEV, d)))
        riding, waiting = ([], waiting) if 0 < l == n_layers - 1 else (waiting, [])
        attn = _attn_bwd(
            pa, o, do, probs, p_sink, q_gain2[l], k_gain2[l], f"attn_bwd_{l}", scatter=tuple(b for _, _, b in riding)
        )
        dq, dkv, halo_prev, halo_next, d_qg, d_kg, d_sk = attn[:7]
        if riding:
            for (dest, layer, _), total in zip(riding, _scatter_finish(attn[7:], f"scatter_finish_{l}")):
                dest[layer] = total
        d_ws_all[l] = d_ws
        dw_args = (x_l, gain[l], scale1[l], shift[l], dq, dkv, halo_prev, halo_next, dpb, f"proj_bwd_dw_{l}")
        if l > 0:
            dw_in_t, dkvb = _proj_bwd_dw(*dw_args)
        else:
            d_ws_wire = jnp.stack(d_ws_all).reshape(-1, LANES).astype(jnp.bfloat16)
            dw_in_t, dkvb, gathered_ws = _proj_bwd_dw(*dw_args, gather=(d_ws_wire,))
        blocks_in = dw_in_t.reshape(4, 2, w_cols, d)
        if l > 0:
            waiting.append((g_w_in, l, blocks_in))
            dx, c0, c1 = _proj_bwd_dx(x_l, dx, dq, dkvb, dpb, w_in_ts[l], gain[l], scale1[l], f"proj_bwd_dx_{l}")
        else:
            *in_flight, token = _scatter_start(blocks_in, "scatter_start_in_0")
            dx, c0, c1 = _proj_bwd_dx(
                x_l, dx, dq, dkvb, dpb, w_in_ts[l], gain[l] + token[0, 0], scale1[l], f"proj_bwd_dx_{l}"
            )
            sent, landed = _scatter_wait(*in_flight, dx, "scatter_wait_in_0")
            g_w_in[l] = _scatter_finish((landed,), "scatter_finish_in_0", own=(sent,))[0]
        c0s, c1s = c0.sum(axis=0), c1.sum(axis=0)
        d_ada_rows[l] = jnp.concatenate([c0s, norm_gain[l] * c1s, d_gate8.sum(axis=0)])
        small[l] = (
            scale1[l, 0] * c1s,
            d_qg.sum(axis=0).reshape(N_HEADS, HEAD_DIM).sum(axis=0),
            d_kg.sum(axis=0).reshape(2, HEAD_DIM).sum(axis=0),
            d_sk.sum(axis=0)[0:N_HEADS],
            d_bs.reshape(BLK, N_GROUPS, HEAD_DIM).sum(axis=2).transpose(1, 0),
        )

    names = ("norm_gain", "q_gain", "k_gain", "sink", "b_s")
    stacked = [jnp.stack([small[l][t] for l in range(n_layers)]) for t in range(len(names))]
    d_ada = jnp.stack(d_ada_rows)
    packed, offsets = _pack_rows(stacked + [d_ada, sq_err[0, 0:1]])
    g_w_in_t, g_w_out = jnp.stack(g_w_in), jnp.stack(g_w_out)
    adam_in = (rows_first(w_in), g_w_in_t, rows_first(m_w_in), rows_first(v_w_in))
    gathered, *upd = _gather_small(packed, adam=(adam_in, (w_out, g_w_out, m_w_out, v_w_out)))
    gathered_ws = gathered_ws.reshape(N_DEV, -1, LANES)
    g_w_in = rows_first(g_w_in_t)
    upd_in, upd_out = [rows_first(u) for u in upd[0:3]], upd[3:6]
    no_weight = jnp.zeros((1,), F32)
    weights = (norm_gain, q_gain, k_gain, sink, b_s, b_ada, no_weight)
    moments_m = (m_norm_gain, m_q_gain, m_k_gain, m_sink, m_b_s, m_b_ada, no_weight)
    moments_v = (v_norm_gain, v_q_gain, v_k_gain, v_sink, v_b_s, v_b_ada, no_weight)
    w_pack, _ = _pack_rows(weights)
    m_pack, _ = _pack_rows(moments_m)
    v_pack, _ = _pack_rows(moments_v)
    shapes = [w.shape for w in weights]
    flat_ws = lambda a: a.reshape(-1, LANES)
    updated = _small_update(gathered, gathered_ws, w_pack, m_pack, v_pack, flat_ws(w_s), flat_ws(m_w_s), flat_ws(v_w_s))
    g_small, d_small, m_small, v_small = (_unpack_rows(p, offsets, shapes) for p in updated[0:4])
    ws_small = [p.reshape(w_s.shape) for p in updated[4:8]]
    loss = g_small[-1][0] * (0.5 / d)

    ada_off = offsets[-2]
    ada_n = -(-n_layers * 3 * d // (SUBLANES * LANES)) * SUBLANES
    d_ada_all = gathered[:, ada_off : ada_off + ada_n].reshape(N_DEV, -1)[:, : n_layers * 3 * d].reshape(N_DEV, n_layers, 3 * d)
    d_ada_cols = lax.dynamic_slice_in_dim(d_ada_all, my * ada_cols, ada_cols, axis=2)
    g_w_ada, *upd_ada = _ada_update(c_all[:, 0, :], d_ada_cols.transpose(1, 0, 2), w_ada, m_w_ada, v_w_ada)

    def ordered(ada_, in_, out_, small_, ws):
        ng, qg, kg, sk, bs, ba, _ = small_
        return (ada_, ba, ng, in_, qg, kg, sk, ws, bs, out_)

    grads = ordered(g_w_ada, g_w_in, g_w_out, g_small, ws_small[0])
    deltas = ordered(upd_ada[0], upd_in[0], upd_out[0], d_small, ws_small[1])
    new_m = ordered(upd_ada[1], upd_in[1], upd_out[1], m_small, ws_small[2])
    new_v = ordered(upd_ada[2], upd_in[2], upd_out[2], v_small, ws_small[3])
    return (loss, dx.reshape(x.shape), *grads, *deltas, *new_m, *new_v)
```

```python
import functools

import jax
import jax.numpy as jnp
from jax import lax
from jax.experimental import pallas as pl
from jax.experimental.pallas import tpu as pltpu

F32 = jnp.float32
MXU_DTYPE = jnp.bfloat16
MESH_ID = pl.DeviceIdType.MESH

N_DEV = 8
HEAD_DIM = 64
N_HEADS = 8
Q_PER_KV = 4
D_ATTN = 512
D_KV = 128
D_GM = 512
N_GROUPS = 8
D_MIX = D_ATTN + D_GM
BLK = 128
LANES = 128
SUBLANES = 8
N_PAIRS = D_ATTN // LANES
D_QKV = D_ATTN + 2 * D_KV
D_REST = D_ATTN + 3 * D_GM
D_IN = D_QKV + D_REST
EPS = 1e-6
NEG_INF = -1e30
ALIBI_SLOPES = tuple(2.0 ** (-8.0 * (h + 1) / N_HEADS) for h in range(N_HEADS))
Q_SCALE = 1.0 / 8.0

ADAM_LR = 0.001
ADAM_B1 = 0.9
ADAM_B2 = 0.999
ADAM_EPS = 1e-08
ADAM_WD = 0.01
ADAM_STEP = 10

TOKEN_TILE = 512
VMEM_LIMIT_BYTES = 56 * 1024 * 1024


def _params(semantics=None):
    return pltpu.CompilerParams(dimension_semantics=semantics, vmem_limit_bytes=VMEM_LIMIT_BYTES)


def _dot(a, b):
    return jnp.dot(a, b, preferred_element_type=F32)


def _dot_nt(a, b):
    return lax.dot_general(a, b, (((1,), (1,)), ((), ())), preferred_element_type=F32)


def _dot_tn(a, b):
    return lax.dot_general(a, b, (((0,), (0,)), ((), ())), preferred_element_type=F32)


def _mx(v):
    return v.astype(MXU_DTYPE)


def _lane_lo(rows):
    return lax.broadcasted_iota(jnp.int32, (rows, LANES), 1) < HEAD_DIM


def _half_ones(width=LANES):
    group_bits = HEAD_DIM.bit_length() - 1
    r = jnp.right_shift(lax.broadcasted_iota(jnp.int32, (width, width), 0), group_bits)
    c = jnp.right_shift(lax.broadcasted_iota(jnp.int32, (width, width), 1), group_bits)
    return jnp.where(r == c, 1.0, 0.0).astype(jnp.bfloat16)


WIDE = 2 * LANES


def _half_sum(v, ones):
    p1 = v.astype(jnp.bfloat16)
    p2 = (v - p1.astype(F32)).astype(jnp.bfloat16)
    return _dot(p1, ones) + _dot(p2, ones)


def _half_rms(v, ones):
    r = lax.rsqrt(_half_sum(v * v, ones) * (1.0 / HEAD_DIM) + EPS)
    return v * r, r


def _half_rms_bwd(dy, vhat, r, ones):
    return r * (dy - vhat * (_half_sum(vhat * dy, ones) * (1.0 / HEAD_DIM)))


def _group_rows(v):
    rows, n = v.shape
    return v.reshape(rows // SUBLANES, SUBLANES, n).sum(axis=0)


def _sigmoid(v):
    return 1.0 / (1.0 + jnp.exp(-v))


ROW_CHUNK = 32
VARIANT_HEADS = ((0, 2, 5, 7), (1, 3, 4, 6))
HEAD_SLOT = {h: (v, s) for v, heads in enumerate(VARIANT_HEADS) for s, h in enumerate(heads)}
STACK = Q_PER_KV * BLK


def _fill_attn_bias(bias_s):
    qi = lax.broadcasted_iota(jnp.int32, (BLK, 3 * BLK), 0)
    ci = lax.broadcasted_iota(jnp.int32, (BLK, 3 * BLK), 1)
    dist = jnp.abs(ci - BLK - qi)
    distf = dist.astype(F32)
    window = dist <= BLK
    for kind, seen in enumerate((window & (ci >= BLK), window, window & (ci < 2 * BLK))):
        for h in range(N_HEADS):
            bias_s[kind, h] = jnp.where(seen, -(ALIBI_SLOPES[h] * distf), NEG_INF)


def _block_kind(block, seq):
    assert seq >= 2 * BLK
    return jnp.where(block == 0, 0, jnp.where(block == seq // BLK - 1, 2, 1))


def _stage_queries(qn, lo_t, j, nb, qs):
    for a in range(2):
        v, slot = HEAD_SLOT[2 * j + a]
        qm = _mx(jnp.where(lo_t, qn, 0.0) if a == 0 else jnp.where(lo_t, 0.0, qn))
        for n in range(nb):
            qs[n, v, slot * BLK : (slot + 1) * BLK, :] = qm[n * BLK : (n + 1) * BLK]


def _unstack_pair(stacked, j, lo):
    (v0, s0), (v1, s1) = HEAD_SLOT[2 * j], HEAD_SLOT[2 * j + 1]
    return jnp.where(lo, stacked[v0][s0 * BLK : (s0 + 1) * BLK], stacked[v1][s1 * BLK : (s1 + 1) * BLK])


def _stage_keys(kvp_ref, qkv_ref, kvn_ref, kg, ones, tile, ks, kr, vs, vr, khat_s=None, rk_s=None):
    pieces = (
        (0, BLK, kvp_ref[:, 0:D_KV], kvp_ref[:, D_KV : 2 * D_KV]),
        (BLK, tile, qkv_ref[:, D_ATTN : D_ATTN + D_KV], qkv_ref[:, D_ATTN + D_KV : D_QKV]),
        (BLK + tile, BLK, kvn_ref[:, 0:D_KV], kvn_ref[:, D_KV : 2 * D_KV]),
    )
    for r0, n, k, v in pieces:
        khat, rk = _half_rms(k, ones)
        kn = khat * kg
        ks[r0 : r0 + n, :] = _mx(kn)
        kr[r0 : r0 + n, :] = _mx(pltpu.roll(kn, HEAD_DIM, 1))
        vs[r0 : r0 + n, :] = _mx(v)
        vr[r0 : r0 + n, :] = _mx(pltpu.roll(v, HEAD_DIM, 1))
        if khat_s is not None:
            khat_s[r0 : r0 + n, :] = khat
            rk_s[r0 : r0 + n, :] = rk


def _halo_specs(tile, seq):
    nb = tile // BLK
    last = seq // BLK - 1
    kv_col = D_ATTN // (2 * D_KV)
    prev = pl.BlockSpec((BLK, 2 * D_KV), lambda i: (jnp.maximum(i * nb - 1, 0), kv_col))
    nxt = pl.BlockSpec((BLK, 2 * D_KV), lambda i: (jnp.minimum((i + 1) * nb, last), kv_col))
    return prev, nxt


def _row_spec(tile, width):
    return pl.BlockSpec((tile, width), lambda i: (i, 0))


def _full_spec(shape):
    nd = len(shape)
    return pl.BlockSpec(shape, lambda i: (0,) * nd)


SMEM_SPEC = pl.BlockSpec(memory_space=pltpu.SMEM)
VMEM_SPEC = pl.BlockSpec(memory_space=pltpu.VMEM)
HBM_SPEC = pl.BlockSpec(memory_space=pltpu.HBM)


def _rider_steps(nt):
    return 0, (3 * nt) // 4, nt - 1


def _gather_rider(sources, gathered, sems, step, nt):
    start, forward, finish = _all_gather_stages(
        [_row_block(g, s.shape[0]) for g, s in zip(gathered, sources)], sems[0], sems[1], sources=sources, local_sems=sems[2]
    )
    at_start, at_forward, at_finish = _rider_steps(nt)
    pl.when(step == at_start)(start)

    def after_compute():
        pl.when(step == at_forward)(forward)
        pl.when(step == at_finish)(finish)

    return after_compute


def _gathered_shapes(gather):
    return [jax.ShapeDtypeStruct((N_DEV * g.shape[0], g.shape[1]), g.dtype) for g in gather]


def _ln_proj_fwd(x, gain, scale1, shift, w_in_t, name):
    seq, d = x.shape
    tile = min(TOKEN_TILE, seq)

    def body(x_ref, g_ref, s1_ref, sh_ref, wt_ref, pa_ref, pb_ref):
        xv = x_ref[...]
        r = lax.rsqrt(jnp.mean(xv * xv, axis=-1, keepdims=True) + EPS)
        h = _mx((xv * r) * g_ref[...] * s1_ref[...] + sh_ref[...])
        pa_ref[...] = _dot_nt(h, wt_ref[0:D_QKV, :])
        pb_ref[...] = _dot_nt(h, wt_ref[D_QKV:D_IN, :])

    vec = _full_spec((1, d))
    return pl.pallas_call(
        body,
        name=name,
        grid=(seq // tile,),
        in_specs=[_row_spec(tile, d), vec, vec, vec, _full_spec((D_IN, d))],
        out_specs=[_row_spec(tile, D_QKV), _row_spec(tile, D_REST)],
        out_shape=[jax.ShapeDtypeStruct((seq, D_QKV), F32), jax.ShapeDtypeStruct((seq, D_REST), F32)],
        compiler_params=_params(("parallel",)),
    )(x, gain, scale1, shift, w_in_t)


def _attn_fwd(pa, q_gain2, k_gain2, sink, name, gather=()):
    seq = pa.shape[0]
    tile = min(TOKEN_TILE, seq)
    nb = tile // BLK
    nt = seq // tile
    ext = tile + 2 * BLK
    n_ride = len(gather)
    riding = n_ride > 0

    def body(sink_ref, qkv_ref, kvp_ref, kvn_ref, qg_ref, kg_ref, *rest):
        i = pl.program_id(0)
        sources, (o_ref, p_ref, psink_ref), gathered = rest[:n_ride], rest[n_ride : n_ride + 3], rest[n_ride + 3 : 2 * n_ride + 3]
        qs, ks, kr, vs, vr, bias_s, s_scr, *sems = rest[2 * n_ride + 3 :]
        if riding:
            after_compute = _gather_rider(sources, gathered, sems, i, nt)

        @pl.when(i == 0)
        def _():
            _fill_attn_bias(bias_s)

        ones = _half_ones()
        lo = _lane_lo(BLK)
        lo_t = _lane_lo(tile)
        head_lane = lax.broadcasted_iota(jnp.int32, (ROW_CHUNK, LANES), 1)
        _stage_keys(kvp_ref, qkv_ref, kvn_ref, kg_ref[...], ones, tile, ks, kr, vs, vr)
        for j in range(N_PAIRS):
            qhat, _ = _half_rms(qkv_ref[:, j * LANES : (j + 1) * LANES], ones)
            _stage_queries(qhat * (qg_ref[...] * Q_SCALE), lo_t, j, nb, qs)

        def block(n, carry):
            r0 = pl.multiple_of(n * BLK, BLK)
            krows = pl.ds(r0, 3 * BLK)
            kind = _block_kind(i * nb + n, seq)
            for v in range(2):
                s_scr[v] = _dot_nt(qs[n, v], (kr if v else ks)[krows, :])
            for rc in range(0, BLK, ROW_CHUNK):
                p_sink = jnp.zeros((ROW_CHUNK, LANES), F32)
                for h in range(N_HEADS):
                    v, slot = HEAD_SLOT[h]
                    sink_h = sink_ref[h]
                    rows = slice(slot * BLK + rc, slot * BLK + rc + ROW_CHUNK)
                    s = s_scr[v, rows, :] + bias_s[kind, h, rc : rc + ROW_CHUNK, :]
                    m = jnp.maximum(jnp.max(s, axis=-1, keepdims=True), sink_h)
                    p = jnp.exp(s - m)
                    e_sink = jnp.exp(sink_h - m)
                    inv = 1.0 / (jnp.sum(p, axis=-1, keepdims=True) + e_sink)
                    p_ref[n, v, rows, :] = _mx(p * inv)
                    p_sink = jnp.where(head_lane == h, e_sink * inv, p_sink)
                psink_ref[pl.ds(pl.multiple_of(r0 + rc, ROW_CHUNK), ROW_CHUNK), :] = p_sink
            outs = [_dot(p_ref[n, v], (vr if v else vs)[krows, :]) for v in range(2)]
            for j in range(N_PAIRS):
                o_ref[pl.ds(r0, BLK), j * LANES : (j + 1) * LANES] = _unstack_pair(outs, j, lo)
            return carry

        lax.fori_loop(0, nb, block, 0)
        if riding:
            after_compute()

    prev, nxt = _halo_specs(tile, seq)
    vec = _full_spec((1, LANES))
    in_specs = [SMEM_SPEC, _row_spec(tile, D_QKV), prev, nxt, vec, vec]
    out_specs = [
        _row_spec(tile, D_ATTN),
        pl.BlockSpec((nb, 2, STACK, 3 * BLK), lambda i: (i, 0, 0, 0)),
        _row_spec(tile, LANES),
    ]
    out_shape = [
        jax.ShapeDtypeStruct((seq, D_ATTN), F32),
        jax.ShapeDtypeStruct((seq // BLK, 2, STACK, 3 * BLK), MXU_DTYPE),
        jax.ShapeDtypeStruct((seq, LANES), F32),
    ]
    scratch = [
        pltpu.VMEM((nb, 2, STACK, LANES), MXU_DTYPE),
        pltpu.VMEM((ext, LANES), MXU_DTYPE),
        pltpu.VMEM((ext, LANES), MXU_DTYPE),
        pltpu.VMEM((ext, LANES), MXU_DTYPE),
        pltpu.VMEM((ext, LANES), MXU_DTYPE),
        pltpu.VMEM((3, N_HEADS, BLK, 3 * BLK), F32),
        pltpu.VMEM((2, STACK, 3 * BLK), F32),
    ]
    return pl.pallas_call(
        body,
        name=name,
        grid=(nt,),
        in_specs=in_specs + [HBM_SPEC] * n_ride,
        out_specs=out_specs + [HBM_SPEC] * n_ride,
        out_shape=out_shape + _gathered_shapes(gather),
        scratch_shapes=scratch + _rider_sems(n_ride),
        compiler_params=_params(("arbitrary",)),
    )(sink, pa, pa, pa, q_gain2, k_gain2, *gather)


def _mix_out_fwd(pb, o, x, gate, w_out, w_s, b_st, name, target=None):
    seq, d = x.shape
    tile = min(TOKEN_TILE, seq)
    nb = tile // BLK
    with_loss = target is not None

    def body(pb_ref, o_ref, x_ref, gate_ref, wo_ref, ws_ref, bs_ref, *rest):
        if with_loss:
            t_ref, xo_ref, acc_ref, sv_ref, y_s, vn_s = rest

            @pl.when(pl.program_id(0) == 0)
            def _():
                acc_ref[...] = jnp.zeros_like(acc_ref)
        else:
            xo_ref, sv_ref, y_s, vn_s = rest
        ones = _half_ones(WIDE)
        lo = _lane_lo(BLK)
        ga = pb_ref[:, 0:D_ATTN]
        y_s[:, 0:D_ATTN] = _mx(o_ref[...] * (ga * _sigmoid(ga)))
        for j in range(D_GM // WIDE):
            vhat, _ = _half_rms(pb_ref[:, 2 * D_GM + j * WIDE : 2 * D_GM + (j + 1) * WIDE], ones)
            vn_s[:, j * WIDE : (j + 1) * WIDE] = _mx(vhat)

        def chunk(n, carry):
            rows = pl.ds(pl.multiple_of(n * BLK, BLK), BLK)
            for j in range(N_PAIRS):
                cols = slice(j * LANES, (j + 1) * LANES)
                vn = vn_s[rows, cols]
                sv = jnp.where(lo, _dot(ws_ref[2 * j], vn), _dot(ws_ref[2 * j + 1], vn)) + bs_ref[:, cols]
                sv_ref[rows, cols] = sv
                u = pb_ref[rows, D_ATTN + j * LANES : D_ATTN + (j + 1) * LANES]
                gg = pb_ref[rows, D_ATTN + 2 * D_GM + j * LANES : D_ATTN + 2 * D_GM + (j + 1) * LANES]
                y_s[rows, D_ATTN + j * LANES : D_ATTN + (j + 1) * LANES] = _mx((u * sv) * (gg * _sigmoid(gg)))
            return carry

        lax.fori_loop(0, nb, chunk, 0)
        y = x_ref[...] + gate_ref[...] * _dot(y_s[...], wo_ref[...])
        if with_loss:
            e = y - t_ref[...]
            xo_ref[...] = e * (1.0 / d)
            acc_ref[...] += jnp.sum(jnp.sum(e * e, axis=-1, keepdims=True), axis=0, keepdims=True)
        else:
            xo_ref[...] = y

    row = _row_spec(tile, d)
    acc_shape = (SUBLANES, LANES)
    return pl.pallas_call(
        body,
        name=name,
        grid=(seq // tile,),
        in_specs=[
            _row_spec(tile, D_REST),
            _row_spec(tile, D_ATTN),
            row,
            _full_spec((1, d)),
            _full_spec((D_MIX, d)),
            _full_spec((N_GROUPS, BLK, BLK)),
            _full_spec((BLK, D_GM)),
        ]
        + ([row] if with_loss else []),
        out_specs=[row] + ([_full_spec(acc_shape)] if with_loss else []) + [_row_spec(tile, D_GM)],
        out_shape=[jax.ShapeDtypeStruct((seq, d), F32)]
        + ([jax.ShapeDtypeStruct(acc_shape, F32)] if with_loss else [])
        + [jax.ShapeDtypeStruct((seq, D_GM), F32)],
        scratch_shapes=[pltpu.VMEM((tile, D_MIX), MXU_DTYPE), pltpu.VMEM((tile, D_GM), MXU_DTYPE)],
        compiler_params=_params(("arbitrary",) if with_loss else ("parallel",)),
    )(pb, o, x, gate, w_out, w_s, b_st, *([target] if with_loss else []))


def _mix_out_bwd(dxn, pb, o, sv, gate, w_out, w_s_t, name):
    seq, d = dxn.shape
    tile = min(TOKEN_TILE, seq)
    nb = tile // BLK
    nt = seq // tile

    def body(dxn_ref, pb_ref, o_ref, sv_ref, gate_ref, wo_ref, wst_ref,
             dpb_ref, do_ref, dwo_ref, dgate_ref, dws_ref, dbs_ref, g_ref, y_s, dy_s, vn_s, rv_s, vnb_s, dsv_s, dvn_s):
        @pl.when(pl.program_id(0) == 0)
        def _():
            g_ref[...] = jnp.zeros_like(g_ref)
            dws_ref[...] = jnp.zeros_like(dws_ref)
            dbs_ref[...] = jnp.zeros_like(dbs_ref)

        ones = _half_ones(WIDE)
        lo = _lane_lo(BLK)
        c_u = slice(D_ATTN, D_ATTN + D_GM)
        c_vg = slice(D_ATTN + D_GM, D_ATTN + 2 * D_GM)
        c_gg = slice(D_ATTN + 2 * D_GM, D_REST)
        dxv = dxn_ref[...]
        dy_s[...] = _dot_nt(_mx(dxv * gate_ref[...]), wo_ref[...])
        ga = pb_ref[:, 0:D_ATTN]
        sig = _sigmoid(ga)
        sil = ga * sig
        ov = o_ref[...]
        y_s[:, 0:D_ATTN] = _mx(ov * sil)
        da = dy_s[:, 0:D_ATTN]
        do_ref[...] = da * sil
        dpb_ref[:, 0:D_ATTN] = (da * ov * (sig * (1.0 + ga * (1.0 - sig)))).astype(dpb_ref.dtype)
        for j in range(D_GM // WIDE):
            cols = slice(j * WIDE, (j + 1) * WIDE)
            vhat, rv = _half_rms(pb_ref[:, 2 * D_GM + j * WIDE : 2 * D_GM + (j + 1) * WIDE], ones)
            vn_s[:, cols] = vhat
            rv_s[:, cols] = rv
            vnb_s[:, cols] = _mx(vhat)

        def gating(n, carry):
            rows = pl.ds(pl.multiple_of(n * BLK, BLK), BLK)
            sv = sv_ref[rows, :]
            u = pb_ref[rows, c_u]
            gg = pb_ref[rows, c_gg]
            sg = _sigmoid(gg)
            silg = gg * sg
            m0 = u * sv
            y_s[rows, D_ATTN:D_MIX] = _mx(m0 * silg)
            dm = dy_s[rows, D_ATTN:D_MIX]
            dm0 = dm * silg
            dpb_ref[rows, c_gg] = (dm * m0 * (sg * (1.0 + gg * (1.0 - sg)))).astype(dpb_ref.dtype)
            dpb_ref[rows, c_u] = (dm0 * sv).astype(dpb_ref.dtype)
            dsv = dm0 * u
            dsv_s[rows, :] = _mx(dsv)
            dbs_ref[...] += dsv
            return carry

        lax.fori_loop(0, nb, gating, 0)

        def spatial_bwd(n, carry):
            rows = pl.ds(pl.multiple_of(n * BLK, BLK), BLK)
            for j in range(N_PAIRS):
                cols = slice(j * LANES, (j + 1) * LANES)
                dsv = dsv_s[rows, cols]
                dvn_s[rows, cols] = jnp.where(lo, _dot(wst_ref[2 * j], dsv), _dot(wst_ref[2 * j + 1], dsv))
            return carry

        lax.fori_loop(0, nb, spatial_bwd, 0)
        zero = jnp.zeros((BLK, LANES), MXU_DTYPE)
        for j in range(N_PAIRS):
            cols = slice(j * LANES, (j + 1) * LANES)
            chunks = [dsv_s[n * BLK : (n + 1) * BLK, cols] for n in range(nb)]
            vn_all = jnp.concatenate([vnb_s[n * BLK : (n + 1) * BLK, cols] for n in range(nb)], axis=1)
            dws_ref[2 * j] += _dot_nt(jnp.concatenate([jnp.where(lo, c, zero) for c in chunks], axis=1), vn_all)
            dws_ref[2 * j + 1] += _dot_nt(jnp.concatenate([jnp.where(lo, zero, c) for c in chunks], axis=1), vn_all)
        for j in range(D_GM // WIDE):
            cols = slice(j * WIDE, (j + 1) * WIDE)
            dpb_ref[:, D_ATTN + D_GM + j * WIDE : D_ATTN + D_GM + (j + 1) * WIDE] = _half_rms_bwd(
                dvn_s[:, cols], vn_s[:, cols], rv_s[:, cols], ones
            ).astype(dpb_ref.dtype)
        g_ref[...] += _dot_tn(y_s[...], _mx(dxv))

        @pl.when(pl.program_id(0) == nt - 1)
        def _():
            gv = g_ref[...]
            dwo_ref[...] = (gv * gate_ref[...]).astype(dwo_ref.dtype)
            dgate_ref[...] = _group_rows(gv * wo_ref[...].astype(F32))

    return pl.pallas_call(
        body,
        name=name,
        grid=(seq // tile,),
        in_specs=[
            _row_spec(tile, d),
            _row_spec(tile, D_REST),
            _row_spec(tile, D_ATTN),
            _row_spec(tile, D_GM),
            _full_spec((1, d)),
            _full_spec((D_MIX, d)),
            _full_spec((N_GROUPS, BLK, BLK)),
        ],
        out_specs=[
            _row_spec(tile, D_REST),
            _row_spec(tile, D_ATTN),
            _full_spec((D_MIX, d)),
            _full_spec((SUBLANES, d)),
            _full_spec((N_GROUPS, BLK, BLK)),
            _full_spec((BLK, D_GM)),
        ],
        out_shape=[
            jax.ShapeDtypeStruct((seq, D_REST), MXU_DTYPE),
            jax.ShapeDtypeStruct((seq, D_ATTN), F32),
            jax.ShapeDtypeStruct((D_MIX, d), jnp.bfloat16),
            jax.ShapeDtypeStruct((SUBLANES, d), F32),
            jax.ShapeDtypeStruct((N_GROUPS, BLK, BLK), F32),
            jax.ShapeDtypeStruct((BLK, D_GM), F32),
        ],
        scratch_shapes=[
            pltpu.VMEM((D_MIX, d), F32),
            pltpu.VMEM((tile, D_MIX), MXU_DTYPE),
            pltpu.VMEM((tile, D_MIX), F32),
            pltpu.VMEM((tile, D_GM), F32),
            pltpu.VMEM((tile, D_GM), F32),
            pltpu.VMEM((tile, D_GM), MXU_DTYPE),
            pltpu.VMEM((tile, D_GM), MXU_DTYPE),
            pltpu.VMEM((tile, D_GM), F32),
        ],
        compiler_params=_params(("arbitrary",)),
    )(dxn, pb, o, sv, gate, w_out, w_s_t)


def _attn_bwd(pa, o, do, probs, p_sink, q_gain2, k_gain2, name, scatter=()):
    seq = pa.shape[0]
    tile = min(TOKEN_TILE, seq)
    nb = tile // BLK
    nt = seq // tile
    ext = tile + 2 * BLK
    n_ride = len(scatter)
    riding = n_ride > 0

    def body(qkv_ref, kvp_ref, kvn_ref, o_ref, do_ref, p_ref, psink_ref, qg_ref, kg_ref, *rest):
        i = pl.program_id(0)
        blocks, rest = rest[:n_ride], rest[n_ride:]
        dq_ref, dkv_ref, hp_ref, hn_ref, dqg_ref, dkg_ref, dsk_ref = rest[:7]
        landing, rest = rest[7 : 7 + n_ride], rest[7 + n_ride :]
        (qs, dos, qhat_s, rq_s, ks, kr, vs, vr, khat_s, rk_s, dqn_s, dka, dva, dp_scr, ds_scr) = rest[:15]
        if riding:
            start, finish = _scatter_stages(blocks, landing, *rest[15:])
            at_start, _, at_finish = _rider_steps(nt)
            pl.when(i == at_start)(start)

        @pl.when(i == 0)
        def _():
            dqg_ref[...] = jnp.zeros_like(dqg_ref)
            dkg_ref[...] = jnp.zeros_like(dkg_ref)
            dsk_ref[...] = jnp.zeros_like(dsk_ref)

        ones = _half_ones()
        lo = _lane_lo(BLK)
        lo_t = _lane_lo(tile)
        lo_c = _lane_lo(ROW_CHUNK)
        qg = qg_ref[...] * Q_SCALE
        kg = kg_ref[...]
        _stage_keys(kvp_ref, qkv_ref, kvn_ref, kg, ones, tile, ks, kr, vs, vr, khat_s, rk_s)
        head_lane = lax.broadcasted_iota(jnp.int32, (tile, LANES), 1)
        d_rows = jnp.zeros((tile, LANES), F32)
        for j in range(N_PAIRS):
            cols = slice(j * LANES, (j + 1) * LANES)
            qhat, rq = _half_rms(qkv_ref[:, cols], ones)
            qhat_s[:, cols] = qhat
            rq_s[:, cols] = rq
            _stage_queries(qhat * qg, lo_t, j, nb, qs)
            dov = do_ref[:, cols]
            _stage_queries(dov, lo_t, j, nb, dos)
            d_pair = _half_sum(dov * o_ref[:, cols], ones)
            d_rows = jnp.where(head_lane == 2 * j, d_pair, d_rows)
            d_rows = jnp.where(head_lane == 2 * j + 1, pltpu.roll(d_pair, HEAD_DIM, 1), d_rows)
        dsk_ref[...] -= _group_rows(psink_ref[...] * d_rows)
        dka[...] = jnp.zeros_like(dka)
        dva[...] = jnp.zeros_like(dva)

        def block(n, carry):
            r0 = pl.multiple_of(n * BLK, BLK)
            krows = pl.ds(r0, 3 * BLK)
            for v in range(2):
                dp_scr[v] = _dot_nt(dos[n, v], (vr if v else vs)[krows, :])
            for h in range(N_HEADS):
                v, slot = HEAD_SLOT[h]
                j, a = divmod(h, 2)
                cols = slice(j * LANES, (j + 1) * LANES)
                for rc in range(0, BLK, ROW_CHUNK):
                    rows = slice(slot * BLK + rc, slot * BLK + rc + ROW_CHUNK)
                    trows = pl.ds(pl.multiple_of(r0 + rc, ROW_CHUNK), ROW_CHUNK)
                    prod = do_ref[trows, cols] * o_ref[trows, cols]
                    prod = jnp.where(lo_c, prod, 0.0) if a == 0 else jnp.where(lo_c, 0.0, prod)
                    dcol = jnp.sum(prod, axis=-1, keepdims=True)
                    ds_scr[v, rows, :] = _mx(p_ref[n, v, rows, :].astype(F32) * (dp_scr[v, rows, :] - dcol))
            dqv = []
            for v in range(2):
                dqv.append(_dot(ds_scr[v], (kr if v else ks)[krows, :]))
                dka[v, krows, :] += _dot_tn(ds_scr[v], qs[n, v])
                dva[v, krows, :] += _dot_tn(p_ref[n, v], dos[n, v])
            for j in range(N_PAIRS):
                dqn_s[pl.ds(r0, BLK), j * LANES : (j + 1) * LANES] = _unstack_pair(dqv, j, lo)
            return carry

        lax.fori_loop(0, nb, block, 0)
        for j in range(N_PAIRS):
            cols = slice(j * LANES, (j + 1) * LANES)
            dqn = dqn_s[:, cols]
            qhat = qhat_s[:, cols]
            dqg_ref[:, cols] += _group_rows(dqn * qhat) * Q_SCALE
            dq_ref[:, cols] = _half_rms_bwd(dqn * qg, qhat, rq_s[:, cols], ones).astype(dq_ref.dtype)
        dkn = dka[0] + pltpu.roll(dka[1], HEAD_DIM, 1)
        khat = khat_s[...]
        dkg_ref[...] += _group_rows(dkn * khat)
        dk = _half_rms_bwd(dkn * kg, khat, rk_s[...], ones)
        dv = dva[0] + pltpu.roll(dva[1], HEAD_DIM, 1)
        hp_ref[:, 0:D_KV] = dk[0:BLK]
        hp_ref[:, D_KV : 2 * D_KV] = dv[0:BLK]
        dkv_ref[:, 0:D_KV] = dk[BLK : BLK + tile]
        dkv_ref[:, D_KV : 2 * D_KV] = dv[BLK : BLK + tile]
        hn_ref[:, 0:D_KV] = dk[BLK + tile : ext]
        hn_ref[:, D_KV : 2 * D_KV] = dv[BLK + tile : ext]
        if riding:
            pl.when(i == at_finish)(finish)

    prev, nxt = _halo_specs(tile, seq)
    vec = _full_spec((1, LANES))
    halo = pl.BlockSpec((None, BLK, 2 * D_KV), lambda i: (i, 0, 0))
    return pl.pallas_call(
        body,
        name=name,
        grid=(nt,),
        in_specs=[
            _row_spec(tile, D_QKV),
            prev,
            nxt,
            _row_spec(tile, D_ATTN),
            _row_spec(tile, D_ATTN),
            pl.BlockSpec((nb, 2, STACK, 3 * BLK), lambda i: (i, 0, 0, 0)),
            _row_spec(tile, LANES),
            vec,
            vec,
        ]
        + [HBM_SPEC] * n_ride,
        out_specs=[
            _row_spec(tile, D_ATTN),
            _row_spec(tile, 2 * D_KV),
            halo,
            halo,
            _full_spec((SUBLANES, D_ATTN)),
            _full_spec((SUBLANES, LANES)),
            _full_spec((SUBLANES, LANES)),
        ]
        + [HBM_SPEC] * n_ride,
        out_shape=[
            jax.ShapeDtypeStruct((seq, D_ATTN), MXU_DTYPE),
            jax.ShapeDtypeStruct((seq, 2 * D_KV), F32),
            jax.ShapeDtypeStruct((nt, BLK, 2 * D_KV), F32),
            jax.ShapeDtypeStruct((nt, BLK, 2 * D_KV), F32),
            jax.ShapeDtypeStruct((SUBLANES, D_ATTN), F32),
            jax.ShapeDtypeStruct((SUBLANES, LANES), F32),
            jax.ShapeDtypeStruct((SUBLANES, LANES), F32),
        ]
        + _landing_shapes(scatter),
        scratch_shapes=[
            pltpu.VMEM((nb, 2, STACK, LANES), MXU_DTYPE),
            pltpu.VMEM((nb, 2, STACK, LANES), MXU_DTYPE),
            pltpu.VMEM((tile, D_ATTN), F32),
            pltpu.VMEM((tile, D_ATTN), F32),
            pltpu.VMEM((ext, LANES), MXU_DTYPE),
            pltpu.VMEM((ext, LANES), MXU_DTYPE),
            pltpu.VMEM((ext, LANES), MXU_DTYPE),
            pltpu.VMEM((ext, LANES), MXU_DTYPE),
            pltpu.VMEM((ext, LANES), F32),
            pltpu.VMEM((ext, LANES), F32),
            pltpu.VMEM((tile, D_ATTN), F32),
            pltpu.VMEM((2, ext, LANES), F32),
            pltpu.VMEM((2, ext, LANES), F32),
            pltpu.VMEM((2, STACK, 3 * BLK), F32),
            pltpu.VMEM((2, STACK, 3 * BLK), MXU_DTYPE),
        ]
        + _rider_sems(n_ride),
        compiler_params=_params(("arbitrary",)),
    )(pa, pa, pa, o, do, probs, p_sink, q_gain2, k_gain2, *scatter)


def _halo_in_specs(tile, nt):
    from_prev = pl.BlockSpec((None, BLK, 2 * D_KV), lambda i: (jnp.maximum(i - 1, 0), 0, 0))
    from_next = pl.BlockSpec((None, BLK, 2 * D_KV), lambda i: (jnp.minimum(i + 1, nt - 1), 0, 0))
    return from_prev, from_next


def _landing_shapes(scatter):
    return [jax.ShapeDtypeStruct((N_DEV,) + b.shape[2:], b.dtype) for b in scatter]


def _rider_sems(n_ride):
    if not n_ride:
        return []
    return [pltpu.SemaphoreType.DMA((7 * n_ride,)), pltpu.SemaphoreType.DMA((7 * n_ride,)), pltpu.SemaphoreType.DMA((n_ride,))]


def _proj_bwd_dx(x, dxn, dq, dkvb, dpb, w_in_t, gain, scale1, name):
    seq, d = x.shape
    tile = min(TOKEN_TILE, seq)

    def row(width):
        return _row_spec(tile, width)

    def body(x_ref, dxn_ref, dq_ref, dkvb_ref, dpb_ref, wt_ref, g_ref, s1_ref, dx_ref, c0_ref, c1_ref):
        @pl.when(pl.program_id(0) == 0)
        def _():
            c0_ref[...] = jnp.zeros_like(c0_ref)
            c1_ref[...] = jnp.zeros_like(c1_ref)

        dh = (
            _dot(dq_ref[...], wt_ref[0:D_ATTN, :])
            + _dot(dkvb_ref[...], wt_ref[D_ATTN:D_QKV, :])
            + _dot(dpb_ref[...], wt_ref[D_QKV:D_IN, :])
        )
        xv = x_ref[...]
        r = lax.rsqrt(jnp.mean(xv * xv, axis=-1, keepdims=True) + EPS)
        xn = xv * r
        c0_ref[...] += _group_rows(dh)
        c1_ref[...] += _group_rows(dh * xn)
        dxn_ = dh * (g_ref[...] * s1_ref[...])
        dx_ref[...] = dxn_ref[...] + r * (dxn_ - xn * jnp.mean(xn * dxn_, axis=-1, keepdims=True))

    vec = _full_spec((1, d))
    return pl.pallas_call(
        body,
        name=name,
        grid=(seq // tile,),
        in_specs=[row(d), row(d), row(D_ATTN), row(2 * D_KV), row(D_REST), _full_spec((D_IN, d)), vec, vec],
        out_specs=[row(d), _full_spec((SUBLANES, d)), _full_spec((SUBLANES, d))],
        out_shape=[
            jax.ShapeDtypeStruct((seq, d), F32),
            jax.ShapeDtypeStruct((SUBLANES, d), F32),
            jax.ShapeDtypeStruct((SUBLANES, d), F32),
        ],
        compiler_params=_params(("arbitrary",)),
    )(x, dxn, dq, dkvb, dpb, w_in_t, gain, scale1)


def _proj_bwd_dw(x, gain, scale1, shift, dq, dkv, halo_prev, halo_next, dpb, name, gather=()):
    seq, d = x.shape
    tile = min(TOKEN_TILE, seq)
    nt = seq // tile
    assert tile >= 2 * BLK
    n_ride = len(gather)

    def body(x_ref, g_ref, s1_ref, sh_ref, dq_ref, dkv_ref, hn_ref, hp_ref, dpb_ref, *rest):
        i = pl.program_id(0)
        sources, rest = rest[:n_ride], rest[n_ride:]
        dw_ref, dkvb_ref = rest[:2]
        gathered, (acc, *sems) = rest[2 : 2 + n_ride], rest[2 + n_ride :]
        after_compute = _gather_rider(sources, gathered, sems, i, nt) if n_ride else None

        @pl.when(i == 0)
        def _():
            acc[...] = jnp.zeros_like(acc)

        top = dkv_ref[0:BLK, :] + jnp.where(i > 0, hn_ref[...], 0.0)
        bot = dkv_ref[tile - BLK : tile, :] + jnp.where(i < nt - 1, hp_ref[...], 0.0)
        dkvb_ref[0:BLK, :] = top.astype(dkvb_ref.dtype)
        dkvb_ref[tile - BLK : tile, :] = bot.astype(dkvb_ref.dtype)
        if tile > 2 * BLK:
            dkvb_ref[BLK : tile - BLK, :] = dkv_ref[BLK : tile - BLK, :].astype(dkvb_ref.dtype)
        xv = x_ref[...]
        r = lax.rsqrt(jnp.mean(xv * xv, axis=-1, keepdims=True) + EPS)
        h = _mx((xv * r) * g_ref[...] * s1_ref[...] + sh_ref[...])
        acc[0:D_ATTN, :] += _dot_tn(dq_ref[...], h)
        acc[D_ATTN:D_QKV, :] += _dot_tn(dkvb_ref[...], h)
        acc[D_QKV:D_IN, :] += _dot_tn(dpb_ref[...], h)

        @pl.when(i == nt - 1)
        def _():
            dw_ref[...] = acc[...].astype(dw_ref.dtype)

        if n_ride:
            after_compute()

    from_prev, from_next = _halo_in_specs(tile, nt)
    vec = _full_spec((1, d))
    return pl.pallas_call(
        body,
        name=name,
        grid=(nt,),
        in_specs=[
            _row_spec(tile, d),
            vec,
            vec,
            vec,
            _row_spec(tile, D_ATTN),
            _row_spec(tile, 2 * D_KV),
            from_prev,
            from_next,
            _row_spec(tile, D_REST),
        ]
        + [HBM_SPEC] * n_ride,
        out_specs=[_full_spec((D_IN, d)), _row_spec(tile, 2 * D_KV)] + [HBM_SPEC] * n_ride,
        out_shape=[jax.ShapeDtypeStruct((D_IN, d), jnp.bfloat16), jax.ShapeDtypeStruct((seq, 2 * D_KV), MXU_DTYPE)]
        + _gathered_shapes(gather),
        scratch_shapes=[pltpu.VMEM((D_IN, d), F32)] + _rider_sems(n_ride),
        compiler_params=_params(("arbitrary",)),
    )(x, gain, scale1, shift, dq, dkv, halo_next, halo_prev, dpb, *gather)


def _adamw_math(w, g, m, v):
    m = ADAM_B1 * m + (1.0 - ADAM_B1) * g
    v = ADAM_B2 * v + (1.0 - ADAM_B2) * (g * g)
    m_hat = m / (1.0 - ADAM_B1**ADAM_STEP)
    v_hat = v / (1.0 - ADAM_B2**ADAM_STEP)
    delta = -ADAM_LR * (m_hat / (jnp.sqrt(v_hat) + ADAM_EPS) + ADAM_WD * w)
    return delta, m, v


def _small_update(gathered, gathered_ws, w, m, v, ws, m_ws, v_ws):
    def body(ga_ref, gws_ref, w_ref, m_ref, v_ref, ws_ref, mws_ref, vws_ref, *outs):
        for src, refs, out in ((ga_ref, (w_ref, m_ref, v_ref), outs[0:4]), (gws_ref, (ws_ref, mws_ref, vws_ref), outs[4:8])):
            g = src[0].astype(F32)
            for j in range(1, N_DEV):
                g = g + src[j].astype(F32)
            out[0][...] = g
            out[1][...], out[2][...], out[3][...] = _adamw_math(refs[0][...], g, refs[1][...], refs[2][...])

    shapes = [jax.ShapeDtypeStruct(w.shape, F32)] * 4 + [jax.ShapeDtypeStruct(ws.shape, F32)] * 4
    return pl.pallas_call(
        body,
        name="small_update",
        in_specs=[VMEM_SPEC] * 8,
        out_specs=[VMEM_SPEC] * 8,
        out_shape=shapes,
        compiler_params=_params(),
    )(gathered, gathered_ws, w, m, v, ws, m_ws, v_ws)


def _ada_update(c_all, d_ada_cols, w, m, v):
    n_layers, d, width = w.shape
    n_seq = c_all.shape[0]
    rows = min(2 * LANES, d)

    def body(c_ref, da_ref, w_ref, m_ref, v_ref, g_ref, d_ref, mo_ref, vo_ref):
        cv = c_ref[...]
        cond = cv * _sigmoid(cv)
        g = lax.dot_general(
            cond, da_ref[0], (((0,), (0,)), ((), ())), preferred_element_type=F32, precision=lax.Precision.HIGHEST
        )
        g_ref[0] = g
        d_ref[0], mo_ref[0], vo_ref[0] = _adamw_math(w_ref[0], g, m_ref[0], v_ref[0])

    slab = pl.BlockSpec((1, rows, width), lambda l, r: (l, r, 0))
    return pl.pallas_call(
        body,
        name="ada_update",
        grid=(n_layers, d // rows),
        in_specs=[
            pl.BlockSpec((n_seq, rows), lambda l, r: (0, r)),
            pl.BlockSpec((1, n_seq, width), lambda l, r: (l, 0, 0)),
            slab,
            slab,
            slab,
        ],
        out_specs=[slab] * 4,
        out_shape=[jax.ShapeDtypeStruct(w.shape, F32)] * 4,
        compiler_params=_params(("arbitrary", "arbitrary")),
    )(c_all, d_ada_cols, w, m, v)


def _position():
    return lax.axis_index("x"), lax.axis_index("y"), lax.axis_index("c")


def _flip(pos, k):
    x, y, c = pos
    return (1 - x if k & 4 else x, 1 - y if k & 2 else y, 1 - c if k & 1 else c)


def _index(pos):
    x, y, c = pos
    return 4 * x + 2 * y + c


def _remote(src, dst, send_sem, recv_sem, to):
    return pltpu.make_async_remote_copy(
        src_ref=src, dst_ref=dst, send_sem=send_sem, recv_sem=recv_sem, device_id=to, device_id_type=MESH_ID
    )


def _all_gather_stages(slots, send_sems, recv_sems, sources=None, local_sems=None):
    me = _position()
    sibling = _flip(me, 1)
    others = (4, 2, 6)
    arrays = range(len(slots))

    def copy(t, k, block, to, own=False):
        slot = slots[t](_index(block))
        src = sources[t] if own and sources is not None else slot
        return _remote(src, slot, send_sems.at[7 * t + k], recv_sems.at[7 * t + k], to)

    def first(t):
        return [copy(t, 0, me, sibling, own=True)] + [copy(t, 1 + j, me, _flip(me, f), own=True) for j, f in enumerate(others)]

    def passed(t, j):
        return copy(t, 4 + j, _flip(me, others[j]), sibling)

    def local(t):
        return pltpu.make_async_copy(sources[t], slots[t](_index(me)), local_sems.at[t])

    def start():
        for t in arrays:
            if sources is not None:
                local(t).start()
            for cp in first(t):
                cp.start()

    def forward():
        for j, f in enumerate(others):
            for t in arrays:
                copy(t, 1 + j, _flip(me, f), me).wait_recv()
                passed(t, j).start()

    def finish():
        for t in arrays:
            copy(t, 0, sibling, me).wait_recv()
            for j, f in enumerate(others):
                copy(t, 4 + j, _flip(sibling, f), me).wait_recv()
        for t in arrays:
            for cp in first(t) + [passed(t, j) for j in range(len(others))]:
                cp.wait_send()
            if sources is not None:
                local(t).wait()

    return start, forward, finish


def _two_level_all_gather(slots, send_sems, recv_sems, between=None):
    start, forward, finish = _all_gather_stages(slots, send_sems, recv_sems)
    start()
    if between is not None:
        between()
    forward()
    finish()


def _row_block(ref, rows):
    return lambda j: ref.at[pl.ds(pl.multiple_of(j * rows, 16), rows), :]


def _scatter_stages(blocks, landing, send_sems, recv_sems, local_sems):
    me = _position()
    my = _index(me)
    arrays = range(len(blocks))

    def copy(t, k):
        px, py, pc = to = _flip(me, k)
        return _remote(blocks[t].at[2 * px + py, pc], landing[t].at[my], send_sems.at[7 * t + k - 1], recv_sems.at[7 * t + k - 1], to)

    def arrival(t, k):
        slot = landing[t].at[_index(_flip(me, k))]
        return _remote(slot, slot, send_sems.at[7 * t + k - 1], recv_sems.at[7 * t + k - 1], _flip(me, k))

    def local(t):
        x, y, c = me
        return pltpu.make_async_copy(blocks[t].at[2 * x + y, c], landing[t].at[my], local_sems.at[t])

    def start():
        for t in arrays:
            local(t).start()
            for k in range(1, N_DEV):
                copy(t, k).start()

    def finish():
        for t in arrays:
            for k in range(1, N_DEV):
                arrival(t, k).wait_recv()
        for t in arrays:
            for k in range(1, N_DEV):
                copy(t, k).wait_send()
            local(t).wait()

    return start, finish


def _ada_exchange(c_ref, w_ref, call_ref, parts_ref, sbuf, sem_s1, sem_r1, sem_s2, sem_r2):
    d = c_ref.shape[-1]
    n_layers = w_ref.shape[0]
    me = _position()
    my = _index(me)
    call_ref[my] = jnp.broadcast_to(c_ref[...], (SUBLANES, d))
    mine = call_ref.at[my]
    first = [_remote(mine, mine, sem_s1.at[k - 1], sem_r1.at[k - 1], _flip(me, k)) for k in range(1, N_DEV)]
    for cp in first:
        cp.start()
    for k in range(1, N_DEV):
        theirs = call_ref.at[_index(_flip(me, k))]
        _remote(theirs, theirs, sem_s1.at[k - 1], sem_r1.at[k - 1], _flip(me, k)).wait_recv()
    cv = call_ref[...].reshape(N_DEV * SUBLANES, d)
    cond = cv * _sigmoid(cv)
    for l in range(n_layers):
        rows = jnp.dot(cond, w_ref[l], preferred_element_type=F32, precision=lax.Precision.HIGHEST)
        for b in range(N_DEV):
            sbuf[b, l] = rows[b * SUBLANES : (b + 1) * SUBLANES]
    parts_ref[my] = sbuf[my]
    second = []
    for k in range(1, N_DEV):
        to = _flip(me, k)
        second.append(_remote(sbuf.at[_index(to)], parts_ref.at[my], sem_s2.at[k - 1], sem_r2.at[k - 1], to))
    for cp in second:
        cp.start()
    for k in range(1, N_DEV):
        theirs = parts_ref.at[_index(_flip(me, k))]
        _remote(theirs, theirs, sem_s2.at[k - 1], sem_r2.at[k - 1], _flip(me, k)).wait_recv()
    for cp in first + second:
        cp.wait_send()


def _gather_weights(w_in_t, w_out, c_row, w_ada):
    n_layers, rows_in, d = w_in_t.shape
    width = w_ada.shape[2]

    def body(wi_ref, wo_ref, c_ref, wa_ref, gi_ref, si_ref, so_ref, call_ref, parts_ref, sbuf, send_sems, recv_sems, own_sem, *ada_sems):
        si_ref[0] = wi_ref[0].astype(si_ref.dtype)
        start, forward, finish = _all_gather_stages(
            (_row_block(gi_ref, rows_in),), send_sems, recv_sems, sources=(si_ref.at[0],), local_sems=own_sem
        )
        start()
        if n_layers > 1:
            si_ref[1:] = wi_ref[1:].astype(si_ref.dtype)
        so_ref[...] = wo_ref[...].astype(so_ref.dtype)
        _ada_exchange(c_ref, wa_ref, call_ref, parts_ref, sbuf, *ada_sems)
        forward()
        finish()

    return pl.pallas_call(
        body,
        name="gather_weights",
        in_specs=[VMEM_SPEC] * 4,
        out_specs=[HBM_SPEC] + [VMEM_SPEC] * 4,
        out_shape=[
            jax.ShapeDtypeStruct((N_DEV * rows_in, d), MXU_DTYPE),
            jax.ShapeDtypeStruct(w_in_t.shape, MXU_DTYPE),
            jax.ShapeDtypeStruct(w_out.shape, MXU_DTYPE),
            jax.ShapeDtypeStruct((N_DEV, SUBLANES, d), F32),
            jax.ShapeDtypeStruct((N_DEV, n_layers, SUBLANES, width), F32),
        ],
        scratch_shapes=[
            pltpu.VMEM((N_DEV, n_layers, SUBLANES, width), F32),
            pltpu.SemaphoreType.DMA((7,)),
            pltpu.SemaphoreType.DMA((7,)),
            pltpu.SemaphoreType.DMA((1,)),
        ]
        + [pltpu.SemaphoreType.DMA((N_DEV - 1,))] * 4,
        compiler_params=_params(),
    )(w_in_t, w_out, c_row, w_ada)


def _gather_small(packed, adam=()):
    n_adam = len(adam)
    n_in, n_out = 4 * n_adam, 3 * n_adam

    def body(p_ref, *rest):
        operands, rest = rest[:n_in], rest[n_in:]
        g_ref, results, rest = rest[0], rest[1 : 1 + n_out], rest[1 + n_out :]
        send_sems, recv_sems, load_sems, store_sems = rest[:4]
        loaded, stored = rest[4 : 4 + n_in], rest[4 + n_in :]
        loads = [pltpu.make_async_copy(operands[k], loaded[k], load_sems.at[k]) for k in range(n_in)]
        stores = [pltpu.make_async_copy(stored[k], results[k], store_sems.at[k]) for k in range(n_out)]
        for load in loads:
            load.start()
        g_ref[_index(_position())] = p_ref[...]

        def updates():
            for t in range(n_adam):
                for load in loads[4 * t : 4 * t + 4]:
                    load.wait()
                w_ref, gr_ref, m_ref, v_ref = loaded[4 * t : 4 * t + 4]
                new = _adamw_math(w_ref[...], gr_ref[...], m_ref[...], v_ref[...])
                for k, value in zip(range(3 * t, 3 * t + 3), new):
                    stored[k][...] = value
                    stores[k].start()

        _two_level_all_gather((lambda j: g_ref.at[j],), send_sems, recv_sems, between=updates)
        for store in stores:
            store.wait()

    moved_in = [a for q in adam for a in q]
    moved_out = [jax.ShapeDtypeStruct(q[0].shape, F32) for q in adam for _ in range(3)]
    return pl.pallas_call(
        body,
        name="gather_small",
        in_specs=[VMEM_SPEC] + [HBM_SPEC] * n_in,
        out_specs=[VMEM_SPEC] + [HBM_SPEC] * n_out,
        out_shape=[jax.ShapeDtypeStruct((N_DEV,) + packed.shape, F32)] + moved_out,
        scratch_shapes=[pltpu.SemaphoreType.DMA((7,)), pltpu.SemaphoreType.DMA((7,))]
        + [pltpu.SemaphoreType.DMA((max(n_in, 1),)), pltpu.SemaphoreType.DMA((max(n_out, 1),))]
        + [pltpu.VMEM(a.shape, F32) for a in moved_in]
        + [pltpu.VMEM(s.shape, F32) for s in moved_out],
        compiler_params=_params(),
    )(packed, *moved_in)


def _scatter_finish(landed, name, own=()):
    n = len(landed)

    def body(*refs):
        if own:
            x, y, c = _position()
            my = _index((x, y, c))
        for t, (src, out) in enumerate(zip(refs[:n], refs[n + len(own) :])):
            g = None
            for j in range(N_DEV):
                part = src[j].astype(F32)
                if own:
                    part = jnp.where(j == my, refs[n + t][2 * x + y, c].astype(F32), part)
                g = part if g is None else g + part
            out[...] = g

    return pl.pallas_call(
        body,
        name=name,
        in_specs=[VMEM_SPEC] * (n + len(own)),
        out_specs=[VMEM_SPEC] * n,
        out_shape=[jax.ShapeDtypeStruct(a.shape[1:], F32) for a in landed],
        compiler_params=_params(),
    )(*landed, *own)


SEM_SPEC = pl.BlockSpec(memory_space=pltpu.SEMAPHORE)
SPLIT_COPY = pltpu.SideEffectType.DATAFLOW_SIDE_EFFECTING


def _scatter_start(blocks, name):
    land_shape = (N_DEV,) + blocks.shape[2:]

    def body(blocks_ref, land_ref, send_sems, recv_sems, blocks_thru, land_thru, token):
        me = _position()
        my = _index(me)
        for k in range(1, N_DEV):
            px, py, pc = to = _flip(me, k)
            _remote(blocks_ref.at[2 * px + py, pc], land_ref.at[my], send_sems.at[k - 1], recv_sems.at[k - 1], to).start()
        token[...] = jnp.zeros_like(token)

    return pl.pallas_call(
        body,
        name=name,
        in_specs=(HBM_SPEC, HBM_SPEC),
        out_specs=(SEM_SPEC, SEM_SPEC, HBM_SPEC, HBM_SPEC, VMEM_SPEC),
        out_shape=(
            pltpu.SemaphoreType.DMA((N_DEV - 1,)),
            pltpu.SemaphoreType.DMA((N_DEV - 1,)),
            pltpu.HBM(blocks.shape, blocks.dtype),
            pltpu.HBM(land_shape, blocks.dtype),
            jax.ShapeDtypeStruct((SUBLANES, LANES), F32),
        ),
        input_output_aliases={0: 2, 1: 3},
        compiler_params=pltpu.CompilerParams(has_side_effects=SPLIT_COPY),
    )(pltpu.with_memory_space_constraint(blocks, pltpu.HBM), pltpu.with_memory_space_constraint(lax.empty(land_shape, blocks.dtype), pltpu.HBM))


def _scatter_wait(send_sems, recv_sems, blocks_thru, land_thru, after, name):
    def body(blocks_ref, land_ref, send_sems, recv_sems, after_ref, blocks_dead, got_ref):
        me = _position()
        my = _index(me)
        for k in range(1, N_DEV):
            px, py, pc = to = _flip(me, k)
            _remote(blocks_ref.at[2 * px + py, pc], land_ref.at[my], send_sems.at[k - 1], recv_sems.at[k - 1], to).wait_send()
        for k in range(1, N_DEV):
            slot = land_ref.at[_index(_flip(me, k))]
            _remote(slot, slot, send_sems.at[k - 1], recv_sems.at[k - 1], _flip(me, k)).wait_recv()

    return pl.pallas_call(
        body,
        name=name,
        in_specs=(HBM_SPEC, HBM_SPEC, SEM_SPEC, SEM_SPEC, pl.BlockSpec(memory_space=pl.ANY)),
        out_specs=(HBM_SPEC, HBM_SPEC),
        out_shape=(pltpu.HBM(blocks_thru.shape, blocks_thru.dtype), pltpu.HBM(land_thru.shape, land_thru.dtype)),
        input_output_aliases={0: 0, 1: 1},
        compiler_params=pltpu.CompilerParams(has_side_effects=SPLIT_COPY),
    )(blocks_thru, land_thru, send_sems, recv_sems, after)


def _pack_rows(parts):
    rows, offsets, at = [], [], 0
    for p in parts:
        flat = p.reshape(-1)
        n = -(-flat.shape[0] // (SUBLANES * LANES)) * SUBLANES
        rows.append(jnp.pad(flat, (0, n * LANES - flat.shape[0])).reshape(n, LANES))
        offsets.append(at)
        at += n
    return jnp.concatenate(rows, axis=0), offsets


def _unpack_rows(packed, offsets, shapes):
    out = []
    for off, shape in zip(offsets, shapes):
        size = 1
        for s in shape:
            size *= s
        n = -(-size // (SUBLANES * LANES)) * SUBLANES
        out.append(packed[off : off + n].reshape(-1)[:size].reshape(shape))
    return out


def kernel(x, c, w_ada, b_ada, norm_gain, w_in, q_gain, k_gain, sink, w_s, b_s, w_out, loss_target, m_w_ada, m_b_ada, m_norm_gain, m_w_in, m_q_gain, m_k_gain, m_sink, m_w_s, m_b_s, m_w_out, v_w_ada, v_b_ada, v_norm_gain, v_w_in, v_q_gain, v_k_gain, v_sink, v_w_s, v_b_s, v_w_out):
    seq, d = x.shape[1], x.shape[2]
    n_layers = w_in.shape[0]
    w_cols = w_in.shape[2]
    ada_cols = w_ada.shape[2]
    my = _index(_position())
    xs = x.reshape(seq, d)
    target = loss_target.reshape(seq, d)

    rows_first = lambda a: a.transpose(0, 2, 1)
    w_in_t0, shard_in, shard_out, c_all, ada_parts = _gather_weights(rows_first(w_in), w_out, c, w_ada)
    w_in_ts, w_outs = [w_in_t0], []
    ada = ada_parts[:, :, 0, :].transpose(1, 0, 2).reshape(n_layers, 3 * d) + b_ada
    shift, scale1, gate = ada[:, None, 0:d], 1.0 + ada[:, None, d : 2 * d], ada[:, None, 2 * d : 3 * d]
    gain = norm_gain[:, None, :]

    w_s_m = w_s.astype(MXU_DTYPE)
    w_s_t = w_s_m.transpose(0, 1, 3, 2)
    b_st = jnp.repeat(b_s.transpose(0, 2, 1), HEAD_DIM, axis=2)
    q_gain2 = jnp.tile(q_gain, (1, 2))[:, None, :]
    k_gain2 = jnp.tile(k_gain, (1, 2))[:, None, :]

    xl, saved = xs, []
    for l in range(n_layers):
        last = l == n_layers - 1
        pa, pb = _ln_proj_fwd(xl, gain[l], scale1[l], shift[l], w_in_ts[l], f"ln_proj_fwd_{l}")
        wanted = ([shard_out[0]] if l == 0 else []) + ([] if last else [shard_out[l + 1], shard_in[l + 1]])
        o, probs, p_sink, *arrived = _attn_fwd(pa, q_gain2[l], k_gain2[l], sink[l], f"attn_fwd_{l}", gather=tuple(wanted))
        if not last:
            w_in_ts.append(arrived.pop())
        w_outs += arrived
        *out, sv = _mix_out_fwd(pb, o, xl, gate[l], w_outs[l], w_s_m[l], b_st[l], f"mix_out_fwd_{l}", target if last else None)
        saved.append((xl, pa, pb, o, probs, p_sink, sv))
        if last:
            dx, sq_err = out
        else:
            (xl,) = out

    g_w_in, g_w_out, small, d_ada_rows = [None] * n_layers, [None] * n_layers, [None] * n_layers, [None] * n_layers
    waiting = []
    d_ws_all = [None] * n_layers
    for l in reversed(range(n_layers)):
        x_l, pa, pb, o, probs, p_sink, sv = saved[l]
        dpb, do, dw_out, d_gate8, d_ws, d_bs = _mix_out_bwd(dx, pb, o, sv, gate[l], w_outs[l], w_s_t[l], f"mix_out_bwd_{l}")
        waiting.append((g_w_out, l, dw_out.reshape(4, 2, D_MIX // N_DEV, d)))
        riding, waiting = ([], waiting) if 0 < l == n_layers - 1 else (waiting, [])
        attn = _attn_bwd(
            pa, o, do, probs, p_sink, q_gain2[l], k_gain2[l], f"attn_bwd_{l}", scatter=tuple(b for _, _, b in riding)
        )
        dq, dkv, halo_prev, halo_next, d_qg, d_kg, d_sk = attn[:7]
        if riding:
            for (dest, layer, _), total in zip(riding, _scatter_finish(attn[7:], f"scatter_finish_{l}")):
                dest[layer] = total
        d_ws_all[l] = d_ws
        dw_args = (x_l, gain[l], scale1[l], shift[l], dq, dkv, halo_prev, halo_next, dpb, f"proj_bwd_dw_{l}")
        if l > 0:
            dw_in_t, dkvb = _proj_bwd_dw(*dw_args)
        else:
            d_ws_wire = jnp.stack(d_ws_all).reshape(-1, LANES).astype(jnp.bfloat16)
            dw_in_t, dkvb, gathered_ws = _proj_bwd_dw(*dw_args, gather=(d_ws_wire,))
        blocks_in = dw_in_t.reshape(4, 2, w_cols, d)
        if l > 0:
            waiting.append((g_w_in, l, blocks_in))
            dx, c0, c1 = _proj_bwd_dx(x_l, dx, dq, dkvb, dpb, w_in_ts[l], gain[l], scale1[l], f"proj_bwd_dx_{l}")
        else:
            *in_flight, token = _scatter_start(blocks_in, "scatter_start_in_0")
            dx, c0, c1 = _proj_bwd_dx(
                x_l, dx, dq, dkvb, dpb, w_in_ts[l], gain[l] + token[0, 0], scale1[l], f"proj_bwd_dx_{l}"
            )
            sent, landed = _scatter_wait(*in_flight, dx, "scatter_wait_in_0")
            g_w_in[l] = _scatter_finish((landed,), "scatter_finish_in_0", own=(sent,))[0]
        c0s, c1s = c0.sum(axis=0), c1.sum(axis=0)
        d_ada_rows[l] = jnp.concatenate([c0s, norm_gain[l] * c1s, d_gate8.sum(axis=0)])
        small[l] = (
            scale1[l, 0] * c1s,
            d_qg.sum(axis=0).reshape(N_HEADS, HEAD_DIM).sum(axis=0),
            d_kg.sum(axis=0).reshape(2, HEAD_DIM).sum(axis=0),
            d_sk.sum(axis=0)[0:N_HEADS],
            d_bs.reshape(BLK, N_GROUPS, HEAD_DIM).sum(axis=2).transpose(1, 0),
        )

    names = ("norm_gain", "q_gain", "k_gain", "sink", "b_s")
    stacked = [jnp.stack([small[l][t] for l in range(n_layers)]) for t in range(len(names))]
    d_ada = jnp.stack(d_ada_rows)
    packed, offsets = _pack_rows(stacked + [d_ada, sq_err[0, 0:1]])
    g_w_in_t, g_w_out = jnp.stack(g_w_in), jnp.stack(g_w_out)
    adam_in = (rows_first(w_in), g_w_in_t, rows_first(m_w_in), rows_first(v_w_in))
    gathered, *upd = _gather_small(packed, adam=(adam_in, (w_out, g_w_out, m_w_out, v_w_out)))
    gathered_ws = gathered_ws.reshape(N_DEV, -1, LANES)
    g_w_in = rows_first(g_w_in_t)
    upd_in, upd_out = [rows_first(u) for u in upd[0:3]], upd[3:6]
    no_weight = jnp.zeros((1,), F32)
    weights = (norm_gain, q_gain, k_gain, sink, b_s, b_ada, no_weight)
    moments_m = (m_norm_gain, m_q_gain, m_k_gain, m_sink, m_b_s, m_b_ada, no_weight)
    moments_v = (v_norm_gain, v_q_gain, v_k_gain, v_sink, v_b_s, v_b_ada, no_weight)
    w_pack, _ = _pack_rows(weights)
    m_pack, _ = _pack_rows(moments_m)
    v_pack, _ = _pack_rows(moments_v)
    shapes = [w.shape for w in weights]
    flat_ws = lambda a: a.reshape(-1, LANES)
    updated = _small_update(gathered, gathered_ws, w_pack, m_pack, v_pack, flat_ws(w_s), flat_ws(m_w_s), flat_ws(v_w_s))
    g_small, d_small, m_small, v_small = (_unpack_rows(p, offsets, shapes) for p in updated[0:4])
    ws_small = [p.reshape(w_s.shape) for p in updated[4:8]]
    loss = g_small[-1][0] * (0.5 / d)

    ada_off = offsets[-2]
    ada_n = -(-n_layers * 3 * d // (SUBLANES * LANES)) * SUBLANES
    d_ada_all = gathered[:, ada_off : ada_off + ada_n].reshape(N_DEV, -1)[:, : n_layers * 3 * d].reshape(N_DEV, n_layers, 3 * d)
    d_ada_cols = lax.dynamic_slice_in_dim(d_ada_all, my * ada_cols, ada_cols, axis=2)
    g_w_ada, *upd_ada = _ada_update(c_all[:, 0, :], d_ada_cols.transpose(1, 0, 2), w_ada, m_w_ada, v_w_ada)

    def ordered(ada_, in_, out_, small_, ws):
        ng, qg, kg, sk, bs, ba, _ = small_
        return (ada_, ba, ng, in_, qg, kg, sk, ws, bs, out_)

    grads = ordered(g_w_ada, g_w_in, g_w_out, g_small, ws_small[0])
    deltas = ordered(upd_ada[0], upd_in[0], upd_out[0], d_small, ws_small[1])
    new_m = ordered(upd_ada[1], upd_in[1], upd_out[1], m_small, ws_small[2])
    new_v = ordered(upd_ada[2], upd_in[2], upd_out[2], v_small, ws_small[3])
    return (loss, dx.reshape(x.shape), *grads, *deltas, *new_m, *new_v)
```

```python
import functools

import jax
import jax.numpy as jnp
from jax import lax
from jax.experimental import pallas as pl
from jax.experimental.pallas import tpu as pltpu

F32 = jnp.float32
MXU_DTYPE = jnp.bfloat16
MESH_ID = pl.DeviceIdType.MESH

N_DEV = 8
HEAD_DIM = 64
N_HEADS = 8
Q_PER_KV = 4
D_ATTN = 512
D_KV = 128
D_GM = 512
N_GROUPS = 8
D_MIX = D_ATTN + D_GM
BLK = 128
LANES = 128
SUBLANES = 8
N_PAIRS = D_ATTN // LANES
D_QKV = D_ATTN + 2 * D_KV
D_REST = D_ATTN + 3 * D_GM
D_IN = D_QKV + D_REST
EPS = 1e-6
NEG_INF = -1e30
ALIBI_SLOPES = tuple(2.0 ** (-8.0 * (h + 1) / N_HEADS) for h in range(N_HEADS))
Q_SCALE = 1.0 / 8.0

ADAM_LR = 0.001
ADAM_B1 = 0.9
ADAM_B2 = 0.999
ADAM_EPS = 1e-08
ADAM_WD = 0.01
ADAM_STEP = 10

TOKEN_TILE = 512
VMEM_LIMIT_BYTES = 56 * 1024 * 1024


def _params(semantics=None):
    return pltpu.CompilerParams(dimension_semantics=semantics, vmem_limit_bytes=VMEM_LIMIT_BYTES)


def _dot(a, b):
    return jnp.dot(a, b, preferred_element_type=F32)


def _dot_nt(a, b):
    return lax.dot_general(a, b, (((1,), (1,)), ((), ())), preferred_element_type=F32)


def _dot_tn(a, b):
    return lax.dot_general(a, b, (((0,), (0,)), ((), ())), preferred_element_type=F32)


def _mx(v):
    return v.astype(MXU_DTYPE)


def _lane_lo(rows):
    return lax.broadcasted_iota(jnp.int32, (rows, LANES), 1) < HEAD_DIM


def _half_ones(width=LANES):
    group_bits = HEAD_DIM.bit_length() - 1
    r = jnp.right_shift(lax.broadcasted_iota(jnp.int32, (width, width), 0), group_bits)
    c = jnp.right_shift(lax.broadcasted_iota(jnp.int32, (width, width), 1), group_bits)
    return jnp.where(r == c, 1.0, 0.0).astype(jnp.bfloat16)


WIDE = 2 * LANES


def _half_sum(v, ones):
    p1 = v.astype(jnp.bfloat16)
    p2 = (v - p1.astype(F32)).astype(jnp.bfloat16)
    return _dot(p1, ones) + _dot(p2, ones)


def _half_rms(v, ones):
    r = lax.rsqrt(_half_sum(v * v, ones) * (1.0 / HEAD_DIM) + EPS)
    return v * r, r


def _half_rms_bwd(dy, vhat, r, ones):
    return r * (dy - vhat * (_half_sum(vhat * dy, ones) * (1.0 / HEAD_DIM)))


def _group_rows(v):
    rows, n = v.shape
    return v.reshape(rows // SUBLANES, SUBLANES, n).sum(axis=0)


def _sigmoid(v):
    return 1.0 / (1.0 + jnp.exp(-v))


ROW_CHUNK = 32
VARIANT_HEADS = ((0, 2, 5, 7), (1, 3, 4, 6))
HEAD_SLOT = {h: (v, s) for v, heads in enumerate(VARIANT_HEADS) for s, h in enumerate(heads)}
STACK = Q_PER_KV * BLK


def _fill_attn_bias(bias_s):
    qi = lax.broadcasted_iota(jnp.int32, (BLK, 3 * BLK), 0)
    ci = lax.broadcasted_iota(jnp.int32, (BLK, 3 * BLK), 1)
    dist = jnp.abs(ci - BLK - qi)
    distf = dist.astype(F32)
    window = dist <= BLK
    for kind, seen in enumerate((window & (ci >= BLK), window, window & (ci < 2 * BLK))):
        for h in range(N_HEADS):
            bias_s[kind, h] = jnp.where(seen, -(ALIBI_SLOPES[h] * distf), NEG_INF)


def _block_kind(block, seq):
    assert seq >= 2 * BLK
    return jnp.where(block == 0, 0, jnp.where(block == seq // BLK - 1, 2, 1))


def _stage_queries(qn, lo_t, j, nb, qs):
    for a in range(2):
        v, slot = HEAD_SLOT[2 * j + a]
        qm = _mx(jnp.where(lo_t, qn, 0.0) if a == 0 else jnp.where(lo_t, 0.0, qn))
        for n in range(nb):
            qs[n, v, slot * BLK : (slot + 1) * BLK, :] = qm[n * BLK : (n + 1) * BLK]


def _unstack_pair(stacked, j, lo):
    (v0, s0), (v1, s1) = HEAD_SLOT[2 * j], HEAD_SLOT[2 * j + 1]
    return jnp.where(lo, stacked[v0][s0 * BLK : (s0 + 1) * BLK], stacked[v1][s1 * BLK : (s1 + 1) * BLK])


def _stage_keys(kvp_ref, qkv_ref, kvn_ref, kg, ones, tile, ks, kr, vs, vr, khat_s=None, rk_s=None):
    pieces = (
        (0, BLK, kvp_ref[:, 0:D_KV], kvp_ref[:, D_KV : 2 * D_KV]),
        (BLK, tile, qkv_ref[:, D_ATTN : D_ATTN + D_KV], qkv_ref[:, D_ATTN + D_KV : D_QKV]),
        (BLK + tile, BLK, kvn_ref[:, 0:D_KV], kvn_ref[:, D_KV : 2 * D_KV]),
    )
    for r0, n, k, v in pieces:
        khat, rk = _half_rms(k, ones)
        kn = khat * kg
        ks[r0 : r0 + n, :] = _mx(kn)
        kr[r0 : r0 + n, :] = _mx(pltpu.roll(kn, HEAD_DIM, 1))
        vs[r0 : r0 + n, :] = _mx(v)
        vr[r0 : r0 + n, :] = _mx(pltpu.roll(v, HEAD_DIM, 1))
        if khat_s is not None:
            khat_s[r0 : r0 + n, :] = khat
            rk_s[r0 : r0 + n, :] = rk


def _halo_specs(tile, seq):
    nb = tile // BLK
    last = seq // BLK - 1
    kv_col = D_ATTN // (2 * D_KV)
    prev = pl.BlockSpec((BLK, 2 * D_KV), lambda i: (jnp.maximum(i * nb - 1, 0), kv_col))
    nxt = pl.BlockSpec((BLK, 2 * D_KV), lambda i: (jnp.minimum((i + 1) * nb, last), kv_col))
    return prev, nxt


def _row_spec(tile, width):
    return pl.BlockSpec((tile, width), lambda i: (i, 0))


def _full_spec(shape):
    nd = len(shape)
    return pl.BlockSpec(shape, lambda i: (0,) * nd)


SMEM_SPEC = pl.BlockSpec(memory_space=pltpu.SMEM)
VMEM_SPEC = pl.BlockSpec(memory_space=pltpu.VMEM)
HBM_SPEC = pl.BlockSpec(memory_space=pltpu.HBM)


def _rider_steps(nt):
    return 0, (3 * nt) // 4, nt - 1


def _gather_rider(sources, gathered, sems, step, nt):
    start, forward, finish = _all_gather_stages(
        [_row_block(g, s.shape[0]) for g, s in zip(gathered, sources)], sems[0], sems[1], sources=sources, local_sems=sems[2]
    )
    at_start, at_forward, at_finish = _rider_steps(nt)
    pl.when(step == at_start)(start)

    def after_compute():
        pl.when(step == at_forward)(forward)
        pl.when(step == at_finish)(finish)

    return after_compute


def _gathered_shapes(gather):
    return [jax.ShapeDtypeStruct((N_DEV * g.shape[0], g.shape[1]), g.dtype) for g in gather]


def _ln_proj_fwd(x, gain, scale1, shift, w_in_t, name):
    seq, d = x.shape
    tile = min(TOKEN_TILE, seq)

    def body(x_ref, g_ref, s1_ref, sh_ref, wt_ref, pa_ref, pb_ref):
        xv = x_ref[...]
        r = lax.rsqrt(jnp.mean(xv * xv, axis=-1, keepdims=True) + EPS)
        h = _mx((xv * r) * g_ref[...] * s1_ref[...] + sh_ref[...])
        pa_ref[...] = _dot_nt(h, wt_ref[0:D_QKV, :])
        pb_ref[...] = _dot_nt(h, wt_ref[D_QKV:D_IN, :])

    vec = _full_spec((1, d))
    return pl.pallas_call(
        body,
        name=name,
        grid=(seq // tile,),
        in_specs=[_row_spec(tile, d), vec, vec, vec, _full_spec((D_IN, d))],
        out_specs=[_row_spec(tile, D_QKV), _row_spec(tile, D_REST)],
        out_shape=[jax.ShapeDtypeStruct((seq, D_QKV), F32), jax.ShapeDtypeStruct((seq, D_REST), F32)],
        compiler_params=_params(("parallel",)),
    )(x, gain, scale1, shift, w_in_t)


def _attn_fwd(pa, q_gain2, k_gain2, sink, name, gather=()):
    seq = pa.shape[0]
    tile = min(TOKEN_TILE, seq)
    nb = tile // BLK
    nt = seq // tile
    ext = tile + 2 * BLK
    n_ride = len(gather)
    riding = n_ride > 0

    def body(sink_ref, qkv_ref, kvp_ref, kvn_ref, qg_ref, kg_ref, *rest):
        i = pl.program_id(0)
        sources, (o_ref, p_ref, psink_ref), gathered = rest[:n_ride], rest[n_ride : n_ride + 3], rest[n_ride + 3 : 2 * n_ride + 3]
        qs, ks, kr, vs, vr, bias_s, s_scr, *sems = rest[2 * n_ride + 3 :]
        if riding:
            after_compute = _gather_rider(sources, gathered, sems, i, nt)

        @pl.when(i == 0)
        def _():
            _fill_attn_bias(bias_s)

        ones = _half_ones()
        lo = _lane_lo(BLK)
        lo_t = _lane_lo(tile)
        head_lane = lax.broadcasted_iota(jnp.int32, (ROW_CHUNK, LANES), 1)
        _stage_keys(kvp_ref, qkv_ref, kvn_ref, kg_ref[...], ones, tile, ks, kr, vs, vr)
        for j in range(N_PAIRS):
            qhat, _ = _half_rms(qkv_ref[:, j * LANES : (j + 1) * LANES], ones)
            _stage_queries(qhat * (qg_ref[...] * Q_SCALE), lo_t, j, nb, qs)

        def block(n, carry):
            r0 = pl.multiple_of(n * BLK, BLK)
            krows = pl.ds(r0, 3 * BLK)
            kind = _block_kind(i * nb + n, seq)
            for v in range(2):
                s_scr[v] = _dot_nt(qs[n, v], (kr if v else ks)[krows, :])
            for rc in range(0, BLK, ROW_CHUNK):
                p_sink = jnp.zeros((ROW_CHUNK, LANES), F32)
                for h in range(N_HEADS):
                    v, slot = HEAD_SLOT[h]
                    sink_h = sink_ref[h]
                    rows = slice(slot * BLK + rc, slot * BLK + rc + ROW_CHUNK)
                    s = s_scr[v, rows, :] + bias_s[kind, h, rc : rc + ROW_CHUNK, :]
                    m = jnp.maximum(jnp.max(s, axis=-1, keepdims=True), sink_h)
                    p = jnp.exp(s - m)
                    e_sink = jnp.exp(sink_h - m)
                    inv = 1.0 / (jnp.sum(p, axis=-1, keepdims=True) + e_sink)
                    p_ref[n, v, rows, :] = _mx(p * inv)
                    p_sink = jnp.where(head_lane == h, e_sink * inv, p_sink)
                psink_ref[pl.ds(pl.multiple_of(r0 + rc, ROW_CHUNK), ROW_CHUNK), :] = p_sink
            outs = [_dot(p_ref[n, v], (vr if v else vs)[krows, :]) for v in range(2)]
            for j in range(N_PAIRS):
                o_ref[pl.ds(r0, BLK), j * LANES : (j + 1) * LANES] = _unstack_pair(outs, j, lo)
            return carry

        lax.fori_loop(0, nb, block, 0)
        if riding:
            after_compute()

    prev, nxt = _halo_specs(tile, seq)
    vec = _full_spec((1, LANES))
    in_specs = [SMEM_SPEC, _row_spec(tile, D_QKV), prev, nxt, vec, vec]
    out_specs = [
        _row_spec(tile, D_ATTN),
        pl.BlockSpec((nb, 2, STACK, 3 * BLK), lambda i: (i, 0, 0, 0)),
        _row_spec(tile, LANES),
    ]
    out_shape = [
        jax.ShapeDtypeStruct((seq, D_ATTN), F32),
        jax.ShapeDtypeStruct((seq // BLK, 2, STACK, 3 * BLK), MXU_DTYPE),
        jax.ShapeDtypeStruct((seq, LANES), F32),
    ]
    scratch = [
        pltpu.VMEM((nb, 2, STACK, LANES), MXU_DTYPE),
        pltpu.VMEM((ext, LANES), MXU_DTYPE),
        pltpu.VMEM((ext, LANES), MXU_DTYPE),
        pltpu.VMEM((ext, LANES), MXU_DTYPE),
        pltpu.VMEM((ext, LANES), MXU_DTYPE),
        pltpu.VMEM((3, N_HEADS, BLK, 3 * BLK), F32),
        pltpu.VMEM((2, STACK, 3 * BLK), F32),
    ]
    return pl.pallas_call(
        body,
        name=name,
        grid=(nt,),
        in_specs=in_specs + [HBM_SPEC] * n_ride,
        out_specs=out_specs + [HBM_SPEC] * n_ride,
        out_shape=out_shape + _gathered_shapes(gather),
        scratch_shapes=scratch + _rider_sems(n_ride),
        compiler_params=_params(("arbitrary",)),
    )(sink, pa, pa, pa, q_gain2, k_gain2, *gather)


def _mix_out_fwd(pb, o, x, gate, w_out, w_s, b_st, name, target=None):
    seq, d = x.shape
    tile = min(TOKEN_TILE, seq)
    nb = tile // BLK
    with_loss = target is not None

    def body(pb_ref, o_ref, x_ref, gate_ref, wo_ref, ws_ref, bs_ref, *rest):
        if with_loss:
            t_ref, xo_ref, acc_ref, sv_ref, y_s, vn_s = rest

            @pl.when(pl.program_id(0) == 0)
            def _():
                acc_ref[...] = jnp.zeros_like(acc_ref)
        else:
            xo_ref, sv_ref, y_s, vn_s = rest
        ones = _half_ones(WIDE)
        lo = _lane_lo(BLK)
        ga = pb_ref[:, 0:D_ATTN]
        y_s[:, 0:D_ATTN] = _mx(o_ref[...] * (ga * _sigmoid(ga)))
        for j in range(D_GM // WIDE):
            vhat, _ = _half_rms(pb_ref[:, 2 * D_GM + j * WIDE : 2 * D_GM + (j + 1) * WIDE], ones)
            vn_s[:, j * WIDE : (j + 1) * WIDE] = _mx(vhat)

        def chunk(n, carry):
            rows = pl.ds(pl.multiple_of(n * BLK, BLK), BLK)
            for j in range(N_PAIRS):
                cols = slice(j * LANES, (j + 1) * LANES)
                vn = vn_s[rows, cols]
                sv = jnp.where(lo, _dot(ws_ref[2 * j], vn), _dot(ws_ref[2 * j + 1], vn)) + bs_ref[:, cols]
                sv_ref[rows, cols] = sv
                u = pb_ref[rows, D_ATTN + j * LANES : D_ATTN + (j + 1) * LANES]
                gg = pb_ref[rows, D_ATTN + 2 * D_GM + j * LANES : D_ATTN + 2 * D_GM + (j + 1) * LANES]
                y_s[rows, D_ATTN + j * LANES : D_ATTN + (j + 1) * LANES] = _mx((u * sv) * (gg * _sigmoid(gg)))
            return carry

        lax.fori_loop(0, nb, chunk, 0)
        y = x_ref[...] + gate_ref[...] * _dot(y_s[...], wo_ref[...])
        if with_loss:
            e = y - t_ref[...]
            xo_ref[...] = e * (1.0 / d)
            acc_ref[...] += jnp.sum(jnp.sum(e * e, axis=-1, keepdims=True), axis=0, keepdims=True)
        else:
            xo_ref[...] = y

    row = _row_spec(tile, d)
    acc_shape = (SUBLANES, LANES)
    return pl.pallas_call(
        body,
        name=name,
        grid=(seq // tile,),
        in_specs=[
            _row_spec(tile, D_REST),
            _row_spec(tile, D_ATTN),
            row,
            _full_spec((1, d)),
            _full_spec((D_MIX, d)),
            _full_spec((N_GROUPS, BLK, BLK)),
            _full_spec((BLK, D_GM)),
        ]
        + ([row] if with_loss else []),
        out_specs=[row] + ([_full_spec(acc_shape)] if with_loss else []) + [_row_spec(tile, D_GM)],
        out_shape=[jax.ShapeDtypeStruct((seq, d), F32)]
        + ([jax.ShapeDtypeStruct(acc_shape, F32)] if with_loss else [])
        + [jax.ShapeDtypeStruct((seq, D_GM), F32)],
        scratch_shapes=[pltpu.VMEM((tile, D_MIX), MXU_DTYPE), pltpu.VMEM((tile, D_GM), MXU_DTYPE)],
        compiler_params=_params(("arbitrary",) if with_loss else ("parallel",)),
    )(pb, o, x, gate, w_out, w_s, b_st, *([target] if with_loss else []))


def _mix_out_bwd(dxn, pb, o, sv, gate, w_out, w_s_t, name):
    seq, d = dxn.shape
    tile = min(TOKEN_TILE, seq)
    nb = tile // BLK
    nt = seq // tile

    def body(dxn_ref, pb_ref, o_ref, sv_ref, gate_ref, wo_ref, wst_ref,
             dpb_ref, do_ref, dwo_ref, dgate_ref, dws_ref, dbs_ref, g_ref, y_s, dy_s, vn_s, rv_s, vnb_s, dsv_s, dvn_s):
        @pl.when(pl.program_id(0) == 0)
        def _():
            g_ref[...] = jnp.zeros_like(g_ref)
            dws_ref[...] = jnp.zeros_like(dws_ref)
            dbs_ref[...] = jnp.zeros_like(dbs_ref)

        ones = _half_ones(WIDE)
        lo = _lane_lo(BLK)
        c_u = slice(D_ATTN, D_ATTN + D_GM)
        c_vg = slice(D_ATTN + D_GM, D_ATTN + 2 * D_GM)
        c_gg = slice(D_ATTN + 2 * D_GM, D_REST)
        dxv = dxn_ref[...]
        dy_s[...] = _dot_nt(_mx(dxv * gate_ref[...]), wo_ref[...])
        ga = pb_ref[:, 0:D_ATTN]
        sig = _sigmoid(ga)
        sil = ga * sig
        ov = o_ref[...]
        y_s[:, 0:D_ATTN] = _mx(ov * sil)
        da = dy_s[:, 0:D_ATTN]
        do_ref[...] = da * sil
        dpb_ref[:, 0:D_ATTN] = (da * ov * (sig * (1.0 + ga * (1.0 - sig)))).astype(dpb_ref.dtype)
        for j in range(D_GM // WIDE):
            cols = slice(j * WIDE, (j + 1) * WIDE)
            vhat, rv = _half_rms(pb_ref[:, 2 * D_GM + j * WIDE : 2 * D_GM + (j + 1) * WIDE], ones)
            vn_s[:, cols] = vhat
            rv_s[:, cols] = rv
            vnb_s[:, cols] = _mx(vhat)

        def gating(n, carry):
            rows = pl.ds(pl.multiple_of(n * BLK, BLK), BLK)
            sv = sv_ref[rows, :]
            u = pb_ref[rows, c_u]
            gg = pb_ref[rows, c_gg]
            sg = _sigmoid(gg)
            silg = gg * sg
            m0 = u * sv
            y_s[rows, D_ATTN:D_MIX] = _mx(m0 * silg)
            dm = dy_s[rows, D_ATTN:D_MIX]
            dm0 = dm * silg
            dpb_ref[rows, c_gg] = (dm * m0 * (sg * (1.0 + gg * (1.0 - sg)))).astype(dpb_ref.dtype)
            dpb_ref[rows, c_u] = (dm0 * sv).astype(dpb_ref.dtype)
            dsv = dm0 * u
            dsv_s[rows, :] = _mx(dsv)
            dbs_ref[...] += dsv
            return carry

        lax.fori_loop(0, nb, gating, 0)

        def spatial_bwd(n, carry):
            rows = pl.ds(pl.multiple_of(n * BLK, BLK), BLK)
            for j in range(N_PAIRS):
                cols = slice(j * LANES, (j + 1) * LANES)
                dsv = dsv_s[rows, cols]
                dvn_s[rows, cols] = jnp.where(lo, _dot(wst_ref[2 * j], dsv), _dot(wst_ref[2 * j + 1], dsv))
            return carry

        lax.fori_loop(0, nb, spatial_bwd, 0)
        zero = jnp.zeros((BLK, LANES), MXU_DTYPE)
        for j in range(N_PAIRS):
            cols = slice(j * LANES, (j + 1) * LANES)
            chunks = [dsv_s[n * BLK : (n + 1) * BLK, cols] for n in range(nb)]
            vn_all = jnp.concatenate([vnb_s[n * BLK : (n + 1) * BLK, cols] for n in range(nb)], axis=1)
            dws_ref[2 * j] += _dot_nt(jnp.concatenate([jnp.where(lo, c, zero) for c in chunks], axis=1), vn_all)
            dws_ref[2 * j + 1] += _dot_nt(jnp.concatenate([jnp.where(lo, zero, c) for c in chunks], axis=1), vn_all)
        for j in range(D_GM // WIDE):
            cols = slice(j * WIDE, (j + 1) * WIDE)
            dpb_ref[:, D_ATTN + D_GM + j * WIDE : D_ATTN + D_GM + (j + 1) * WIDE] = _half_rms_bwd(
                dvn_s[:, cols], vn_s[:, cols], rv_s[:, cols], ones
            ).astype(dpb_ref.dtype)
        g_ref[...] += _dot_tn(y_s[...], _mx(dxv))

        @pl.when(pl.program_id(0) == nt - 1)
        def _():
            gv = g_ref[...]
            dwo_ref[...] = (gv * gate_ref[...]).astype(dwo_ref.dtype)
            dgate_ref[...] = _group_rows(gv * wo_ref[...].astype(F32))

    return pl.pallas_call(
        body,
        name=name,
        grid=(seq // tile,),
        in_specs=[
            _row_spec(tile, d),
            _row_spec(tile, D_REST),
            _row_spec(tile, D_ATTN),
            _row_spec(tile, D_GM),
            _full_spec((1, d)),
            _full_spec((D_MIX, d)),
            _full_spec((N_GROUPS, BLK, BLK)),
        ],
        out_specs=[
            _row_spec(tile, D_REST),
            _row_spec(tile, D_ATTN),
            _full_spec((D_MIX, d)),
            _full_spec((SUBLANES, d)),
            _full_spec((N_GROUPS, BLK, BLK)),
            _full_spec((BLK, D_GM)),
        ],
        out_shape=[
            jax.ShapeDtypeStruct((seq, D_REST), MXU_DTYPE),
            jax.ShapeDtypeStruct((seq, D_ATTN), F32),
            jax.ShapeDtypeStruct((D_MIX, d), jnp.bfloat16),
            jax.ShapeDtypeStruct((SUBLANES, d), F32),
            jax.ShapeDtypeStruct((N_GROUPS, BLK, BLK), F32),
            jax.ShapeDtypeStruct((BLK, D_GM), F32),
        ],
        scratch_shapes=[
            pltpu.VMEM((D_MIX, d), F32),
            pltpu.VMEM((tile, D_MIX), MXU_DTYPE),
            pltpu.VMEM((tile, D_MIX), F32),
            pltpu.VMEM((tile, D_GM), F32),
            pltpu.VMEM((tile, D_GM), F32),
            pltpu.VMEM((tile, D_GM), MXU_DTYPE),
            pltpu.VMEM((tile, D_GM), MXU_DTYPE),
            pltpu.VMEM((tile, D_GM), F32),
        ],
        compiler_params=_params(("arbitrary",)),
    )(dxn, pb, o, sv, gate, w_out, w_s_t)


def _attn_bwd(pa, o, do, probs, p_sink, q_gain2, k_gain2, name, scatter=()):
    seq = pa.shape[0]
    tile = min(TOKEN_TILE, seq)
    nb = tile // BLK
    nt = seq // tile
    ext = tile + 2 * BLK
    n_ride = len(scatter)
    riding = n_ride > 0

    def body(qkv_ref, kvp_ref, kvn_ref, o_ref, do_ref, p_ref, psink_ref, qg_ref, kg_ref, *rest):
        i = pl.program_id(0)
        blocks, rest = rest[:n_ride], rest[n_ride:]
        dq_ref, dkv_ref, hp_ref, hn_ref, dqg_ref, dkg_ref, dsk_ref = rest[:7]
        landing, rest = rest[7 : 7 + n_ride], rest[7 + n_ride :]
        (qs, dos, qhat_s, rq_s, ks, kr, vs, vr, khat_s, rk_s, dqn_s, dka, dva, dp_scr, ds_scr) = rest[:15]
        if riding:
            start, finish = _scatter_stages(blocks, landing, *rest[15:])
            at_start, _, at_finish = _rider_steps(nt)
            pl.when(i == at_start)(start)

        @pl.when(i == 0)
        def _():
            dqg_ref[...] = jnp.zeros_like(dqg_ref)
            dkg_ref[...] = jnp.zeros_like(dkg_ref)
            dsk_ref[...] = jnp.zeros_like(dsk_ref)

        ones = _half_ones()
        lo = _lane_lo(BLK)
        lo_t = _lane_lo(tile)
        lo_c = _lane_lo(ROW_CHUNK)
        qg = qg_ref[...] * Q_SCALE
        kg = kg_ref[...]
        _stage_keys(kvp_ref, qkv_ref, kvn_ref, kg, ones, tile, ks, kr, vs, vr, khat_s, rk_s)
        head_lane = lax.broadcasted_iota(jnp.int32, (tile, LANES), 1)
        d_rows = jnp.zeros((tile, LANES), F32)
        for j in range(N_PAIRS):
            cols = slice(j * LANES, (j + 1) * LANES)
            qhat, rq = _half_rms(qkv_ref[:, cols], ones)
            qhat_s[:, cols] = qhat
            rq_s[:, cols] = rq
            _stage_queries(qhat * qg, lo_t, j, nb, qs)
            dov = do_ref[:, cols]
            _stage_queries(dov, lo_t, j, nb, dos)
            d_pair = _half_sum(dov * o_ref[:, cols], ones)
            d_rows = jnp.where(head_lane == 2 * j, d_pair, d_rows)
            d_rows = jnp.where(head_lane == 2 * j + 1, pltpu.roll(d_pair, HEAD_DIM, 1), d_rows)
        dsk_ref[...] -= _group_rows(psink_ref[...] * d_rows)
        dka[...] = jnp.zeros_like(dka)
        dva[...] = jnp.zeros_like(dva)

        def block(n, carry):
            r0 = pl.multiple_of(n * BLK, BLK)
            krows = pl.ds(r0, 3 * BLK)
            for v in range(2):
                dp_scr[v] = _dot_nt(dos[n, v], (vr if v else vs)[krows, :])
            for h in range(N_HEADS):
                v, slot = HEAD_SLOT[h]
                j, a = divmod(h, 2)
                cols = slice(j * LANES, (j + 1) * LANES)
                for rc in range(0, BLK, ROW_CHUNK):
                    rows = slice(slot * BLK + rc, slot * BLK + rc + ROW_CHUNK)
                    trows = pl.ds(pl.multiple_of(r0 + rc, ROW_CHUNK), ROW_CHUNK)
                    prod = do_ref[trows, cols] * o_ref[trows, cols]
                    prod = jnp.where(lo_c, prod, 0.0) if a == 0 else jnp.where(lo_c, 0.0, prod)
                    dcol = jnp.sum(prod, axis=-1, keepdims=True)
                    ds_scr[v, rows, :] = _mx(p_ref[n, v, rows, :].astype(F32) * (dp_scr[v, rows, :] - dcol))
            dqv = []
            for v in range(2):
                dqv.append(_dot(ds_scr[v], (kr if v else ks)[krows, :]))
                dka[v, krows, :] += _dot_tn(ds_scr[v], qs[n, v])
                dva[v, krows, :] += _dot_tn(p_ref[n, v], dos[n, v])
            for j in range(N_PAIRS):
                dqn_s[pl.ds(r0, BLK), j * LANES : (j + 1) * LANES] = _unstack_pair(dqv, j, lo)
            return carry

        lax.fori_loop(0, nb, block, 0)
        for j in range(N_PAIRS):
            cols = slice(j * LANES, (j + 1) * LANES)
            dqn = dqn_s[:, cols]
            qhat = qhat_s[:, cols]
            dqg_ref[:, cols] += _group_rows(dqn * qhat) * Q_SCALE
            dq_ref[:, cols] = _half_rms_bwd(dqn * qg, qhat, rq_s[:, cols], ones).astype(dq_ref.dtype)
        dkn = dka[0] + pltpu.roll(dka[1], HEAD_DIM, 1)
        khat = khat_s[...]
        dkg_ref[...] += _group_rows(dkn * khat)
        dk = _half_rms_bwd(dkn * kg, khat, rk_s[...], ones)
        dv = dva[0] + pltpu.roll(dva[1], HEAD_DIM, 1)
        hp_ref[:, 0:D_KV] = dk[0:BLK]
        hp_ref[:, D_KV : 2 * D_KV] = dv[0:BLK]
        dkv_ref[:, 0:D_KV] = dk[BLK : BLK + tile]
        dkv_ref[:, D_KV : 2 * D_KV] = dv[BLK : BLK + tile]
        hn_ref[:, 0:D_KV] = dk[BLK + tile : ext]
        hn_ref[:, D_KV : 2 * D_KV] = dv[BLK + tile : ext]
        if riding:
            pl.when(i == at_finish)(finish)

    prev, nxt = _halo_specs(tile, seq)
    vec = _full_spec((1, LANES))
    halo = pl.BlockSpec((None, BLK, 2 * D_KV), lambda i: (i, 0, 0))
    return pl.pallas_call(
        body,
        name=name,
        grid=(nt,),
        in_specs=[
            _row_spec(tile, D_QKV),
            prev,
            nxt,
            _row_spec(tile, D_ATTN),
            _row_spec(tile, D_ATTN),
            pl.BlockSpec((nb, 2, STACK, 3 * BLK), lambda i: (i, 0, 0, 0)),
            _row_spec(tile, LANES),
            vec,
            vec,
        ]
        + [HBM_SPEC] * n_ride,
        out_specs=[
            _row_spec(tile, D_ATTN),
            _row_spec(tile, 2 * D_KV),
            halo,
            halo,
            _full_spec((SUBLANES, D_ATTN)),
            _full_spec((SUBLANES, LANES)),
            _full_spec((SUBLANES, LANES)),
        ]
        + [HBM_SPEC] * n_ride,
        out_shape=[
            jax.ShapeDtypeStruct((seq, D_ATTN), MXU_DTYPE),
            jax.ShapeDtypeStruct((seq, 2 * D_KV), F32),
            jax.ShapeDtypeStruct((nt, BLK, 2 * D_KV), F32),
            jax.ShapeDtypeStruct((nt, BLK, 2 * D_KV), F32),
            jax.ShapeDtypeStruct((SUBLANES, D_ATTN), F32),
            jax.ShapeDtypeStruct((SUBLANES, LANES), F32),
            jax.ShapeDtypeStruct((SUBLANES, LANES), F32),
        ]
        + _landing_shapes(scatter),
        scratch_shapes=[
            pltpu.VMEM((nb, 2, STACK, LANES), MXU_DTYPE),
            pltpu.VMEM((nb, 2, STACK, LANES), MXU_DTYPE),
            pltpu.VMEM((tile, D_ATTN), F32),
            pltpu.VMEM((tile, D_ATTN), F32),
            pltpu.VMEM((ext, LANES), MXU_DTYPE),
            pltpu.VMEM((ext, LANES), MXU_DTYPE),
            pltpu.VMEM((ext, LANES), MXU_DTYPE),
            pltpu.VMEM((ext, LANES), MXU_DTYPE),
            pltpu.VMEM((ext, LANES), F32),
            pltpu.VMEM((ext, LANES), F32),
            pltpu.VMEM((tile, D_ATTN), F32),
            pltpu.VMEM((2, ext, LANES), F32),
            pltpu.VMEM((2, ext, LANES), F32),
            pltpu.VMEM((2, STACK, 3 * BLK), F32),
            pltpu.VMEM((2, STACK, 3 * BLK), MXU_DTYPE),
        ]
        + _rider_sems(n_ride),
        compiler_params=_params(("arbitrary",)),
    )(pa, pa, pa, o, do, probs, p_sink, q_gain2, k_gain2, *scatter)


def _halo_in_specs(tile, nt):
    from_prev = pl.BlockSpec((None, BLK, 2 * D_KV), lambda i: (jnp.maximum(i - 1, 0), 0, 0))
    from_next = pl.BlockSpec((None, BLK, 2 * D_KV), lambda i: (jnp.minimum(i + 1, nt - 1), 0, 0))
    return from_prev, from_next


def _landing_shapes(scatter):
    return [jax.ShapeDtypeStruct((N_DEV,) + b.shape[2:], b.dtype) for b in scatter]


def _rider_sems(n_ride):
    if not n_ride:
        return []
    return [pltpu.SemaphoreType.DMA((7 * n_ride,)), pltpu.SemaphoreType.DMA((7 * n_ride,)), pltpu.SemaphoreType.DMA((n_ride,))]


def _proj_bwd_dx(x, dxn, dq, dkvb, dpb, w_in_t, gain, scale1, name):
    seq, d = x.shape
    tile = min(TOKEN_TILE, seq)

    def row(width):
        return _row_spec(tile, width)

    def body(x_ref, dxn_ref, dq_ref, dkvb_ref, dpb_ref, wt_ref, g_ref, s1_ref, dx_ref, c0_ref, c1_ref):
        @pl.when(pl.program_id(0) == 0)
        def _():
            c0_ref[...] = jnp.zeros_like(c0_ref)
            c1_ref[...] = jnp.zeros_like(c1_ref)

        dh = (
            _dot(dq_ref[...], wt_ref[0:D_ATTN, :])
            + _dot(dkvb_ref[...], wt_ref[D_ATTN:D_QKV, :])
            + _dot(dpb_ref[...], wt_ref[D_QKV:D_IN, :])
        )
        xv = x_ref[...]
        r = lax.rsqrt(jnp.mean(xv * xv, axis=-1, keepdims=True) + EPS)
        xn = xv * r
        c0_ref[...] += _group_rows(dh)
        c1_ref[...] += _group_rows(dh * xn)
        dxn_ = dh * (g_ref[...] * s1_ref[...])
        dx_ref[...] = dxn_ref[...] + r * (dxn_ - xn * jnp.mean(xn * dxn_, axis=-1, keepdims=True))

    vec = _full_spec((1, d))
    return pl.pallas_call(
        body,
        name=name,
        grid=(seq // tile,),
        in_specs=[row(d), row(d), row(D_ATTN), row(2 * D_KV), row(D_REST), _full_spec((D_IN, d)), vec, vec],
        out_specs=[row(d), _full_spec((SUBLANES, d)), _full_spec((SUBLANES, d))],
        out_shape=[
            jax.ShapeDtypeStruct((seq, d), F32),
            jax.ShapeDtypeStruct((SUBLANES, d), F32),
            jax.ShapeDtypeStruct((SUBLANES, d), F32),
        ],
        compiler_params=_params(("arbitrary",)),
    )(x, dxn, dq, dkvb, dpb, w_in_t, gain, scale1)


def _proj_bwd_dw(x, gain, scale1, shift, dq, dkv, halo_prev, halo_next, dpb, name, gather=()):
    seq, d = x.shape
    tile = min(TOKEN_TILE, seq)
    nt = seq // tile
    assert tile >= 2 * BLK
    n_ride = len(gather)

    def body(x_ref, g_ref, s1_ref, sh_ref, dq_ref, dkv_ref, hn_ref, hp_ref, dpb_ref, *rest):
        i = pl.program_id(0)
        sources, rest = rest[:n_ride], rest[n_ride:]
        dw_ref, dkvb_ref = rest[:2]
        gathered, (acc, *sems) = rest[2 : 2 + n_ride], rest[2 + n_ride :]
        after_compute = _gather_rider(sources, gathered, sems, i, nt) if n_ride else None

        @pl.when(i == 0)
        def _():
            acc[...] = jnp.zeros_like(acc)

        top = dkv_ref[0:BLK, :] + jnp.where(i > 0, hn_ref[...], 0.0)
        bot = dkv_ref[tile - BLK : tile, :] + jnp.where(i < nt - 1, hp_ref[...], 0.0)
        dkvb_ref[0:BLK, :] = top.astype(dkvb_ref.dtype)
        dkvb_ref[tile - BLK : tile, :] = bot.astype(dkvb_ref.dtype)
        if tile > 2 * BLK:
            dkvb_ref[BLK : tile - BLK, :] = dkv_ref[BLK : tile - BLK, :].astype(dkvb_ref.dtype)
        xv = x_ref[...]
        r = lax.rsqrt(jnp.mean(xv * xv, axis=-1, keepdims=True) + EPS)
        h = _mx((xv * r) * g_ref[...] * s1_ref[...] + sh_ref[...])
        acc[0:D_ATTN, :] += _dot_tn(dq_ref[...], h)
        acc[D_ATTN:D_QKV, :] += _dot_tn(dkvb_ref[...], h)
        acc[D_QKV:D_IN, :] += _dot_tn(dpb_ref[...], h)

        @pl.when(i == nt - 1)
        def _():
            dw_ref[...] = acc[...].astype(dw_ref.dtype)

        if n_ride:
            after_compute()

    from_prev, from_next = _halo_in_specs(tile, nt)
    vec = _full_spec((1, d))
    return pl.pallas_call(
        body,
        name=name,
        grid=(nt,),
        in_specs=[
            _row_spec(tile, d),
            vec,
            vec,
            vec,
            _row_spec(tile, D_ATTN),
            _row_spec(tile, 2 * D_KV),
            from_prev,
            from_next,
            _row_spec(tile, D_REST),
        ]
        + [HBM_SPEC] * n_ride,
        out_specs=[_full_spec((D_IN, d)), _row_spec(tile, 2 * D_KV)] + [HBM_SPEC] * n_ride,
        out_shape=[jax.ShapeDtypeStruct((D_IN, d), jnp.bfloat16), jax.ShapeDtypeStruct((seq, 2 * D_KV), MXU_DTYPE)]
        + _gathered_shapes(gather),
        scratch_shapes=[pltpu.VMEM((D_IN, d), F32)] + _rider_sems(n_ride),
        compiler_params=_params(("arbitrary",)),
    )(x, gain, scale1, shift, dq, dkv, halo_next, halo_prev, dpb, *gather)


def _adamw_math(w, g, m, v):
    m = ADAM_B1 * m + (1.0 - ADAM_B1) * g
    v = ADAM_B2 * v + (1.0 - ADAM_B2) * (g * g)
    m_hat = m * (1.0 / (1.0 - ADAM_B1**ADAM_STEP))
    v_hat = v * (1.0 / (1.0 - ADAM_B2**ADAM_STEP))
    delta = -ADAM_LR * (m_hat / (jnp.sqrt(v_hat) + ADAM_EPS) + ADAM_WD * w)
    return delta, m, v


def _small_update(gathered, gathered_ws, w, m, v, ws, m_ws, v_ws):
    def body(ga_ref, gws_ref, w_ref, m_ref, v_ref, ws_ref, mws_ref, vws_ref, *outs):
        for src, refs, out in ((ga_ref, (w_ref, m_ref, v_ref), outs[0:4]), (gws_ref, (ws_ref, mws_ref, vws_ref), outs[4:8])):
            g = src[0].astype(F32)
            for j in range(1, N_DEV):
                g = g + src[j].astype(F32)
            out[0][...] = g
            out[1][...], out[2][...], out[3][...] = _adamw_math(refs[0][...], g, refs[1][...], refs[2][...])

    shapes = [jax.ShapeDtypeStruct(w.shape, F32)] * 4 + [jax.ShapeDtypeStruct(ws.shape, F32)] * 4
    return pl.pallas_call(
        body,
        name="small_update",
        in_specs=[VMEM_SPEC] * 8,
        out_specs=[VMEM_SPEC] * 8,
        out_shape=shapes,
        compiler_params=_params(),
    )(gathered, gathered_ws, w, m, v, ws, m_ws, v_ws)


def _ada_update(c_all, d_ada_cols, w, m, v):
    n_layers = w.shape[0]

    def body(c_ref, da_ref, w_ref, m_ref, v_ref, g_ref, d_ref, mo_ref, vo_ref):
        cv = c_ref[...]
        cond = cv * _sigmoid(cv)
        for l in range(n_layers):
            g = lax.dot_general(
                cond, da_ref[l], (((0,), (0,)), ((), ())), preferred_element_type=F32, precision=lax.Precision.HIGHEST
            )
            g_ref[l] = g
            d_ref[l], mo_ref[l], vo_ref[l] = _adamw_math(w_ref[l], g, m_ref[l], v_ref[l])

    return pl.pallas_call(
        body,
        name="ada_update",
        in_specs=[VMEM_SPEC] * 5,
        out_specs=[VMEM_SPEC] * 4,
        out_shape=[jax.ShapeDtypeStruct(w.shape, F32)] * 4,
        compiler_params=_params(),
    )(c_all, d_ada_cols, w, m, v)


def _position():
    return lax.axis_index("x"), lax.axis_index("y"), lax.axis_index("c")


def _flip(pos, k):
    x, y, c = pos
    return (1 - x if k & 4 else x, 1 - y if k & 2 else y, 1 - c if k & 1 else c)


def _index(pos):
    x, y, c = pos
    return 4 * x + 2 * y + c


def _remote(src, dst, send_sem, recv_sem, to):
    return pltpu.make_async_remote_copy(
        src_ref=src, dst_ref=dst, send_sem=send_sem, recv_sem=recv_sem, device_id=to, device_id_type=MESH_ID
    )


def _all_gather_stages(slots, send_sems, recv_sems, sources=None, local_sems=None):
    me = _position()
    sibling = _flip(me, 1)
    others = (4, 2, 6)
    arrays = range(len(slots))

    def copy(t, k, block, to, own=False):
        slot = slots[t](_index(block))
        src = sources[t] if own and sources is not None else slot
        return _remote(src, slot, send_sems.at[7 * t + k], recv_sems.at[7 * t + k], to)

    def first(t):
        return [copy(t, 0, me, sibling, own=True)] + [copy(t, 1 + j, me, _flip(me, f), own=True) for j, f in enumerate(others)]

    def passed(t, j):
        return copy(t, 4 + j, _flip(me, others[j]), sibling)

    def local(t):
        return pltpu.make_async_copy(sources[t], slots[t](_index(me)), local_sems.at[t])

    def start():
        for t in arrays:
            if sources is not None:
                local(t).start()
            for cp in first(t):
                cp.start()

    def forward():
        for j, f in enumerate(others):
            for t in arrays:
                copy(t, 1 + j, _flip(me, f), me).wait_recv()
                passed(t, j).start()

    def finish():
        for t in arrays:
            copy(t, 0, sibling, me).wait_recv()
            for j, f in enumerate(others):
                copy(t, 4 + j, _flip(sibling, f), me).wait_recv()
        for t in arrays:
            for cp in first(t) + [passed(t, j) for j in range(len(others))]:
                cp.wait_send()
            if sources is not None:
                local(t).wait()

    return start, forward, finish


def _two_level_all_gather(slots, send_sems, recv_sems, between=None):
    start, forward, finish = _all_gather_stages(slots, send_sems, recv_sems)
    start()
    if between is not None:
        between()
    forward()
    finish()


def _row_block(ref, rows):
    return lambda j: ref.at[pl.ds(pl.multiple_of(j * rows, 16), rows), :]


def _scatter_stages(blocks, landing, send_sems, recv_sems, local_sems):
    me = _position()
    my = _index(me)
    arrays = range(len(blocks))

    def copy(t, k):
        px, py, pc = to = _flip(me, k)
        return _remote(blocks[t].at[2 * px + py, pc], landing[t].at[my], send_sems.at[7 * t + k - 1], recv_sems.at[7 * t + k - 1], to)

    def arrival(t, k):
        slot = landing[t].at[_index(_flip(me, k))]
        return _remote(slot, slot, send_sems.at[7 * t + k - 1], recv_sems.at[7 * t + k - 1], _flip(me, k))

    def local(t):
        x, y, c = me
        return pltpu.make_async_copy(blocks[t].at[2 * x + y, c], landing[t].at[my], local_sems.at[t])

    def start():
        for t in arrays:
            local(t).start()
            for k in range(1, N_DEV):
                copy(t, k).start()

    def finish():
        for t in arrays:
            for k in range(1, N_DEV):
                arrival(t, k).wait_recv()
        for t in arrays:
            for k in range(1, N_DEV):
                copy(t, k).wait_send()
            local(t).wait()

    return start, finish


def _ada_exchange(c_ref, w_ref, call_ref, parts_ref, sbuf, sem_s1, sem_r1, sem_s2, sem_r2):
    d = c_ref.shape[-1]
    n_layers = w_ref.shape[0]
    me = _position()
    my = _index(me)
    call_ref[my] = jnp.broadcast_to(c_ref[...], (SUBLANES, d))
    mine = call_ref.at[my]
    first = [_remote(mine, mine, sem_s1.at[k - 1], sem_r1.at[k - 1], _flip(me, k)) for k in range(1, N_DEV)]
    for cp in first:
        cp.start()
    for k in range(1, N_DEV):
        theirs = call_ref.at[_index(_flip(me, k))]
        _remote(theirs, theirs, sem_s1.at[k - 1], sem_r1.at[k - 1], _flip(me, k)).wait_recv()
    cv = call_ref[...].reshape(N_DEV * SUBLANES, d)
    cond = cv * _sigmoid(cv)
    for l in range(n_layers):
        rows = jnp.dot(cond, w_ref[l], preferred_element_type=F32, precision=lax.Precision.HIGHEST)
        for b in range(N_DEV):
            sbuf[b, l] = rows[b * SUBLANES : (b + 1) * SUBLANES]
    parts_ref[my] = sbuf[my]
    second = []
    for k in range(1, N_DEV):
        to = _flip(me, k)
        second.append(_remote(sbuf.at[_index(to)], parts_ref.at[my], sem_s2.at[k - 1], sem_r2.at[k - 1], to))
    for cp in second:
        cp.start()
    for k in range(1, N_DEV):
        theirs = parts_ref.at[_index(_flip(me, k))]
        _remote(theirs, theirs, sem_s2.at[k - 1], sem_r2.at[k - 1], _flip(me, k)).wait_recv()
    for cp in first + second:
        cp.wait_send()


def _gather_weights(w_in_t, w_out, c_row, w_ada):
    n_layers, rows_in, d = w_in_t.shape
    width = w_ada.shape[2]

    def body(wi_ref, wo_ref, c_ref, wa_ref, gi_ref, si_ref, so_ref, call_ref, parts_ref, sbuf, send_sems, recv_sems, *ada_sems):
        my = _index(_position())
        si_ref[...] = wi_ref[...].astype(si_ref.dtype)
        so_ref[...] = wo_ref[...].astype(so_ref.dtype)
        gi_ref[pl.ds(pl.multiple_of(my * rows_in, 16), rows_in), :] = si_ref[0]
        _two_level_all_gather(
            (_row_block(gi_ref, rows_in),),
            send_sems,
            recv_sems,
            between=functools.partial(_ada_exchange, c_ref, wa_ref, call_ref, parts_ref, sbuf, *ada_sems),
        )

    return pl.pallas_call(
        body,
        name="gather_weights",
        in_specs=[VMEM_SPEC] * 4,
        out_specs=[VMEM_SPEC] * 5,
        out_shape=[
            jax.ShapeDtypeStruct((N_DEV * rows_in, d), MXU_DTYPE),
            jax.ShapeDtypeStruct(w_in_t.shape, MXU_DTYPE),
            jax.ShapeDtypeStruct(w_out.shape, MXU_DTYPE),
            jax.ShapeDtypeStruct((N_DEV, SUBLANES, d), F32),
            jax.ShapeDtypeStruct((N_DEV, n_layers, SUBLANES, width), F32),
        ],
        scratch_shapes=[
            pltpu.VMEM((N_DEV, n_layers, SUBLANES, width), F32),
            pltpu.SemaphoreType.DMA((7,)),
            pltpu.SemaphoreType.DMA((7,)),
        ]
        + [pltpu.SemaphoreType.DMA((N_DEV - 1,))] * 4,
        compiler_params=_params(),
    )(w_in_t, w_out, c_row, w_ada)


def _gather_small(packed, adam=()):
    n_adam = len(adam)
    n_in, n_out = 4 * n_adam, 3 * n_adam

    def body(p_ref, *rest):
        operands, rest = rest[:n_in], rest[n_in:]
        g_ref, results, rest = rest[0], rest[1 : 1 + n_out], rest[1 + n_out :]
        send_sems, recv_sems, load_sems, store_sems = rest[:4]
        loaded, stored = rest[4 : 4 + n_in], rest[4 + n_in :]
        loads = [pltpu.make_async_copy(operands[k], loaded[k], load_sems.at[k]) for k in range(n_in)]
        stores = [pltpu.make_async_copy(stored[k], results[k], store_sems.at[k]) for k in range(n_out)]
        for load in loads:
            load.start()
        g_ref[_index(_position())] = p_ref[...]

        def updates():
            for t in range(n_adam):
                for load in loads[4 * t : 4 * t + 4]:
                    load.wait()
                w_ref, gr_ref, m_ref, v_ref = loaded[4 * t : 4 * t + 4]
                new = _adamw_math(w_ref[...], gr_ref[...], m_ref[...], v_ref[...])
                for k, value in zip(range(3 * t, 3 * t + 3), new):
                    stored[k][...] = value
                    stores[k].start()

        _two_level_all_gather((lambda j: g_ref.at[j],), send_sems, recv_sems, between=updates)
        for store in stores:
            store.wait()

    moved_in = [a for q in adam for a in q]
    moved_out = [jax.ShapeDtypeStruct(q[0].shape, F32) for q in adam for _ in range(3)]
    return pl.pallas_call(
        body,
        name="gather_small",
        in_specs=[VMEM_SPEC] + [HBM_SPEC] * n_in,
        out_specs=[VMEM_SPEC] + [HBM_SPEC] * n_out,
        out_shape=[jax.ShapeDtypeStruct((N_DEV,) + packed.shape, F32)] + moved_out,
        scratch_shapes=[pltpu.SemaphoreType.DMA((7,)), pltpu.SemaphoreType.DMA((7,))]
        + [pltpu.SemaphoreType.DMA((max(n_in, 1),)), pltpu.SemaphoreType.DMA((max(n_out, 1),))]
        + [pltpu.VMEM(a.shape, F32) for a in moved_in]
        + [pltpu.VMEM(s.shape, F32) for s in moved_out],
        compiler_params=_params(),
    )(packed, *moved_in)


def _scatter_finish(landed, name, own=()):
    n = len(landed)

    def body(*refs):
        if own:
            x, y, c = _position()
            my = _index((x, y, c))
        for t, (src, out) in enumerate(zip(refs[:n], refs[n + len(own) :])):
            g = None
            for j in range(N_DEV):
                part = src[j].astype(F32)
                if own:
                    part = jnp.where(j == my, refs[n + t][2 * x + y, c].astype(F32), part)
                g = part if g is None else g + part
            out[...] = g

    return pl.pallas_call(
        body,
        name=name,
        in_specs=[VMEM_SPEC] * (n + len(own)),
        out_specs=[VMEM_SPEC] * n,
        out_shape=[jax.ShapeDtypeStruct(a.shape[1:], F32) for a in landed],
        compiler_params=_params(),
    )(*landed, *own)


SEM_SPEC = pl.BlockSpec(memory_space=pltpu.SEMAPHORE)
SPLIT_COPY = pltpu.SideEffectType.DATAFLOW_SIDE_EFFECTING


def _scatter_start(blocks, name):
    land_shape = (N_DEV,) + blocks.shape[2:]

    def body(blocks_ref, land_ref, send_sems, recv_sems, blocks_thru, land_thru, token):
        me = _position()
        my = _index(me)
        for k in range(1, N_DEV):
            px, py, pc = to = _flip(me, k)
            _remote(blocks_ref.at[2 * px + py, pc], land_ref.at[my], send_sems.at[k - 1], recv_sems.at[k - 1], to).start()
        token[...] = jnp.zeros_like(token)

    return pl.pallas_call(
        body,
        name=name,
        in_specs=(HBM_SPEC, HBM_SPEC),
        out_specs=(SEM_SPEC, SEM_SPEC, HBM_SPEC, HBM_SPEC, VMEM_SPEC),
        out_shape=(
            pltpu.SemaphoreType.DMA((N_DEV - 1,)),
            pltpu.SemaphoreType.DMA((N_DEV - 1,)),
            pltpu.HBM(blocks.shape, blocks.dtype),
            pltpu.HBM(land_shape, blocks.dtype),
            jax.ShapeDtypeStruct((SUBLANES, LANES), F32),
        ),
        input_output_aliases={0: 2, 1: 3},
        compiler_params=pltpu.CompilerParams(has_side_effects=SPLIT_COPY),
    )(pltpu.with_memory_space_constraint(blocks, pltpu.HBM), pltpu.with_memory_space_constraint(lax.empty(land_shape, blocks.dtype), pltpu.HBM))


def _scatter_wait(send_sems, recv_sems, blocks_thru, land_thru, after, name):
    def body(blocks_ref, land_ref, send_sems, recv_sems, after_ref, blocks_dead, got_ref):
        me = _position()
        my = _index(me)
        for k in range(1, N_DEV):
            px, py, pc = to = _flip(me, k)
            _remote(blocks_ref.at[2 * px + py, pc], land_ref.at[my], send_sems.at[k - 1], recv_sems.at[k - 1], to).wait_send()
        for k in range(1, N_DEV):
            slot = land_ref.at[_index(_flip(me, k))]
            _remote(slot, slot, send_sems.at[k - 1], recv_sems.at[k - 1], _flip(me, k)).wait_recv()

    return pl.pallas_call(
        body,
        name=name,
        in_specs=(HBM_SPEC, HBM_SPEC, SEM_SPEC, SEM_SPEC, pl.BlockSpec(memory_space=pl.ANY)),
        out_specs=(HBM_SPEC, HBM_SPEC),
        out_shape=(pltpu.HBM(blocks_thru.shape, blocks_thru.dtype), pltpu.HBM(land_thru.shape, land_thru.dtype)),
        input_output_aliases={0: 0, 1: 1},
        compiler_params=pltpu.CompilerParams(has_side_effects=SPLIT_COPY),
    )(blocks_thru, land_thru, send_sems, recv_sems, after)


def _pack_rows(parts):
    rows, offsets, at = [], [], 0
    for p in parts:
        flat = p.reshape(-1)
        n = -(-flat.shape[0] // (SUBLANES * LANES)) * SUBLANES
        rows.append(jnp.pad(flat, (0, n * LANES - flat.shape[0])).reshape(n, LANES))
        offsets.append(at)
        at += n
    return jnp.concatenate(rows, axis=0), offsets


def _unpack_rows(packed, offsets, shapes):
    out = []
    for off, shape in zip(offsets, shapes):
        size = 1
        for s in shape:
            size *= s
        n = -(-size // (SUBLANES * LANES)) * SUBLANES
        out.append(packed[off : off + n].reshape(-1)[:size].reshape(shape))
    return out


def kernel(x, c, w_ada, b_ada, norm_gain, w_in, q_gain, k_gain, sink, w_s, b_s, w_out, loss_target, m_w_ada, m_b_ada, m_norm_gain, m_w_in, m_q_gain, m_k_gain, m_sink, m_w_s, m_b_s, m_w_out, v_w_ada, v_b_ada, v_norm_gain, v_w_in, v_q_gain, v_k_gain, v_sink, v_w_s, v_b_s, v_w_out):
    seq, d = x.shape[1], x.shape[2]
    n_layers = w_in.shape[0]
    w_cols = w_in.shape[2]
    ada_cols = w_ada.shape[2]
    my = _index(_position())
    xs = x.reshape(seq, d)
    target = loss_target.reshape(seq, d)

    rows_first = lambda a: a.transpose(0, 2, 1)
    w_in_t0, shard_in, shard_out, c_all, ada_parts = _gather_weights(rows_first(w_in), w_out, c, w_ada)
    w_in_ts, w_outs = [w_in_t0], []
    ada = ada_parts[:, :, 0, :].transpose(1, 0, 2).reshape(n_layers, 3 * d) + b_ada
    shift, scale1, gate = ada[:, None, 0:d], 1.0 + ada[:, None, d : 2 * d], ada[:, None, 2 * d : 3 * d]
    gain = norm_gain[:, None, :]

    w_s_m = w_s.astype(MXU_DTYPE)
    w_s_t = w_s_m.transpose(0, 1, 3, 2)
    b_st = jnp.repeat(b_s.transpose(0, 2, 1), HEAD_DIM, axis=2)
    q_gain2 = jnp.tile(q_gain, (1, 2))[:, None, :]
    k_gain2 = jnp.tile(k_gain, (1, 2))[:, None, :]

    xl, saved = xs, []
    for l in range(n_layers):
        last = l == n_layers - 1
        pa, pb = _ln_proj_fwd(xl, gain[l], scale1[l], shift[l], w_in_ts[l], f"ln_proj_fwd_{l}")
        wanted = ([shard_out[0]] if l == 0 else []) + ([] if last else [shard_out[l + 1], shard_in[l + 1]])
        o, probs, p_sink, *arrived = _attn_fwd(pa, q_gain2[l], k_gain2[l], sink[l], f"attn_fwd_{l}", gather=tuple(wanted))
        if not last:
            w_in_ts.append(arrived.pop())
        w_outs += arrived
        *out, sv = _mix_out_fwd(pb, o, xl, gate[l], w_outs[l], w_s_m[l], b_st[l], f"mix_out_fwd_{l}", target if last else None)
        saved.append((xl, pa, pb, o, probs, p_sink, sv))
        if last:
            dx, sq_err = out
        else:
            (xl,) = out

    g_w_in, g_w_out, small, d_ada_rows = [None] * n_layers, [None] * n_layers, [None] * n_layers, [None] * n_layers
    waiting = []
    d_ws_all = [None] * n_layers
    for l in reversed(range(n_layers)):
        x_l, pa, pb, o, probs, p_sink, sv = saved[l]
        dpb, do, dw_out, d_gate8, d_ws, d_bs = _mix_out_bwd(dx, pb, o, sv, gate[l], w_outs[l], w_s_t[l], f"mix_out_bwd_{l}")
        waiting.append((g_w_out, l, dw_out.reshape(4, 2, D_MIX // N_DEV, d)))
        riding, waiting = ([], waiting) if 0 < l == n_layers - 1 else (waiting, [])
        attn = _attn_bwd(
            pa, o, do, probs, p_sink, q_gain2[l], k_gain2[l], f"attn_bwd_{l}", scatter=tuple(b for _, _, b in riding)
        )
        dq, dkv, halo_prev, halo_next, d_qg, d_kg, d_sk = attn[:7]
        if riding:
            for (dest, layer, _), total in zip(riding, _scatter_finish(attn[7:], f"scatter_finish_{l}")):
                dest[layer] = total
        d_ws_all[l] = d_ws
        dw_args = (x_l, gain[l], scale1[l], shift[l], dq, dkv, halo_prev, halo_next, dpb, f"proj_bwd_dw_{l}")
        if l > 0:
            dw_in_t, dkvb = _proj_bwd_dw(*dw_args)
        else:
            d_ws_wire = jnp.stack(d_ws_all).reshape(-1, LANES).astype(jnp.bfloat16)
            dw_in_t, dkvb, gathered_ws = _proj_bwd_dw(*dw_args, gather=(d_ws_wire,))
        blocks_in = dw_in_t.reshape(4, 2, w_cols, d)
        if l > 0:
            waiting.append((g_w_in, l, blocks_in))
            dx, c0, c1 = _proj_bwd_dx(x_l, dx, dq, dkvb, dpb, w_in_ts[l], gain[l], scale1[l], f"proj_bwd_dx_{l}")
        else:
            *in_flight, token = _scatter_start(blocks_in, "scatter_start_in_0")
            dx, c0, c1 = _proj_bwd_dx(
                x_l, dx, dq, dkvb, dpb, w_in_ts[l], gain[l] + token[0, 0], scale1[l], f"proj_bwd_dx_{l}"
            )
            sent, landed = _scatter_wait(*in_flight, dx, "scatter_wait_in_0")
            g_w_in[l] = _scatter_finish((landed,), "scatter_finish_in_0", own=(sent,))[0]
        c0s, c1s = c0.sum(axis=0), c1.sum(axis=0)
        d_ada_rows[l] = jnp.concatenate([c0s, norm_gain[l] * c1s, d_gate8.sum(axis=0)])
        small[l] = (
            scale1[l, 0] * c1s,
            d_qg.sum(axis=0).reshape(N_HEADS, HEAD_DIM).sum(axis=0),
            d_kg.sum(axis=0).reshape(2, HEAD_DIM).sum(axis=0),
            d_sk.sum(axis=0)[0:N_HEADS],
            d_bs.reshape(BLK, N_GROUPS, HEAD_DIM).sum(axis=2).transpose(1, 0),
        )

    names = ("norm_gain", "q_gain", "k_gain", "sink", "b_s")
    stacked = [jnp.stack([small[l][t] for l in range(n_layers)]) for t in range(len(names))]
    d_ada = jnp.stack(d_ada_rows)
    packed, offsets = _pack_rows(stacked + [d_ada, sq_err[0, 0:1]])
    g_w_in_t, g_w_out = jnp.stack(g_w_in), jnp.stack(g_w_out)
    adam_in = (rows_first(w_in), g_w_in_t, rows_first(m_w_in), rows_first(v_w_in))
    gathered, *upd = _gather_small(packed, adam=(adam_in, (w_out, g_w_out, m_w_out, v_w_out)))
    gathered_ws = gathered_ws.reshape(N_DEV, -1, LANES)
    g_w_in = rows_first(g_w_in_t)
    upd_in, upd_out = [rows_first(u) for u in upd[0:3]], upd[3:6]
    no_weight = jnp.zeros((1,), F32)
    weights = (norm_gain, q_gain, k_gain, sink, b_s, b_ada, no_weight)
    moments_m = (m_norm_gain, m_q_gain, m_k_gain, m_sink, m_b_s, m_b_ada, no_weight)
    moments_v = (v_norm_gain, v_q_gain, v_k_gain, v_sink, v_b_s, v_b_ada, no_weight)
    w_pack, _ = _pack_rows(weights)
    m_pack, _ = _pack_rows(moments_m)
    v_pack, _ = _pack_rows(moments_v)
    shapes = [w.shape for w in weights]
    flat_ws = lambda a: a.reshape(-1, LANES)
    updated = _small_update(gathered, gathered_ws, w_pack, m_pack, v_pack, flat_ws(w_s), flat_ws(m_w_s), flat_ws(v_w_s))
    g_small, d_small, m_small, v_small = (_unpack_rows(p, offsets, shapes) for p in updated[0:4])
    ws_small = [p.reshape(w_s.shape) for p in updated[4:8]]
    loss = g_small[-1][0] * (0.5 / d)

    ada_off = offsets[-2]
    ada_n = -(-n_layers * 3 * d // (SUBLANES * LANES)) * SUBLANES
    d_ada_all = gathered[:, ada_off : ada_off + ada_n].reshape(N_DEV, -1)[:, : n_layers * 3 * d].reshape(N_DEV, n_layers, 3 * d)
    d_ada_cols = lax.dynamic_slice_in_dim(d_ada_all, my * ada_cols, ada_cols, axis=2)
    g_w_ada, *upd_ada = _ada_update(c_all[:, 0, :], d_ada_cols.transpose(1, 0, 2), w_ada, m_w_ada, v_w_ada)

    def ordered(ada_, in_, out_, small_, ws):
        ng, qg, kg, sk, bs, ba, _ = small_
        return (ada_, ba, ng, in_, qg, kg, sk, ws, bs, out_)

    grads = ordered(g_w_ada, g_w_in, g_w_out, g_small, ws_small[0])
    deltas = ordered(upd_ada[0], upd_in[0], upd_out[0], d_small, ws_small[1])
    new_m = ordered(upd_ada[1], upd_in[1], upd_out[1], m_small, ws_small[2])
    new_v = ordered(upd_ada[2], upd_in[2], upd_out[2], v_small, ws_small[3])
    return (loss, dx.reshape(x.shape), *grads, *deltas, *new_m, *new_v)
```

```python
import functools

import jax
import jax.numpy as jnp
from jax import lax
from jax.experimental import pallas as pl
from jax.experimental.pallas import tpu as pltpu

F32 = jnp.float32
MXU_DTYPE = jnp.bfloat16
MESH_ID = pl.DeviceIdType.MESH

N_DEV = 8
HEAD_DIM = 64
N_HEADS = 8
Q_PER_KV = 4
D_ATTN = 512
D_KV = 128
D_GM = 512
N_GROUPS = 8
D_MIX = D_ATTN + D_GM
BLK = 128
LANES = 128
SUBLANES = 8
N_PAIRS = D_ATTN // LANES
D_QKV = D_ATTN + 2 * D_KV
D_REST = D_ATTN + 3 * D_GM
D_IN = D_QKV + D_REST
EPS = 1e-6
NEG_INF = -1e30
ALIBI_SLOPES = tuple(2.0 ** (-8.0 * (h + 1) / N_HEADS) for h in range(N_HEADS))
Q_SCALE = 1.0 / 8.0

ADAM_LR = 0.001
ADAM_B1 = 0.9
ADAM_B2 = 0.999
ADAM_EPS = 1e-08
ADAM_WD = 0.01
ADAM_STEP = 10

TOKEN_TILE = 512
VMEM_LIMIT_BYTES = 56 * 1024 * 1024


def _params(semantics=None):
    return pltpu.CompilerParams(dimension_semantics=semantics, vmem_limit_bytes=VMEM_LIMIT_BYTES)


def _dot(a, b):
    return jnp.dot(a, b, preferred_element_type=F32)


def _dot_nt(a, b):
    return lax.dot_general(a, b, (((1,), (1,)), ((), ())), preferred_element_type=F32)


def _dot_tn(a, b):
    return lax.dot_general(a, b, (((0,), (0,)), ((), ())), preferred_element_type=F32)


def _mx(v):
    return v.astype(MXU_DTYPE)


def _lane_lo(rows):
    return lax.broadcasted_iota(jnp.int32, (rows, LANES), 1) < HEAD_DIM


def _half_ones(width=LANES):
    group_bits = HEAD_DIM.bit_length() - 1
    r = jnp.right_shift(lax.broadcasted_iota(jnp.int32, (width, width), 0), group_bits)
    c = jnp.right_shift(lax.broadcasted_iota(jnp.int32, (width, width), 1), group_bits)
    return jnp.where(r == c, 1.0, 0.0).astype(jnp.bfloat16)


WIDE = 2 * LANES


def _half_sum(v, ones):
    p1 = v.astype(jnp.bfloat16)
    p2 = (v - p1.astype(F32)).astype(jnp.bfloat16)
    return _dot(p1, ones) + _dot(p2, ones)


def _half_rms(v, ones):
    r = lax.rsqrt(_half_sum(v * v, ones) * (1.0 / HEAD_DIM) + EPS)
    return v * r, r


def _half_rms_bwd(dy, vhat, r, ones):
    return r * (dy - vhat * (_half_sum(vhat * dy, ones) * (1.0 / HEAD_DIM)))


def _group_rows(v):
    rows, n = v.shape
    return v.reshape(rows // SUBLANES, SUBLANES, n).sum(axis=0)


def _sigmoid(v):
    return 1.0 / (1.0 + jnp.exp(-v))


ROW_CHUNK = 32
VARIANT_HEADS = ((0, 2, 5, 7), (1, 3, 4, 6))
HEAD_SLOT = {h: (v, s) for v, heads in enumerate(VARIANT_HEADS) for s, h in enumerate(heads)}
STACK = Q_PER_KV * BLK


def _fill_attn_bias(bias_s):
    qi = lax.broadcasted_iota(jnp.int32, (BLK, 3 * BLK), 0)
    ci = lax.broadcasted_iota(jnp.int32, (BLK, 3 * BLK), 1)
    dist = jnp.abs(ci - BLK - qi)
    distf = dist.astype(F32)
    window = dist <= BLK
    for kind, seen in enumerate((window & (ci >= BLK), window, window & (ci < 2 * BLK))):
        for h in range(N_HEADS):
            bias_s[kind, h] = jnp.where(seen, -(ALIBI_SLOPES[h] * distf), NEG_INF)


def _block_kind(block, seq):
    assert seq >= 2 * BLK
    return jnp.where(block == 0, 0, jnp.where(block == seq // BLK - 1, 2, 1))


def _stage_queries(qn, lo_t, j, nb, qs):
    for a in range(2):
        v, slot = HEAD_SLOT[2 * j + a]
        qm = _mx(jnp.where(lo_t, qn, 0.0) if a == 0 else jnp.where(lo_t, 0.0, qn))
        for n in range(nb):
            qs[n, v, slot * BLK : (slot + 1) * BLK, :] = qm[n * BLK : (n + 1) * BLK]


def _unstack_pair(stacked, j, lo):
    (v0, s0), (v1, s1) = HEAD_SLOT[2 * j], HEAD_SLOT[2 * j + 1]
    return jnp.where(lo, stacked[v0][s0 * BLK : (s0 + 1) * BLK], stacked[v1][s1 * BLK : (s1 + 1) * BLK])


def _stage_keys(kvp_ref, qkv_ref, kvn_ref, kg, ones, tile, ks, kr, vs, vr, khat_s=None, rk_s=None):
    pieces = (
        (0, BLK, kvp_ref[:, 0:D_KV], kvp_ref[:, D_KV : 2 * D_KV]),
        (BLK, tile, qkv_ref[:, D_ATTN : D_ATTN + D_KV], qkv_ref[:, D_ATTN + D_KV : D_QKV]),
        (BLK + tile, BLK, kvn_ref[:, 0:D_KV], kvn_ref[:, D_KV : 2 * D_KV]),
    )
    for r0, n, k, v in pieces:
        khat, rk = _half_rms(k, ones)
        kn = khat * kg
        ks[r0 : r0 + n, :] = _mx(kn)
        kr[r0 : r0 + n, :] = _mx(pltpu.roll(kn, HEAD_DIM, 1))
        vs[r0 : r0 + n, :] = _mx(v)
        vr[r0 : r0 + n, :] = _mx(pltpu.roll(v, HEAD_DIM, 1))
        if khat_s is not None:
            khat_s[r0 : r0 + n, :] = khat
            rk_s[r0 : r0 + n, :] = rk


def _halo_specs(tile, seq):
    nb = tile // BLK
    last = seq // BLK - 1
    kv_col = D_ATTN // (2 * D_KV)
    prev = pl.BlockSpec((BLK, 2 * D_KV), lambda i: (jnp.maximum(i * nb - 1, 0), kv_col))
    nxt = pl.BlockSpec((BLK, 2 * D_KV), lambda i: (jnp.minimum((i + 1) * nb, last), kv_col))
    return prev, nxt


def _row_spec(tile, width):
    return pl.BlockSpec((tile, width), lambda i: (i, 0))


def _full_spec(shape):
    nd = len(shape)
    return pl.BlockSpec(shape, lambda i: (0,) * nd)


SMEM_SPEC = pl.BlockSpec(memory_space=pltpu.SMEM)
VMEM_SPEC = pl.BlockSpec(memory_space=pltpu.VMEM)
HBM_SPEC = pl.BlockSpec(memory_space=pltpu.HBM)


def _rider_steps(nt):
    return 0, (3 * nt) // 4, nt - 1


def _gather_rider(sources, gathered, sems, step, nt):
    start, forward, finish = _all_gather_stages(
        [_row_block(g, s.shape[0]) for g, s in zip(gathered, sources)], sems[0], sems[1], sources=sources, local_sems=sems[2]
    )
    at_start, at_forward, at_finish = _rider_steps(nt)
    pl.when(step == at_start)(start)

    def after_compute():
        pl.when(step == at_forward)(forward)
        pl.when(step == at_finish)(finish)

    return after_compute


def _gathered_shapes(gather):
    return [jax.ShapeDtypeStruct((N_DEV * g.shape[0], g.shape[1]), g.dtype) for g in gather]


def _ln_proj_fwd(x, gain, scale1, shift, w_in_t, name):
    seq, d = x.shape
    tile = min(TOKEN_TILE, seq)

    def body(x_ref, g_ref, s1_ref, sh_ref, wt_ref, pa_ref, pb_ref):
        xv = x_ref[...]
        r = lax.rsqrt(jnp.mean(xv * xv, axis=-1, keepdims=True) + EPS)
        h = _mx((xv * r) * g_ref[...] * s1_ref[...] + sh_ref[...])
        pa_ref[...] = _dot_nt(h, wt_ref[0:D_QKV, :])
        pb_ref[...] = _dot_nt(h, wt_ref[D_QKV:D_IN, :])

    vec = _full_spec((1, d))
    return pl.pallas_call(
        body,
        name=name,
        grid=(seq // tile,),
        in_specs=[_row_spec(tile, d), vec, vec, vec, _full_spec((D_IN, d))],
        out_specs=[_row_spec(tile, D_QKV), _row_spec(tile, D_REST)],
        out_shape=[jax.ShapeDtypeStruct((seq, D_QKV), F32), jax.ShapeDtypeStruct((seq, D_REST), F32)],
        compiler_params=_params(("parallel",)),
    )(x, gain, scale1, shift, w_in_t)


def _attn_fwd(pa, q_gain2, k_gain2, sink, name, gather=()):
    seq = pa.shape[0]
    tile = min(TOKEN_TILE, seq)
    nb = tile // BLK
    nt = seq // tile
    ext = tile + 2 * BLK
    n_ride = len(gather)
    riding = n_ride > 0

    def body(sink_ref, qkv_ref, kvp_ref, kvn_ref, qg_ref, kg_ref, *rest):
        i = pl.program_id(0)
        sources, (o_ref, p_ref, psink_ref), gathered = rest[:n_ride], rest[n_ride : n_ride + 3], rest[n_ride + 3 : 2 * n_ride + 3]
        qs, ks, kr, vs, vr, bias_s, s_scr, *sems = rest[2 * n_ride + 3 :]
        if riding:
            after_compute = _gather_rider(sources, gathered, sems, i, nt)

        @pl.when(i == 0)
        def _():
            _fill_attn_bias(bias_s)

        ones = _half_ones()
        lo = _lane_lo(BLK)
        lo_t = _lane_lo(tile)
        head_lane = lax.broadcasted_iota(jnp.int32, (ROW_CHUNK, LANES), 1)
        _stage_keys(kvp_ref, qkv_ref, kvn_ref, kg_ref[...], ones, tile, ks, kr, vs, vr)
        for j in range(N_PAIRS):
            qhat, _ = _half_rms(qkv_ref[:, j * LANES : (j + 1) * LANES], ones)
            _stage_queries(qhat * (qg_ref[...] * Q_SCALE), lo_t, j, nb, qs)

        def block(n, carry):
            r0 = pl.multiple_of(n * BLK, BLK)
            krows = pl.ds(r0, 3 * BLK)
            kind = _block_kind(i * nb + n, seq)
            for v in range(2):
                s_scr[v] = _dot_nt(qs[n, v], (kr if v else ks)[krows, :])
            for rc in range(0, BLK, ROW_CHUNK):
                p_sink = jnp.zeros((ROW_CHUNK, LANES), F32)
                for h in range(N_HEADS):
                    v, slot = HEAD_SLOT[h]
                    sink_h = sink_ref[h]
                    rows = slice(slot * BLK + rc, slot * BLK + rc + ROW_CHUNK)
                    s = s_scr[v, rows, :] + bias_s[kind, h, rc : rc + ROW_CHUNK, :]
                    m = jnp.maximum(jnp.max(s, axis=-1, keepdims=True), sink_h)
                    p = jnp.exp(s - m)
                    e_sink = jnp.exp(sink_h - m)
                    inv = 1.0 / (jnp.sum(p, axis=-1, keepdims=True) + e_sink)
                    p_ref[n, v, rows, :] = _mx(p * inv)
                    p_sink = jnp.where(head_lane == h, e_sink * inv, p_sink)
                psink_ref[pl.ds(pl.multiple_of(r0 + rc, ROW_CHUNK), ROW_CHUNK), :] = p_sink
            outs = [_dot(p_ref[n, v], (vr if v else vs)[krows, :]) for v in range(2)]
            for j in range(N_PAIRS):
                o_ref[pl.ds(r0, BLK), j * LANES : (j + 1) * LANES] = _unstack_pair(outs, j, lo)
            return carry

        lax.fori_loop(0, nb, block, 0)
        if riding:
            after_compute()

    prev, nxt = _halo_specs(tile, seq)
    vec = _full_spec((1, LANES))
    in_specs = [SMEM_SPEC, _row_spec(tile, D_QKV), prev, nxt, vec, vec]
    out_specs = [
        _row_spec(tile, D_ATTN),
        pl.BlockSpec((nb, 2, STACK, 3 * BLK), lambda i: (i, 0, 0, 0)),
        _row_spec(tile, LANES),
    ]
    out_shape = [
        jax.ShapeDtypeStruct((seq, D_ATTN), F32),
        jax.ShapeDtypeStruct((seq // BLK, 2, STACK, 3 * BLK), MXU_DTYPE),
        jax.ShapeDtypeStruct((seq, LANES), F32),
    ]
    scratch = [
        pltpu.VMEM((nb, 2, STACK, LANES), MXU_DTYPE),
        pltpu.VMEM((ext, LANES), MXU_DTYPE),
        pltpu.VMEM((ext, LANES), MXU_DTYPE),
        pltpu.VMEM((ext, LANES), MXU_DTYPE),
        pltpu.VMEM((ext, LANES), MXU_DTYPE),
        pltpu.VMEM((3, N_HEADS, BLK, 3 * BLK), F32),
        pltpu.VMEM((2, STACK, 3 * BLK), F32),
    ]
    return pl.pallas_call(
        body,
        name=name,
        grid=(nt,),
        in_specs=in_specs + [HBM_SPEC] * n_ride,
        out_specs=out_specs + [HBM_SPEC] * n_ride,
        out_shape=out_shape + _gathered_shapes(gather),
        scratch_shapes=scratch + _rider_sems(n_ride),
        compiler_params=_params(("arbitrary",)),
    )(sink, pa, pa, pa, q_gain2, k_gain2, *gather)


def _mix_out_fwd(pb, o, x, gate, w_out, w_s, b_st, name, target=None):
    seq, d = x.shape
    tile = min(TOKEN_TILE, seq)
    nb = tile // BLK
    with_loss = target is not None

    def body(pb_ref, o_ref, x_ref, gate_ref, wo_ref, ws_ref, bs_ref, *rest):
        if with_loss:
            t_ref, xo_ref, acc_ref, sv_ref, y_s, vn_s = rest

            @pl.when(pl.program_id(0) == 0)
            def _():
                acc_ref[...] = jnp.zeros_like(acc_ref)
        else:
            xo_ref, sv_ref, y_s, vn_s = rest
        ones = _half_ones(WIDE)
        lo = _lane_lo(BLK)
        ga = pb_ref[:, 0:D_ATTN]
        y_s[:, 0:D_ATTN] = _mx(o_ref[...] * (ga * _sigmoid(ga)))
        for j in range(D_GM // WIDE):
            vhat, _ = _half_rms(pb_ref[:, 2 * D_GM + j * WIDE : 2 * D_GM + (j + 1) * WIDE], ones)
            vn_s[:, j * WIDE : (j + 1) * WIDE] = _mx(vhat)

        def chunk(n, carry):
            rows = pl.ds(pl.multiple_of(n * BLK, BLK), BLK)
            for j in range(N_PAIRS):
                cols = slice(j * LANES, (j + 1) * LANES)
                vn = vn_s[rows, cols]
                sv = jnp.where(lo, _dot(ws_ref[2 * j], vn), _dot(ws_ref[2 * j + 1], vn)) + bs_ref[:, cols]
                sv_ref[rows, cols] = sv
                u = pb_ref[rows, D_ATTN + j * LANES : D_ATTN + (j + 1) * LANES]
                gg = pb_ref[rows, D_ATTN + 2 * D_GM + j * LANES : D_ATTN + 2 * D_GM + (j + 1) * LANES]
                y_s[rows, D_ATTN + j * LANES : D_ATTN + (j + 1) * LANES] = _mx((u * sv) * (gg * _sigmoid(gg)))
            return carry

        lax.fori_loop(0, nb, chunk, 0)
        y = x_ref[...] + gate_ref[...] * _dot(y_s[...], wo_ref[...])
        if with_loss:
            e = y - t_ref[...]
            xo_ref[...] = e * (1.0 / d)
            acc_ref[...] += jnp.sum(jnp.sum(e * e, axis=-1, keepdims=True), axis=0, keepdims=True)
        else:
            xo_ref[...] = y

    row = _row_spec(tile, d)
    acc_shape = (SUBLANES, LANES)
    return pl.pallas_call(
        body,
        name=name,
        grid=(seq // tile,),
        in_specs=[
            _row_spec(tile, D_REST),
            _row_spec(tile, D_ATTN),
            row,
            _full_spec((1, d)),
            _full_spec((D_MIX, d)),
            _full_spec((N_GROUPS, BLK, BLK)),
            _full_spec((BLK, D_GM)),
        ]
        + ([row] if with_loss else []),
        out_specs=[row] + ([_full_spec(acc_shape)] if with_loss else []) + [_row_spec(tile, D_GM)],
        out_shape=[jax.ShapeDtypeStruct((seq, d), F32)]
        + ([jax.ShapeDtypeStruct(acc_shape, F32)] if with_loss else [])
        + [jax.ShapeDtypeStruct((seq, D_GM), F32)],
        scratch_shapes=[pltpu.VMEM((tile, D_MIX), MXU_DTYPE), pltpu.VMEM((tile, D_GM), MXU_DTYPE)],
        compiler_params=_params(("arbitrary",) if with_loss else ("parallel",)),
    )(pb, o, x, gate, w_out, w_s, b_st, *([target] if with_loss else []))


def _mix_out_bwd(dxn, pb, o, sv, gate, w_out, w_s_t, name):
    seq, d = dxn.shape
    tile = min(TOKEN_TILE, seq)
    nb = tile // BLK
    nt = seq // tile

    def body(dxn_ref, pb_ref, o_ref, sv_ref, gate_ref, wo_ref, wst_ref,
             dpb_ref, do_ref, dwo_ref, dgate_ref, dws_ref, dbs_ref, g_ref, y_s, dy_s, vn_s, rv_s, vnb_s, dsv_s, dvn_s):
        @pl.when(pl.program_id(0) == 0)
        def _():
            g_ref[...] = jnp.zeros_like(g_ref)
            dws_ref[...] = jnp.zeros_like(dws_ref)
            dbs_ref[...] = jnp.zeros_like(dbs_ref)

        ones = _half_ones(WIDE)
        lo = _lane_lo(BLK)
        c_u = slice(D_ATTN, D_ATTN + D_GM)
        c_vg = slice(D_ATTN + D_GM, D_ATTN + 2 * D_GM)
        c_gg = slice(D_ATTN + 2 * D_GM, D_REST)
        dxv = dxn_ref[...]
        dy_s[...] = _dot_nt(_mx(dxv * gate_ref[...]), wo_ref[...])
        ga = pb_ref[:, 0:D_ATTN]
        sig = _sigmoid(ga)
        sil = ga * sig
        ov = o_ref[...]
        y_s[:, 0:D_ATTN] = _mx(ov * sil)
        da = dy_s[:, 0:D_ATTN]
        do_ref[...] = da * sil
        dpb_ref[:, 0:D_ATTN] = (da * ov * (sig * (1.0 + ga * (1.0 - sig)))).astype(dpb_ref.dtype)
        for j in range(D_GM // WIDE):
            cols = slice(j * WIDE, (j + 1) * WIDE)
            vhat, rv = _half_rms(pb_ref[:, 2 * D_GM + j * WIDE : 2 * D_GM + (j + 1) * WIDE], ones)
            vn_s[:, cols] = vhat
            rv_s[:, cols] = rv
            vnb_s[:, cols] = _mx(vhat)

        def gating(n, carry):
            rows = pl.ds(pl.multiple_of(n * BLK, BLK), BLK)
            sv = sv_ref[rows, :]
            u = pb_ref[rows, c_u]
            gg = pb_ref[rows, c_gg]
            sg = _sigmoid(gg)
            silg = gg * sg
            m0 = u * sv
            y_s[rows, D_ATTN:D_MIX] = _mx(m0 * silg)
            dm = dy_s[rows, D_ATTN:D_MIX]
            dm0 = dm * silg
            dpb_ref[rows, c_gg] = (dm * m0 * (sg * (1.0 + gg * (1.0 - sg)))).astype(dpb_ref.dtype)
            dpb_ref[rows, c_u] = (dm0 * sv).astype(dpb_ref.dtype)
            dsv = dm0 * u
            dsv_s[rows, :] = _mx(dsv)
            dbs_ref[...] += dsv
            return carry

        lax.fori_loop(0, nb, gating, 0)

        def spatial_bwd(n, carry):
            rows = pl.ds(pl.multiple_of(n * BLK, BLK), BLK)
            for j in range(N_PAIRS):
                cols = slice(j * LANES, (j + 1) * LANES)
                dsv = dsv_s[rows, cols]
                dvn_s[rows, cols] = jnp.where(lo, _dot(wst_ref[2 * j], dsv), _dot(wst_ref[2 * j + 1], dsv))
            return carry

        lax.fori_loop(0, nb, spatial_bwd, 0)
        zero = jnp.zeros((BLK, LANES), MXU_DTYPE)
        for j in range(N_PAIRS):
            cols = slice(j * LANES, (j + 1) * LANES)
            chunks = [dsv_s[n * BLK : (n + 1) * BLK, cols] for n in range(nb)]
            vn_all = jnp.concatenate([vnb_s[n * BLK : (n + 1) * BLK, cols] for n in range(nb)], axis=1)
            dws_ref[2 * j] += _dot_nt(jnp.concatenate([jnp.where(lo, c, zero) for c in chunks], axis=1), vn_all)
            dws_ref[2 * j + 1] += _dot_nt(jnp.concatenate([jnp.where(lo, zero, c) for c in chunks], axis=1), vn_all)
        for j in range(D_GM // WIDE):
            cols = slice(j * WIDE, (j + 1) * WIDE)
            dpb_ref[:, D_ATTN + D_GM + j * WIDE : D_ATTN + D_GM + (j + 1) * WIDE] = _half_rms_bwd(
                dvn_s[:, cols], vn_s[:, cols], rv_s[:, cols], ones
            ).astype(dpb_ref.dtype)
        g_ref[...] += _dot_tn(y_s[...], _mx(dxv))

        @pl.when(pl.program_id(0) == nt - 1)
        def _():
            gv = g_ref[...]
            dwo_ref[...] = (gv * gate_ref[...]).astype(dwo_ref.dtype)
            dgate_ref[...] = _group_rows(gv * wo_ref[...].astype(F32))

    return pl.pallas_call(
        body,
        name=name,
        grid=(seq // tile,),
        in_specs=[
            _row_spec(tile, d),
            _row_spec(tile, D_REST),
            _row_spec(tile, D_ATTN),
            _row_spec(tile, D_GM),
            _full_spec((1, d)),
            _full_spec((D_MIX, d)),
            _full_spec((N_GROUPS, BLK, BLK)),
        ],
        out_specs=[
            _row_spec(tile, D_REST),
            _row_spec(tile, D_ATTN),
            _full_spec((D_MIX, d)),
            _full_spec((SUBLANES, d)),
            _full_spec((N_GROUPS, BLK, BLK)),
            _full_spec((BLK, D_GM)),
        ],
        out_shape=[
            jax.ShapeDtypeStruct((seq, D_REST), MXU_DTYPE),
            jax.ShapeDtypeStruct((seq, D_ATTN), F32),
            jax.ShapeDtypeStruct((D_MIX, d), jnp.bfloat16),
            jax.ShapeDtypeStruct((SUBLANES, d), F32),
            jax.ShapeDtypeStruct((N_GROUPS, BLK, BLK), F32),
            jax.ShapeDtypeStruct((BLK, D_GM), F32),
        ],
        scratch_shapes=[
            pltpu.VMEM((D_MIX, d), F32),
            pltpu.VMEM((tile, D_MIX), MXU_DTYPE),
            pltpu.VMEM((tile, D_MIX), F32),
            pltpu.VMEM((tile, D_GM), F32),
            pltpu.VMEM((tile, D_GM), F32),
            pltpu.VMEM((tile, D_GM), MXU_DTYPE),
            pltpu.VMEM((tile, D_GM), MXU_DTYPE),
            pltpu.VMEM((tile, D_GM), F32),
        ],
        compiler_params=_params(("arbitrary",)),
    )(dxn, pb, o, sv, gate, w_out, w_s_t)


def _attn_bwd(pa, o, do, probs, p_sink, q_gain2, k_gain2, name, scatter=()):
    seq = pa.shape[0]
    tile = min(TOKEN_TILE, seq)
    nb = tile // BLK
    nt = seq // tile
    ext = tile + 2 * BLK
    n_ride = len(scatter)
    riding = n_ride > 0

    def body(qkv_ref, kvp_ref, kvn_ref, o_ref, do_ref, p_ref, psink_ref, qg_ref, kg_ref, *rest):
        i = pl.program_id(0)
        blocks, rest = rest[:n_ride], rest[n_ride:]
        dq_ref, dkv_ref, hp_ref, hn_ref, dqg_ref, dkg_ref, dsk_ref = rest[:7]
        landing, rest = rest[7 : 7 + n_ride], rest[7 + n_ride :]
        (qs, dos, qhat_s, rq_s, ks, kr, vs, vr, khat_s, rk_s, dqn_s, dka, dva, dp_scr, ds_scr) = rest[:15]
        if riding:
            start, finish = _scatter_stages(blocks, landing, *rest[15:])
            at_start, _, at_finish = _rider_steps(nt)
            pl.when(i == at_start)(start)

        @pl.when(i == 0)
        def _():
            dqg_ref[...] = jnp.zeros_like(dqg_ref)
            dkg_ref[...] = jnp.zeros_like(dkg_ref)
            dsk_ref[...] = jnp.zeros_like(dsk_ref)

        ones = _half_ones()
        lo = _lane_lo(BLK)
        lo_t = _lane_lo(tile)
        lo_c = _lane_lo(ROW_CHUNK)
        qg = qg_ref[...] * Q_SCALE
        kg = kg_ref[...]
        _stage_keys(kvp_ref, qkv_ref, kvn_ref, kg, ones, tile, ks, kr, vs, vr, khat_s, rk_s)
        head_lane = lax.broadcasted_iota(jnp.int32, (tile, LANES), 1)
        d_rows = jnp.zeros((tile, LANES), F32)
        for j in range(N_PAIRS):
            cols = slice(j * LANES, (j + 1) * LANES)
            qhat, rq = _half_rms(qkv_ref[:, cols], ones)
            qhat_s[:, cols] = qhat
            rq_s[:, cols] = rq
            _stage_queries(qhat * qg, lo_t, j, nb, qs)
            dov = do_ref[:, cols]
            _stage_queries(dov, lo_t, j, nb, dos)
            d_pair = _half_sum(dov * o_ref[:, cols], ones)
            d_rows = jnp.where(head_lane == 2 * j, d_pair, d_rows)
            d_rows = jnp.where(head_lane == 2 * j + 1, pltpu.roll(d_pair, HEAD_DIM, 1), d_rows)
        dsk_ref[...] -= _group_rows(psink_ref[...] * d_rows)
        dka[...] = jnp.zeros_like(dka)
        dva[...] = jnp.zeros_like(dva)

        def block(n, carry):
            r0 = pl.multiple_of(n * BLK, BLK)
            krows = pl.ds(r0, 3 * BLK)
            for v in range(2):
                dp_scr[v] = _dot_nt(dos[n, v], (vr if v else vs)[krows, :])
            for h in range(N_HEADS):
                v, slot = HEAD_SLOT[h]
                j, a = divmod(h, 2)
                cols = slice(j * LANES, (j + 1) * LANES)
                for rc in range(0, BLK, ROW_CHUNK):
                    rows = slice(slot * BLK + rc, slot * BLK + rc + ROW_CHUNK)
                    trows = pl.ds(pl.multiple_of(r0 + rc, ROW_CHUNK), ROW_CHUNK)
                    prod = do_ref[trows, cols] * o_ref[trows, cols]
                    prod = jnp.where(lo_c, prod, 0.0) if a == 0 else jnp.where(lo_c, 0.0, prod)
                    dcol = jnp.sum(prod, axis=-1, keepdims=True)
                    ds_scr[v, rows, :] = _mx(p_ref[n, v, rows, :].astype(F32) * (dp_scr[v, rows, :] - dcol))
            dqv = []
            for v in range(2):
                dqv.append(_dot(ds_scr[v], (kr if v else ks)[krows, :]))
                dka[v, krows, :] += _dot_tn(ds_scr[v], qs[n, v])
                dva[v, krows, :] += _dot_tn(p_ref[n, v], dos[n, v])
            for j in range(N_PAIRS):
                dqn_s[pl.ds(r0, BLK), j * LANES : (j + 1) * LANES] = _unstack_pair(dqv, j, lo)
            return carry

        lax.fori_loop(0, nb, block, 0)
        for j in range(N_PAIRS):
            cols = slice(j * LANES, (j + 1) * LANES)
            dqn = dqn_s[:, cols]
            qhat = qhat_s[:, cols]
            dqg_ref[:, cols] += _group_rows(dqn * qhat) * Q_SCALE
            dq_ref[:, cols] = _half_rms_bwd(dqn * qg, qhat, rq_s[:, cols], ones).astype(dq_ref.dtype)
        dkn = dka[0] + pltpu.roll(dka[1], HEAD_DIM, 1)
        khat = khat_s[...]
        dkg_ref[...] += _group_rows(dkn * khat)
        dk = _half_rms_bwd(dkn * kg, khat, rk_s[...], ones)
        dv = dva[0] + pltpu.roll(dva[1], HEAD_DIM, 1)
        hp_ref[:, 0:D_KV] = dk[0:BLK]
        hp_ref[:, D_KV : 2 * D_KV] = dv[0:BLK]
        dkv_ref[:, 0:D_KV] = dk[BLK : BLK + tile]
        dkv_ref[:, D_KV : 2 * D_KV] = dv[BLK : BLK + tile]
        hn_ref[:, 0:D_KV] = dk[BLK + tile : ext]
        hn_ref[:, D_KV : 2 * D_KV] = dv[BLK + tile : ext]
        if riding:
            pl.when(i == at_finish)(finish)

    prev, nxt = _halo_specs(tile, seq)
    vec = _full_spec((1, LANES))
    halo = pl.BlockSpec((None, BLK, 2 * D_KV), lambda i: (i, 0, 0))
    return pl.pallas_call(
        body,
        name=name,
        grid=(nt,),
        in_specs=[
            _row_spec(tile, D_QKV),
            prev,
            nxt,
            _row_spec(tile, D_ATTN),
            _row_spec(tile, D_ATTN),
            pl.BlockSpec((nb, 2, STACK, 3 * BLK), lambda i: (i, 0, 0, 0)),
            _row_spec(tile, LANES),
            vec,
            vec,
        ]
        + [HBM_SPEC] * n_ride,
        out_specs=[
            _row_spec(tile, D_ATTN),
            _row_spec(tile, 2 * D_KV),
            halo,
            halo,
            _full_spec((SUBLANES, D_ATTN)),
            _full_spec((SUBLANES, LANES)),
            _full_spec((SUBLANES, LANES)),
        ]
        + [HBM_SPEC] * n_ride,
        out_shape=[
            jax.ShapeDtypeStruct((seq, D_ATTN), MXU_DTYPE),
            jax.ShapeDtypeStruct((seq, 2 * D_KV), F32),
            jax.ShapeDtypeStruct((nt, BLK, 2 * D_KV), F32),
            jax.ShapeDtypeStruct((nt, BLK, 2 * D_KV), F32),
            jax.ShapeDtypeStruct((SUBLANES, D_ATTN), F32),
            jax.ShapeDtypeStruct((SUBLANES, LANES), F32),
            jax.ShapeDtypeStruct((SUBLANES, LANES), F32),
        ]
        + _landing_shapes(scatter),
        scratch_shapes=[
            pltpu.VMEM((nb, 2, STACK, LANES), MXU_DTYPE),
            pltpu.VMEM((nb, 2, STACK, LANES), MXU_DTYPE),
            pltpu.VMEM((tile, D_ATTN), F32),
            pltpu.VMEM((tile, D_ATTN), F32),
            pltpu.VMEM((ext, LANES), MXU_DTYPE),
            pltpu.VMEM((ext, LANES), MXU_DTYPE),
            pltpu.VMEM((ext, LANES), MXU_DTYPE),
            pltpu.VMEM((ext, LANES), MXU_DTYPE),
            pltpu.VMEM((ext, LANES), F32),
            pltpu.VMEM((ext, LANES), F32),
            pltpu.VMEM((tile, D_ATTN), F32),
            pltpu.VMEM((2, ext, LANES), F32),
            pltpu.VMEM((2, ext, LANES), F32),
            pltpu.VMEM((2, STACK, 3 * BLK), F32),
            pltpu.VMEM((2, STACK, 3 * BLK), MXU_DTYPE),
        ]
        + _rider_sems(n_ride),
        compiler_params=_params(("arbitrary",)),
    )(pa, pa, pa, o, do, probs, p_sink, q_gain2, k_gain2, *scatter)


def _halo_in_specs(tile, nt):
    from_prev = pl.BlockSpec((None, BLK, 2 * D_KV), lambda i: (jnp.maximum(i - 1, 0), 0, 0))
    from_next = pl.BlockSpec((None, BLK, 2 * D_KV), lambda i: (jnp.minimum(i + 1, nt - 1), 0, 0))
    return from_prev, from_next


def _landing_shapes(scatter):
    return [jax.ShapeDtypeStruct((N_DEV,) + b.shape[2:], b.dtype) for b in scatter]


def _rider_sems(n_ride):
    if not n_ride:
        return []
    return [pltpu.SemaphoreType.DMA((7 * n_ride,)), pltpu.SemaphoreType.DMA((7 * n_ride,)), pltpu.SemaphoreType.DMA((n_ride,))]


def _proj_bwd_dx(x, dxn, dq, dkvb, dpb, w_in_t, gain, scale1, name):
    seq, d = x.shape
    tile = min(TOKEN_TILE, seq)

    def row(width):
        return _row_spec(tile, width)

    def body(x_ref, dxn_ref, dq_ref, dkvb_ref, dpb_ref, wt_ref, g_ref, s1_ref, dx_ref, c0_ref, c1_ref):
        @pl.when(pl.program_id(0) == 0)
        def _():
            c0_ref[...] = jnp.zeros_like(c0_ref)
            c1_ref[...] = jnp.zeros_like(c1_ref)

        dh = (
            _dot(dq_ref[...], wt_ref[0:D_ATTN, :])
            + _dot(dkvb_ref[...], wt_ref[D_ATTN:D_QKV, :])
            + _dot(dpb_ref[...], wt_ref[D_QKV:D_IN, :])
        )
        xv = x_ref[...]
        r = lax.rsqrt(jnp.mean(xv * xv, axis=-1, keepdims=True) + EPS)
        xn = xv * r
        c0_ref[...] += _group_rows(dh)
        c1_ref[...] += _group_rows(dh * xn)
        dxn_ = dh * (g_ref[...] * s1_ref[...])
        dx_ref[...] = dxn_ref[...] + r * (dxn_ - xn * jnp.mean(xn * dxn_, axis=-1, keepdims=True))

    vec = _full_spec((1, d))
    return pl.pallas_call(
        body,
        name=name,
        grid=(seq // tile,),
        in_specs=[row(d), row(d), row(D_ATTN), row(2 * D_KV), row(D_REST), _full_spec((D_IN, d)), vec, vec],
        out_specs=[row(d), _full_spec((SUBLANES, d)), _full_spec((SUBLANES, d))],
        out_shape=[
            jax.ShapeDtypeStruct((seq, d), F32),
            jax.ShapeDtypeStruct((SUBLANES, d), F32),
            jax.ShapeDtypeStruct((SUBLANES, d), F32),
        ],
        compiler_params=_params(("arbitrary",)),
    )(x, dxn, dq, dkvb, dpb, w_in_t, gain, scale1)


def _proj_bwd_dw(x, gain, scale1, shift, dq, dkv, halo_prev, halo_next, dpb, name, gather=()):
    seq, d = x.shape
    tile = min(TOKEN_TILE, seq)
    nt = seq // tile
    assert tile >= 2 * BLK
    n_ride = len(gather)

    def body(x_ref, g_ref, s1_ref, sh_ref, dq_ref, dkv_ref, hn_ref, hp_ref, dpb_ref, *rest):
        i = pl.program_id(0)
        sources, rest = rest[:n_ride], rest[n_ride:]
        dw_ref, dkvb_ref = rest[:2]
        gathered, (acc, *sems) = rest[2 : 2 + n_ride], rest[2 + n_ride :]
        after_compute = _gather_rider(sources, gathered, sems, i, nt) if n_ride else None

        @pl.when(i == 0)
        def _():
            acc[...] = jnp.zeros_like(acc)

        top = dkv_ref[0:BLK, :] + jnp.where(i > 0, hn_ref[...], 0.0)
        bot = dkv_ref[tile - BLK : tile, :] + jnp.where(i < nt - 1, hp_ref[...], 0.0)
        dkvb_ref[0:BLK, :] = top.astype(dkvb_ref.dtype)
        dkvb_ref[tile - BLK : tile, :] = bot.astype(dkvb_ref.dtype)
        if tile > 2 * BLK:
            dkvb_ref[BLK : tile - BLK, :] = dkv_ref[BLK : tile - BLK, :].astype(dkvb_ref.dtype)
        xv = x_ref[...]
        r = lax.rsqrt(jnp.mean(xv * xv, axis=-1, keepdims=True) + EPS)
        h = _mx((xv * r) * g_ref[...] * s1_ref[...] + sh_ref[...])
        acc[0:D_ATTN, :] += _dot_tn(dq_ref[...], h)
        acc[D_ATTN:D_QKV, :] += _dot_tn(dkvb_ref[...], h)
        acc[D_QKV:D_IN, :] += _dot_tn(dpb_ref[...], h)

        @pl.when(i == nt - 1)
        def _():
            dw_ref[...] = acc[...].astype(dw_ref.dtype)

        if n_ride:
            after_compute()

    from_prev, from_next = _halo_in_specs(tile, nt)
    vec = _full_spec((1, d))
    return pl.pallas_call(
        body,
        name=name,
        grid=(nt,),
        in_specs=[
            _row_spec(tile, d),
            vec,
            vec,
            vec,
            _row_spec(tile, D_ATTN),
            _row_spec(tile, 2 * D_KV),
            from_prev,
            from_next,
            _row_spec(tile, D_REST),
        ]
        + [HBM_SPEC] * n_ride,
        out_specs=[_full_spec((D_IN, d)), _row_spec(tile, 2 * D_KV)] + [HBM_SPEC] * n_ride,
        out_shape=[jax.ShapeDtypeStruct((D_IN, d), jnp.bfloat16), jax.ShapeDtypeStruct((seq, 2 * D_KV), MXU_DTYPE)]
        + _gathered_shapes(gather),
        scratch_shapes=[pltpu.VMEM((D_IN, d), F32)] + _rider_sems(n_ride),
        compiler_params=_params(("arbitrary",)),
    )(x, gain, scale1, shift, dq, dkv, halo_next, halo_prev, dpb, *gather)


def _adamw_math(w, g, m, v):
    m = ADAM_B1 * m + (1.0 - ADAM_B1) * g
    v = ADAM_B2 * v + (1.0 - ADAM_B2) * (g * g)
    m_hat = m / (1.0 - ADAM_B1**ADAM_STEP)
    v_hat = v / (1.0 - ADAM_B2**ADAM_STEP)
    delta = -ADAM_LR * (m_hat / (jnp.sqrt(v_hat) + ADAM_EPS) + ADAM_WD * w)
    return delta, m, v


def _small_update(gathered, gathered_ws, w, m, v, ws, m_ws, v_ws):
    def body(ga_ref, gws_ref, w_ref, m_ref, v_ref, ws_ref, mws_ref, vws_ref, *outs):
        for src, refs, out in ((ga_ref, (w_ref, m_ref, v_ref), outs[0:4]), (gws_ref, (ws_ref, mws_ref, vws_ref), outs[4:8])):
            g = src[0].astype(F32)
            for j in range(1, N_DEV):
                g = g + src[j].astype(F32)
            out[0][...] = g
            out[1][...], out[2][...], out[3][...] = _adamw_math(refs[0][...], g, refs[1][...], refs[2][...])

    shapes = [jax.ShapeDtypeStruct(w.shape, F32)] * 4 + [jax.ShapeDtypeStruct(ws.shape, F32)] * 4
    return pl.pallas_call(
        body,
        name="small_update",
        in_specs=[VMEM_SPEC] * 8,
        out_specs=[VMEM_SPEC] * 8,
        out_shape=shapes,
        compiler_params=_params(),
    )(gathered, gathered_ws, w, m, v, ws, m_ws, v_ws)


def _ada_update(c_all, d_ada_cols, w, m, v):
    n_layers = w.shape[0]

    def body(c_ref, da_ref, w_ref, m_ref, v_ref, g_ref, d_ref, mo_ref, vo_ref):
        cv = c_ref[...]
        cond = cv * _sigmoid(cv)
        for l in range(n_layers):
            g = lax.dot_general(
                cond, da_ref[l], (((0,), (0,)), ((), ())), preferred_element_type=F32, precision=lax.Precision.HIGHEST
            )
            g_ref[l] = g
            d_ref[l], mo_ref[l], vo_ref[l] = _adamw_math(w_ref[l], g, m_ref[l], v_ref[l])

    return pl.pallas_call(
        body,
        name="ada_update",
        in_specs=[VMEM_SPEC] * 5,
        out_specs=[VMEM_SPEC] * 4,
        out_shape=[jax.ShapeDtypeStruct(w.shape, F32)] * 4,
        compiler_params=_params(),
    )(c_all, d_ada_cols, w, m, v)


def _position():
    return lax.axis_index("x"), lax.axis_index("y"), lax.axis_index("c")


def _flip(pos, k):
    x, y, c = pos
    return (1 - x if k & 4 else x, 1 - y if k & 2 else y, 1 - c if k & 1 else c)


def _index(pos):
    x, y, c = pos
    return 4 * x + 2 * y + c


def _remote(src, dst, send_sem, recv_sem, to):
    return pltpu.make_async_remote_copy(
        src_ref=src, dst_ref=dst, send_sem=send_sem, recv_sem=recv_sem, device_id=to, device_id_type=MESH_ID
    )


def _all_gather_stages(slots, send_sems, recv_sems, sources=None, local_sems=None):
    me = _position()
    sibling = _flip(me, 1)
    others = (4, 2, 6)
    arrays = range(len(slots))

    def copy(t, k, block, to, own=False):
        slot = slots[t](_index(block))
        src = sources[t] if own and sources is not None else slot
        return _remote(src, slot, send_sems.at[7 * t + k], recv_sems.at[7 * t + k], to)

    def first(t):
        return [copy(t, 0, me, sibling, own=True)] + [copy(t, 1 + j, me, _flip(me, f), own=True) for j, f in enumerate(others)]

    def passed(t, j):
        return copy(t, 4 + j, _flip(me, others[j]), sibling)

    def local(t):
        return pltpu.make_async_copy(sources[t], slots[t](_index(me)), local_sems.at[t])

    def start():
        for t in arrays:
            if sources is not None:
                local(t).start()
            for cp in first(t):
                cp.start()

    def forward():
        for j, f in enumerate(others):
            for t in arrays:
                copy(t, 1 + j, _flip(me, f), me).wait_recv()
                passed(t, j).start()

    def finish():
        for t in arrays:
            copy(t, 0, sibling, me).wait_recv()
            for j, f in enumerate(others):
                copy(t, 4 + j, _flip(sibling, f), me).wait_recv()
        for t in arrays:
            for cp in first(t) + [passed(t, j) for j in range(len(others))]:
                cp.wait_send()
            if sources is not None:
                local(t).wait()

    return start, forward, finish


def _two_level_all_gather(slots, send_sems, recv_sems, between=None):
    start, forward, finish = _all_gather_stages(slots, send_sems, recv_sems)
    start()
    if between is not None:
        between()
    forward()
    finish()


def _row_block(ref, rows):
    return lambda j: ref.at[pl.ds(pl.multiple_of(j * rows, 16), rows), :]


def _scatter_stages(blocks, landing, send_sems, recv_sems, local_sems):
    me = _position()
    my = _index(me)
    arrays = range(len(blocks))

    def copy(t, k):
        px, py, pc = to = _flip(me, k)
        return _remote(blocks[t].at[2 * px + py, pc], landing[t].at[my], send_sems.at[7 * t + k - 1], recv_sems.at[7 * t + k - 1], to)

    def arrival(t, k):
        slot = landing[t].at[_index(_flip(me, k))]
        return _remote(slot, slot, send_sems.at[7 * t + k - 1], recv_sems.at[7 * t + k - 1], _flip(me, k))

    def local(t):
        x, y, c = me
        return pltpu.make_async_copy(blocks[t].at[2 * x + y, c], landing[t].at[my], local_sems.at[t])

    def start():
        for t in arrays:
            local(t).start()
            for k in range(1, N_DEV):
                copy(t, k).start()

    def finish():
        for t in arrays:
            for k in range(1, N_DEV):
                arrival(t, k).wait_recv()
        for t in arrays:
            for k in range(1, N_DEV):
                copy(t, k).wait_send()
            local(t).wait()

    return start, finish


def _ada_exchange(c_ref, w_ref, call_ref, parts_ref, sbuf, sem_s1, sem_r1, sem_s2, sem_r2):
    d = c_ref.shape[-1]
    n_layers = w_ref.shape[0]
    me = _position()
    my = _index(me)
    call_ref[my] = jnp.broadcast_to(c_ref[...], (SUBLANES, d))
    mine = call_ref.at[my]
    first = [_remote(mine, mine, sem_s1.at[k - 1], sem_r1.at[k - 1], _flip(me, k)) for k in range(1, N_DEV)]
    for cp in first:
        cp.start()
    for k in range(1, N_DEV):
        theirs = call_ref.at[_index(_flip(me, k))]
        _remote(theirs, theirs, sem_s1.at[k - 1], sem_r1.at[k - 1], _flip(me, k)).wait_recv()
    cv = call_ref[...].reshape(N_DEV * SUBLANES, d)
    cond = cv * _sigmoid(cv)
    for l in range(n_layers):
        rows = jnp.dot(cond, w_ref[l], preferred_element_type=F32, precision=lax.Precision.HIGHEST)
        for b in range(N_DEV):
            sbuf[b, l] = rows[b * SUBLANES : (b + 1) * SUBLANES]
    parts_ref[my] = sbuf[my]
    second = []
    for k in range(1, N_DEV):
        to = _flip(me, k)
        second.append(_remote(sbuf.at[_index(to)], parts_ref.at[my], sem_s2.at[k - 1], sem_r2.at[k - 1], to))
    for cp in second:
        cp.start()
    for k in range(1, N_DEV):
        theirs = parts_ref.at[_index(_flip(me, k))]
        _remote(theirs, theirs, sem_s2.at[k - 1], sem_r2.at[k - 1], _flip(me, k)).wait_recv()
    for cp in first + second:
        cp.wait_send()


def _gather_weights(w_in_t, w_out, c_row, w_ada):
    n_layers, rows_in, d = w_in_t.shape
    width = w_ada.shape[2]

    def body(wi_ref, wo_ref, c_ref, wa_ref, gi_ref, si_ref, so_ref, call_ref, parts_ref, sbuf, send_sems, recv_sems, *ada_sems):
        my = _index(_position())
        si_ref[...] = wi_ref[...].astype(si_ref.dtype)
        so_ref[...] = wo_ref[...].astype(so_ref.dtype)
        gi_ref[pl.ds(pl.multiple_of(my * rows_in, 16), rows_in), :] = si_ref[0]
        _two_level_all_gather(
            (_row_block(gi_ref, rows_in),),
            send_sems,
            recv_sems,
            between=functools.partial(_ada_exchange, c_ref, wa_ref, call_ref, parts_ref, sbuf, *ada_sems),
        )

    return pl.pallas_call(
        body,
        name="gather_weights",
        in_specs=[VMEM_SPEC] * 4,
        out_specs=[VMEM_SPEC] * 5,
        out_shape=[
            jax.ShapeDtypeStruct((N_DEV * rows_in, d), MXU_DTYPE),
            jax.ShapeDtypeStruct(w_in_t.shape, MXU_DTYPE),
            jax.ShapeDtypeStruct(w_out.shape, MXU_DTYPE),
            jax.ShapeDtypeStruct((N_DEV, SUBLANES, d), F32),
            jax.ShapeDtypeStruct((N_DEV, n_layers, SUBLANES, width), F32),
        ],
        scratch_shapes=[
            pltpu.VMEM((N_DEV, n_layers, SUBLANES, width), F32),
            pltpu.SemaphoreType.DMA((7,)),
            pltpu.SemaphoreType.DMA((7,)),
        ]
        + [pltpu.SemaphoreType.DMA((N_DEV - 1,))] * 4,
        compiler_params=_params(),
    )(w_in_t, w_out, c_row, w_ada)


def _gather_small(packed, adam=()):
    n_adam = len(adam)
    n_in, n_out = 4 * n_adam, 3 * n_adam

    def body(p_ref, *rest):
        operands, rest = rest[:n_in], rest[n_in:]
        g_ref, results, rest = rest[0], rest[1 : 1 + n_out], rest[1 + n_out :]
        send_sems, recv_sems, load_sems, store_sems = rest[:4]
        loaded, stored = rest[4 : 4 + n_in], rest[4 + n_in :]
        loads = [pltpu.make_async_copy(operands[k], loaded[k], load_sems.at[k]) for k in range(n_in)]
        stores = [pltpu.make_async_copy(stored[k], results[k], store_sems.at[k]) for k in range(n_out)]
        for load in loads:
            load.start()
        g_ref[_index(_position())] = p_ref[...]

        def updates():
            for t in range(n_adam):
                for load in loads[4 * t : 4 * t + 4]:
                    load.wait()
                w_ref, gr_ref, m_ref, v_ref = loaded[4 * t : 4 * t + 4]
                new = _adamw_math(w_ref[...], gr_ref[...], m_ref[...], v_ref[...])
                for k, value in zip(range(3 * t, 3 * t + 3), new):
                    stored[k][...] = value
                    stores[k].start()

        _two_level_all_gather((lambda j: g_ref.at[j],), send_sems, recv_sems, between=updates)
        for store in stores:
            store.wait()

    moved_in = [a for q in adam for a in q]
    moved_out = [jax.ShapeDtypeStruct(q[0].shape, F32) for q in adam for _ in range(3)]
    return pl.pallas_call(
        body,
        name="gather_small",
        in_specs=[VMEM_SPEC] + [HBM_SPEC] * n_in,
        out_specs=[VMEM_SPEC] + [HBM_SPEC] * n_out,
        out_shape=[jax.ShapeDtypeStruct((N_DEV,) + packed.shape, F32)] + moved_out,
        scratch_shapes=[pltpu.SemaphoreType.DMA((7,)), pltpu.SemaphoreType.DMA((7,))]
        + [pltpu.SemaphoreType.DMA((max(n_in, 1),)), pltpu.SemaphoreType.DMA((max(n_out, 1),))]
        + [pltpu.VMEM(a.shape, F32) for a in moved_in]
        + [pltpu.VMEM(s.shape, F32) for s in moved_out],
        compiler_params=_params(),
    )(packed, *moved_in)


def _scatter_finish(landed, name, own=(), behind=()):
    n = len(landed)
    n_in = n + len(own) + len(behind)

    def body(*refs):
        if own:
            x, y, c = _position()
            my = _index((x, y, c))
        for t, (src, out) in enumerate(zip(refs[:n], refs[n_in:])):
            g = None
            for j in range(N_DEV):
                part = src[j].astype(F32)
                if own:
                    part = jnp.where(j == my, refs[n + t][2 * x + y, c].astype(F32), part)
                g = part if g is None else g + part
            if behind:
                out[0] = g
                for k, done in enumerate(refs[n + len(own) : n_in]):
                    out[1 + k] = done[...]
            else:
                out[...] = g

    assert not behind or n == 1
    stacked = (1 + len(behind),) if behind else ()
    return pl.pallas_call(
        body,
        name=name,
        in_specs=[VMEM_SPEC] * n_in,
        out_specs=[VMEM_SPEC] * n,
        out_shape=[jax.ShapeDtypeStruct(stacked + a.shape[1:], F32) for a in landed],
        compiler_params=_params(),
    )(*landed, *own, *behind)


SEM_SPEC = pl.BlockSpec(memory_space=pltpu.SEMAPHORE)
SPLIT_COPY = pltpu.SideEffectType.DATAFLOW_SIDE_EFFECTING


def _scatter_start(blocks, name):
    land_shape = (N_DEV,) + blocks.shape[2:]

    def body(blocks_ref, land_ref, send_sems, recv_sems, blocks_thru, land_thru, token):
        me = _position()
        my = _index(me)
        for k in range(1, N_DEV):
            px, py, pc = to = _flip(me, k)
            _remote(blocks_ref.at[2 * px + py, pc], land_ref.at[my], send_sems.at[k - 1], recv_sems.at[k - 1], to).start()
        token[...] = jnp.zeros_like(token)

    return pl.pallas_call(
        body,
        name=name,
        in_specs=(HBM_SPEC, HBM_SPEC),
        out_specs=(SEM_SPEC, SEM_SPEC, HBM_SPEC, HBM_SPEC, VMEM_SPEC),
        out_shape=(
            pltpu.SemaphoreType.DMA((N_DEV - 1,)),
            pltpu.SemaphoreType.DMA((N_DEV - 1,)),
            pltpu.HBM(blocks.shape, blocks.dtype),
            pltpu.HBM(land_shape, blocks.dtype),
            jax.ShapeDtypeStruct((SUBLANES, LANES), F32),
        ),
        input_output_aliases={0: 2, 1: 3},
        compiler_params=pltpu.CompilerParams(has_side_effects=SPLIT_COPY),
    )(pltpu.with_memory_space_constraint(blocks, pltpu.HBM), pltpu.with_memory_space_constraint(lax.empty(land_shape, blocks.dtype), pltpu.HBM))


def _scatter_wait(send_sems, recv_sems, blocks_thru, land_thru, after, name):
    def body(blocks_ref, land_ref, send_sems, recv_sems, after_ref, blocks_dead, got_ref):
        me = _position()
        my = _index(me)
        for k in range(1, N_DEV):
            px, py, pc = to = _flip(me, k)
            _remote(blocks_ref.at[2 * px + py, pc], land_ref.at[my], send_sems.at[k - 1], recv_sems.at[k - 1], to).wait_send()
        for k in range(1, N_DEV):
            slot = land_ref.at[_index(_flip(me, k))]
            _remote(slot, slot, send_sems.at[k - 1], recv_sems.at[k - 1], _flip(me, k)).wait_recv()

    return pl.pallas_call(
        body,
        name=name,
        in_specs=(HBM_SPEC, HBM_SPEC, SEM_SPEC, SEM_SPEC, pl.BlockSpec(memory_space=pl.ANY)),
        out_specs=(HBM_SPEC, HBM_SPEC),
        out_shape=(pltpu.HBM(blocks_thru.shape, blocks_thru.dtype), pltpu.HBM(land_thru.shape, land_thru.dtype)),
        input_output_aliases={0: 0, 1: 1},
        compiler_params=pltpu.CompilerParams(has_side_effects=SPLIT_COPY),
    )(blocks_thru, land_thru, send_sems, recv_sems, after)


def _pack_rows(parts):
    rows, offsets, at = [], [], 0
    for p in parts:
        flat = p.reshape(-1)
        n = -(-flat.shape[0] // (SUBLANES * LANES)) * SUBLANES
        rows.append(jnp.pad(flat, (0, n * LANES - flat.shape[0])).reshape(n, LANES))
        offsets.append(at)
        at += n
    return jnp.concatenate(rows, axis=0), offsets


def _unpack_rows(packed, offsets, shapes):
    out = []
    for off, shape in zip(offsets, shapes):
        size = 1
        for s in shape:
            size *= s
        n = -(-size // (SUBLANES * LANES)) * SUBLANES
        out.append(packed[off : off + n].reshape(-1)[:size].reshape(shape))
    return out


def kernel(x, c, w_ada, b_ada, norm_gain, w_in, q_gain, k_gain, sink, w_s, b_s, w_out, loss_target, m_w_ada, m_b_ada, m_norm_gain, m_w_in, m_q_gain, m_k_gain, m_sink, m_w_s, m_b_s, m_w_out, v_w_ada, v_b_ada, v_norm_gain, v_w_in, v_q_gain, v_k_gain, v_sink, v_w_s, v_b_s, v_w_out):
    seq, d = x.shape[1], x.shape[2]
    n_layers = w_in.shape[0]
    w_cols = w_in.shape[2]
    ada_cols = w_ada.shape[2]
    my = _index(_position())
    xs = x.reshape(seq, d)
    target = loss_target.reshape(seq, d)

    rows_first = lambda a: a.transpose(0, 2, 1)
    w_in_t0, shard_in, shard_out, c_all, ada_parts = _gather_weights(rows_first(w_in), w_out, c, w_ada)
    w_in_ts, w_outs = [w_in_t0], []
    ada = ada_parts[:, :, 0, :].transpose(1, 0, 2).reshape(n_layers, 3 * d) + b_ada
    shift, scale1, gate = ada[:, None, 0:d], 1.0 + ada[:, None, d : 2 * d], ada[:, None, 2 * d : 3 * d]
    gain = norm_gain[:, None, :]

    w_s_m = w_s.astype(MXU_DTYPE)
    w_s_t = w_s_m.transpose(0, 1, 3, 2)
    b_st = jnp.repeat(b_s.transpose(0, 2, 1), HEAD_DIM, axis=2)
    q_gain2 = jnp.tile(q_gain, (1, 2))[:, None, :]
    k_gain2 = jnp.tile(k_gain, (1, 2))[:, None, :]

    xl, saved = xs, []
    for l in range(n_layers):
        last = l == n_layers - 1
        pa, pb = _ln_proj_fwd(xl, gain[l], scale1[l], shift[l], w_in_ts[l], f"ln_proj_fwd_{l}")
        wanted = ([shard_out[0]] if l == 0 else []) + ([] if last else [shard_out[l + 1], shard_in[l + 1]])
        o, probs, p_sink, *arrived = _attn_fwd(pa, q_gain2[l], k_gain2[l], sink[l], f"attn_fwd_{l}", gather=tuple(wanted))
        if not last:
            w_in_ts.append(arrived.pop())
        w_outs += arrived
        *out, sv = _mix_out_fwd(pb, o, xl, gate[l], w_outs[l], w_s_m[l], b_st[l], f"mix_out_fwd_{l}", target if last else None)
        saved.append((xl, pa, pb, o, probs, p_sink, sv))
        if last:
            dx, sq_err = out
        else:
            (xl,) = out

    g_w_in, g_w_out, small, d_ada_rows = [None] * n_layers, [None] * n_layers, [None] * n_layers, [None] * n_layers
    waiting = []
    d_ws_all = [None] * n_layers
    for l in reversed(range(n_layers)):
        x_l, pa, pb, o, probs, p_sink, sv = saved[l]
        dpb, do, dw_out, d_gate8, d_ws, d_bs = _mix_out_bwd(dx, pb, o, sv, gate[l], w_outs[l], w_s_t[l], f"mix_out_bwd_{l}")
        waiting.append((g_w_out, l, dw_out.reshape(4, 2, D_MIX // N_DEV, d)))
        riding, waiting = ([], waiting) if 0 < l == n_layers - 1 else (waiting, [])
        attn = _attn_bwd(
            pa, o, do, probs, p_sink, q_gain2[l], k_gain2[l], f"attn_bwd_{l}", scatter=tuple(b for _, _, b in riding)
        )
        dq, dkv, halo_prev, halo_next, d_qg, d_kg, d_sk = attn[:7]
        if riding:
            for (dest, layer, _), total in zip(riding, _scatter_finish(attn[7:], f"scatter_finish_{l}")):
                dest[layer] = total
        d_ws_all[l] = d_ws
        dw_args = (x_l, gain[l], scale1[l], shift[l], dq, dkv, halo_prev, halo_next, dpb, f"proj_bwd_dw_{l}")
        if l > 0:
            dw_in_t, dkvb = _proj_bwd_dw(*dw_args)
        else:
            d_ws_wire = jnp.stack(d_ws_all).reshape(-1, LANES).astype(jnp.bfloat16)
            dw_in_t, dkvb, gathered_ws = _proj_bwd_dw(*dw_args, gather=(d_ws_wire,))
        blocks_in = dw_in_t.reshape(4, 2, w_cols, d)
        if l > 0:
            waiting.append((g_w_in, l, blocks_in))
            dx, c0, c1 = _proj_bwd_dx(x_l, dx, dq, dkvb, dpb, w_in_ts[l], gain[l], scale1[l], f"proj_bwd_dx_{l}")
        else:
            *in_flight, token = _scatter_start(blocks_in, "scatter_start_in_0")
            dx, c0, c1 = _proj_bwd_dx(
                x_l, dx, dq, dkvb, dpb, w_in_ts[l], gain[l] + token[0, 0], scale1[l], f"proj_bwd_dx_{l}"
            )
            sent, landed = _scatter_wait(*in_flight, dx, "scatter_wait_in_0")
            g_w_in_t = _scatter_finish((landed,), "scatter_finish_in_0", own=(sent,), behind=tuple(g_w_in[1:]))[0]
        c0s, c1s = c0.sum(axis=0), c1.sum(axis=0)
        d_ada_rows[l] = jnp.concatenate([c0s, norm_gain[l] * c1s, d_gate8.sum(axis=0)])
        small[l] = (
            scale1[l, 0] * c1s,
            d_qg.sum(axis=0).reshape(N_HEADS, HEAD_DIM).sum(axis=0),
            d_kg.sum(axis=0).reshape(2, HEAD_DIM).sum(axis=0),
            d_sk.sum(axis=0)[0:N_HEADS],
            d_bs.reshape(BLK, N_GROUPS, HEAD_DIM).sum(axis=2).transpose(1, 0),
        )

    names = ("norm_gain", "q_gain", "k_gain", "sink", "b_s")
    stacked = [jnp.stack([small[l][t] for l in range(n_layers)]) for t in range(len(names))]
    d_ada = jnp.stack(d_ada_rows)
    packed, offsets = _pack_rows(stacked + [d_ada, sq_err[0, 0:1]])
    g_w_out = jnp.stack(g_w_out)
    adam_in = (rows_first(w_in), g_w_in_t, rows_first(m_w_in), rows_first(v_w_in))
    gathered, *upd = _gather_small(packed, adam=(adam_in, (w_out, g_w_out, m_w_out, v_w_out)))
    gathered_ws = gathered_ws.reshape(N_DEV, -1, LANES)
    g_w_in = rows_first(g_w_in_t)
    upd_in, upd_out = [rows_first(u) for u in upd[0:3]], upd[3:6]
    no_weight = jnp.zeros((1,), F32)
    weights = (norm_gain, q_gain, k_gain, sink, b_s, b_ada, no_weight)
    moments_m = (m_norm_gain, m_q_gain, m_k_gain, m_sink, m_b_s, m_b_ada, no_weight)
    moments_v = (v_norm_gain, v_q_gain, v_k_gain, v_sink, v_b_s, v_b_ada, no_weight)
    w_pack, _ = _pack_rows(weights)
    m_pack, _ = _pack_rows(moments_m)
    v_pack, _ = _pack_rows(moments_v)
    shapes = [w.shape for w in weights]
    flat_ws = lambda a: a.reshape(-1, LANES)
    updated = _small_update(gathered, gathered_ws, w_pack, m_pack, v_pack, flat_ws(w_s), flat_ws(m_w_s), flat_ws(v_w_s))
    g_small, d_small, m_small, v_small = (_unpack_rows(p, offsets, shapes) for p in updated[0:4])
    ws_small = [p.reshape(w_s.shape) for p in updated[4:8]]
    loss = g_small[-1][0] * (0.5 / d)

    ada_off = offsets[-2]
    ada_n = -(-n_layers * 3 * d // (SUBLANES * LANES)) * SUBLANES
    d_ada_all = gathered[:, ada_off : ada_off + ada_n].reshape(N_DEV, -1)[:, : n_layers * 3 * d].reshape(N_DEV, n_layers, 3 * d)
    d_ada_cols = lax.dynamic_slice_in_dim(d_ada_all, my * ada_cols, ada_cols, axis=2)
    g_w_ada, *upd_ada = _ada_update(c_all[:, 0, :], d_ada_cols.transpose(1, 0, 2), w_ada, m_w_ada, v_w_ada)

    def ordered(ada_, in_, out_, small_, ws):
        ng, qg, kg, sk, bs, ba, _ = small_
        return (ada_, ba, ng, in_, qg, kg, sk, ws, bs, out_)

    grads = ordered(g_w_ada, g_w_in, g_w_out, g_small, ws_small[0])
    deltas = ordered(upd_ada[0], upd_in[0], upd_out[0], d_small, ws_small[1])
    new_m = ordered(upd_ada[1], upd_in[1], upd_out[1], m_small, ws_small[2])
    new_v = ordered(upd_ada[2], upd_in[2], upd_out[2], v_small, ws_small[3])
    return (loss, dx.reshape(x.shape), *grads, *deltas, *new_m, *new_v)
```

```python
import functools

import jax
import jax.numpy as jnp
from jax import lax
from jax.experimental import pallas as pl
from jax.experimental.pallas import tpu as pltpu

F32 = jnp.float32
MXU_DTYPE = jnp.bfloat16
MESH_ID = pl.DeviceIdType.MESH

N_DEV = 8
HEAD_DIM = 64
N_HEADS = 8
Q_PER_KV = 4
D_ATTN = 512
D_KV = 128
D_GM = 512
N_GROUPS = 8
D_MIX = D_ATTN + D_GM
BLK = 128
LANES = 128
SUBLANES = 8
N_PAIRS = D_ATTN // LANES
D_QKV = D_ATTN + 2 * D_KV
D_REST = D_ATTN + 3 * D_GM
D_IN = D_QKV + D_REST
EPS = 1e-6
NEG_INF = -1e30
ALIBI_SLOPES = tuple(2.0 ** (-8.0 * (h + 1) / N_HEADS) for h in range(N_HEADS))
Q_SCALE = 1.0 / 8.0

ADAM_LR = 0.001
ADAM_B1 = 0.9
ADAM_B2 = 0.999
ADAM_EPS = 1e-08
ADAM_WD = 0.01
ADAM_STEP = 10

TOKEN_TILE = 512
PB_SLOTS = 3
VMEM_LIMIT_BYTES = 56 * 1024 * 1024


def _params(semantics=None):
    return pltpu.CompilerParams(dimension_semantics=semantics, vmem_limit_bytes=VMEM_LIMIT_BYTES)


def _dot(a, b):
    return jnp.dot(a, b, preferred_element_type=F32)


def _dot_nt(a, b):
    return lax.dot_general(a, b, (((1,), (1,)), ((), ())), preferred_element_type=F32)


def _dot_tn(a, b):
    return lax.dot_general(a, b, (((0,), (0,)), ((), ())), preferred_element_type=F32)


def _mx(v):
    return v.astype(MXU_DTYPE)


def _lane_lo(rows):
    return lax.broadcasted_iota(jnp.int32, (rows, LANES), 1) < HEAD_DIM


def _half_ones(width=LANES):
    group_bits = HEAD_DIM.bit_length() - 1
    r = jnp.right_shift(lax.broadcasted_iota(jnp.int32, (width, width), 0), group_bits)
    c = jnp.right_shift(lax.broadcasted_iota(jnp.int32, (width, width), 1), group_bits)
    return jnp.where(r == c, 1.0, 0.0).astype(jnp.bfloat16)


WIDE = 2 * LANES


def _half_sum(v, ones):
    p1 = v.astype(jnp.bfloat16)
    p2 = (v - p1.astype(F32)).astype(jnp.bfloat16)
    return _dot(p1, ones) + _dot(p2, ones)


def _half_rms(v, ones):
    r = lax.rsqrt(_half_sum(v * v, ones) * (1.0 / HEAD_DIM) + EPS)
    return v * r, r


def _half_rms_bwd(dy, vhat, r, ones):
    return r * (dy - vhat * (_half_sum(vhat * dy, ones) * (1.0 / HEAD_DIM)))


def _group_rows(v):
    rows, n = v.shape
    return v.reshape(rows // SUBLANES, SUBLANES, n).sum(axis=0)


def _sigmoid(v):
    return 1.0 / (1.0 + jnp.exp(-v))


ROW_CHUNK = 32
VARIANT_HEADS = ((0, 2, 5, 7), (1, 3, 4, 6))
HEAD_SLOT = {h: (v, s) for v, heads in enumerate(VARIANT_HEADS) for s, h in enumerate(heads)}
STACK = Q_PER_KV * BLK


def _fill_attn_bias(bias_s):
    qi = lax.broadcasted_iota(jnp.int32, (BLK, 3 * BLK), 0)
    ci = lax.broadcasted_iota(jnp.int32, (BLK, 3 * BLK), 1)
    dist = jnp.abs(ci - BLK - qi)
    distf = dist.astype(F32)
    window = dist <= BLK
    for kind, seen in enumerate((window & (ci >= BLK), window, window & (ci < 2 * BLK))):
        for h in range(N_HEADS):
            bias_s[kind, h] = jnp.where(seen, -(ALIBI_SLOPES[h] * distf), NEG_INF)


def _block_kind(block, seq):
    assert seq >= 2 * BLK
    return jnp.where(block == 0, 0, jnp.where(block == seq // BLK - 1, 2, 1))


def _stage_queries(qn, lo_t, j, nb, qs):
    for a in range(2):
        v, slot = HEAD_SLOT[2 * j + a]
        qm = _mx(jnp.where(lo_t, qn, 0.0) if a == 0 else jnp.where(lo_t, 0.0, qn))
        for n in range(nb):
            qs[n, v, slot * BLK : (slot + 1) * BLK, :] = qm[n * BLK : (n + 1) * BLK]


def _unstack_pair(stacked, j, lo):
    (v0, s0), (v1, s1) = HEAD_SLOT[2 * j], HEAD_SLOT[2 * j + 1]
    return jnp.where(lo, stacked[v0][s0 * BLK : (s0 + 1) * BLK], stacked[v1][s1 * BLK : (s1 + 1) * BLK])


def _stage_keys(kvp_ref, qkv_ref, kvn_ref, kg, ones, tile, ks, kr, vs, vr, khat_s=None, rk_s=None):
    pieces = (
        (0, BLK, kvp_ref[:, 0:D_KV], kvp_ref[:, D_KV : 2 * D_KV]),
        (BLK, tile, qkv_ref[:, D_ATTN : D_ATTN + D_KV], qkv_ref[:, D_ATTN + D_KV : D_QKV]),
        (BLK + tile, BLK, kvn_ref[:, 0:D_KV], kvn_ref[:, D_KV : 2 * D_KV]),
    )
    for r0, n, k, v in pieces:
        khat, rk = _half_rms(k, ones)
        kn = khat * kg
        ks[r0 : r0 + n, :] = _mx(kn)
        kr[r0 : r0 + n, :] = _mx(pltpu.roll(kn, HEAD_DIM, 1))
        vs[r0 : r0 + n, :] = _mx(v)
        vr[r0 : r0 + n, :] = _mx(pltpu.roll(v, HEAD_DIM, 1))
        if khat_s is not None:
            khat_s[r0 : r0 + n, :] = khat
            rk_s[r0 : r0 + n, :] = rk


def _halo_specs(tile, seq):
    nb = tile // BLK
    last = seq // BLK - 1
    kv_col = D_ATTN // (2 * D_KV)
    prev = pl.BlockSpec((BLK, 2 * D_KV), lambda i: (jnp.maximum(i * nb - 1, 0), kv_col))
    nxt = pl.BlockSpec((BLK, 2 * D_KV), lambda i: (jnp.minimum((i + 1) * nb, last), kv_col))
    return prev, nxt


def _row_spec(tile, width):
    return pl.BlockSpec((tile, width), lambda i: (i, 0))


def _full_spec(shape):
    nd = len(shape)
    return pl.BlockSpec(shape, lambda i: (0,) * nd)


SMEM_SPEC = pl.BlockSpec(memory_space=pltpu.SMEM)
VMEM_SPEC = pl.BlockSpec(memory_space=pltpu.VMEM)
HBM_SPEC = pl.BlockSpec(memory_space=pltpu.HBM)


def _rider_steps(nt):
    return 0, (3 * nt) // 4, nt - 1


def _gather_rider(sources, gathered, sems, step, nt):
    start, forward, finish = _all_gather_stages(
        [_row_block(g, s.shape[0]) for g, s in zip(gathered, sources)], sems[0], sems[1], sources=sources, local_sems=sems[2]
    )
    at_start, at_forward, at_finish = _rider_steps(nt)
    pl.when(step == at_start)(start)

    def after_compute():
        pl.when(step == at_forward)(forward)
        pl.when(step == at_finish)(finish)

    return after_compute


def _gathered_shapes(gather):
    return [jax.ShapeDtypeStruct((N_DEV * g.shape[0], g.shape[1]), g.dtype) for g in gather]


def _ln_proj_fwd(x, gain, scale1, shift, w_in_t, name):
    seq, d = x.shape
    tile = min(TOKEN_TILE, seq)

    def body(x_ref, g_ref, s1_ref, sh_ref, wt_ref, pa_ref, pb_ref):
        xv = x_ref[...]
        r = lax.rsqrt(jnp.mean(xv * xv, axis=-1, keepdims=True) + EPS)
        h = _mx((xv * r) * g_ref[...] * s1_ref[...] + sh_ref[...])
        pa_ref[...] = _dot_nt(h, wt_ref[0:D_QKV, :])
        pb_ref[...] = _dot_nt(h, wt_ref[D_QKV:D_IN, :])

    vec = _full_spec((1, d))
    return pl.pallas_call(
        body,
        name=name,
        grid=(seq // tile,),
        in_specs=[_row_spec(tile, d), vec, vec, vec, _full_spec((D_IN, d))],
        out_specs=[_row_spec(tile, D_QKV), _row_spec(tile, D_REST)],
        out_shape=[jax.ShapeDtypeStruct((seq, D_QKV), F32), jax.ShapeDtypeStruct((seq, D_REST), F32)],
        compiler_params=_params(("parallel",)),
    )(x, gain, scale1, shift, w_in_t)


def _attn_fwd(pa, q_gain2, k_gain2, sink, name, gather=()):
    seq = pa.shape[0]
    tile = min(TOKEN_TILE, seq)
    nb = tile // BLK
    nt = seq // tile
    ext = tile + 2 * BLK
    n_ride = len(gather)
    riding = n_ride > 0

    def body(sink_ref, qkv_ref, kvp_ref, kvn_ref, qg_ref, kg_ref, *rest):
        i = pl.program_id(0)
        sources, (o_ref, p_ref, psink_ref), gathered = rest[:n_ride], rest[n_ride : n_ride + 3], rest[n_ride + 3 : 2 * n_ride + 3]
        qs, ks, kr, vs, vr, bias_s, s_scr, *sems = rest[2 * n_ride + 3 :]
        if riding:
            after_compute = _gather_rider(sources, gathered, sems, i, nt)

        @pl.when(i == 0)
        def _():
            _fill_attn_bias(bias_s)

        ones = _half_ones()
        lo = _lane_lo(BLK)
        lo_t = _lane_lo(tile)
        head_lane = lax.broadcasted_iota(jnp.int32, (ROW_CHUNK, LANES), 1)
        _stage_keys(kvp_ref, qkv_ref, kvn_ref, kg_ref[...], ones, tile, ks, kr, vs, vr)
        for j in range(N_PAIRS):
            qhat, _ = _half_rms(qkv_ref[:, j * LANES : (j + 1) * LANES], ones)
            _stage_queries(qhat * (qg_ref[...] * Q_SCALE), lo_t, j, nb, qs)

        def block(n, carry):
            r0 = pl.multiple_of(n * BLK, BLK)
            krows = pl.ds(r0, 3 * BLK)
            kind = _block_kind(i * nb + n, seq)
            for v in range(2):
                s_scr[v] = _dot_nt(qs[n, v], (kr if v else ks)[krows, :])
            for rc in range(0, BLK, ROW_CHUNK):
                p_sink = jnp.zeros((ROW_CHUNK, LANES), F32)
                for h in range(N_HEADS):
                    v, slot = HEAD_SLOT[h]
                    sink_h = sink_ref[h]
                    rows = slice(slot * BLK + rc, slot * BLK + rc + ROW_CHUNK)
                    s = s_scr[v, rows, :] + bias_s[kind, h, rc : rc + ROW_CHUNK, :]
                    m = jnp.maximum(jnp.max(s, axis=-1, keepdims=True), sink_h)
                    p = jnp.exp(s - m)
                    e_sink = jnp.exp(sink_h - m)
                    inv = 1.0 / (jnp.sum(p, axis=-1, keepdims=True) + e_sink)
                    p_ref[n, v, rows, :] = _mx(p * inv)
                    p_sink = jnp.where(head_lane == h, e_sink * inv, p_sink)
                psink_ref[pl.ds(pl.multiple_of(r0 + rc, ROW_CHUNK), ROW_CHUNK), :] = p_sink
            outs = [_dot(p_ref[n, v], (vr if v else vs)[krows, :]) for v in range(2)]
            for j in range(N_PAIRS):
                o_ref[pl.ds(r0, BLK), j * LANES : (j + 1) * LANES] = _unstack_pair(outs, j, lo)
            return carry

        lax.fori_loop(0, nb, block, 0)
        if riding:
            after_compute()

    prev, nxt = _halo_specs(tile, seq)
    vec = _full_spec((1, LANES))
    in_specs = [SMEM_SPEC, _row_spec(tile, D_QKV), prev, nxt, vec, vec]
    out_specs = [
        _row_spec(tile, D_ATTN),
        pl.BlockSpec((nb, 2, STACK, 3 * BLK), lambda i: (i, 0, 0, 0)),
        _row_spec(tile, LANES),
    ]
    out_shape = [
        jax.ShapeDtypeStruct((seq, D_ATTN), F32),
        jax.ShapeDtypeStruct((seq // BLK, 2, STACK, 3 * BLK), MXU_DTYPE),
        jax.ShapeDtypeStruct((seq, LANES), F32),
    ]
    scratch = [
        pltpu.VMEM((nb, 2, STACK, LANES), MXU_DTYPE),
        pltpu.VMEM((ext, LANES), MXU_DTYPE),
        pltpu.VMEM((ext, LANES), MXU_DTYPE),
        pltpu.VMEM((ext, LANES), MXU_DTYPE),
        pltpu.VMEM((ext, LANES), MXU_DTYPE),
        pltpu.VMEM((3, N_HEADS, BLK, 3 * BLK), F32),
        pltpu.VMEM((2, STACK, 3 * BLK), F32),
    ]
    return pl.pallas_call(
        body,
        name=name,
        grid=(nt,),
        in_specs=in_specs + [HBM_SPEC] * n_ride,
        out_specs=out_specs + [HBM_SPEC] * n_ride,
        out_shape=out_shape + _gathered_shapes(gather),
        scratch_shapes=scratch + _rider_sems(n_ride),
        compiler_params=_params(("arbitrary",)),
    )(sink, pa, pa, pa, q_gain2, k_gain2, *gather)


def _mix_out_fwd(pb, o, x, gate, w_out, w_s, b_st, name, target=None):
    seq, d = x.shape
    tile = min(TOKEN_TILE, seq)
    nb = tile // BLK
    with_loss = target is not None

    nt = seq // tile

    def body(pb_hbm, o_ref, x_ref, gate_ref, wo_ref, ws_ref, bs_ref, *rest):
        *rest, ring, ring_sems = rest
        i = pl.program_id(0)

        def fetch(step):
            slot = step % PB_SLOTS
            rows = pl.ds(pl.multiple_of(step * tile, tile), tile)
            return pltpu.make_async_copy(pb_hbm.at[rows, :], ring.at[slot], ring_sems.at[slot])

        @pl.when(i == 0)
        def _():
            for step in range(min(PB_SLOTS - 1, nt)):
                fetch(step).start()

        @pl.when(i + (PB_SLOTS - 1) < nt)
        def _():
            fetch(i + (PB_SLOTS - 1)).start()

        fetch(i).wait()
        pb_ref = ring.at[i % PB_SLOTS]
        if with_loss:
            t_ref, xo_ref, acc_ref, sv_ref, y_s, vn_s = rest

            @pl.when(i == 0)
            def _():
                acc_ref[...] = jnp.zeros_like(acc_ref)
        else:
            xo_ref, sv_ref, y_s, vn_s = rest
        ones = _half_ones(WIDE)
        lo = _lane_lo(BLK)
        ga = pb_ref[:, 0:D_ATTN]
        y_s[:, 0:D_ATTN] = _mx(o_ref[...] * (ga * _sigmoid(ga)))
        for j in range(D_GM // WIDE):
            vhat, _ = _half_rms(pb_ref[:, 2 * D_GM + j * WIDE : 2 * D_GM + (j + 1) * WIDE], ones)
            vn_s[:, j * WIDE : (j + 1) * WIDE] = _mx(vhat)

        def chunk(n, carry):
            rows = pl.ds(pl.multiple_of(n * BLK, BLK), BLK)
            for j in range(N_PAIRS):
                cols = slice(j * LANES, (j + 1) * LANES)
                vn = vn_s[rows, cols]
                sv = jnp.where(lo, _dot(ws_ref[2 * j], vn), _dot(ws_ref[2 * j + 1], vn)) + bs_ref[:, cols]
                sv_ref[rows, cols] = sv
                u = pb_ref[rows, D_ATTN + j * LANES : D_ATTN + (j + 1) * LANES]
                gg = pb_ref[rows, D_ATTN + 2 * D_GM + j * LANES : D_ATTN + 2 * D_GM + (j + 1) * LANES]
                y_s[rows, D_ATTN + j * LANES : D_ATTN + (j + 1) * LANES] = _mx((u * sv) * (gg * _sigmoid(gg)))
            return carry

        lax.fori_loop(0, nb, chunk, 0)
        y = x_ref[...] + gate_ref[...] * _dot(y_s[...], wo_ref[...])
        if with_loss:
            e = y - t_ref[...]
            xo_ref[...] = e * (1.0 / d)
            acc_ref[...] += jnp.sum(jnp.sum(e * e, axis=-1, keepdims=True), axis=0, keepdims=True)
        else:
            xo_ref[...] = y

    row = _row_spec(tile, d)
    acc_shape = (SUBLANES, LANES)
    return pl.pallas_call(
        body,
        name=name,
        grid=(nt,),
        in_specs=[
            HBM_SPEC,
            _row_spec(tile, D_ATTN),
            row,
            _full_spec((1, d)),
            _full_spec((D_MIX, d)),
            _full_spec((N_GROUPS, BLK, BLK)),
            _full_spec((BLK, D_GM)),
        ]
        + ([row] if with_loss else []),
        out_specs=[row] + ([_full_spec(acc_shape)] if with_loss else []) + [_row_spec(tile, D_GM)],
        out_shape=[jax.ShapeDtypeStruct((seq, d), F32)]
        + ([jax.ShapeDtypeStruct(acc_shape, F32)] if with_loss else [])
        + [jax.ShapeDtypeStruct((seq, D_GM), F32)],
        scratch_shapes=[
            pltpu.VMEM((tile, D_MIX), MXU_DTYPE),
            pltpu.VMEM((tile, D_GM), MXU_DTYPE),
            pltpu.VMEM((PB_SLOTS, tile, D_REST), pb.dtype),
            pltpu.SemaphoreType.DMA((PB_SLOTS,)),
        ],
        compiler_params=_params(("arbitrary",)),
    )(pb, o, x, gate, w_out, w_s, b_st, *([target] if with_loss else []))


def _mix_out_bwd(dxn, pb, o, sv, gate, w_out, w_s_t, name):
    seq, d = dxn.shape
    tile = min(TOKEN_TILE, seq)
    nb = tile // BLK
    nt = seq // tile

    def body(dxn_ref, pb_ref, o_ref, sv_ref, gate_ref, wo_ref, wst_ref,
             dpb_ref, do_ref, dwo_ref, dgate_ref, dws_ref, dbs_ref, g_ref, y_s, dy_s, vn_s, rv_s, vnb_s, dsv_s, dvn_s):
        @pl.when(pl.program_id(0) == 0)
        def _():
            g_ref[...] = jnp.zeros_like(g_ref)
            dws_ref[...] = jnp.zeros_like(dws_ref)
            dbs_ref[...] = jnp.zeros_like(dbs_ref)

        ones = _half_ones(WIDE)
        lo = _lane_lo(BLK)
        c_u = slice(D_ATTN, D_ATTN + D_GM)
        c_vg = slice(D_ATTN + D_GM, D_ATTN + 2 * D_GM)
        c_gg = slice(D_ATTN + 2 * D_GM, D_REST)
        dxv = dxn_ref[...]
        dy_s[...] = _dot_nt(_mx(dxv * gate_ref[...]), wo_ref[...])
        ga = pb_ref[:, 0:D_ATTN]
        sig = _sigmoid(ga)
        sil = ga * sig
        ov = o_ref[...]
        y_s[:, 0:D_ATTN] = _mx(ov * sil)
        da = dy_s[:, 0:D_ATTN]
        do_ref[...] = da * sil
        dpb_ref[:, 0:D_ATTN] = (da * ov * (sig * (1.0 + ga * (1.0 - sig)))).astype(dpb_ref.dtype)
        for j in range(D_GM // WIDE):
            cols = slice(j * WIDE, (j + 1) * WIDE)
            vhat, rv = _half_rms(pb_ref[:, 2 * D_GM + j * WIDE : 2 * D_GM + (j + 1) * WIDE], ones)
            vn_s[:, cols] = vhat
            rv_s[:, cols] = rv
            vnb_s[:, cols] = _mx(vhat)

        def gating(n, carry):
            rows = pl.ds(pl.multiple_of(n * BLK, BLK), BLK)
            sv = sv_ref[rows, :]
            u = pb_ref[rows, c_u]
            gg = pb_ref[rows, c_gg]
            sg = _sigmoid(gg)
            silg = gg * sg
            m0 = u * sv
            y_s[rows, D_ATTN:D_MIX] = _mx(m0 * silg)
            dm = dy_s[rows, D_ATTN:D_MIX]
            dm0 = dm * silg
            dpb_ref[rows, c_gg] = (dm * m0 * (sg * (1.0 + gg * (1.0 - sg)))).astype(dpb_ref.dtype)
            dpb_ref[rows, c_u] = (dm0 * sv).astype(dpb_ref.dtype)
            dsv = dm0 * u
            dsv_s[rows, :] = _mx(dsv)
            dbs_ref[...] += dsv
            return carry

        lax.fori_loop(0, nb, gating, 0)

        def spatial_bwd(n, carry):
            rows = pl.ds(pl.multiple_of(n * BLK, BLK), BLK)
            for j in range(N_PAIRS):
                cols = slice(j * LANES, (j + 1) * LANES)
                dsv = dsv_s[rows, cols]
                dvn_s[rows, cols] = jnp.where(lo, _dot(wst_ref[2 * j], dsv), _dot(wst_ref[2 * j + 1], dsv))
            return carry

        lax.fori_loop(0, nb, spatial_bwd, 0)
        zero = jnp.zeros((BLK, LANES), MXU_DTYPE)
        for j in range(N_PAIRS):
            cols = slice(j * LANES, (j + 1) * LANES)
            chunks = [dsv_s[n * BLK : (n + 1) * BLK, cols] for n in range(nb)]
            vn_all = jnp.concatenate([vnb_s[n * BLK : (n + 1) * BLK, cols] for n in range(nb)], axis=1)
            dws_ref[2 * j] += _dot_nt(jnp.concatenate([jnp.where(lo, c, zero) for c in chunks], axis=1), vn_all)
            dws_ref[2 * j + 1] += _dot_nt(jnp.concatenate([jnp.where(lo, zero, c) for c in chunks], axis=1), vn_all)
        for j in range(D_GM // WIDE):
            cols = slice(j * WIDE, (j + 1) * WIDE)
            dpb_ref[:, D_ATTN + D_GM + j * WIDE : D_ATTN + D_GM + (j + 1) * WIDE] = _half_rms_bwd(
                dvn_s[:, cols], vn_s[:, cols], rv_s[:, cols], ones
            ).astype(dpb_ref.dtype)
        g_ref[...] += _dot_tn(y_s[...], _mx(dxv))

        @pl.when(pl.program_id(0) == nt - 1)
        def _():
            gv = g_ref[...]
            dwo_ref[...] = (gv * gate_ref[...]).astype(dwo_ref.dtype)
            dgate_ref[...] = _group_rows(gv * wo_ref[...].astype(F32))

    return pl.pallas_call(
        body,
        name=name,
        grid=(seq // tile,),
        in_specs=[
            _row_spec(tile, d),
            _row_spec(tile, D_REST),
            _row_spec(tile, D_ATTN),
            _row_spec(tile, D_GM),
            _full_spec((1, d)),
            _full_spec((D_MIX, d)),
            _full_spec((N_GROUPS, BLK, BLK)),
        ],
        out_specs=[
            _row_spec(tile, D_REST),
            _row_spec(tile, D_ATTN),
            _full_spec((D_MIX, d)),
            _full_spec((SUBLANES, d)),
            _full_spec((N_GROUPS, BLK, BLK)),
            _full_spec((BLK, D_GM)),
        ],
        out_shape=[
            jax.ShapeDtypeStruct((seq, D_REST), MXU_DTYPE),
            jax.ShapeDtypeStruct((seq, D_ATTN), F32),
            jax.ShapeDtypeStruct((D_MIX, d), jnp.bfloat16),
            jax.ShapeDtypeStruct((SUBLANES, d), F32),
            jax.ShapeDtypeStruct((N_GROUPS, BLK, BLK), F32),
            jax.ShapeDtypeStruct((BLK, D_GM), F32),
        ],
        scratch_shapes=[
            pltpu.VMEM((D_MIX, d), F32),
            pltpu.VMEM((tile, D_MIX), MXU_DTYPE),
            pltpu.VMEM((tile, D_MIX), F32),
            pltpu.VMEM((tile, D_GM), F32),
            pltpu.VMEM((tile, D_GM), F32),
            pltpu.VMEM((tile, D_GM), MXU_DTYPE),
            pltpu.VMEM((tile, D_GM), MXU_DTYPE),
            pltpu.VMEM((tile, D_GM), F32),
        ],
        compiler_params=_params(("arbitrary",)),
    )(dxn, pb, o, sv, gate, w_out, w_s_t)


def _attn_bwd(pa, o, do, probs, p_sink, q_gain2, k_gain2, name, scatter=()):
    seq = pa.shape[0]
    tile = min(TOKEN_TILE, seq)
    nb = tile // BLK
    nt = seq // tile
    ext = tile + 2 * BLK
    n_ride = len(scatter)
    riding = n_ride > 0

    def body(qkv_ref, kvp_ref, kvn_ref, o_ref, do_ref, p_ref, psink_ref, qg_ref, kg_ref, *rest):
        i = pl.program_id(0)
        blocks, rest = rest[:n_ride], rest[n_ride:]
        dq_ref, dkv_ref, hp_ref, hn_ref, dqg_ref, dkg_ref, dsk_ref = rest[:7]
        landing, rest = rest[7 : 7 + n_ride], rest[7 + n_ride :]
        (qs, dos, qhat_s, rq_s, ks, kr, vs, vr, khat_s, rk_s, dqn_s, dka, dva, dp_scr, ds_scr) = rest[:15]
        if riding:
            start, finish = _scatter_stages(blocks, landing, *rest[15:])
            at_start, _, at_finish = _rider_steps(nt)
            pl.when(i == at_start)(start)

        @pl.when(i == 0)
        def _():
            dqg_ref[...] = jnp.zeros_like(dqg_ref)
            dkg_ref[...] = jnp.zeros_like(dkg_ref)
            dsk_ref[...] = jnp.zeros_like(dsk_ref)

        ones = _half_ones()
        lo = _lane_lo(BLK)
        lo_t = _lane_lo(tile)
        lo_c = _lane_lo(ROW_CHUNK)
        qg = qg_ref[...] * Q_SCALE
        kg = kg_ref[...]
        _stage_keys(kvp_ref, qkv_ref, kvn_ref, kg, ones, tile, ks, kr, vs, vr, khat_s, rk_s)
        head_lane = lax.broadcasted_iota(jnp.int32, (tile, LANES), 1)
        d_rows = jnp.zeros((tile, LANES), F32)
        for j in range(N_PAIRS):
            cols = slice(j * LANES, (j + 1) * LANES)
            qhat, rq = _half_rms(qkv_ref[:, cols], ones)
            qhat_s[:, cols] = qhat
            rq_s[:, cols] = rq
            _stage_queries(qhat * qg, lo_t, j, nb, qs)
            dov = do_ref[:, cols]
            _stage_queries(dov, lo_t, j, nb, dos)
            d_pair = _half_sum(dov * o_ref[:, cols], ones)
            d_rows = jnp.where(head_lane == 2 * j, d_pair, d_rows)
            d_rows = jnp.where(head_lane == 2 * j + 1, pltpu.roll(d_pair, HEAD_DIM, 1), d_rows)
        dsk_ref[...] -= _group_rows(psink_ref[...] * d_rows)
        dka[...] = jnp.zeros_like(dka)
        dva[...] = jnp.zeros_like(dva)

        def block(n, carry):
            r0 = pl.multiple_of(n * BLK, BLK)
            krows = pl.ds(r0, 3 * BLK)
            for v in range(2):
                dp_scr[v] = _dot_nt(dos[n, v], (vr if v else vs)[krows, :])
            for h in range(N_HEADS):
                v, slot = HEAD_SLOT[h]
                j, a = divmod(h, 2)
                cols = slice(j * LANES, (j + 1) * LANES)
                for rc in range(0, BLK, ROW_CHUNK):
                    rows = slice(slot * BLK + rc, slot * BLK + rc + ROW_CHUNK)
                    trows = pl.ds(pl.multiple_of(r0 + rc, ROW_CHUNK), ROW_CHUNK)
                    prod = do_ref[trows, cols] * o_ref[trows, cols]
                    prod = jnp.where(lo_c, prod, 0.0) if a == 0 else jnp.where(lo_c, 0.0, prod)
                    dcol = jnp.sum(prod, axis=-1, keepdims=True)
                    ds_scr[v, rows, :] = _mx(p_ref[n, v, rows, :].astype(F32) * (dp_scr[v, rows, :] - dcol))
            dqv = []
            for v in range(2):
                dqv.append(_dot(ds_scr[v], (kr if v else ks)[krows, :]))
                dka[v, krows, :] += _dot_tn(ds_scr[v], qs[n, v])
                dva[v, krows, :] += _dot_tn(p_ref[n, v], dos[n, v])
            for j in range(N_PAIRS):
                dqn_s[pl.ds(r0, BLK), j * LANES : (j + 1) * LANES] = _unstack_pair(dqv, j, lo)
            return carry

        lax.fori_loop(0, nb, block, 0)
        for j in range(N_PAIRS):
            cols = slice(j * LANES, (j + 1) * LANES)
            dqn = dqn_s[:, cols]
            qhat = qhat_s[:, cols]
            dqg_ref[:, cols] += _group_rows(dqn * qhat) * Q_SCALE
            dq_ref[:, cols] = _half_rms_bwd(dqn * qg, qhat, rq_s[:, cols], ones).astype(dq_ref.dtype)
        dkn = dka[0] + pltpu.roll(dka[1], HEAD_DIM, 1)
        khat = khat_s[...]
        dkg_ref[...] += _group_rows(dkn * khat)
        dk = _half_rms_bwd(dkn * kg, khat, rk_s[...], ones)
        dv = dva[0] + pltpu.roll(dva[1], HEAD_DIM, 1)
        hp_ref[:, 0:D_KV] = dk[0:BLK]
        hp_ref[:, D_KV : 2 * D_KV] = dv[0:BLK]
        dkv_ref[:, 0:D_KV] = dk[BLK : BLK + tile]
        dkv_ref[:, D_KV : 2 * D_KV] = dv[BLK : BLK + tile]
        hn_ref[:, 0:D_KV] = dk[BLK + tile : ext]
        hn_ref[:, D_KV : 2 * D_KV] = dv[BLK + tile : ext]
        if riding:
            pl.when(i == at_finish)(finish)

    prev, nxt = _halo_specs(tile, seq)
    vec = _full_spec((1, LANES))
    halo = pl.BlockSpec((None, BLK, 2 * D_KV), lambda i: (i, 0, 0))
    return pl.pallas_call(
        body,
        name=name,
        grid=(nt,),
        in_specs=[
            _row_spec(tile, D_QKV),
            prev,
            nxt,
            _row_spec(tile, D_ATTN),
            _row_spec(tile, D_ATTN),
            pl.BlockSpec((nb, 2, STACK, 3 * BLK), lambda i: (i, 0, 0, 0)),
            _row_spec(tile, LANES),
            vec,
            vec,
        ]
        + [HBM_SPEC] * n_ride,
        out_specs=[
            _row_spec(tile, D_ATTN),
            _row_spec(tile, 2 * D_KV),
            halo,
            halo,
            _full_spec((SUBLANES, D_ATTN)),
            _full_spec((SUBLANES, LANES)),
            _full_spec((SUBLANES, LANES)),
        ]
        + [HBM_SPEC] * n_ride,
        out_shape=[
            jax.ShapeDtypeStruct((seq, D_ATTN), MXU_DTYPE),
            jax.ShapeDtypeStruct((seq, 2 * D_KV), F32),
            jax.ShapeDtypeStruct((nt, BLK, 2 * D_KV), F32),
            jax.ShapeDtypeStruct((nt, BLK, 2 * D_KV), F32),
            jax.ShapeDtypeStruct((SUBLANES, D_ATTN), F32),
            jax.ShapeDtypeStruct((SUBLANES, LANES), F32),
            jax.ShapeDtypeStruct((SUBLANES, LANES), F32),
        ]
        + _landing_shapes(scatter),
        scratch_shapes=[
            pltpu.VMEM((nb, 2, STACK, LANES), MXU_DTYPE),
            pltpu.VMEM((nb, 2, STACK, LANES), MXU_DTYPE),
            pltpu.VMEM((tile, D_ATTN), F32),
            pltpu.VMEM((tile, D_ATTN), F32),
            pltpu.VMEM((ext, LANES), MXU_DTYPE),
            pltpu.VMEM((ext, LANES), MXU_DTYPE),
            pltpu.VMEM((ext, LANES), MXU_DTYPE),
            pltpu.VMEM((ext, LANES), MXU_DTYPE),
            pltpu.VMEM((ext, LANES), F32),
            pltpu.VMEM((ext, LANES), F32),
            pltpu.VMEM((tile, D_ATTN), F32),
            pltpu.VMEM((2, ext, LANES), F32),
            pltpu.VMEM((2, ext, LANES), F32),
            pltpu.VMEM((2, STACK, 3 * BLK), F32),
            pltpu.VMEM((2, STACK, 3 * BLK), MXU_DTYPE),
        ]
        + _rider_sems(n_ride),
        compiler_params=_params(("arbitrary",)),
    )(pa, pa, pa, o, do, probs, p_sink, q_gain2, k_gain2, *scatter)


def _halo_in_specs(tile, nt):
    from_prev = pl.BlockSpec((None, BLK, 2 * D_KV), lambda i: (jnp.maximum(i - 1, 0), 0, 0))
    from_next = pl.BlockSpec((None, BLK, 2 * D_KV), lambda i: (jnp.minimum(i + 1, nt - 1), 0, 0))
    return from_prev, from_next


def _landing_shapes(scatter):
    return [jax.ShapeDtypeStruct((N_DEV,) + b.shape[2:], b.dtype) for b in scatter]


def _rider_sems(n_ride):
    if not n_ride:
        return []
    return [pltpu.SemaphoreType.DMA((7 * n_ride,)), pltpu.SemaphoreType.DMA((7 * n_ride,)), pltpu.SemaphoreType.DMA((n_ride,))]


def _proj_bwd_dx(x, dxn, dq, dkvb, dpb, w_in_t, gain, scale1, name):
    seq, d = x.shape
    tile = min(TOKEN_TILE, seq)

    def row(width):
        return _row_spec(tile, width)

    def body(x_ref, dxn_ref, dq_ref, dkvb_ref, dpb_ref, wt_ref, g_ref, s1_ref, dx_ref, c0_ref, c1_ref):
        @pl.when(pl.program_id(0) == 0)
        def _():
            c0_ref[...] = jnp.zeros_like(c0_ref)
            c1_ref[...] = jnp.zeros_like(c1_ref)

        dh = (
            _dot(dq_ref[...], wt_ref[0:D_ATTN, :])
            + _dot(dkvb_ref[...], wt_ref[D_ATTN:D_QKV, :])
            + _dot(dpb_ref[...], wt_ref[D_QKV:D_IN, :])
        )
        xv = x_ref[...]
        r = lax.rsqrt(jnp.mean(xv * xv, axis=-1, keepdims=True) + EPS)
        xn = xv * r
        c0_ref[...] += _group_rows(dh)
        c1_ref[...] += _group_rows(dh * xn)
        dxn_ = dh * (g_ref[...] * s1_ref[...])
        dx_ref[...] = dxn_ref[...] + r * (dxn_ - xn * jnp.mean(xn * dxn_, axis=-1, keepdims=True))

    vec = _full_spec((1, d))
    return pl.pallas_call(
        body,
        name=name,
        grid=(seq // tile,),
        in_specs=[row(d), row(d), row(D_ATTN), row(2 * D_KV), row(D_REST), _full_spec((D_IN, d)), vec, vec],
        out_specs=[row(d), _full_spec((SUBLANES, d)), _full_spec((SUBLANES, d))],
        out_shape=[
            jax.ShapeDtypeStruct((seq, d), F32),
            jax.ShapeDtypeStruct((SUBLANES, d), F32),
            jax.ShapeDtypeStruct((SUBLANES, d), F32),
        ],
        compiler_params=_params(("arbitrary",)),
    )(x, dxn, dq, dkvb, dpb, w_in_t, gain, scale1)


def _proj_bwd_dw(x, gain, scale1, shift, dq, dkv, halo_prev, halo_next, dpb, name, gather=()):
    seq, d = x.shape
    tile = min(TOKEN_TILE, seq)
    nt = seq // tile
    assert tile >= 2 * BLK
    n_ride = len(gather)

    def body(x_ref, g_ref, s1_ref, sh_ref, dq_ref, dkv_ref, hn_ref, hp_ref, dpb_ref, *rest):
        i = pl.program_id(0)
        sources, rest = rest[:n_ride], rest[n_ride:]
        dw_ref, dkvb_ref = rest[:2]
        gathered, (acc, *sems) = rest[2 : 2 + n_ride], rest[2 + n_ride :]
        after_compute = _gather_rider(sources, gathered, sems, i, nt) if n_ride else None

        @pl.when(i == 0)
        def _():
            acc[...] = jnp.zeros_like(acc)

        top = dkv_ref[0:BLK, :] + jnp.where(i > 0, hn_ref[...], 0.0)
        bot = dkv_ref[tile - BLK : tile, :] + jnp.where(i < nt - 1, hp_ref[...], 0.0)
        dkvb_ref[0:BLK, :] = top.astype(dkvb_ref.dtype)
        dkvb_ref[tile - BLK : tile, :] = bot.astype(dkvb_ref.dtype)
        if tile > 2 * BLK:
            dkvb_ref[BLK : tile - BLK, :] = dkv_ref[BLK : tile - BLK, :].astype(dkvb_ref.dtype)
        xv = x_ref[...]
        r = lax.rsqrt(jnp.mean(xv * xv, axis=-1, keepdims=True) + EPS)
        h = _mx((xv * r) * g_ref[...] * s1_ref[...] + sh_ref[...])
        acc[0:D_ATTN, :] += _dot_tn(dq_ref[...], h)
        acc[D_ATTN:D_QKV, :] += _dot_tn(dkvb_ref[...], h)
        acc[D_QKV:D_IN, :] += _dot_tn(dpb_ref[...], h)

        @pl.when(i == nt - 1)
        def _():
            dw_ref[...] = acc[...].astype(dw_ref.dtype)

        if n_ride:
            after_compute()

    from_prev, from_next = _halo_in_specs(tile, nt)
    vec = _full_spec((1, d))
    return pl.pallas_call(
        body,
        name=name,
        grid=(nt,),
        in_specs=[
            _row_spec(tile, d),
            vec,
            vec,
            vec,
            _row_spec(tile, D_ATTN),
            _row_spec(tile, 2 * D_KV),
            from_prev,
            from_next,
            _row_spec(tile, D_REST),
        ]
        + [HBM_SPEC] * n_ride,
        out_specs=[_full_spec((D_IN, d)), _row_spec(tile, 2 * D_KV)] + [HBM_SPEC] * n_ride,
        out_shape=[jax.ShapeDtypeStruct((D_IN, d), jnp.bfloat16), jax.ShapeDtypeStruct((seq, 2 * D_KV), MXU_DTYPE)]
        + _gathered_shapes(gather),
        scratch_shapes=[pltpu.VMEM((D_IN, d), F32)] + _rider_sems(n_ride),
        compiler_params=_params(("arbitrary",)),
    )(x, gain, scale1, shift, dq, dkv, halo_next, halo_prev, dpb, *gather)


def _adamw_math(w, g, m, v):
    m = ADAM_B1 * m + (1.0 - ADAM_B1) * g
    v = ADAM_B2 * v + (1.0 - ADAM_B2) * (g * g)
    m_hat = m / (1.0 - ADAM_B1**ADAM_STEP)
    v_hat = v / (1.0 - ADAM_B2**ADAM_STEP)
    delta = -ADAM_LR * (m_hat / (jnp.sqrt(v_hat) + ADAM_EPS) + ADAM_WD * w)
    return delta, m, v


def _small_update(gathered, gathered_ws, w, m, v, ws, m_ws, v_ws):
    def body(ga_ref, gws_ref, w_ref, m_ref, v_ref, ws_ref, mws_ref, vws_ref, *outs):
        for src, refs, out in ((ga_ref, (w_ref, m_ref, v_ref), outs[0:4]), (gws_ref, (ws_ref, mws_ref, vws_ref), outs[4:8])):
            g = src[0].astype(F32)
            for j in range(1, N_DEV):
                g = g + src[j].astype(F32)
            out[0][...] = g
            out[1][...], out[2][...], out[3][...] = _adamw_math(refs[0][...], g, refs[1][...], refs[2][...])

    shapes = [jax.ShapeDtypeStruct(w.shape, F32)] * 4 + [jax.ShapeDtypeStruct(ws.shape, F32)] * 4
    return pl.pallas_call(
        body,
        name="small_update",
        in_specs=[VMEM_SPEC] * 8,
        out_specs=[VMEM_SPEC] * 8,
        out_shape=shapes,
        compiler_params=_params(),
    )(gathered, gathered_ws, w, m, v, ws, m_ws, v_ws)


def _ada_update(c_all, d_ada_cols, w, m, v):
    n_layers = w.shape[0]

    def body(c_ref, da_ref, w_ref, m_ref, v_ref, g_ref, d_ref, mo_ref, vo_ref):
        cv = c_ref[...]
        cond = cv * _sigmoid(cv)
        for l in range(n_layers):
            g = lax.dot_general(
                cond, da_ref[l], (((0,), (0,)), ((), ())), preferred_element_type=F32, precision=lax.Precision.HIGHEST
            )
            g_ref[l] = g
            d_ref[l], mo_ref[l], vo_ref[l] = _adamw_math(w_ref[l], g, m_ref[l], v_ref[l])

    return pl.pallas_call(
        body,
        name="ada_update",
        in_specs=[VMEM_SPEC] * 5,
        out_specs=[VMEM_SPEC] * 4,
        out_shape=[jax.ShapeDtypeStruct(w.shape, F32)] * 4,
        compiler_params=_params(),
    )(c_all, d_ada_cols, w, m, v)


def _position():
    return lax.axis_index("x"), lax.axis_index("y"), lax.axis_index("c")


def _flip(pos, k):
    x, y, c = pos
    return (1 - x if k & 4 else x, 1 - y if k & 2 else y, 1 - c if k & 1 else c)


def _index(pos):
    x, y, c = pos
    return 4 * x + 2 * y + c


def _remote(src, dst, send_sem, recv_sem, to):
    return pltpu.make_async_remote_copy(
        src_ref=src, dst_ref=dst, send_sem=send_sem, recv_sem=recv_sem, device_id=to, device_id_type=MESH_ID
    )


def _all_gather_stages(slots, send_sems, recv_sems, sources=None, local_sems=None):
    me = _position()
    sibling = _flip(me, 1)
    others = (4, 2, 6)
    arrays = range(len(slots))

    def copy(t, k, block, to, own=False):
        slot = slots[t](_index(block))
        src = sources[t] if own and sources is not None else slot
        return _remote(src, slot, send_sems.at[7 * t + k], recv_sems.at[7 * t + k], to)

    def first(t):
        return [copy(t, 0, me, sibling, own=True)] + [copy(t, 1 + j, me, _flip(me, f), own=True) for j, f in enumerate(others)]

    def passed(t, j):
        return copy(t, 4 + j, _flip(me, others[j]), sibling)

    def local(t):
        return pltpu.make_async_copy(sources[t], slots[t](_index(me)), local_sems.at[t])

    def start():
        for t in arrays:
            if sources is not None:
                local(t).start()
            for cp in first(t):
                cp.start()

    def forward():
        for j, f in enumerate(others):
            for t in arrays:
                copy(t, 1 + j, _flip(me, f), me).wait_recv()
                passed(t, j).start()

    def finish():
        for t in arrays:
            copy(t, 0, sibling, me).wait_recv()
            for j, f in enumerate(others):
                copy(t, 4 + j, _flip(sibling, f), me).wait_recv()
        for t in arrays:
            for cp in first(t) + [passed(t, j) for j in range(len(others))]:
                cp.wait_send()
            if sources is not None:
                local(t).wait()

    return start, forward, finish


def _two_level_all_gather(slots, send_sems, recv_sems, between=None):
    start, forward, finish = _all_gather_stages(slots, send_sems, recv_sems)
    start()
    if between is not None:
        between()
    forward()
    finish()


def _row_block(ref, rows):
    return lambda j: ref.at[pl.ds(pl.multiple_of(j * rows, 16), rows), :]


def _scatter_stages(blocks, landing, send_sems, recv_sems, local_sems):
    me = _position()
    my = _index(me)
    arrays = range(len(blocks))

    def copy(t, k):
        px, py, pc = to = _flip(me, k)
        return _remote(blocks[t].at[2 * px + py, pc], landing[t].at[my], send_sems.at[7 * t + k - 1], recv_sems.at[7 * t + k - 1], to)

    def arrival(t, k):
        slot = landing[t].at[_index(_flip(me, k))]
        return _remote(slot, slot, send_sems.at[7 * t + k - 1], recv_sems.at[7 * t + k - 1], _flip(me, k))

    def local(t):
        x, y, c = me
        return pltpu.make_async_copy(blocks[t].at[2 * x + y, c], landing[t].at[my], local_sems.at[t])

    def start():
        for t in arrays:
            local(t).start()
            for k in range(1, N_DEV):
                copy(t, k).start()

    def finish():
        for t in arrays:
            for k in range(1, N_DEV):
                arrival(t, k).wait_recv()
        for t in arrays:
            for k in range(1, N_DEV):
                copy(t, k).wait_send()
            local(t).wait()

    return start, finish


def _ada_exchange(c_ref, w_ref, call_ref, parts_ref, sbuf, sem_s1, sem_r1, sem_s2, sem_r2):
    d = c_ref.shape[-1]
    n_layers = w_ref.shape[0]
    me = _position()
    my = _index(me)
    call_ref[my] = jnp.broadcast_to(c_ref[...], (SUBLANES, d))
    mine = call_ref.at[my]
    first = [_remote(mine, mine, sem_s1.at[k - 1], sem_r1.at[k - 1], _flip(me, k)) for k in range(1, N_DEV)]
    for cp in first:
        cp.start()
    for k in range(1, N_DEV):
        theirs = call_ref.at[_index(_flip(me, k))]
        _remote(theirs, theirs, sem_s1.at[k - 1], sem_r1.at[k - 1], _flip(me, k)).wait_recv()
    cv = call_ref[...].reshape(N_DEV * SUBLANES, d)
    cond = cv * _sigmoid(cv)
    for l in range(n_layers):
        rows = jnp.dot(cond, w_ref[l], preferred_element_type=F32, precision=lax.Precision.HIGHEST)
        for b in range(N_DEV):
            sbuf[b, l] = rows[b * SUBLANES : (b + 1) * SUBLANES]
    parts_ref[my] = sbuf[my]
    second = []
    for k in range(1, N_DEV):
        to = _flip(me, k)
        second.append(_remote(sbuf.at[_index(to)], parts_ref.at[my], sem_s2.at[k - 1], sem_r2.at[k - 1], to))
    for cp in second:
        cp.start()
    for k in range(1, N_DEV):
        theirs = parts_ref.at[_index(_flip(me, k))]
        _remote(theirs, theirs, sem_s2.at[k - 1], sem_r2.at[k - 1], _flip(me, k)).wait_recv()
    for cp in first + second:
        cp.wait_send()


def _gather_weights(w_in_t, w_out, c_row, w_ada):
    n_layers, rows_in, d = w_in_t.shape
    width = w_ada.shape[2]

    def body(wi_ref, wo_ref, c_ref, wa_ref, gi_ref, si_ref, so_ref, call_ref, parts_ref, sbuf, send_sems, recv_sems, *ada_sems):
        my = _index(_position())
        si_ref[...] = wi_ref[...].astype(si_ref.dtype)
        so_ref[...] = wo_ref[...].astype(so_ref.dtype)
        gi_ref[pl.ds(pl.multiple_of(my * rows_in, 16), rows_in), :] = si_ref[0]
        _two_level_all_gather(
            (_row_block(gi_ref, rows_in),),
            send_sems,
            recv_sems,
            between=functools.partial(_ada_exchange, c_ref, wa_ref, call_ref, parts_ref, sbuf, *ada_sems),
        )

    return pl.pallas_call(
        body,
        name="gather_weights",
        in_specs=[VMEM_SPEC] * 4,
        out_specs=[VMEM_SPEC] * 5,
        out_shape=[
            jax.ShapeDtypeStruct((N_DEV * rows_in, d), MXU_DTYPE),
            jax.ShapeDtypeStruct(w_in_t.shape, MXU_DTYPE),
            jax.ShapeDtypeStruct(w_out.shape, MXU_DTYPE),
            jax.ShapeDtypeStruct((N_DEV, SUBLANES, d), F32),
            jax.ShapeDtypeStruct((N_DEV, n_layers, SUBLANES, width), F32),
        ],
        scratch_shapes=[
            pltpu.VMEM((N_DEV, n_layers, SUBLANES, width), F32),
            pltpu.SemaphoreType.DMA((7,)),
            pltpu.SemaphoreType.DMA((7,)),
        ]
        + [pltpu.SemaphoreType.DMA((N_DEV - 1,))] * 4,
        compiler_params=_params(),
    )(w_in_t, w_out, c_row, w_ada)


def _gather_small(packed, adam=()):
    n_adam = len(adam)
    n_in, n_out = 4 * n_adam, 3 * n_adam

    def body(p_ref, *rest):
        operands, rest = rest[:n_in], rest[n_in:]
        g_ref, results, rest = rest[0], rest[1 : 1 + n_out], rest[1 + n_out :]
        send_sems, recv_sems, load_sems, store_sems = rest[:4]
        loaded, stored = rest[4 : 4 + n_in], rest[4 + n_in :]
        loads = [pltpu.make_async_copy(operands[k], loaded[k], load_sems.at[k]) for k in range(n_in)]
        stores = [pltpu.make_async_copy(stored[k], results[k], store_sems.at[k]) for k in range(n_out)]
        for load in loads:
            load.start()
        g_ref[_index(_position())] = p_ref[...]

        def updates():
            for t in range(n_adam):
                for load in loads[4 * t : 4 * t + 4]:
                    load.wait()
                w_ref, gr_ref, m_ref, v_ref = loaded[4 * t : 4 * t + 4]
                new = _adamw_math(w_ref[...], gr_ref[...], m_ref[...], v_ref[...])
                for k, value in zip(range(3 * t, 3 * t + 3), new):
                    stored[k][...] = value
                    stores[k].start()

        _two_level_all_gather((lambda j: g_ref.at[j],), send_sems, recv_sems, between=updates)
        for store in stores:
            store.wait()

    moved_in = [a for q in adam for a in q]
    moved_out = [jax.ShapeDtypeStruct(q[0].shape, F32) for q in adam for _ in range(3)]
    return pl.pallas_call(
        body,
        name="gather_small",
        in_specs=[VMEM_SPEC] + [HBM_SPEC] * n_in,
        out_specs=[VMEM_SPEC] + [HBM_SPEC] * n_out,
        out_shape=[jax.ShapeDtypeStruct((N_DEV,) + packed.shape, F32)] + moved_out,
        scratch_shapes=[pltpu.SemaphoreType.DMA((7,)), pltpu.SemaphoreType.DMA((7,))]
        + [pltpu.SemaphoreType.DMA((max(n_in, 1),)), pltpu.SemaphoreType.DMA((max(n_out, 1),))]
        + [pltpu.VMEM(a.shape, F32) for a in moved_in]
        + [pltpu.VMEM(s.shape, F32) for s in moved_out],
        compiler_params=_params(),
    )(packed, *moved_in)


def _scatter_finish(landed, name, own=()):
    n = len(landed)

    def body(*refs):
        if own:
            x, y, c = _position()
            my = _index((x, y, c))
        for t, (src, out) in enumerate(zip(refs[:n], refs[n + len(own) :])):
            g = None
            for j in range(N_DEV):
                part = src[j].astype(F32)
                if own:
                    part = jnp.where(j == my, refs[n + t][2 * x + y, c].astype(F32), part)
                g = part if g is None else g + part
            out[...] = g

    return pl.pallas_call(
        body,
        name=name,
        in_specs=[VMEM_SPEC] * (n + len(own)),
        out_specs=[VMEM_SPEC] * n,
        out_shape=[jax.ShapeDtypeStruct(a.shape[1:], F32) for a in landed],
        compiler_params=_params(),
    )(*landed, *own)


SEM_SPEC = pl.BlockSpec(memory_space=pltpu.SEMAPHORE)
SPLIT_COPY = pltpu.SideEffectType.DATAFLOW_SIDE_EFFECTING


def _scatter_start(blocks, name):
    land_shape = (N_DEV,) + blocks.shape[2:]

    def body(blocks_ref, land_ref, send_sems, recv_sems, blocks_thru, land_thru, token):
        me = _position()
        my = _index(me)
        for k in range(1, N_DEV):
            px, py, pc = to = _flip(me, k)
            _remote(blocks_ref.at[2 * px + py, pc], land_ref.at[my], send_sems.at[k - 1], recv_sems.at[k - 1], to).start()
        token[...] = jnp.zeros_like(token)

    return pl.pallas_call(
        body,
        name=name,
        in_specs=(HBM_SPEC, HBM_SPEC),
        out_specs=(SEM_SPEC, SEM_SPEC, HBM_SPEC, HBM_SPEC, VMEM_SPEC),
        out_shape=(
            pltpu.SemaphoreType.DMA((N_DEV - 1,)),
            pltpu.SemaphoreType.DMA((N_DEV - 1,)),
            pltpu.HBM(blocks.shape, blocks.dtype),
            pltpu.HBM(land_shape, blocks.dtype),
            jax.ShapeDtypeStruct((SUBLANES, LANES), F32),
        ),
        input_output_aliases={0: 2, 1: 3},
        compiler_params=pltpu.CompilerParams(has_side_effects=SPLIT_COPY),
    )(pltpu.with_memory_space_constraint(blocks, pltpu.HBM), pltpu.with_memory_space_constraint(lax.empty(land_shape, blocks.dtype), pltpu.HBM))


def _scatter_wait(send_sems, recv_sems, blocks_thru, land_thru, after, name):
    def body(blocks_ref, land_ref, send_sems, recv_sems, after_ref, blocks_dead, got_ref):
        me = _position()
        my = _index(me)
        for k in range(1, N_DEV):
            px, py, pc = to = _flip(me, k)
            _remote(blocks_ref.at[2 * px + py, pc], land_ref.at[my], send_sems.at[k - 1], recv_sems.at[k - 1], to).wait_send()
        for k in range(1, N_DEV):
            slot = land_ref.at[_index(_flip(me, k))]
            _remote(slot, slot, send_sems.at[k - 1], recv_sems.at[k - 1], _flip(me, k)).wait_recv()

    return pl.pallas_call(
        body,
        name=name,
        in_specs=(HBM_SPEC, HBM_SPEC, SEM_SPEC, SEM_SPEC, pl.BlockSpec(memory_space=pl.ANY)),
        out_specs=(HBM_SPEC, HBM_SPEC),
        out_shape=(pltpu.HBM(blocks_thru.shape, blocks_thru.dtype), pltpu.HBM(land_thru.shape, land_thru.dtype)),
        input_output_aliases={0: 0, 1: 1},
        compiler_params=pltpu.CompilerParams(has_side_effects=SPLIT_COPY),
    )(blocks_thru, land_thru, send_sems, recv_sems, after)


def _pack_rows(parts):
    rows, offsets, at = [], [], 0
    for p in parts:
        flat = p.reshape(-1)
        n = -(-flat.shape[0] // (SUBLANES * LANES)) * SUBLANES
        rows.append(jnp.pad(flat, (0, n * LANES - flat.shape[0])).reshape(n, LANES))
        offsets.append(at)
        at += n
    return jnp.concatenate(rows, axis=0), offsets


def _unpack_rows(packed, offsets, shapes):
    out = []
    for off, shape in zip(offsets, shapes):
        size = 1
        for s in shape:
            size *= s
        n = -(-size // (SUBLANES * LANES)) * SUBLANES
        out.append(packed[off : off + n].reshape(-1)[:size].reshape(shape))
    return out


def kernel(x, c, w_ada, b_ada, norm_gain, w_in, q_gain, k_gain, sink, w_s, b_s, w_out, loss_target, m_w_ada, m_b_ada, m_norm_gain, m_w_in, m_q_gain, m_k_gain, m_sink, m_w_s, m_b_s, m_w_out, v_w_ada, v_b_ada, v_norm_gain, v_w_in, v_q_gain, v_k_gain, v_sink, v_w_s, v_b_s, v_w_out):
    seq, d = x.shape[1], x.shape[2]
    n_layers = w_in.shape[0]
    w_cols = w_in.shape[2]
    ada_cols = w_ada.shape[2]
    my = _index(_position())
    xs = x.reshape(seq, d)
    target = loss_target.reshape(seq, d)

    rows_first = lambda a: a.transpose(0, 2, 1)
    w_in_t0, shard_in, shard_out, c_all, ada_parts = _gather_weights(rows_first(w_in), w_out, c, w_ada)
    w_in_ts, w_outs = [w_in_t0], []
    ada = ada_parts[:, :, 0, :].transpose(1, 0, 2).reshape(n_layers, 3 * d) + b_ada
    shift, scale1, gate = ada[:, None, 0:d], 1.0 + ada[:, None, d : 2 * d], ada[:, None, 2 * d : 3 * d]
    gain = norm_gain[:, None, :]

    w_s_m = w_s.astype(MXU_DTYPE)
    w_s_t = w_s_m.transpose(0, 1, 3, 2)
    b_st = jnp.repeat(b_s.transpose(0, 2, 1), HEAD_DIM, axis=2)
    q_gain2 = jnp.tile(q_gain, (1, 2))[:, None, :]
    k_gain2 = jnp.tile(k_gain, (1, 2))[:, None, :]

    xl, saved = xs, []
    for l in range(n_layers):
        last = l == n_layers - 1
        pa, pb = _ln_proj_fwd(xl, gain[l], scale1[l], shift[l], w_in_ts[l], f"ln_proj_fwd_{l}")
        wanted = ([shard_out[0]] if l == 0 else []) + ([] if last else [shard_out[l + 1], shard_in[l + 1]])
        o, probs, p_sink, *arrived = _attn_fwd(pa, q_gain2[l], k_gain2[l], sink[l], f"attn_fwd_{l}", gather=tuple(wanted))
        if not last:
            w_in_ts.append(arrived.pop())
        w_outs += arrived
        *out, sv = _mix_out_fwd(pb, o, xl, gate[l], w_outs[l], w_s_m[l], b_st[l], f"mix_out_fwd_{l}", target if last else None)
        saved.append((xl, pa, pb, o, probs, p_sink, sv))
        if last:
            dx, sq_err = out
        else:
            (xl,) = out

    g_w_in, g_w_out, small, d_ada_rows = [None] * n_layers, [None] * n_layers, [None] * n_layers, [None] * n_layers
    waiting = []
    d_ws_all = [None] * n_layers
    for l in reversed(range(n_layers)):
        x_l, pa, pb, o, probs, p_sink, sv = saved[l]
        dpb, do, dw_out, d_gate8, d_ws, d_bs = _mix_out_bwd(dx, pb, o, sv, gate[l], w_outs[l], w_s_t[l], f"mix_out_bwd_{l}")
        waiting.append((g_w_out, l, dw_out.reshape(4, 2, D_MIX // N_DEV, d)))
        riding, waiting = ([], waiting) if 0 < l == n_layers - 1 else (waiting, [])
        attn = _attn_bwd(
            pa, o, do, probs, p_sink, q_gain2[l], k_gain2[l], f"attn_bwd_{l}", scatter=tuple(b for _, _, b in riding)
        )
        dq, dkv, halo_prev, halo_next, d_qg, d_kg, d_sk = attn[:7]
        if riding:
            for (dest, layer, _), total in zip(riding, _scatter_finish(attn[7:], f"scatter_finish_{l}")):
                dest[layer] = total
        d_ws_all[l] = d_ws
        dw_args = (x_l, gain[l], scale1[l], shift[l], dq, dkv, halo_prev, halo_next, dpb, f"proj_bwd_dw_{l}")
        if l > 0:
            dw_in_t, dkvb = _proj_bwd_dw(*dw_args)
        else:
            d_ws_wire = jnp.stack(d_ws_all).reshape(-1, LANES).astype(jnp.bfloat16)
            dw_in_t, dkvb, gathered_ws = _proj_bwd_dw(*dw_args, gather=(d_ws_wire,))
        blocks_in = dw_in_t.reshape(4, 2, w_cols, d)
        if l > 0:
            waiting.append((g_w_in, l, blocks_in))
            dx, c0, c1 = _proj_bwd_dx(x_l, dx, dq, dkvb, dpb, w_in_ts[l], gain[l], scale1[l], f"proj_bwd_dx_{l}")
        else:
            *in_flight, token = _scatter_start(blocks_in, "scatter_start_in_0")
            dx, c0, c1 = _proj_bwd_dx(
                x_l, dx, dq, dkvb, dpb, w_in_ts[l], gain[l] + token[0, 0], scale1[l], f"proj_bwd_dx_{l}"
            )
            sent, landed = _scatter_wait(*in_flight, dx, "scatter_wait_in_0")
            g_w_in[l] = _scatter_finish((landed,), "scatter_finish_in_0", own=(sent,))[0]
        c0s, c1s = c0.sum(axis=0), c1.sum(axis=0)
        d_ada_rows[l] = jnp.concatenate([c0s, norm_gain[l] * c1s, d_gate8.sum(axis=0)])
        small[l] = (
            scale1[l, 0] * c1s,
            d_qg.sum(axis=0).reshape(N_HEADS, HEAD_DIM).sum(axis=0),
            d_kg.sum(axis=0).reshape(2, HEAD_DIM).sum(axis=0),
            d_sk.sum(axis=0)[0:N_HEADS],
            d_bs.reshape(BLK, N_GROUPS, HEAD_DIM).sum(axis=2).transpose(1, 0),
        )

    names = ("norm_gain", "q_gain", "k_gain", "sink", "b_s")
    stacked = [jnp.stack([small[l][t] for l in range(n_layers)]) for t in range(len(names))]
    d_ada = jnp.stack(d_ada_rows)
    packed, offsets = _pack_rows(stacked + [d_ada, sq_err[0, 0:1]])
    g_w_in_t, g_w_out = jnp.stack(g_w_in), jnp.stack(g_w_out)
    adam_in = (rows_first(w_in), g_w_in_t, rows_first(m_w_in), rows_first(v_w_in))
    gathered, *upd = _gather_small(packed, adam=(adam_in, (w_out, g_w_out, m_w_out, v_w_out)))
    gathered_ws = gathered_ws.reshape(N_DEV, -1, LANES)
    g_w_in = rows_first(g_w_in_t)
    upd_in, upd_out = [rows_first(u) for u in upd[0:3]], upd[3:6]
    no_weight = jnp.zeros((1,), F32)
    weights = (norm_gain, q_gain, k_gain, sink, b_s, b_ada, no_weight)
    moments_m = (m_norm_gain, m_q_gain, m_k_gain, m_sink, m_b_s, m_b_ada, no_weight)
    moments_v = (v_norm_gain, v_q_gain, v_k_gain, v_sink, v_b_s, v_b_ada, no_weight)
    w_pack, _ = _pack_rows(weights)
    m_pack, _ = _pack_rows(moments_m)
    v_pack, _ = _pack_rows(moments_v)
    shapes = [w.shape for w in weights]
    flat_ws = lambda a: a.reshape(-1, LANES)
    updated = _small_update(gathered, gathered_ws, w_pack, m_pack, v_pack, flat_ws(w_s), flat_ws(m_w_s), flat_ws(v_w_s))
    g_small, d_small, m_small, v_small = (_unpack_rows(p, offsets, shapes) for p in updated[0:4])
    ws_small = [p.reshape(w_s.shape) for p in updated[4:8]]
    loss = g_small[-1][0] * (0.5 / d)

    ada_off = offsets[-2]
    ada_n = -(-n_layers * 3 * d // (SUBLANES * LANES)) * SUBLANES
    d_ada_all = gathered[:, ada_off : ada_off + ada_n].reshape(N_DEV, -1)[:, : n_layers * 3 * d].reshape(N_DEV, n_layers, 3 * d)
    d_ada_cols = lax.dynamic_slice_in_dim(d_ada_all, my * ada_cols, ada_cols, axis=2)
    g_w_ada, *upd_ada = _ada_update(c_all[:, 0, :], d_ada_cols.transpose(1, 0, 2), w_ada, m_w_ada, v_w_ada)

    def ordered(ada_, in_, out_, small_, ws):
        ng, qg, kg, sk, bs, ba, _ = small_
        return (ada_, ba, ng, in_, qg, kg, sk, ws, bs, out_)

    grads = ordered(g_w_ada, g_w_in, g_w_out, g_small, ws_small[0])
    deltas = ordered(upd_ada[0], upd_in[0], upd_out[0], d_small, ws_small[1])
    new_m = ordered(upd_ada[1], upd_in[1], upd_out[1], m_small, ws_small[2])
    new_v = ordered(upd_ada[2], upd_in[2], upd_out[2], v_small, ws_small[3])
    return (loss, dx.reshape(x.shape), *grads, *deltas, *new_m, *new_v)
```
